```python
import jax, jax.numpy as jnp
from jax import lax
import numpy as np

D_MODEL = 1024
BATCH = 8
SEQ = 2048
DEPTH = 1
DEC_BATCH = 128
DEC_SEQ = 1
PAST_LEN = 16384
PAGE_SIZE = 128

D_MIX = D_MODEL
W_A = D_MIX // 2
W_B = D_MIX - W_A
HGRN_HEAD_DIM = 128
H_A = W_A // HGRN_HEAD_DIM
DK = HGRN_HEAD_DIM
DV = HGRN_HEAD_DIM
CHUNK = 64
LRU_BLOCKS = 8
LRU_BLOCK_DIM = W_B // LRU_BLOCKS
CONV_W = 4
LRU_C = 8.0
N_GROUPS = 4
EXPERTS_PER_GROUP = 8
N_EXPERTS = N_GROUPS * EXPERTS_PER_GROUP
TOP_K_IN_GROUP = 2
D_EXPERT = D_MODEL // 4
IN_COLS = 4 * W_A + 2 * W_B
EPS = 1e-6

kernel_name = "hymba_style_hgrn2_rglru_hiermoe_step"

F32 = jnp.float32


def _rmsnorm(x, g):
    xf = x.astype(F32)
    y = xf * lax.rsqrt(jnp.mean(xf * xf, axis=-1, keepdims=True) + EPS) * g.astype(F32)
    return y.astype(x.dtype)


def _hgrn_chunk(S, inp):
    q, k, v, logf = inp
    C = q.shape[2]
    b = jnp.cumsum(logf, axis=2)
    causal = jnp.tril(jnp.ones((C, C), bool))[None, None, :, :, None]
    diff = b[:, :, :, None, :] - b[:, :, None, :, :]
    decay = jnp.where(causal, jnp.exp(jnp.where(causal, diff, 0.0)), 0.0)
    scores = jnp.sum(q[:, :, :, None, :] * k[:, :, None, :, :] * decay, axis=-1)
    o = (jnp.einsum('bhts,bhse->bhte', scores, v)
         + jnp.einsum('bhtd,bhde->bhte', q * jnp.exp(b), S))
    b_last = b[:, :, -1:, :]
    S_new = (jnp.exp(b_last[:, :, 0, :])[..., None] * S
             + jnp.einsum('bhsd,bhse->bhde', k * jnp.exp(b_last - b), v))
    return S_new, o


def _hgrn2(q, k, v, logf, S0):
    B, L = q.shape[0], q.shape[1]
    c = min(CHUNK, L)
    n = -(-L // c)
    pad = n * c - L

    def blocks(t):
        t = jnp.pad(t, ((0, 0), (0, pad), (0, 0), (0, 0)))
        return t.reshape(B, n, c, H_A, t.shape[-1]).transpose(1, 0, 3, 2, 4)

    S, o = lax.scan(_hgrn_chunk, S0, (blocks(q), blocks(k), blocks(v), blocks(logf)))
    o = o.transpose(1, 0, 3, 2, 4).reshape(B, n * c, H_A, DV)[:, :L]
    return o, S


def _causal_conv(x, buf, w, b):
    L = x.shape[1]
    xx = jnp.concatenate([buf.astype(x.dtype), x], axis=1)
    y = b + sum(xx[:, i:i + L] * w[i] for i in range(CONV_W))
    return y, xx[:, -(CONV_W - 1):]


def _linear_scan(a, u, h0):
    u = u.at[:, 0].add(a[:, 0] * h0)

    def comb(l, r):
        al, bl = l
        ar, br = r
        return al * ar, ar * bl + br

    _, h = lax.associative_scan(comb, (a, u), axis=1)
    return h


def _rglru(xc, h0, fresh, wx, bx, wa, ba, lam):
    B, L, _ = xc.shape
    xf = xc.astype(F32)
    xb = xf.reshape(B, L, LRU_BLOCKS, LRU_BLOCK_DIM)
    gate_x = jax.nn.sigmoid(jnp.einsum('blnc,ncd->blnd', xb, wx.astype(F32)).reshape(B, L, W_B) + bx.astype(F32))
    gate_a = jax.nn.sigmoid(jnp.einsum('blnc,ncd->blnd', xb, wa.astype(F32)).reshape(B, L, W_B) + ba.astype(F32))
    log_a = -LRU_C * gate_a * jax.nn.softplus(-lam.astype(F32))
    a = jnp.exp(log_a)
    mult = jnp.sqrt(-jnp.expm1(2.0 * log_a))
    if fresh:
        reset = (jnp.arange(L) == 0)[None, :, None]
        mult = jnp.where(reset, 1.0, mult)
        a = jnp.where(reset, 0.0, a)
    h = _linear_scan(a, gate_x * xf * mult, h0.astype(F32))
    return h, h[:, -1]


def _mixer(xn, S0, h0, buf0, fresh, lb, w_in, hg_g, conv_w, conv_b, wx, bx, wa, ba, lam, w_out):
    B, L, _ = xn.shape
    proj = xn @ w_in
    q, fx, v, g, xr, gr = jnp.split(
        proj, [W_A, 2 * W_A, 3 * W_A, 4 * W_A, 4 * W_A + W_B], axis=-1)
    f = lb + (1.0 - lb) * jax.nn.sigmoid(fx.astype(F32))
    logf = jnp.log(f)
    k = 1.0 - f
    hs = lambda t: t.astype(F32).reshape(B, L, H_A, -1)
    o, S_new = _hgrn2(hs(q), hs(k), hs(v), hs(logf), S0.astype(F32))
    o = o * lax.rsqrt(jnp.mean(o * o, axis=-1, keepdims=True) + EPS) * hg_g.astype(F32).reshape(H_A, DV)
    ya = (o.reshape(B, L, W_A) * jax.nn.silu(g.astype(F32))).astype(xn.dtype)
    xc, buf_new = _causal_conv(xr, buf0, conv_w, conv_b)
    h, h_last = _rglru(xc, h0, fresh, wx, bx, wa, ba, lam)
    yb = (h * jax.nn.gelu(gr.astype(F32))).astype(xn.dtype)
    out = jnp.concatenate([ya, yb], axis=-1) @ w_out
    return out, S_new, h_last, buf_new


def _hier_moe(xn, wg, bg, we, be, w1, w3, w2):
    B, L, D = xn.shape
    xt = xn.reshape(B * L, D)
    xf = xt.astype(F32)
    p_group = jax.nn.softmax(xf @ wg.astype(F32) + bg.astype(F32), axis=-1)
    p_top, g_idx = lax.top_k(p_group, 1)
    logits = jnp.einsum('td,gde->tge', xf, we.astype(F32)) + be.astype(F32)
    sel = jnp.take_along_axis(logits, g_idx[:, :, None], axis=1)[:, 0]
    top_l, e_idx = lax.top_k(sel, TOP_K_IN_GROUP)
    w = jax.nn.softmax(top_l, axis=-1) * p_top
    flat = g_idx * EXPERTS_PER_GROUP + e_idx
    gates = jnp.sum(jax.nn.one_hot(flat, N_EXPERTS, dtype=F32) * w[..., None], axis=1)
    h = jax.nn.silu(jnp.einsum('td,edf->tef', xt, w1)) * jnp.einsum('td,edf->tef', xt, w3)
    h = h * gates[:, :, None].astype(h.dtype)
    y = jnp.einsum('tef,efd->td', h, w2)
    return y.reshape(B, L, D).astype(xn.dtype)


def _trunk(x, S0, h0, buf0, fresh, p):
    (lower_bounds, ln1_g, w_in, hgrn_norm_g, conv_w, conv_b, lru_wx, lru_bx, lru_wa, lru_ba,
     lru_lambda, w_out, ln2_g, router_wg, router_bg, router_we, router_be,
     exp_w1, exp_w3, exp_w2, final_g) = p
    lbs = jnp.cumsum(jax.nn.softmax(lower_bounds.astype(F32), axis=0), axis=0)
    Ss, hs, bufs = [], [], []
    for l in range(DEPTH):
        xn = _rmsnorm(x, ln1_g[l])
        mix, S, h, buf = _mixer(xn, S0[l], h0[l], buf0[l], fresh, lbs[l], w_in[l], hgrn_norm_g[l],
                                conv_w[l], conv_b[l], lru_wx[l], lru_bx[l], lru_wa[l], lru_ba[l],
                                lru_lambda[l], w_out[l])
        x = x + mix
        x = x + _hier_moe(_rmsnorm(x, ln2_g[l]), router_wg[l], router_bg[l], router_we[l],
                          router_be[l], exp_w1[l], exp_w3[l], exp_w2[l])
        Ss.append(S)
        hs.append(h)
        bufs.append(buf)
    return _rmsnorm(x, final_g), jnp.stack(Ss), jnp.stack(hs), jnp.stack(bufs)


def setup_inputs(seed: int = 0) -> dict:
    key = jax.random.key(seed)
    ks = jax.random.split(key, 32)
    nrm = lambda k, shape, s: jax.random.normal(k, shape, F32) * s
    a0 = jax.random.uniform(ks[0], (DEPTH, W_B), F32, 0.9, 0.999)
    s = a0 ** (1.0 / LRU_C)
    lam = jnp.log(s) - jnp.log1p(-s)
    return {
        "x_prompt": nrm(ks[1], (BATCH, SEQ, D_MODEL), 1.0),
        "x_sample": nrm(ks[2], (DEC_BATCH, DEC_SEQ, D_MODEL), 1.0),
        "state_hgrn": nrm(ks[3], (DEPTH, DEC_BATCH, H_A, DK, DV), 0.5),
        "state_rglru": nrm(ks[4], (DEPTH, DEC_BATCH, W_B), 0.5),
        "state_conv": nrm(ks[5], (DEPTH, DEC_BATCH, CONV_W - 1, W_B), 1.0),
        "lower_bounds": nrm(ks[6], (DEPTH + 1, W_A), 0.1),
        "ln1_g": 1.0 + nrm(ks[7], (DEPTH, D_MODEL), 0.02),
        "w_in": nrm(ks[8], (DEPTH, D_MODEL, IN_COLS), D_MODEL ** -0.5),
        "hgrn_norm_g": 1.0 + nrm(ks[9], (DEPTH, W_A), 0.02),
        "conv_w": nrm(ks[10], (DEPTH, CONV_W, W_B), CONV_W ** -0.5),
        "conv_b": nrm(ks[11], (DEPTH, W_B), 0.01),
        "lru_wx": nrm(ks[12], (DEPTH, LRU_BLOCKS, LRU_BLOCK_DIM, LRU_BLOCK_DIM), LRU_BLOCK_DIM ** -0.5),
        "lru_bx": nrm(ks[13], (DEPTH, W_B), 0.01),
        "lru_wa": nrm(ks[14], (DEPTH, LRU_BLOCKS, LRU_BLOCK_DIM, LRU_BLOCK_DIM), LRU_BLOCK_DIM ** -0.5),
        "lru_ba": nrm(ks[15], (DEPTH, W_B), 0.01),
        "lru_lambda": lam,
        "w_out": nrm(ks[16], (DEPTH, D_MIX, D_MODEL), D_MIX ** -0.5),
        "ln2_g": 1.0 + nrm(ks[17], (DEPTH, D_MODEL), 0.02),
        "router_wg": nrm(ks[18], (DEPTH, D_MODEL, N_GROUPS), D_MODEL ** -0.5),
        "router_bg": nrm(ks[19], (DEPTH, N_GROUPS), 0.01),
        "router_we": nrm(ks[20], (DEPTH, N_GROUPS, D_MODEL, EXPERTS_PER_GROUP), D_MODEL ** -0.5),
        "router_be": nrm(ks[21], (DEPTH, N_GROUPS, EXPERTS_PER_GROUP), 0.01),
        "exp_w1": nrm(ks[22], (DEPTH, N_EXPERTS, D_MODEL, D_EXPERT), D_MODEL ** -0.5),
        "exp_w3": nrm(ks[23], (DEPTH, N_EXPERTS, D_MODEL, D_EXPERT), D_MODEL ** -0.5),
        "exp_w2": nrm(ks[24], (DEPTH, N_EXPERTS, D_EXPERT, D_MODEL), D_EXPERT ** -0.5),
        "final_g": 1.0 + nrm(ks[25], (D_MODEL,), 0.02),
    }


def reference(x_prompt, x_sample, state_hgrn, state_rglru, state_conv, lower_bounds, ln1_g, w_in,
              hgrn_norm_g, conv_w, conv_b, lru_wx, lru_bx, lru_wa, lru_ba, lru_lambda, w_out, ln2_g,
              router_wg, router_bg, router_we, router_be, exp_w1, exp_w3, exp_w2, final_g):
    p = (lower_bounds, ln1_g, w_in, hgrn_norm_g, conv_w, conv_b, lru_wx, lru_bx, lru_wa, lru_ba,
         lru_lambda, w_out, ln2_g, router_wg, router_bg, router_we, router_be,
         exp_w1, exp_w3, exp_w2, final_g)
    S0p = jnp.zeros((DEPTH, BATCH, H_A, DK, DV), F32)
    h0p = jnp.zeros((DEPTH, BATCH, W_B), F32)
    b0p = jnp.zeros((DEPTH, BATCH, CONV_W - 1, W_B), x_prompt.dtype)
    y_prompt, hg_p, lru_p, conv_p = _trunk(x_prompt, S0p, h0p, b0p, True, p)
    y_sample, hg_s, lru_s, conv_s = _trunk(x_sample, state_hgrn, state_rglru, state_conv, False, p)
    return (y_prompt, y_sample,
            hg_p.astype(state_hgrn.dtype), lru_p.astype(state_rglru.dtype), conv_p.astype(state_conv.dtype),
            hg_s.astype(state_hgrn.dtype), lru_s.astype(state_rglru.dtype), conv_s.astype(state_conv.dtype))
```

```python
import functools

import jax
import jax.numpy as jnp
from jax import lax
from jax.experimental import pallas as pl
from jax.experimental.pallas import tpu as pltpu

F32 = jnp.float32
BF16 = jnp.bfloat16
HIGHEST = lax.Precision.HIGHEST

EPS = 1e-6
LRU_C = 8.0
HEAD_DIM = 128
CHUNK = 64
SUB = 16
ROUTER_LANES = 128
VMEM_LIMIT_BYTES = 56 * 1024 * 1024

NT_DIMS = (((1,), (1,)), ((), ()))
TN_DIMS = (((0,), (0,)), ((), ()))


def _rms(x, g):
    return x * lax.rsqrt(jnp.mean(x * x, axis=-1, keepdims=True) + EPS) * g


def _sigmoid(x):
    return 1.0 / (1.0 + jnp.exp(-x))


def _silu(x):
    return x * _sigmoid(x)


def _gelu_tanh(x):
    c = 0.7978845608028654
    return x * (0.5 * (1.0 + jnp.tanh(c * (x + 0.044715 * (x * x * x)))))


def _softplus(z):
    return jnp.maximum(z, 0.0) + jnp.log1p(jnp.exp(-jnp.abs(z)))


def _expm1(x):
    u = jnp.exp(x)
    um1 = u - 1.0
    small = um1 * x / jnp.log(u)
    return jnp.where(um1 == 0.0, x, jnp.where(jnp.abs(x) < 0.5, small, um1))


def _forget_lower_bound(lbw):
    m = jnp.max(lbw, axis=0, keepdims=True)
    e = jnp.exp(lbw - m)
    return e[0:1, :] / jnp.sum(e, axis=0, keepdims=True)


def _route(logits, n_groups, per_group):
    n = logits.shape[-1]
    col = lax.broadcasted_iota(jnp.int32, logits.shape, 1)
    neg = jnp.float32(-jnp.inf)
    big = jnp.int32(n)
    is_g = col < n_groups
    lg = jnp.where(is_g, logits, neg)
    mg = jnp.max(lg, axis=-1, keepdims=True)
    g_idx = jnp.min(jnp.where(lg == mg, col, big), axis=-1, keepdims=True)
    p_top = 1.0 / jnp.sum(jnp.where(is_g, jnp.exp(logits - mg), 0.0), axis=-1, keepdims=True)
    lo = n_groups + per_group * g_idx
    le = jnp.where((col >= lo) & (col < lo + per_group), logits, neg)
    m1 = jnp.max(le, axis=-1, keepdims=True)
    i1 = jnp.min(jnp.where(le == m1, col, big), axis=-1, keepdims=True)
    le2 = jnp.where(col == i1, neg, le)
    m2 = jnp.max(le2, axis=-1, keepdims=True)
    i2 = jnp.min(jnp.where(le2 == m2, col, big), axis=-1, keepdims=True)
    e2 = jnp.exp(m2 - m1)
    den = 1.0 + e2
    w1 = p_top / den
    w2 = p_top * (e2 / den)
    return jnp.where(col == i1, w1, 0.0) + jnp.where(col == i2, w2, 0.0)


def _mixer_prompt_kernel(x_ref, lbw_ref, g1_ref, win_ref, hgg_ref, cw_ref, cb_ref, wx_ref, bx_ref,
                         wa_ref, ba_ref, lam_ref, wout_ref, g2_ref, wr_ref, br_ref,
                         x1_ref, xn2_ref, gates_ref, sout_ref, hout_ref, cout_ref,
                         proj_s, k_s, o_s, st_s, xr_s, a_s, u_s, hcar_s,
                         *, n_groups, per_group):
    lb_t = x_ref.shape[1]
    wa_w = k_s.shape[1]
    wb_w = a_s.shape[1]
    n_heads = wa_w // HEAD_DIM
    j = pl.program_id(1)
    nj = pl.num_programs(1)

    @pl.when(j == 0)
    def _():
        st_s[...] = jnp.zeros_like(st_s)
        hcar_s[...] = jnp.zeros_like(hcar_s)
        xr_s[0:8, :] = jnp.zeros((8, wb_w), F32)

    x = x_ref[0]
    xn = _rms(x, g1_ref[...]).astype(BF16)
    n_cols = win_ref.shape[1]
    for c in range(0, n_cols, 512):
        proj_s[:, c:c + 512] = jnp.dot(xn, win_ref[:, c:c + 512], preferred_element_type=F32)

    lb = _forget_lower_bound(lbw_ref[...])
    f = lb + (1.0 - lb) * _sigmoid(proj_s[:, wa_w:2 * wa_w])
    k_s[...] = 1.0 - f
    logf = jnp.log(f)
    r_i = lax.broadcasted_iota(jnp.int32, (CHUNK, CHUNK), 0)
    c_i = lax.broadcasted_iota(jnp.int32, (CHUNK, CHUNK), 1)
    tri = (r_i >= c_i).astype(F32)
    for c in range(0, lb_t, CHUNK):
        proj_s[c:c + CHUNK, wa_w:2 * wa_w] = jnp.dot(tri, logf[c:c + CHUNK, :], precision=HIGHEST,
                                                      preferred_element_type=F32)

    row_sub = lax.broadcasted_iota(jnp.int32, (SUB, 1), 0)

    def chunk_body(ci, carry):
        r0 = pl.multiple_of(ci * CHUNK, CHUNK)
        for h in range(n_heads):
            hs = slice(h * HEAD_DIM, (h + 1) * HEAD_DIM)
            q = proj_s[pl.ds(r0, CHUNK), hs]
            b = proj_s[pl.ds(r0, CHUNK), wa_w + h * HEAD_DIM:wa_w + (h + 1) * HEAD_DIM]
            v = proj_s[pl.ds(r0, CHUNK), 2 * wa_w + h * HEAD_DIM:2 * wa_w + (h + 1) * HEAD_DIM]
            k = k_s[pl.ds(r0, CHUNK), hs]
            st = st_s[h]
            b_last = b[CHUNK - 1:CHUNK, :]
            o = lax.dot_general((q * jnp.exp(b)).astype(BF16), st.astype(BF16), NT_DIMS,
                                preferred_element_type=F32)
            k_end = k * jnp.exp(b_last - b)
            st_s[h] = st * jnp.exp(b_last) + lax.dot_general(v.astype(BF16), k_end.astype(BF16), TN_DIMS,
                                                              preferred_element_type=F32)
            outs = []
            for i in range(CHUNK // SUB):
                lo = i * SUB
                qi, bi, ki, vi = q[lo:lo + SUB], b[lo:lo + SUB], k[lo:lo + SUB], v[lo:lo + SUB]
                od = o[lo:lo + SUB]
                for s in range(SUB):
                    e = jnp.exp(jnp.minimum(bi - bi[s:s + 1], 0.0))
                    col = jnp.sum(qi * (ki[s:s + 1] * e), axis=-1, keepdims=True)
                    od = od + jnp.where(row_sub >= s, col, 0.0) * vi[s:s + 1]
                if i > 0:
                    r = b[lo - 1:lo]
                    qt = (qi * jnp.exp(bi - r)).astype(BF16)
                    kt = (k[:lo] * jnp.exp(r - b[:lo])).astype(BF16)
                    a = lax.dot_general(qt, kt, NT_DIMS, preferred_element_type=F32)
                    od = od + jnp.dot(a.astype(BF16), v[:lo].astype(BF16), preferred_element_type=F32)
                outs.append(od)
            o_s[pl.ds(r0, CHUNK), hs] = jnp.concatenate(outs, axis=0)
        return carry

    lax.fori_loop(0, lb_t // CHUNK, chunk_body, 0)

    ya = []
    for h in range(n_heads):
        hs = slice(h * HEAD_DIM, (h + 1) * HEAD_DIM)
        oh = o_s[:, hs]
        oh = oh * lax.rsqrt(jnp.mean(oh * oh, axis=-1, keepdims=True) + EPS) * hgg_ref[:, hs]
        ya.append(oh * _silu(proj_s[:, 3 * wa_w + h * HEAD_DIM:3 * wa_w + (h + 1) * HEAD_DIM]))

    xb0 = 4 * wa_w
    xr_s[pl.ds(8, lb_t), :] = proj_s[:, xb0:xb0 + wb_w]
    xc = (cb_ref[...] + cw_ref[3:4, :] * xr_s[pl.ds(8, lb_t), :] + cw_ref[2:3, :] * xr_s[pl.ds(7, lb_t), :]
          + cw_ref[1:2, :] * xr_s[pl.ds(6, lb_t), :] + cw_ref[0:1, :] * xr_s[pl.ds(5, lb_t), :])
    tail = xr_s[pl.ds(lb_t + 5, 3), :]
    xr_s[5:8, :] = tail
    cout_ref[0] = tail
    xcb = xc.astype(BF16)
    gate_x = _sigmoid(jnp.dot(xcb, wx_ref[...], preferred_element_type=F32) + bx_ref[...])
    gate_a = _sigmoid(jnp.dot(xcb, wa_ref[...], preferred_element_type=F32) + ba_ref[...])
    log_a = (-LRU_C) * gate_a * _softplus(-lam_ref[...])
    a = jnp.exp(log_a)
    mult = jnp.sqrt(-_expm1(2.0 * log_a))
    first = (lax.broadcasted_iota(jnp.int32, (lb_t, 1), 0) == 0) & (j == 0)
    a = jnp.where(first, 0.0, a)
    mult = jnp.where(first, 1.0, mult)
    a_s[...] = a
    u_s[...] = gate_x * xc * mult
    row8 = lax.broadcasted_iota(jnp.int32, (8, 1), 0)

    def scan_body(gi, carry):
        r0 = pl.multiple_of(gi * 8, 8)
        aa = a_s[pl.ds(r0, 8), :]
        uu = u_s[pl.ds(r0, 8), :]
        for s in (1, 2, 4):
            m = row8 >= s
            uu = jnp.where(m, aa * pltpu.roll(uu, s, 0) + uu, uu)
            aa = jnp.where(m, aa * pltpu.roll(aa, s, 0), aa)
        hh = aa * carry + uu
        u_s[pl.ds(r0, 8), :] = hh
        return hh[7:8, :]

    h_last = lax.fori_loop(0, lb_t // 8, scan_body, hcar_s[...])
    hcar_s[...] = h_last
    hout_ref[0] = h_last
    yb = u_s[...] * _gelu_tanh(proj_s[:, xb0 + wb_w:xb0 + 2 * wb_w])

    y = jnp.concatenate(ya + [yb], axis=-1).astype(BF16)
    x1 = x + jnp.dot(y, wout_ref[...], preferred_element_type=F32)
    x1_ref[0] = x1
    xn2 = _rms(x1, g2_ref[...])
    xn2_ref[0] = xn2.astype(BF16)
    logits = jnp.dot(xn2, wr_ref[...], precision=HIGHEST, preferred_element_type=F32) + br_ref[...]
    gates_ref[0] = _route(logits, n_groups, per_group)

    @pl.when(j == nj - 1)
    def _():
        for h in range(n_heads):
            sout_ref[0, h] = st_s[h].T


def _const_spec(shape):
    nd = len(shape)
    return pl.BlockSpec(shape, lambda *_: (0,) * nd)


def _mixer_prompt(x, p, lb_t):
    bsz, seq, d = x.shape
    wa_w = p["hgg"].shape[1]
    wb_w = p["cb"].shape[1]
    n_heads = wa_w // HEAD_DIM
    n_cols = p["win_bf"].shape[1]
    weights = [p["lbw"], p["g1"], p["win_bf"], p["hgg"], p["cw"], p["cb"], p["wx_bf"], p["bx"],
               p["wa_bf"], p["ba"], p["lam"], p["wout_bf"], p["g2"], p["wr"], p["br"]]
    tile = lambda w: pl.BlockSpec((1, lb_t, w), lambda b, j: (b, j, 0))
    out_shape = (
        jax.ShapeDtypeStruct((bsz, seq, d), F32),
        jax.ShapeDtypeStruct((bsz, seq, d), BF16),
        jax.ShapeDtypeStruct((bsz, seq, ROUTER_LANES), F32),
        jax.ShapeDtypeStruct((bsz, n_heads, HEAD_DIM, HEAD_DIM), F32),
        jax.ShapeDtypeStruct((bsz, 1, wb_w), F32),
        jax.ShapeDtypeStruct((bsz, 3, wb_w), F32),
    )
    out_specs = (
        tile(d), tile(d), tile(ROUTER_LANES),
        pl.BlockSpec((1, n_heads, HEAD_DIM, HEAD_DIM), lambda b, j: (b, 0, 0, 0)),
        pl.BlockSpec((1, 1, wb_w), lambda b, j: (b, 0, 0)),
        pl.BlockSpec((1, 3, wb_w), lambda b, j: (b, 0, 0)),
    )
    scratch = [
        pltpu.VMEM((lb_t, n_cols), F32),
        pltpu.VMEM((lb_t, wa_w), F32),
        pltpu.VMEM((lb_t, wa_w), F32),
        pltpu.VMEM((n_heads, HEAD_DIM, HEAD_DIM), F32),
        pltpu.VMEM((lb_t + 8, wb_w), F32),
        pltpu.VMEM((lb_t, wb_w), F32),
        pltpu.VMEM((lb_t, wb_w), F32),
        pltpu.VMEM((1, wb_w), F32),
    ]
    kern = functools.partial(_mixer_prompt_kernel, n_groups=p["n_groups"], per_group=p["per_group"])
    return pl.pallas_call(
        kern,
        grid=(bsz, seq // lb_t),
        in_specs=[tile(d)] + [_const_spec(w.shape) for w in weights],
        out_specs=out_specs,
        out_shape=out_shape,
        scratch_shapes=scratch,
        compiler_params=pltpu.CompilerParams(dimension_semantics=("arbitrary", "arbitrary"),
                                             vmem_limit_bytes=VMEM_LIMIT_BYTES),
        name="mixer_prompt",
    )(x, *weights)


def _moe_kernel(x1_ref, xn2_ref, gates_ref, w1_ref, w3_ref, w2_ref, gf_ref, y_ref, acc_s, *, n_groups):
    e = pl.program_id(1)

    @pl.when(e == 0)
    def _():
        acc_s[...] = jnp.zeros_like(acc_s)

    xb = xn2_ref[...]
    col = lax.broadcasted_iota(jnp.int32, gates_ref.shape, 1)
    gate = jnp.sum(jnp.where(col == e + n_groups, gates_ref[...], 0.0), axis=-1, keepdims=True)
    h = _silu(jnp.dot(xb, w1_ref[0], preferred_element_type=F32)) * jnp.dot(xb, w3_ref[0], preferred_element_type=F32)
    acc_s[...] += jnp.dot((h * gate).astype(BF16), w2_ref[0], preferred_element_type=F32)

    @pl.when(e == pl.num_programs(1) - 1)
    def _():
        y_ref[...] = _rms(x1_ref[...] + acc_s[...], gf_ref[...])


def _moe_dense(x1, xn2, gates, p, tm):
    t, d = x1.shape
    n_exp, _, d_exp = p["w1_bf"].shape
    row = lambda w: pl.BlockSpec((tm, w), lambda i, e: (i, 0))
    return pl.pallas_call(
        functools.partial(_moe_kernel, n_groups=p["n_groups"]),
        grid=(t // tm, n_exp),
        in_specs=[row(d), row(d), row(ROUTER_LANES),
                  pl.BlockSpec((1, d, d_exp), lambda i, e: (e, 0, 0)),
                  pl.BlockSpec((1, d, d_exp), lambda i, e: (e, 0, 0)),
                  pl.BlockSpec((1, d_exp, d), lambda i, e: (e, 0, 0)),
                  _const_spec(p["gf"].shape)],
        out_specs=row(d),
        out_shape=jax.ShapeDtypeStruct((t, d), F32),
        scratch_shapes=[pltpu.VMEM((tm, d), F32)],
        compiler_params=pltpu.CompilerParams(dimension_semantics=("arbitrary", "arbitrary"),
                                             vmem_limit_bytes=VMEM_LIMIT_BYTES),
        name="moe_dense",
    )(x1, xn2, gates, p["w1_bf"], p["w3_bf"], p["w2_bf"], p["gf"])


def _sample_in_kernel(x_ref, c0_ref, c1_ref, c2_ref, h0_ref, lbw_ref, g1_ref, win_ref, cw_ref, cb_ref,
                      wx_ref, bx_ref, wa_ref, ba_ref, lam_ref,
                      q_ref, f_ref, k_ref, v_ref, g_ref, yb_ref, hnew_ref, xr_ref):
    wa_w = q_ref.shape[1]
    wb_w = yb_ref.shape[1]
    xn = _rms(x_ref[...], g1_ref[...])
    proj = jnp.dot(xn, win_ref[...], precision=HIGHEST, preferred_element_type=F32)
    lb = _forget_lower_bound(lbw_ref[...])
    f = lb + (1.0 - lb) * _sigmoid(proj[:, wa_w:2 * wa_w])
    q_ref[...] = proj[:, 0:wa_w]
    f_ref[...] = f
    k_ref[...] = 1.0 - f
    v_ref[...] = proj[:, 2 * wa_w:3 * wa_w]
    g_ref[...] = proj[:, 3 * wa_w:4 * wa_w]
    xr = proj[:, 4 * wa_w:4 * wa_w + wb_w]
    xr_ref[...] = xr
    xc = (cb_ref[...] + cw_ref[0:1, :] * c0_ref[...] + cw_ref[1:2, :] * c1_ref[...]
          + cw_ref[2:3, :] * c2_ref[...] + cw_ref[3:4, :] * xr)
    gate_x = _sigmoid(jnp.dot(xc, wx_ref[...], precision=HIGHEST, preferred_element_type=F32) + bx_ref[...])
    gate_a = _sigmoid(jnp.dot(xc, wa_ref[...], precision=HIGHEST, preferred_element_type=F32) + ba_ref[...])
    log_a = (-LRU_C) * gate_a * _softplus(-lam_ref[...])
    a = jnp.exp(log_a)
    mult = jnp.sqrt(-_expm1(2.0 * log_a))
    h = a * h0_ref[...] + gate_x * xc * mult
    hnew_ref[...] = h
    yb_ref[...] = h * _gelu_tanh(proj[:, 4 * wa_w + wb_w:4 * wa_w + 2 * wb_w])


def _sample_state_kernel(s_ref, q_ref, f_ref, k_ref, v_ref, snew_ref, o_ref):
    tb, n_heads = s_ref.shape[0], s_ref.shape[1]
    r_i = lax.broadcasted_iota(jnp.int32, (HEAD_DIM, HEAD_DIM), 0)
    c_i = lax.broadcasted_iota(jnp.int32, (HEAD_DIM, HEAD_DIM), 1)
    eye = r_i == c_i
    for h in range(n_heads):
        hs = slice(h * HEAD_DIM, (h + 1) * HEAD_DIM)
        q8 = q_ref[:, hs]
        rows = []
        for t in range(tb):
            df = jnp.where(eye, f_ref[t:t + 1, hs], 0.0)
            dk = jnp.where(eye, k_ref[t:t + 1, hs], 0.0)
            vb = jnp.broadcast_to(v_ref[t:t + 1, hs], (HEAD_DIM, HEAD_DIM))
            s_new = jnp.dot(jnp.concatenate([df, dk], axis=1), jnp.concatenate([s_ref[t, h], vb], axis=0),
                            precision=HIGHEST, preferred_element_type=F32)
            snew_ref[t, h] = s_new
            rows.append(jnp.dot(q8, s_new, precision=HIGHEST, preferred_element_type=F32)[t:t + 1, :])
        o_ref[:, hs] = jnp.concatenate(rows, axis=0)


def _sample_out_kernel(x_ref, o_ref, g_ref, yb_ref, hgg_ref, wout_ref, g2_ref, wr_ref, br_ref,
                       x1_ref, xn2_ref, gates_ref, *, n_groups, per_group):
    wa_w = o_ref.shape[1]
    ya = []
    for h in range(wa_w // HEAD_DIM):
        hs = slice(h * HEAD_DIM, (h + 1) * HEAD_DIM)
        oh = o_ref[:, hs]
        oh = oh * lax.rsqrt(jnp.mean(oh * oh, axis=-1, keepdims=True) + EPS) * hgg_ref[:, hs]
        ya.append(oh * _silu(g_ref[:, hs]))
    y = jnp.concatenate(ya + [yb_ref[...]], axis=-1)
    x1 = x_ref[...] + jnp.dot(y, wout_ref[...], precision=HIGHEST, preferred_element_type=F32)
    x1_ref[...] = x1
    xn2 = _rms(x1, g2_ref[...])
    xn2_ref[...] = xn2.astype(BF16)
    logits = jnp.dot(xn2, wr_ref[...], precision=HIGHEST, preferred_element_type=F32) + br_ref[...]
    gates_ref[...] = _route(logits, n_groups, per_group)


def _whole(kernel, out_shape, *args, name):
    return pl.pallas_call(
        kernel, out_shape=out_shape,
        compiler_params=pltpu.CompilerParams(vmem_limit_bytes=VMEM_LIMIT_BYTES), name=name)(*args)


def _mixer_sample(x, s0, h0, c0, p, tb):
    n, d = x.shape
    wa_w = p["hgg"].shape[1]
    wb_w = p["cb"].shape[1]
    n_heads = wa_w // HEAD_DIM
    sd = lambda w: jax.ShapeDtypeStruct((n, w), F32)
    q, f, k, v, g, yb, h_new, xr = _whole(
        _sample_in_kernel, (sd(wa_w),) * 5 + (sd(wb_w),) * 3,
        x, c0[:, 0, :], c0[:, 1, :], c0[:, 2, :], h0, p["lbw"], p["g1"], p["win"], p["cw"], p["cb"],
        p["wx"], p["bx"], p["wa"], p["ba"], p["lam"], name="sample_in")
    tok = lambda w: pl.BlockSpec((tb, w), lambda i: (i, 0))
    st = pl.BlockSpec((tb, n_heads, HEAD_DIM, HEAD_DIM), lambda i: (i, 0, 0, 0))
    s_new, o = pl.pallas_call(
        _sample_state_kernel,
        grid=(n // tb,),
        in_specs=[st, tok(wa_w), tok(wa_w), tok(wa_w), tok(wa_w)],
        out_specs=(st, tok(wa_w)),
        out_shape=(jax.ShapeDtypeStruct(s0.shape, F32), sd(wa_w)),
        compiler_params=pltpu.CompilerParams(dimension_semantics=("arbitrary",),
                                             vmem_limit_bytes=VMEM_LIMIT_BYTES),
        name="sample_state",
    )(s0, q, f, k, v)
    x1, xn2, gates = _whole(
        functools.partial(_sample_out_kernel, n_groups=p["n_groups"], per_group=p["per_group"]),
        (sd(d), jax.ShapeDtypeStruct((n, d), BF16), sd(ROUTER_LANES)),
        x, o, g, yb, p["hgg"], p["wout"], p["g2"], p["wr"], p["br"], name="sample_out")
    c_new = jnp.stack([c0[:, 1, :], c0[:, 2, :], xr], axis=1)
    return x1, xn2, gates, s_new, h_new, c_new


def _block_diag(w):
    n, c, _ = w.shape
    eye = jnp.eye(n, dtype=w.dtype)
    return (w[:, :, None, :] * eye[:, None, :, None]).reshape(n * c, n * c)


def _prepare(lower_bounds, ln1_g, w_in, hgrn_norm_g, conv_w, conv_b, lru_wx, lru_bx, lru_wa, lru_ba,
             lru_lambda, w_out, ln2_g, router_wg, router_bg, router_we, router_be, exp_w1, exp_w3,
             exp_w2, final_g):
    d = w_in.shape[1]
    n_groups = router_wg.shape[-1]
    per_group = router_we.shape[-1]
    row = lambda a: a.reshape(1, -1).astype(F32)
    we = jnp.transpose(router_we[0], (1, 0, 2)).reshape(d, n_groups * per_group)
    pad = ROUTER_LANES - n_groups - n_groups * per_group
    wr = jnp.concatenate([router_wg[0], we, jnp.zeros((d, pad), F32)], axis=1)
    br = jnp.concatenate([router_bg[0], router_be[0].reshape(-1), jnp.zeros((pad,), F32)]).reshape(1, -1)
    wx = _block_diag(lru_wx[0])
    wa = _block_diag(lru_wa[0])
    return dict(
        n_groups=n_groups, per_group=per_group,
        lbw=lower_bounds.astype(F32), g1=row(ln1_g[0]), win=w_in[0], win_bf=w_in[0].astype(BF16),
        hgg=row(hgrn_norm_g[0]), cw=conv_w[0], cb=row(conv_b[0]),
        wx=wx, wx_bf=wx.astype(BF16), bx=row(lru_bx[0]), wa=wa, wa_bf=wa.astype(BF16), ba=row(lru_ba[0]),
        lam=row(lru_lambda[0]), wout=w_out[0], wout_bf=w_out[0].astype(BF16), g2=row(ln2_g[0]),
        wr=wr, br=br, w1_bf=exp_w1[0].astype(BF16), w3_bf=exp_w3[0].astype(BF16),
        w2_bf=exp_w2[0].astype(BF16), gf=row(final_g))


def kernel(x_prompt, x_sample, state_hgrn, state_rglru, state_conv, lower_bounds, ln1_g, w_in, hgrn_norm_g, conv_w, conv_b, lru_wx, lru_bx, lru_wa, lru_ba, lru_lambda, w_out, ln2_g, router_wg, router_bg, router_we, router_be, exp_w1, exp_w3, exp_w2, final_g):
    assert w_in.shape[0] == 1, "single-layer trunk"
    p = _prepare(lower_bounds, ln1_g, w_in, hgrn_norm_g, conv_w, conv_b, lru_wx, lru_bx, lru_wa, lru_ba,
                 lru_lambda, w_out, ln2_g, router_wg, router_bg, router_we, router_be, exp_w1, exp_w3,
                 exp_w2, final_g)
    bsz, seq, d = x_prompt.shape
    lb_t = min(512, seq)
    x1, xn2, gates, s_p, h_p, c_p = _mixer_prompt(x_prompt, p, lb_t)
    t = bsz * seq
    y_p = _moe_dense(x1.reshape(t, d), xn2.reshape(t, d), gates.reshape(t, ROUTER_LANES), p, min(1024, t))

    n = x_sample.shape[0]
    x1s, xn2s, gates_s, s_s, h_s, c_s = _mixer_sample(x_sample[:, 0, :], state_hgrn[0], state_rglru[0],
                                                      state_conv[0], p, 8)
    y_s = _moe_dense(x1s, xn2s, gates_s, p, n)
    return (y_p.reshape(bsz, seq, d), y_s.reshape(n, 1, d),
            s_p[None], h_p.reshape(1, bsz, -1), c_p[None],
            s_s[None], h_s[None], c_s[None])
```

```python
import functools

import jax
import jax.numpy as jnp
from jax import lax
from jax.experimental import pallas as pl
from jax.experimental.pallas import tpu as pltpu

F32 = jnp.float32
BF16 = jnp.bfloat16
HIGHEST = lax.Precision.HIGHEST

EPS = 1e-6
LRU_C = 8.0
HEAD_DIM = 128
CHUNK = 64
SUB = 16
ROUTER_LANES = 128
TOK_TILE = 256
SEG = 16
SORT_ROWS = 384
ROW_TILE = 512
VMEM_LIMIT_BYTES = 56 * 1024 * 1024

NT_DIMS = (((1,), (1,)), ((), ()))
TN_DIMS = (((0,), (0,)), ((), ()))


def _rms(x, g):
    return x * lax.rsqrt(jnp.mean(x * x, axis=-1, keepdims=True) + EPS) * g


def _sigmoid(x):
    return 1.0 / (1.0 + jnp.exp(-x))


def _silu(x):
    return x * _sigmoid(x)


def _gelu_tanh(x):
    c = 0.7978845608028654
    return x * (0.5 * (1.0 + jnp.tanh(c * (x + 0.044715 * (x * x * x)))))


def _softplus(z):
    return jnp.maximum(z, 0.0) + jnp.log1p(jnp.exp(-jnp.abs(z)))


def _expm1(x):
    u = jnp.exp(x)
    um1 = u - 1.0
    small = um1 * x / jnp.log(u)
    return jnp.where(um1 == 0.0, x, jnp.where(jnp.abs(x) < 0.5, small, um1))


def _forget_lower_bound(lbw):
    m = jnp.max(lbw, axis=0, keepdims=True)
    e = jnp.exp(lbw - m)
    return e[0:1, :] / jnp.sum(e, axis=0, keepdims=True)


def _route(logits, n_groups, per_group):
    n = logits.shape[-1]
    col = lax.broadcasted_iota(jnp.int32, logits.shape, 1)
    neg = jnp.float32(-jnp.inf)
    big = jnp.int32(n)
    is_g = col < n_groups
    lg = jnp.where(is_g, logits, neg)
    mg = jnp.max(lg, axis=-1, keepdims=True)
    g_idx = jnp.min(jnp.where(lg == mg, col, big), axis=-1, keepdims=True)
    p_top = 1.0 / jnp.sum(jnp.where(is_g, jnp.exp(logits - mg), 0.0), axis=-1, keepdims=True)
    lo = n_groups + per_group * g_idx
    le = jnp.where((col >= lo) & (col < lo + per_group), logits, neg)
    m1 = jnp.max(le, axis=-1, keepdims=True)
    i1 = jnp.min(jnp.where(le == m1, col, big), axis=-1, keepdims=True)
    le2 = jnp.where(col == i1, neg, le)
    m2 = jnp.max(le2, axis=-1, keepdims=True)
    i2 = jnp.min(jnp.where(le2 == m2, col, big), axis=-1, keepdims=True)
    e2 = jnp.exp(m2 - m1)
    den = 1.0 + e2
    w1 = p_top / den
    w2 = p_top * (e2 / den)
    gates = jnp.where(col == i1, w1, 0.0) + jnp.where(col == i2, w2, 0.0)
    return gates + jnp.where(col == 0, g_idx.astype(F32), 0.0)


def _mixer_prompt_kernel(x_ref, lbw_ref, g1_ref, win_ref, hgg_ref, cw_ref, cb_ref, wx_ref, bx_ref,
                         wa_ref, ba_ref, lam_ref, wout_ref, g2_ref, wr_ref, br_ref,
                         x1_ref, xn2_ref, gates_ref, cnt_ref, sout_ref, hout_ref, cout_ref,
                         proj_s, k_s, o_s, st_s, xr_s, a_s, u_s, hcar_s,
                         *, n_groups, per_group):
    lb_t = x_ref.shape[1]
    wa_w = k_s.shape[1]
    wb_w = a_s.shape[1]
    n_heads = wa_w // HEAD_DIM
    j = pl.program_id(1)
    nj = pl.num_programs(1)

    @pl.when(j == 0)
    def _():
        st_s[...] = jnp.zeros_like(st_s)
        hcar_s[...] = jnp.zeros_like(hcar_s)
        xr_s[0:8, :] = jnp.zeros((8, wb_w), F32)

    x = x_ref[0]
    xn = _rms(x, g1_ref[...]).astype(BF16)
    n_cols = win_ref.shape[1]
    for c in range(0, n_cols, 512):
        proj_s[:, c:c + 512] = jnp.dot(xn, win_ref[:, c:c + 512], preferred_element_type=F32)

    lb = _forget_lower_bound(lbw_ref[...])
    f = lb + (1.0 - lb) * _sigmoid(proj_s[:, wa_w:2 * wa_w])
    k_s[...] = 1.0 - f
    logf = jnp.log(f)
    r_i = lax.broadcasted_iota(jnp.int32, (CHUNK, CHUNK), 0)
    c_i = lax.broadcasted_iota(jnp.int32, (CHUNK, CHUNK), 1)
    tri = (r_i >= c_i).astype(F32)
    for c in range(0, lb_t, CHUNK):
        proj_s[c:c + CHUNK, wa_w:2 * wa_w] = jnp.dot(tri, logf[c:c + CHUNK, :], precision=HIGHEST,
                                                      preferred_element_type=F32)

    row_sub = lax.broadcasted_iota(jnp.int32, (SUB, 1), 0)

    def chunk_body(ci, carry):
        r0 = pl.multiple_of(ci * CHUNK, CHUNK)
        for h in range(n_heads):
            hs = slice(h * HEAD_DIM, (h + 1) * HEAD_DIM)
            q = proj_s[pl.ds(r0, CHUNK), hs]
            b = proj_s[pl.ds(r0, CHUNK), wa_w + h * HEAD_DIM:wa_w + (h + 1) * HEAD_DIM]
            v = proj_s[pl.ds(r0, CHUNK), 2 * wa_w + h * HEAD_DIM:2 * wa_w + (h + 1) * HEAD_DIM]
            k = k_s[pl.ds(r0, CHUNK), hs]
            st = st_s[h]
            b_last = b[CHUNK - 1:CHUNK, :]
            o = lax.dot_general((q * jnp.exp(b)).astype(BF16), st.astype(BF16), NT_DIMS,
                                preferred_element_type=F32)
            k_end = k * jnp.exp(b_last - b)
            st_s[h] = st * jnp.exp(b_last) + lax.dot_general(v.astype(BF16), k_end.astype(BF16), TN_DIMS,
                                                              preferred_element_type=F32)
            outs = []
            for i in range(CHUNK // SUB):
                lo = i * SUB
                qi, bi, ki, vi = q[lo:lo + SUB], b[lo:lo + SUB], k[lo:lo + SUB], v[lo:lo + SUB]
                od = o[lo:lo + SUB]
                for s in range(SUB):
                    e = jnp.exp(jnp.minimum(bi - bi[s:s + 1], 0.0))
                    col = jnp.sum(qi * (ki[s:s + 1] * e), axis=-1, keepdims=True)
                    od = od + jnp.where(row_sub >= s, col, 0.0) * vi[s:s + 1]
                if i > 0:
                    r = b[lo - 1:lo]
                    qt = (qi * jnp.exp(bi - r)).astype(BF16)
                    kt = (k[:lo] * jnp.exp(r - b[:lo])).astype(BF16)
                    a = lax.dot_general(qt, kt, NT_DIMS, preferred_element_type=F32)
                    od = od + jnp.dot(a.astype(BF16), v[:lo].astype(BF16), preferred_element_type=F32)
                outs.append(od)
            o_s[pl.ds(r0, CHUNK), hs] = jnp.concatenate(outs, axis=0)
        return carry

    lax.fori_loop(0, lb_t // CHUNK, chunk_body, 0)

    ya = []
    for h in range(n_heads):
        hs = slice(h * HEAD_DIM, (h + 1) * HEAD_DIM)
        oh = o_s[:, hs]
        oh = oh * lax.rsqrt(jnp.mean(oh * oh, axis=-1, keepdims=True) + EPS) * hgg_ref[:, hs]
        ya.append(oh * _silu(proj_s[:, 3 * wa_w + h * HEAD_DIM:3 * wa_w + (h + 1) * HEAD_DIM]))

    xb0 = 4 * wa_w
    xr_s[pl.ds(8, lb_t), :] = proj_s[:, xb0:xb0 + wb_w]
    xc = (cb_ref[...] + cw_ref[3:4, :] * xr_s[pl.ds(8, lb_t), :] + cw_ref[2:3, :] * xr_s[pl.ds(7, lb_t), :]
          + cw_ref[1:2, :] * xr_s[pl.ds(6, lb_t), :] + cw_ref[0:1, :] * xr_s[pl.ds(5, lb_t), :])
    tail = xr_s[pl.ds(lb_t + 5, 3), :]
    xr_s[5:8, :] = tail
    cout_ref[0] = tail
    xcb = xc.astype(BF16)
    gate_x = _sigmoid(jnp.dot(xcb, wx_ref[...], preferred_element_type=F32) + bx_ref[...])
    gate_a = _sigmoid(jnp.dot(xcb, wa_ref[...], preferred_element_type=F32) + ba_ref[...])
    log_a = (-LRU_C) * gate_a * _softplus(-lam_ref[...])
    a = jnp.exp(log_a)
    mult = jnp.sqrt(-_expm1(2.0 * log_a))
    first = (lax.broadcasted_iota(jnp.int32, (lb_t, 1), 0) == 0) & (j == 0)
    a = jnp.where(first, 0.0, a)
    mult = jnp.where(first, 1.0, mult)
    a_s[...] = a
    u_s[...] = gate_x * xc * mult
    row8 = lax.broadcasted_iota(jnp.int32, (8, 1), 0)

    def scan_body(gi, carry):
        r0 = pl.multiple_of(gi * 8, 8)
        aa = a_s[pl.ds(r0, 8), :]
        uu = u_s[pl.ds(r0, 8), :]
        for s in (1, 2, 4):
            m = row8 >= s
            uu = jnp.where(m, aa * pltpu.roll(uu, s, 0) + uu, uu)
            aa = jnp.where(m, aa * pltpu.roll(aa, s, 0), aa)
        hh = aa * carry + uu
        u_s[pl.ds(r0, 8), :] = hh
        return hh[7:8, :]

    h_last = lax.fori_loop(0, lb_t // 8, scan_body, hcar_s[...])
    hcar_s[...] = h_last
    hout_ref[0] = h_last
    yb = u_s[...] * _gelu_tanh(proj_s[:, xb0 + wb_w:xb0 + 2 * wb_w])

    y = jnp.concatenate(ya + [yb], axis=-1).astype(BF16)
    x1 = x + jnp.dot(y, wout_ref[...], preferred_element_type=F32)
    x1_ref[0] = x1
    xn2 = _rms(x1, g2_ref[...])
    xn2_ref[0] = xn2.astype(BF16)
    logits = jnp.dot(xn2, wr_ref[...], precision=HIGHEST, preferred_element_type=F32) + br_ref[...]
    gates = _route(logits, n_groups, per_group)
    gates_ref[0] = gates
    col_t = lax.broadcasted_iota(jnp.int32, (TOK_TILE, ROUTER_LANES), 1)
    for t in range(lb_t // TOK_TILE):
        gi = gates[t * TOK_TILE:(t + 1) * TOK_TILE, 0:1].astype(jnp.int32)
        cnt_ref[0, t:t + 1, :] = jnp.sum(jnp.where(col_t == gi, 1, 0), axis=0, keepdims=True)

    @pl.when(j == nj - 1)
    def _():
        for h in range(n_heads):
            sout_ref[0, h] = st_s[h].T


def _const_spec(shape):
    nd = len(shape)
    return pl.BlockSpec(shape, lambda *_: (0,) * nd)


def _mixer_prompt(x, p, lb_t):
    bsz, seq, d = x.shape
    wa_w = p["hgg"].shape[1]
    wb_w = p["cb"].shape[1]
    n_heads = wa_w // HEAD_DIM
    n_cols = p["win_bf"].shape[1]
    weights = [p["lbw"], p["g1"], p["win_bf"], p["hgg"], p["cw"], p["cb"], p["wx_bf"], p["bx"],
               p["wa_bf"], p["ba"], p["lam"], p["wout_bf"], p["g2"], p["wr"], p["br"]]
    tile = lambda w: pl.BlockSpec((1, lb_t, w), lambda b, j: (b, j, 0))
    out_shape = (
        jax.ShapeDtypeStruct((bsz, seq, d), F32),
        jax.ShapeDtypeStruct((bsz, seq, d), BF16),
        jax.ShapeDtypeStruct((bsz, seq, ROUTER_LANES), F32),
        jax.ShapeDtypeStruct((bsz * (seq // lb_t), lb_t // TOK_TILE, ROUTER_LANES), jnp.int32),
        jax.ShapeDtypeStruct((bsz, n_heads, HEAD_DIM, HEAD_DIM), F32),
        jax.ShapeDtypeStruct((bsz, 1, wb_w), F32),
        jax.ShapeDtypeStruct((bsz, 3, wb_w), F32),
    )
    out_specs = (
        tile(d), tile(d), tile(ROUTER_LANES),
        pl.BlockSpec((1, lb_t // TOK_TILE, ROUTER_LANES), lambda b, j: (b * (seq // lb_t) + j, 0, 0)),
        pl.BlockSpec((1, n_heads, HEAD_DIM, HEAD_DIM), lambda b, j: (b, 0, 0, 0)),
        pl.BlockSpec((1, 1, wb_w), lambda b, j: (b, 0, 0)),
        pl.BlockSpec((1, 3, wb_w), lambda b, j: (b, 0, 0)),
    )
    scratch = [
        pltpu.VMEM((lb_t, n_cols), F32),
        pltpu.VMEM((lb_t, wa_w), F32),
        pltpu.VMEM((lb_t, wa_w), F32),
        pltpu.VMEM((n_heads, HEAD_DIM, HEAD_DIM), F32),
        pltpu.VMEM((lb_t + 8, wb_w), F32),
        pltpu.VMEM((lb_t, wb_w), F32),
        pltpu.VMEM((lb_t, wb_w), F32),
        pltpu.VMEM((1, wb_w), F32),
    ]
    kern = functools.partial(_mixer_prompt_kernel, n_groups=p["n_groups"], per_group=p["per_group"])
    return pl.pallas_call(
        kern,
        grid=(bsz, seq // lb_t),
        in_specs=[tile(d)] + [_const_spec(w.shape) for w in weights],
        out_specs=out_specs,
        out_shape=out_shape,
        scratch_shapes=scratch,
        compiler_params=pltpu.CompilerParams(dimension_semantics=("arbitrary", "arbitrary"),
                                             vmem_limit_bytes=VMEM_LIMIT_BYTES),
        name="mixer_prompt",
    )(x, *weights)


def _moe_kernel(x1_ref, xn2_ref, gates_ref, w1_ref, w3_ref, w2_ref, gf_ref, y_ref, acc_s, *, n_groups):
    e = pl.program_id(1)

    @pl.when(e == 0)
    def _():
        acc_s[...] = jnp.zeros_like(acc_s)

    xb = xn2_ref[...]
    col = lax.broadcasted_iota(jnp.int32, gates_ref.shape, 1)
    gate = jnp.sum(jnp.where(col == e + n_groups, gates_ref[...], 0.0), axis=-1, keepdims=True)
    h = _silu(jnp.dot(xb, w1_ref[0], preferred_element_type=F32)) * jnp.dot(xb, w3_ref[0], preferred_element_type=F32)
    acc_s[...] += jnp.dot((h * gate).astype(BF16), w2_ref[0], preferred_element_type=F32)

    @pl.when(e == pl.num_programs(1) - 1)
    def _():
        y_ref[...] = _rms(x1_ref[...] + acc_s[...], gf_ref[...])


def _moe_dense(x1, xn2, gates, p, tm):
    t, d = x1.shape
    n_exp, _, d_exp = p["w1_bf"].shape
    row = lambda w: pl.BlockSpec((tm, w), lambda i, e: (i, 0))
    return pl.pallas_call(
        functools.partial(_moe_kernel, n_groups=p["n_groups"]),
        grid=(t // tm, n_exp),
        in_specs=[row(d), row(d), row(ROUTER_LANES),
                  pl.BlockSpec((1, d, d_exp), lambda i, e: (e, 0, 0)),
                  pl.BlockSpec((1, d, d_exp), lambda i, e: (e, 0, 0)),
                  pl.BlockSpec((1, d_exp, d), lambda i, e: (e, 0, 0)),
                  _const_spec(p["gf"].shape)],
        out_specs=row(d),
        out_shape=jax.ShapeDtypeStruct((t, d), F32),
        scratch_shapes=[pltpu.VMEM((tm, d), F32)],
        compiler_params=pltpu.CompilerParams(dimension_semantics=("arbitrary", "arbitrary"),
                                             vmem_limit_bytes=VMEM_LIMIT_BYTES),
        name="moe_dense",
    )(x1, xn2, gates, p["w1_bf"], p["w3_bf"], p["w2_bf"], p["gf"])


def _seg_pad(n):
    return jnp.bitwise_and(n + (SEG - 1), -SEG)


def _row_tile_pad(n):
    return jnp.bitwise_and(n + (ROW_TILE - 1), -ROW_TILE)


def _tile_segments(cnt_ref, tile, n_groups):
    padded = [_seg_pad(cnt_ref[tile * n_groups + g]) for g in range(n_groups)]
    starts, acc = [], jnp.int32(0)
    for g in range(n_groups):
        starts.append(acc)
        acc = acc + padded[g]
    return padded, starts, acc


def _group_bases(cnt_ref, n_tiles, n_groups):
    def body(t, tot):
        return tuple(tot[g] + _seg_pad(cnt_ref[t * n_groups + g]) for g in range(n_groups))
    totals = lax.fori_loop(0, n_tiles, body, (jnp.int32(0),) * n_groups)
    bases, ends, acc = [], [], jnp.int32(0)
    for g in range(n_groups):
        bases.append(acc)
        acc = acc + _row_tile_pad(totals[g])
        ends.append(acc)
    return bases, ends


def _chunk_row(c, padded, starts, gstart):
    lo = c * SEG
    row = jnp.int32(0)
    for g in range(len(padded)):
        inside = (lo >= starts[g]) & (lo < starts[g] + padded[g])
        row = row + jnp.where(inside, gstart[g] + lo - starts[g], 0)
    return pl.multiple_of(row, SEG)


def _sort_matrix(gates, starts):
    n_groups = len(starts)
    col = lax.broadcasted_iota(jnp.int32, (TOK_TILE, ROUTER_LANES), 1)
    gi = gates[:, 0:1].astype(jnp.int32)
    onehot = col == gi
    r_i = lax.broadcasted_iota(jnp.int32, (TOK_TILE, TOK_TILE), 0)
    c_i = lax.broadcasted_iota(jnp.int32, (TOK_TILE, TOK_TILE), 1)
    earlier = jnp.where(r_i > c_i, 1.0, 0.0).astype(BF16)
    before = jnp.dot(earlier, jnp.where(onehot, 1.0, 0.0).astype(BF16), preferred_element_type=F32)
    rank = jnp.sum(jnp.where(onehot, before, 0.0), axis=-1, keepdims=True).astype(jnp.int32)
    base = jnp.zeros((TOK_TILE, 1), jnp.int32)
    for g in range(n_groups):
        base = base + jnp.where(gi == g, starts[g], 0)
    lane = lax.broadcasted_iota(jnp.int32, (TOK_TILE, SORT_ROWS), 1)
    return lane == base + rank


def _dispatch_copies(xbuf, gbuf, xs_hbm, gs_hbm, sem, slot, c, row):
    return (pltpu.make_async_copy(xbuf.at[slot, pl.ds(c * SEG, SEG), :], xs_hbm.at[pl.ds(row, SEG), :], sem.at[slot]),
            pltpu.make_async_copy(gbuf.at[slot, pl.ds(c * SEG, SEG), :], gs_hbm.at[pl.ds(row, SEG), :], sem.at[slot]))


def _moe_dispatch_kernel(cnt_ref, xn2_ref, gates_ref, xs_in, gs_in, xs_hbm, gs_hbm, tg_ref,
                         xbuf, gbuf, sem, gstart_s, *, n_groups, per_group):
    del xs_in, gs_in
    j = pl.program_id(0)
    n_tiles = pl.num_programs(0)
    n_chunks = (TOK_TILE + n_groups * SEG) // SEG
    slot = lax.rem(j, 2)

    @pl.when(j == 0)
    def _():
        bases, ends = _group_bases(cnt_ref, n_tiles, n_groups)
        for g in range(n_groups):
            gstart_s[g] = bases[g]
        n_rt = tg_ref.shape[0] - 1
        for i in range(n_rt):
            tg = jnp.int32(0)
            for g in range(n_groups):
                tg = tg + jnp.where(ends[g] <= i * ROW_TILE, 1, 0)
            tg_ref[i] = tg
        tg_ref[n_rt] = ends[n_groups - 1] // ROW_TILE

    def wait_tile(tile, slot_):
        _, _, used = _tile_segments(cnt_ref, tile, n_groups)
        for c in range(n_chunks):
            @pl.when(c * SEG < used)
            def _():
                for cp in _dispatch_copies(xbuf, gbuf, xs_hbm, gs_hbm, sem, slot_, c, 0):
                    cp.wait()

    @pl.when(j >= 2)
    def _():
        wait_tile(j - 2, slot)

    padded, starts, used = _tile_segments(cnt_ref, j, n_groups)
    gates = gates_ref[...]
    sort_t = _sort_matrix(gates, starts)
    xbuf[slot] = lax.dot_general(jnp.where(sort_t, 1.0, 0.0).astype(BF16), xn2_ref[...], TN_DIMS,
                                 preferred_element_type=F32).astype(BF16)
    r_i = lax.broadcasted_iota(jnp.int32, (ROUTER_LANES, ROUTER_LANES), 0)
    c_i = lax.broadcasted_iota(jnp.int32, (ROUTER_LANES, ROUTER_LANES), 1)
    fold = jnp.where((r_i >= n_groups) & (r_i < n_groups * (per_group + 1))
                     & (lax.rem(r_i - n_groups, per_group) == c_i), 1.0, 0.0)
    slot_gates = jnp.dot(gates, fold, precision=HIGHEST, preferred_element_type=F32)
    gbuf[slot] = lax.dot_general(jnp.where(sort_t, 1.0, 0.0), slot_gates, TN_DIMS, precision=HIGHEST,
                                 preferred_element_type=F32)
    gstart = [gstart_s[g] for g in range(n_groups)]
    for c in range(n_chunks):
        @pl.when(c * SEG < used)
        def _():
            for cp in _dispatch_copies(xbuf, gbuf, xs_hbm, gs_hbm, sem, slot, c,
                                       _chunk_row(c, padded, starts, gstart)):
                cp.start()
    for g in range(n_groups):
        gstart_s[g] = gstart[g] + padded[g]

    @pl.when(j == n_tiles - 1)
    def _():
        @pl.when(j >= 1)
        def _():
            wait_tile(j - 1, 1 - slot)
        wait_tile(j, slot)


def _moe_group_kernel(tg_ref, xs_ref, gs_ref, w1_ref, w3_ref, w2_ref, ys_ref, *, n_groups, per_group):
    i = pl.program_id(0)

    @pl.when(tg_ref[i] < n_groups)
    def _():
        xb = xs_ref[...]
        acc = jnp.zeros(ys_ref.shape, F32)
        for s in range(per_group):
            h = (_silu(jnp.dot(xb, w1_ref[s], preferred_element_type=F32))
                 * jnp.dot(xb, w3_ref[s], preferred_element_type=F32) * gs_ref[:, s:s + 1])
            acc = acc + jnp.dot(h.astype(BF16), w2_ref[s], preferred_element_type=F32)
        ys_ref[...] = acc.astype(BF16)


def _moe_combine_kernel(cnt_ref, gates_ref, x1_ref, gf_ref, ys_hbm, y_ref, ybuf, sem, gstart_s, *, n_groups):
    j = pl.program_id(0)
    n_tiles = pl.num_programs(0)
    n_chunks = SORT_ROWS // SEG
    slot = lax.rem(j, 2)

    def copies(slot_, c, row):
        return pltpu.make_async_copy(ys_hbm.at[pl.ds(row, SEG), :], ybuf.at[slot_, pl.ds(c * SEG, SEG), :],
                                     sem.at[slot_])

    def fetch_tile(tile, slot_):
        padded, starts, _ = _tile_segments(cnt_ref, tile, n_groups)
        gstart = [gstart_s[g] for g in range(n_groups)]
        for c in range(n_chunks):
            copies(slot_, c, _chunk_row(c, padded, starts, gstart)).start()
        for g in range(n_groups):
            gstart_s[g] = gstart[g] + padded[g]

    @pl.when(j == 0)
    def _():
        bases, _ = _group_bases(cnt_ref, n_tiles, n_groups)
        for g in range(n_groups):
            gstart_s[g] = bases[g]
        fetch_tile(0, 0)

    @pl.when(j + 1 < n_tiles)
    def _():
        fetch_tile(j + 1, 1 - slot)

    for c in range(n_chunks):
        copies(slot, c, 0).wait()
    _, starts, _ = _tile_segments(cnt_ref, j, n_groups)
    sort_t = _sort_matrix(gates_ref[...], starts)
    moe = jnp.dot(jnp.where(sort_t, 1.0, 0.0).astype(BF16), ybuf[slot], preferred_element_type=F32)
    y_ref[...] = _rms(x1_ref[...] + moe, gf_ref[...])


def _moe_sorted(x1, xn2, gates, counts, p):
    t, d = x1.shape
    n_groups, per_group = p["n_groups"], p["per_group"]
    n_tiles = t // TOK_TILE
    d_exp = p["w1_bf"].shape[2]
    cap = t + n_tiles * n_groups * SEG + n_groups * ROW_TILE
    n_rt = -(-cap // ROW_TILE)
    rows = n_rt * ROW_TILE
    cnt = counts[:, :, :n_groups].reshape(-1)
    params = pltpu.CompilerParams(dimension_semantics=("arbitrary",), vmem_limit_bytes=VMEM_LIMIT_BYTES)
    any_spec = pl.BlockSpec(memory_space=pl.ANY)
    tok = lambda w: pl.BlockSpec((TOK_TILE, w), lambda j, c: (j, 0))

    xs, gs, tile_group = pl.pallas_call(
        functools.partial(_moe_dispatch_kernel, n_groups=n_groups, per_group=per_group),
        grid_spec=pltpu.PrefetchScalarGridSpec(
            num_scalar_prefetch=1, grid=(n_tiles,),
            in_specs=[tok(d), tok(ROUTER_LANES), any_spec, any_spec],
            out_specs=(any_spec, any_spec, pl.BlockSpec(memory_space=pltpu.SMEM)),
            scratch_shapes=[pltpu.VMEM((2, SORT_ROWS, d), BF16), pltpu.VMEM((2, SORT_ROWS, ROUTER_LANES), F32),
                            pltpu.SemaphoreType.DMA((2,)), pltpu.SMEM((n_groups,), jnp.int32)]),
        out_shape=(jax.ShapeDtypeStruct((rows, d), BF16), jax.ShapeDtypeStruct((rows, ROUTER_LANES), F32),
                   jax.ShapeDtypeStruct((n_rt + 1,), jnp.int32)),
        input_output_aliases={3: 0, 4: 1},
        compiler_params=params, name="moe_dispatch",
    )(cnt, xn2, gates, jnp.zeros((rows, d), BF16), jnp.zeros((rows, ROUTER_LANES), F32))

    used_tile = lambda i, tg: (jnp.minimum(i, tg[n_rt] - 1), 0)
    group_w = lambda i, tg: (jnp.minimum(tg[i], n_groups - 1), 0, 0)
    ys = pl.pallas_call(
        functools.partial(_moe_group_kernel, n_groups=n_groups, per_group=per_group),
        grid_spec=pltpu.PrefetchScalarGridSpec(
            num_scalar_prefetch=1, grid=(n_rt,),
            in_specs=[pl.BlockSpec((ROW_TILE, d), used_tile), pl.BlockSpec((ROW_TILE, ROUTER_LANES), used_tile),
                      pl.BlockSpec((per_group, d, d_exp), group_w), pl.BlockSpec((per_group, d, d_exp), group_w),
                      pl.BlockSpec((per_group, d_exp, d), group_w)],
            out_specs=pl.BlockSpec((ROW_TILE, d), used_tile)),
        out_shape=jax.ShapeDtypeStruct((rows, d), BF16),
        input_output_aliases={1: 0},
        compiler_params=params, name="moe_experts",
    )(tile_group, xs, gs, p["w1_bf"], p["w3_bf"], p["w2_bf"])

    return pl.pallas_call(
        functools.partial(_moe_combine_kernel, n_groups=n_groups),
        grid_spec=pltpu.PrefetchScalarGridSpec(
            num_scalar_prefetch=1, grid=(n_tiles,),
            in_specs=[tok(ROUTER_LANES), tok(d), pl.BlockSpec(p["gf"].shape, lambda j, c: (0, 0)), any_spec],
            out_specs=tok(d),
            scratch_shapes=[pltpu.VMEM((2, SORT_ROWS, d), BF16), pltpu.SemaphoreType.DMA((2,)),
                            pltpu.SMEM((n_groups,), jnp.int32)]),
        out_shape=jax.ShapeDtypeStruct((t, d), F32),
        compiler_params=params, name="moe_combine",
    )(cnt, gates, x1, p["gf"], ys)


def _sample_in_kernel(x_ref, c0_ref, c1_ref, c2_ref, h0_ref, lbw_ref, g1_ref, win_ref, cw_ref, cb_ref,
                      wx_ref, bx_ref, wa_ref, ba_ref, lam_ref,
                      q_ref, f_ref, k_ref, v_ref, g_ref, yb_ref, hnew_ref, xr_ref):
    wa_w = q_ref.shape[1]
    wb_w = yb_ref.shape[1]
    xn = _rms(x_ref[...], g1_ref[...])
    proj = jnp.dot(xn, win_ref[...], precision=HIGHEST, preferred_element_type=F32)
    lb = _forget_lower_bound(lbw_ref[...])
    f = lb + (1.0 - lb) * _sigmoid(proj[:, wa_w:2 * wa_w])
    q_ref[...] = proj[:, 0:wa_w]
    f_ref[...] = f
    k_ref[...] = 1.0 - f
    v_ref[...] = proj[:, 2 * wa_w:3 * wa_w]
    g_ref[...] = proj[:, 3 * wa_w:4 * wa_w]
    xr = proj[:, 4 * wa_w:4 * wa_w + wb_w]
    xr_ref[...] = xr
    xc = (cb_ref[...] + cw_ref[0:1, :] * c0_ref[...] + cw_ref[1:2, :] * c1_ref[...]
          + cw_ref[2:3, :] * c2_ref[...] + cw_ref[3:4, :] * xr)
    gate_x = _sigmoid(jnp.dot(xc, wx_ref[...], precision=HIGHEST, preferred_element_type=F32) + bx_ref[...])
    gate_a = _sigmoid(jnp.dot(xc, wa_ref[...], precision=HIGHEST, preferred_element_type=F32) + ba_ref[...])
    log_a = (-LRU_C) * gate_a * _softplus(-lam_ref[...])
    a = jnp.exp(log_a)
    mult = jnp.sqrt(-_expm1(2.0 * log_a))
    h = a * h0_ref[...] + gate_x * xc * mult
    hnew_ref[...] = h
    yb_ref[...] = h * _gelu_tanh(proj[:, 4 * wa_w + wb_w:4 * wa_w + 2 * wb_w])


def _sample_state_kernel(s_ref, q_ref, f_ref, k_ref, v_ref, snew_ref, o_ref):
    tb, n_heads = s_ref.shape[0], s_ref.shape[1]
    r_i = lax.broadcasted_iota(jnp.int32, (HEAD_DIM, HEAD_DIM), 0)
    c_i = lax.broadcasted_iota(jnp.int32, (HEAD_DIM, HEAD_DIM), 1)
    eye = r_i == c_i
    for h in range(n_heads):
        hs = slice(h * HEAD_DIM, (h + 1) * HEAD_DIM)
        q8 = q_ref[:, hs]
        rows = []
        for t in range(tb):
            df = jnp.where(eye, f_ref[t:t + 1, hs], 0.0)
            dk = jnp.where(eye, k_ref[t:t + 1, hs], 0.0)
            vb = jnp.broadcast_to(v_ref[t:t + 1, hs], (HEAD_DIM, HEAD_DIM))
            s_new = jnp.dot(jnp.concatenate([df, dk], axis=1), jnp.concatenate([s_ref[t, h], vb], axis=0),
                            precision=HIGHEST, preferred_element_type=F32)
            snew_ref[t, h] = s_new
            rows.append(jnp.dot(q8, s_new, precision=HIGHEST, preferred_element_type=F32)[t:t + 1, :])
        o_ref[:, hs] = jnp.concatenate(rows, axis=0)


def _sample_out_kernel(x_ref, o_ref, g_ref, yb_ref, hgg_ref, wout_ref, g2_ref, wr_ref, br_ref,
                       x1_ref, xn2_ref, gates_ref, *, n_groups, per_group):
    wa_w = o_ref.shape[1]
    ya = []
    for h in range(wa_w // HEAD_DIM):
        hs = slice(h * HEAD_DIM, (h + 1) * HEAD_DIM)
        oh = o_ref[:, hs]
        oh = oh * lax.rsqrt(jnp.mean(oh * oh, axis=-1, keepdims=True) + EPS) * hgg_ref[:, hs]
        ya.append(oh * _silu(g_ref[:, hs]))
    y = jnp.concatenate(ya + [yb_ref[...]], axis=-1)
    x1 = x_ref[...] + jnp.dot(y, wout_ref[...], precision=HIGHEST, preferred_element_type=F32)
    x1_ref[...] = x1
    xn2 = _rms(x1, g2_ref[...])
    xn2_ref[...] = xn2.astype(BF16)
    logits = jnp.dot(xn2, wr_ref[...], precision=HIGHEST, preferred_element_type=F32) + br_ref[...]
    gates_ref[...] = _route(logits, n_groups, per_group)


def _whole(kernel, out_shape, *args, name):
    return pl.pallas_call(
        kernel, out_shape=out_shape,
        compiler_params=pltpu.CompilerParams(vmem_limit_bytes=VMEM_LIMIT_BYTES), name=name)(*args)


def _mixer_sample(x, s0, h0, c0, p, tb):
    n, d = x.shape
    wa_w = p["hgg"].shape[1]
    wb_w = p["cb"].shape[1]
    n_heads = wa_w // HEAD_DIM
    sd = lambda w: jax.ShapeDtypeStruct((n, w), F32)
    q, f, k, v, g, yb, h_new, xr = _whole(
        _sample_in_kernel, (sd(wa_w),) * 5 + (sd(wb_w),) * 3,
        x, c0[:, 0, :], c0[:, 1, :], c0[:, 2, :], h0, p["lbw"], p["g1"], p["win"], p["cw"], p["cb"],
        p["wx"], p["bx"], p["wa"], p["ba"], p["lam"], name="sample_in")
    tok = lambda w: pl.BlockSpec((tb, w), lambda i: (i, 0))
    st = pl.BlockSpec((tb, n_heads, HEAD_DIM, HEAD_DIM), lambda i: (i, 0, 0, 0))
    s_new, o = pl.pallas_call(
        _sample_state_kernel,
        grid=(n // tb,),
        in_specs=[st, tok(wa_w), tok(wa_w), tok(wa_w), tok(wa_w)],
        out_specs=(st, tok(wa_w)),
        out_shape=(jax.ShapeDtypeStruct(s0.shape, F32), sd(wa_w)),
        compiler_params=pltpu.CompilerParams(dimension_semantics=("arbitrary",),
                                             vmem_limit_bytes=VMEM_LIMIT_BYTES),
        name="sample_state",
    )(s0, q, f, k, v)
    x1, xn2, gates = _whole(
        functools.partial(_sample_out_kernel, n_groups=p["n_groups"], per_group=p["per_group"]),
        (sd(d), jax.ShapeDtypeStruct((n, d), BF16), sd(ROUTER_LANES)),
        x, o, g, yb, p["hgg"], p["wout"], p["g2"], p["wr"], p["br"], name="sample_out")
    c_new = jnp.stack([c0[:, 1, :], c0[:, 2, :], xr], axis=1)
    return x1, xn2, gates, s_new, h_new, c_new


def _block_diag(w):
    n, c, _ = w.shape
    eye = jnp.eye(n, dtype=w.dtype)
    return (w[:, :, None, :] * eye[:, None, :, None]).reshape(n * c, n * c)


def _prepare(lower_bounds, ln1_g, w_in, hgrn_norm_g, conv_w, conv_b, lru_wx, lru_bx, lru_wa, lru_ba,
             lru_lambda, w_out, ln2_g, router_wg, router_bg, router_we, router_be, exp_w1, exp_w3,
             exp_w2, final_g):
    d = w_in.shape[1]
    n_groups = router_wg.shape[-1]
    per_group = router_we.shape[-1]
    row = lambda a: a.reshape(1, -1).astype(F32)
    we = jnp.transpose(router_we[0], (1, 0, 2)).reshape(d, n_groups * per_group)
    pad = ROUTER_LANES - n_groups - n_groups * per_group
    wr = jnp.concatenate([router_wg[0], we, jnp.zeros((d, pad), F32)], axis=1)
    br = jnp.concatenate([router_bg[0], router_be[0].reshape(-1), jnp.zeros((pad,), F32)]).reshape(1, -1)
    wx = _block_diag(lru_wx[0])
    wa = _block_diag(lru_wa[0])
    return dict(
        n_groups=n_groups, per_group=per_group,
        lbw=lower_bounds.astype(F32), g1=row(ln1_g[0]), win=w_in[0], win_bf=w_in[0].astype(BF16),
        hgg=row(hgrn_norm_g[0]), cw=conv_w[0], cb=row(conv_b[0]),
        wx=wx, wx_bf=wx.astype(BF16), bx=row(lru_bx[0]), wa=wa, wa_bf=wa.astype(BF16), ba=row(lru_ba[0]),
        lam=row(lru_lambda[0]), wout=w_out[0], wout_bf=w_out[0].astype(BF16), g2=row(ln2_g[0]),
        wr=wr, br=br, w1_bf=exp_w1[0].astype(BF16), w3_bf=exp_w3[0].astype(BF16),
        w2_bf=exp_w2[0].astype(BF16), gf=row(final_g))


def kernel(x_prompt, x_sample, state_hgrn, state_rglru, state_conv, lower_bounds, ln1_g, w_in, hgrn_norm_g, conv_w, conv_b, lru_wx, lru_bx, lru_wa, lru_ba, lru_lambda, w_out, ln2_g, router_wg, router_bg, router_we, router_be, exp_w1, exp_w3, exp_w2, final_g):
    assert w_in.shape[0] == 1, "single-layer trunk"
    p = _prepare(lower_bounds, ln1_g, w_in, hgrn_norm_g, conv_w, conv_b, lru_wx, lru_bx, lru_wa, lru_ba,
                 lru_lambda, w_out, ln2_g, router_wg, router_bg, router_we, router_be, exp_w1, exp_w3,
                 exp_w2, final_g)
    bsz, seq, d = x_prompt.shape
    lb_t = min(512, seq)
    x1, xn2, gates, counts, s_p, h_p, c_p = _mixer_prompt(x_prompt, p, lb_t)
    t = bsz * seq
    y_p = _moe_sorted(x1.reshape(t, d), xn2.reshape(t, d), gates.reshape(t, ROUTER_LANES), counts, p)

    n = x_sample.shape[0]
    x1s, xn2s, gates_s, s_s, h_s, c_s = _mixer_sample(x_sample[:, 0, :], state_hgrn[0], state_rglru[0],
                                                      state_conv[0], p, 8)
    y_s = _moe_dense(x1s, xn2s, gates_s, p, n)
    return (y_p.reshape(bsz, seq, d), y_s.reshape(n, 1, d),
            s_p[None], h_p.reshape(1, bsz, -1), c_p[None],
            s_s[None], h_s[None], c_s[None])
```

```python
import functools

import jax
import jax.numpy as jnp
from jax import lax
from jax.experimental import pallas as pl
from jax.experimental.pallas import tpu as pltpu

F32 = jnp.float32
BF16 = jnp.bfloat16
HIGHEST = lax.Precision.HIGHEST

EPS = 1e-6
LRU_C = 8.0
LOG2E = 1.4426950408889634
HEAD_DIM = 128
CHUNK = 64
SUB = 16
ROUTER_LANES = 128
TOK_TILE = 256
SEG = 16
SORT_ROWS = 384
ROW_TILE = 512
VMEM_LIMIT_BYTES = 56 * 1024 * 1024

NT_DIMS = (((1,), (1,)), ((), ()))
TN_DIMS = (((0,), (0,)), ((), ()))


def _rms(x, g):
    return x * lax.rsqrt(jnp.mean(x * x, axis=-1, keepdims=True) + EPS) * g


def _sigmoid(x):
    return 1.0 / (1.0 + jnp.exp(-x))


def _silu(x):
    return x * _sigmoid(x)


def _gelu_tanh(x):
    c = 0.7978845608028654
    return x * (0.5 * (1.0 + jnp.tanh(c * (x + 0.044715 * (x * x * x)))))


def _softplus(z):
    return jnp.maximum(z, 0.0) + jnp.log1p(jnp.exp(-jnp.abs(z)))


def _expm1(x):
    u = jnp.exp(x)
    um1 = u - 1.0
    small = um1 * x / jnp.log(u)
    return jnp.where(um1 == 0.0, x, jnp.where(jnp.abs(x) < 0.5, small, um1))


def _forget_lower_bound(lbw):
    m = jnp.max(lbw, axis=0, keepdims=True)
    e = jnp.exp(lbw - m)
    return e[0:1, :] / jnp.sum(e, axis=0, keepdims=True)


def _route(logits, n_groups, per_group):
    n = logits.shape[-1]
    col = lax.broadcasted_iota(jnp.int32, logits.shape, 1)
    neg = jnp.float32(-jnp.inf)
    big = jnp.int32(n)
    is_g = col < n_groups
    lg = jnp.where(is_g, logits, neg)
    mg = jnp.max(lg, axis=-1, keepdims=True)
    g_idx = jnp.min(jnp.where(lg == mg, col, big), axis=-1, keepdims=True)
    p_top = 1.0 / jnp.sum(jnp.where(is_g, jnp.exp(logits - mg), 0.0), axis=-1, keepdims=True)
    lo = n_groups + per_group * g_idx
    le = jnp.where((col >= lo) & (col < lo + per_group), logits, neg)
    m1 = jnp.max(le, axis=-1, keepdims=True)
    i1 = jnp.min(jnp.where(le == m1, col, big), axis=-1, keepdims=True)
    le2 = jnp.where(col == i1, neg, le)
    m2 = jnp.max(le2, axis=-1, keepdims=True)
    i2 = jnp.min(jnp.where(le2 == m2, col, big), axis=-1, keepdims=True)
    e2 = jnp.exp(m2 - m1)
    den = 1.0 + e2
    w1 = p_top / den
    w2 = p_top * (e2 / den)
    gates = jnp.where(col == i1, w1, 0.0) + jnp.where(col == i2, w2, 0.0)
    return gates + jnp.where(col == 0, g_idx.astype(F32), 0.0)


def _mixer_prompt_kernel(x_ref, lbw_ref, g1_ref, win_ref, hgg_ref, cw_ref, cb_ref, wx_ref, bx_ref,
                         wa_ref, ba_ref, lam_ref, wout_ref, g2_ref, wr_ref, br_ref,
                         x1_ref, xn2_ref, gates_ref, cnt_ref, sout_ref, hout_ref, cout_ref,
                         proj_s, k_s, o_s, st_s, xr_s, a_s, u_s, hcar_s,
                         *, n_groups, per_group):
    lb_t = x_ref.shape[1]
    wa_w = k_s.shape[1]
    wb_w = a_s.shape[1]
    n_heads = wa_w // HEAD_DIM
    j = pl.program_id(1)
    nj = pl.num_programs(1)

    @pl.when(j == 0)
    def _():
        st_s[...] = jnp.zeros_like(st_s)
        hcar_s[...] = jnp.zeros_like(hcar_s)
        xr_s[0:8, :] = jnp.zeros((8, wb_w), F32)

    x = x_ref[0]
    xn = _rms(x, g1_ref[...]).astype(BF16)
    n_cols = win_ref.shape[1]
    for c in range(0, n_cols, 512):
        proj_s[:, c:c + 512] = jnp.dot(xn, win_ref[:, c:c + 512], preferred_element_type=F32)

    lb = _forget_lower_bound(lbw_ref[...])
    f = lb + (1.0 - lb) * _sigmoid(proj_s[:, wa_w:2 * wa_w])
    k_s[...] = 1.0 - f
    logf = jnp.log(f)
    r_i = lax.broadcasted_iota(jnp.int32, (CHUNK, CHUNK), 0)
    c_i = lax.broadcasted_iota(jnp.int32, (CHUNK, CHUNK), 1)
    tri = (r_i >= c_i).astype(F32)
    for c in range(0, lb_t, CHUNK):
        proj_s[c:c + CHUNK, wa_w:2 * wa_w] = LOG2E * jnp.dot(tri, logf[c:c + CHUNK, :], precision=HIGHEST,
                                                              preferred_element_type=F32)

    row_sub = lax.broadcasted_iota(jnp.int32, (SUB, HEAD_DIM), 0)
    lane_sub = lax.broadcasted_iota(jnp.int32, (SUB, HEAD_DIM), 1)
    ones_kk = jnp.ones((HEAD_DIM, HEAD_DIM), BF16)

    n_sub = CHUNK // SUB

    def chunk_body(ci, carry):
        r0 = pl.multiple_of(ci * CHUNK, CHUNK)
        first = []
        for h in range(n_heads):
            hs = slice(h * HEAD_DIM, (h + 1) * HEAD_DIM)
            q = proj_s[pl.ds(r0, CHUNK), hs]
            b = proj_s[pl.ds(r0, CHUNK), wa_w + h * HEAD_DIM:wa_w + (h + 1) * HEAD_DIM]
            v = proj_s[pl.ds(r0, CHUNK), 2 * wa_w + h * HEAD_DIM:2 * wa_w + (h + 1) * HEAD_DIM]
            k = k_s[pl.ds(r0, CHUNK), hs]
            vb = v.astype(BF16)
            st = st_s[h]
            b_last = b[CHUNK - 1:CHUNK, :]
            o = lax.dot_general((q * jnp.exp2(b)).astype(BF16), st.astype(BF16), NT_DIMS,
                                preferred_element_type=F32)
            k_end = k * jnp.exp2(b_last - b)
            st_s[h] = st * jnp.exp2(b_last) + lax.dot_general(vb, k_end.astype(BF16), TN_DIMS,
                                                               preferred_element_type=F32)
            terms, off = [], []
            for i in range(n_sub):
                lo = i * SUB
                qi, bi, ki = q[lo:lo + SUB], b[lo:lo + SUB], k[lo:lo + SUB]
                terms += [(qi * (ki[s:s + 1] * jnp.exp2(bi - bi[s:s + 1]))).astype(BF16) for s in range(SUB)]
                if i > 0:
                    r = b[lo - 1:lo]
                    qt = (qi * jnp.exp2(bi - r)).astype(BF16)
                    kt = (k[:lo] * jnp.exp2(r - b[:lo])).astype(BF16)
                    off.append(lax.dot_general(qt, kt, NT_DIMS, preferred_element_type=F32))
            sums = jnp.dot(jnp.concatenate(terms, axis=0), ones_kk, preferred_element_type=F32)
            first.append((o, vb, off, sums))
        for h in range(n_heads):
            o, vb, off, sums = first[h]
            outs = []
            for i in range(n_sub):
                lo = i * SUB
                sc = jnp.zeros((SUB, HEAD_DIM), F32)
                for s in range(SUB):
                    sc = jnp.where(lane_sub == s, sums[(i * SUB + s) * SUB:(i * SUB + s + 1) * SUB], sc)
                sc = jnp.where(row_sub >= lane_sub, sc, 0.0)[:, :SUB]
                od = o[lo:lo + SUB] + jnp.dot(sc.astype(BF16), vb[lo:lo + SUB], preferred_element_type=F32)
                if i > 0:
                    od = od + jnp.dot(off[i - 1].astype(BF16), vb[:lo], preferred_element_type=F32)
                outs.append(od)
            o_s[pl.ds(r0, CHUNK), h * HEAD_DIM:(h + 1) * HEAD_DIM] = jnp.concatenate(outs, axis=0)
        return carry

    lax.fori_loop(0, lb_t // CHUNK, chunk_body, 0)

    ya = []
    for h in range(n_heads):
        hs = slice(h * HEAD_DIM, (h + 1) * HEAD_DIM)
        oh = o_s[:, hs]
        oh = oh * lax.rsqrt(jnp.mean(oh * oh, axis=-1, keepdims=True) + EPS) * hgg_ref[:, hs]
        ya.append(oh * _silu(proj_s[:, 3 * wa_w + h * HEAD_DIM:3 * wa_w + (h + 1) * HEAD_DIM]))

    xb0 = 4 * wa_w
    xr_s[pl.ds(8, lb_t), :] = proj_s[:, xb0:xb0 + wb_w]
    xc = (cb_ref[...] + cw_ref[3:4, :] * xr_s[pl.ds(8, lb_t), :] + cw_ref[2:3, :] * xr_s[pl.ds(7, lb_t), :]
          + cw_ref[1:2, :] * xr_s[pl.ds(6, lb_t), :] + cw_ref[0:1, :] * xr_s[pl.ds(5, lb_t), :])
    tail = xr_s[pl.ds(lb_t + 5, 3), :]
    xr_s[5:8, :] = tail
    cout_ref[0] = tail
    xcb = xc.astype(BF16)
    gate_x = _sigmoid(jnp.dot(xcb, wx_ref[...], preferred_element_type=F32) + bx_ref[...])
    gate_a = _sigmoid(jnp.dot(xcb, wa_ref[...], preferred_element_type=F32) + ba_ref[...])
    log_a = (-LRU_C) * gate_a * _softplus(-lam_ref[...])
    a = jnp.exp(log_a)
    mult = jnp.sqrt(-_expm1(2.0 * log_a))
    first = (lax.broadcasted_iota(jnp.int32, (lb_t, 1), 0) == 0) & (j == 0)
    a = jnp.where(first, 0.0, a)
    mult = jnp.where(first, 1.0, mult)
    a_s[...] = a
    u_s[...] = gate_x * xc * mult
    row8 = lax.broadcasted_iota(jnp.int32, (8, 1), 0)

    def scan_body(gi, carry):
        r0 = pl.multiple_of(gi * 8, 8)
        aa = a_s[pl.ds(r0, 8), :]
        uu = u_s[pl.ds(r0, 8), :]
        for s in (1, 2, 4):
            m = row8 >= s
            uu = jnp.where(m, aa * pltpu.roll(uu, s, 0) + uu, uu)
            aa = jnp.where(m, aa * pltpu.roll(aa, s, 0), aa)
        hh = aa * carry + uu
        u_s[pl.ds(r0, 8), :] = hh
        return hh[7:8, :]

    h_last = lax.fori_loop(0, lb_t // 8, scan_body, hcar_s[...])
    hcar_s[...] = h_last
    hout_ref[0] = h_last
    yb = u_s[...] * _gelu_tanh(proj_s[:, xb0 + wb_w:xb0 + 2 * wb_w])

    y = jnp.concatenate(ya + [yb], axis=-1).astype(BF16)
    x1 = x + jnp.dot(y, wout_ref[...], preferred_element_type=F32)
    x1_ref[0] = x1
    xn2 = _rms(x1, g2_ref[...]).astype(BF16)
    xn2_ref[0] = xn2
    logits = jnp.dot(xn2, wr_ref[...], preferred_element_type=F32) + br_ref[...]
    gates = _route(logits, n_groups, per_group)
    gates_ref[0] = gates
    col_t = lax.broadcasted_iota(jnp.int32, (TOK_TILE, ROUTER_LANES), 1)
    for t in range(lb_t // TOK_TILE):
        gi = gates[t * TOK_TILE:(t + 1) * TOK_TILE, 0:1].astype(jnp.int32)
        cnt_ref[0, t:t + 1, :] = jnp.sum(jnp.where(col_t == gi, 1, 0), axis=0, keepdims=True)

    @pl.when(j == nj - 1)
    def _():
        for h in range(n_heads):
            sout_ref[0, h] = st_s[h].T


def _const_spec(shape):
    nd = len(shape)
    return pl.BlockSpec(shape, lambda *_: (0,) * nd)


def _mixer_prompt(x, p, lb_t):
    bsz, seq, d = x.shape
    wa_w = p["hgg"].shape[1]
    wb_w = p["cb"].shape[1]
    n_heads = wa_w // HEAD_DIM
    n_cols = p["win_bf"].shape[1]
    weights = [p["lbw"], p["g1"], p["win_bf"], p["hgg"], p["cw"], p["cb"], p["wx_bf"], p["bx"],
               p["wa_bf"], p["ba"], p["lam"], p["wout_bf"], p["g2"], p["wr"].astype(BF16), p["br"]]
    tile = lambda w: pl.BlockSpec((1, lb_t, w), lambda b, j: (b, j, 0))
    out_shape = (
        jax.ShapeDtypeStruct((bsz, seq, d), F32),
        jax.ShapeDtypeStruct((bsz, seq, d), BF16),
        jax.ShapeDtypeStruct((bsz, seq, ROUTER_LANES), F32),
        jax.ShapeDtypeStruct((bsz * (seq // lb_t), lb_t // TOK_TILE, ROUTER_LANES), jnp.int32),
        jax.ShapeDtypeStruct((bsz, n_heads, HEAD_DIM, HEAD_DIM), F32),
        jax.ShapeDtypeStruct((bsz, 1, wb_w), F32),
        jax.ShapeDtypeStruct((bsz, 3, wb_w), F32),
    )
    out_specs = (
        tile(d), tile(d), tile(ROUTER_LANES),
        pl.BlockSpec((1, lb_t // TOK_TILE, ROUTER_LANES), lambda b, j: (b * (seq // lb_t) + j, 0, 0)),
        pl.BlockSpec((1, n_heads, HEAD_DIM, HEAD_DIM), lambda b, j: (b, 0, 0, 0)),
        pl.BlockSpec((1, 1, wb_w), lambda b, j: (b, 0, 0)),
        pl.BlockSpec((1, 3, wb_w), lambda b, j: (b, 0, 0)),
    )
    scratch = [
        pltpu.VMEM((lb_t, n_cols), F32),
        pltpu.VMEM((lb_t, wa_w), F32),
        pltpu.VMEM((lb_t, wa_w), F32),
        pltpu.VMEM((n_heads, HEAD_DIM, HEAD_DIM), F32),
        pltpu.VMEM((lb_t + 8, wb_w), F32),
        pltpu.VMEM((lb_t, wb_w), F32),
        pltpu.VMEM((lb_t, wb_w), F32),
        pltpu.VMEM((1, wb_w), F32),
    ]
    kern = functools.partial(_mixer_prompt_kernel, n_groups=p["n_groups"], per_group=p["per_group"])
    return pl.pallas_call(
        kern,
        grid=(bsz, seq // lb_t),
        in_specs=[tile(d)] + [_const_spec(w.shape) for w in weights],
        out_specs=out_specs,
        out_shape=out_shape,
        scratch_shapes=scratch,
        compiler_params=pltpu.CompilerParams(dimension_semantics=("arbitrary", "arbitrary"),
                                             vmem_limit_bytes=VMEM_LIMIT_BYTES),
        name="mixer_prompt",
    )(x, *weights)


def _moe_kernel(x1_ref, xn2_ref, gates_ref, w1_ref, w3_ref, w2_ref, gf_ref, y_ref, acc_s, *, n_groups):
    e = pl.program_id(1)

    @pl.when(e == 0)
    def _():
        acc_s[...] = jnp.zeros_like(acc_s)

    xb = xn2_ref[...]
    col = lax.broadcasted_iota(jnp.int32, gates_ref.shape, 1)
    gate = jnp.sum(jnp.where(col == e + n_groups, gates_ref[...], 0.0), axis=-1, keepdims=True)
    h = _silu(jnp.dot(xb, w1_ref[0], preferred_element_type=F32)) * jnp.dot(xb, w3_ref[0], preferred_element_type=F32)
    acc_s[...] += jnp.dot((h * gate).astype(BF16), w2_ref[0], preferred_element_type=F32)

    @pl.when(e == pl.num_programs(1) - 1)
    def _():
        y_ref[...] = _rms(x1_ref[...] + acc_s[...], gf_ref[...])


def _moe_dense(x1, xn2, gates, p, tm):
    t, d = x1.shape
    n_exp, _, d_exp = p["w1_bf"].shape
    row = lambda w: pl.BlockSpec((tm, w), lambda i, e: (i, 0))
    return pl.pallas_call(
        functools.partial(_moe_kernel, n_groups=p["n_groups"]),
        grid=(t // tm, n_exp),
        in_specs=[row(d), row(d), row(ROUTER_LANES),
                  pl.BlockSpec((1, d, d_exp), lambda i, e: (e, 0, 0)),
                  pl.BlockSpec((1, d, d_exp), lambda i, e: (e, 0, 0)),
                  pl.BlockSpec((1, d_exp, d), lambda i, e: (e, 0, 0)),
                  _const_spec(p["gf"].shape)],
        out_specs=row(d),
        out_shape=jax.ShapeDtypeStruct((t, d), F32),
        scratch_shapes=[pltpu.VMEM((tm, d), F32)],
        compiler_params=pltpu.CompilerParams(dimension_semantics=("arbitrary", "arbitrary"),
                                             vmem_limit_bytes=VMEM_LIMIT_BYTES),
        name="moe_dense",
    )(x1, xn2, gates, p["w1_bf"], p["w3_bf"], p["w2_bf"], p["gf"])


def _seg_pad(n):
    return jnp.bitwise_and(n + (SEG - 1), -SEG)


def _row_tile_pad(n):
    return jnp.bitwise_and(n + (ROW_TILE - 1), -ROW_TILE)


def _tile_segments(cnt_ref, tile, n_groups):
    padded = [_seg_pad(cnt_ref[tile * n_groups + g]) for g in range(n_groups)]
    starts, acc = [], jnp.int32(0)
    for g in range(n_groups):
        starts.append(acc)
        acc = acc + padded[g]
    return padded, starts, acc


def _group_bases(cnt_ref, n_tiles, n_groups):
    def body(t, tot):
        return tuple(tot[g] + _seg_pad(cnt_ref[t * n_groups + g]) for g in range(n_groups))
    totals = lax.fori_loop(0, n_tiles, body, (jnp.int32(0),) * n_groups)
    bases, ends, acc = [], [], jnp.int32(0)
    for g in range(n_groups):
        bases.append(acc)
        acc = acc + _row_tile_pad(totals[g])
        ends.append(acc)
    return bases, ends


def _chunk_row(c, padded, starts, gstart):
    lo = c * SEG
    row = jnp.int32(0)
    for g in range(len(padded)):
        inside = (lo >= starts[g]) & (lo < starts[g] + padded[g])
        row = row + jnp.where(inside, gstart[g] + lo - starts[g], 0)
    return pl.multiple_of(row, SEG)


def _sort_matrix(gates, starts):
    n_groups = len(starts)
    col = lax.broadcasted_iota(jnp.int32, (TOK_TILE, ROUTER_LANES), 1)
    gi = gates[:, 0:1].astype(jnp.int32)
    onehot = col == gi
    r_i = lax.broadcasted_iota(jnp.int32, (TOK_TILE, TOK_TILE), 0)
    c_i = lax.broadcasted_iota(jnp.int32, (TOK_TILE, TOK_TILE), 1)
    earlier = jnp.where(r_i > c_i, 1.0, 0.0).astype(BF16)
    before = jnp.dot(earlier, jnp.where(onehot, 1.0, 0.0).astype(BF16), preferred_element_type=F32)
    rank = jnp.sum(jnp.where(onehot, before, 0.0), axis=-1, keepdims=True).astype(jnp.int32)
    base = jnp.zeros((TOK_TILE, 1), jnp.int32)
    for g in range(n_groups):
        base = base + jnp.where(gi == g, starts[g], 0)
    lane = lax.broadcasted_iota(jnp.int32, (TOK_TILE, SORT_ROWS), 1)
    return lane == base + rank


def _sort_matrix_rows(gates, starts):
    n_groups = len(starts)
    col = lax.broadcasted_iota(jnp.int32, (TOK_TILE, ROUTER_LANES), 1)
    onehot_t = jnp.where(col == gates[:, 0:1].astype(jnp.int32), 1.0, 0.0).T
    r_i = lax.broadcasted_iota(jnp.int32, (TOK_TILE, TOK_TILE), 0)
    c_i = lax.broadcasted_iota(jnp.int32, (TOK_TILE, TOK_TILE), 1)
    later = jnp.where(r_i < c_i, 1.0, 0.0).astype(BF16)
    before = jnp.dot(onehot_t.astype(BF16), later, preferred_element_type=F32)
    g_row = lax.broadcasted_iota(jnp.int32, (ROUTER_LANES, 1), 0)
    base = jnp.zeros((ROUTER_LANES, 1), jnp.int32)
    for g in range(n_groups):
        base = base + jnp.where(g_row == g, starts[g], 0)
    dest = jnp.sum(jnp.where(onehot_t > 0.0, before + base.astype(F32), 0.0), axis=0, keepdims=True)
    row = lax.broadcasted_iota(jnp.int32, (SORT_ROWS, TOK_TILE), 0)
    return row == dest.astype(jnp.int32)


def _dispatch_copies(xbuf, gbuf, xs_hbm, gs_hbm, sem, slot, c, row):
    return (pltpu.make_async_copy(xbuf.at[slot, pl.ds(c * SEG, SEG), :], xs_hbm.at[pl.ds(row, SEG), :], sem.at[slot]),
            pltpu.make_async_copy(gbuf.at[slot, pl.ds(c * SEG, SEG), :], gs_hbm.at[pl.ds(row, SEG), :], sem.at[slot]))


def _moe_dispatch_kernel(cnt_ref, xn2_ref, gates_ref, xs_in, gs_in, xs_hbm, gs_hbm, tg_ref,
                         xbuf, gbuf, sem, gstart_s, *, n_groups, per_group):
    del xs_in, gs_in
    j = pl.program_id(0)
    n_tiles = pl.num_programs(0)
    n_chunks = (TOK_TILE + n_groups * SEG) // SEG
    slot = lax.rem(j, 2)

    @pl.when(j == 0)
    def _():
        bases, ends = _group_bases(cnt_ref, n_tiles, n_groups)
        for g in range(n_groups):
            gstart_s[g] = bases[g]
        n_rt = tg_ref.shape[0] - 1
        for i in range(n_rt):
            tg = jnp.int32(0)
            for g in range(n_groups):
                tg = tg + jnp.where(ends[g] <= i * ROW_TILE, 1, 0)
            tg_ref[i] = tg
        tg_ref[n_rt] = ends[n_groups - 1] // ROW_TILE

    def wait_tile(tile, slot_):
        _, _, used = _tile_segments(cnt_ref, tile, n_groups)
        for c in range(n_chunks):
            @pl.when(c * SEG < used)
            def _():
                for cp in _dispatch_copies(xbuf, gbuf, xs_hbm, gs_hbm, sem, slot_, c, 0):
                    cp.wait()

    @pl.when(j >= 2)
    def _():
        wait_tile(j - 2, slot)

    padded, starts, used = _tile_segments(cnt_ref, j, n_groups)
    gates = gates_ref[...]
    d = xn2_ref.shape[1]
    g_hi = gates.astype(BF16)
    g_lo = (gates - g_hi.astype(F32)).astype(BF16)
    sort_m = jnp.where(_sort_matrix_rows(gates, starts), 1.0, 0.0).astype(BF16)
    moved = jnp.dot(sort_m, jnp.concatenate([xn2_ref[...], g_hi, g_lo], axis=1), preferred_element_type=F32)
    xbuf[slot] = moved[:, :d].astype(BF16)
    gbuf[slot] = moved[:, d:d + ROUTER_LANES] + moved[:, d + ROUTER_LANES:]
    gstart = [gstart_s[g] for g in range(n_groups)]
    for c in range(n_chunks):
        @pl.when(c * SEG < used)
        def _():
            for cp in _dispatch_copies(xbuf, gbuf, xs_hbm, gs_hbm, sem, slot, c,
                                       _chunk_row(c, padded, starts, gstart)):
                cp.start()
    for g in range(n_groups):
        gstart_s[g] = gstart[g] + padded[g]

    @pl.when(j == n_tiles - 1)
    def _():
        @pl.when(j >= 1)
        def _():
            wait_tile(j - 1, 1 - slot)
        wait_tile(j, slot)


def _moe_group_kernel(tg_ref, xs_ref, gs_ref, w1_ref, w3_ref, w2_ref, ys_ref, *, n_groups, per_group):
    i = pl.program_id(0)

    @pl.when(tg_ref[i] < n_groups)
    def _():
        xb = xs_ref[...]
        gates = pltpu.roll(gs_ref[...], ROUTER_LANES - n_groups - per_group * tg_ref[i], 1)
        acc = jnp.zeros(ys_ref.shape, F32)
        for s in range(per_group):
            h = (_silu(jnp.dot(xb, w1_ref[s], preferred_element_type=F32))
                 * jnp.dot(xb, w3_ref[s], preferred_element_type=F32) * gates[:, s:s + 1])
            acc = acc + jnp.dot(h.astype(BF16), w2_ref[s], preferred_element_type=F32)
        ys_ref[...] = acc.astype(BF16)


def _moe_combine_kernel(cnt_ref, gates_ref, x1_ref, gf_ref, ys_hbm, y_ref, ybuf, sem, gstart_s, *, n_groups):
    j = pl.program_id(0)
    n_tiles = pl.num_programs(0)
    n_chunks = SORT_ROWS // SEG
    slot = lax.rem(j, 2)

    def copies(slot_, c, row):
        return pltpu.make_async_copy(ys_hbm.at[pl.ds(row, SEG), :], ybuf.at[slot_, pl.ds(c * SEG, SEG), :],
                                     sem.at[slot_])

    def fetch_tile(tile, slot_):
        padded, starts, _ = _tile_segments(cnt_ref, tile, n_groups)
        gstart = [gstart_s[g] for g in range(n_groups)]
        for c in range(n_chunks):
            copies(slot_, c, _chunk_row(c, padded, starts, gstart)).start()
        for g in range(n_groups):
            gstart_s[g] = gstart[g] + padded[g]

    @pl.when(j == 0)
    def _():
        bases, _ = _group_bases(cnt_ref, n_tiles, n_groups)
        for g in range(n_groups):
            gstart_s[g] = bases[g]
        fetch_tile(0, 0)

    @pl.when(j + 1 < n_tiles)
    def _():
        fetch_tile(j + 1, 1 - slot)

    for c in range(n_chunks):
        copies(slot, c, 0).wait()
    _, starts, _ = _tile_segments(cnt_ref, j, n_groups)
    sort_t = _sort_matrix(gates_ref[...], starts)
    moe = jnp.dot(jnp.where(sort_t, 1.0, 0.0).astype(BF16), ybuf[slot], preferred_element_type=F32)
    y_ref[...] = _rms(x1_ref[...] + moe, gf_ref[...])


def _moe_sorted(x1, xn2, gates, counts, p):
    t, d = x1.shape
    n_groups, per_group = p["n_groups"], p["per_group"]
    n_tiles = t // TOK_TILE
    d_exp = p["w1_bf"].shape[2]
    cap = t + n_tiles * n_groups * SEG + n_groups * ROW_TILE
    n_rt = -(-cap // ROW_TILE)
    rows = n_rt * ROW_TILE
    cnt = counts[:, :, :n_groups].reshape(-1)
    params = pltpu.CompilerParams(dimension_semantics=("arbitrary",), vmem_limit_bytes=VMEM_LIMIT_BYTES)
    any_spec = pl.BlockSpec(memory_space=pl.ANY)
    tok = lambda w: pl.BlockSpec((TOK_TILE, w), lambda j, c: (j, 0))

    xs, gs, tile_group = pl.pallas_call(
        functools.partial(_moe_dispatch_kernel, n_groups=n_groups, per_group=per_group),
        grid_spec=pltpu.PrefetchScalarGridSpec(
            num_scalar_prefetch=1, grid=(n_tiles,),
            in_specs=[tok(d), tok(ROUTER_LANES), any_spec, any_spec],
            out_specs=(any_spec, any_spec, pl.BlockSpec(memory_space=pltpu.SMEM)),
            scratch_shapes=[pltpu.VMEM((2, SORT_ROWS, d), BF16), pltpu.VMEM((2, SORT_ROWS, ROUTER_LANES), F32),
                            pltpu.SemaphoreType.DMA((2,)), pltpu.SMEM((n_groups,), jnp.int32)]),
        out_shape=(jax.ShapeDtypeStruct((rows, d), BF16), jax.ShapeDtypeStruct((rows, ROUTER_LANES), F32),
                   jax.ShapeDtypeStruct((n_rt + 1,), jnp.int32)),
        input_output_aliases={3: 0, 4: 1},
        compiler_params=params, name="moe_dispatch",
    )(cnt, xn2, gates, jnp.zeros((rows, d), BF16), jnp.zeros((rows, ROUTER_LANES), F32))

    used_tile = lambda i, tg: (jnp.minimum(i, tg[n_rt] - 1), 0)
    group_w = lambda i, tg: (jnp.minimum(tg[i], n_groups - 1), 0, 0)
    ys = pl.pallas_call(
        functools.partial(_moe_group_kernel, n_groups=n_groups, per_group=per_group),
        grid_spec=pltpu.PrefetchScalarGridSpec(
            num_scalar_prefetch=1, grid=(n_rt,),
            in_specs=[pl.BlockSpec((ROW_TILE, d), used_tile), pl.BlockSpec((ROW_TILE, ROUTER_LANES), used_tile),
                      pl.BlockSpec((per_group, d, d_exp), group_w), pl.BlockSpec((per_group, d, d_exp), group_w),
                      pl.BlockSpec((per_group, d_exp, d), group_w)],
            out_specs=pl.BlockSpec((ROW_TILE, d), used_tile)),
        out_shape=jax.ShapeDtypeStruct((rows, d), BF16),
        input_output_aliases={1: 0},
        compiler_params=params, name="moe_experts",
    )(tile_group, xs, gs, p["w1_bf"], p["w3_bf"], p["w2_bf"])

    return pl.pallas_call(
        functools.partial(_moe_combine_kernel, n_groups=n_groups),
        grid_spec=pltpu.PrefetchScalarGridSpec(
            num_scalar_prefetch=1, grid=(n_tiles,),
            in_specs=[tok(ROUTER_LANES), tok(d), pl.BlockSpec(p["gf"].shape, lambda j, c: (0, 0)), any_spec],
            out_specs=tok(d),
            scratch_shapes=[pltpu.VMEM((2, SORT_ROWS, d), BF16), pltpu.SemaphoreType.DMA((2,)),
                            pltpu.SMEM((n_groups,), jnp.int32)]),
        out_shape=jax.ShapeDtypeStruct((t, d), F32),
        compiler_params=params, name="moe_combine",
    )(cnt, gates, x1, p["gf"], ys)


def _sample_in_kernel(x_ref, c0_ref, c1_ref, c2_ref, h0_ref, lbw_ref, g1_ref, win_ref, cw_ref, cb_ref,
                      wx_ref, bx_ref, wa_ref, ba_ref, lam_ref,
                      q_ref, f_ref, k_ref, v_ref, g_ref, yb_ref, hnew_ref, xr_ref):
    wa_w = v_ref.shape[1]
    wb_w = yb_ref.shape[1]
    xn = _rms(x_ref[...], g1_ref[...])
    proj = jnp.dot(xn, win_ref[...], precision=HIGHEST, preferred_element_type=F32)
    lb = _forget_lower_bound(lbw_ref[...])
    f = lb + (1.0 - lb) * _sigmoid(proj[:, wa_w:2 * wa_w])
    q_ref[...] = proj[:, 0:wa_w].T
    f_ref[...] = f.T
    k_ref[...] = (1.0 - f).T
    v_ref[...] = proj[:, 2 * wa_w:3 * wa_w]
    g_ref[...] = proj[:, 3 * wa_w:4 * wa_w]
    xr = proj[:, 4 * wa_w:4 * wa_w + wb_w]
    xr_ref[...] = xr
    xc = (cb_ref[...] + cw_ref[0:1, :] * c0_ref[...] + cw_ref[1:2, :] * c1_ref[...]
          + cw_ref[2:3, :] * c2_ref[...] + cw_ref[3:4, :] * xr)
    gate_x = _sigmoid(jnp.dot(xc, wx_ref[...], precision=HIGHEST, preferred_element_type=F32) + bx_ref[...])
    gate_a = _sigmoid(jnp.dot(xc, wa_ref[...], precision=HIGHEST, preferred_element_type=F32) + ba_ref[...])
    log_a = (-LRU_C) * gate_a * _softplus(-lam_ref[...])
    a = jnp.exp(log_a)
    mult = jnp.sqrt(-_expm1(2.0 * log_a))
    h = a * h0_ref[...] + gate_x * xc * mult
    hnew_ref[...] = h
    yb_ref[...] = h * _gelu_tanh(proj[:, 4 * wa_w + wb_w:4 * wa_w + 2 * wb_w])


def _sample_state_kernel(s_ref, qt_ref, ft_ref, kt_ref, v_ref, snew_ref, o_ref):
    tb, n_heads = s_ref.shape[0], s_ref.shape[1]
    n_tok = qt_ref.shape[1]
    shift = lax.rem(n_tok - pl.program_id(0) * tb, n_tok)
    sq = (HEAD_DIM, HEAD_DIM)
    for h in range(n_heads):
        hs = slice(h * HEAD_DIM, (h + 1) * HEAD_DIM)
        qh = pltpu.roll(qt_ref[hs, :], shift, 1)
        fh = pltpu.roll(ft_ref[hs, :], shift, 1)
        kh = pltpu.roll(kt_ref[hs, :], shift, 1)
        rows = []
        for t in range(tb):
            s_new = (jnp.broadcast_to(fh[:, t:t + 1], sq) * s_ref[t, h]
                     + jnp.broadcast_to(kh[:, t:t + 1], sq) * v_ref[t:t + 1, hs])
            snew_ref[t, h] = s_new
            rows.append(jnp.sum(jnp.broadcast_to(qh[:, t:t + 1], sq) * s_new, axis=0, keepdims=True))
        o_ref[:, hs] = jnp.concatenate(rows, axis=0)


def _sample_out_kernel(x_ref, o_ref, g_ref, yb_ref, hgg_ref, wout_ref, g2_ref, wr_ref, br_ref,
                       x1_ref, xn2_ref, gates_ref, *, n_groups, per_group):
    wa_w = o_ref.shape[1]
    ya = []
    for h in range(wa_w // HEAD_DIM):
        hs = slice(h * HEAD_DIM, (h + 1) * HEAD_DIM)
        oh = o_ref[:, hs]
        oh = oh * lax.rsqrt(jnp.mean(oh * oh, axis=-1, keepdims=True) + EPS) * hgg_ref[:, hs]
        ya.append(oh * _silu(g_ref[:, hs]))
    y = jnp.concatenate(ya + [yb_ref[...]], axis=-1)
    x1 = x_ref[...] + jnp.dot(y, wout_ref[...], precision=HIGHEST, preferred_element_type=F32)
    x1_ref[...] = x1
    xn2 = _rms(x1, g2_ref[...])
    xn2_ref[...] = xn2.astype(BF16)
    logits = jnp.dot(xn2, wr_ref[...], precision=HIGHEST, preferred_element_type=F32) + br_ref[...]
    gates_ref[...] = _route(logits, n_groups, per_group)


def _whole(kernel, out_shape, *args, name):
    return pl.pallas_call(
        kernel, out_shape=out_shape,
        compiler_params=pltpu.CompilerParams(vmem_limit_bytes=VMEM_LIMIT_BYTES), name=name)(*args)


def _mixer_sample(x, s0, h0, c0, p, tb):
    n, d = x.shape
    wa_w = p["hgg"].shape[1]
    wb_w = p["cb"].shape[1]
    n_heads = wa_w // HEAD_DIM
    sd = lambda w: jax.ShapeDtypeStruct((n, w), F32)
    key_major = jax.ShapeDtypeStruct((wa_w, n), F32)
    q, f, k, v, g, yb, h_new, xr = _whole(
        _sample_in_kernel, (key_major,) * 3 + (sd(wa_w),) * 2 + (sd(wb_w),) * 3,
        x, c0[:, 0, :], c0[:, 1, :], c0[:, 2, :], h0, p["lbw"], p["g1"], p["win"], p["cw"], p["cb"],
        p["wx"], p["bx"], p["wa"], p["ba"], p["lam"], name="sample_in")
    tok = lambda w: pl.BlockSpec((tb, w), lambda i: (i, 0))
    st = pl.BlockSpec((tb, n_heads, HEAD_DIM, HEAD_DIM), lambda i: (i, 0, 0, 0))
    s_new, o = pl.pallas_call(
        _sample_state_kernel,
        grid=(n // tb,),
        in_specs=[st, _const_spec((wa_w, n)), _const_spec((wa_w, n)), _const_spec((wa_w, n)), tok(wa_w)],
        out_specs=(st, tok(wa_w)),
        out_shape=(jax.ShapeDtypeStruct(s0.shape, F32), sd(wa_w)),
        compiler_params=pltpu.CompilerParams(dimension_semantics=("arbitrary",),
                                             vmem_limit_bytes=VMEM_LIMIT_BYTES),
        name="sample_state",
    )(s0, q, f, k, v)
    x1, xn2, gates = _whole(
        functools.partial(_sample_out_kernel, n_groups=p["n_groups"], per_group=p["per_group"]),
        (sd(d), jax.ShapeDtypeStruct((n, d), BF16), sd(ROUTER_LANES)),
        x, o, g, yb, p["hgg"], p["wout"], p["g2"], p["wr"], p["br"], name="sample_out")
    c_new = jnp.stack([c0[:, 1, :], c0[:, 2, :], xr], axis=1)
    return x1, xn2, gates, s_new, h_new, c_new


def _block_diag(w):
    n, c, _ = w.shape
    eye = jnp.eye(n, dtype=w.dtype)
    return (w[:, :, None, :] * eye[:, None, :, None]).reshape(n * c, n * c)


def _prepare(lower_bounds, ln1_g, w_in, hgrn_norm_g, conv_w, conv_b, lru_wx, lru_bx, lru_wa, lru_ba,
             lru_lambda, w_out, ln2_g, router_wg, router_bg, router_we, router_be, exp_w1, exp_w3,
             exp_w2, final_g):
    d = w_in.shape[1]
    n_groups = router_wg.shape[-1]
    per_group = router_we.shape[-1]
    row = lambda a: a.reshape(1, -1).astype(F32)
    we = jnp.transpose(router_we[0], (1, 0, 2)).reshape(d, n_groups * per_group)
    pad = ROUTER_LANES - n_groups - n_groups * per_group
    wr = jnp.concatenate([router_wg[0], we, jnp.zeros((d, pad), F32)], axis=1)
    br = jnp.concatenate([router_bg[0], router_be[0].reshape(-1), jnp.zeros((pad,), F32)]).reshape(1, -1)
    wx = _block_diag(lru_wx[0])
    wa = _block_diag(lru_wa[0])
    return dict(
        n_groups=n_groups, per_group=per_group,
        lbw=lower_bounds.astype(F32), g1=row(ln1_g[0]), win=w_in[0], win_bf=w_in[0].astype(BF16),
        hgg=row(hgrn_norm_g[0]), cw=conv_w[0], cb=row(conv_b[0]),
        wx=wx, wx_bf=wx.astype(BF16), bx=row(lru_bx[0]), wa=wa, wa_bf=wa.astype(BF16), ba=row(lru_ba[0]),
        lam=row(lru_lambda[0]), wout=w_out[0], wout_bf=w_out[0].astype(BF16), g2=row(ln2_g[0]),
        wr=wr, br=br, w1_bf=exp_w1[0].astype(BF16), w3_bf=exp_w3[0].astype(BF16),
        w2_bf=exp_w2[0].astype(BF16), gf=row(final_g))


def kernel(x_prompt, x_sample, state_hgrn, state_rglru, state_conv, lower_bounds, ln1_g, w_in, hgrn_norm_g, conv_w, conv_b, lru_wx, lru_bx, lru_wa, lru_ba, lru_lambda, w_out, ln2_g, router_wg, router_bg, router_we, router_be, exp_w1, exp_w3, exp_w2, final_g):
    assert w_in.shape[0] == 1, "single-layer trunk"
    p = _prepare(lower_bounds, ln1_g, w_in, hgrn_norm_g, conv_w, conv_b, lru_wx, lru_bx, lru_wa, lru_ba,
                 lru_lambda, w_out, ln2_g, router_wg, router_bg, router_we, router_be, exp_w1, exp_w3,
                 exp_w2, final_g)
    bsz, seq, d = x_prompt.shape
    lb_t = min(512, seq)
    x1, xn2, gates, counts, s_p, h_p, c_p = _mixer_prompt(x_prompt, p, lb_t)
    t = bsz * seq
    y_p = _moe_sorted(x1.reshape(t, d), xn2.reshape(t, d), gates.reshape(t, ROUTER_LANES), counts, p)

    n = x_sample.shape[0]
    x1s, xn2s, gates_s, s_s, h_s, c_s = _mixer_sample(x_sample[:, 0, :], state_hgrn[0], state_rglru[0],
                                                      state_conv[0], p, 8)
    y_s = _moe_dense(x1s, xn2s, gates_s, p, n)
    return (y_p.reshape(bsz, seq, d), y_s.reshape(n, 1, d),
            s_p[None], h_p.reshape(1, bsz, -1), c_p[None],
            s_s[None], h_s[None], c_s[None])
```

```python
import functools

import jax
import jax.numpy as jnp
from jax import lax
from jax.experimental import pallas as pl
from jax.experimental.pallas import tpu as pltpu

F32 = jnp.float32
BF16 = jnp.bfloat16
HIGHEST = lax.Precision.HIGHEST

EPS = 1e-6
LRU_C = 8.0
LOG2E = 1.4426950408889634
HEAD_DIM = 128
CHUNK = 64
SUB = 16
UNROLL = 4
ROUTER_LANES = 128
TOK_TILE = 256
SEG = 16
SORT_ROWS = 384
ROW_TILE = 512
VMEM_LIMIT_BYTES = 56 * 1024 * 1024

NT_DIMS = (((1,), (1,)), ((), ()))
TN_DIMS = (((0,), (0,)), ((), ()))


def _rms(x, g):
    return x * lax.rsqrt(jnp.mean(x * x, axis=-1, keepdims=True) + EPS) * g


def _sigmoid(x):
    return 1.0 / (1.0 + jnp.exp(-x))


def _silu(x):
    return x * _sigmoid(x)


def _gelu_tanh(x):
    c = 0.7978845608028654
    return x * (0.5 * (1.0 + jnp.tanh(c * (x + 0.044715 * (x * x * x)))))


def _softplus(z):
    return jnp.maximum(z, 0.0) + jnp.log1p(jnp.exp(-jnp.abs(z)))


def _expm1(x):
    u = jnp.exp(x)
    um1 = u - 1.0
    small = um1 * x / jnp.log(u)
    return jnp.where(um1 == 0.0, x, jnp.where(jnp.abs(x) < 0.5, small, um1))


def _forget_lower_bound(lbw):
    m = jnp.max(lbw, axis=0, keepdims=True)
    e = jnp.exp(lbw - m)
    return e[0:1, :] / jnp.sum(e, axis=0, keepdims=True)


def _route(logits, n_groups, per_group):
    n = logits.shape[-1]
    col = lax.broadcasted_iota(jnp.int32, logits.shape, 1)
    neg = jnp.float32(-jnp.inf)
    big = jnp.int32(n)
    is_g = col < n_groups
    lg = jnp.where(is_g, logits, neg)
    mg = jnp.max(lg, axis=-1, keepdims=True)
    g_idx = jnp.min(jnp.where(lg == mg, col, big), axis=-1, keepdims=True)
    p_top = 1.0 / jnp.sum(jnp.where(is_g, jnp.exp(logits - mg), 0.0), axis=-1, keepdims=True)
    lo = n_groups + per_group * g_idx
    le = jnp.where((col >= lo) & (col < lo + per_group), logits, neg)
    m1 = jnp.max(le, axis=-1, keepdims=True)
    i1 = jnp.min(jnp.where(le == m1, col, big), axis=-1, keepdims=True)
    le2 = jnp.where(col == i1, neg, le)
    m2 = jnp.max(le2, axis=-1, keepdims=True)
    i2 = jnp.min(jnp.where(le2 == m2, col, big), axis=-1, keepdims=True)
    e2 = jnp.exp(m2 - m1)
    den = 1.0 + e2
    w1 = p_top / den
    w2 = p_top * (e2 / den)
    gates = jnp.where(col == i1, w1, 0.0) + jnp.where(col == i2, w2, 0.0)
    return gates + jnp.where(col == 0, g_idx.astype(F32), 0.0)


def _mixer_prompt_kernel(x_ref, lbw_ref, g1_ref, win_ref, hgg_ref, cw_ref, cb_ref, wx_ref, bx_ref,
                         wa_ref, ba_ref, lam_ref, wout_ref, g2_ref, wr_ref, br_ref,
                         x1_ref, xn2_ref, gates_ref, cnt_ref, sout_ref, hout_ref, cout_ref,
                         proj_s, k_s, o_s, st_s, xr_s, a_s, u_s, hcar_s,
                         *, n_groups, per_group):
    lb_t = x_ref.shape[1]
    wa_w = k_s.shape[1]
    wb_w = a_s.shape[1]
    n_heads = wa_w // HEAD_DIM
    j = pl.program_id(1)
    nj = pl.num_programs(1)

    @pl.when(j == 0)
    def _():
        st_s[...] = jnp.zeros_like(st_s)
        hcar_s[...] = jnp.zeros_like(hcar_s)
        xr_s[0:8, :] = jnp.zeros((8, wb_w), F32)

    x = x_ref[0]
    xn = _rms(x, g1_ref[...]).astype(BF16)
    n_cols = win_ref.shape[1]
    for c in range(0, n_cols, 512):
        proj_s[:, c:c + 512] = jnp.dot(xn, win_ref[:, c:c + 512], preferred_element_type=F32)

    lb = _forget_lower_bound(lbw_ref[...])
    f = lb + (1.0 - lb) * _sigmoid(proj_s[:, wa_w:2 * wa_w])
    k_s[...] = 1.0 - f
    logf = jnp.log(f)
    r_i = lax.broadcasted_iota(jnp.int32, (CHUNK, CHUNK), 0)
    c_i = lax.broadcasted_iota(jnp.int32, (CHUNK, CHUNK), 1)
    tri = jnp.where(r_i >= c_i, 1.0, 0.0).astype(BF16)
    lf_hi = logf.astype(BF16)
    rest = logf - lf_hi.astype(F32)
    lf_mid = rest.astype(BF16)
    lf_lo = (rest - lf_mid.astype(F32)).astype(BF16)
    for c in range(0, lb_t, CHUNK):
        cum = [jnp.dot(tri, part[c:c + CHUNK, :], preferred_element_type=F32) for part in (lf_lo, lf_mid, lf_hi)]
        proj_s[c:c + CHUNK, wa_w:2 * wa_w] = LOG2E * ((cum[0] + cum[1]) + cum[2])

    row_sub = lax.broadcasted_iota(jnp.int32, (SUB, HEAD_DIM), 0)
    lane_sub = lax.broadcasted_iota(jnp.int32, (SUB, HEAD_DIM), 1)
    assert n_heads % 2 == 0
    r_kk = lax.broadcasted_iota(jnp.int32, (2 * HEAD_DIM, 2 * HEAD_DIM), 0)
    c_kk = lax.broadcasted_iota(jnp.int32, (2 * HEAD_DIM, 2 * HEAD_DIM), 1)
    ones_kk = jnp.where((r_kk < HEAD_DIM) == (c_kk < HEAD_DIM), 1.0, 0.0).astype(BF16)

    n_sub = CHUNK // SUB
    half = SUB // 2
    lower_left = (row_sub >= half) & (lane_sub < half)
    own_lane = jnp.where(row_sub >= half, half, 0)

    def chunk_start(r0):
        first = []
        for h in range(n_heads):
            hs = slice(h * HEAD_DIM, (h + 1) * HEAD_DIM)
            q = proj_s[pl.ds(r0, CHUNK), hs]
            b = proj_s[pl.ds(r0, CHUNK), wa_w + h * HEAD_DIM:wa_w + (h + 1) * HEAD_DIM]
            v = proj_s[pl.ds(r0, CHUNK), 2 * wa_w + h * HEAD_DIM:2 * wa_w + (h + 1) * HEAD_DIM]
            k = k_s[pl.ds(r0, CHUNK), hs]
            vb = v.astype(BF16)
            st = st_s[h]
            b_last = b[CHUNK - 1:CHUNK, :]
            o = lax.dot_general((q * jnp.exp2(b)).astype(BF16), st.astype(BF16), NT_DIMS,
                                preferred_element_type=F32)
            k_end = k * jnp.exp2(b_last - b)
            st_s[h] = st * jnp.exp2(b_last) + lax.dot_general(vb, k_end.astype(BF16), TN_DIMS,
                                                               preferred_element_type=F32)
            terms, off, mid = [], [], []
            for i in range(n_sub):
                lo = i * SUB
                qi, bi, ki = q[lo:lo + SUB], b[lo:lo + SUB], k[lo:lo + SUB]
                q3, b3, k3 = (a.reshape(2, half, HEAD_DIM) for a in (qi, bi, ki))
                terms += [(q3 * (k3[:, j:j + 1, :] * jnp.exp2(b3 - b3[:, j:j + 1, :]))).reshape(SUB, HEAD_DIM)
                          .astype(BF16) for j in range(half)]
                rm = bi[half - 1:half]
                mid.append(lax.dot_general((qi * jnp.exp2(bi - rm)).astype(BF16),
                                           (ki * jnp.exp2(rm - bi)).astype(BF16), NT_DIMS,
                                           preferred_element_type=F32))
                if i > 0:
                    r = b[lo - 1:lo]
                    qt = (qi * jnp.exp2(bi - r)).astype(BF16)
                    kt = (k[:lo] * jnp.exp2(r - b[:lo])).astype(BF16)
                    off.append(lax.dot_general(qt, kt, NT_DIMS, preferred_element_type=F32))
            first.append([o, vb, off, jnp.concatenate(terms, axis=0), mid])
        for h in range(0, n_heads, 2):
            both = jnp.concatenate([first[h][3], first[h + 1][3]], axis=1)
            sums = jnp.dot(both, ones_kk, preferred_element_type=F32)
            first[h][3] = sums[:, :HEAD_DIM]
            first[h + 1][3] = sums[:, HEAD_DIM:]
        return first

    def chunk_finish(r0, first):
        for h in range(n_heads):
            o, vb, off, sums, mid = first[h]
            outs = []
            for i in range(n_sub):
                lo = i * SUB
                sc = jnp.zeros((SUB, HEAD_DIM), F32)
                for j in range(half):
                    row0 = (i * half + j) * SUB
                    sc = jnp.where(lane_sub == own_lane + j, sums[row0:row0 + SUB], sc)
                sc = jnp.where(row_sub >= lane_sub, sc, 0.0)[:, :SUB]
                sc = jnp.where(lower_left[:, :SUB], mid[i], sc)
                od = o[lo:lo + SUB] + jnp.dot(sc.astype(BF16), vb[lo:lo + SUB], preferred_element_type=F32)
                if i > 0:
                    od = od + jnp.dot(off[i - 1].astype(BF16), vb[:lo], preferred_element_type=F32)
                outs.append(od)
            o_s[pl.ds(r0, CHUNK), h * HEAD_DIM:(h + 1) * HEAD_DIM] = jnp.concatenate(outs, axis=0)

    def chunks_body(ci, carry):
        rows = [pl.multiple_of((ci * UNROLL + u) * CHUNK, CHUNK) for u in range(UNROLL)]
        started = [chunk_start(r0) for r0 in rows]
        for r0, first in zip(rows, started):
            chunk_finish(r0, first)
        return carry

    assert lb_t % (CHUNK * UNROLL) == 0
    lax.fori_loop(0, lb_t // (CHUNK * UNROLL), chunks_body, 0)

    ya = []
    for h in range(n_heads):
        hs = slice(h * HEAD_DIM, (h + 1) * HEAD_DIM)
        oh = o_s[:, hs]
        oh = oh * lax.rsqrt(jnp.mean(oh * oh, axis=-1, keepdims=True) + EPS) * hgg_ref[:, hs]
        ya.append(oh * _silu(proj_s[:, 3 * wa_w + h * HEAD_DIM:3 * wa_w + (h + 1) * HEAD_DIM]))

    xb0 = 4 * wa_w
    xr_s[pl.ds(8, lb_t), :] = proj_s[:, xb0:xb0 + wb_w]
    xc = (cb_ref[...] + cw_ref[3:4, :] * xr_s[pl.ds(8, lb_t), :] + cw_ref[2:3, :] * xr_s[pl.ds(7, lb_t), :]
          + cw_ref[1:2, :] * xr_s[pl.ds(6, lb_t), :] + cw_ref[0:1, :] * xr_s[pl.ds(5, lb_t), :])
    tail = xr_s[pl.ds(lb_t + 5, 3), :]
    xr_s[5:8, :] = tail
    cout_ref[0] = tail
    xcb = xc.astype(BF16)
    gate_x = _sigmoid(jnp.dot(xcb, wx_ref[...], preferred_element_type=F32) + bx_ref[...])
    gate_a = _sigmoid(jnp.dot(xcb, wa_ref[...], preferred_element_type=F32) + ba_ref[...])
    log_a = (-LRU_C) * gate_a * _softplus(-lam_ref[...])
    a = jnp.exp(log_a)
    mult = jnp.sqrt(-_expm1(2.0 * log_a))
    first = (lax.broadcasted_iota(jnp.int32, (lb_t, 1), 0) == 0) & (j == 0)
    a = jnp.where(first, 0.0, a)
    mult = jnp.where(first, 1.0, mult)
    a_s[...] = a
    u_s[...] = gate_x * xc * mult
    row8 = lax.broadcasted_iota(jnp.int32, (8, 1), 0)

    def scan_body(gi, carry):
        r0 = pl.multiple_of(gi * 8, 8)
        aa = a_s[pl.ds(r0, 8), :]
        uu = u_s[pl.ds(r0, 8), :]
        for s in (1, 2, 4):
            m = row8 >= s
            uu = jnp.where(m, aa * pltpu.roll(uu, s, 0) + uu, uu)
            aa = jnp.where(m, aa * pltpu.roll(aa, s, 0), aa)
        hh = aa * carry + uu
        u_s[pl.ds(r0, 8), :] = hh
        return hh[7:8, :]

    h_last = lax.fori_loop(0, lb_t // 8, scan_body, hcar_s[...])
    hcar_s[...] = h_last
    hout_ref[0] = h_last
    yb = u_s[...] * _gelu_tanh(proj_s[:, xb0 + wb_w:xb0 + 2 * wb_w])

    y = jnp.concatenate(ya + [yb], axis=-1).astype(BF16)
    x1 = x + jnp.dot(y, wout_ref[...], preferred_element_type=F32)
    x1_ref[0] = x1
    xn2 = _rms(x1, g2_ref[...]).astype(BF16)
    xn2_ref[0] = xn2
    logits = jnp.dot(xn2, wr_ref[...], preferred_element_type=F32) + br_ref[...]
    gates = _route(logits, n_groups, per_group)
    gates_ref[0] = gates
    col_t = lax.broadcasted_iota(jnp.int32, (TOK_TILE, ROUTER_LANES), 1)
    for t in range(lb_t // TOK_TILE):
        gi = gates[t * TOK_TILE:(t + 1) * TOK_TILE, 0:1].astype(jnp.int32)
        cnt_ref[0, t:t + 1, :] = jnp.sum(jnp.where(col_t == gi, 1, 0), axis=0, keepdims=True)

    @pl.when(j == nj - 1)
    def _():
        for h in range(n_heads):
            sout_ref[0, h] = st_s[h].T


def _const_spec(shape):
    nd = len(shape)
    return pl.BlockSpec(shape, lambda *_: (0,) * nd)


def _mixer_prompt(x, p, lb_t):
    bsz, seq, d = x.shape
    wa_w = p["hgg"].shape[1]
    wb_w = p["cb"].shape[1]
    n_heads = wa_w // HEAD_DIM
    n_cols = p["win_bf"].shape[1]
    weights = [p["lbw"], p["g1"], p["win_bf"], p["hgg"], p["cw"], p["cb"], p["wx_bf"], p["bx"],
               p["wa_bf"], p["ba"], p["lam"], p["wout_bf"], p["g2"], p["wr"].astype(BF16), p["br"]]
    nj = seq // lb_t
    tile = lambda w: pl.BlockSpec((1, lb_t, w), lambda b, j: (b, j, 0))
    out_shape = (
        jax.ShapeDtypeStruct((bsz, seq, d), F32),
        jax.ShapeDtypeStruct((bsz, seq, d), BF16),
        jax.ShapeDtypeStruct((bsz, seq, ROUTER_LANES), F32),
        jax.ShapeDtypeStruct((bsz * (seq // lb_t), lb_t // TOK_TILE, ROUTER_LANES), jnp.int32),
        jax.ShapeDtypeStruct((bsz, n_heads, HEAD_DIM, HEAD_DIM), F32),
        jax.ShapeDtypeStruct((bsz, 1, wb_w), F32),
        jax.ShapeDtypeStruct((bsz, 3, wb_w), F32),
    )
    out_specs = (
        tile(d), tile(d), tile(ROUTER_LANES),
        pl.BlockSpec((1, lb_t // TOK_TILE, ROUTER_LANES), lambda b, j: (b * nj + j, 0, 0)),
        pl.BlockSpec((1, n_heads, HEAD_DIM, HEAD_DIM), lambda b, j: (b, 0, 0, 0)),
        pl.BlockSpec((1, 1, wb_w), lambda b, j: (b, 0, 0)),
        pl.BlockSpec((1, 3, wb_w), lambda b, j: (b, 0, 0)),
    )
    scratch = [
        pltpu.VMEM((lb_t, n_cols), F32),
        pltpu.VMEM((lb_t, wa_w), F32),
        pltpu.VMEM((lb_t, wa_w), F32),
        pltpu.VMEM((n_heads, HEAD_DIM, HEAD_DIM), F32),
        pltpu.VMEM((lb_t + 8, wb_w), F32),
        pltpu.VMEM((lb_t, wb_w), F32),
        pltpu.VMEM((lb_t, wb_w), F32),
        pltpu.VMEM((1, wb_w), F32),
    ]
    kern = functools.partial(_mixer_prompt_kernel, n_groups=p["n_groups"], per_group=p["per_group"])
    return pl.pallas_call(
        kern,
        grid=(bsz, nj),
        in_specs=[tile(d)] + [_const_spec(w.shape) for w in weights],
        out_specs=out_specs,
        out_shape=out_shape,
        scratch_shapes=scratch,
        compiler_params=pltpu.CompilerParams(dimension_semantics=("arbitrary", "arbitrary"),
                                             vmem_limit_bytes=VMEM_LIMIT_BYTES),
        name="mixer_prompt",
    )(x, *weights)


def _moe_kernel(x1_ref, xn2_ref, gates_ref, w1_ref, w3_ref, w2_ref, gf_ref, y_ref, acc_s, *, n_groups):
    e = pl.program_id(1)

    @pl.when(e == 0)
    def _():
        acc_s[...] = jnp.zeros_like(acc_s)

    xb = xn2_ref[...]
    col = lax.broadcasted_iota(jnp.int32, gates_ref.shape, 1)
    gate = jnp.sum(jnp.where(col == e + n_groups, gates_ref[...], 0.0), axis=-1, keepdims=True)
    h = _silu(jnp.dot(xb, w1_ref[0], preferred_element_type=F32)) * jnp.dot(xb, w3_ref[0], preferred_element_type=F32)
    acc_s[...] += jnp.dot((h * gate).astype(BF16), w2_ref[0], preferred_element_type=F32)

    @pl.when(e == pl.num_programs(1) - 1)
    def _():
        y_ref[...] = _rms(x1_ref[...] + acc_s[...], gf_ref[...])


def _moe_dense(x1, xn2, gates, p, tm):
    t, d = x1.shape
    n_exp, _, d_exp = p["w1_bf"].shape
    row = lambda w: pl.BlockSpec((tm, w), lambda i, e: (i, 0))
    return pl.pallas_call(
        functools.partial(_moe_kernel, n_groups=p["n_groups"]),
        grid=(t // tm, n_exp),
        in_specs=[row(d), row(d), row(ROUTER_LANES),
                  pl.BlockSpec((1, d, d_exp), lambda i, e: (e, 0, 0)),
                  pl.BlockSpec((1, d, d_exp), lambda i, e: (e, 0, 0)),
                  pl.BlockSpec((1, d_exp, d), lambda i, e: (e, 0, 0)),
                  _const_spec(p["gf"].shape)],
        out_specs=row(d),
        out_shape=jax.ShapeDtypeStruct((t, d), F32),
        scratch_shapes=[pltpu.VMEM((tm, d), F32)],
        compiler_params=pltpu.CompilerParams(dimension_semantics=("arbitrary", "arbitrary"),
                                             vmem_limit_bytes=VMEM_LIMIT_BYTES),
        name="moe_dense",
    )(x1, xn2, gates, p["w1_bf"], p["w3_bf"], p["w2_bf"], p["gf"])


def _seg_pad(n):
    return jnp.bitwise_and(n + (SEG - 1), -SEG)


def _row_tile_pad(n):
    return jnp.bitwise_and(n + (ROW_TILE - 1), -ROW_TILE)


def _tile_segments(cnt_ref, tile, n_groups):
    padded = [_seg_pad(cnt_ref[tile * n_groups + g]) for g in range(n_groups)]
    starts, acc = [], jnp.int32(0)
    for g in range(n_groups):
        starts.append(acc)
        acc = acc + padded[g]
    return padded, starts, acc


def _group_bases(cnt_ref, n_tiles, n_groups):
    def body(t, tot):
        return tuple(tot[g] + _seg_pad(cnt_ref[t * n_groups + g]) for g in range(n_groups))
    totals = lax.fori_loop(0, n_tiles, body, (jnp.int32(0),) * n_groups)
    bases, ends, acc = [], [], jnp.int32(0)
    for g in range(n_groups):
        bases.append(acc)
        acc = acc + _row_tile_pad(totals[g])
        ends.append(acc)
    return bases, ends


def _chunk_row(c, padded, starts, gstart):
    lo = c * SEG
    row = jnp.int32(0)
    for g in range(len(padded)):
        inside = (lo >= starts[g]) & (lo < starts[g] + padded[g])
        row = row + jnp.where(inside, gstart[g] + lo - starts[g], 0)
    return pl.multiple_of(row, SEG)


def _sort_matrix(gates, starts):
    n_groups = len(starts)
    col = lax.broadcasted_iota(jnp.int32, (TOK_TILE, ROUTER_LANES), 1)
    gi = gates[:, 0:1].astype(jnp.int32)
    onehot = col == gi
    r_i = lax.broadcasted_iota(jnp.int32, (TOK_TILE, TOK_TILE), 0)
    c_i = lax.broadcasted_iota(jnp.int32, (TOK_TILE, TOK_TILE), 1)
    earlier = jnp.where(r_i > c_i, 1.0, 0.0).astype(BF16)
    before = jnp.dot(earlier, jnp.where(onehot, 1.0, 0.0).astype(BF16), preferred_element_type=F32)
    rank = jnp.sum(jnp.where(onehot, before, 0.0), axis=-1, keepdims=True).astype(jnp.int32)
    base = jnp.zeros((TOK_TILE, 1), jnp.int32)
    for g in range(n_groups):
        base = base + jnp.where(gi == g, starts[g], 0)
    lane = lax.broadcasted_iota(jnp.int32, (TOK_TILE, SORT_ROWS), 1)
    return lane == base + rank


def _sort_matrix_rows(gates, starts):
    n_groups = len(starts)
    col = lax.broadcasted_iota(jnp.int32, (TOK_TILE, ROUTER_LANES), 1)
    onehot_t = jnp.where(col == gates[:, 0:1].astype(jnp.int32), 1.0, 0.0).T
    r_i = lax.broadcasted_iota(jnp.int32, (TOK_TILE, TOK_TILE), 0)
    c_i = lax.broadcasted_iota(jnp.int32, (TOK_TILE, TOK_TILE), 1)
    later = jnp.where(r_i < c_i, 1.0, 0.0).astype(BF16)
    before = jnp.dot(onehot_t.astype(BF16), later, preferred_element_type=F32)
    g_row = lax.broadcasted_iota(jnp.int32, (ROUTER_LANES, 1), 0)
    base = jnp.zeros((ROUTER_LANES, 1), jnp.int32)
    for g in range(n_groups):
        base = base + jnp.where(g_row == g, starts[g], 0)
    dest = jnp.sum(jnp.where(onehot_t > 0.0, before + base.astype(F32), 0.0), axis=0, keepdims=True)
    row = lax.broadcasted_iota(jnp.int32, (SORT_ROWS, TOK_TILE), 0)
    return row == dest.astype(jnp.int32)


def _dispatch_copies(xbuf, gbuf, xs_hbm, gs_hbm, sem, slot, c, row):
    return (pltpu.make_async_copy(xbuf.at[slot, pl.ds(c * SEG, SEG), :], xs_hbm.at[pl.ds(row, SEG), :], sem.at[slot]),
            pltpu.make_async_copy(gbuf.at[slot, pl.ds(c * SEG, SEG), :], gs_hbm.at[pl.ds(row, SEG), :], sem.at[slot]))


def _moe_dispatch_kernel(cnt_ref, xn2_ref, gates_ref, xs_in, gs_in, xs_hbm, gs_hbm, tg_ref,
                         xbuf, gbuf, sem, gstart_s, *, n_groups, per_group):
    del xs_in, gs_in
    j = pl.program_id(0)
    n_tiles = pl.num_programs(0)
    n_chunks = (TOK_TILE + n_groups * SEG) // SEG
    slot = lax.rem(j, 2)

    @pl.when(j == 0)
    def _():
        bases, ends = _group_bases(cnt_ref, n_tiles, n_groups)
        for g in range(n_groups):
            gstart_s[g] = bases[g]
        n_rt = tg_ref.shape[0] - 1
        for i in range(n_rt):
            tg = jnp.int32(0)
            for g in range(n_groups):
                tg = tg + jnp.where(ends[g] <= i * ROW_TILE, 1, 0)
            tg_ref[i] = tg
        tg_ref[n_rt] = ends[n_groups - 1] // ROW_TILE

    def wait_tile(tile, slot_):
        _, _, used = _tile_segments(cnt_ref, tile, n_groups)
        for c in range(n_chunks):
            @pl.when(c * SEG < used)
            def _():
                for cp in _dispatch_copies(xbuf, gbuf, xs_hbm, gs_hbm, sem, slot_, c, 0):
                    cp.wait()

    @pl.when(j >= 2)
    def _():
        wait_tile(j - 2, slot)

    padded, starts, used = _tile_segments(cnt_ref, j, n_groups)
    gates = gates_ref[...]
    d = xn2_ref.shape[1]
    g_hi = gates.astype(BF16)
    g_lo = (gates - g_hi.astype(F32)).astype(BF16)
    sort_m = jnp.where(_sort_matrix_rows(gates, starts), 1.0, 0.0).astype(BF16)
    moved = jnp.dot(sort_m, jnp.concatenate([xn2_ref[...], g_hi, g_lo], axis=1), preferred_element_type=F32)
    xbuf[slot] = moved[:, :d].astype(BF16)
    gbuf[slot] = moved[:, d:d + ROUTER_LANES] + moved[:, d + ROUTER_LANES:]
    gstart = [gstart_s[g] for g in range(n_groups)]
    for c in range(n_chunks):
        @pl.when(c * SEG < used)
        def _():
            for cp in _dispatch_copies(xbuf, gbuf, xs_hbm, gs_hbm, sem, slot, c,
                                       _chunk_row(c, padded, starts, gstart)):
                cp.start()
    for g in range(n_groups):
        gstart_s[g] = gstart[g] + padded[g]

    @pl.when(j == n_tiles - 1)
    def _():
        @pl.when(j >= 1)
        def _():
            wait_tile(j - 1, 1 - slot)
        wait_tile(j, slot)


def _moe_group_kernel(tg_ref, xs_ref, gs_ref, w1_ref, w3_ref, w2_ref, ys_ref, *, n_groups, per_group):
    i = pl.program_id(0)

    @pl.when(tg_ref[i] < n_groups)
    def _():
        xb = xs_ref[...]
        gates = pltpu.roll(gs_ref[...], ROUTER_LANES - n_groups - per_group * tg_ref[i], 1)
        acc = jnp.zeros(ys_ref.shape, F32)
        for s in range(per_group):
            h = (_silu(jnp.dot(xb, w1_ref[s], preferred_element_type=F32))
                 * jnp.dot(xb, w3_ref[s], preferred_element_type=F32) * gates[:, s:s + 1])
            acc = acc + jnp.dot(h.astype(BF16), w2_ref[s], preferred_element_type=F32)
        ys_ref[...] = acc.astype(BF16)


def _moe_combine_kernel(cnt_ref, gates_ref, x1_ref, gf_ref, ys_hbm, y_ref, ybuf, sem, gstart_s, *, n_groups):
    j = pl.program_id(0)
    n_tiles = pl.num_programs(0)
    n_chunks = SORT_ROWS // SEG
    slot = lax.rem(j, 2)

    def copies(slot_, c, row):
        return pltpu.make_async_copy(ys_hbm.at[pl.ds(row, SEG), :], ybuf.at[slot_, pl.ds(c * SEG, SEG), :],
                                     sem.at[slot_])

    def fetch_tile(tile, slot_):
        padded, starts, _ = _tile_segments(cnt_ref, tile, n_groups)
        gstart = [gstart_s[g] for g in range(n_groups)]
        for c in range(n_chunks):
            copies(slot_, c, _chunk_row(c, padded, starts, gstart)).start()
        for g in range(n_groups):
            gstart_s[g] = gstart[g] + padded[g]

    @pl.when(j == 0)
    def _():
        bases, _ = _group_bases(cnt_ref, n_tiles, n_groups)
        for g in range(n_groups):
            gstart_s[g] = bases[g]
        fetch_tile(0, 0)

    @pl.when(j + 1 < n_tiles)
    def _():
        fetch_tile(j + 1, 1 - slot)

    for c in range(n_chunks):
        copies(slot, c, 0).wait()
    _, starts, _ = _tile_segments(cnt_ref, j, n_groups)
    sort_t = _sort_matrix(gates_ref[...], starts)
    moe = jnp.dot(jnp.where(sort_t, 1.0, 0.0).astype(BF16), ybuf[slot], preferred_element_type=F32)
    y_ref[...] = _rms(x1_ref[...] + moe, gf_ref[...])


def _moe_sorted(x1, xn2, gates, counts, p):
    t, d = x1.shape
    n_groups, per_group = p["n_groups"], p["per_group"]
    n_tiles = t // TOK_TILE
    d_exp = p["w1_bf"].shape[2]
    cap = t + n_tiles * n_groups * SEG + n_groups * ROW_TILE
    n_rt = -(-cap // ROW_TILE)
    rows = n_rt * ROW_TILE
    cnt = counts[:, :, :n_groups].reshape(-1)
    params = pltpu.CompilerParams(dimension_semantics=("arbitrary",), vmem_limit_bytes=VMEM_LIMIT_BYTES)
    any_spec = pl.BlockSpec(memory_space=pl.ANY)
    tok = lambda w: pl.BlockSpec((TOK_TILE, w), lambda j, c: (j, 0))

    xs, gs, tile_group = pl.pallas_call(
        functools.partial(_moe_dispatch_kernel, n_groups=n_groups, per_group=per_group),
        grid_spec=pltpu.PrefetchScalarGridSpec(
            num_scalar_prefetch=1, grid=(n_tiles,),
            in_specs=[tok(d), tok(ROUTER_LANES), any_spec, any_spec],
            out_specs=(any_spec, any_spec, pl.BlockSpec(memory_space=pltpu.SMEM)),
            scratch_shapes=[pltpu.VMEM((2, SORT_ROWS, d), BF16), pltpu.VMEM((2, SORT_ROWS, ROUTER_LANES), F32),
                            pltpu.SemaphoreType.DMA((2,)), pltpu.SMEM((n_groups,), jnp.int32)]),
        out_shape=(jax.ShapeDtypeStruct((rows, d), BF16), jax.ShapeDtypeStruct((rows, ROUTER_LANES), F32),
                   jax.ShapeDtypeStruct((n_rt + 1,), jnp.int32)),
        input_output_aliases={3: 0, 4: 1},
        compiler_params=params, name="moe_dispatch",
    )(cnt, xn2, gates, jnp.zeros((rows, d), BF16), jnp.zeros((rows, ROUTER_LANES), F32))

    used_tile = lambda i, tg: (jnp.minimum(i, tg[n_rt] - 1), 0)
    group_w = lambda i, tg: (jnp.minimum(tg[i], n_groups - 1), 0, 0)
    ys = pl.pallas_call(
        functools.partial(_moe_group_kernel, n_groups=n_groups, per_group=per_group),
        grid_spec=pltpu.PrefetchScalarGridSpec(
            num_scalar_prefetch=1, grid=(n_rt,),
            in_specs=[pl.BlockSpec((ROW_TILE, d), used_tile), pl.BlockSpec((ROW_TILE, ROUTER_LANES), used_tile),
                      pl.BlockSpec((per_group, d, d_exp), group_w), pl.BlockSpec((per_group, d, d_exp), group_w),
                      pl.BlockSpec((per_group, d_exp, d), group_w)],
            out_specs=pl.BlockSpec((ROW_TILE, d), used_tile)),
        out_shape=jax.ShapeDtypeStruct((rows, d), BF16),
        input_output_aliases={1: 0},
        compiler_params=params, name="moe_experts",
    )(tile_group, xs, gs, p["w1_bf"], p["w3_bf"], p["w2_bf"])

    return pl.pallas_call(
        functools.partial(_moe_combine_kernel, n_groups=n_groups),
        grid_spec=pltpu.PrefetchScalarGridSpec(
            num_scalar_prefetch=1, grid=(n_tiles,),
            in_specs=[tok(ROUTER_LANES), tok(d), pl.BlockSpec(p["gf"].shape, lambda j, c: (0, 0)), any_spec],
            out_specs=tok(d),
            scratch_shapes=[pltpu.VMEM((2, SORT_ROWS, d), BF16), pltpu.SemaphoreType.DMA((2,)),
                            pltpu.SMEM((n_groups,), jnp.int32)]),
        out_shape=jax.ShapeDtypeStruct((t, d), F32),
        compiler_params=params, name="moe_combine",
    )(cnt, gates, x1, p["gf"], ys)


def _sample_in_kernel(x_ref, c0_ref, c1_ref, c2_ref, h0_ref, lbw_ref, g1_ref, win_ref, cw_ref, cb_ref,
                      wx_ref, bx_ref, wa_ref, ba_ref, lam_ref,
                      q_ref, f_ref, k_ref, v_ref, g_ref, yb_ref, hnew_ref, xr_ref):
    wa_w = v_ref.shape[1]
    wb_w = yb_ref.shape[1]
    xn = _rms(x_ref[...], g1_ref[...])
    proj = jnp.dot(xn, win_ref[...], precision=HIGHEST, preferred_element_type=F32)
    lb = _forget_lower_bound(lbw_ref[...])
    f = lb + (1.0 - lb) * _sigmoid(proj[:, wa_w:2 * wa_w])
    q_ref[...] = proj[:, 0:wa_w].T
    f_ref[...] = f.T
    k_ref[...] = (1.0 - f).T
    v_ref[...] = proj[:, 2 * wa_w:3 * wa_w]
    g_ref[...] = proj[:, 3 * wa_w:4 * wa_w]
    xr = proj[:, 4 * wa_w:4 * wa_w + wb_w]
    xr_ref[...] = xr
    xc = (cb_ref[...] + cw_ref[0:1, :] * c0_ref[...] + cw_ref[1:2, :] * c1_ref[...]
          + cw_ref[2:3, :] * c2_ref[...] + cw_ref[3:4, :] * xr)
    gate_x = _sigmoid(jnp.dot(xc, wx_ref[...], precision=HIGHEST, preferred_element_type=F32) + bx_ref[...])
    gate_a = _sigmoid(jnp.dot(xc, wa_ref[...], precision=HIGHEST, preferred_element_type=F32) + ba_ref[...])
    log_a = (-LRU_C) * gate_a * _softplus(-lam_ref[...])
    a = jnp.exp(log_a)
    mult = jnp.sqrt(-_expm1(2.0 * log_a))
    h = a * h0_ref[...] + gate_x * xc * mult
    hnew_ref[...] = h
    yb_ref[...] = h * _gelu_tanh(proj[:, 4 * wa_w + wb_w:4 * wa_w + 2 * wb_w])


def _sample_state_kernel(s_ref, qt_ref, ft_ref, kt_ref, v_ref, snew_ref, o_ref):
    tb, n_heads = s_ref.shape[0], s_ref.shape[1]
    n_tok = qt_ref.shape[1]
    shift = lax.rem(n_tok - pl.program_id(0) * tb, n_tok)
    sq = (HEAD_DIM, HEAD_DIM)
    for h in range(n_heads):
        hs = slice(h * HEAD_DIM, (h + 1) * HEAD_DIM)
        qh = pltpu.roll(qt_ref[hs, :], shift, 1)
        fh = pltpu.roll(ft_ref[hs, :], shift, 1)
        kh = pltpu.roll(kt_ref[hs, :], shift, 1)
        rows = []
        for t in range(tb):
            s_new = (jnp.broadcast_to(fh[:, t:t + 1], sq) * s_ref[t, h]
                     + jnp.broadcast_to(kh[:, t:t + 1], sq) * v_ref[t:t + 1, hs])
            snew_ref[t, h] = s_new
            rows.append(jnp.sum(jnp.broadcast_to(qh[:, t:t + 1], sq) * s_new, axis=0, keepdims=True))
        o_ref[:, hs] = jnp.concatenate(rows, axis=0)


def _sample_out_kernel(x_ref, o_ref, g_ref, yb_ref, hgg_ref, wout_ref, g2_ref, wr_ref, br_ref,
                       x1_ref, xn2_ref, gates_ref, *, n_groups, per_group):
    wa_w = o_ref.shape[1]
    ya = []
    for h in range(wa_w // HEAD_DIM):
        hs = slice(h * HEAD_DIM, (h + 1) * HEAD_DIM)
        oh = o_ref[:, hs]
        oh = oh * lax.rsqrt(jnp.mean(oh * oh, axis=-1, keepdims=True) + EPS) * hgg_ref[:, hs]
        ya.append(oh * _silu(g_ref[:, hs]))
    y = jnp.concatenate(ya + [yb_ref[...]], axis=-1)
    x1 = x_ref[...] + jnp.dot(y, wout_ref[...], precision=HIGHEST, preferred_element_type=F32)
    x1_ref[...] = x1
    xn2 = _rms(x1, g2_ref[...])
    xn2_ref[...] = xn2.astype(BF16)
    logits = jnp.dot(xn2, wr_ref[...], precision=HIGHEST, preferred_element_type=F32) + br_ref[...]
    gates_ref[...] = _route(logits, n_groups, per_group)


def _whole(kernel, out_shape, *args, name):
    return pl.pallas_call(
        kernel, out_shape=out_shape,
        compiler_params=pltpu.CompilerParams(vmem_limit_bytes=VMEM_LIMIT_BYTES), name=name)(*args)


def _mixer_sample(x, s0, h0, c0, p, tb):
    n, d = x.shape
    wa_w = p["hgg"].shape[1]
    wb_w = p["cb"].shape[1]
    n_heads = wa_w // HEAD_DIM
    sd = lambda w: jax.ShapeDtypeStruct((n, w), F32)
    key_major = jax.ShapeDtypeStruct((wa_w, n), F32)
    q, f, k, v, g, yb, h_new, xr = _whole(
        _sample_in_kernel, (key_major,) * 3 + (sd(wa_w),) * 2 + (sd(wb_w),) * 3,
        x, c0[:, 0, :], c0[:, 1, :], c0[:, 2, :], h0, p["lbw"], p["g1"], p["win"], p["cw"], p["cb"],
        p["wx"], p["bx"], p["wa"], p["ba"], p["lam"], name="sample_in")
    tok = lambda w: pl.BlockSpec((tb, w), lambda i: (i, 0))
    st = pl.BlockSpec((tb, n_heads, HEAD_DIM, HEAD_DIM), lambda i: (i, 0, 0, 0))
    s_new, o = pl.pallas_call(
        _sample_state_kernel,
        grid=(n // tb,),
        in_specs=[st, _const_spec((wa_w, n)), _const_spec((wa_w, n)), _const_spec((wa_w, n)), tok(wa_w)],
        out_specs=(st, tok(wa_w)),
        out_shape=(jax.ShapeDtypeStruct(s0.shape, F32), sd(wa_w)),
        compiler_params=pltpu.CompilerParams(dimension_semantics=("arbitrary",),
                                             vmem_limit_bytes=VMEM_LIMIT_BYTES),
        name="sample_state",
    )(s0, q, f, k, v)
    x1, xn2, gates = _whole(
        functools.partial(_sample_out_kernel, n_groups=p["n_groups"], per_group=p["per_group"]),
        (sd(d), jax.ShapeDtypeStruct((n, d), BF16), sd(ROUTER_LANES)),
        x, o, g, yb, p["hgg"], p["wout"], p["g2"], p["wr"], p["br"], name="sample_out")
    c_new = jnp.stack([c0[:, 1, :], c0[:, 2, :], xr], axis=1)
    return x1, xn2, gates, s_new, h_new, c_new


def _block_diag(w):
    n, c, _ = w.shape
    eye = jnp.eye(n, dtype=w.dtype)
    return (w[:, :, None, :] * eye[:, None, :, None]).reshape(n * c, n * c)


def _prepare(lower_bounds, ln1_g, w_in, hgrn_norm_g, conv_w, conv_b, lru_wx, lru_bx, lru_wa, lru_ba,
             lru_lambda, w_out, ln2_g, router_wg, router_bg, router_we, router_be, exp_w1, exp_w3,
             exp_w2, final_g):
    d = w_in.shape[1]
    n_groups = router_wg.shape[-1]
    per_group = router_we.shape[-1]
    row = lambda a: a.reshape(1, -1).astype(F32)
    we = jnp.transpose(router_we[0], (1, 0, 2)).reshape(d, n_groups * per_group)
    pad = ROUTER_LANES - n_groups - n_groups * per_group
    wr = jnp.concatenate([router_wg[0], we, jnp.zeros((d, pad), F32)], axis=1)
    br = jnp.concatenate([router_bg[0], router_be[0].reshape(-1), jnp.zeros((pad,), F32)]).reshape(1, -1)
    wx = _block_diag(lru_wx[0])
    wa = _block_diag(lru_wa[0])
    return dict(
        n_groups=n_groups, per_group=per_group,
        lbw=lower_bounds.astype(F32), g1=row(ln1_g[0]), win=w_in[0], win_bf=w_in[0].astype(BF16),
        hgg=row(hgrn_norm_g[0]), cw=conv_w[0], cb=row(conv_b[0]),
        wx=wx, wx_bf=wx.astype(BF16), bx=row(lru_bx[0]), wa=wa, wa_bf=wa.astype(BF16), ba=row(lru_ba[0]),
        lam=row(lru_lambda[0]), wout=w_out[0], wout_bf=w_out[0].astype(BF16), g2=row(ln2_g[0]),
        wr=wr, br=br, w1_bf=exp_w1[0].astype(BF16), w3_bf=exp_w3[0].astype(BF16),
        w2_bf=exp_w2[0].astype(BF16), gf=row(final_g))


def kernel(x_prompt, x_sample, state_hgrn, state_rglru, state_conv, lower_bounds, ln1_g, w_in, hgrn_norm_g, conv_w, conv_b, lru_wx, lru_bx, lru_wa, lru_ba, lru_lambda, w_out, ln2_g, router_wg, router_bg, router_we, router_be, exp_w1, exp_w3, exp_w2, final_g):
    assert w_in.shape[0] == 1, "single-layer trunk"
    p = _prepare(lower_bounds, ln1_g, w_in, hgrn_norm_g, conv_w, conv_b, lru_wx, lru_bx, lru_wa, lru_ba,
                 lru_lambda, w_out, ln2_g, router_wg, router_bg, router_we, router_be, exp_w1, exp_w3,
                 exp_w2, final_g)
    bsz, seq, d = x_prompt.shape
    lb_t = min(512, seq)
    x1, xn2, gates, counts, s_p, h_p, c_p = _mixer_prompt(x_prompt, p, lb_t)
    t = bsz * seq
    y_p = _moe_sorted(x1.reshape(t, d), xn2.reshape(t, d), gates.reshape(t, ROUTER_LANES), counts, p)

    n = x_sample.shape[0]
    x1s, xn2s, gates_s, s_s, h_s, c_s = _mixer_sample(x_sample[:, 0, :], state_hgrn[0], state_rglru[0],
                                                      state_conv[0], p, 8)
    y_s = _moe_dense(x1s, xn2s, gates_s, p, n)
    return (y_p.reshape(bsz, seq, d), y_s.reshape(n, 1, d),
            s_p[None], h_p.reshape(1, bsz, -1), c_p[None],
            s_s[None], h_s[None], c_s[None])
```

```python
import functools

import jax
import jax.numpy as jnp
from jax import lax
from jax.experimental import pallas as pl
from jax.experimental.pallas import tpu as pltpu

F32 = jnp.float32
BF16 = jnp.bfloat16
HIGHEST = lax.Precision.HIGHEST

EPS = 1e-6
LRU_C = 8.0
LOG2E = 1.4426950408889634
HEAD_DIM = 128
CHUNK = 64
SUB = 16
UNROLL = 4
ROUTER_LANES = 128
TOK_TILE = 512
SEG = 16
SORT_ROWS = 640
ROW_TILE = 512
VMEM_LIMIT_BYTES = 56 * 1024 * 1024

NT_DIMS = (((1,), (1,)), ((), ()))
TN_DIMS = (((0,), (0,)), ((), ()))


def _rms(x, g):
    return x * lax.rsqrt(jnp.mean(x * x, axis=-1, keepdims=True) + EPS) * g


def _sigmoid(x):
    return 1.0 / (1.0 + jnp.exp(-x))


def _silu(x):
    return x * _sigmoid(x)


def _gelu_tanh(x):
    c = 0.7978845608028654
    return x * (0.5 * (1.0 + jnp.tanh(c * (x + 0.044715 * (x * x * x)))))


def _softplus(z):
    return jnp.maximum(z, 0.0) + jnp.log1p(jnp.exp(-jnp.abs(z)))


def _expm1(x):
    u = jnp.exp(x)
    um1 = u - 1.0
    small = um1 * x / jnp.log(u)
    return jnp.where(um1 == 0.0, x, jnp.where(jnp.abs(x) < 0.5, small, um1))


def _forget_lower_bound(lbw):
    m = jnp.max(lbw, axis=0, keepdims=True)
    e = jnp.exp(lbw - m)
    return e[0:1, :] / jnp.sum(e, axis=0, keepdims=True)


def _route(logits, n_groups, per_group):
    n = logits.shape[-1]
    col = lax.broadcasted_iota(jnp.int32, logits.shape, 1)
    neg = jnp.float32(-jnp.inf)
    big = jnp.int32(n)
    is_g = col < n_groups
    lg = jnp.where(is_g, logits, neg)
    mg = jnp.max(lg, axis=-1, keepdims=True)
    g_idx = jnp.min(jnp.where(lg == mg, col, big), axis=-1, keepdims=True)
    p_top = 1.0 / jnp.sum(jnp.where(is_g, jnp.exp(logits - mg), 0.0), axis=-1, keepdims=True)
    lo = n_groups + per_group * g_idx
    le = jnp.where((col >= lo) & (col < lo + per_group), logits, neg)
    m1 = jnp.max(le, axis=-1, keepdims=True)
    i1 = jnp.min(jnp.where(le == m1, col, big), axis=-1, keepdims=True)
    le2 = jnp.where(col == i1, neg, le)
    m2 = jnp.max(le2, axis=-1, keepdims=True)
    i2 = jnp.min(jnp.where(le2 == m2, col, big), axis=-1, keepdims=True)
    e2 = jnp.exp(m2 - m1)
    den = 1.0 + e2
    w1 = p_top / den
    w2 = p_top * (e2 / den)
    gates = jnp.where(col == i1, w1, 0.0) + jnp.where(col == i2, w2, 0.0)
    return gates + jnp.where(col == 0, g_idx.astype(F32), 0.0)


def _mixer_prompt_kernel(x_ref, lbw_ref, g1_ref, win_ref, hgg_ref, cw_ref, cb_ref, wx_ref, bx_ref,
                         wa_ref, ba_ref, lam_ref, wout_ref, g2_ref, wr_ref, br_ref,
                         x1_ref, xn2_ref, gates_ref, cnt_ref, sout_ref, hout_ref, cout_ref,
                         proj_s, k_s, o_s, st_s, xr_s, a_s, u_s, hcar_s,
                         *, n_groups, per_group):
    lb_t = x_ref.shape[1]
    wa_w = k_s.shape[1]
    wb_w = a_s.shape[1]
    n_heads = wa_w // HEAD_DIM
    j = pl.program_id(1)
    nj = pl.num_programs(1)

    @pl.when(j == 0)
    def _():
        st_s[...] = jnp.zeros_like(st_s)
        hcar_s[...] = jnp.zeros_like(hcar_s)
        xr_s[0:8, :] = jnp.zeros((8, wb_w), F32)

    x = x_ref[0]
    xn = _rms(x, g1_ref[...]).astype(BF16)
    for c, w in ((wa_w, wa_w), (4 * wa_w, wb_w), (4 * wa_w + wb_w, wb_w), (0, wa_w), (2 * wa_w, wa_w),
                 (3 * wa_w, wa_w)):
        proj_s[:, c:c + w] = jnp.dot(xn, win_ref[:, c:c + w], preferred_element_type=F32)

    xb0 = 4 * wa_w
    xr_s[pl.ds(8, lb_t), :] = proj_s[:, xb0:xb0 + wb_w]
    xc = (cb_ref[...] + cw_ref[3:4, :] * xr_s[pl.ds(8, lb_t), :] + cw_ref[2:3, :] * xr_s[pl.ds(7, lb_t), :]
          + cw_ref[1:2, :] * xr_s[pl.ds(6, lb_t), :] + cw_ref[0:1, :] * xr_s[pl.ds(5, lb_t), :])
    tail = xr_s[pl.ds(lb_t + 5, 3), :]
    xr_s[5:8, :] = tail
    cout_ref[0] = tail
    xcb = xc.astype(BF16)
    gate_x = _sigmoid(jnp.dot(xcb, wx_ref[...], preferred_element_type=F32) + bx_ref[...])
    gate_a = _sigmoid(jnp.dot(xcb, wa_ref[...], preferred_element_type=F32) + ba_ref[...])
    log_a = (-LRU_C) * gate_a * _softplus(-lam_ref[...])
    a = jnp.exp(log_a)
    mult = jnp.sqrt(-_expm1(2.0 * log_a))
    first = (lax.broadcasted_iota(jnp.int32, (lb_t, 1), 0) == 0) & (j == 0)
    a = jnp.where(first, 0.0, a)
    mult = jnp.where(first, 1.0, mult)
    a_s[...] = a
    u_s[...] = gate_x * xc * mult

    lb = _forget_lower_bound(lbw_ref[...])
    f = lb + (1.0 - lb) * _sigmoid(proj_s[:, wa_w:2 * wa_w])
    k_s[...] = 1.0 - f
    logf = jnp.log(f)
    r_i = lax.broadcasted_iota(jnp.int32, (CHUNK, CHUNK), 0)
    c_i = lax.broadcasted_iota(jnp.int32, (CHUNK, CHUNK), 1)
    tri = jnp.where(r_i >= c_i, 1.0, 0.0).astype(BF16)
    lf_hi = logf.astype(BF16)
    rest = logf - lf_hi.astype(F32)
    lf_mid = rest.astype(BF16)
    lf_lo = (rest - lf_mid.astype(F32)).astype(BF16)
    for c in range(0, lb_t, CHUNK):
        cum = [jnp.dot(tri, part[c:c + CHUNK, :], preferred_element_type=F32) for part in (lf_lo, lf_mid, lf_hi)]
        proj_s[c:c + CHUNK, wa_w:2 * wa_w] = LOG2E * ((cum[0] + cum[1]) + cum[2])

    row_sub = lax.broadcasted_iota(jnp.int32, (SUB, HEAD_DIM), 0)
    lane_sub = lax.broadcasted_iota(jnp.int32, (SUB, HEAD_DIM), 1)
    assert n_heads % 2 == 0
    r_kk = lax.broadcasted_iota(jnp.int32, (2 * HEAD_DIM, 2 * HEAD_DIM), 0)
    c_kk = lax.broadcasted_iota(jnp.int32, (2 * HEAD_DIM, 2 * HEAD_DIM), 1)
    ones_kk = jnp.where((r_kk < HEAD_DIM) == (c_kk < HEAD_DIM), 1.0, 0.0).astype(BF16)

    n_sub = CHUNK // SUB
    half = SUB // 2
    lower_left = (row_sub >= half) & (lane_sub < half)
    own_lane = jnp.where(row_sub >= half, half, 0)

    def chunk_start(r0):
        first = []
        for h in range(n_heads):
            hs = slice(h * HEAD_DIM, (h + 1) * HEAD_DIM)
            q = proj_s[pl.ds(r0, CHUNK), hs]
            b = proj_s[pl.ds(r0, CHUNK), wa_w + h * HEAD_DIM:wa_w + (h + 1) * HEAD_DIM]
            v = proj_s[pl.ds(r0, CHUNK), 2 * wa_w + h * HEAD_DIM:2 * wa_w + (h + 1) * HEAD_DIM]
            k = k_s[pl.ds(r0, CHUNK), hs]
            vb = v.astype(BF16)
            st = st_s[h]
            b_last = b[CHUNK - 1:CHUNK, :]
            o = lax.dot_general((q * jnp.exp2(b)).astype(BF16), st.astype(BF16), NT_DIMS,
                                preferred_element_type=F32)
            k_end = k * jnp.exp2(b_last - b)
            st_s[h] = st * jnp.exp2(b_last) + lax.dot_general(vb, k_end.astype(BF16), TN_DIMS,
                                                               preferred_element_type=F32)
            terms, off, mid = [], [], []
            for i in range(n_sub):
                lo = i * SUB
                qi, bi, ki = q[lo:lo + SUB], b[lo:lo + SUB], k[lo:lo + SUB]
                q3, b3, k3 = (a.reshape(2, half, HEAD_DIM) for a in (qi, bi, ki))
                terms += [(q3 * (k3[:, j:j + 1, :] * jnp.exp2(b3 - b3[:, j:j + 1, :]))).reshape(SUB, HEAD_DIM)
                          .astype(BF16) for j in range(half)]
                rm = bi[half - 1:half]
                mid.append(lax.dot_general((qi * jnp.exp2(bi - rm)).astype(BF16),
                                           (ki * jnp.exp2(rm - bi)).astype(BF16), NT_DIMS,
                                           preferred_element_type=F32))
                if i > 0:
                    r = b[lo - 1:lo]
                    qt = (qi * jnp.exp2(bi - r)).astype(BF16)
                    kt = (k[:lo] * jnp.exp2(r - b[:lo])).astype(BF16)
                    off.append(lax.dot_general(qt, kt, NT_DIMS, preferred_element_type=F32))
            first.append([o, vb, off, jnp.concatenate(terms, axis=0), mid])
        for h in range(0, n_heads, 2):
            both = jnp.concatenate([first[h][3], first[h + 1][3]], axis=1)
            sums = jnp.dot(both, ones_kk, preferred_element_type=F32)
            first[h][3] = sums[:, :HEAD_DIM]
            first[h + 1][3] = sums[:, HEAD_DIM:]
        return first

    def chunk_finish(r0, first):
        for h in range(n_heads):
            o, vb, off, sums, mid = first[h]
            outs = []
            for i in range(n_sub):
                lo = i * SUB
                sc = jnp.zeros((SUB, HEAD_DIM), F32)
                for j in range(half):
                    row0 = (i * half + j) * SUB
                    sc = jnp.where(lane_sub == own_lane + j, sums[row0:row0 + SUB], sc)
                sc = jnp.where(row_sub >= lane_sub, sc, 0.0)[:, :SUB]
                sc = jnp.where(lower_left[:, :SUB], mid[i], sc)
                od = o[lo:lo + SUB] + jnp.dot(sc.astype(BF16), vb[lo:lo + SUB], preferred_element_type=F32)
                if i > 0:
                    od = od + jnp.dot(off[i - 1].astype(BF16), vb[:lo], preferred_element_type=F32)
                outs.append(od)
            o_s[pl.ds(r0, CHUNK), h * HEAD_DIM:(h + 1) * HEAD_DIM] = jnp.concatenate(outs, axis=0)

    def chunks_body(ci, carry):
        rows = [pl.multiple_of((ci * UNROLL + u) * CHUNK, CHUNK) for u in range(UNROLL)]
        started = [chunk_start(r0) for r0 in rows]
        for r0, first in zip(rows, started):
            chunk_finish(r0, first)
        return carry

    row8 = lax.broadcasted_iota(jnp.int32, (8, 1), 0)

    def scan_body(gi, carry):
        r0 = pl.multiple_of(gi * 8, 8)
        aa = a_s[pl.ds(r0, 8), :]
        uu = u_s[pl.ds(r0, 8), :]
        for s in (1, 2, 4):
            m = row8 >= s
            uu = jnp.where(m, aa * pltpu.roll(uu, s, 0) + uu, uu)
            aa = jnp.where(m, aa * pltpu.roll(aa, s, 0), aa)
        hh = aa * carry + uu
        u_s[pl.ds(r0, 8), :] = hh
        return hh[7:8, :]

    h_last = lax.fori_loop(0, lb_t // 8, scan_body, hcar_s[...])
    hcar_s[...] = h_last
    hout_ref[0] = h_last

    assert lb_t % (CHUNK * UNROLL) == 0
    lax.fori_loop(0, lb_t // (CHUNK * UNROLL), chunks_body, 0)

    ya = []
    for h in range(n_heads):
        hs = slice(h * HEAD_DIM, (h + 1) * HEAD_DIM)
        oh = o_s[:, hs]
        oh = oh * lax.rsqrt(jnp.mean(oh * oh, axis=-1, keepdims=True) + EPS) * hgg_ref[:, hs]
        ya.append(oh * _silu(proj_s[:, 3 * wa_w + h * HEAD_DIM:3 * wa_w + (h + 1) * HEAD_DIM]))
    yb = u_s[...] * _gelu_tanh(proj_s[:, xb0 + wb_w:xb0 + 2 * wb_w])

    y = jnp.concatenate(ya + [yb], axis=-1).astype(BF16)
    x1 = x + jnp.dot(y, wout_ref[...], preferred_element_type=F32)
    x1_ref[0] = x1
    xn2 = _rms(x1, g2_ref[...]).astype(BF16)
    xn2_ref[0] = xn2
    logits = jnp.dot(xn2, wr_ref[...], preferred_element_type=F32) + br_ref[...]
    gates = _route(logits, n_groups, per_group)
    gates_ref[0] = gates
    col_t = lax.broadcasted_iota(jnp.int32, (TOK_TILE, ROUTER_LANES), 1)
    for t in range(lb_t // TOK_TILE):
        gi = gates[t * TOK_TILE:(t + 1) * TOK_TILE, 0:1].astype(jnp.int32)
        cnt_ref[0, t:t + 1, :] = jnp.sum(jnp.where(col_t == gi, 1, 0), axis=0, keepdims=True)

    @pl.when(j == nj - 1)
    def _():
        for h in range(n_heads):
            sout_ref[0, h] = st_s[h].T


def _const_spec(shape):
    nd = len(shape)
    return pl.BlockSpec(shape, lambda *_: (0,) * nd)


def _mixer_prompt(x, p, lb_t):
    bsz, seq, d = x.shape
    wa_w = p["hgg"].shape[1]
    wb_w = p["cb"].shape[1]
    n_heads = wa_w // HEAD_DIM
    n_cols = p["win_bf"].shape[1]
    weights = [p["lbw"], p["g1"], p["win_bf"], p["hgg"], p["cw"], p["cb"], p["wx_bf"], p["bx"],
               p["wa_bf"], p["ba"], p["lam"], p["wout_bf"], p["g2"], p["wr"].astype(BF16), p["br"]]
    nj = seq // lb_t
    tile = lambda w: pl.BlockSpec((1, lb_t, w), lambda b, j: (b, j, 0))
    out_shape = (
        jax.ShapeDtypeStruct((bsz, seq, d), F32),
        jax.ShapeDtypeStruct((bsz, seq, d), BF16),
        jax.ShapeDtypeStruct((bsz, seq, ROUTER_LANES), F32),
        jax.ShapeDtypeStruct((bsz * (seq // lb_t), lb_t // TOK_TILE, ROUTER_LANES), jnp.int32),
        jax.ShapeDtypeStruct((bsz, n_heads, HEAD_DIM, HEAD_DIM), F32),
        jax.ShapeDtypeStruct((bsz, 1, wb_w), F32),
        jax.ShapeDtypeStruct((bsz, 3, wb_w), F32),
    )
    out_specs = (
        tile(d), tile(d), tile(ROUTER_LANES),
        pl.BlockSpec((1, lb_t // TOK_TILE, ROUTER_LANES), lambda b, j: (b * nj + j, 0, 0)),
        pl.BlockSpec((1, n_heads, HEAD_DIM, HEAD_DIM), lambda b, j: (b, 0, 0, 0)),
        pl.BlockSpec((1, 1, wb_w), lambda b, j: (b, 0, 0)),
        pl.BlockSpec((1, 3, wb_w), lambda b, j: (b, 0, 0)),
    )
    scratch = [
        pltpu.VMEM((lb_t, n_cols), F32),
        pltpu.VMEM((lb_t, wa_w), F32),
        pltpu.VMEM((lb_t, wa_w), F32),
        pltpu.VMEM((n_heads, HEAD_DIM, HEAD_DIM), F32),
        pltpu.VMEM((lb_t + 8, wb_w), F32),
        pltpu.VMEM((lb_t, wb_w), F32),
        pltpu.VMEM((lb_t, wb_w), F32),
        pltpu.VMEM((1, wb_w), F32),
    ]
    kern = functools.partial(_mixer_prompt_kernel, n_groups=p["n_groups"], per_group=p["per_group"])
    return pl.pallas_call(
        kern,
        grid=(bsz, nj),
        in_specs=[tile(d)] + [_const_spec(w.shape) for w in weights],
        out_specs=out_specs,
        out_shape=out_shape,
        scratch_shapes=scratch,
        compiler_params=pltpu.CompilerParams(dimension_semantics=("arbitrary", "arbitrary"),
                                             vmem_limit_bytes=VMEM_LIMIT_BYTES),
        name="mixer_prompt",
    )(x, *weights)


def _moe_kernel(x1_ref, xn2_ref, gates_ref, w1_ref, w3_ref, w2_ref, gf_ref, y_ref, acc_s, *, n_groups):
    e = pl.program_id(1)

    @pl.when(e == 0)
    def _():
        acc_s[...] = jnp.zeros_like(acc_s)

    xb = xn2_ref[...]
    per_step = w1_ref.shape[0]
    gates = pltpu.roll(gates_ref[...], ROUTER_LANES - n_groups - per_step * e, 1)
    acc = acc_s[...]
    for s in range(per_step):
        h = (_silu(jnp.dot(xb, w1_ref[s], preferred_element_type=F32))
             * jnp.dot(xb, w3_ref[s], preferred_element_type=F32) * gates[:, s:s + 1])
        acc = acc + jnp.dot(h.astype(BF16), w2_ref[s], preferred_element_type=F32)
    acc_s[...] = acc

    @pl.when(e == pl.num_programs(1) - 1)
    def _():
        y_ref[...] = _rms(x1_ref[...] + acc_s[...], gf_ref[...])


def _moe_dense(x1, xn2, gates, p, tm):
    t, d = x1.shape
    n_exp, _, d_exp = p["w1_bf"].shape
    per_step = p["per_group"]
    row = lambda w: pl.BlockSpec((tm, w), lambda i, e: (i, 0))
    return pl.pallas_call(
        functools.partial(_moe_kernel, n_groups=p["n_groups"]),
        grid=(t // tm, n_exp // per_step),
        in_specs=[row(d), row(d), row(ROUTER_LANES),
                  pl.BlockSpec((per_step, d, d_exp), lambda i, e: (e, 0, 0)),
                  pl.BlockSpec((per_step, d, d_exp), lambda i, e: (e, 0, 0)),
                  pl.BlockSpec((per_step, d_exp, d), lambda i, e: (e, 0, 0)),
                  _const_spec(p["gf"].shape)],
        out_specs=row(d),
        out_shape=jax.ShapeDtypeStruct((t, d), F32),
        scratch_shapes=[pltpu.VMEM((tm, d), F32)],
        compiler_params=pltpu.CompilerParams(dimension_semantics=("arbitrary", "arbitrary"),
                                             vmem_limit_bytes=VMEM_LIMIT_BYTES),
        name="moe_dense",
    )(x1, xn2, gates, p["w1_bf"], p["w3_bf"], p["w2_bf"], p["gf"])


def _seg_pad(n):
    return jnp.bitwise_and(n + (SEG - 1), -SEG)


def _row_tile_pad(n):
    return jnp.bitwise_and(n + (ROW_TILE - 1), -ROW_TILE)


def _tile_segments(cnt_ref, tile, n_groups):
    padded = [_seg_pad(cnt_ref[tile * n_groups + g]) for g in range(n_groups)]
    starts, acc = [], jnp.int32(0)
    for g in range(n_groups):
        starts.append(acc)
        acc = acc + padded[g]
    return padded, starts, acc


def _group_bases(cnt_ref, n_tiles, n_groups):
    def body(t, tot):
        return tuple(tot[g] + _seg_pad(cnt_ref[t * n_groups + g]) for g in range(n_groups))
    totals = lax.fori_loop(0, n_tiles, body, (jnp.int32(0),) * n_groups)
    bases, ends, acc = [], [], jnp.int32(0)
    for g in range(n_groups):
        bases.append(acc)
        acc = acc + _row_tile_pad(totals[g])
        ends.append(acc)
    return bases, ends


def _chunk_row(c, padded, starts, gstart):
    lo = c * SEG
    row = jnp.int32(0)
    for g in range(len(padded)):
        inside = (lo >= starts[g]) & (lo < starts[g] + padded[g])
        row = row + jnp.where(inside, gstart[g] + lo - starts[g], 0)
    return pl.multiple_of(row, SEG)


def _sort_matrix(gates, starts):
    n_groups = len(starts)
    col = lax.broadcasted_iota(jnp.int32, (TOK_TILE, ROUTER_LANES), 1)
    gi = gates[:, 0:1].astype(jnp.int32)
    onehot = col == gi
    r_i = lax.broadcasted_iota(jnp.int32, (TOK_TILE, TOK_TILE), 0)
    c_i = lax.broadcasted_iota(jnp.int32, (TOK_TILE, TOK_TILE), 1)
    earlier = jnp.where(r_i > c_i, 1.0, 0.0).astype(BF16)
    before = jnp.dot(earlier, jnp.where(onehot, 1.0, 0.0).astype(BF16), preferred_element_type=F32)
    rank = jnp.sum(jnp.where(onehot, before, 0.0), axis=-1, keepdims=True).astype(jnp.int32)
    base = jnp.zeros((TOK_TILE, 1), jnp.int32)
    for g in range(n_groups):
        base = base + jnp.where(gi == g, starts[g], 0)
    lane = lax.broadcasted_iota(jnp.int32, (TOK_TILE, SORT_ROWS), 1)
    return lane == base + rank


def _sort_matrix_rows(gates, starts):
    n_groups = len(starts)
    col = lax.broadcasted_iota(jnp.int32, (TOK_TILE, ROUTER_LANES), 1)
    onehot_t = jnp.where(col == gates[:, 0:1].astype(jnp.int32), 1.0, 0.0).T
    r_i = lax.broadcasted_iota(jnp.int32, (TOK_TILE, TOK_TILE), 0)
    c_i = lax.broadcasted_iota(jnp.int32, (TOK_TILE, TOK_TILE), 1)
    later = jnp.where(r_i < c_i, 1.0, 0.0).astype(BF16)
    before = jnp.dot(onehot_t.astype(BF16), later, preferred_element_type=F32)
    g_row = lax.broadcasted_iota(jnp.int32, (ROUTER_LANES, 1), 0)
    base = jnp.zeros((ROUTER_LANES, 1), jnp.int32)
    for g in range(n_groups):
        base = base + jnp.where(g_row == g, starts[g], 0)
    dest = jnp.sum(jnp.where(onehot_t > 0.0, before + base.astype(F32), 0.0), axis=0, keepdims=True)
    row = lax.broadcasted_iota(jnp.int32, (SORT_ROWS, TOK_TILE), 0)
    return row == dest.astype(jnp.int32)


def _dispatch_copies(xbuf, gbuf, xs_hbm, gs_hbm, sem, slot, c, row):
    return (pltpu.make_async_copy(xbuf.at[slot, pl.ds(c * SEG, SEG), :], xs_hbm.at[pl.ds(row, SEG), :], sem.at[slot]),
            pltpu.make_async_copy(gbuf.at[slot, pl.ds(c * SEG, SEG), :], gs_hbm.at[pl.ds(row, SEG), :], sem.at[slot]))


def _moe_dispatch_kernel(cnt_ref, xn2_ref, gates_ref, xs_in, gs_in, xs_hbm, gs_hbm, tg_ref,
                         xbuf, gbuf, sem, gstart_s, *, n_groups, per_group):
    del xs_in, gs_in
    j = pl.program_id(0)
    n_tiles = pl.num_programs(0)
    n_chunks = (TOK_TILE + n_groups * SEG) // SEG
    slot = lax.rem(j, 2)

    @pl.when(j == 0)
    def _():
        bases, ends = _group_bases(cnt_ref, n_tiles, n_groups)
        for g in range(n_groups):
            gstart_s[g] = bases[g]
        n_rt = tg_ref.shape[0] - 1
        for i in range(n_rt):
            tg = jnp.int32(0)
            for g in range(n_groups):
                tg = tg + jnp.where(ends[g] <= i * ROW_TILE, 1, 0)
            tg_ref[i] = tg
        tg_ref[n_rt] = ends[n_groups - 1] // ROW_TILE

    def wait_tile(tile, slot_):
        _, _, used = _tile_segments(cnt_ref, tile, n_groups)
        for c in range(n_chunks):
            @pl.when(c * SEG < used)
            def _():
                for cp in _dispatch_copies(xbuf, gbuf, xs_hbm, gs_hbm, sem, slot_, c, 0):
                    cp.wait()

    @pl.when(j >= 2)
    def _():
        wait_tile(j - 2, slot)

    padded, starts, used = _tile_segments(cnt_ref, j, n_groups)
    gates = gates_ref[...]
    d = xn2_ref.shape[1]
    g_hi = gates.astype(BF16)
    g_lo = (gates - g_hi.astype(F32)).astype(BF16)
    sort_m = jnp.where(_sort_matrix_rows(gates, starts), 1.0, 0.0).astype(BF16)
    moved = jnp.dot(sort_m, jnp.concatenate([xn2_ref[...], g_hi, g_lo], axis=1), preferred_element_type=F32)
    xbuf[slot] = moved[:, :d].astype(BF16)
    gbuf[slot] = moved[:, d:d + ROUTER_LANES] + moved[:, d + ROUTER_LANES:]
    gstart = [gstart_s[g] for g in range(n_groups)]
    for c in range(n_chunks):
        @pl.when(c * SEG < used)
        def _():
            for cp in _dispatch_copies(xbuf, gbuf, xs_hbm, gs_hbm, sem, slot, c,
                                       _chunk_row(c, padded, starts, gstart)):
                cp.start()
    for g in range(n_groups):
        gstart_s[g] = gstart[g] + padded[g]

    @pl.when(j == n_tiles - 1)
    def _():
        @pl.when(j >= 1)
        def _():
            wait_tile(j - 1, 1 - slot)
        wait_tile(j, slot)


def _moe_group_kernel(tg_ref, xs_ref, gs_ref, w1_ref, w3_ref, w2_ref, ys_ref, *, n_groups, per_group):
    i = pl.program_id(0)

    @pl.when(tg_ref[i] < n_groups)
    def _():
        xb = xs_ref[...]
        gates = pltpu.roll(gs_ref[...], ROUTER_LANES - n_groups - per_group * tg_ref[i], 1)
        acc = jnp.zeros(ys_ref.shape, F32)
        for s in range(per_group):
            h = (_silu(jnp.dot(xb, w1_ref[s], preferred_element_type=F32))
                 * jnp.dot(xb, w3_ref[s], preferred_element_type=F32) * gates[:, s:s + 1])
            acc = acc + jnp.dot(h.astype(BF16), w2_ref[s], preferred_element_type=F32)
        ys_ref[...] = acc.astype(BF16)


def _moe_combine_kernel(cnt_ref, gates_ref, x1_ref, gf_ref, ys_hbm, y_ref, ybuf, sem, gstart_s, *, n_groups):
    j = pl.program_id(0)
    n_tiles = pl.num_programs(0)
    n_chunks = SORT_ROWS // SEG
    slot = lax.rem(j, 2)

    def copies(slot_, c, row):
        return pltpu.make_async_copy(ys_hbm.at[pl.ds(row, SEG), :], ybuf.at[slot_, pl.ds(c * SEG, SEG), :],
                                     sem.at[slot_])

    def fetch_tile(tile, slot_):
        padded, starts, _ = _tile_segments(cnt_ref, tile, n_groups)
        gstart = [gstart_s[g] for g in range(n_groups)]
        for c in range(n_chunks):
            copies(slot_, c, _chunk_row(c, padded, starts, gstart)).start()
        for g in range(n_groups):
            gstart_s[g] = gstart[g] + padded[g]

    @pl.when(j == 0)
    def _():
        bases, _ = _group_bases(cnt_ref, n_tiles, n_groups)
        for g in range(n_groups):
            gstart_s[g] = bases[g]
        fetch_tile(0, 0)

    @pl.when(j + 1 < n_tiles)
    def _():
        fetch_tile(j + 1, 1 - slot)

    for c in range(n_chunks):
        copies(slot, c, 0).wait()
    _, starts, _ = _tile_segments(cnt_ref, j, n_groups)
    sort_t = _sort_matrix(gates_ref[...], starts)
    moe = jnp.dot(jnp.where(sort_t, 1.0, 0.0).astype(BF16), ybuf[slot], preferred_element_type=F32)
    y_ref[...] = _rms(x1_ref[...] + moe, gf_ref[...])


def _moe_sorted(x1, xn2, gates, counts, p):
    t, d = x1.shape
    n_groups, per_group = p["n_groups"], p["per_group"]
    n_tiles = t // TOK_TILE
    d_exp = p["w1_bf"].shape[2]
    cap = t + n_tiles * n_groups * SEG + n_groups * ROW_TILE
    n_rt = -(-cap // ROW_TILE)
    rows = n_rt * ROW_TILE
    cnt = counts[:, :, :n_groups].reshape(-1)
    params = pltpu.CompilerParams(dimension_semantics=("arbitrary",), vmem_limit_bytes=VMEM_LIMIT_BYTES)
    any_spec = pl.BlockSpec(memory_space=pl.ANY)
    tok = lambda w: pl.BlockSpec((TOK_TILE, w), lambda j, c: (j, 0))

    xs, gs, tile_group = pl.pallas_call(
        functools.partial(_moe_dispatch_kernel, n_groups=n_groups, per_group=per_group),
        grid_spec=pltpu.PrefetchScalarGridSpec(
            num_scalar_prefetch=1, grid=(n_tiles,),
            in_specs=[tok(d), tok(ROUTER_LANES), any_spec, any_spec],
            out_specs=(any_spec, any_spec, pl.BlockSpec(memory_space=pltpu.SMEM)),
            scratch_shapes=[pltpu.VMEM((2, SORT_ROWS, d), BF16), pltpu.VMEM((2, SORT_ROWS, ROUTER_LANES), F32),
                            pltpu.SemaphoreType.DMA((2,)), pltpu.SMEM((n_groups,), jnp.int32)]),
        out_shape=(jax.ShapeDtypeStruct((rows, d), BF16), jax.ShapeDtypeStruct((rows, ROUTER_LANES), F32),
                   jax.ShapeDtypeStruct((n_rt + 1,), jnp.int32)),
        input_output_aliases={3: 0, 4: 1},
        compiler_params=params, name="moe_dispatch",
    )(cnt, xn2, gates, jnp.zeros((rows, d), BF16), jnp.zeros((rows, ROUTER_LANES), F32))

    used_tile = lambda i, tg: (jnp.minimum(i, tg[n_rt] - 1), 0)
    group_w = lambda i, tg: (jnp.minimum(tg[i], n_groups - 1), 0, 0)
    ys = pl.pallas_call(
        functools.partial(_moe_group_kernel, n_groups=n_groups, per_group=per_group),
        grid_spec=pltpu.PrefetchScalarGridSpec(
            num_scalar_prefetch=1, grid=(n_rt,),
            in_specs=[pl.BlockSpec((ROW_TILE, d), used_tile), pl.BlockSpec((ROW_TILE, ROUTER_LANES), used_tile),
                      pl.BlockSpec((per_group, d, d_exp), group_w), pl.BlockSpec((per_group, d, d_exp), group_w),
                      pl.BlockSpec((per_group, d_exp, d), group_w)],
            out_specs=pl.BlockSpec((ROW_TILE, d), used_tile)),
        out_shape=jax.ShapeDtypeStruct((rows, d), BF16),
        input_output_aliases={1: 0},
        compiler_params=params, name="moe_experts",
    )(tile_group, xs, gs, p["w1_bf"], p["w3_bf"], p["w2_bf"])

    return pl.pallas_call(
        functools.partial(_moe_combine_kernel, n_groups=n_groups),
        grid_spec=pltpu.PrefetchScalarGridSpec(
            num_scalar_prefetch=1, grid=(n_tiles,),
            in_specs=[tok(ROUTER_LANES), tok(d), pl.BlockSpec(p["gf"].shape, lambda j, c: (0, 0)), any_spec],
            out_specs=tok(d),
            scratch_shapes=[pltpu.VMEM((2, SORT_ROWS, d), BF16), pltpu.SemaphoreType.DMA((2,)),
                            pltpu.SMEM((n_groups,), jnp.int32)]),
        out_shape=jax.ShapeDtypeStruct((t, d), F32),
        compiler_params=params, name="moe_combine",
    )(cnt, gates, x1, p["gf"], ys)


def _sample_in_kernel(x_ref, c0_ref, c1_ref, c2_ref, h0_ref, lbw_ref, g1_ref, win_ref, cw_ref, cb_ref,
                      wx_ref, bx_ref, wa_ref, ba_ref, lam_ref,
                      q_ref, f_ref, k_ref, v_ref, g_ref, yb_ref, hnew_ref, xr_ref):
    wa_w = v_ref.shape[1]
    wb_w = yb_ref.shape[1]
    xn = _rms(x_ref[...], g1_ref[...])
    proj = jnp.dot(xn, win_ref[...], precision=HIGHEST, preferred_element_type=F32)
    lb = _forget_lower_bound(lbw_ref[...])
    f = lb + (1.0 - lb) * _sigmoid(proj[:, wa_w:2 * wa_w])
    q_ref[...] = proj[:, 0:wa_w].T
    f_ref[...] = f.T
    k_ref[...] = (1.0 - f).T
    v_ref[...] = proj[:, 2 * wa_w:3 * wa_w]
    g_ref[...] = proj[:, 3 * wa_w:4 * wa_w]
    xr = proj[:, 4 * wa_w:4 * wa_w + wb_w]
    xr_ref[...] = xr
    xc = (cb_ref[...] + cw_ref[0:1, :] * c0_ref[...] + cw_ref[1:2, :] * c1_ref[...]
          + cw_ref[2:3, :] * c2_ref[...] + cw_ref[3:4, :] * xr)
    gate_x = _sigmoid(jnp.dot(xc, wx_ref[...], precision=HIGHEST, preferred_element_type=F32) + bx_ref[...])
    gate_a = _sigmoid(jnp.dot(xc, wa_ref[...], precision=HIGHEST, preferred_element_type=F32) + ba_ref[...])
    log_a = (-LRU_C) * gate_a * _softplus(-lam_ref[...])
    a = jnp.exp(log_a)
    mult = jnp.sqrt(-_expm1(2.0 * log_a))
    h = a * h0_ref[...] + gate_x * xc * mult
    hnew_ref[...] = h
    yb_ref[...] = h * _gelu_tanh(proj[:, 4 * wa_w + wb_w:4 * wa_w + 2 * wb_w])


def _sample_state_kernel(s_ref, qt_ref, ft_ref, kt_ref, v_ref, snew_ref, o_ref):
    tb, n_heads = s_ref.shape[0], s_ref.shape[1]
    n_tok = qt_ref.shape[1]
    shift = lax.rem(n_tok - pl.program_id(0) * tb, n_tok)
    sq = (HEAD_DIM, HEAD_DIM)
    for h in range(n_heads):
        hs = slice(h * HEAD_DIM, (h + 1) * HEAD_DIM)
        qh = pltpu.roll(qt_ref[hs, :], shift, 1)
        fh = pltpu.roll(ft_ref[hs, :], shift, 1)
        kh = pltpu.roll(kt_ref[hs, :], shift, 1)
        rows = []
        for t in range(tb):
            s_new = (jnp.broadcast_to(fh[:, t:t + 1], sq) * s_ref[t, h]
                     + jnp.broadcast_to(kh[:, t:t + 1], sq) * v_ref[t:t + 1, hs])
            snew_ref[t, h] = s_new
            rows.append(jnp.sum(jnp.broadcast_to(qh[:, t:t + 1], sq) * s_new, axis=0, keepdims=True))
        o_ref[:, hs] = jnp.concatenate(rows, axis=0)


def _sample_out_kernel(x_ref, o_ref, g_ref, yb_ref, hgg_ref, wout_ref, g2_ref, wr_ref, br_ref,
                       x1_ref, xn2_ref, gates_ref, *, n_groups, per_group):
    wa_w = o_ref.shape[1]
    ya = []
    for h in range(wa_w // HEAD_DIM):
        hs = slice(h * HEAD_DIM, (h + 1) * HEAD_DIM)
        oh = o_ref[:, hs]
        oh = oh * lax.rsqrt(jnp.mean(oh * oh, axis=-1, keepdims=True) + EPS) * hgg_ref[:, hs]
        ya.append(oh * _silu(g_ref[:, hs]))
    y = jnp.concatenate(ya + [yb_ref[...]], axis=-1)
    x1 = x_ref[...] + jnp.dot(y, wout_ref[...], precision=HIGHEST, preferred_element_type=F32)
    x1_ref[...] = x1
    xn2 = _rms(x1, g2_ref[...])
    xn2_ref[...] = xn2.astype(BF16)
    logits = jnp.dot(xn2, wr_ref[...], precision=HIGHEST, preferred_element_type=F32) + br_ref[...]
    gates_ref[...] = _route(logits, n_groups, per_group)


def _whole(kernel, out_shape, *args, name):
    return pl.pallas_call(
        kernel, out_shape=out_shape,
        compiler_params=pltpu.CompilerParams(vmem_limit_bytes=VMEM_LIMIT_BYTES), name=name)(*args)


def _mixer_sample(x, s0, h0, c0, p, tb):
    n, d = x.shape
    wa_w = p["hgg"].shape[1]
    wb_w = p["cb"].shape[1]
    n_heads = wa_w // HEAD_DIM
    sd = lambda w: jax.ShapeDtypeStruct((n, w), F32)
    key_major = jax.ShapeDtypeStruct((wa_w, n), F32)
    q, f, k, v, g, yb, h_new, xr = _whole(
        _sample_in_kernel, (key_major,) * 3 + (sd(wa_w),) * 2 + (sd(wb_w),) * 3,
        x, c0[:, 0, :], c0[:, 1, :], c0[:, 2, :], h0, p["lbw"], p["g1"], p["win"], p["cw"], p["cb"],
        p["wx"], p["bx"], p["wa"], p["ba"], p["lam"], name="sample_in")
    tok = lambda w: pl.BlockSpec((tb, w), lambda i: (i, 0))
    st = pl.BlockSpec((tb, n_heads, HEAD_DIM, HEAD_DIM), lambda i: (i, 0, 0, 0))
    s_new, o = pl.pallas_call(
        _sample_state_kernel,
        grid=(n // tb,),
        in_specs=[st, _const_spec((wa_w, n)), _const_spec((wa_w, n)), _const_spec((wa_w, n)), tok(wa_w)],
        out_specs=(st, tok(wa_w)),
        out_shape=(jax.ShapeDtypeStruct(s0.shape, F32), sd(wa_w)),
        compiler_params=pltpu.CompilerParams(dimension_semantics=("arbitrary",),
                                             vmem_limit_bytes=VMEM_LIMIT_BYTES),
        name="sample_state",
    )(s0, q, f, k, v)
    x1, xn2, gates = _whole(
        functools.partial(_sample_out_kernel, n_groups=p["n_groups"], per_group=p["per_group"]),
        (sd(d), jax.ShapeDtypeStruct((n, d), BF16), sd(ROUTER_LANES)),
        x, o, g, yb, p["hgg"], p["wout"], p["g2"], p["wr"], p["br"], name="sample_out")
    c_new = jnp.stack([c0[:, 1, :], c0[:, 2, :], xr], axis=1)
    return x1, xn2, gates, s_new, h_new, c_new


def _block_diag(w):
    n, c, _ = w.shape
    eye = jnp.eye(n, dtype=w.dtype)
    return (w[:, :, None, :] * eye[:, None, :, None]).reshape(n * c, n * c)


def _prepare(lower_bounds, ln1_g, w_in, hgrn_norm_g, conv_w, conv_b, lru_wx, lru_bx, lru_wa, lru_ba,
             lru_lambda, w_out, ln2_g, router_wg, router_bg, router_we, router_be, exp_w1, exp_w3,
             exp_w2, final_g):
    d = w_in.shape[1]
    n_groups = router_wg.shape[-1]
    per_group = router_we.shape[-1]
    row = lambda a: a.reshape(1, -1).astype(F32)
    we = jnp.transpose(router_we[0], (1, 0, 2)).reshape(d, n_groups * per_group)
    pad = ROUTER_LANES - n_groups - n_groups * per_group
    wr = jnp.concatenate([router_wg[0], we, jnp.zeros((d, pad), F32)], axis=1)
    br = jnp.concatenate([router_bg[0], router_be[0].reshape(-1), jnp.zeros((pad,), F32)]).reshape(1, -1)
    wx = _block_diag(lru_wx[0])
    wa = _block_diag(lru_wa[0])
    return dict(
        n_groups=n_groups, per_group=per_group,
        lbw=lower_bounds.astype(F32), g1=row(ln1_g[0]), win=w_in[0], win_bf=w_in[0].astype(BF16),
        hgg=row(hgrn_norm_g[0]), cw=conv_w[0], cb=row(conv_b[0]),
        wx=wx, wx_bf=wx.astype(BF16), bx=row(lru_bx[0]), wa=wa, wa_bf=wa.astype(BF16), ba=row(lru_ba[0]),
        lam=row(lru_lambda[0]), wout=w_out[0], wout_bf=w_out[0].astype(BF16), g2=row(ln2_g[0]),
        wr=wr, br=br, w1_bf=exp_w1[0].astype(BF16), w3_bf=exp_w3[0].astype(BF16),
        w2_bf=exp_w2[0].astype(BF16), gf=row(final_g))


def kernel(x_prompt, x_sample, state_hgrn, state_rglru, state_conv, lower_bounds, ln1_g, w_in, hgrn_norm_g, conv_w, conv_b, lru_wx, lru_bx, lru_wa, lru_ba, lru_lambda, w_out, ln2_g, router_wg, router_bg, router_we, router_be, exp_w1, exp_w3, exp_w2, final_g):
    assert w_in.shape[0] == 1, "single-layer trunk"
    p = _prepare(lower_bounds, ln1_g, w_in, hgrn_norm_g, conv_w, conv_b, lru_wx, lru_bx, lru_wa, lru_ba,
                 lru_lambda, w_out, ln2_g, router_wg, router_bg, router_we, router_be, exp_w1, exp_w3,
                 exp_w2, final_g)
    bsz, seq, d = x_prompt.shape
    lb_t = min(512, seq)
    x1, xn2, gates, counts, s_p, h_p, c_p = _mixer_prompt(x_prompt, p, lb_t)
    t = bsz * seq
    y_p = _moe_sorted(x1.reshape(t, d), xn2.reshape(t, d), gates.reshape(t, ROUTER_LANES), counts, p)

    n = x_sample.shape[0]
    x1s, xn2s, gates_s, s_s, h_s, c_s = _mixer_sample(x_sample[:, 0, :], state_hgrn[0], state_rglru[0],
                                                      state_conv[0], p, 8)
    y_s = _moe_dense(x1s, xn2s, gates_s, p, n)
    return (y_p.reshape(bsz, seq, d), y_s.reshape(n, 1, d),
            s_p[None], h_p.reshape(1, bsz, -1), c_p[None],
            s_s[None], h_s[None], c_s[None])
```

```python
import functools

import jax
import jax.numpy as jnp
from jax import lax
from jax.experimental import pallas as pl
from jax.experimental.pallas import tpu as pltpu

F32 = jnp.float32
BF16 = jnp.bfloat16
HIGHEST = lax.Precision.HIGHEST

EPS = 1e-6
LRU_C = 8.0
LOG2E = 1.4426950408889634
HEAD_DIM = 128
CHUNK = 64
SUB = 16
UNROLL = 4
ROUTER_LANES = 128
TOK_TILE = 512
SEG = 16
MOVE = 64
MAX_GROUPS = 4
SORT_ROWS = TOK_TILE + MAX_GROUPS * MOVE
ROW_TILE = 512
VMEM_LIMIT_BYTES = 56 * 1024 * 1024

NT_DIMS = (((1,), (1,)), ((), ()))
TN_DIMS = (((0,), (0,)), ((), ()))


def _rms(x, g):
    return x * lax.rsqrt(jnp.mean(x * x, axis=-1, keepdims=True) + EPS) * g


def _sigmoid(x):
    return 1.0 / (1.0 + jnp.exp(-x))


def _silu(x):
    return x * _sigmoid(x)


def _gelu_tanh(x):
    c = 0.7978845608028654
    return x * (0.5 * (1.0 + jnp.tanh(c * (x + 0.044715 * (x * x * x)))))


def _softplus(z):
    return jnp.maximum(z, 0.0) + jnp.log1p(jnp.exp(-jnp.abs(z)))


def _expm1(x):
    u = jnp.exp(x)
    um1 = u - 1.0
    small = um1 * x / jnp.log(u)
    return jnp.where(um1 == 0.0, x, jnp.where(jnp.abs(x) < 0.5, small, um1))


def _forget_lower_bound(lbw):
    m = jnp.max(lbw, axis=0, keepdims=True)
    e = jnp.exp(lbw - m)
    return e[0:1, :] / jnp.sum(e, axis=0, keepdims=True)


def _route(logits, n_groups, per_group):
    n = logits.shape[-1]
    col = lax.broadcasted_iota(jnp.int32, logits.shape, 1)
    neg = jnp.float32(-jnp.inf)
    big = jnp.int32(n)
    is_g = col < n_groups
    lg = jnp.where(is_g, logits, neg)
    mg = jnp.max(lg, axis=-1, keepdims=True)
    g_idx = jnp.min(jnp.where(lg == mg, col, big), axis=-1, keepdims=True)
    p_top = 1.0 / jnp.sum(jnp.where(is_g, jnp.exp(logits - mg), 0.0), axis=-1, keepdims=True)
    lo = n_groups + per_group * g_idx
    le = jnp.where((col >= lo) & (col < lo + per_group), logits, neg)
    m1 = jnp.max(le, axis=-1, keepdims=True)
    i1 = jnp.min(jnp.where(le == m1, col, big), axis=-1, keepdims=True)
    le2 = jnp.where(col == i1, neg, le)
    m2 = jnp.max(le2, axis=-1, keepdims=True)
    i2 = jnp.min(jnp.where(le2 == m2, col, big), axis=-1, keepdims=True)
    e2 = jnp.exp(m2 - m1)
    den = 1.0 + e2
    w1 = p_top / den
    w2 = p_top * (e2 / den)
    gates = jnp.where(col == i1, w1, 0.0) + jnp.where(col == i2, w2, 0.0)
    return gates + jnp.where(col == 0, g_idx.astype(F32), 0.0)


def _mixer_prompt_kernel(x_ref, lbw_ref, g1_ref, win_ref, hgg_ref, cw_ref, cb_ref, wx_ref, bx_ref,
                         wa_ref, ba_ref, lam_ref, wout_ref, g2_ref, wr_ref, br_ref,
                         x1_ref, xn2_ref, gates_ref, cnt_ref, sout_ref, hout_ref, cout_ref,
                         proj_s, k_s, o_s, st_s, xr_s, a_s, u_s, hcar_s,
                         *, n_groups, per_group):
    lb_t = x_ref.shape[1]
    wa_w = k_s.shape[1]
    wb_w = a_s.shape[1]
    n_heads = wa_w // HEAD_DIM
    j = pl.program_id(1)
    nj = pl.num_programs(1)

    @pl.when(j == 0)
    def _():
        st_s[...] = jnp.zeros_like(st_s)
        hcar_s[...] = jnp.zeros_like(hcar_s)
        xr_s[0:8, :] = jnp.zeros((8, wb_w), F32)

    x = x_ref[0]
    xn = _rms(x, g1_ref[...]).astype(BF16)
    for c, w in ((wa_w, wa_w), (4 * wa_w, wb_w), (4 * wa_w + wb_w, wb_w), (0, wa_w), (2 * wa_w, wa_w),
                 (3 * wa_w, wa_w)):
        proj_s[:, c:c + w] = jnp.dot(xn, win_ref[:, c:c + w], preferred_element_type=F32)

    xb0 = 4 * wa_w
    xr_s[pl.ds(8, lb_t), :] = proj_s[:, xb0:xb0 + wb_w]
    xc = (cb_ref[...] + cw_ref[3:4, :] * xr_s[pl.ds(8, lb_t), :] + cw_ref[2:3, :] * xr_s[pl.ds(7, lb_t), :]
          + cw_ref[1:2, :] * xr_s[pl.ds(6, lb_t), :] + cw_ref[0:1, :] * xr_s[pl.ds(5, lb_t), :])
    tail = xr_s[pl.ds(lb_t + 5, 3), :]
    xr_s[5:8, :] = tail
    cout_ref[0] = tail
    xcb = xc.astype(BF16)
    gate_x = _sigmoid(jnp.dot(xcb, wx_ref[...], preferred_element_type=F32) + bx_ref[...])
    gate_a = _sigmoid(jnp.dot(xcb, wa_ref[...], preferred_element_type=F32) + ba_ref[...])
    log_a = (-LRU_C) * gate_a * _softplus(-lam_ref[...])
    a = jnp.exp(log_a)
    mult = jnp.sqrt(-_expm1(2.0 * log_a))
    first = (lax.broadcasted_iota(jnp.int32, (lb_t, 1), 0) == 0) & (j == 0)
    a = jnp.where(first, 0.0, a)
    mult = jnp.where(first, 1.0, mult)
    a_s[...] = a
    u_s[...] = gate_x * xc * mult

    lb = _forget_lower_bound(lbw_ref[...])
    f = lb + (1.0 - lb) * _sigmoid(proj_s[:, wa_w:2 * wa_w])
    k_s[...] = 1.0 - f
    logf = jnp.log(f)
    r_i = lax.broadcasted_iota(jnp.int32, (CHUNK, CHUNK), 0)
    c_i = lax.broadcasted_iota(jnp.int32, (CHUNK, CHUNK), 1)
    tri = jnp.where(r_i >= c_i, 1.0, 0.0).astype(BF16)
    lf_hi = logf.astype(BF16)
    rest = logf - lf_hi.astype(F32)
    lf_mid = rest.astype(BF16)
    lf_lo = (rest - lf_mid.astype(F32)).astype(BF16)
    for c in range(0, lb_t, CHUNK):
        cum = [jnp.dot(tri, part[c:c + CHUNK, :], preferred_element_type=F32) for part in (lf_lo, lf_mid, lf_hi)]
        proj_s[c:c + CHUNK, wa_w:2 * wa_w] = LOG2E * ((cum[0] + cum[1]) + cum[2])

    row_sub = lax.broadcasted_iota(jnp.int32, (SUB, HEAD_DIM), 0)
    lane_sub = lax.broadcasted_iota(jnp.int32, (SUB, HEAD_DIM), 1)
    assert n_heads % 2 == 0
    r_kk = lax.broadcasted_iota(jnp.int32, (2 * HEAD_DIM, 2 * HEAD_DIM), 0)
    c_kk = lax.broadcasted_iota(jnp.int32, (2 * HEAD_DIM, 2 * HEAD_DIM), 1)
    ones_kk = jnp.where((r_kk < HEAD_DIM) == (c_kk < HEAD_DIM), 1.0, 0.0).astype(BF16)

    n_sub = CHUNK // SUB
    half = SUB // 2
    lower_left = (row_sub >= half) & (lane_sub < half)
    own_lane = jnp.where(row_sub >= half, half, 0)

    def chunk_start(r0):
        first = []
        for h in range(n_heads):
            hs = slice(h * HEAD_DIM, (h + 1) * HEAD_DIM)
            q = proj_s[pl.ds(r0, CHUNK), hs]
            b = proj_s[pl.ds(r0, CHUNK), wa_w + h * HEAD_DIM:wa_w + (h + 1) * HEAD_DIM]
            v = proj_s[pl.ds(r0, CHUNK), 2 * wa_w + h * HEAD_DIM:2 * wa_w + (h + 1) * HEAD_DIM]
            k = k_s[pl.ds(r0, CHUNK), hs]
            vb = v.astype(BF16)
            st = st_s[h]
            b_last = b[CHUNK - 1:CHUNK, :]
            o = lax.dot_general((q * jnp.exp2(b)).astype(BF16), st.astype(BF16), NT_DIMS,
                                preferred_element_type=F32)
            k_end = k * jnp.exp2(b_last - b)
            st_s[h] = st * jnp.exp2(b_last) + lax.dot_general(vb, k_end.astype(BF16), TN_DIMS,
                                                               preferred_element_type=F32)
            terms, off, mid = [], [], []
            for i in range(n_sub):
                lo = i * SUB
                qi, bi, ki = q[lo:lo + SUB], b[lo:lo + SUB], k[lo:lo + SUB]
                q3, b3, k3 = (a.reshape(2, half, HEAD_DIM) for a in (qi, bi, ki))
                terms += [(q3 * (k3[:, j:j + 1, :] * jnp.exp2(b3 - b3[:, j:j + 1, :]))).reshape(SUB, HEAD_DIM)
                          .astype(BF16) for j in range(half)]
                rm = bi[half - 1:half]
                mid.append(lax.dot_general((qi * jnp.exp2(bi - rm)).astype(BF16),
                                           (ki * jnp.exp2(rm - bi)).astype(BF16), NT_DIMS,
                                           preferred_element_type=F32))
                if i > 0:
                    r = b[lo - 1:lo]
                    qt = (qi * jnp.exp2(bi - r)).astype(BF16)
                    kt = (k[:lo] * jnp.exp2(r - b[:lo])).astype(BF16)
                    off.append(lax.dot_general(qt, kt, NT_DIMS, preferred_element_type=F32))
            first.append([o, vb, off, jnp.concatenate(terms, axis=0), mid])
        for h in range(0, n_heads, 2):
            both = jnp.concatenate([first[h][3], first[h + 1][3]], axis=1)
            sums = jnp.dot(both, ones_kk, preferred_element_type=F32)
            first[h][3] = sums[:, :HEAD_DIM]
            first[h + 1][3] = sums[:, HEAD_DIM:]
        return first

    def chunk_finish(r0, first):
        for h in range(n_heads):
            o, vb, off, sums, mid = first[h]
            outs = []
            for i in range(n_sub):
                lo = i * SUB
                sc = jnp.zeros((SUB, HEAD_DIM), F32)
                for j in range(half):
                    row0 = (i * half + j) * SUB
                    sc = jnp.where(lane_sub == own_lane + j, sums[row0:row0 + SUB], sc)
                sc = jnp.where(row_sub >= lane_sub, sc, 0.0)[:, :SUB]
                sc = jnp.where(lower_left[:, :SUB], mid[i], sc)
                od = o[lo:lo + SUB] + jnp.dot(sc.astype(BF16), vb[lo:lo + SUB], preferred_element_type=F32)
                if i > 0:
                    od = od + jnp.dot(off[i - 1].astype(BF16), vb[:lo], preferred_element_type=F32)
                outs.append(od)
            o_s[pl.ds(r0, CHUNK), h * HEAD_DIM:(h + 1) * HEAD_DIM] = jnp.concatenate(outs, axis=0)

    def chunks_body(ci, carry):
        rows = [pl.multiple_of((ci * UNROLL + u) * CHUNK, CHUNK) for u in range(UNROLL)]
        started = [chunk_start(r0) for r0 in rows]
        for r0, first in zip(rows, started):
            chunk_finish(r0, first)
        return carry

    row8 = lax.broadcasted_iota(jnp.int32, (8, 1), 0)

    def scan_body(gi, carry):
        r0 = pl.multiple_of(gi * 8, 8)
        aa = a_s[pl.ds(r0, 8), :]
        uu = u_s[pl.ds(r0, 8), :]
        for s in (1, 2, 4):
            m = row8 >= s
            uu = jnp.where(m, aa * pltpu.roll(uu, s, 0) + uu, uu)
            aa = jnp.where(m, aa * pltpu.roll(aa, s, 0), aa)
        hh = aa * carry + uu
        u_s[pl.ds(r0, 8), :] = hh
        return hh[7:8, :]

    h_last = lax.fori_loop(0, lb_t // 8, scan_body, hcar_s[...])
    hcar_s[...] = h_last
    hout_ref[0] = h_last

    assert lb_t % (CHUNK * UNROLL) == 0
    lax.fori_loop(0, lb_t // (CHUNK * UNROLL), chunks_body, 0)

    ya = []
    for h in range(n_heads):
        hs = slice(h * HEAD_DIM, (h + 1) * HEAD_DIM)
        oh = o_s[:, hs]
        oh = oh * lax.rsqrt(jnp.mean(oh * oh, axis=-1, keepdims=True) + EPS) * hgg_ref[:, hs]
        ya.append(oh * _silu(proj_s[:, 3 * wa_w + h * HEAD_DIM:3 * wa_w + (h + 1) * HEAD_DIM]))
    yb = u_s[...] * _gelu_tanh(proj_s[:, xb0 + wb_w:xb0 + 2 * wb_w])

    y = jnp.concatenate(ya + [yb], axis=-1).astype(BF16)
    x1 = x + jnp.dot(y, wout_ref[...], preferred_element_type=F32)
    x1_ref[0] = x1
    xn2 = _rms(x1, g2_ref[...]).astype(BF16)
    xn2_ref[0] = xn2
    logits = jnp.dot(xn2, wr_ref[...], preferred_element_type=F32) + br_ref[...]
    gates = _route(logits, n_groups, per_group)
    gates_ref[0] = gates
    col_t = lax.broadcasted_iota(jnp.int32, (TOK_TILE, ROUTER_LANES), 1)
    for t in range(lb_t // TOK_TILE):
        gi = gates[t * TOK_TILE:(t + 1) * TOK_TILE, 0:1].astype(jnp.int32)
        cnt_ref[0, t:t + 1, :] = jnp.sum(jnp.where(col_t == gi, 1, 0), axis=0, keepdims=True)

    @pl.when(j == nj - 1)
    def _():
        for h in range(n_heads):
            sout_ref[0, h] = st_s[h].T


def _const_spec(shape):
    nd = len(shape)
    return pl.BlockSpec(shape, lambda *_: (0,) * nd)


def _mixer_prompt(x, p, lb_t):
    bsz, seq, d = x.shape
    wa_w = p["hgg"].shape[1]
    wb_w = p["cb"].shape[1]
    n_heads = wa_w // HEAD_DIM
    n_cols = p["win_bf"].shape[1]
    weights = [p["lbw"], p["g1"], p["win_bf"], p["hgg"], p["cw"], p["cb"], p["wx_bf"], p["bx"],
               p["wa_bf"], p["ba"], p["lam"], p["wout_bf"], p["g2"], p["wr"].astype(BF16), p["br"]]
    nj = seq // lb_t
    tile = lambda w: pl.BlockSpec((1, lb_t, w), lambda b, j: (b, j, 0))
    out_shape = (
        jax.ShapeDtypeStruct((bsz, seq, d), F32),
        jax.ShapeDtypeStruct((bsz, seq, d), BF16),
        jax.ShapeDtypeStruct((bsz, seq, ROUTER_LANES), F32),
        jax.ShapeDtypeStruct((bsz * (seq // lb_t), lb_t // TOK_TILE, ROUTER_LANES), jnp.int32),
        jax.ShapeDtypeStruct((bsz, n_heads, HEAD_DIM, HEAD_DIM), F32),
        jax.ShapeDtypeStruct((bsz, 1, wb_w), F32),
        jax.ShapeDtypeStruct((bsz, 3, wb_w), F32),
    )
    out_specs = (
        tile(d), tile(d), tile(ROUTER_LANES),
        pl.BlockSpec((1, lb_t // TOK_TILE, ROUTER_LANES), lambda b, j: (b * nj + j, 0, 0)),
        pl.BlockSpec((1, n_heads, HEAD_DIM, HEAD_DIM), lambda b, j: (b, 0, 0, 0)),
        pl.BlockSpec((1, 1, wb_w), lambda b, j: (b, 0, 0)),
        pl.BlockSpec((1, 3, wb_w), lambda b, j: (b, 0, 0)),
    )
    scratch = [
        pltpu.VMEM((lb_t, n_cols), F32),
        pltpu.VMEM((lb_t, wa_w), F32),
        pltpu.VMEM((lb_t, wa_w), F32),
        pltpu.VMEM((n_heads, HEAD_DIM, HEAD_DIM), F32),
        pltpu.VMEM((lb_t + 8, wb_w), F32),
        pltpu.VMEM((lb_t, wb_w), F32),
        pltpu.VMEM((lb_t, wb_w), F32),
        pltpu.VMEM((1, wb_w), F32),
    ]
    kern = functools.partial(_mixer_prompt_kernel, n_groups=p["n_groups"], per_group=p["per_group"])
    return pl.pallas_call(
        kern,
        grid=(bsz, nj),
        in_specs=[tile(d)] + [_const_spec(w.shape) for w in weights],
        out_specs=out_specs,
        out_shape=out_shape,
        scratch_shapes=scratch,
        compiler_params=pltpu.CompilerParams(dimension_semantics=("arbitrary", "arbitrary"),
                                             vmem_limit_bytes=VMEM_LIMIT_BYTES),
        name="mixer_prompt",
    )(x, *weights)


def _moe_kernel(x1_ref, xn2_ref, gates_ref, w1_ref, w3_ref, w2_ref, gf_ref, y_ref, acc_s, *, n_groups):
    e = pl.program_id(1)

    @pl.when(e == 0)
    def _():
        acc_s[...] = jnp.zeros_like(acc_s)

    xb = xn2_ref[...]
    per_step = w1_ref.shape[0]
    gates = pltpu.roll(gates_ref[...], ROUTER_LANES - n_groups - per_step * e, 1)
    acc = acc_s[...]
    for s in range(per_step):
        h = (_silu(jnp.dot(xb, w1_ref[s], preferred_element_type=F32))
             * jnp.dot(xb, w3_ref[s], preferred_element_type=F32) * gates[:, s:s + 1])
        acc = acc + jnp.dot(h.astype(BF16), w2_ref[s], preferred_element_type=F32)
    acc_s[...] = acc

    @pl.when(e == pl.num_programs(1) - 1)
    def _():
        y_ref[...] = _rms(x1_ref[...] + acc_s[...], gf_ref[...])


def _moe_dense(x1, xn2, gates, p, tm):
    t, d = x1.shape
    n_exp, _, d_exp = p["w1_bf"].shape
    per_step = p["per_group"]
    row = lambda w: pl.BlockSpec((tm, w), lambda i, e: (i, 0))
    return pl.pallas_call(
        functools.partial(_moe_kernel, n_groups=p["n_groups"]),
        grid=(t // tm, n_exp // per_step),
        in_specs=[row(d), row(d), row(ROUTER_LANES),
                  pl.BlockSpec((per_step, d, d_exp), lambda i, e: (e, 0, 0)),
                  pl.BlockSpec((per_step, d, d_exp), lambda i, e: (e, 0, 0)),
                  pl.BlockSpec((per_step, d_exp, d), lambda i, e: (e, 0, 0)),
                  _const_spec(p["gf"].shape)],
        out_specs=row(d),
        out_shape=jax.ShapeDtypeStruct((t, d), F32),
        scratch_shapes=[pltpu.VMEM((tm, d), F32)],
        compiler_params=pltpu.CompilerParams(dimension_semantics=("arbitrary", "arbitrary"),
                                             vmem_limit_bytes=VMEM_LIMIT_BYTES),
        name="moe_dense",
    )(x1, xn2, gates, p["w1_bf"], p["w3_bf"], p["w2_bf"], p["gf"])


def _round_up(n, m):
    return jnp.bitwise_and(n + (m - 1), -m)


def _tile_segments(cnt_ref, tile, n_groups):
    padded = [_round_up(cnt_ref[tile * n_groups + g], SEG) for g in range(n_groups)]
    starts, acc = [], jnp.int32(0)
    for g in range(n_groups):
        starts.append(acc)
        acc = acc + _round_up(padded[g], MOVE)
    return padded, starts


def _group_bases(cnt_ref, n_tiles, n_groups):
    def body(t, tot):
        return tuple(tot[g] + _round_up(cnt_ref[t * n_groups + g], SEG) for g in range(n_groups))
    totals = lax.fori_loop(0, n_tiles, body, (jnp.int32(0),) * n_groups)
    bases, ends, acc = [], [], jnp.int32(0)
    for g in range(n_groups):
        bases.append(acc)
        acc = acc + _round_up(totals[g] + (MOVE - SEG), ROW_TILE)
        ends.append(acc)
    return bases, ends


def _chunk_row(c, padded, starts, gstart, spare):
    lo = c * MOVE
    row = jnp.int32(0)
    hit = jnp.int32(0)
    for g in range(len(padded)):
        inside = (lo >= starts[g]) & (lo < starts[g] + padded[g])
        row = row + jnp.where(inside, gstart[g] + lo - starts[g], 0)
        hit = hit + jnp.where(inside, 1, 0)
    return pl.multiple_of(jnp.where(hit > 0, row, spare + lo), SEG)


def _sort_matrix(gates, starts):
    n_groups = len(starts)
    col = lax.broadcasted_iota(jnp.int32, (TOK_TILE, ROUTER_LANES), 1)
    gi = gates[:, 0:1].astype(jnp.int32)
    onehot = col == gi
    r_i = lax.broadcasted_iota(jnp.int32, (TOK_TILE, TOK_TILE), 0)
    c_i = lax.broadcasted_iota(jnp.int32, (TOK_TILE, TOK_TILE), 1)
    earlier = jnp.where(r_i > c_i, 1.0, 0.0).astype(BF16)
    before = jnp.dot(earlier, jnp.where(onehot, 1.0, 0.0).astype(BF16), preferred_element_type=F32)
    rank = jnp.sum(jnp.where(onehot, before, 0.0), axis=-1, keepdims=True).astype(jnp.int32)
    base = jnp.zeros((TOK_TILE, 1), jnp.int32)
    for g in range(n_groups):
        base = base + jnp.where(gi == g, starts[g], 0)
    lane = lax.broadcasted_iota(jnp.int32, (TOK_TILE, SORT_ROWS), 1)
    return lane == base + rank


def _sort_matrix_rows(gates, starts):
    n_groups = len(starts)
    col = lax.broadcasted_iota(jnp.int32, (TOK_TILE, ROUTER_LANES), 1)
    onehot_t = jnp.where(col == gates[:, 0:1].astype(jnp.int32), 1.0, 0.0).T
    r_i = lax.broadcasted_iota(jnp.int32, (TOK_TILE, TOK_TILE), 0)
    c_i = lax.broadcasted_iota(jnp.int32, (TOK_TILE, TOK_TILE), 1)
    later = jnp.where(r_i < c_i, 1.0, 0.0).astype(BF16)
    before = jnp.dot(onehot_t.astype(BF16), later, preferred_element_type=F32)
    g_row = lax.broadcasted_iota(jnp.int32, (ROUTER_LANES, 1), 0)
    base = jnp.zeros((ROUTER_LANES, 1), jnp.int32)
    for g in range(n_groups):
        base = base + jnp.where(g_row == g, starts[g], 0)
    dest = jnp.sum(jnp.where(onehot_t > 0.0, before + base.astype(F32), 0.0), axis=0, keepdims=True)
    row = lax.broadcasted_iota(jnp.int32, (SORT_ROWS, TOK_TILE), 0)
    return row == dest.astype(jnp.int32)


def _dispatch_copies(xbuf, gbuf, xs_hbm, gs_hbm, sem, slot, c, row):
    return (pltpu.make_async_copy(xbuf.at[slot, pl.ds(c * MOVE, MOVE), :], xs_hbm.at[pl.ds(row, MOVE), :],
                                  sem.at[slot]),
            pltpu.make_async_copy(gbuf.at[slot, pl.ds(c * MOVE, MOVE), :], gs_hbm.at[pl.ds(row, MOVE), :],
                                  sem.at[slot]))


def _moe_dispatch_kernel(cnt_ref, xn2_ref, gates_ref, xs_in, gs_in, xs_hbm, gs_hbm, tg_ref,
                         xbuf, gbuf, sem, gstart_s, *, n_groups, per_group, spare_row):
    del xs_in, gs_in
    assert n_groups <= MAX_GROUPS
    j = pl.program_id(0)
    n_tiles = pl.num_programs(0)
    n_chunks = SORT_ROWS // MOVE
    slot = lax.rem(j, 2)

    @pl.when(j == 0)
    def _():
        bases, ends = _group_bases(cnt_ref, n_tiles, n_groups)
        for g in range(n_groups):
            gstart_s[g] = bases[g]
        n_rt = tg_ref.shape[0] - 1
        for i in range(n_rt):
            tg = jnp.int32(0)
            for g in range(n_groups):
                tg = tg + jnp.where(ends[g] <= i * ROW_TILE, 1, 0)
            tg_ref[i] = tg
        tg_ref[n_rt] = ends[n_groups - 1] // ROW_TILE

    def wait_copies(slot_):
        for c in range(n_chunks):
            for cp in _dispatch_copies(xbuf, gbuf, xs_hbm, gs_hbm, sem, slot_, c, 0):
                cp.wait()

    padded, starts = _tile_segments(cnt_ref, j, n_groups)
    gates = gates_ref[...]
    d = xn2_ref.shape[1]
    g_hi = gates.astype(BF16)
    g_lo = (gates - g_hi.astype(F32)).astype(BF16)
    sort_m = jnp.where(_sort_matrix_rows(gates, starts), 1.0, 0.0).astype(BF16)
    moved = jnp.dot(sort_m, jnp.concatenate([xn2_ref[...], g_hi, g_lo], axis=1), preferred_element_type=F32)
    xbuf[slot] = moved[:, :d].astype(BF16)
    gbuf[slot] = moved[:, d:d + ROUTER_LANES] + moved[:, d + ROUTER_LANES:]
    @pl.when(j >= 1)
    def _():
        wait_copies(1 - slot)

    gstart = [gstart_s[g] for g in range(n_groups)]
    for c in range(n_chunks):
        row = _chunk_row(c, padded, starts, gstart, spare_row + slot * SORT_ROWS)
        for cp in _dispatch_copies(xbuf, gbuf, xs_hbm, gs_hbm, sem, slot, c, row):
            cp.start()
    for g in range(n_groups):
        gstart_s[g] = gstart[g] + padded[g]

    @pl.when(j == n_tiles - 1)
    def _():
        wait_copies(slot)


def _moe_group_kernel(tg_ref, xs_ref, gs_ref, w1_ref, w3_ref, w2_ref, ys_ref, *, n_groups, per_group):
    i = pl.program_id(0)

    @pl.when(tg_ref[i] < n_groups)
    def _():
        xb = xs_ref[...]
        gates = pltpu.roll(gs_ref[...], ROUTER_LANES - n_groups - per_group * tg_ref[i], 1)
        acc = jnp.zeros(ys_ref.shape, F32)
        for s in range(per_group):
            h = (_silu(jnp.dot(xb, w1_ref[s], preferred_element_type=F32))
                 * jnp.dot(xb, w3_ref[s], preferred_element_type=F32) * gates[:, s:s + 1])
            acc = acc + jnp.dot(h.astype(BF16), w2_ref[s], preferred_element_type=F32)
        ys_ref[...] = acc.astype(BF16)


def _moe_combine_kernel(cnt_ref, gates_ref, x1_ref, gf_ref, ys_hbm, y_ref, ybuf, sem, gstart_s, *, n_groups):
    j = pl.program_id(0)
    n_tiles = pl.num_programs(0)
    n_chunks = SORT_ROWS // MOVE
    slot = lax.rem(j, 2)

    def copies(slot_, c, row):
        return pltpu.make_async_copy(ys_hbm.at[pl.ds(row, MOVE), :], ybuf.at[slot_, pl.ds(c * MOVE, MOVE), :],
                                     sem.at[slot_])

    def fetch_tile(tile, slot_):
        padded, starts = _tile_segments(cnt_ref, tile, n_groups)
        gstart = [gstart_s[g] for g in range(n_groups)]
        for c in range(n_chunks):
            copies(slot_, c, _chunk_row(c, padded, starts, gstart, 0)).start()
        for g in range(n_groups):
            gstart_s[g] = gstart[g] + padded[g]

    @pl.when(j == 0)
    def _():
        bases, _ = _group_bases(cnt_ref, n_tiles, n_groups)
        for g in range(n_groups):
            gstart_s[g] = bases[g]
        fetch_tile(0, 0)

    @pl.when(j + 1 < n_tiles)
    def _():
        fetch_tile(j + 1, 1 - slot)

    for c in range(n_chunks):
        copies(slot, c, 0).wait()
    _, starts = _tile_segments(cnt_ref, j, n_groups)
    sort_t = _sort_matrix(gates_ref[...], starts)
    moe = jnp.dot(jnp.where(sort_t, 1.0, 0.0).astype(BF16), ybuf[slot], preferred_element_type=F32)
    y_ref[...] = _rms(x1_ref[...] + moe, gf_ref[...])


def _moe_sorted(x1, xn2, gates, counts, p):
    t, d = x1.shape
    n_groups, per_group = p["n_groups"], p["per_group"]
    n_tiles = t // TOK_TILE
    d_exp = p["w1_bf"].shape[2]
    cap = t + n_tiles * n_groups * SEG + n_groups * (ROW_TILE + MOVE)
    n_rt = -(-cap // ROW_TILE)
    spare_row = n_rt * ROW_TILE
    rows = spare_row + 2 * SORT_ROWS
    cnt = counts[:, :, :n_groups].reshape(-1)
    params = pltpu.CompilerParams(dimension_semantics=("arbitrary",), vmem_limit_bytes=VMEM_LIMIT_BYTES)
    any_spec = pl.BlockSpec(memory_space=pl.ANY)
    tok = lambda w: pl.BlockSpec((TOK_TILE, w), lambda j, c: (j, 0))

    xs, gs, tile_group = pl.pallas_call(
        functools.partial(_moe_dispatch_kernel, n_groups=n_groups, per_group=per_group, spare_row=spare_row),
        grid_spec=pltpu.PrefetchScalarGridSpec(
            num_scalar_prefetch=1, grid=(n_tiles,),
            in_specs=[tok(d), tok(ROUTER_LANES), any_spec, any_spec],
            out_specs=(any_spec, any_spec, pl.BlockSpec(memory_space=pltpu.SMEM)),
            scratch_shapes=[pltpu.VMEM((2, SORT_ROWS, d), BF16), pltpu.VMEM((2, SORT_ROWS, ROUTER_LANES), F32),
                            pltpu.SemaphoreType.DMA((2,)), pltpu.SMEM((n_groups,), jnp.int32)]),
        out_shape=(jax.ShapeDtypeStruct((rows, d), BF16), jax.ShapeDtypeStruct((rows, ROUTER_LANES), F32),
                   jax.ShapeDtypeStruct((n_rt + 1,), jnp.int32)),
        input_output_aliases={3: 0, 4: 1},
        compiler_params=params, name="moe_dispatch",
    )(cnt, xn2, gates, jnp.zeros((rows, d), BF16), jnp.zeros((rows, ROUTER_LANES), F32))

    used_tile = lambda i, tg: (jnp.minimum(i, tg[n_rt] - 1), 0)
    group_w = lambda i, tg: (jnp.minimum(tg[i], n_groups - 1), 0, 0)
    ys = pl.pallas_call(
        functools.partial(_moe_group_kernel, n_groups=n_groups, per_group=per_group),
        grid_spec=pltpu.PrefetchScalarGridSpec(
            num_scalar_prefetch=1, grid=(n_rt,),
            in_specs=[pl.BlockSpec((ROW_TILE, d), used_tile), pl.BlockSpec((ROW_TILE, ROUTER_LANES), used_tile),
                      pl.BlockSpec((per_group, d, d_exp), group_w), pl.BlockSpec((per_group, d, d_exp), group_w),
                      pl.BlockSpec((per_group, d_exp, d), group_w)],
            out_specs=pl.BlockSpec((ROW_TILE, d), used_tile)),
        out_shape=jax.ShapeDtypeStruct((rows, d), BF16),
        input_output_aliases={1: 0},
        compiler_params=params, name="moe_experts",
    )(tile_group, xs, gs, p["w1_bf"], p["w3_bf"], p["w2_bf"])

    return pl.pallas_call(
        functools.partial(_moe_combine_kernel, n_groups=n_groups),
        grid_spec=pltpu.PrefetchScalarGridSpec(
            num_scalar_prefetch=1, grid=(n_tiles,),
            in_specs=[tok(ROUTER_LANES), tok(d), pl.BlockSpec(p["gf"].shape, lambda j, c: (0, 0)), any_spec],
            out_specs=tok(d),
            scratch_shapes=[pltpu.VMEM((2, SORT_ROWS, d), BF16), pltpu.SemaphoreType.DMA((2,)),
                            pltpu.SMEM((n_groups,), jnp.int32)]),
        out_shape=jax.ShapeDtypeStruct((t, d), F32),
        compiler_params=params, name="moe_combine",
    )(cnt, gates, x1, p["gf"], ys)


def _sample_in_kernel(x_ref, c0_ref, c1_ref, c2_ref, h0_ref, lbw_ref, g1_ref, win_ref, cw_ref, cb_ref,
                      wx_ref, bx_ref, wa_ref, ba_ref, lam_ref,
                      q_ref, f_ref, k_ref, v_ref, g_ref, yb_ref, hnew_ref, xr_ref):
    wa_w = v_ref.shape[1]
    wb_w = yb_ref.shape[1]
    xn = _rms(x_ref[...], g1_ref[...])
    proj = jnp.dot(xn, win_ref[...], precision=HIGHEST, preferred_element_type=F32)
    lb = _forget_lower_bound(lbw_ref[...])
    f = lb + (1.0 - lb) * _sigmoid(proj[:, wa_w:2 * wa_w])
    q_ref[...] = proj[:, 0:wa_w].T
    f_ref[...] = f.T
    k_ref[...] = (1.0 - f).T
    v_ref[...] = proj[:, 2 * wa_w:3 * wa_w]
    g_ref[...] = proj[:, 3 * wa_w:4 * wa_w]
    xr = proj[:, 4 * wa_w:4 * wa_w + wb_w]
    xr_ref[...] = xr
    xc = (cb_ref[...] + cw_ref[0:1, :] * c0_ref[...] + cw_ref[1:2, :] * c1_ref[...]
          + cw_ref[2:3, :] * c2_ref[...] + cw_ref[3:4, :] * xr)
    gate_x = _sigmoid(jnp.dot(xc, wx_ref[...], precision=HIGHEST, preferred_element_type=F32) + bx_ref[...])
    gate_a = _sigmoid(jnp.dot(xc, wa_ref[...], precision=HIGHEST, preferred_element_type=F32) + ba_ref[...])
    log_a = (-LRU_C) * gate_a * _softplus(-lam_ref[...])
    a = jnp.exp(log_a)
    mult = jnp.sqrt(-_expm1(2.0 * log_a))
    h = a * h0_ref[...] + gate_x * xc * mult
    hnew_ref[...] = h
    yb_ref[...] = h * _gelu_tanh(proj[:, 4 * wa_w + wb_w:4 * wa_w + 2 * wb_w])


def _sample_state_kernel(s_ref, qt_ref, ft_ref, kt_ref, v_ref, snew_ref, o_ref):
    tb, n_heads = s_ref.shape[0], s_ref.shape[1]
    n_tok = qt_ref.shape[1]
    shift = lax.rem(n_tok - pl.program_id(0) * tb, n_tok)
    sq = (HEAD_DIM, HEAD_DIM)
    for h in range(n_heads):
        hs = slice(h * HEAD_DIM, (h + 1) * HEAD_DIM)
        qh = pltpu.roll(qt_ref[hs, :], shift, 1)
        fh = pltpu.roll(ft_ref[hs, :], shift, 1)
        kh = pltpu.roll(kt_ref[hs, :], shift, 1)
        rows = []
        for t in range(tb):
            s_new = (jnp.broadcast_to(fh[:, t:t + 1], sq) * s_ref[t, h]
                     + jnp.broadcast_to(kh[:, t:t + 1], sq) * v_ref[t:t + 1, hs])
            snew_ref[t, h] = s_new
            rows.append(jnp.sum(jnp.broadcast_to(qh[:, t:t + 1], sq) * s_new, axis=0, keepdims=True))
        o_ref[:, hs] = jnp.concatenate(rows, axis=0)


def _sample_out_kernel(x_ref, o_ref, g_ref, yb_ref, hgg_ref, wout_ref, g2_ref, wr_ref, br_ref,
                       x1_ref, xn2_ref, gates_ref, *, n_groups, per_group):
    wa_w = o_ref.shape[1]
    ya = []
    for h in range(wa_w // HEAD_DIM):
        hs = slice(h * HEAD_DIM, (h + 1) * HEAD_DIM)
        oh = o_ref[:, hs]
        oh = oh * lax.rsqrt(jnp.mean(oh * oh, axis=-1, keepdims=True) + EPS) * hgg_ref[:, hs]
        ya.append(oh * _silu(g_ref[:, hs]))
    y = jnp.concatenate(ya + [yb_ref[...]], axis=-1)
    x1 = x_ref[...] + jnp.dot(y, wout_ref[...], precision=HIGHEST, preferred_element_type=F32)
    x1_ref[...] = x1
    xn2 = _rms(x1, g2_ref[...])
    xn2_ref[...] = xn2.astype(BF16)
    logits = jnp.dot(xn2, wr_ref[...], precision=HIGHEST, preferred_element_type=F32) + br_ref[...]
    gates_ref[...] = _route(logits, n_groups, per_group)


def _whole(kernel, out_shape, *args, name):
    return pl.pallas_call(
        kernel, out_shape=out_shape,
        compiler_params=pltpu.CompilerParams(vmem_limit_bytes=VMEM_LIMIT_BYTES), name=name)(*args)


def _mixer_sample(x, s0, h0, c0, p, tb):
    n, d = x.shape
    wa_w = p["hgg"].shape[1]
    wb_w = p["cb"].shape[1]
    n_heads = wa_w // HEAD_DIM
    sd = lambda w: jax.ShapeDtypeStruct((n, w), F32)
    key_major = jax.ShapeDtypeStruct((wa_w, n), F32)
    q, f, k, v, g, yb, h_new, xr = _whole(
        _sample_in_kernel, (key_major,) * 3 + (sd(wa_w),) * 2 + (sd(wb_w),) * 3,
        x, c0[:, 0, :], c0[:, 1, :], c0[:, 2, :], h0, p["lbw"], p["g1"], p["win"], p["cw"], p["cb"],
        p["wx"], p["bx"], p["wa"], p["ba"], p["lam"], name="sample_in")
    tok = lambda w: pl.BlockSpec((tb, w), lambda i: (i, 0))
    st = pl.BlockSpec((tb, n_heads, HEAD_DIM, HEAD_DIM), lambda i: (i, 0, 0, 0))
    s_new, o = pl.pallas_call(
        _sample_state_kernel,
        grid=(n // tb,),
        in_specs=[st, _const_spec((wa_w, n)), _const_spec((wa_w, n)), _const_spec((wa_w, n)), tok(wa_w)],
        out_specs=(st, tok(wa_w)),
        out_shape=(jax.ShapeDtypeStruct(s0.shape, F32), sd(wa_w)),
        compiler_params=pltpu.CompilerParams(dimension_semantics=("arbitrary",),
                                             vmem_limit_bytes=VMEM_LIMIT_BYTES),
        name="sample_state",
    )(s0, q, f, k, v)
    x1, xn2, gates = _whole(
        functools.partial(_sample_out_kernel, n_groups=p["n_groups"], per_group=p["per_group"]),
        (sd(d), jax.ShapeDtypeStruct((n, d), BF16), sd(ROUTER_LANES)),
        x, o, g, yb, p["hgg"], p["wout"], p["g2"], p["wr"], p["br"], name="sample_out")
    c_new = jnp.stack([c0[:, 1, :], c0[:, 2, :], xr], axis=1)
    return x1, xn2, gates, s_new, h_new, c_new


def _block_diag(w):
    n, c, _ = w.shape
    eye = jnp.eye(n, dtype=w.dtype)
    return (w[:, :, None, :] * eye[:, None, :, None]).reshape(n * c, n * c)


def _prepare(lower_bounds, ln1_g, w_in, hgrn_norm_g, conv_w, conv_b, lru_wx, lru_bx, lru_wa, lru_ba,
             lru_lambda, w_out, ln2_g, router_wg, router_bg, router_we, router_be, exp_w1, exp_w3,
             exp_w2, final_g):
    d = w_in.shape[1]
    n_groups = router_wg.shape[-1]
    per_group = router_we.shape[-1]
    row = lambda a: a.reshape(1, -1).astype(F32)
    we = jnp.transpose(router_we[0], (1, 0, 2)).reshape(d, n_groups * per_group)
    pad = ROUTER_LANES - n_groups - n_groups * per_group
    wr = jnp.concatenate([router_wg[0], we, jnp.zeros((d, pad), F32)], axis=1)
    br = jnp.concatenate([router_bg[0], router_be[0].reshape(-1), jnp.zeros((pad,), F32)]).reshape(1, -1)
    wx = _block_diag(lru_wx[0])
    wa = _block_diag(lru_wa[0])
    return dict(
        n_groups=n_groups, per_group=per_group,
        lbw=lower_bounds.astype(F32), g1=row(ln1_g[0]), win=w_in[0], win_bf=w_in[0].astype(BF16),
        hgg=row(hgrn_norm_g[0]), cw=conv_w[0], cb=row(conv_b[0]),
        wx=wx, wx_bf=wx.astype(BF16), bx=row(lru_bx[0]), wa=wa, wa_bf=wa.astype(BF16), ba=row(lru_ba[0]),
        lam=row(lru_lambda[0]), wout=w_out[0], wout_bf=w_out[0].astype(BF16), g2=row(ln2_g[0]),
        wr=wr, br=br, w1_bf=exp_w1[0].astype(BF16), w3_bf=exp_w3[0].astype(BF16),
        w2_bf=exp_w2[0].astype(BF16), gf=row(final_g))


def kernel(x_prompt, x_sample, state_hgrn, state_rglru, state_conv, lower_bounds, ln1_g, w_in, hgrn_norm_g, conv_w, conv_b, lru_wx, lru_bx, lru_wa, lru_ba, lru_lambda, w_out, ln2_g, router_wg, router_bg, router_we, router_be, exp_w1, exp_w3, exp_w2, final_g):
    assert w_in.shape[0] == 1, "single-layer trunk"
    p = _prepare(lower_bounds, ln1_g, w_in, hgrn_norm_g, conv_w, conv_b, lru_wx, lru_bx, lru_wa, lru_ba,
                 lru_lambda, w_out, ln2_g, router_wg, router_bg, router_we, router_be, exp_w1, exp_w3,
                 exp_w2, final_g)
    bsz, seq, d = x_prompt.shape
    lb_t = min(512, seq)
    x1, xn2, gates, counts, s_p, h_p, c_p = _mixer_prompt(x_prompt, p, lb_t)
    t = bsz * seq
    y_p = _moe_sorted(x1.reshape(t, d), xn2.reshape(t, d), gates.reshape(t, ROUTER_LANES), counts, p)

    n = x_sample.shape[0]
    x1s, xn2s, gates_s, s_s, h_s, c_s = _mixer_sample(x_sample[:, 0, :], state_hgrn[0], state_rglru[0],
                                                      state_conv[0], p, 8)
    y_s = _moe_dense(x1s, xn2s, gates_s, p, n)
    return (y_p.reshape(bsz, seq, d), y_s.reshape(n, 1, d),
            s_p[None], h_p.reshape(1, bsz, -1), c_p[None],
            s_s[None], h_s[None], c_s[None])
```

```python
import functools

import jax
import jax.numpy as jnp
from jax import lax
from jax.experimental import pallas as pl
from jax.experimental.pallas import tpu as pltpu

F32 = jnp.float32
BF16 = jnp.bfloat16
HIGHEST = lax.Precision.HIGHEST

EPS = 1e-6
LRU_C = 8.0
LOG2E = 1.4426950408889634
HEAD_DIM = 128
CHUNK = 64
SUB = 16
UNROLL = 4
ROUTER_LANES = 128
TOK_TILE = 512
SEG = 16
SORT_ROWS = 640
ROW_TILE = 512
VMEM_LIMIT_BYTES = 56 * 1024 * 1024

NT_DIMS = (((1,), (1,)), ((), ()))
TN_DIMS = (((0,), (0,)), ((), ()))


def _rms(x, g):
    return x * lax.rsqrt(jnp.mean(x * x, axis=-1, keepdims=True) + EPS) * g


def _sigmoid(x):
    return 1.0 / (1.0 + jnp.exp(-x))


def _silu(x):
    return x * _sigmoid(x)


def _gelu_tanh(x):
    c = 0.7978845608028654
    return x * (0.5 * (1.0 + jnp.tanh(c * (x + 0.044715 * (x * x * x)))))


def _softplus(z):
    return jnp.maximum(z, 0.0) + jnp.log1p(jnp.exp(-jnp.abs(z)))


def _expm1(x):
    u = jnp.exp(x)
    um1 = u - 1.0
    small = um1 * x / jnp.log(u)
    return jnp.where(um1 == 0.0, x, jnp.where(jnp.abs(x) < 0.5, small, um1))


def _forget_lower_bound(lbw):
    m = jnp.max(lbw, axis=0, keepdims=True)
    e = jnp.exp(lbw - m)
    return e[0:1, :] / jnp.sum(e, axis=0, keepdims=True)


def _route(logits, n_groups, per_group):
    n = logits.shape[-1]
    col = lax.broadcasted_iota(jnp.int32, logits.shape, 1)
    neg = jnp.float32(-jnp.inf)
    big = jnp.int32(n)
    is_g = col < n_groups
    lg = jnp.where(is_g, logits, neg)
    mg = jnp.max(lg, axis=-1, keepdims=True)
    g_idx = jnp.min(jnp.where(lg == mg, col, big), axis=-1, keepdims=True)
    p_top = 1.0 / jnp.sum(jnp.where(is_g, jnp.exp(logits - mg), 0.0), axis=-1, keepdims=True)
    lo = n_groups + per_group * g_idx
    le = jnp.where((col >= lo) & (col < lo + per_group), logits, neg)
    m1 = jnp.max(le, axis=-1, keepdims=True)
    i1 = jnp.min(jnp.where(le == m1, col, big), axis=-1, keepdims=True)
    le2 = jnp.where(col == i1, neg, le)
    m2 = jnp.max(le2, axis=-1, keepdims=True)
    i2 = jnp.min(jnp.where(le2 == m2, col, big), axis=-1, keepdims=True)
    e2 = jnp.exp(m2 - m1)
    den = 1.0 + e2
    w1 = p_top / den
    w2 = p_top * (e2 / den)
    gates = jnp.where(col == i1, w1, 0.0) + jnp.where(col == i2, w2, 0.0)
    return gates + jnp.where(col == 0, g_idx.astype(F32), 0.0)


def _route_rows(logits_t, n_groups, per_group):
    assert n_groups <= 8 and per_group == 8
    n_tok = logits_t.shape[1]
    row = lax.broadcasted_iota(jnp.int32, (8, n_tok), 0)
    neg = jnp.float32(-jnp.inf)
    big = jnp.int32(8)
    lg = jnp.where(row < n_groups, logits_t[0:8], neg)
    mg = jnp.max(lg, axis=0, keepdims=True)
    g_idx = jnp.min(jnp.where(lg == mg, row, big), axis=0, keepdims=True)
    p_top = 1.0 / jnp.sum(jnp.where(row < n_groups, jnp.exp(lg - mg), 0.0), axis=0, keepdims=True)
    sel = logits_t[8:16]
    for g in range(1, n_groups):
        sel = jnp.where(g_idx == g, logits_t[8 + 8 * g:16 + 8 * g], sel)
    m1 = jnp.max(sel, axis=0, keepdims=True)
    i1 = jnp.min(jnp.where(sel == m1, row, big), axis=0, keepdims=True)
    sel2 = jnp.where(row == i1, neg, sel)
    m2 = jnp.max(sel2, axis=0, keepdims=True)
    i2 = jnp.min(jnp.where(sel2 == m2, row, big), axis=0, keepdims=True)
    e2 = jnp.exp(m2 - m1)
    den = 1.0 + e2
    w1 = p_top / den
    w2 = p_top * (e2 / den)
    own = jnp.where(row == i1, w1, 0.0) + jnp.where(row == i2, w2, 0.0)
    blocks = [jnp.where(row == 0, g_idx.astype(F32), 0.0)]
    blocks += [jnp.where(g_idx == g, own, 0.0) for g in range(n_groups)]
    blocks += [jnp.zeros((8, n_tok), F32)] * (ROUTER_LANES // 8 - len(blocks))
    return jnp.concatenate(blocks, axis=0), g_idx


def _mixer_prompt_kernel(x_ref, lbw_ref, g1_ref, win_ref, hgg_ref, cw_ref, cb_ref, wx_ref, bx_ref,
                         wa_ref, ba_ref, lam_ref, wout_ref, g2_ref, wr_ref, br_ref,
                         x1_ref, xn2_ref, gates_ref, cnt_ref, sout_ref, hout_ref, cout_ref,
                         proj_s, k_s, o_s, st_s, xr_s, a_s, u_s, hcar_s, yb_out_s,
                         *, n_groups, per_group):
    lb_t = x_ref.shape[1]
    wa_w = k_s.shape[1]
    wb_w = a_s.shape[1]
    n_heads = wa_w // HEAD_DIM
    j = pl.program_id(1)
    nj = pl.num_programs(1)

    @pl.when(j == 0)
    def _():
        st_s[...] = jnp.zeros_like(st_s)
        hcar_s[...] = jnp.zeros_like(hcar_s)
        xr_s[0:8, :] = jnp.zeros((8, wb_w), F32)

    x = x_ref[0]
    xn = _rms(x, g1_ref[...]).astype(BF16)
    for c, w in ((wa_w, wa_w), (4 * wa_w, wb_w), (4 * wa_w + wb_w, wb_w), (0, wa_w), (2 * wa_w, wa_w),
                 (3 * wa_w, wa_w)):
        proj_s[:, c:c + w] = jnp.dot(xn, win_ref[:, c:c + w], preferred_element_type=F32)

    xb0 = 4 * wa_w
    xr_s[pl.ds(8, lb_t), :] = proj_s[:, xb0:xb0 + wb_w]
    xc = (cb_ref[...] + cw_ref[3:4, :] * xr_s[pl.ds(8, lb_t), :] + cw_ref[2:3, :] * xr_s[pl.ds(7, lb_t), :]
          + cw_ref[1:2, :] * xr_s[pl.ds(6, lb_t), :] + cw_ref[0:1, :] * xr_s[pl.ds(5, lb_t), :])
    tail = xr_s[pl.ds(lb_t + 5, 3), :]
    xr_s[5:8, :] = tail
    cout_ref[0] = tail
    xcb = xc.astype(BF16)
    gate_x = _sigmoid(jnp.dot(xcb, wx_ref[...], preferred_element_type=F32) + bx_ref[...])
    gate_a = _sigmoid(jnp.dot(xcb, wa_ref[...], preferred_element_type=F32) + ba_ref[...])
    log_a = (-LRU_C) * gate_a * _softplus(-lam_ref[...])
    a = jnp.exp(log_a)
    mult = jnp.sqrt(-_expm1(2.0 * log_a))
    first = (lax.broadcasted_iota(jnp.int32, (lb_t, 1), 0) == 0) & (j == 0)
    a = jnp.where(first, 0.0, a)
    mult = jnp.where(first, 1.0, mult)
    a_s[...] = a
    u_s[...] = gate_x * xc * mult

    lb = _forget_lower_bound(lbw_ref[...])
    f = lb + (1.0 - lb) * _sigmoid(proj_s[:, wa_w:2 * wa_w])
    k_s[...] = 1.0 - f
    logf = jnp.log(f)
    r_i = lax.broadcasted_iota(jnp.int32, (CHUNK, CHUNK), 0)
    c_i = lax.broadcasted_iota(jnp.int32, (CHUNK, CHUNK), 1)
    tri = jnp.where(r_i >= c_i, 1.0, 0.0).astype(BF16)
    lf_hi = logf.astype(BF16)
    rest = logf - lf_hi.astype(F32)
    lf_mid = rest.astype(BF16)
    lf_lo = (rest - lf_mid.astype(F32)).astype(BF16)
    for c in range(0, lb_t, CHUNK):
        cum = [jnp.dot(tri, part[c:c + CHUNK, :], preferred_element_type=F32) for part in (lf_lo, lf_mid, lf_hi)]
        proj_s[c:c + CHUNK, wa_w:2 * wa_w] = LOG2E * ((cum[0] + cum[1]) + cum[2])

    row_sub = lax.broadcasted_iota(jnp.int32, (SUB, HEAD_DIM), 0)
    lane_sub = lax.broadcasted_iota(jnp.int32, (SUB, HEAD_DIM), 1)
    assert n_heads % 2 == 0
    r_kk = lax.broadcasted_iota(jnp.int32, (2 * HEAD_DIM, 2 * HEAD_DIM), 0)
    c_kk = lax.broadcasted_iota(jnp.int32, (2 * HEAD_DIM, 2 * HEAD_DIM), 1)
    ones_kk = jnp.where((r_kk < HEAD_DIM) == (c_kk < HEAD_DIM), 1.0, 0.0).astype(BF16)

    n_sub = CHUNK // SUB
    half = SUB // 2
    lower_left = (row_sub >= half) & (lane_sub < half)
    own_lane = jnp.where(row_sub >= half, half, 0)

    def chunk_start(r0):
        first = []
        for h in range(n_heads):
            hs = slice(h * HEAD_DIM, (h + 1) * HEAD_DIM)
            q = proj_s[pl.ds(r0, CHUNK), hs]
            b = proj_s[pl.ds(r0, CHUNK), wa_w + h * HEAD_DIM:wa_w + (h + 1) * HEAD_DIM]
            v = proj_s[pl.ds(r0, CHUNK), 2 * wa_w + h * HEAD_DIM:2 * wa_w + (h + 1) * HEAD_DIM]
            k = k_s[pl.ds(r0, CHUNK), hs]
            vb = v.astype(BF16)
            st = st_s[h]
            b_last = b[CHUNK - 1:CHUNK, :]
            o = lax.dot_general((q * jnp.exp2(b)).astype(BF16), st.astype(BF16), NT_DIMS,
                                preferred_element_type=F32)
            k_end = k * jnp.exp2(b_last - b)
            st_s[h] = st * jnp.exp2(b_last) + lax.dot_general(vb, k_end.astype(BF16), TN_DIMS,
                                                               preferred_element_type=F32)
            terms, off, mid = [], [], []
            for i in range(n_sub):
                lo = i * SUB
                qi, bi, ki = q[lo:lo + SUB], b[lo:lo + SUB], k[lo:lo + SUB]
                q3, b3, k3 = (a.reshape(2, half, HEAD_DIM) for a in (qi, bi, ki))
                terms += [(q3 * (k3[:, j:j + 1, :] * jnp.exp2(b3 - b3[:, j:j + 1, :]))).reshape(SUB, HEAD_DIM)
                          .astype(BF16) for j in range(half)]
                rm = bi[half - 1:half]
                mid.append(lax.dot_general((qi * jnp.exp2(bi - rm)).astype(BF16),
                                           (ki * jnp.exp2(rm - bi)).astype(BF16), NT_DIMS,
                                           preferred_element_type=F32))
                if i > 0:
                    r = b[lo - 1:lo]
                    qt = (qi * jnp.exp2(bi - r)).astype(BF16)
                    kt = (k[:lo] * jnp.exp2(r - b[:lo])).astype(BF16)
                    off.append(lax.dot_general(qt, kt, NT_DIMS, preferred_element_type=F32))
            first.append([o, vb, off, jnp.concatenate(terms, axis=0), mid])
        for h in range(0, n_heads, 2):
            both = jnp.concatenate([first[h][3], first[h + 1][3]], axis=1)
            sums = jnp.dot(both, ones_kk, preferred_element_type=F32)
            first[h][3] = sums[:, :HEAD_DIM]
            first[h + 1][3] = sums[:, HEAD_DIM:]
        return first

    def chunk_finish(r0, first):
        for h in range(n_heads):
            o, vb, off, sums, mid = first[h]
            outs = []
            for i in range(n_sub):
                lo = i * SUB
                sc = jnp.zeros((SUB, HEAD_DIM), F32)
                for j in range(half):
                    row0 = (i * half + j) * SUB
                    sc = jnp.where(lane_sub == own_lane + j, sums[row0:row0 + SUB], sc)
                sc = jnp.where(row_sub >= lane_sub, sc, 0.0)[:, :SUB]
                sc = jnp.where(lower_left[:, :SUB], mid[i], sc)
                od = o[lo:lo + SUB] + jnp.dot(sc.astype(BF16), vb[lo:lo + SUB], preferred_element_type=F32)
                if i > 0:
                    od = od + jnp.dot(off[i - 1].astype(BF16), vb[:lo], preferred_element_type=F32)
                outs.append(od)
            o_s[pl.ds(r0, CHUNK), h * HEAD_DIM:(h + 1) * HEAD_DIM] = jnp.concatenate(outs, axis=0)

    def chunks_body(ci, carry):
        rows = [pl.multiple_of((ci * UNROLL + u) * CHUNK, CHUNK) for u in range(UNROLL)]
        started = [chunk_start(r0) for r0 in rows]
        for r0, first in zip(rows, started):
            chunk_finish(r0, first)
        return carry

    row8 = lax.broadcasted_iota(jnp.int32, (8, 1), 0)

    def scan_body(gi, carry):
        r0 = pl.multiple_of(gi * 8, 8)
        aa = a_s[pl.ds(r0, 8), :]
        uu = u_s[pl.ds(r0, 8), :]
        for s in (1, 2, 4):
            m = row8 >= s
            uu = jnp.where(m, aa * pltpu.roll(uu, s, 0) + uu, uu)
            aa = jnp.where(m, aa * pltpu.roll(aa, s, 0), aa)
        hh = aa * carry + uu
        u_s[pl.ds(r0, 8), :] = hh
        return hh[7:8, :]

    h_last = lax.fori_loop(0, lb_t // 8, scan_body, hcar_s[...])
    hcar_s[...] = h_last
    hout_ref[0] = h_last
    yb = (u_s[...] * _gelu_tanh(proj_s[:, xb0 + wb_w:xb0 + 2 * wb_w])).astype(BF16)
    yb_out_s[...] = jnp.dot(yb, wout_ref[wa_w:, :], preferred_element_type=F32)

    assert lb_t % (CHUNK * UNROLL) == 0
    lax.fori_loop(0, lb_t // (CHUNK * UNROLL), chunks_body, 0)

    ya = []
    for h in range(n_heads):
        hs = slice(h * HEAD_DIM, (h + 1) * HEAD_DIM)
        oh = o_s[:, hs]
        oh = oh * lax.rsqrt(jnp.mean(oh * oh, axis=-1, keepdims=True) + EPS) * hgg_ref[:, hs]
        ya.append(oh * _silu(proj_s[:, 3 * wa_w + h * HEAD_DIM:3 * wa_w + (h + 1) * HEAD_DIM]))

    ya = jnp.concatenate(ya, axis=-1).astype(BF16)
    x1 = x + (jnp.dot(ya, wout_ref[:wa_w, :], preferred_element_type=F32) + yb_out_s[...])
    x1_ref[0] = x1
    xn2 = _rms(x1, g2_ref[...]).astype(BF16)
    xn2_ref[0] = xn2
    logits_t = lax.dot_general(wr_ref[...], xn2, NT_DIMS, preferred_element_type=F32) + br_ref[...]
    gates_t, g_idx = _route_rows(logits_t, n_groups, per_group)
    gates = gates_t.T
    lane = lax.broadcasted_iota(jnp.int32, gates.shape, 1)
    gates_ref[0] = jnp.where(lane == 0, gates,
                             pltpu.roll(jnp.where(lane == 0, 0.0, gates), ROUTER_LANES - (8 - n_groups), 1))
    lane_c = lax.broadcasted_iota(jnp.int32, (1, ROUTER_LANES), 1)
    for t in range(lb_t // TOK_TILE):
        gi = g_idx[:, t * TOK_TILE:(t + 1) * TOK_TILE]
        cnt = jnp.zeros((1, ROUTER_LANES), jnp.int32)
        for g in range(n_groups):
            cnt = cnt + jnp.where(lane_c == g, jnp.sum(jnp.where(gi == g, 1, 0), axis=1, keepdims=True), 0)
        cnt_ref[0, t:t + 1, :] = cnt

    @pl.when(j == nj - 1)
    def _():
        for h in range(n_heads):
            sout_ref[0, h] = st_s[h].T


def _const_spec(shape):
    nd = len(shape)
    return pl.BlockSpec(shape, lambda *_: (0,) * nd)


def _mixer_prompt(x, p, lb_t):
    bsz, seq, d = x.shape
    wa_w = p["hgg"].shape[1]
    wb_w = p["cb"].shape[1]
    n_heads = wa_w // HEAD_DIM
    n_cols = p["win_bf"].shape[1]
    weights = [p["lbw"], p["g1"], p["win_bf"], p["hgg"], p["cw"], p["cb"], p["wx_bf"], p["bx"],
               p["wa_bf"], p["ba"], p["lam"], p["wout_bf"], p["g2"], p["wr_t_bf"], p["br_t"]]
    nj = seq // lb_t
    tile = lambda w: pl.BlockSpec((1, lb_t, w), lambda b, j: (b, j, 0))
    out_shape = (
        jax.ShapeDtypeStruct((bsz, seq, d), F32),
        jax.ShapeDtypeStruct((bsz, seq, d), BF16),
        jax.ShapeDtypeStruct((bsz, seq, ROUTER_LANES), F32),
        jax.ShapeDtypeStruct((bsz * (seq // lb_t), lb_t // TOK_TILE, ROUTER_LANES), jnp.int32),
        jax.ShapeDtypeStruct((bsz, n_heads, HEAD_DIM, HEAD_DIM), F32),
        jax.ShapeDtypeStruct((bsz, 1, wb_w), F32),
        jax.ShapeDtypeStruct((bsz, 3, wb_w), F32),
    )
    out_specs = (
        tile(d), tile(d), tile(ROUTER_LANES),
        pl.BlockSpec((1, lb_t // TOK_TILE, ROUTER_LANES), lambda b, j: (b * nj + j, 0, 0)),
        pl.BlockSpec((1, n_heads, HEAD_DIM, HEAD_DIM), lambda b, j: (b, 0, 0, 0)),
        pl.BlockSpec((1, 1, wb_w), lambda b, j: (b, 0, 0)),
        pl.BlockSpec((1, 3, wb_w), lambda b, j: (b, 0, 0)),
    )
    scratch = [
        pltpu.VMEM((lb_t, n_cols), F32),
        pltpu.VMEM((lb_t, wa_w), F32),
        pltpu.VMEM((lb_t, wa_w), F32),
        pltpu.VMEM((n_heads, HEAD_DIM, HEAD_DIM), F32),
        pltpu.VMEM((lb_t + 8, wb_w), F32),
        pltpu.VMEM((lb_t, wb_w), F32),
        pltpu.VMEM((lb_t, wb_w), F32),
        pltpu.VMEM((1, wb_w), F32),
        pltpu.VMEM((lb_t, d), F32),
    ]
    kern = functools.partial(_mixer_prompt_kernel, n_groups=p["n_groups"], per_group=p["per_group"])
    return pl.pallas_call(
        kern,
        grid=(bsz, nj),
        in_specs=[tile(d)] + [_const_spec(w.shape) for w in weights],
        out_specs=out_specs,
        out_shape=out_shape,
        scratch_shapes=scratch,
        compiler_params=pltpu.CompilerParams(dimension_semantics=("arbitrary", "arbitrary"),
                                             vmem_limit_bytes=VMEM_LIMIT_BYTES),
        name="mixer_prompt",
    )(x, *weights)


def _moe_kernel(x1_ref, xn2_ref, gates_ref, w1_ref, w3_ref, w2_ref, gf_ref, y_ref, acc_s, *, n_groups):
    e = pl.program_id(1)

    @pl.when(e == 0)
    def _():
        acc_s[...] = jnp.zeros_like(acc_s)

    xb = xn2_ref[...]
    per_step = w1_ref.shape[0]
    gates = pltpu.roll(gates_ref[...], ROUTER_LANES - n_groups - per_step * e, 1)
    acc = acc_s[...]
    for s in range(per_step):
        h = (_silu(jnp.dot(xb, w1_ref[s], preferred_element_type=F32))
             * jnp.dot(xb, w3_ref[s], preferred_element_type=F32) * gates[:, s:s + 1])
        acc = acc + jnp.dot(h.astype(BF16), w2_ref[s], preferred_element_type=F32)
    acc_s[...] = acc

    @pl.when(e == pl.num_programs(1) - 1)
    def _():
        y_ref[...] = _rms(x1_ref[...] + acc_s[...], gf_ref[...])


def _moe_dense(x1, xn2, gates, p, tm):
    t, d = x1.shape
    n_exp, _, d_exp = p["w1_bf"].shape
    per_step = p["per_group"]
    row = lambda w: pl.BlockSpec((tm, w), lambda i, e: (i, 0))
    return pl.pallas_call(
        functools.partial(_moe_kernel, n_groups=p["n_groups"]),
        grid=(t // tm, n_exp // per_step),
        in_specs=[row(d), row(d), row(ROUTER_LANES),
                  pl.BlockSpec((per_step, d, d_exp), lambda i, e: (e, 0, 0)),
                  pl.BlockSpec((per_step, d, d_exp), lambda i, e: (e, 0, 0)),
                  pl.BlockSpec((per_step, d_exp, d), lambda i, e: (e, 0, 0)),
                  _const_spec(p["gf"].shape)],
        out_specs=row(d),
        out_shape=jax.ShapeDtypeStruct((t, d), F32),
        scratch_shapes=[pltpu.VMEM((tm, d), F32)],
        compiler_params=pltpu.CompilerParams(dimension_semantics=("arbitrary", "arbitrary"),
                                             vmem_limit_bytes=VMEM_LIMIT_BYTES),
        name="moe_dense",
    )(x1, xn2, gates, p["w1_bf"], p["w3_bf"], p["w2_bf"], p["gf"])


def _seg_pad(n):
    return jnp.bitwise_and(n + (SEG - 1), -SEG)


def _row_tile_pad(n):
    return jnp.bitwise_and(n + (ROW_TILE - 1), -ROW_TILE)


def _tile_segments(cnt_ref, tile, n_groups):
    padded = [_seg_pad(cnt_ref[tile * n_groups + g]) for g in range(n_groups)]
    starts, acc = [], jnp.int32(0)
    for g in range(n_groups):
        starts.append(acc)
        acc = acc + padded[g]
    return padded, starts, acc


def _group_bases(cnt_ref, n_tiles, n_groups):
    def body(t, tot):
        return tuple(tot[g] + _seg_pad(cnt_ref[t * n_groups + g]) for g in range(n_groups))
    totals = lax.fori_loop(0, n_tiles, body, (jnp.int32(0),) * n_groups)
    bases, ends, acc = [], [], jnp.int32(0)
    for g in range(n_groups):
        bases.append(acc)
        acc = acc + _row_tile_pad(totals[g])
        ends.append(acc)
    return bases, ends


def _chunk_row(c, padded, starts, gstart):
    lo = c * SEG
    row = jnp.int32(0)
    for g in range(len(padded)):
        inside = (lo >= starts[g]) & (lo < starts[g] + padded[g])
        row = row + jnp.where(inside, gstart[g] + lo - starts[g], 0)
    return pl.multiple_of(row, SEG)


def _sort_matrix(gates, starts):
    n_groups = len(starts)
    col = lax.broadcasted_iota(jnp.int32, (TOK_TILE, ROUTER_LANES), 1)
    gi = gates[:, 0:1].astype(jnp.int32)
    onehot = col == gi
    r_i = lax.broadcasted_iota(jnp.int32, (TOK_TILE, TOK_TILE), 0)
    c_i = lax.broadcasted_iota(jnp.int32, (TOK_TILE, TOK_TILE), 1)
    earlier = jnp.where(r_i > c_i, 1.0, 0.0).astype(BF16)
    before = jnp.dot(earlier, jnp.where(onehot, 1.0, 0.0).astype(BF16), preferred_element_type=F32)
    rank = jnp.sum(jnp.where(onehot, before, 0.0), axis=-1, keepdims=True).astype(jnp.int32)
    base = jnp.zeros((TOK_TILE, 1), jnp.int32)
    for g in range(n_groups):
        base = base + jnp.where(gi == g, starts[g], 0)
    lane = lax.broadcasted_iota(jnp.int32, (TOK_TILE, SORT_ROWS), 1)
    return lane == base + rank


def _sort_matrix_rows(gates, starts):
    n_groups = len(starts)
    col = lax.broadcasted_iota(jnp.int32, (TOK_TILE, ROUTER_LANES), 1)
    onehot_t = jnp.where(col == gates[:, 0:1].astype(jnp.int32), 1.0, 0.0).T
    r_i = lax.broadcasted_iota(jnp.int32, (TOK_TILE, TOK_TILE), 0)
    c_i = lax.broadcasted_iota(jnp.int32, (TOK_TILE, TOK_TILE), 1)
    later = jnp.where(r_i < c_i, 1.0, 0.0).astype(BF16)
    before = jnp.dot(onehot_t.astype(BF16), later, preferred_element_type=F32)
    g_row = lax.broadcasted_iota(jnp.int32, (ROUTER_LANES, 1), 0)
    base = jnp.zeros((ROUTER_LANES, 1), jnp.int32)
    for g in range(n_groups):
        base = base + jnp.where(g_row == g, starts[g], 0)
    dest = jnp.sum(jnp.where(onehot_t > 0.0, before + base.astype(F32), 0.0), axis=0, keepdims=True)
    row = lax.broadcasted_iota(jnp.int32, (SORT_ROWS, TOK_TILE), 0)
    return row == dest.astype(jnp.int32)


def _dispatch_copies(xbuf, gbuf, xs_hbm, gs_hbm, sem, slot, c, row):
    return (pltpu.make_async_copy(xbuf.at[slot, pl.ds(c * SEG, SEG), :], xs_hbm.at[pl.ds(row, SEG), :], sem.at[slot]),
            pltpu.make_async_copy(gbuf.at[slot, pl.ds(c * SEG, SEG), :], gs_hbm.at[pl.ds(row, SEG), :], sem.at[slot]))


def _moe_dispatch_kernel(cnt_ref, xn2_ref, gates_ref, xs_in, gs_in, xs_hbm, gs_hbm, tg_ref,
                         xbuf, gbuf, sem, gstart_s, *, n_groups, per_group):
    del xs_in, gs_in
    j = pl.program_id(0)
    n_tiles = pl.num_programs(0)
    n_chunks = (TOK_TILE + n_groups * SEG) // SEG
    slot = lax.rem(j, 2)

    @pl.when(j == 0)
    def _():
        bases, ends = _group_bases(cnt_ref, n_tiles, n_groups)
        for g in range(n_groups):
            gstart_s[g] = bases[g]
        n_rt = tg_ref.shape[0] - 1
        for i in range(n_rt):
            tg = jnp.int32(0)
            for g in range(n_groups):
                tg = tg + jnp.where(ends[g] <= i * ROW_TILE, 1, 0)
            tg_ref[i] = tg
        tg_ref[n_rt] = ends[n_groups - 1] // ROW_TILE

    def wait_tile(tile, slot_):
        _, _, used = _tile_segments(cnt_ref, tile, n_groups)
        for c in range(n_chunks):
            @pl.when(c * SEG < used)
            def _():
                for cp in _dispatch_copies(xbuf, gbuf, xs_hbm, gs_hbm, sem, slot_, c, 0):
                    cp.wait()

    @pl.when(j >= 2)
    def _():
        wait_tile(j - 2, slot)

    padded, starts, used = _tile_segments(cnt_ref, j, n_groups)
    gates = gates_ref[...]
    d = xn2_ref.shape[1]
    g_hi = gates.astype(BF16)
    g_lo = (gates - g_hi.astype(F32)).astype(BF16)
    sort_m = jnp.where(_sort_matrix_rows(gates, starts), 1.0, 0.0).astype(BF16)
    moved = jnp.dot(sort_m, jnp.concatenate([xn2_ref[...], g_hi, g_lo], axis=1), preferred_element_type=F32)
    xbuf[slot] = moved[:, :d].astype(BF16)
    gbuf[slot] = moved[:, d:d + ROUTER_LANES] + moved[:, d + ROUTER_LANES:]
    gstart = [gstart_s[g] for g in range(n_groups)]
    for c in range(n_chunks):
        @pl.when(c * SEG < used)
        def _():
            for cp in _dispatch_copies(xbuf, gbuf, xs_hbm, gs_hbm, sem, slot, c,
                                       _chunk_row(c, padded, starts, gstart)):
                cp.start()
    for g in range(n_groups):
        gstart_s[g] = gstart[g] + padded[g]

    @pl.when(j == n_tiles - 1)
    def _():
        @pl.when(j >= 1)
        def _():
            wait_tile(j - 1, 1 - slot)
        wait_tile(j, slot)


def _moe_group_kernel(tg_ref, xs_ref, gs_ref, w1_ref, w3_ref, w2_ref, ys_ref, *, n_groups, per_group):
    i = pl.program_id(0)

    @pl.when(tg_ref[i] < n_groups)
    def _():
        xb = xs_ref[...]
        gates = pltpu.roll(gs_ref[...], ROUTER_LANES - n_groups - per_group * tg_ref[i], 1)
        acc = jnp.zeros(ys_ref.shape, F32)
        for s in range(per_group):
            h = (_silu(jnp.dot(xb, w1_ref[s], preferred_element_type=F32))
                 * jnp.dot(xb, w3_ref[s], preferred_element_type=F32) * gates[:, s:s + 1])
            acc = acc + jnp.dot(h.astype(BF16), w2_ref[s], preferred_element_type=F32)
        ys_ref[...] = acc.astype(BF16)


def _moe_combine_kernel(cnt_ref, gates_ref, x1_ref, gf_ref, ys_hbm, y_ref, ybuf, sem, gstart_s, *, n_groups):
    j = pl.program_id(0)
    n_tiles = pl.num_programs(0)
    n_chunks = SORT_ROWS // SEG
    slot = lax.rem(j, 2)

    def copies(slot_, c, row):
        return pltpu.make_async_copy(ys_hbm.at[pl.ds(row, SEG), :], ybuf.at[slot_, pl.ds(c * SEG, SEG), :],
                                     sem.at[slot_])

    def fetch_tile(tile, slot_):
        padded, starts, _ = _tile_segments(cnt_ref, tile, n_groups)
        gstart = [gstart_s[g] for g in range(n_groups)]
        for c in range(n_chunks):
            copies(slot_, c, _chunk_row(c, padded, starts, gstart)).start()
        for g in range(n_groups):
            gstart_s[g] = gstart[g] + padded[g]

    @pl.when(j == 0)
    def _():
        bases, _ = _group_bases(cnt_ref, n_tiles, n_groups)
        for g in range(n_groups):
            gstart_s[g] = bases[g]
        fetch_tile(0, 0)

    @pl.when(j + 1 < n_tiles)
    def _():
        fetch_tile(j + 1, 1 - slot)

    for c in range(n_chunks):
        copies(slot, c, 0).wait()
    _, starts, _ = _tile_segments(cnt_ref, j, n_groups)
    sort_t = _sort_matrix(gates_ref[...], starts)
    moe = jnp.dot(jnp.where(sort_t, 1.0, 0.0).astype(BF16), ybuf[slot], preferred_element_type=F32)
    y_ref[...] = _rms(x1_ref[...] + moe, gf_ref[...])


def _moe_sorted(x1, xn2, gates, counts, p):
    t, d = x1.shape
    n_groups, per_group = p["n_groups"], p["per_group"]
    n_tiles = t // TOK_TILE
    d_exp = p["w1_bf"].shape[2]
    cap = t + n_tiles * n_groups * SEG + n_groups * ROW_TILE
    n_rt = -(-cap // ROW_TILE)
    rows = n_rt * ROW_TILE
    cnt = counts[:, :, :n_groups].reshape(-1)
    params = pltpu.CompilerParams(dimension_semantics=("arbitrary",), vmem_limit_bytes=VMEM_LIMIT_BYTES)
    any_spec = pl.BlockSpec(memory_space=pl.ANY)
    tok = lambda w: pl.BlockSpec((TOK_TILE, w), lambda j, c: (j, 0))

    xs, gs, tile_group = pl.pallas_call(
        functools.partial(_moe_dispatch_kernel, n_groups=n_groups, per_group=per_group),
        grid_spec=pltpu.PrefetchScalarGridSpec(
            num_scalar_prefetch=1, grid=(n_tiles,),
            in_specs=[tok(d), tok(ROUTER_LANES), any_spec, any_spec],
            out_specs=(any_spec, any_spec, pl.BlockSpec(memory_space=pltpu.SMEM)),
            scratch_shapes=[pltpu.VMEM((2, SORT_ROWS, d), BF16), pltpu.VMEM((2, SORT_ROWS, ROUTER_LANES), F32),
                            pltpu.SemaphoreType.DMA((2,)), pltpu.SMEM((n_groups,), jnp.int32)]),
        out_shape=(jax.ShapeDtypeStruct((rows, d), BF16), jax.ShapeDtypeStruct((rows, ROUTER_LANES), F32),
                   jax.ShapeDtypeStruct((n_rt + 1,), jnp.int32)),
        input_output_aliases={3: 0, 4: 1},
        compiler_params=params, name="moe_dispatch",
    )(cnt, xn2, gates, jnp.zeros((rows, d), BF16), jnp.zeros((rows, ROUTER_LANES), F32))

    used_tile = lambda i, tg: (jnp.minimum(i, tg[n_rt] - 1), 0)
    group_w = lambda i, tg: (jnp.minimum(tg[i], n_groups - 1), 0, 0)
    ys = pl.pallas_call(
        functools.partial(_moe_group_kernel, n_groups=n_groups, per_group=per_group),
        grid_spec=pltpu.PrefetchScalarGridSpec(
            num_scalar_prefetch=1, grid=(n_rt,),
            in_specs=[pl.BlockSpec((ROW_TILE, d), used_tile), pl.BlockSpec((ROW_TILE, ROUTER_LANES), used_tile),
                      pl.BlockSpec((per_group, d, d_exp), group_w), pl.BlockSpec((per_group, d, d_exp), group_w),
                      pl.BlockSpec((per_group, d_exp, d), group_w)],
            out_specs=pl.BlockSpec((ROW_TILE, d), used_tile)),
        out_shape=jax.ShapeDtypeStruct((rows, d), BF16),
        input_output_aliases={1: 0},
        compiler_params=params, name="moe_experts",
    )(tile_group, xs, gs, p["w1_bf"], p["w3_bf"], p["w2_bf"])

    return pl.pallas_call(
        functools.partial(_moe_combine_kernel, n_groups=n_groups),
        grid_spec=pltpu.PrefetchScalarGridSpec(
            num_scalar_prefetch=1, grid=(n_tiles,),
            in_specs=[tok(ROUTER_LANES), tok(d), pl.BlockSpec(p["gf"].shape, lambda j, c: (0, 0)), any_spec],
            out_specs=tok(d),
            scratch_shapes=[pltpu.VMEM((2, SORT_ROWS, d), BF16), pltpu.SemaphoreType.DMA((2,)),
                            pltpu.SMEM((n_groups,), jnp.int32)]),
        out_shape=jax.ShapeDtypeStruct((t, d), F32),
        compiler_params=params, name="moe_combine",
    )(cnt, gates, x1, p["gf"], ys)


def _sample_in_kernel(x_ref, c0_ref, c1_ref, c2_ref, h0_ref, lbw_ref, g1_ref, win_ref, cw_ref, cb_ref,
                      wx_ref, bx_ref, wa_ref, ba_ref, lam_ref,
                      q_ref, f_ref, k_ref, v_ref, g_ref, yb_ref, hnew_ref, xr_ref):
    wa_w = v_ref.shape[1]
    wb_w = yb_ref.shape[1]
    xn = _rms(x_ref[...], g1_ref[...])
    proj = jnp.dot(xn, win_ref[...], precision=HIGHEST, preferred_element_type=F32)
    lb = _forget_lower_bound(lbw_ref[...])
    f = lb + (1.0 - lb) * _sigmoid(proj[:, wa_w:2 * wa_w])
    q_ref[...] = proj[:, 0:wa_w].T
    f_ref[...] = f.T
    k_ref[...] = (1.0 - f).T
    v_ref[...] = proj[:, 2 * wa_w:3 * wa_w]
    g_ref[...] = proj[:, 3 * wa_w:4 * wa_w]
    xr = proj[:, 4 * wa_w:4 * wa_w + wb_w]
    xr_ref[...] = xr
    xc = (cb_ref[...] + cw_ref[0:1, :] * c0_ref[...] + cw_ref[1:2, :] * c1_ref[...]
          + cw_ref[2:3, :] * c2_ref[...] + cw_ref[3:4, :] * xr)
    gate_x = _sigmoid(jnp.dot(xc, wx_ref[...], precision=HIGHEST, preferred_element_type=F32) + bx_ref[...])
    gate_a = _sigmoid(jnp.dot(xc, wa_ref[...], precision=HIGHEST, preferred_element_type=F32) + ba_ref[...])
    log_a = (-LRU_C) * gate_a * _softplus(-lam_ref[...])
    a = jnp.exp(log_a)
    mult = jnp.sqrt(-_expm1(2.0 * log_a))
    h = a * h0_ref[...] + gate_x * xc * mult
    hnew_ref[...] = h
    yb_ref[...] = h * _gelu_tanh(proj[:, 4 * wa_w + wb_w:4 * wa_w + 2 * wb_w])


def _sample_state_kernel(s_ref, qt_ref, ft_ref, kt_ref, v_ref, snew_ref, o_ref):
    tb, n_heads = s_ref.shape[0], s_ref.shape[1]
    n_tok = qt_ref.shape[1]
    shift = lax.rem(n_tok - pl.program_id(0) * tb, n_tok)
    sq = (HEAD_DIM, HEAD_DIM)
    for h in range(n_heads):
        hs = slice(h * HEAD_DIM, (h + 1) * HEAD_DIM)
        qh = pltpu.roll(qt_ref[hs, :], shift, 1)
        fh = pltpu.roll(ft_ref[hs, :], shift, 1)
        kh = pltpu.roll(kt_ref[hs, :], shift, 1)
        rows = []
        for t in range(tb):
            s_new = (jnp.broadcast_to(fh[:, t:t + 1], sq) * s_ref[t, h]
                     + jnp.broadcast_to(kh[:, t:t + 1], sq) * v_ref[t:t + 1, hs])
            snew_ref[t, h] = s_new
            rows.append(jnp.sum(jnp.broadcast_to(qh[:, t:t + 1], sq) * s_new, axis=0, keepdims=True))
        o_ref[:, hs] = jnp.concatenate(rows, axis=0)


def _sample_out_kernel(x_ref, o_ref, g_ref, yb_ref, hgg_ref, wout_ref, g2_ref, wr_ref, br_ref,
                       x1_ref, xn2_ref, gates_ref, *, n_groups, per_group):
    wa_w = o_ref.shape[1]
    ya = []
    for h in range(wa_w // HEAD_DIM):
        hs = slice(h * HEAD_DIM, (h + 1) * HEAD_DIM)
        oh = o_ref[:, hs]
        oh = oh * lax.rsqrt(jnp.mean(oh * oh, axis=-1, keepdims=True) + EPS) * hgg_ref[:, hs]
        ya.append(oh * _silu(g_ref[:, hs]))
    y = jnp.concatenate(ya + [yb_ref[...]], axis=-1)
    x1 = x_ref[...] + jnp.dot(y, wout_ref[...], precision=HIGHEST, preferred_element_type=F32)
    x1_ref[...] = x1
    xn2 = _rms(x1, g2_ref[...])
    xn2_ref[...] = xn2.astype(BF16)
    logits = jnp.dot(xn2, wr_ref[...], precision=HIGHEST, preferred_element_type=F32) + br_ref[...]
    gates_ref[...] = _route(logits, n_groups, per_group)


def _whole(kernel, out_shape, *args, name):
    return pl.pallas_call(
        kernel, out_shape=out_shape,
        compiler_params=pltpu.CompilerParams(vmem_limit_bytes=VMEM_LIMIT_BYTES), name=name)(*args)


def _mixer_sample(x, s0, h0, c0, p, tb):
    n, d = x.shape
    wa_w = p["hgg"].shape[1]
    wb_w = p["cb"].shape[1]
    n_heads = wa_w // HEAD_DIM
    sd = lambda w: jax.ShapeDtypeStruct((n, w), F32)
    key_major = jax.ShapeDtypeStruct((wa_w, n), F32)
    q, f, k, v, g, yb, h_new, xr = _whole(
        _sample_in_kernel, (key_major,) * 3 + (sd(wa_w),) * 2 + (sd(wb_w),) * 3,
        x, c0[:, 0, :], c0[:, 1, :], c0[:, 2, :], h0, p["lbw"], p["g1"], p["win"], p["cw"], p["cb"],
        p["wx"], p["bx"], p["wa"], p["ba"], p["lam"], name="sample_in")
    tok = lambda w: pl.BlockSpec((tb, w), lambda i: (i, 0))
    st = pl.BlockSpec((tb, n_heads, HEAD_DIM, HEAD_DIM), lambda i: (i, 0, 0, 0))
    s_new, o = pl.pallas_call(
        _sample_state_kernel,
        grid=(n // tb,),
        in_specs=[st, _const_spec((wa_w, n)), _const_spec((wa_w, n)), _const_spec((wa_w, n)), tok(wa_w)],
        out_specs=(st, tok(wa_w)),
        out_shape=(jax.ShapeDtypeStruct(s0.shape, F32), sd(wa_w)),
        compiler_params=pltpu.CompilerParams(dimension_semantics=("arbitrary",),
                                             vmem_limit_bytes=VMEM_LIMIT_BYTES),
        name="sample_state",
    )(s0, q, f, k, v)
    x1, xn2, gates = _whole(
        functools.partial(_sample_out_kernel, n_groups=p["n_groups"], per_group=p["per_group"]),
        (sd(d), jax.ShapeDtypeStruct((n, d), BF16), sd(ROUTER_LANES)),
        x, o, g, yb, p["hgg"], p["wout"], p["g2"], p["wr"], p["br"], name="sample_out")
    c_new = jnp.stack([c0[:, 1, :], c0[:, 2, :], xr], axis=1)
    return x1, xn2, gates, s_new, h_new, c_new


def _block_diag(w):
    n, c, _ = w.shape
    eye = jnp.eye(n, dtype=w.dtype)
    return (w[:, :, None, :] * eye[:, None, :, None]).reshape(n * c, n * c)


def _prepare(lower_bounds, ln1_g, w_in, hgrn_norm_g, conv_w, conv_b, lru_wx, lru_bx, lru_wa, lru_ba,
             lru_lambda, w_out, ln2_g, router_wg, router_bg, router_we, router_be, exp_w1, exp_w3,
             exp_w2, final_g):
    d = w_in.shape[1]
    n_groups = router_wg.shape[-1]
    per_group = router_we.shape[-1]
    row = lambda a: a.reshape(1, -1).astype(F32)
    we = jnp.transpose(router_we[0], (1, 0, 2)).reshape(d, n_groups * per_group)
    pad = ROUTER_LANES - n_groups - n_groups * per_group
    wr = jnp.concatenate([router_wg[0], we, jnp.zeros((d, pad), F32)], axis=1)
    br = jnp.concatenate([router_bg[0], router_be[0].reshape(-1), jnp.zeros((pad,), F32)]).reshape(1, -1)
    rows_t = -(-(8 + n_groups * per_group) // 16) * 16
    wr_t = jnp.concatenate([router_wg[0].T, jnp.zeros((8 - n_groups, d), F32), we.T,
                            jnp.zeros((rows_t - 8 - n_groups * per_group, d), F32)], axis=0)
    br_t = jnp.concatenate([router_bg[0], jnp.zeros((8 - n_groups,), F32), router_be[0].reshape(-1),
                            jnp.zeros((rows_t - 8 - n_groups * per_group,), F32)]).reshape(-1, 1)
    wx = _block_diag(lru_wx[0])
    wa = _block_diag(lru_wa[0])
    return dict(
        wr_t_bf=wr_t.astype(BF16), br_t=br_t,
        n_groups=n_groups, per_group=per_group,
        lbw=lower_bounds.astype(F32), g1=row(ln1_g[0]), win=w_in[0], win_bf=w_in[0].astype(BF16),
        hgg=row(hgrn_norm_g[0]), cw=conv_w[0], cb=row(conv_b[0]),
        wx=wx, wx_bf=wx.astype(BF16), bx=row(lru_bx[0]), wa=wa, wa_bf=wa.astype(BF16), ba=row(lru_ba[0]),
        lam=row(lru_lambda[0]), wout=w_out[0], wout_bf=w_out[0].astype(BF16), g2=row(ln2_g[0]),
        wr=wr, br=br, w1_bf=exp_w1[0].astype(BF16), w3_bf=exp_w3[0].astype(BF16),
        w2_bf=exp_w2[0].astype(BF16), gf=row(final_g))


def kernel(x_prompt, x_sample, state_hgrn, state_rglru, state_conv, lower_bounds, ln1_g, w_in, hgrn_norm_g, conv_w, conv_b, lru_wx, lru_bx, lru_wa, lru_ba, lru_lambda, w_out, ln2_g, router_wg, router_bg, router_we, router_be, exp_w1, exp_w3, exp_w2, final_g):
    assert w_in.shape[0] == 1, "single-layer trunk"
    p = _prepare(lower_bounds, ln1_g, w_in, hgrn_norm_g, conv_w, conv_b, lru_wx, lru_bx, lru_wa, lru_ba,
                 lru_lambda, w_out, ln2_g, router_wg, router_bg, router_we, router_be, exp_w1, exp_w3,
                 exp_w2, final_g)
    bsz, seq, d = x_prompt.shape
    lb_t = min(512, seq)
    x1, xn2, gates, counts, s_p, h_p, c_p = _mixer_prompt(x_prompt, p, lb_t)
    t = bsz * seq
    y_p = _moe_sorted(x1.reshape(t, d), xn2.reshape(t, d), gates.reshape(t, ROUTER_LANES), counts, p)

    n = x_sample.shape[0]
    x1s, xn2s, gates_s, s_s, h_s, c_s = _mixer_sample(x_sample[:, 0, :], state_hgrn[0], state_rglru[0],
                                                      state_conv[0], p, 8)
    y_s = _moe_dense(x1s, xn2s, gates_s, p, n)
    return (y_p.reshape(bsz, seq, d), y_s.reshape(n, 1, d),
            s_p[None], h_p.reshape(1, bsz, -1), c_p[None],
            s_s[None], h_s[None], c_s[None])
```

```python
import functools

import jax
import jax.numpy as jnp
from jax import lax
from jax.experimental import pallas as pl
from jax.experimental.pallas import tpu as pltpu

F32 = jnp.float32
BF16 = jnp.bfloat16
HIGHEST = lax.Precision.HIGHEST

EPS = 1e-6
LRU_C = 8.0
LOG2E = 1.4426950408889634
HEAD_DIM = 128
CHUNK = 64
SUB = 16
UNROLL = 4
ROUTER_LANES = 128
PROMPT_EXPERT_LANE = 8
TOK_TILE = 512
SEG = 16
SORT_ROWS = 640
ROW_TILE = 512
VMEM_LIMIT_BYTES = 56 * 1024 * 1024

NT_DIMS = (((1,), (1,)), ((), ()))
TN_DIMS = (((0,), (0,)), ((), ()))


def _rms(x, g):
    return x * lax.rsqrt(jnp.mean(x * x, axis=-1, keepdims=True) + EPS) * g


def _sigmoid(x):
    return 1.0 / (1.0 + jnp.exp(-x))


def _silu(x):
    return x * _sigmoid(x)


def _gelu_tanh(x):
    c = 0.7978845608028654
    return x * (0.5 * (1.0 + jnp.tanh(c * (x + 0.044715 * (x * x * x)))))


def _softplus(z):
    return jnp.maximum(z, 0.0) + jnp.log1p(jnp.exp(-jnp.abs(z)))


def _expm1(x):
    u = jnp.exp(x)
    um1 = u - 1.0
    small = um1 * x / jnp.log(u)
    return jnp.where(um1 == 0.0, x, jnp.where(jnp.abs(x) < 0.5, small, um1))


def _forget_lower_bound(lbw):
    m = jnp.max(lbw, axis=0, keepdims=True)
    e = jnp.exp(lbw - m)
    return e[0:1, :] / jnp.sum(e, axis=0, keepdims=True)


def _route(logits, n_groups, per_group):
    n = logits.shape[-1]
    col = lax.broadcasted_iota(jnp.int32, logits.shape, 1)
    neg = jnp.float32(-jnp.inf)
    big = jnp.int32(n)
    is_g = col < n_groups
    lg = jnp.where(is_g, logits, neg)
    mg = jnp.max(lg, axis=-1, keepdims=True)
    g_idx = jnp.min(jnp.where(lg == mg, col, big), axis=-1, keepdims=True)
    p_top = 1.0 / jnp.sum(jnp.where(is_g, jnp.exp(logits - mg), 0.0), axis=-1, keepdims=True)
    lo = n_groups + per_group * g_idx
    le = jnp.where((col >= lo) & (col < lo + per_group), logits, neg)
    m1 = jnp.max(le, axis=-1, keepdims=True)
    i1 = jnp.min(jnp.where(le == m1, col, big), axis=-1, keepdims=True)
    le2 = jnp.where(col == i1, neg, le)
    m2 = jnp.max(le2, axis=-1, keepdims=True)
    i2 = jnp.min(jnp.where(le2 == m2, col, big), axis=-1, keepdims=True)
    e2 = jnp.exp(m2 - m1)
    den = 1.0 + e2
    w1 = p_top / den
    w2 = p_top * (e2 / den)
    gates = jnp.where(col == i1, w1, 0.0) + jnp.where(col == i2, w2, 0.0)
    return gates + jnp.where(col == 0, g_idx.astype(F32), 0.0)


def _route_rows(logits_t, n_groups, per_group):
    assert n_groups <= 8 and per_group == 8
    n_tok = logits_t.shape[1]
    row = lax.broadcasted_iota(jnp.int32, (8, n_tok), 0)
    neg = jnp.float32(-jnp.inf)
    big = jnp.int32(8)
    lg = jnp.where(row < n_groups, logits_t[0:8], neg)
    mg = jnp.max(lg, axis=0, keepdims=True)
    g_idx = jnp.min(jnp.where(lg == mg, row, big), axis=0, keepdims=True)
    p_top = 1.0 / jnp.sum(jnp.where(row < n_groups, jnp.exp(lg - mg), 0.0), axis=0, keepdims=True)
    sel = logits_t[8:16]
    for g in range(1, n_groups):
        sel = jnp.where(g_idx == g, logits_t[8 + 8 * g:16 + 8 * g], sel)
    m1 = jnp.max(sel, axis=0, keepdims=True)
    i1 = jnp.min(jnp.where(sel == m1, row, big), axis=0, keepdims=True)
    sel2 = jnp.where(row == i1, neg, sel)
    m2 = jnp.max(sel2, axis=0, keepdims=True)
    i2 = jnp.min(jnp.where(sel2 == m2, row, big), axis=0, keepdims=True)
    e2 = jnp.exp(m2 - m1)
    den = 1.0 + e2
    w1 = p_top / den
    w2 = p_top * (e2 / den)
    own = jnp.where(row == i1, w1, 0.0) + jnp.where(row == i2, w2, 0.0)
    blocks = [jnp.where(row == 0, g_idx.astype(F32), 0.0)]
    blocks += [jnp.where(g_idx == g, own, 0.0) for g in range(n_groups)]
    blocks += [jnp.zeros((8, n_tok), F32)] * (ROUTER_LANES // 8 - len(blocks))
    return jnp.concatenate(blocks, axis=0), g_idx


def _mixer_prompt_kernel(x_ref, lbw_ref, g1_ref, win_ref, hgg_ref, cw_ref, cb_ref, wx_ref, bx_ref,
                         wa_ref, ba_ref, lam_ref, wout_ref, g2_ref, wr_ref, br_ref,
                         x1_ref, xn2_ref, gates_ref, cnt_ref, sout_ref, hout_ref, cout_ref,
                         proj_s, k_s, o_s, st_s, xr_s, a_s, u_s, hcar_s, yb_out_s,
                         *, n_groups, per_group):
    lb_t = x_ref.shape[1]
    wa_w = k_s.shape[1]
    wb_w = a_s.shape[1]
    n_heads = wa_w // HEAD_DIM
    j = pl.program_id(1)
    nj = pl.num_programs(1)

    @pl.when(j == 0)
    def _():
        st_s[...] = jnp.zeros_like(st_s)
        hcar_s[...] = jnp.zeros_like(hcar_s)
        xr_s[0:8, :] = jnp.zeros((8, wb_w), F32)

    x = x_ref[0]
    xn = _rms(x, g1_ref[...]).astype(BF16)
    for c, w in ((wa_w, wa_w), (4 * wa_w, wb_w), (4 * wa_w + wb_w, wb_w), (0, wa_w), (2 * wa_w, wa_w),
                 (3 * wa_w, wa_w)):
        proj_s[:, c:c + w] = jnp.dot(xn, win_ref[:, c:c + w], preferred_element_type=F32)

    xb0 = 4 * wa_w
    xr_s[pl.ds(8, lb_t), :] = proj_s[:, xb0:xb0 + wb_w]
    xc = (cb_ref[...] + cw_ref[3:4, :] * xr_s[pl.ds(8, lb_t), :] + cw_ref[2:3, :] * xr_s[pl.ds(7, lb_t), :]
          + cw_ref[1:2, :] * xr_s[pl.ds(6, lb_t), :] + cw_ref[0:1, :] * xr_s[pl.ds(5, lb_t), :])
    tail = xr_s[pl.ds(lb_t + 5, 3), :]
    xr_s[5:8, :] = tail
    cout_ref[0] = tail
    xcb = xc.astype(BF16)
    gate_x = _sigmoid(jnp.dot(xcb, wx_ref[...], preferred_element_type=F32) + bx_ref[...])
    gate_a = _sigmoid(jnp.dot(xcb, wa_ref[...], preferred_element_type=F32) + ba_ref[...])
    log_a = (-LRU_C) * gate_a * _softplus(-lam_ref[...])
    a = jnp.exp(log_a)
    mult = jnp.sqrt((1.0 - a) * (1.0 + a))
    first =(lax.broadcasted_iota(jnp.int32, (lb_t, 1), 0) == 0) & (j == 0)
    a = jnp.where(first, 0.0, a)
    mult = jnp.where(first, 1.0, mult)
    a_s[...] = a
    u_s[...] = gate_x * xc * mult

    lb = _forget_lower_bound(lbw_ref[...])
    f = lb + (1.0 - lb) * _sigmoid(proj_s[:, wa_w:2 * wa_w])
    k_s[...] = 1.0 - f
    logf = jnp.log(f)
    r_i = lax.broadcasted_iota(jnp.int32, (CHUNK, CHUNK), 0)
    c_i = lax.broadcasted_iota(jnp.int32, (CHUNK, CHUNK), 1)
    tri = jnp.where(r_i >= c_i, 1.0, 0.0).astype(BF16)
    lf_hi = logf.astype(BF16)
    rest = logf - lf_hi.astype(F32)
    lf_mid = rest.astype(BF16)
    lf_lo = (rest - lf_mid.astype(F32)).astype(BF16)
    for c in range(0, lb_t, CHUNK):
        cum = [jnp.dot(tri, part[c:c + CHUNK, :], preferred_element_type=F32) for part in (lf_lo, lf_mid, lf_hi)]
        proj_s[c:c + CHUNK, wa_w:2 * wa_w] = LOG2E * ((cum[0] + cum[1]) + cum[2])

    row_sub = lax.broadcasted_iota(jnp.int32, (SUB, HEAD_DIM), 0)
    lane_sub = lax.broadcasted_iota(jnp.int32, (SUB, HEAD_DIM), 1)
    assert n_heads % 2 == 0
    r_kk = lax.broadcasted_iota(jnp.int32, (2 * HEAD_DIM, 2 * HEAD_DIM), 0)
    c_kk = lax.broadcasted_iota(jnp.int32, (2 * HEAD_DIM, 2 * HEAD_DIM), 1)
    ones_kk = jnp.where((r_kk < HEAD_DIM) == (c_kk < HEAD_DIM), 1.0, 0.0).astype(BF16)

    n_sub = CHUNK // SUB
    half = SUB // 2
    lower_left = (row_sub >= half) & (lane_sub < half)
    own_lane = jnp.where(row_sub >= half, half, 0)

    def chunk_start(r0):
        first = []
        for h in range(n_heads):
            hs = slice(h * HEAD_DIM, (h + 1) * HEAD_DIM)
            q = proj_s[pl.ds(r0, CHUNK), hs]
            b = proj_s[pl.ds(r0, CHUNK), wa_w + h * HEAD_DIM:wa_w + (h + 1) * HEAD_DIM]
            v = proj_s[pl.ds(r0, CHUNK), 2 * wa_w + h * HEAD_DIM:2 * wa_w + (h + 1) * HEAD_DIM]
            k = k_s[pl.ds(r0, CHUNK), hs]
            vb = v.astype(BF16)
            st = st_s[h]
            b_last = b[CHUNK - 1:CHUNK, :]
            o = lax.dot_general((q * jnp.exp2(b)).astype(BF16), st.astype(BF16), NT_DIMS,
                                preferred_element_type=F32)
            k_end = k * jnp.exp2(b_last - b)
            st_s[h] = st * jnp.exp2(b_last) + lax.dot_general(vb, k_end.astype(BF16), TN_DIMS,
                                                               preferred_element_type=F32)
            terms, off, mid = [], [], []
            for i in range(n_sub):
                lo = i * SUB
                qi, bi, ki = q[lo:lo + SUB], b[lo:lo + SUB], k[lo:lo + SUB]
                q3, b3, k3 = (a.reshape(2, half, HEAD_DIM) for a in (qi, bi, ki))
                terms += [(q3 * (k3[:, j:j + 1, :] * jnp.exp2(b3 - b3[:, j:j + 1, :]))).reshape(SUB, HEAD_DIM)
                          .astype(BF16) for j in range(half)]
                rm = bi[half - 1:half]
                mid.append(lax.dot_general((qi * jnp.exp2(bi - rm)).astype(BF16),
                                           (ki * jnp.exp2(rm - bi)).astype(BF16), NT_DIMS,
                                           preferred_element_type=F32))
                if i > 0:
                    r = b[lo - 1:lo]
                    qt = (qi * jnp.exp2(bi - r)).astype(BF16)
                    kt = (k[:lo] * jnp.exp2(r - b[:lo])).astype(BF16)
                    off.append(lax.dot_general(qt, kt, NT_DIMS, preferred_element_type=F32))
            first.append([o, vb, off, jnp.concatenate(terms, axis=0), mid])
        for h in range(0, n_heads, 2):
            both = jnp.concatenate([first[h][3], first[h + 1][3]], axis=1)
            sums = jnp.dot(both, ones_kk, preferred_element_type=F32)
            first[h][3] = sums[:, :HEAD_DIM]
            first[h + 1][3] = sums[:, HEAD_DIM:]
        return first

    def chunk_finish(r0, first):
        for h in range(n_heads):
            o, vb, off, sums, mid = first[h]
            outs = []
            for i in range(n_sub):
                lo = i * SUB
                sc = jnp.zeros((SUB, HEAD_DIM), F32)
                for j in range(half):
                    row0 = (i * half + j) * SUB
                    sc = jnp.where(lane_sub == own_lane + j, sums[row0:row0 + SUB], sc)
                sc = jnp.where(row_sub >= lane_sub, sc, 0.0)[:, :SUB]
                sc = jnp.where(lower_left[:, :SUB], mid[i], sc)
                od = o[lo:lo + SUB] + jnp.dot(sc.astype(BF16), vb[lo:lo + SUB], preferred_element_type=F32)
                if i > 0:
                    od = od + jnp.dot(off[i - 1].astype(BF16), vb[:lo], preferred_element_type=F32)
                outs.append(od)
            o_s[pl.ds(r0, CHUNK), h * HEAD_DIM:(h + 1) * HEAD_DIM] = jnp.concatenate(outs, axis=0)

    def chunks_body(ci, carry):
        rows = [pl.multiple_of((ci * UNROLL + u) * CHUNK, CHUNK) for u in range(UNROLL)]
        started = [chunk_start(r0) for r0 in rows]
        for r0, first in zip(rows, started):
            chunk_finish(r0, first)
        return carry

    row8 = lax.broadcasted_iota(jnp.int32, (8, 1), 0)

    def scan_body(gi, carry):
        r0 = pl.multiple_of(gi * 8, 8)
        aa = a_s[pl.ds(r0, 8), :]
        uu = u_s[pl.ds(r0, 8), :]
        for s in (1, 2, 4):
            m = row8 >= s
            uu = jnp.where(m, aa * pltpu.roll(uu, s, 0) + uu, uu)
            aa = jnp.where(m, aa * pltpu.roll(aa, s, 0), aa)
        hh = aa * carry + uu
        u_s[pl.ds(r0, 8), :] = hh
        return hh[7:8, :]

    h_last = lax.fori_loop(0, lb_t // 8, scan_body, hcar_s[...])
    hcar_s[...] = h_last
    hout_ref[0] = h_last
    yb = (u_s[...] * _gelu_tanh(proj_s[:, xb0 + wb_w:xb0 + 2 * wb_w])).astype(BF16)
    yb_out_s[...] = jnp.dot(yb, wout_ref[wa_w:, :], preferred_element_type=F32)

    assert lb_t % (CHUNK * UNROLL) == 0
    lax.fori_loop(0, lb_t // (CHUNK * UNROLL), chunks_body, 0)

    ya = []
    for h in range(n_heads):
        hs = slice(h * HEAD_DIM, (h + 1) * HEAD_DIM)
        oh = o_s[:, hs]
        oh = oh * lax.rsqrt(jnp.mean(oh * oh, axis=-1, keepdims=True) + EPS) * hgg_ref[:, hs]
        ya.append(oh * _silu(proj_s[:, 3 * wa_w + h * HEAD_DIM:3 * wa_w + (h + 1) * HEAD_DIM]))

    ya = jnp.concatenate(ya, axis=-1).astype(BF16)
    x1 = x + (jnp.dot(ya, wout_ref[:wa_w, :], preferred_element_type=F32) + yb_out_s[...])
    x1_ref[0] = x1
    xn2 = _rms(x1, g2_ref[...]).astype(BF16)
    xn2_ref[0] = xn2
    logits_t = lax.dot_general(wr_ref[...], xn2, NT_DIMS, preferred_element_type=F32) + br_ref[...]
    gates_t, g_idx = _route_rows(logits_t, n_groups, per_group)
    gates_ref[0] = gates_t.T
    lane_c = lax.broadcasted_iota(jnp.int32, (1, ROUTER_LANES), 1)
    for t in range(lb_t // TOK_TILE):
        gi = g_idx[:, t * TOK_TILE:(t + 1) * TOK_TILE]
        cnt = jnp.zeros((1, ROUTER_LANES), jnp.int32)
        for g in range(n_groups):
            cnt = cnt + jnp.where(lane_c == g, jnp.sum(jnp.where(gi == g, 1, 0), axis=1, keepdims=True), 0)
        cnt_ref[0, t:t + 1, :] = cnt

    @pl.when(j == nj - 1)
    def _():
        for h in range(n_heads):
            sout_ref[0, h] = st_s[h].T


def _const_spec(shape):
    nd = len(shape)
    return pl.BlockSpec(shape, lambda *_: (0,) * nd)


def _mixer_prompt(x, p, lb_t):
    bsz, seq, d = x.shape
    wa_w = p["hgg"].shape[1]
    wb_w = p["cb"].shape[1]
    n_heads = wa_w // HEAD_DIM
    n_cols = p["win_bf"].shape[1]
    weights = [p["lbw"], p["g1"], p["win_bf"], p["hgg"], p["cw"], p["cb"], p["wx_bf"], p["bx"],
               p["wa_bf"], p["ba"], p["lam"], p["wout_bf"], p["g2"], p["wr_t_bf"], p["br_t"]]
    nj = seq // lb_t
    tile = lambda w: pl.BlockSpec((1, lb_t, w), lambda b, j: (b, j, 0))
    out_shape = (
        jax.ShapeDtypeStruct((bsz, seq, d), F32),
        jax.ShapeDtypeStruct((bsz, seq, d), BF16),
        jax.ShapeDtypeStruct((bsz, seq, ROUTER_LANES), F32),
        jax.ShapeDtypeStruct((bsz * (seq // lb_t), lb_t // TOK_TILE, ROUTER_LANES), jnp.int32),
        jax.ShapeDtypeStruct((bsz, n_heads, HEAD_DIM, HEAD_DIM), F32),
        jax.ShapeDtypeStruct((bsz, 1, wb_w), F32),
        jax.ShapeDtypeStruct((bsz, 3, wb_w), F32),
    )
    out_specs = (
        tile(d), tile(d), tile(ROUTER_LANES),
        pl.BlockSpec((1, lb_t // TOK_TILE, ROUTER_LANES), lambda b, j: (b * nj + j, 0, 0)),
        pl.BlockSpec((1, n_heads, HEAD_DIM, HEAD_DIM), lambda b, j: (b, 0, 0, 0)),
        pl.BlockSpec((1, 1, wb_w), lambda b, j: (b, 0, 0)),
        pl.BlockSpec((1, 3, wb_w), lambda b, j: (b, 0, 0)),
    )
    scratch = [
        pltpu.VMEM((lb_t, n_cols), F32),
        pltpu.VMEM((lb_t, wa_w), F32),
        pltpu.VMEM((lb_t, wa_w), F32),
        pltpu.VMEM((n_heads, HEAD_DIM, HEAD_DIM), F32),
        pltpu.VMEM((lb_t + 8, wb_w), F32),
        pltpu.VMEM((lb_t, wb_w), F32),
        pltpu.VMEM((lb_t, wb_w), F32),
        pltpu.VMEM((1, wb_w), F32),
        pltpu.VMEM((lb_t, d), F32),
    ]
    kern = functools.partial(_mixer_prompt_kernel, n_groups=p["n_groups"], per_group=p["per_group"])
    return pl.pallas_call(
        kern,
        grid=(bsz, nj),
        in_specs=[tile(d)] + [_const_spec(w.shape) for w in weights],
        out_specs=out_specs,
        out_shape=out_shape,
        scratch_shapes=scratch,
        compiler_params=pltpu.CompilerParams(dimension_semantics=("arbitrary", "arbitrary"),
                                             vmem_limit_bytes=VMEM_LIMIT_BYTES),
        name="mixer_prompt",
    )(x, *weights)


def _moe_kernel(x1_ref, xn2_ref, gates_ref, w1_ref, w3_ref, w2_ref, gf_ref, y_ref, acc_s, *, n_groups):
    e = pl.program_id(1)

    @pl.when(e == 0)
    def _():
        acc_s[...] = jnp.zeros_like(acc_s)

    xb = xn2_ref[...]
    per_step = w1_ref.shape[0]
    gates = pltpu.roll(gates_ref[...], ROUTER_LANES - n_groups - per_step * e, 1)
    acc = acc_s[...]
    for s in range(per_step):
        h = (_silu(jnp.dot(xb, w1_ref[s], preferred_element_type=F32))
             * jnp.dot(xb, w3_ref[s], preferred_element_type=F32) * gates[:, s:s + 1])
        acc = acc + jnp.dot(h.astype(BF16), w2_ref[s], preferred_element_type=F32)
    acc_s[...] = acc

    @pl.when(e == pl.num_programs(1) - 1)
    def _():
        y_ref[...] = _rms(x1_ref[...] + acc_s[...], gf_ref[...])


def _moe_dense(x1, xn2, gates, p, tm):
    t, d = x1.shape
    n_exp, _, d_exp = p["w1_bf"].shape
    per_step = p["per_group"]
    row = lambda w: pl.BlockSpec((tm, w), lambda i, e: (i, 0))
    return pl.pallas_call(
        functools.partial(_moe_kernel, n_groups=p["n_groups"]),
        grid=(t // tm, n_exp // per_step),
        in_specs=[row(d), row(d), row(ROUTER_LANES),
                  pl.BlockSpec((per_step, d, d_exp), lambda i, e: (e, 0, 0)),
                  pl.BlockSpec((per_step, d, d_exp), lambda i, e: (e, 0, 0)),
                  pl.BlockSpec((per_step, d_exp, d), lambda i, e: (e, 0, 0)),
                  _const_spec(p["gf"].shape)],
        out_specs=row(d),
        out_shape=jax.ShapeDtypeStruct((t, d), F32),
        scratch_shapes=[pltpu.VMEM((tm, d), F32)],
        compiler_params=pltpu.CompilerParams(dimension_semantics=("arbitrary", "arbitrary"),
                                             vmem_limit_bytes=VMEM_LIMIT_BYTES),
        name="moe_dense",
    )(x1, xn2, gates, p["w1_bf"], p["w3_bf"], p["w2_bf"], p["gf"])


def _seg_pad(n):
    return jnp.bitwise_and(n + (SEG - 1), -SEG)


def _row_tile_pad(n):
    return jnp.bitwise_and(n + (ROW_TILE - 1), -ROW_TILE)


def _tile_segments(cnt_ref, tile, n_groups):
    padded = [_seg_pad(cnt_ref[tile * n_groups + g]) for g in range(n_groups)]
    starts, acc = [], jnp.int32(0)
    for g in range(n_groups):
        starts.append(acc)
        acc = acc + padded[g]
    return padded, starts, acc


def _group_bases(cnt_ref, n_tiles, n_groups):
    def body(t, tot):
        return tuple(tot[g] + _seg_pad(cnt_ref[t * n_groups + g]) for g in range(n_groups))
    totals = lax.fori_loop(0, n_tiles, body, (jnp.int32(0),) * n_groups)
    bases, ends, acc = [], [], jnp.int32(0)
    for g in range(n_groups):
        bases.append(acc)
        acc = acc + _row_tile_pad(totals[g])
        ends.append(acc)
    return bases, ends


def _chunk_row(c, padded, starts, gstart):
    lo = c * SEG
    row = jnp.int32(0)
    for g in range(len(padded)):
        inside = (lo >= starts[g]) & (lo < starts[g] + padded[g])
        row = row + jnp.where(inside, gstart[g] + lo - starts[g], 0)
    return pl.multiple_of(row, SEG)


def _sort_matrix(gates, starts):
    n_groups = len(starts)
    col = lax.broadcasted_iota(jnp.int32, (TOK_TILE, ROUTER_LANES), 1)
    gi = gates[:, 0:1].astype(jnp.int32)
    onehot = col == gi
    r_i = lax.broadcasted_iota(jnp.int32, (TOK_TILE, TOK_TILE), 0)
    c_i = lax.broadcasted_iota(jnp.int32, (TOK_TILE, TOK_TILE), 1)
    earlier = jnp.where(r_i > c_i, 1.0, 0.0).astype(BF16)
    before = jnp.dot(earlier, jnp.where(onehot, 1.0, 0.0).astype(BF16), preferred_element_type=F32)
    rank = jnp.sum(jnp.where(onehot, before, 0.0), axis=-1, keepdims=True).astype(jnp.int32)
    base = jnp.zeros((TOK_TILE, 1), jnp.int32)
    for g in range(n_groups):
        base = base + jnp.where(gi == g, starts[g], 0)
    lane = lax.broadcasted_iota(jnp.int32, (TOK_TILE, SORT_ROWS), 1)
    return lane == base + rank


def _sort_matrix_rows(gates, starts):
    n_groups = len(starts)
    col = lax.broadcasted_iota(jnp.int32, (TOK_TILE, ROUTER_LANES), 1)
    onehot_t = jnp.where(col == gates[:, 0:1].astype(jnp.int32), 1.0, 0.0).T
    r_i = lax.broadcasted_iota(jnp.int32, (TOK_TILE, TOK_TILE), 0)
    c_i = lax.broadcasted_iota(jnp.int32, (TOK_TILE, TOK_TILE), 1)
    later = jnp.where(r_i < c_i, 1.0, 0.0).astype(BF16)
    before = jnp.dot(onehot_t.astype(BF16), later, preferred_element_type=F32)
    g_row = lax.broadcasted_iota(jnp.int32, (ROUTER_LANES, 1), 0)
    base = jnp.zeros((ROUTER_LANES, 1), jnp.int32)
    for g in range(n_groups):
        base = base + jnp.where(g_row == g, starts[g], 0)
    dest = jnp.sum(jnp.where(onehot_t > 0.0, before + base.astype(F32), 0.0), axis=0, keepdims=True)
    row = lax.broadcasted_iota(jnp.int32, (SORT_ROWS, TOK_TILE), 0)
    return row == dest.astype(jnp.int32)


def _dispatch_copies(xbuf, gbuf, xs_hbm, gs_hbm, sem, slot, c, row):
    return (pltpu.make_async_copy(xbuf.at[slot, pl.ds(c * SEG, SEG), :], xs_hbm.at[pl.ds(row, SEG), :], sem.at[slot]),
            pltpu.make_async_copy(gbuf.at[slot, pl.ds(c * SEG, SEG), :], gs_hbm.at[pl.ds(row, SEG), :], sem.at[slot]))


def _moe_dispatch_kernel(cnt_ref, xn2_ref, gates_ref, xs_in, gs_in, xs_hbm, gs_hbm, tg_ref,
                         xbuf, gbuf, sem, gstart_s, *, n_groups, per_group):
    del xs_in, gs_in
    j = pl.program_id(0)
    n_tiles = pl.num_programs(0)
    n_chunks = (TOK_TILE + n_groups * SEG) // SEG
    slot = lax.rem(j, 2)

    @pl.when(j == 0)
    def _():
        bases, ends = _group_bases(cnt_ref, n_tiles, n_groups)
        for g in range(n_groups):
            gstart_s[g] = bases[g]
        n_rt = tg_ref.shape[0] - 1
        for i in range(n_rt):
            tg = jnp.int32(0)
            for g in range(n_groups):
                tg = tg + jnp.where(ends[g] <= i * ROW_TILE, 1, 0)
            tg_ref[i] = tg
        tg_ref[n_rt] = ends[n_groups - 1] // ROW_TILE

    def wait_tile(tile, slot_):
        _, _, used = _tile_segments(cnt_ref, tile, n_groups)
        for c in range(n_chunks):
            @pl.when(c * SEG < used)
            def _():
                for cp in _dispatch_copies(xbuf, gbuf, xs_hbm, gs_hbm, sem, slot_, c, 0):
                    cp.wait()

    @pl.when(j >= 2)
    def _():
        wait_tile(j - 2, slot)

    padded, starts, used = _tile_segments(cnt_ref, j, n_groups)
    gates = gates_ref[...]
    d = xn2_ref.shape[1]
    g_hi = gates.astype(BF16)
    g_lo = (gates - g_hi.astype(F32)).astype(BF16)
    sort_m = jnp.where(_sort_matrix_rows(gates, starts), 1.0, 0.0).astype(BF16)
    moved = jnp.dot(sort_m, jnp.concatenate([xn2_ref[...], g_hi, g_lo], axis=1), preferred_element_type=F32)
    xbuf[slot] = moved[:, :d].astype(BF16)
    gbuf[slot] = moved[:, d:d + ROUTER_LANES] + moved[:, d + ROUTER_LANES:]
    gstart = [gstart_s[g] for g in range(n_groups)]
    for c in range(n_chunks):
        @pl.when(c * SEG < used)
        def _():
            for cp in _dispatch_copies(xbuf, gbuf, xs_hbm, gs_hbm, sem, slot, c,
                                       _chunk_row(c, padded, starts, gstart)):
                cp.start()
    for g in range(n_groups):
        gstart_s[g] = gstart[g] + padded[g]

    @pl.when(j == n_tiles - 1)
    def _():
        @pl.when(j >= 1)
        def _():
            wait_tile(j - 1, 1 - slot)
        wait_tile(j, slot)


def _moe_group_kernel(tg_ref, xs_ref, gs_ref, w1_ref, w3_ref, w2_ref, ys_ref, *, n_groups, per_group):
    i = pl.program_id(0)

    @pl.when(tg_ref[i] < n_groups)
    def _():
        xb = xs_ref[...]
        gates = pltpu.roll(gs_ref[...], ROUTER_LANES - PROMPT_EXPERT_LANE - per_group * tg_ref[i], 1)
        acc = jnp.zeros(ys_ref.shape, F32)
        for s in range(per_group):
            h = (_silu(jnp.dot(xb, w1_ref[s], preferred_element_type=F32))
                 * jnp.dot(xb, w3_ref[s], preferred_element_type=F32) * gates[:, s:s + 1])
            acc = acc + jnp.dot(h.astype(BF16), w2_ref[s], preferred_element_type=F32)
        ys_ref[...] = acc.astype(BF16)


def _moe_combine_kernel(cnt_ref, gates_ref, x1_ref, gf_ref, ys_hbm, y_ref, ybuf, sem, gstart_s, *, n_groups):
    j = pl.program_id(0)
    n_tiles = pl.num_programs(0)
    n_chunks = SORT_ROWS // SEG
    slot = lax.rem(j, 2)

    def copies(slot_, c, row):
        return pltpu.make_async_copy(ys_hbm.at[pl.ds(row, SEG), :], ybuf.at[slot_, pl.ds(c * SEG, SEG), :],
                                     sem.at[slot_])

    def fetch_tile(tile, slot_):
        padded, starts, _ = _tile_segments(cnt_ref, tile, n_groups)
        gstart = [gstart_s[g] for g in range(n_groups)]
        for c in range(n_chunks):
            copies(slot_, c, _chunk_row(c, padded, starts, gstart)).start()
        for g in range(n_groups):
            gstart_s[g] = gstart[g] + padded[g]

    @pl.when(j == 0)
    def _():
        bases, _ = _group_bases(cnt_ref, n_tiles, n_groups)
        for g in range(n_groups):
            gstart_s[g] = bases[g]
        fetch_tile(0, 0)

    @pl.when(j + 1 < n_tiles)
    def _():
        fetch_tile(j + 1, 1 - slot)

    for c in range(n_chunks):
        copies(slot, c, 0).wait()
    _, starts, _ = _tile_segments(cnt_ref, j, n_groups)
    sort_t = _sort_matrix(gates_ref[...], starts)
    moe = jnp.dot(jnp.where(sort_t, 1.0, 0.0).astype(BF16), ybuf[slot], preferred_element_type=F32)
    y_ref[...] = _rms(x1_ref[...] + moe, gf_ref[...])


def _moe_sorted(x1, xn2, gates, counts, p):
    t, d = x1.shape
    n_groups, per_group = p["n_groups"], p["per_group"]
    n_tiles = t // TOK_TILE
    d_exp = p["w1_bf"].shape[2]
    cap = t + n_tiles * n_groups * SEG + n_groups * ROW_TILE
    n_rt = -(-cap // ROW_TILE)
    rows = n_rt * ROW_TILE
    cnt = counts[:, :, :n_groups].reshape(-1)
    params = pltpu.CompilerParams(dimension_semantics=("arbitrary",), vmem_limit_bytes=VMEM_LIMIT_BYTES)
    any_spec = pl.BlockSpec(memory_space=pl.ANY)
    tok = lambda w: pl.BlockSpec((TOK_TILE, w), lambda j, c: (j, 0))

    xs, gs, tile_group = pl.pallas_call(
        functools.partial(_moe_dispatch_kernel, n_groups=n_groups, per_group=per_group),
        grid_spec=pltpu.PrefetchScalarGridSpec(
            num_scalar_prefetch=1, grid=(n_tiles,),
            in_specs=[tok(d), tok(ROUTER_LANES), any_spec, any_spec],
            out_specs=(any_spec, any_spec, pl.BlockSpec(memory_space=pltpu.SMEM)),
            scratch_shapes=[pltpu.VMEM((2, SORT_ROWS, d), BF16), pltpu.VMEM((2, SORT_ROWS, ROUTER_LANES), F32),
                            pltpu.SemaphoreType.DMA((2,)), pltpu.SMEM((n_groups,), jnp.int32)]),
        out_shape=(jax.ShapeDtypeStruct((rows, d), BF16), jax.ShapeDtypeStruct((rows, ROUTER_LANES), F32),
                   jax.ShapeDtypeStruct((n_rt + 1,), jnp.int32)),
        input_output_aliases={3: 0, 4: 1},
        compiler_params=params, name="moe_dispatch",
    )(cnt, xn2, gates, jnp.zeros((rows, d), BF16), jnp.zeros((rows, ROUTER_LANES), F32))

    used_tile = lambda i, tg: (jnp.minimum(i, tg[n_rt] - 1), 0)
    group_w = lambda i, tg: (jnp.minimum(tg[i], n_groups - 1), 0, 0)
    ys = pl.pallas_call(
        functools.partial(_moe_group_kernel, n_groups=n_groups, per_group=per_group),
        grid_spec=pltpu.PrefetchScalarGridSpec(
            num_scalar_prefetch=1, grid=(n_rt,),
            in_specs=[pl.BlockSpec((ROW_TILE, d), used_tile), pl.BlockSpec((ROW_TILE, ROUTER_LANES), used_tile),
                      pl.BlockSpec((per_group, d, d_exp), group_w), pl.BlockSpec((per_group, d, d_exp), group_w),
                      pl.BlockSpec((per_group, d_exp, d), group_w)],
            out_specs=pl.BlockSpec((ROW_TILE, d), used_tile)),
        out_shape=jax.ShapeDtypeStruct((rows, d), BF16),
        input_output_aliases={1: 0},
        compiler_params=params, name="moe_experts",
    )(tile_group, xs, gs, p["w1_bf"], p["w3_bf"], p["w2_bf"])

    return pl.pallas_call(
        functools.partial(_moe_combine_kernel, n_groups=n_groups),
        grid_spec=pltpu.PrefetchScalarGridSpec(
            num_scalar_prefetch=1, grid=(n_tiles,),
            in_specs=[tok(ROUTER_LANES), tok(d), pl.BlockSpec(p["gf"].shape, lambda j, c: (0, 0)), any_spec],
            out_specs=tok(d),
            scratch_shapes=[pltpu.VMEM((2, SORT_ROWS, d), BF16), pltpu.SemaphoreType.DMA((2,)),
                            pltpu.SMEM((n_groups,), jnp.int32)]),
        out_shape=jax.ShapeDtypeStruct((t, d), F32),
        compiler_params=params, name="moe_combine",
    )(cnt, gates, x1, p["gf"], ys)


def _sample_in_kernel(x_ref, c0_ref, c1_ref, c2_ref, h0_ref, lbw_ref, g1_ref, win_ref, cw_ref, cb_ref,
                      wx_ref, bx_ref, wa_ref, ba_ref, lam_ref,
                      q_ref, f_ref, v_ref, g_ref, yb_ref, hnew_ref, xr_ref):
    wa_w = v_ref.shape[1]
    wb_w = yb_ref.shape[1]
    xn = _rms(x_ref[...], g1_ref[...])
    proj = jnp.dot(xn, win_ref[...], precision=HIGHEST, preferred_element_type=F32)
    lb = _forget_lower_bound(lbw_ref[...])
    f = lb + (1.0 - lb) * _sigmoid(proj[:, wa_w:2 * wa_w])
    q_ref[...] = proj[:, 0:wa_w].T
    f_ref[...] = f.T
    v_ref[...] = proj[:, 2 * wa_w:3 * wa_w]
    g_ref[...] = proj[:, 3 * wa_w:4 * wa_w]
    xr = proj[:, 4 * wa_w:4 * wa_w + wb_w]
    xr_ref[...] = xr
    xc = (cb_ref[...] + cw_ref[0:1, :] * c0_ref[...] + cw_ref[1:2, :] * c1_ref[...]
          + cw_ref[2:3, :] * c2_ref[...] + cw_ref[3:4, :] * xr)
    gate_x = _sigmoid(jnp.dot(xc, wx_ref[...], precision=HIGHEST, preferred_element_type=F32) + bx_ref[...])
    gate_a = _sigmoid(jnp.dot(xc, wa_ref[...], precision=HIGHEST, preferred_element_type=F32) + ba_ref[...])
    log_a = (-LRU_C) * gate_a * _softplus(-lam_ref[...])
    a = jnp.exp(log_a)
    mult = jnp.sqrt(-_expm1(2.0 * log_a))
    h = a * h0_ref[...] + gate_x * xc * mult
    hnew_ref[...] = h
    yb_ref[...] = h * _gelu_tanh(proj[:, 4 * wa_w + wb_w:4 * wa_w + 2 * wb_w])


def _sample_state_kernel(s_ref, qt_ref, ft_ref, v_ref, snew_ref, o_ref):
    tb, n_heads = s_ref.shape[0], s_ref.shape[1]
    n_tok = qt_ref.shape[1]
    shift = lax.rem(n_tok - pl.program_id(0) * tb, n_tok)
    sq = (HEAD_DIM, HEAD_DIM)
    for h in range(n_heads):
        hs = slice(h * HEAD_DIM, (h + 1) * HEAD_DIM)
        qh = pltpu.roll(qt_ref[hs, :], shift, 1)
        fh = pltpu.roll(ft_ref[hs, :], shift, 1)
        rows = []
        for t in range(tb):
            f_all = jnp.broadcast_to(fh[:, t:t + 1], sq)
            s_new = f_all * s_ref[t, h] + (1.0 - f_all) * v_ref[t:t + 1, hs]
            snew_ref[t, h] = s_new
            rows.append(jnp.sum(jnp.broadcast_to(qh[:, t:t + 1], sq) * s_new, axis=0, keepdims=True))
        o_ref[:, hs] = jnp.concatenate(rows, axis=0)


def _sample_out_kernel(x_ref, o_ref, g_ref, yb_ref, hgg_ref, wout_ref, g2_ref, wr_ref, br_ref,
                       x1_ref, xn2_ref, gates_ref, *, n_groups, per_group):
    wa_w = o_ref.shape[1]
    ya = []
    for h in range(wa_w // HEAD_DIM):
        hs = slice(h * HEAD_DIM, (h + 1) * HEAD_DIM)
        oh = o_ref[:, hs]
        oh = oh * lax.rsqrt(jnp.mean(oh * oh, axis=-1, keepdims=True) + EPS) * hgg_ref[:, hs]
        ya.append(oh * _silu(g_ref[:, hs]))
    y = jnp.concatenate(ya + [yb_ref[...]], axis=-1)
    x1 = x_ref[...] + jnp.dot(y, wout_ref[...], precision=HIGHEST, preferred_element_type=F32)
    x1_ref[...] = x1
    xn2 = _rms(x1, g2_ref[...])
    xn2_ref[...] = xn2.astype(BF16)
    logits = jnp.dot(xn2, wr_ref[...], precision=HIGHEST, preferred_element_type=F32) + br_ref[...]
    gates_ref[...] = _route(logits, n_groups, per_group)


def _whole(kernel, out_shape, *args, name):
    return pl.pallas_call(
        kernel, out_shape=out_shape,
        compiler_params=pltpu.CompilerParams(vmem_limit_bytes=VMEM_LIMIT_BYTES), name=name)(*args)


def _mixer_sample(x, s0, h0, c0, p, tb):
    n, d = x.shape
    wa_w = p["hgg"].shape[1]
    wb_w = p["cb"].shape[1]
    n_heads = wa_w // HEAD_DIM
    sd = lambda w: jax.ShapeDtypeStruct((n, w), F32)
    key_major = jax.ShapeDtypeStruct((wa_w, n), F32)
    q, f, v, g, yb, h_new, xr = _whole(
        _sample_in_kernel, (key_major,) * 2 + (sd(wa_w),) * 2 + (sd(wb_w),) * 3,
        x, c0[:, 0, :], c0[:, 1, :], c0[:, 2, :], h0, p["lbw"], p["g1"], p["win"], p["cw"], p["cb"],
        p["wx"], p["bx"], p["wa"], p["ba"], p["lam"], name="sample_in")
    tok = lambda w: pl.BlockSpec((tb, w), lambda i: (i, 0))
    st = pl.BlockSpec((tb, n_heads, HEAD_DIM, HEAD_DIM), lambda i: (i, 0, 0, 0))
    s_new, o = pl.pallas_call(
        _sample_state_kernel,
        grid=(n // tb,),
        in_specs=[st, _const_spec((wa_w, n)), _const_spec((wa_w, n)), tok(wa_w)],
        out_specs=(st, tok(wa_w)),
        out_shape=(jax.ShapeDtypeStruct(s0.shape, F32), sd(wa_w)),
        compiler_params=pltpu.CompilerParams(dimension_semantics=("arbitrary",),
                                             vmem_limit_bytes=VMEM_LIMIT_BYTES),
        name="sample_state",
    )(s0, q, f, v)
    x1, xn2, gates = _whole(
        functools.partial(_sample_out_kernel, n_groups=p["n_groups"], per_group=p["per_group"]),
        (sd(d), jax.ShapeDtypeStruct((n, d), BF16), sd(ROUTER_LANES)),
        x, o, g, yb, p["hgg"], p["wout"], p["g2"], p["wr"], p["br"], name="sample_out")
    c_new = jnp.stack([c0[:, 1, :], c0[:, 2, :], xr], axis=1)
    return x1, xn2, gates, s_new, h_new, c_new


def _block_diag(w):
    n, c, _ = w.shape
    eye = jnp.eye(n, dtype=w.dtype)
    return (w[:, :, None, :] * eye[:, None, :, None]).reshape(n * c, n * c)


def _prepare(lower_bounds, ln1_g, w_in, hgrn_norm_g, conv_w, conv_b, lru_wx, lru_bx, lru_wa, lru_ba,
             lru_lambda, w_out, ln2_g, router_wg, router_bg, router_we, router_be, exp_w1, exp_w3,
             exp_w2, final_g):
    d = w_in.shape[1]
    n_groups = router_wg.shape[-1]
    per_group = router_we.shape[-1]
    row = lambda a: a.reshape(1, -1).astype(F32)
    we = jnp.transpose(router_we[0], (1, 0, 2)).reshape(d, n_groups * per_group)
    pad = ROUTER_LANES - n_groups - n_groups * per_group
    wr = jnp.concatenate([router_wg[0], we, jnp.zeros((d, pad), F32)], axis=1)
    br = jnp.concatenate([router_bg[0], router_be[0].reshape(-1), jnp.zeros((pad,), F32)]).reshape(1, -1)
    rows_t = -(-(8 + n_groups * per_group) // 16) * 16
    wr_t = jnp.concatenate([router_wg[0].T, jnp.zeros((8 - n_groups, d), F32), we.T,
                            jnp.zeros((rows_t - 8 - n_groups * per_group, d), F32)], axis=0)
    br_t = jnp.concatenate([router_bg[0], jnp.zeros((8 - n_groups,), F32), router_be[0].reshape(-1),
                            jnp.zeros((rows_t - 8 - n_groups * per_group,), F32)]).reshape(-1, 1)
    wx = _block_diag(lru_wx[0])
    wa = _block_diag(lru_wa[0])
    return dict(
        wr_t_bf=wr_t.astype(BF16), br_t=br_t,
        n_groups=n_groups, per_group=per_group,
        lbw=lower_bounds.astype(F32), g1=row(ln1_g[0]), win=w_in[0], win_bf=w_in[0].astype(BF16),
        hgg=row(hgrn_norm_g[0]), cw=conv_w[0], cb=row(conv_b[0]),
        wx=wx, wx_bf=wx.astype(BF16), bx=row(lru_bx[0]), wa=wa, wa_bf=wa.astype(BF16), ba=row(lru_ba[0]),
        lam=row(lru_lambda[0]), wout=w_out[0], wout_bf=w_out[0].astype(BF16), g2=row(ln2_g[0]),
        wr=wr, br=br, w1_bf=exp_w1[0].astype(BF16), w3_bf=exp_w3[0].astype(BF16),
        w2_bf=exp_w2[0].astype(BF16), gf=row(final_g))


def kernel(x_prompt, x_sample, state_hgrn, state_rglru, state_conv, lower_bounds, ln1_g, w_in, hgrn_norm_g, conv_w, conv_b, lru_wx, lru_bx, lru_wa, lru_ba, lru_lambda, w_out, ln2_g, router_wg, router_bg, router_we, router_be, exp_w1, exp_w3, exp_w2, final_g):
    assert w_in.shape[0] == 1, "single-layer trunk"
    p = _prepare(lower_bounds, ln1_g, w_in, hgrn_norm_g, conv_w, conv_b, lru_wx, lru_bx, lru_wa, lru_ba,
                 lru_lambda, w_out, ln2_g, router_wg, router_bg, router_we, router_be, exp_w1, exp_w3,
                 exp_w2, final_g)
    bsz, seq, d = x_prompt.shape
    lb_t = min(512, seq)
    x1, xn2, gates, counts, s_p, h_p, c_p = _mixer_prompt(x_prompt, p, lb_t)
    t = bsz * seq
    y_p = _moe_sorted(x1.reshape(t, d), xn2.reshape(t, d), gates.reshape(t, ROUTER_LANES), counts, p)

    n = x_sample.shape[0]
    x1s, xn2s, gates_s, s_s, h_s, c_s = _mixer_sample(x_sample[:, 0, :], state_hgrn[0], state_rglru[0],
                                                      state_conv[0], p, 8)
    y_s = _moe_dense(x1s, xn2s, gates_s, p, n)
    return (y_p.reshape(bsz, seq, d), y_s.reshape(n, 1, d),
            s_p[None], h_p.reshape(1, bsz, -1), c_p[None],
            s_s[None], h_s[None], c_s[None])
```

```python
import functools

import jax
import jax.numpy as jnp
from jax import lax
from jax.experimental import pallas as pl
from jax.experimental.pallas import tpu as pltpu

F32 = jnp.float32
BF16 = jnp.bfloat16
HIGHEST = lax.Precision.HIGHEST

EPS = 1e-6
LRU_C = 8.0
LOG2E = 1.4426950408889634
HEAD_DIM = 128
CHUNK = 64
SUB = 16
UNROLL = 4
ROUTER_LANES = 128
PROMPT_EXPERT_LANE = 8
TOK_TILE = 512
SEG = 16
SORT_ROWS = 640
ROW_TILE = 512
VMEM_LIMIT_BYTES = 56 * 1024 * 1024

NT_DIMS = (((1,), (1,)), ((), ()))
TN_DIMS = (((0,), (0,)), ((), ()))


def _rms(x, g):
    return x * lax.rsqrt(jnp.mean(x * x, axis=-1, keepdims=True) + EPS) * g


def _sigmoid(x):
    return 1.0 / (1.0 + jnp.exp(-x))


def _silu(x):
    return x * _sigmoid(x)


def _gelu_tanh(x):
    c = 0.7978845608028654
    return x * (0.5 * (1.0 + jnp.tanh(c * (x + 0.044715 * (x * x * x)))))


def _softplus(z):
    return jnp.maximum(z, 0.0) + jnp.log1p(jnp.exp(-jnp.abs(z)))


def _expm1(x):
    u = jnp.exp(x)
    um1 = u - 1.0
    small = um1 * x / jnp.log(u)
    return jnp.where(um1 == 0.0, x, jnp.where(jnp.abs(x) < 0.5, small, um1))


def _forget_lower_bound(lbw):
    m = jnp.max(lbw, axis=0, keepdims=True)
    e = jnp.exp(lbw - m)
    return e[0:1, :] / jnp.sum(e, axis=0, keepdims=True)


def _route(logits, n_groups, per_group):
    n = logits.shape[-1]
    col = lax.broadcasted_iota(jnp.int32, logits.shape, 1)
    neg = jnp.float32(-jnp.inf)
    big = jnp.int32(n)
    is_g = col < n_groups
    lg = jnp.where(is_g, logits, neg)
    mg = jnp.max(lg, axis=-1, keepdims=True)
    g_idx = jnp.min(jnp.where(lg == mg, col, big), axis=-1, keepdims=True)
    p_top = 1.0 / jnp.sum(jnp.where(is_g, jnp.exp(logits - mg), 0.0), axis=-1, keepdims=True)
    lo = n_groups + per_group * g_idx
    le = jnp.where((col >= lo) & (col < lo + per_group), logits, neg)
    m1 = jnp.max(le, axis=-1, keepdims=True)
    i1 = jnp.min(jnp.where(le == m1, col, big), axis=-1, keepdims=True)
    le2 = jnp.where(col == i1, neg, le)
    m2 = jnp.max(le2, axis=-1, keepdims=True)
    i2 = jnp.min(jnp.where(le2 == m2, col, big), axis=-1, keepdims=True)
    e2 = jnp.exp(m2 - m1)
    den = 1.0 + e2
    w1 = p_top / den
    w2 = p_top * (e2 / den)
    gates = jnp.where(col == i1, w1, 0.0) + jnp.where(col == i2, w2, 0.0)
    return gates + jnp.where(col == 0, g_idx.astype(F32), 0.0)


def _route_rows(logits_t, n_groups, per_group):
    assert n_groups <= 8 and per_group == 8
    n_tok = logits_t.shape[1]
    row = lax.broadcasted_iota(jnp.int32, (8, n_tok), 0)
    neg = jnp.float32(-jnp.inf)
    big = jnp.int32(8)
    lg = jnp.where(row < n_groups, logits_t[0:8], neg)
    mg = jnp.max(lg, axis=0, keepdims=True)
    g_idx = jnp.min(jnp.where(lg == mg, row, big), axis=0, keepdims=True)
    p_top = 1.0 / jnp.sum(jnp.where(row < n_groups, jnp.exp(lg - mg), 0.0), axis=0, keepdims=True)
    sel = logits_t[8:16]
    for g in range(1, n_groups):
        sel = jnp.where(g_idx == g, logits_t[8 + 8 * g:16 + 8 * g], sel)
    m1 = jnp.max(sel, axis=0, keepdims=True)
    i1 = jnp.min(jnp.where(sel == m1, row, big), axis=0, keepdims=True)
    sel2 = jnp.where(row == i1, neg, sel)
    m2 = jnp.max(sel2, axis=0, keepdims=True)
    i2 = jnp.min(jnp.where(sel2 == m2, row, big), axis=0, keepdims=True)
    e2 = jnp.exp(m2 - m1)
    den = 1.0 + e2
    w1 = p_top / den
    w2 = p_top * (e2 / den)
    own = jnp.where(row == i1, w1, 0.0) + jnp.where(row == i2, w2, 0.0)
    blocks = [jnp.where(row == 0, g_idx.astype(F32), 0.0)]
    blocks += [jnp.where(g_idx == g, own, 0.0) for g in range(n_groups)]
    blocks += [jnp.zeros((8, n_tok), F32)] * (ROUTER_LANES // 8 - len(blocks))
    return jnp.concatenate(blocks, axis=0), g_idx


def _mixer_prompt_kernel(x_ref, lbw_ref, g1_ref, win_ref, hgg_ref, cw_ref, cb_ref, wx_ref, bx_ref,
                         wa_ref, ba_ref, lam_ref, wout_ref, g2_ref, wr_ref, br_ref,
                         x1_ref, xn2_ref, gates_ref, cnt_ref, sout_ref, hout_ref, cout_ref,
                         proj_s, k_s, b_s, o_s, st_s, xr_s, a_s, u_s, hcar_s, yb_out_s,
                         *, n_groups, per_group):
    lb_t = x_ref.shape[1]
    wa_w = o_s.shape[1]
    wb_w = a_s.shape[1]
    n_heads = wa_w // HEAD_DIM
    j = pl.program_id(1)
    nj = pl.num_programs(1)

    @pl.when(j == 0)
    def _():
        st_s[...] = jnp.zeros_like(st_s)
        hcar_s[...] = jnp.zeros_like(hcar_s)
        xr_s[0:8, :] = jnp.zeros((8, wb_w), F32)

    x = x_ref[0]
    xn = _rms(x, g1_ref[...]).astype(BF16)
    for c, w in ((wa_w, wa_w), (4 * wa_w, wb_w), (4 * wa_w + wb_w, wb_w), (0, wa_w), (2 * wa_w, wa_w),
                 (3 * wa_w, wa_w)):
        proj_s[:, c:c + w] = jnp.dot(xn, win_ref[:, c:c + w], preferred_element_type=F32)

    xb0 = 4 * wa_w
    xr_s[pl.ds(8, lb_t), :] = proj_s[:, xb0:xb0 + wb_w]
    xc = (cb_ref[...] + cw_ref[3:4, :] * xr_s[pl.ds(8, lb_t), :] + cw_ref[2:3, :] * xr_s[pl.ds(7, lb_t), :]
          + cw_ref[1:2, :] * xr_s[pl.ds(6, lb_t), :] + cw_ref[0:1, :] * xr_s[pl.ds(5, lb_t), :])
    tail = xr_s[pl.ds(lb_t + 5, 3), :]
    xr_s[5:8, :] = tail
    cout_ref[0] = tail
    xcb = xc.astype(BF16)
    gate_x = _sigmoid(jnp.dot(xcb, wx_ref[...], preferred_element_type=F32) + bx_ref[...])
    gate_a = _sigmoid(jnp.dot(xcb, wa_ref[...], preferred_element_type=F32) + ba_ref[...])
    log_a = (-LRU_C) * gate_a * _softplus(-lam_ref[...])
    a = jnp.exp(log_a)
    mult = jnp.sqrt((1.0 - a) * (1.0 + a))
    first =(lax.broadcasted_iota(jnp.int32, (lb_t, 1), 0) == 0) & (j == 0)
    a = jnp.where(first, 0.0, a)
    mult = jnp.where(first, 1.0, mult)
    a_s[...] = a
    u_s[...] = gate_x * xc * mult

    lb = _forget_lower_bound(lbw_ref[...])
    f = lb + (1.0 - lb) * _sigmoid(proj_s[:, wa_w:2 * wa_w])
    for h in range(n_heads):
        k_s[h] = 1.0 - f[:, h * HEAD_DIM:(h + 1) * HEAD_DIM]
    logf = jnp.log(f)
    r_i = lax.broadcasted_iota(jnp.int32, (CHUNK, CHUNK), 0)
    c_i = lax.broadcasted_iota(jnp.int32, (CHUNK, CHUNK), 1)
    tri = jnp.where(r_i >= c_i, 1.0, 0.0).astype(BF16)
    lf_hi = logf.astype(BF16)
    rest = logf - lf_hi.astype(F32)
    lf_mid = rest.astype(BF16)
    lf_lo = (rest - lf_mid.astype(F32)).astype(BF16)
    for c in range(0, lb_t, CHUNK):
        cum = [jnp.dot(tri, part[c:c + CHUNK, :], preferred_element_type=F32) for part in (lf_lo, lf_mid, lf_hi)]
        b_all = LOG2E * ((cum[0] + cum[1]) + cum[2])
        for h in range(n_heads):
            b_s[h, c:c + CHUNK, :] = b_all[:, h * HEAD_DIM:(h + 1) * HEAD_DIM]

    row_sub = lax.broadcasted_iota(jnp.int32, (SUB, HEAD_DIM), 0)
    lane_sub = lax.broadcasted_iota(jnp.int32, (SUB, HEAD_DIM), 1)
    assert n_heads % 2 == 0
    r_kk = lax.broadcasted_iota(jnp.int32, (2 * HEAD_DIM, 2 * HEAD_DIM), 0)
    c_kk = lax.broadcasted_iota(jnp.int32, (2 * HEAD_DIM, 2 * HEAD_DIM), 1)
    ones_kk = jnp.where((r_kk < HEAD_DIM) == (c_kk < HEAD_DIM), 1.0, 0.0).astype(BF16)

    n_sub = CHUNK // SUB
    half = SUB // 2
    lower_left = (row_sub >= half) & (lane_sub < half)
    own_lane = jnp.where(row_sub >= half, half, 0)

    def chunk_start(r0):
        first = []
        for h in range(n_heads):
            hs = slice(h * HEAD_DIM, (h + 1) * HEAD_DIM)
            q = proj_s[pl.ds(r0, CHUNK), hs]
            b = b_s[h, pl.ds(r0, CHUNK), :]
            v = proj_s[pl.ds(r0, CHUNK), 2 * wa_w + h * HEAD_DIM:2 * wa_w + (h + 1) * HEAD_DIM]
            k = k_s[h, pl.ds(r0, CHUNK), :]

            def key_rows(ref, lo, j):
                return jnp.stack([jnp.broadcast_to(ref[h, pl.ds(r0 + lo + hf * half + j, 1), :], (half, HEAD_DIM))
                                  for hf in range(2)])
            vb = v.astype(BF16)
            st = st_s[h]
            b_last = b[CHUNK - 1:CHUNK, :]
            o = lax.dot_general((q * jnp.exp2(b)).astype(BF16), st.astype(BF16), NT_DIMS,
                                preferred_element_type=F32)
            k_end = k * jnp.exp2(b_last - b)
            st_s[h] = st * jnp.exp2(b_last) + lax.dot_general(vb, k_end.astype(BF16), TN_DIMS,
                                                               preferred_element_type=F32)
            terms, off, mid = [], [], []
            for i in range(n_sub):
                lo = i * SUB
                qi, bi, ki = q[lo:lo + SUB], b[lo:lo + SUB], k[lo:lo + SUB]
                q3, b3 = (a.reshape(2, half, HEAD_DIM) for a in (qi, bi))
                terms += [(q3 * (key_rows(k_s, lo, j) * jnp.exp2(b3 - key_rows(b_s, lo, j))))
                          .reshape(SUB, HEAD_DIM).astype(BF16) for j in range(half)]
                rm = bi[half - 1:half]
                mid.append(lax.dot_general((qi * jnp.exp2(bi - rm)).astype(BF16),
                                           (ki * jnp.exp2(rm - bi)).astype(BF16), NT_DIMS,
                                           preferred_element_type=F32))
                if i > 0:
                    r = b[lo - 1:lo]
                    qt = (qi * jnp.exp2(bi - r)).astype(BF16)
                    kt = (k[:lo] * jnp.exp2(r - b[:lo])).astype(BF16)
                    off.append(lax.dot_general(qt, kt, NT_DIMS, preferred_element_type=F32))
            first.append([o, vb, off, jnp.concatenate(terms, axis=0), mid])
        for h in range(0, n_heads, 2):
            both = jnp.concatenate([first[h][3], first[h + 1][3]], axis=1)
            sums = jnp.dot(both, ones_kk, preferred_element_type=F32)
            first[h][3] = sums[:, :HEAD_DIM]
            first[h + 1][3] = sums[:, HEAD_DIM:]
        return first

    def chunk_finish(r0, first):
        for h in range(n_heads):
            o, vb, off, sums, mid = first[h]
            outs = []
            for i in range(n_sub):
                lo = i * SUB
                sc = jnp.zeros((SUB, HEAD_DIM), F32)
                for j in range(half):
                    row0 = (i * half + j) * SUB
                    sc = jnp.where(lane_sub == own_lane + j, sums[row0:row0 + SUB], sc)
                sc = jnp.where(row_sub >= lane_sub, sc, 0.0)[:, :SUB]
                sc = jnp.where(lower_left[:, :SUB], mid[i], sc)
                od = o[lo:lo + SUB] + jnp.dot(sc.astype(BF16), vb[lo:lo + SUB], preferred_element_type=F32)
                if i > 0:
                    od = od + jnp.dot(off[i - 1].astype(BF16), vb[:lo], preferred_element_type=F32)
                outs.append(od)
            o_s[pl.ds(r0, CHUNK), h * HEAD_DIM:(h + 1) * HEAD_DIM] = jnp.concatenate(outs, axis=0)

    def chunks_body(ci, carry):
        rows = [pl.multiple_of((ci * UNROLL + u) * CHUNK, CHUNK) for u in range(UNROLL)]
        started = [chunk_start(r0) for r0 in rows]
        for r0, first in zip(rows, started):
            chunk_finish(r0, first)
        return carry

    row8 = lax.broadcasted_iota(jnp.int32, (8, 1), 0)

    def scan_body(gi, carry):
        r0 = pl.multiple_of(gi * 8, 8)
        aa = a_s[pl.ds(r0, 8), :]
        uu = u_s[pl.ds(r0, 8), :]
        for s in (1, 2, 4):
            m = row8 >= s
            uu = jnp.where(m, aa * pltpu.roll(uu, s, 0) + uu, uu)
            aa = jnp.where(m, aa * pltpu.roll(aa, s, 0), aa)
        hh = aa * carry + uu
        u_s[pl.ds(r0, 8), :] = hh
        return hh[7:8, :]

    h_last = lax.fori_loop(0, lb_t // 8, scan_body, hcar_s[...])
    hcar_s[...] = h_last
    hout_ref[0] = h_last
    yb = (u_s[...] * _gelu_tanh(proj_s[:, xb0 + wb_w:xb0 + 2 * wb_w])).astype(BF16)
    yb_out_s[...] = jnp.dot(yb, wout_ref[wa_w:, :], preferred_element_type=F32)

    assert lb_t % (CHUNK * UNROLL) == 0
    lax.fori_loop(0, lb_t // (CHUNK * UNROLL), chunks_body, 0)

    ya = []
    for h in range(n_heads):
        hs = slice(h * HEAD_DIM, (h + 1) * HEAD_DIM)
        oh = o_s[:, hs]
        oh = oh * lax.rsqrt(jnp.mean(oh * oh, axis=-1, keepdims=True) + EPS) * hgg_ref[:, hs]
        ya.append(oh * _silu(proj_s[:, 3 * wa_w + h * HEAD_DIM:3 * wa_w + (h + 1) * HEAD_DIM]))

    ya = jnp.concatenate(ya, axis=-1).astype(BF16)
    x1 = x + (jnp.dot(ya, wout_ref[:wa_w, :], preferred_element_type=F32) + yb_out_s[...])
    x1_ref[0] = x1
    xn2 = _rms(x1, g2_ref[...]).astype(BF16)
    xn2_ref[0] = xn2
    logits_t = lax.dot_general(wr_ref[...], xn2, NT_DIMS, preferred_element_type=F32) + br_ref[...]
    gates_t, g_idx = _route_rows(logits_t, n_groups, per_group)
    gates_ref[0] = gates_t.T
    lane_c = lax.broadcasted_iota(jnp.int32, (1, ROUTER_LANES), 1)
    for t in range(lb_t // TOK_TILE):
        gi = g_idx[:, t * TOK_TILE:(t + 1) * TOK_TILE]
        cnt = jnp.zeros((1, ROUTER_LANES), jnp.int32)
        for g in range(n_groups):
            cnt = cnt + jnp.where(lane_c == g, jnp.sum(jnp.where(gi == g, 1, 0), axis=1, keepdims=True), 0)
        cnt_ref[0, t:t + 1, :] = cnt

    @pl.when(j == nj - 1)
    def _():
        for h in range(n_heads):
            sout_ref[0, h] = st_s[h].T


def _const_spec(shape):
    nd = len(shape)
    return pl.BlockSpec(shape, lambda *_: (0,) * nd)


def _mixer_prompt(x, p, lb_t):
    bsz, seq, d = x.shape
    wa_w = p["hgg"].shape[1]
    wb_w = p["cb"].shape[1]
    n_heads = wa_w // HEAD_DIM
    n_cols = p["win_bf"].shape[1]
    weights = [p["lbw"], p["g1"], p["win_bf"], p["hgg"], p["cw"], p["cb"], p["wx_bf"], p["bx"],
               p["wa_bf"], p["ba"], p["lam"], p["wout_bf"], p["g2"], p["wr_t_bf"], p["br_t"]]
    nj = seq // lb_t
    tile = lambda w: pl.BlockSpec((1, lb_t, w), lambda b, j: (b, j, 0))
    out_shape = (
        jax.ShapeDtypeStruct((bsz, seq, d), F32),
        jax.ShapeDtypeStruct((bsz, seq, d), BF16),
        jax.ShapeDtypeStruct((bsz, seq, ROUTER_LANES), F32),
        jax.ShapeDtypeStruct((bsz * (seq // lb_t), lb_t // TOK_TILE, ROUTER_LANES), jnp.int32),
        jax.ShapeDtypeStruct((bsz, n_heads, HEAD_DIM, HEAD_DIM), F32),
        jax.ShapeDtypeStruct((bsz, 1, wb_w), F32),
        jax.ShapeDtypeStruct((bsz, 3, wb_w), F32),
    )
    out_specs = (
        tile(d), tile(d), tile(ROUTER_LANES),
        pl.BlockSpec((1, lb_t // TOK_TILE, ROUTER_LANES), lambda b, j: (b * nj + j, 0, 0)),
        pl.BlockSpec((1, n_heads, HEAD_DIM, HEAD_DIM), lambda b, j: (b, 0, 0, 0)),
        pl.BlockSpec((1, 1, wb_w), lambda b, j: (b, 0, 0)),
        pl.BlockSpec((1, 3, wb_w), lambda b, j: (b, 0, 0)),
    )
    scratch = [
        pltpu.VMEM((lb_t, n_cols), F32),
        pltpu.VMEM((n_heads, lb_t, HEAD_DIM), F32),
        pltpu.VMEM((n_heads, lb_t, HEAD_DIM), F32),
        pltpu.VMEM((lb_t, wa_w), F32),
        pltpu.VMEM((n_heads, HEAD_DIM, HEAD_DIM), F32),
        pltpu.VMEM((lb_t + 8, wb_w), F32),
        pltpu.VMEM((lb_t, wb_w), F32),
        pltpu.VMEM((lb_t, wb_w), F32),
        pltpu.VMEM((1, wb_w), F32),
        pltpu.VMEM((lb_t, d), F32),
    ]
    kern = functools.partial(_mixer_prompt_kernel, n_groups=p["n_groups"], per_group=p["per_group"])
    return pl.pallas_call(
        kern,
        grid=(bsz, nj),
        in_specs=[tile(d)] + [_const_spec(w.shape) for w in weights],
        out_specs=out_specs,
        out_shape=out_shape,
        scratch_shapes=scratch,
        compiler_params=pltpu.CompilerParams(dimension_semantics=("arbitrary", "arbitrary"),
                                             vmem_limit_bytes=VMEM_LIMIT_BYTES),
        name="mixer_prompt",
    )(x, *weights)


def _moe_kernel(x1_ref, xn2_ref, gates_ref, w1_ref, w3_ref, w2_ref, gf_ref, y_ref, acc_s, *, n_groups):
    e = pl.program_id(1)

    @pl.when(e == 0)
    def _():
        acc_s[...] = jnp.zeros_like(acc_s)

    xb = xn2_ref[...]
    per_step = w1_ref.shape[0]
    gates = pltpu.roll(gates_ref[...], ROUTER_LANES - n_groups - per_step * e, 1)
    acc = acc_s[...]
    for s in range(per_step):
        h = (_silu(jnp.dot(xb, w1_ref[s], preferred_element_type=F32))
             * jnp.dot(xb, w3_ref[s], preferred_element_type=F32) * gates[:, s:s + 1])
        acc = acc + jnp.dot(h.astype(BF16), w2_ref[s], preferred_element_type=F32)
    acc_s[...] = acc

    @pl.when(e == pl.num_programs(1) - 1)
    def _():
        y_ref[...] = _rms(x1_ref[...] + acc_s[...], gf_ref[...])


def _moe_dense(x1, xn2, gates, p, tm):
    t, d = x1.shape
    n_exp, _, d_exp = p["w1_bf"].shape
    per_step = p["per_group"]
    row = lambda w: pl.BlockSpec((tm, w), lambda i, e: (i, 0))
    return pl.pallas_call(
        functools.partial(_moe_kernel, n_groups=p["n_groups"]),
        grid=(t // tm, n_exp // per_step),
        in_specs=[row(d), row(d), row(ROUTER_LANES),
                  pl.BlockSpec((per_step, d, d_exp), lambda i, e: (e, 0, 0)),
                  pl.BlockSpec((per_step, d, d_exp), lambda i, e: (e, 0, 0)),
                  pl.BlockSpec((per_step, d_exp, d), lambda i, e: (e, 0, 0)),
                  _const_spec(p["gf"].shape)],
        out_specs=row(d),
        out_shape=jax.ShapeDtypeStruct((t, d), F32),
        scratch_shapes=[pltpu.VMEM((tm, d), F32)],
        compiler_params=pltpu.CompilerParams(dimension_semantics=("arbitrary", "arbitrary"),
                                             vmem_limit_bytes=VMEM_LIMIT_BYTES),
        name="moe_dense",
    )(x1, xn2, gates, p["w1_bf"], p["w3_bf"], p["w2_bf"], p["gf"])


def _seg_pad(n):
    return jnp.bitwise_and(n + (SEG - 1), -SEG)


def _row_tile_pad(n):
    return jnp.bitwise_and(n + (ROW_TILE - 1), -ROW_TILE)


def _tile_segments(cnt_ref, tile, n_groups):
    padded = [_seg_pad(cnt_ref[tile * n_groups + g]) for g in range(n_groups)]
    starts, acc = [], jnp.int32(0)
    for g in range(n_groups):
        starts.append(acc)
        acc = acc + padded[g]
    return padded, starts, acc


def _group_bases(cnt_ref, n_tiles, n_groups):
    def body(t, tot):
        return tuple(tot[g] + _seg_pad(cnt_ref[t * n_groups + g]) for g in range(n_groups))
    totals = lax.fori_loop(0, n_tiles, body, (jnp.int32(0),) * n_groups)
    bases, ends, acc = [], [], jnp.int32(0)
    for g in range(n_groups):
        bases.append(acc)
        acc = acc + _row_tile_pad(totals[g])
        ends.append(acc)
    return bases, ends


def _chunk_rows(n_chunks, starts, gstart):
    rows, row = [], None
    for c in range(n_chunks):
        row = gstart[0] if c == 0 else row + SEG
        for g in range(1, len(starts)):
            row = jnp.where(starts[g] == c * SEG, gstart[g], row)
        rows.append(pl.multiple_of(row, SEG))
    return rows


def _sort_matrix(gates, starts):
    n_groups = len(starts)
    col = lax.broadcasted_iota(jnp.int32, (TOK_TILE, ROUTER_LANES), 1)
    gi = gates[:, 0:1].astype(jnp.int32)
    onehot = col == gi
    r_i = lax.broadcasted_iota(jnp.int32, (TOK_TILE, TOK_TILE), 0)
    c_i = lax.broadcasted_iota(jnp.int32, (TOK_TILE, TOK_TILE), 1)
    earlier = jnp.where(r_i > c_i, 1.0, 0.0).astype(BF16)
    before = jnp.dot(earlier, jnp.where(onehot, 1.0, 0.0).astype(BF16), preferred_element_type=F32)
    rank = jnp.sum(jnp.where(onehot, before, 0.0), axis=-1, keepdims=True).astype(jnp.int32)
    base = jnp.zeros((TOK_TILE, 1), jnp.int32)
    for g in range(n_groups):
        base = base + jnp.where(gi == g, starts[g], 0)
    lane = lax.broadcasted_iota(jnp.int32, (TOK_TILE, SORT_ROWS), 1)
    return lane == base + rank


def _sort_matrix_rows(gates, starts):
    n_groups = len(starts)
    col = lax.broadcasted_iota(jnp.int32, (TOK_TILE, ROUTER_LANES), 1)
    onehot_t = jnp.where(col == gates[:, 0:1].astype(jnp.int32), 1.0, 0.0).T
    r_i = lax.broadcasted_iota(jnp.int32, (TOK_TILE, TOK_TILE), 0)
    c_i = lax.broadcasted_iota(jnp.int32, (TOK_TILE, TOK_TILE), 1)
    later = jnp.where(r_i < c_i, 1.0, 0.0).astype(BF16)
    before = jnp.dot(onehot_t.astype(BF16), later, preferred_element_type=F32)
    g_row = lax.broadcasted_iota(jnp.int32, (ROUTER_LANES, 1), 0)
    base = jnp.zeros((ROUTER_LANES, 1), jnp.int32)
    for g in range(n_groups):
        base = base + jnp.where(g_row == g, starts[g], 0)
    dest = jnp.sum(jnp.where(onehot_t > 0.0, before + base.astype(F32), 0.0), axis=0, keepdims=True)
    row = lax.broadcasted_iota(jnp.int32, (SORT_ROWS, TOK_TILE), 0)
    return row == dest.astype(jnp.int32)


def _dispatch_copies(xbuf, gbuf, xs_hbm, gs_hbm, sem, slot, c, row):
    return (pltpu.make_async_copy(xbuf.at[slot, pl.ds(c * SEG, SEG), :], xs_hbm.at[pl.ds(row, SEG), :], sem.at[slot]),
            pltpu.make_async_copy(gbuf.at[slot, pl.ds(c * SEG, SEG), :], gs_hbm.at[pl.ds(row, SEG), :], sem.at[slot]))


def _moe_dispatch_kernel(cnt_ref, xn2_ref, gates_ref, xs_in, gs_in, xs_hbm, gs_hbm, tg_ref,
                         xbuf, gbuf, sem, gstart_s, *, n_groups, per_group):
    del xs_in, gs_in
    j = pl.program_id(0)
    n_tiles = pl.num_programs(0)
    n_chunks = (TOK_TILE + n_groups * SEG) // SEG
    slot = lax.rem(j, 2)

    @pl.when(j == 0)
    def _():
        bases, ends = _group_bases(cnt_ref, n_tiles, n_groups)
        for g in range(n_groups):
            gstart_s[g] = bases[g]
        n_rt = tg_ref.shape[0] - 1
        for i in range(n_rt):
            tg = jnp.int32(0)
            for g in range(n_groups):
                tg = tg + jnp.where(ends[g] <= i * ROW_TILE, 1, 0)
            tg_ref[i] = tg
        tg_ref[n_rt] = ends[n_groups - 1] // ROW_TILE

    def wait_tile(tile, slot_):
        _, _, used = _tile_segments(cnt_ref, tile, n_groups)
        for c in range(n_chunks):
            @pl.when(c * SEG < used)
            def _():
                for cp in _dispatch_copies(xbuf, gbuf, xs_hbm, gs_hbm, sem, slot_, c, 0):
                    cp.wait()

    @pl.when(j >= 2)
    def _():
        wait_tile(j - 2, slot)

    padded, starts, used = _tile_segments(cnt_ref, j, n_groups)
    gates = gates_ref[...]
    d = xn2_ref.shape[1]
    g_hi = gates.astype(BF16)
    g_lo = (gates - g_hi.astype(F32)).astype(BF16)
    sort_m = jnp.where(_sort_matrix_rows(gates, starts), 1.0, 0.0).astype(BF16)
    moved = jnp.dot(sort_m, jnp.concatenate([xn2_ref[...], g_hi, g_lo], axis=1), preferred_element_type=F32)
    xbuf[slot] = moved[:, :d].astype(BF16)
    gbuf[slot] = moved[:, d:d + ROUTER_LANES] + moved[:, d + ROUTER_LANES:]
    gstart = [gstart_s[g] for g in range(n_groups)]
    rows = _chunk_rows(n_chunks, starts, gstart)
    for c in range(n_chunks):
        @pl.when(c * SEG < used)
        def _():
            for cp in _dispatch_copies(xbuf, gbuf, xs_hbm, gs_hbm, sem, slot, c, rows[c]):
                cp.start()
    for g in range(n_groups):
        gstart_s[g] = gstart[g] + padded[g]

    @pl.when(j == n_tiles - 1)
    def _():
        @pl.when(j >= 1)
        def _():
            wait_tile(j - 1, 1 - slot)
        wait_tile(j, slot)


def _moe_group_kernel(tg_ref, xs_ref, gs_ref, w1_ref, w3_ref, w2_ref, ys_ref, *, n_groups, per_group):
    i = pl.program_id(0)

    @pl.when(tg_ref[i] < n_groups)
    def _():
        xb = xs_ref[...]
        gates = pltpu.roll(gs_ref[...], ROUTER_LANES - PROMPT_EXPERT_LANE - per_group * tg_ref[i], 1)
        acc = jnp.zeros(ys_ref.shape, F32)
        for s in range(per_group):
            h = (_silu(jnp.dot(xb, w1_ref[s], preferred_element_type=F32))
                 * jnp.dot(xb, w3_ref[s], preferred_element_type=F32) * gates[:, s:s + 1])
            acc = acc + jnp.dot(h.astype(BF16), w2_ref[s], preferred_element_type=F32)
        ys_ref[...] = acc.astype(BF16)


def _moe_combine_kernel(cnt_ref, gates_ref, x1_ref, gf_ref, ys_hbm, y_ref, ybuf, sem, gstart_s, *, n_groups):
    j = pl.program_id(0)
    n_tiles = pl.num_programs(0)
    n_chunks = SORT_ROWS // SEG
    slot = lax.rem(j, 2)

    def copies(slot_, c, row):
        return pltpu.make_async_copy(ys_hbm.at[pl.ds(row, SEG), :], ybuf.at[slot_, pl.ds(c * SEG, SEG), :],
                                     sem.at[slot_])

    def fetch_tile(tile, slot_):
        padded, starts, _ = _tile_segments(cnt_ref, tile, n_groups)
        gstart = [gstart_s[g] for g in range(n_groups)]
        for c, row in enumerate(_chunk_rows(n_chunks, starts, gstart)):
            copies(slot_, c, row).start()
        for g in range(n_groups):
            gstart_s[g] = gstart[g] + padded[g]

    @pl.when(j == 0)
    def _():
        bases, _ = _group_bases(cnt_ref, n_tiles, n_groups)
        for g in range(n_groups):
            gstart_s[g] = bases[g]
        fetch_tile(0, 0)

    @pl.when(j + 1 < n_tiles)
    def _():
        fetch_tile(j + 1, 1 - slot)

    for c in range(n_chunks):
        copies(slot, c, 0).wait()
    _, starts, _ = _tile_segments(cnt_ref, j, n_groups)
    sort_t = _sort_matrix(gates_ref[...], starts)
    moe = jnp.dot(jnp.where(sort_t, 1.0, 0.0).astype(BF16), ybuf[slot], preferred_element_type=F32)
    y_ref[...] = _rms(x1_ref[...] + moe, gf_ref[...])


def _moe_sorted(x1, xn2, gates, counts, p):
    t, d = x1.shape
    n_groups, per_group = p["n_groups"], p["per_group"]
    n_tiles = t // TOK_TILE
    d_exp = p["w1_bf"].shape[2]
    cap = t + n_tiles * n_groups * SEG + n_groups * ROW_TILE + SORT_ROWS
    n_rt = -(-cap // ROW_TILE)
    rows = n_rt * ROW_TILE
    cnt = counts[:, :, :n_groups].reshape(-1)
    params = pltpu.CompilerParams(dimension_semantics=("arbitrary",), vmem_limit_bytes=VMEM_LIMIT_BYTES)
    any_spec = pl.BlockSpec(memory_space=pl.ANY)
    tok = lambda w: pl.BlockSpec((TOK_TILE, w), lambda j, c: (j, 0))

    xs, gs, tile_group = pl.pallas_call(
        functools.partial(_moe_dispatch_kernel, n_groups=n_groups, per_group=per_group),
        grid_spec=pltpu.PrefetchScalarGridSpec(
            num_scalar_prefetch=1, grid=(n_tiles,),
            in_specs=[tok(d), tok(ROUTER_LANES), any_spec, any_spec],
            out_specs=(any_spec, any_spec, pl.BlockSpec(memory_space=pltpu.SMEM)),
            scratch_shapes=[pltpu.VMEM((2, SORT_ROWS, d), BF16), pltpu.VMEM((2, SORT_ROWS, ROUTER_LANES), F32),
                            pltpu.SemaphoreType.DMA((2,)), pltpu.SMEM((n_groups,), jnp.int32)]),
        out_shape=(jax.ShapeDtypeStruct((rows, d), BF16), jax.ShapeDtypeStruct((rows, ROUTER_LANES), F32),
                   jax.ShapeDtypeStruct((n_rt + 1,), jnp.int32)),
        input_output_aliases={3: 0, 4: 1},
        compiler_params=params, name="moe_dispatch",
    )(cnt, xn2, gates, jnp.zeros((rows, d), BF16), jnp.zeros((rows, ROUTER_LANES), F32))

    used_tile = lambda i, tg: (jnp.minimum(i, tg[n_rt] - 1), 0)
    group_w = lambda i, tg: (jnp.minimum(tg[i], n_groups - 1), 0, 0)
    ys = pl.pallas_call(
        functools.partial(_moe_group_kernel, n_groups=n_groups, per_group=per_group),
        grid_spec=pltpu.PrefetchScalarGridSpec(
            num_scalar_prefetch=1, grid=(n_rt,),
            in_specs=[pl.BlockSpec((ROW_TILE, d), used_tile), pl.BlockSpec((ROW_TILE, ROUTER_LANES), used_tile),
                      pl.BlockSpec((per_group, d, d_exp), group_w), pl.BlockSpec((per_group, d, d_exp), group_w),
                      pl.BlockSpec((per_group, d_exp, d), group_w)],
            out_specs=pl.BlockSpec((ROW_TILE, d), used_tile)),
        out_shape=jax.ShapeDtypeStruct((rows, d), BF16),
        input_output_aliases={1: 0},
        compiler_params=params, name="moe_experts",
    )(tile_group, xs, gs, p["w1_bf"], p["w3_bf"], p["w2_bf"])

    return pl.pallas_call(
        functools.partial(_moe_combine_kernel, n_groups=n_groups),
        grid_spec=pltpu.PrefetchScalarGridSpec(
            num_scalar_prefetch=1, grid=(n_tiles,),
            in_specs=[tok(ROUTER_LANES), tok(d), pl.BlockSpec(p["gf"].shape, lambda j, c: (0, 0)), any_spec],
            out_specs=tok(d),
            scratch_shapes=[pltpu.VMEM((2, SORT_ROWS, d), BF16), pltpu.SemaphoreType.DMA((2,)),
                            pltpu.SMEM((n_groups,), jnp.int32)]),
        out_shape=jax.ShapeDtypeStruct((t, d), F32),
        compiler_params=params, name="moe_combine",
    )(cnt, gates, x1, p["gf"], ys)


def _sample_in_kernel(x_ref, c0_ref, c1_ref, c2_ref, h0_ref, lbw_ref, g1_ref, win_ref, cw_ref, cb_ref,
                      wx_ref, bx_ref, wa_ref, ba_ref, lam_ref,
                      q_ref, f_ref, v_ref, g_ref, yb_ref, hnew_ref, xr_ref):
    wa_w = v_ref.shape[1]
    wb_w = yb_ref.shape[1]
    xn = _rms(x_ref[...], g1_ref[...])
    proj = jnp.dot(xn, win_ref[...], precision=HIGHEST, preferred_element_type=F32)
    lb = _forget_lower_bound(lbw_ref[...])
    f = lb + (1.0 - lb) * _sigmoid(proj[:, wa_w:2 * wa_w])
    q_ref[...] = proj[:, 0:wa_w].T
    f_ref[...] = f.T
    v_ref[...] = proj[:, 2 * wa_w:3 * wa_w]
    g_ref[...] = proj[:, 3 * wa_w:4 * wa_w]
    xr = proj[:, 4 * wa_w:4 * wa_w + wb_w]
    xr_ref[...] = xr
    xc = (cb_ref[...] + cw_ref[0:1, :] * c0_ref[...] + cw_ref[1:2, :] * c1_ref[...]
          + cw_ref[2:3, :] * c2_ref[...] + cw_ref[3:4, :] * xr)
    gate_x = _sigmoid(jnp.dot(xc, wx_ref[...], precision=HIGHEST, preferred_element_type=F32) + bx_ref[...])
    gate_a = _sigmoid(jnp.dot(xc, wa_ref[...], precision=HIGHEST, preferred_element_type=F32) + ba_ref[...])
    log_a = (-LRU_C) * gate_a * _softplus(-lam_ref[...])
    a = jnp.exp(log_a)
    mult = jnp.sqrt(-_expm1(2.0 * log_a))
    h = a * h0_ref[...] + gate_x * xc * mult
    hnew_ref[...] = h
    yb_ref[...] = h * _gelu_tanh(proj[:, 4 * wa_w + wb_w:4 * wa_w + 2 * wb_w])


def _sample_state_kernel(s_ref, qt_ref, ft_ref, v_ref, snew_ref, o_ref):
    tb, n_heads = s_ref.shape[0], s_ref.shape[1]
    n_tok = qt_ref.shape[1]
    shift = lax.rem(n_tok - pl.program_id(0) * tb, n_tok)
    sq = (HEAD_DIM, HEAD_DIM)
    for h in range(n_heads):
        hs = slice(h * HEAD_DIM, (h + 1) * HEAD_DIM)
        qh = pltpu.roll(qt_ref[hs, :], shift, 1)
        fh = pltpu.roll(ft_ref[hs, :], shift, 1)
        rows = []
        for t in range(tb):
            f_all = jnp.broadcast_to(fh[:, t:t + 1], sq)
            s_new = f_all * s_ref[t, h] + (1.0 - f_all) * v_ref[t:t + 1, hs]
            snew_ref[t, h] = s_new
            rows.append(jnp.sum(jnp.broadcast_to(qh[:, t:t + 1], sq) * s_new, axis=0, keepdims=True))
        o_ref[:, hs] = jnp.concatenate(rows, axis=0)


def _sample_out_kernel(x_ref, o_ref, g_ref, yb_ref, hgg_ref, wout_ref, g2_ref, wr_ref, br_ref,
                       x1_ref, xn2_ref, gates_ref, *, n_groups, per_group):
    wa_w = o_ref.shape[1]
    ya = []
    for h in range(wa_w // HEAD_DIM):
        hs = slice(h * HEAD_DIM, (h + 1) * HEAD_DIM)
        oh = o_ref[:, hs]
        oh = oh * lax.rsqrt(jnp.mean(oh * oh, axis=-1, keepdims=True) + EPS) * hgg_ref[:, hs]
        ya.append(oh * _silu(g_ref[:, hs]))
    y = jnp.concatenate(ya + [yb_ref[...]], axis=-1)
    x1 = x_ref[...] + jnp.dot(y, wout_ref[...], precision=HIGHEST, preferred_element_type=F32)
    x1_ref[...] = x1
    xn2 = _rms(x1, g2_ref[...])
    xn2_ref[...] = xn2.astype(BF16)
    logits = jnp.dot(xn2, wr_ref[...], precision=HIGHEST, preferred_element_type=F32) + br_ref[...]
    gates_ref[...] = _route(logits, n_groups, per_group)


def _whole(kernel, out_shape, *args, name):
    return pl.pallas_call(
        kernel, out_shape=out_shape,
        compiler_params=pltpu.CompilerParams(vmem_limit_bytes=VMEM_LIMIT_BYTES), name=name)(*args)


def _mixer_sample(x, s0, h0, c0, p, tb):
    n, d = x.shape
    wa_w = p["hgg"].shape[1]
    wb_w = p["cb"].shape[1]
    n_heads = wa_w // HEAD_DIM
    sd = lambda w: jax.ShapeDtypeStruct((n, w), F32)
    key_major = jax.ShapeDtypeStruct((wa_w, n), F32)
    q, f, v, g, yb, h_new, xr = _whole(
        _sample_in_kernel, (key_major,) * 2 + (sd(wa_w),) * 2 + (sd(wb_w),) * 3,
        x, c0[:, 0, :], c0[:, 1, :], c0[:, 2, :], h0, p["lbw"], p["g1"], p["win"], p["cw"], p["cb"],
        p["wx"], p["bx"], p["wa"], p["ba"], p["lam"], name="sample_in")
    tok = lambda w: pl.BlockSpec((tb, w), lambda i: (i, 0))
    st = pl.BlockSpec((tb, n_heads, HEAD_DIM, HEAD_DIM), lambda i: (i, 0, 0, 0))
    s_new, o = pl.pallas_call(
        _sample_state_kernel,
        grid=(n // tb,),
        in_specs=[st, _const_spec((wa_w, n)), _const_spec((wa_w, n)), tok(wa_w)],
        out_specs=(st, tok(wa_w)),
        out_shape=(jax.ShapeDtypeStruct(s0.shape, F32), sd(wa_w)),
        compiler_params=pltpu.CompilerParams(dimension_semantics=("arbitrary",),
                                             vmem_limit_bytes=VMEM_LIMIT_BYTES),
        name="sample_state",
    )(s0, q, f, v)
    x1, xn2, gates = _whole(
        functools.partial(_sample_out_kernel, n_groups=p["n_groups"], per_group=p["per_group"]),
        (sd(d), jax.ShapeDtypeStruct((n, d), BF16), sd(ROUTER_LANES)),
        x, o, g, yb, p["hgg"], p["wout"], p["g2"], p["wr"], p["br"], name="sample_out")
    c_new = jnp.stack([c0[:, 1, :], c0[:, 2, :], xr], axis=1)
    return x1, xn2, gates, s_new, h_new, c_new


def _block_diag(w):
    n, c, _ = w.shape
    eye = jnp.eye(n, dtype=w.dtype)
    return (w[:, :, None, :] * eye[:, None, :, None]).reshape(n * c, n * c)


def _prepare(lower_bounds, ln1_g, w_in, hgrn_norm_g, conv_w, conv_b, lru_wx, lru_bx, lru_wa, lru_ba,
             lru_lambda, w_out, ln2_g, router_wg, router_bg, router_we, router_be, exp_w1, exp_w3,
             exp_w2, final_g):
    d = w_in.shape[1]
    n_groups = router_wg.shape[-1]
    per_group = router_we.shape[-1]
    row = lambda a: a.reshape(1, -1).astype(F32)
    we = jnp.transpose(router_we[0], (1, 0, 2)).reshape(d, n_groups * per_group)
    pad = ROUTER_LANES - n_groups - n_groups * per_group
    wr = jnp.concatenate([router_wg[0], we, jnp.zeros((d, pad), F32)], axis=1)
    br = jnp.concatenate([router_bg[0], router_be[0].reshape(-1), jnp.zeros((pad,), F32)]).reshape(1, -1)
    rows_t = -(-(8 + n_groups * per_group) // 16) * 16
    wr_t = jnp.concatenate([router_wg[0].T, jnp.zeros((8 - n_groups, d), F32), we.T,
                            jnp.zeros((rows_t - 8 - n_groups * per_group, d), F32)], axis=0)
    br_t = jnp.concatenate([router_bg[0], jnp.zeros((8 - n_groups,), F32), router_be[0].reshape(-1),
                            jnp.zeros((rows_t - 8 - n_groups * per_group,), F32)]).reshape(-1, 1)
    wx = _block_diag(lru_wx[0])
    wa = _block_diag(lru_wa[0])
    return dict(
        wr_t_bf=wr_t.astype(BF16), br_t=br_t,
        n_groups=n_groups, per_group=per_group,
        lbw=lower_bounds.astype(F32), g1=row(ln1_g[0]), win=w_in[0], win_bf=w_in[0].astype(BF16),
        hgg=row(hgrn_norm_g[0]), cw=conv_w[0], cb=row(conv_b[0]),
        wx=wx, wx_bf=wx.astype(BF16), bx=row(lru_bx[0]), wa=wa, wa_bf=wa.astype(BF16), ba=row(lru_ba[0]),
        lam=row(lru_lambda[0]), wout=w_out[0], wout_bf=w_out[0].astype(BF16), g2=row(ln2_g[0]),
        wr=wr, br=br, w1_bf=exp_w1[0].astype(BF16), w3_bf=exp_w3[0].astype(BF16),
        w2_bf=exp_w2[0].astype(BF16), gf=row(final_g))


def kernel(x_prompt, x_sample, state_hgrn, state_rglru, state_conv, lower_bounds, ln1_g, w_in, hgrn_norm_g, conv_w, conv_b, lru_wx, lru_bx, lru_wa, lru_ba, lru_lambda, w_out, ln2_g, router_wg, router_bg, router_we, router_be, exp_w1, exp_w3, exp_w2, final_g):
    assert w_in.shape[0] == 1, "single-layer trunk"
    p = _prepare(lower_bounds, ln1_g, w_in, hgrn_norm_g, conv_w, conv_b, lru_wx, lru_bx, lru_wa, lru_ba,
                 lru_lambda, w_out, ln2_g, router_wg, router_bg, router_we, router_be, exp_w1, exp_w3,
                 exp_w2, final_g)
    bsz, seq, d = x_prompt.shape
    lb_t = min(512, seq)
    x1, xn2, gates, counts, s_p, h_p, c_p = _mixer_prompt(x_prompt, p, lb_t)
    t = bsz * seq
    y_p = _moe_sorted(x1.reshape(t, d), xn2.reshape(t, d), gates.reshape(t, ROUTER_LANES), counts, p)

    n = x_sample.shape[0]
    x1s, xn2s, gates_s, s_s, h_s, c_s = _mixer_sample(x_sample[:, 0, :], state_hgrn[0], state_rglru[0],
                                                      state_conv[0], p, 8)
    y_s = _moe_dense(x1s, xn2s, gates_s, p, n)
    return (y_p.reshape(bsz, seq, d), y_s.reshape(n, 1, d),
            s_p[None], h_p.reshape(1, bsz, -1), c_p[None],
            s_s[None], h_s[None], c_s[None])
```

```python
import functools

import jax
import jax.numpy as jnp
from jax import lax
from jax.experimental import pallas as pl
from jax.experimental.pallas import tpu as pltpu

F32 = jnp.float32
BF16 = jnp.bfloat16
HIGHEST = lax.Precision.HIGHEST

EPS = 1e-6
LRU_C = 8.0
LOG2E = 1.4426950408889634
HEAD_DIM = 128
CHUNK = 64
SUB = 16
UNROLL = 4
ROUTER_LANES = 128
PROMPT_EXPERT_LANE = 8
TOK_TILE = 512
SEG = 16
SORT_ROWS = 640
ROW_TILE = 512
VMEM_LIMIT_BYTES = 56 * 1024 * 1024

NT_DIMS = (((1,), (1,)), ((), ()))
TN_DIMS = (((0,), (0,)), ((), ()))


def _rms(x, g):
    return x * lax.rsqrt(jnp.mean(x * x, axis=-1, keepdims=True) + EPS) * g


def _sigmoid(x):
    return 1.0 / (1.0 + jnp.exp(-x))


def _silu(x):
    return x * _sigmoid(x)


def _gelu_tanh(x):
    c = 0.7978845608028654
    return x * (0.5 * (1.0 + jnp.tanh(c * (x + 0.044715 * (x * x * x)))))


def _softplus(z):
    return jnp.maximum(z, 0.0) + jnp.log1p(jnp.exp(-jnp.abs(z)))


def _expm1(x):
    u = jnp.exp(x)
    um1 = u - 1.0
    small = um1 * x / jnp.log(u)
    return jnp.where(um1 == 0.0, x, jnp.where(jnp.abs(x) < 0.5, small, um1))


def _forget_lower_bound(lbw):
    m = jnp.max(lbw, axis=0, keepdims=True)
    e = jnp.exp(lbw - m)
    return e[0:1, :] / jnp.sum(e, axis=0, keepdims=True)


def _route(logits, n_groups, per_group):
    n = logits.shape[-1]
    col = lax.broadcasted_iota(jnp.int32, logits.shape, 1)
    neg = jnp.float32(-jnp.inf)
    big = jnp.int32(n)
    is_g = col < n_groups
    lg = jnp.where(is_g, logits, neg)
    mg = jnp.max(lg, axis=-1, keepdims=True)
    g_idx = jnp.min(jnp.where(lg == mg, col, big), axis=-1, keepdims=True)
    p_top = 1.0 / jnp.sum(jnp.where(is_g, jnp.exp(logits - mg), 0.0), axis=-1, keepdims=True)
    lo = n_groups + per_group * g_idx
    le = jnp.where((col >= lo) & (col < lo + per_group), logits, neg)
    m1 = jnp.max(le, axis=-1, keepdims=True)
    i1 = jnp.min(jnp.where(le == m1, col, big), axis=-1, keepdims=True)
    le2 = jnp.where(col == i1, neg, le)
    m2 = jnp.max(le2, axis=-1, keepdims=True)
    i2 = jnp.min(jnp.where(le2 == m2, col, big), axis=-1, keepdims=True)
    e2 = jnp.exp(m2 - m1)
    den = 1.0 + e2
    w1 = p_top / den
    w2 = p_top * (e2 / den)
    gates = jnp.where(col == i1, w1, 0.0) + jnp.where(col == i2, w2, 0.0)
    return gates + jnp.where(col == 0, g_idx.astype(F32), 0.0)


def _route_rows(logits_t, n_groups, per_group):
    assert n_groups <= 8 and per_group == 8
    n_tok = logits_t.shape[1]
    row = lax.broadcasted_iota(jnp.int32, (8, n_tok), 0)
    neg = jnp.float32(-jnp.inf)
    big = jnp.int32(8)
    lg = jnp.where(row < n_groups, logits_t[0:8], neg)
    mg = jnp.max(lg, axis=0, keepdims=True)
    g_idx = jnp.min(jnp.where(lg == mg, row, big), axis=0, keepdims=True)
    p_top = 1.0 / jnp.sum(jnp.where(row < n_groups, jnp.exp(lg - mg), 0.0), axis=0, keepdims=True)
    sel = logits_t[8:16]
    for g in range(1, n_groups):
        sel = jnp.where(g_idx == g, logits_t[8 + 8 * g:16 + 8 * g], sel)
    m1 = jnp.max(sel, axis=0, keepdims=True)
    i1 = jnp.min(jnp.where(sel == m1, row, big), axis=0, keepdims=True)
    sel2 = jnp.where(row == i1, neg, sel)
    m2 = jnp.max(sel2, axis=0, keepdims=True)
    i2 = jnp.min(jnp.where(sel2 == m2, row, big), axis=0, keepdims=True)
    e2 = jnp.exp(m2 - m1)
    den = 1.0 + e2
    w1 = p_top / den
    w2 = p_top * (e2 / den)
    own = jnp.where(row == i1, w1, 0.0) + jnp.where(row == i2, w2, 0.0)
    blocks = [jnp.where(row == 0, g_idx.astype(F32), 0.0)]
    blocks += [jnp.where(g_idx == g, own, 0.0) for g in range(n_groups)]
    blocks += [jnp.zeros((8, n_tok), F32)] * (ROUTER_LANES // 8 - len(blocks))
    return jnp.concatenate(blocks, axis=0), g_idx


def _mixer_prompt_kernel(x_ref, lbw_ref, g1_ref, win_ref, hgg_ref, cw_ref, cb_ref, wx_ref, bx_ref,
                         wa_ref, ba_ref, lam_ref, wout_ref, g2_ref, wr_ref, br_ref,
                         x1_ref, xn2_ref, gates_ref, cnt_ref, sout_ref, hout_ref, cout_ref,
                         proj_s, k_s, b_s, o_s, st_s, xr_s, a_s, u_s, hcar_s, yb_out_s,
                         *, n_groups, per_group):
    lb_t = x_ref.shape[1]
    wa_w = o_s.shape[1]
    wb_w = a_s.shape[1]
    n_heads = wa_w // HEAD_DIM
    j = pl.program_id(1)
    nj = pl.num_programs(1)

    @pl.when(j == 0)
    def _():
        st_s[...] = jnp.zeros_like(st_s)
        hcar_s[...] = jnp.zeros_like(hcar_s)
        xr_s[0:8, :] = jnp.zeros((8, wb_w), F32)

    x = x_ref[0]
    xn = _rms(x, g1_ref[...]).astype(BF16)
    for c, w in ((wa_w, wa_w), (4 * wa_w, wb_w), (4 * wa_w + wb_w, wb_w), (0, wa_w), (2 * wa_w, wa_w),
                 (3 * wa_w, wa_w)):
        proj_s[:, c:c + w] = jnp.dot(xn, win_ref[:, c:c + w], preferred_element_type=F32)

    xb0 = 4 * wa_w
    xr_s[pl.ds(8, lb_t), :] = proj_s[:, xb0:xb0 + wb_w]
    xc = (cb_ref[...] + cw_ref[3:4, :] * xr_s[pl.ds(8, lb_t), :] + cw_ref[2:3, :] * xr_s[pl.ds(7, lb_t), :]
          + cw_ref[1:2, :] * xr_s[pl.ds(6, lb_t), :] + cw_ref[0:1, :] * xr_s[pl.ds(5, lb_t), :])
    tail = xr_s[pl.ds(lb_t + 5, 3), :]
    xr_s[5:8, :] = tail
    cout_ref[0] = tail
    xcb = xc.astype(BF16)
    gate_x = _sigmoid(jnp.dot(xcb, wx_ref[...], preferred_element_type=F32) + bx_ref[...])
    gate_a = _sigmoid(jnp.dot(xcb, wa_ref[...], preferred_element_type=F32) + ba_ref[...])
    log_a = (-LRU_C) * gate_a * _softplus(-lam_ref[...])
    a = jnp.exp(log_a)
    mult = jnp.sqrt((1.0 - a) * (1.0 + a))
    first =(lax.broadcasted_iota(jnp.int32, (lb_t, 1), 0) == 0) & (j == 0)
    a = jnp.where(first, 0.0, a)
    mult = jnp.where(first, 1.0, mult)
    a_s[...] = a
    u_s[...] = gate_x * xc * mult

    lb = _forget_lower_bound(lbw_ref[...])
    f = lb + (1.0 - lb) * _sigmoid(proj_s[:, wa_w:2 * wa_w])
    for h in range(n_heads):
        k_s[h] = 1.0 - f[:, h * HEAD_DIM:(h + 1) * HEAD_DIM]
    logf = jnp.log(f)
    r_i = lax.broadcasted_iota(jnp.int32, (CHUNK, CHUNK), 0)
    c_i = lax.broadcasted_iota(jnp.int32, (CHUNK, CHUNK), 1)
    tri = jnp.where(r_i >= c_i, 1.0, 0.0).astype(BF16)
    lf_hi = logf.astype(BF16)
    rest = logf - lf_hi.astype(F32)
    lf_mid = rest.astype(BF16)
    lf_lo = (rest - lf_mid.astype(F32)).astype(BF16)
    for c in range(0, lb_t, CHUNK):
        cum = [jnp.dot(tri, part[c:c + CHUNK, :], preferred_element_type=F32) for part in (lf_lo, lf_mid, lf_hi)]
        b_all = LOG2E * ((cum[0] + cum[1]) + cum[2])
        for h in range(n_heads):
            b_s[h, c:c + CHUNK, :] = b_all[:, h * HEAD_DIM:(h + 1) * HEAD_DIM]

    row_sub = lax.broadcasted_iota(jnp.int32, (SUB, HEAD_DIM), 0)
    lane_sub = lax.broadcasted_iota(jnp.int32, (SUB, HEAD_DIM), 1)
    assert n_heads % 2 == 0
    r_kk = lax.broadcasted_iota(jnp.int32, (2 * HEAD_DIM, 2 * HEAD_DIM), 0)
    c_kk = lax.broadcasted_iota(jnp.int32, (2 * HEAD_DIM, 2 * HEAD_DIM), 1)
    ones_kk = jnp.where((r_kk < HEAD_DIM) == (c_kk < HEAD_DIM), 1.0, 0.0).astype(BF16)

    n_sub = CHUNK // SUB
    half = SUB // 2
    lower_left = (row_sub >= half) & (lane_sub < half)
    own_lane = jnp.where(row_sub >= half, half, 0)

    def chunk_start(r0):
        first = []
        for h in range(n_heads):
            hs = slice(h * HEAD_DIM, (h + 1) * HEAD_DIM)
            q = proj_s[pl.ds(r0, CHUNK), hs]
            b = b_s[h, pl.ds(r0, CHUNK), :]
            v = proj_s[pl.ds(r0, CHUNK), 2 * wa_w + h * HEAD_DIM:2 * wa_w + (h + 1) * HEAD_DIM]
            k = k_s[h, pl.ds(r0, CHUNK), :]

            def key_rows(ref, lo, j):
                return jnp.stack([jnp.broadcast_to(ref[h, pl.ds(r0 + lo + hf * half + j, 1), :], (half, HEAD_DIM))
                                  for hf in range(2)])
            vb = v.astype(BF16)
            st = st_s[h]
            b_last = b[CHUNK - 1:CHUNK, :]
            o = lax.dot_general((q * jnp.exp2(b)).astype(BF16), st.astype(BF16), NT_DIMS,
                                preferred_element_type=F32)
            k_end = k * jnp.exp2(b_last - b)
            st_s[h] = st * jnp.exp2(b_last) + lax.dot_general(vb, k_end.astype(BF16), TN_DIMS,
                                                               preferred_element_type=F32)
            terms, off, mid = [], [], []
            for i in range(n_sub):
                lo = i * SUB
                qi, bi, ki = q[lo:lo + SUB], b[lo:lo + SUB], k[lo:lo + SUB]
                q3, b3 = (a.reshape(2, half, HEAD_DIM) for a in (qi, bi))
                terms += [(q3 * (key_rows(k_s, lo, j) * jnp.exp2(b3 - key_rows(b_s, lo, j))))
                          .reshape(SUB, HEAD_DIM).astype(BF16) for j in range(half)]
                rm = bi[half - 1:half]
                mid.append(lax.dot_general((qi * jnp.exp2(bi - rm)).astype(BF16),
                                           (ki * jnp.exp2(rm - bi)).astype(BF16), NT_DIMS,
                                           preferred_element_type=F32))
                if i > 0:
                    r = b[lo - 1:lo]
                    qt = (qi * jnp.exp2(bi - r)).astype(BF16)
                    kt = (k[:lo] * jnp.exp2(r - b[:lo])).astype(BF16)
                    off.append(lax.dot_general(qt, kt, NT_DIMS, preferred_element_type=F32))
            first.append([o, vb, off, jnp.concatenate(terms, axis=0), mid])
        for h in range(0, n_heads, 2):
            both = jnp.concatenate([first[h][3], first[h + 1][3]], axis=1)
            sums = jnp.dot(both, ones_kk, preferred_element_type=F32)
            first[h][3] = sums[:, :HEAD_DIM]
            first[h + 1][3] = sums[:, HEAD_DIM:]
        return first

    def chunk_finish(r0, first):
        for h in range(n_heads):
            o, vb, off, sums, mid = first[h]
            outs = []
            for i in range(n_sub):
                lo = i * SUB
                sc = jnp.zeros((SUB, HEAD_DIM), F32)
                for j in range(half):
                    row0 = (i * half + j) * SUB
                    sc = jnp.where(lane_sub == own_lane + j, sums[row0:row0 + SUB], sc)
                sc = jnp.where(row_sub >= lane_sub, sc, 0.0)[:, :SUB]
                sc = jnp.where(lower_left[:, :SUB], mid[i], sc)
                od = o[lo:lo + SUB] + jnp.dot(sc.astype(BF16), vb[lo:lo + SUB], preferred_element_type=F32)
                if i > 0:
                    od = od + jnp.dot(off[i - 1].astype(BF16), vb[:lo], preferred_element_type=F32)
                outs.append(od)
            o_s[pl.ds(r0, CHUNK), h * HEAD_DIM:(h + 1) * HEAD_DIM] = jnp.concatenate(outs, axis=0)

    def chunks_body(ci, carry):
        rows = [pl.multiple_of((ci * UNROLL + u) * CHUNK, CHUNK) for u in range(UNROLL)]
        started = [chunk_start(r0) for r0 in rows]
        for r0, first in zip(rows, started):
            chunk_finish(r0, first)
        return carry

    row8 = lax.broadcasted_iota(jnp.int32, (8, 1), 0)

    def scan_body(gi, carry):
        r0 = pl.multiple_of(gi * 8, 8)
        aa = a_s[pl.ds(r0, 8), :]
        uu = u_s[pl.ds(r0, 8), :]
        for s in (1, 2, 4):
            m = row8 >= s
            uu = jnp.where(m, aa * pltpu.roll(uu, s, 0) + uu, uu)
            aa = jnp.where(m, aa * pltpu.roll(aa, s, 0), aa)
        hh = aa * carry + uu
        u_s[pl.ds(r0, 8), :] = hh
        return hh[7:8, :]

    h_last = lax.fori_loop(0, lb_t // 8, scan_body, hcar_s[...])
    hcar_s[...] = h_last
    hout_ref[0] = h_last
    yb = (u_s[...] * _gelu_tanh(proj_s[:, xb0 + wb_w:xb0 + 2 * wb_w])).astype(BF16)
    yb_out_s[...] = jnp.dot(yb, wout_ref[wa_w:, :], preferred_element_type=F32)

    assert lb_t % (CHUNK * UNROLL) == 0
    lax.fori_loop(0, lb_t // (CHUNK * UNROLL), chunks_body, 0)

    ya = []
    for h in range(n_heads):
        hs = slice(h * HEAD_DIM, (h + 1) * HEAD_DIM)
        oh = o_s[:, hs]
        oh = oh * lax.rsqrt(jnp.mean(oh * oh, axis=-1, keepdims=True) + EPS) * hgg_ref[:, hs]
        ya.append(oh * _silu(proj_s[:, 3 * wa_w + h * HEAD_DIM:3 * wa_w + (h + 1) * HEAD_DIM]))

    ya = jnp.concatenate(ya, axis=-1).astype(BF16)
    x1 = x + (jnp.dot(ya, wout_ref[:wa_w, :], preferred_element_type=F32) + yb_out_s[...])
    x1_ref[0] = x1
    xn2 = _rms(x1, g2_ref[...]).astype(BF16)
    xn2_ref[0] = xn2
    logits_t = lax.dot_general(wr_ref[...], xn2, NT_DIMS, preferred_element_type=F32) + br_ref[...]
    gates_t, g_idx = _route_rows(logits_t, n_groups, per_group)
    gates_ref[0] = gates_t.T
    lane_c = lax.broadcasted_iota(jnp.int32, (1, ROUTER_LANES), 1)
    for t in range(lb_t // TOK_TILE):
        gi = g_idx[:, t * TOK_TILE:(t + 1) * TOK_TILE]
        cnt = jnp.zeros((1, ROUTER_LANES), jnp.int32)
        for g in range(n_groups):
            cnt = cnt + jnp.where(lane_c == g, jnp.sum(jnp.where(gi == g, 1, 0), axis=1, keepdims=True), 0)
        cnt_ref[0, t:t + 1, :] = cnt

    @pl.when(j == nj - 1)
    def _():
        for h in range(n_heads):
            sout_ref[0, h] = st_s[h].T


def _const_spec(shape):
    nd = len(shape)
    return pl.BlockSpec(shape, lambda *_: (0,) * nd)


def _mixer_prompt(x, p, lb_t):
    bsz, seq, d = x.shape
    wa_w = p["hgg"].shape[1]
    wb_w = p["cb"].shape[1]
    n_heads = wa_w // HEAD_DIM
    n_cols = p["win_bf"].shape[1]
    weights = [p["lbw"], p["g1"], p["win_bf"], p["hgg"], p["cw"], p["cb"], p["wx_bf"], p["bx"],
               p["wa_bf"], p["ba"], p["lam"], p["wout_bf"], p["g2"], p["wr_t_bf"], p["br_t"]]
    nj = seq // lb_t
    tile = lambda w: pl.BlockSpec((1, lb_t, w), lambda b, j: (b, j, 0))
    out_shape = (
        jax.ShapeDtypeStruct((bsz, seq, d), F32),
        jax.ShapeDtypeStruct((bsz, seq, d), BF16),
        jax.ShapeDtypeStruct((bsz, seq, ROUTER_LANES), F32),
        jax.ShapeDtypeStruct((bsz * (seq // lb_t), lb_t // TOK_TILE, ROUTER_LANES), jnp.int32),
        jax.ShapeDtypeStruct((bsz, n_heads, HEAD_DIM, HEAD_DIM), F32),
        jax.ShapeDtypeStruct((bsz, 1, wb_w), F32),
        jax.ShapeDtypeStruct((bsz, 3, wb_w), F32),
    )
    out_specs = (
        tile(d), tile(d), tile(ROUTER_LANES),
        pl.BlockSpec((1, lb_t // TOK_TILE, ROUTER_LANES), lambda b, j: (b * nj + j, 0, 0)),
        pl.BlockSpec((1, n_heads, HEAD_DIM, HEAD_DIM), lambda b, j: (b, 0, 0, 0)),
        pl.BlockSpec((1, 1, wb_w), lambda b, j: (b, 0, 0)),
        pl.BlockSpec((1, 3, wb_w), lambda b, j: (b, 0, 0)),
    )
    scratch = [
        pltpu.VMEM((lb_t, n_cols), F32),
        pltpu.VMEM((n_heads, lb_t, HEAD_DIM), F32),
        pltpu.VMEM((n_heads, lb_t, HEAD_DIM), F32),
        pltpu.VMEM((lb_t, wa_w), F32),
        pltpu.VMEM((n_heads, HEAD_DIM, HEAD_DIM), F32),
        pltpu.VMEM((lb_t + 8, wb_w), F32),
        pltpu.VMEM((lb_t, wb_w), F32),
        pltpu.VMEM((lb_t, wb_w), F32),
        pltpu.VMEM((1, wb_w), F32),
        pltpu.VMEM((lb_t, d), F32),
    ]
    kern = functools.partial(_mixer_prompt_kernel, n_groups=p["n_groups"], per_group=p["per_group"])
    return pl.pallas_call(
        kern,
        grid=(bsz, nj),
        in_specs=[tile(d)] + [_const_spec(w.shape) for w in weights],
        out_specs=out_specs,
        out_shape=out_shape,
        scratch_shapes=scratch,
        compiler_params=pltpu.CompilerParams(dimension_semantics=("arbitrary", "arbitrary"),
                                             vmem_limit_bytes=VMEM_LIMIT_BYTES),
        name="mixer_prompt",
    )(x, *weights)


def _moe_kernel(x1_ref, xn2_ref, gates_ref, w1_ref, w3_ref, w2_ref, gf_ref, y_ref, acc_s, *, n_groups):
    e = pl.program_id(1)

    @pl.when(e == 0)
    def _():
        acc_s[...] = jnp.zeros_like(acc_s)

    xb = xn2_ref[...]
    per_step = w1_ref.shape[0]
    gates = pltpu.roll(gates_ref[...], ROUTER_LANES - n_groups - per_step * e, 1)
    acc = acc_s[...]
    for s in range(per_step):
        h = (_silu(jnp.dot(xb, w1_ref[s], preferred_element_type=F32))
             * jnp.dot(xb, w3_ref[s], preferred_element_type=F32) * gates[:, s:s + 1])
        acc = acc + jnp.dot(h.astype(BF16), w2_ref[s].astype(BF16), preferred_element_type=F32)
    acc_s[...] = acc

    @pl.when(e == pl.num_programs(1) - 1)
    def _():
        y_ref[...] = _rms(x1_ref[...] + acc_s[...], gf_ref[...])


def _moe_dense(x1, xn2, gates, p, tm):
    t, d = x1.shape
    n_exp, _, d_exp = p["w1_bf"].shape
    per_step = p["per_group"]
    row = lambda w: pl.BlockSpec((tm, w), lambda i, e: (i, 0))
    return pl.pallas_call(
        functools.partial(_moe_kernel, n_groups=p["n_groups"]),
        grid=(t // tm, n_exp // per_step),
        in_specs=[row(d), row(d), row(ROUTER_LANES),
                  pl.BlockSpec((per_step, d, d_exp), lambda i, e: (e, 0, 0)),
                  pl.BlockSpec((per_step, d, d_exp), lambda i, e: (e, 0, 0)),
                  pl.BlockSpec((per_step, d_exp, d), lambda i, e: (e, 0, 0)),
                  _const_spec(p["gf"].shape)],
        out_specs=row(d),
        out_shape=jax.ShapeDtypeStruct((t, d), F32),
        scratch_shapes=[pltpu.VMEM((tm, d), F32)],
        compiler_params=pltpu.CompilerParams(dimension_semantics=("arbitrary", "arbitrary"),
                                             vmem_limit_bytes=VMEM_LIMIT_BYTES),
        name="moe_dense",
    )(x1, xn2, gates, p["w1_bf"], p["w3_bf"], p["w2"], p["gf"])


def _seg_pad(n):
    return jnp.bitwise_and(n + (SEG - 1), -SEG)


def _row_tile_pad(n):
    return jnp.bitwise_and(n + (ROW_TILE - 1), -ROW_TILE)


def _tile_segments(cnt_ref, tile, n_groups):
    padded = [_seg_pad(cnt_ref[tile * n_groups + g]) for g in range(n_groups)]
    starts, acc = [], jnp.int32(0)
    for g in range(n_groups):
        starts.append(acc)
        acc = acc + padded[g]
    return padded, starts, acc


def _group_bases(cnt_ref, n_tiles, n_groups):
    def body(t, tot):
        return tuple(tot[g] + _seg_pad(cnt_ref[t * n_groups + g]) for g in range(n_groups))
    totals = lax.fori_loop(0, n_tiles, body, (jnp.int32(0),) * n_groups)
    bases, ends, acc = [], [], jnp.int32(0)
    for g in range(n_groups):
        bases.append(acc)
        acc = acc + _row_tile_pad(totals[g])
        ends.append(acc)
    return bases, ends


def _chunk_rows(n_chunks, starts, gstart):
    rows, row = [], None
    for c in range(n_chunks):
        row = gstart[0] if c == 0 else row + SEG
        for g in range(1, len(starts)):
            row = jnp.where(starts[g] == c * SEG, gstart[g], row)
        rows.append(pl.multiple_of(row, SEG))
    return rows


def _sort_matrix(gates, starts):
    n_groups = len(starts)
    col = lax.broadcasted_iota(jnp.int32, (TOK_TILE, ROUTER_LANES), 1)
    gi = gates[:, 0:1].astype(jnp.int32)
    onehot = col == gi
    r_i = lax.broadcasted_iota(jnp.int32, (TOK_TILE, TOK_TILE), 0)
    c_i = lax.broadcasted_iota(jnp.int32, (TOK_TILE, TOK_TILE), 1)
    earlier = jnp.where(r_i > c_i, 1.0, 0.0).astype(BF16)
    before = jnp.dot(earlier, jnp.where(onehot, 1.0, 0.0).astype(BF16), preferred_element_type=F32)
    rank = jnp.sum(jnp.where(onehot, before, 0.0), axis=-1, keepdims=True).astype(jnp.int32)
    base = jnp.zeros((TOK_TILE, 1), jnp.int32)
    for g in range(n_groups):
        base = base + jnp.where(gi == g, starts[g], 0)
    lane = lax.broadcasted_iota(jnp.int32, (TOK_TILE, SORT_ROWS), 1)
    return lane == base + rank


def _sort_matrix_rows(gates, starts):
    n_groups = len(starts)
    col = lax.broadcasted_iota(jnp.int32, (TOK_TILE, ROUTER_LANES), 1)
    onehot_t = jnp.where(col == gates[:, 0:1].astype(jnp.int32), 1.0, 0.0).T
    r_i = lax.broadcasted_iota(jnp.int32, (TOK_TILE, TOK_TILE), 0)
    c_i = lax.broadcasted_iota(jnp.int32, (TOK_TILE, TOK_TILE), 1)
    later = jnp.where(r_i < c_i, 1.0, 0.0).astype(BF16)
    before = jnp.dot(onehot_t.astype(BF16), later, preferred_element_type=F32)
    g_row = lax.broadcasted_iota(jnp.int32, (ROUTER_LANES, 1), 0)
    base = jnp.zeros((ROUTER_LANES, 1), jnp.int32)
    for g in range(n_groups):
        base = base + jnp.where(g_row == g, starts[g], 0)
    dest = jnp.sum(jnp.where(onehot_t > 0.0, before + base.astype(F32), 0.0), axis=0, keepdims=True)
    row = lax.broadcasted_iota(jnp.int32, (SORT_ROWS, TOK_TILE), 0)
    return row == dest.astype(jnp.int32)


def _dispatch_copies(xbuf, gbuf, xs_hbm, gs_hbm, sem, slot, c, row):
    return (pltpu.make_async_copy(xbuf.at[slot, pl.ds(c * SEG, SEG), :], xs_hbm.at[pl.ds(row, SEG), :], sem.at[slot]),
            pltpu.make_async_copy(gbuf.at[slot, pl.ds(c * SEG, SEG), :], gs_hbm.at[pl.ds(row, SEG), :], sem.at[slot]))


def _moe_dispatch_kernel(cnt_ref, xn2_ref, gates_ref, xs_in, gs_in, xs_hbm, gs_hbm, tg_ref,
                         xbuf, gbuf, sem, gstart_s, *, n_groups, per_group):
    del xs_in, gs_in
    j = pl.program_id(0)
    n_tiles = pl.num_programs(0)
    n_chunks = (TOK_TILE + n_groups * SEG) // SEG
    slot = lax.rem(j, 2)

    @pl.when(j == 0)
    def _():
        bases, ends = _group_bases(cnt_ref, n_tiles, n_groups)
        for g in range(n_groups):
            gstart_s[g] = bases[g]
        n_rt = tg_ref.shape[0] - 1
        for i in range(n_rt):
            tg = jnp.int32(0)
            for g in range(n_groups):
                tg = tg + jnp.where(ends[g] <= i * ROW_TILE, 1, 0)
            tg_ref[i] = tg
        tg_ref[n_rt] = ends[n_groups - 1] // ROW_TILE

    def wait_tile(tile, slot_):
        _, _, used = _tile_segments(cnt_ref, tile, n_groups)
        for c in range(n_chunks):
            @pl.when(c * SEG < used)
            def _():
                for cp in _dispatch_copies(xbuf, gbuf, xs_hbm, gs_hbm, sem, slot_, c, 0):
                    cp.wait()

    @pl.when(j >= 2)
    def _():
        wait_tile(j - 2, slot)

    padded, starts, used = _tile_segments(cnt_ref, j, n_groups)
    gates = gates_ref[...]
    d = xn2_ref.shape[1]
    g_hi = gates.astype(BF16)
    g_lo = (gates - g_hi.astype(F32)).astype(BF16)
    sort_m = jnp.where(_sort_matrix_rows(gates, starts), 1.0, 0.0).astype(BF16)
    moved = jnp.dot(sort_m, jnp.concatenate([xn2_ref[...], g_hi, g_lo], axis=1), preferred_element_type=F32)
    xbuf[slot] = moved[:, :d].astype(BF16)
    gbuf[slot] = moved[:, d:d + ROUTER_LANES] + moved[:, d + ROUTER_LANES:]
    gstart = [gstart_s[g] for g in range(n_groups)]
    rows = _chunk_rows(n_chunks, starts, gstart)
    for c in range(n_chunks):
        @pl.when(c * SEG < used)
        def _():
            for cp in _dispatch_copies(xbuf, gbuf, xs_hbm, gs_hbm, sem, slot, c, rows[c]):
                cp.start()
    for g in range(n_groups):
        gstart_s[g] = gstart[g] + padded[g]

    @pl.when(j == n_tiles - 1)
    def _():
        @pl.when(j >= 1)
        def _():
            wait_tile(j - 1, 1 - slot)
        wait_tile(j, slot)


def _moe_group_kernel(tg_ref, xs_ref, gs_ref, w1_ref, w3_ref, w2_ref, ys_ref, w2_s, *, n_groups, per_group):
    i = pl.program_id(0)

    @pl.when((i == 0) | (tg_ref[i] != tg_ref[jnp.maximum(i - 1, 0)]))
    def _():
        w2_s[...] = w2_ref[...].astype(BF16)

    @pl.when(tg_ref[i] < n_groups)
    def _():
        xb = xs_ref[...]
        gates = pltpu.roll(gs_ref[...], ROUTER_LANES - PROMPT_EXPERT_LANE - per_group * tg_ref[i], 1)
        acc = jnp.zeros(ys_ref.shape, F32)
        for s in range(per_group):
            h = (_silu(jnp.dot(xb, w1_ref[s], preferred_element_type=F32))
                 * jnp.dot(xb, w3_ref[s], preferred_element_type=F32) * gates[:, s:s + 1])
            acc = acc + jnp.dot(h.astype(BF16), w2_s[s], preferred_element_type=F32)
        ys_ref[...] = acc.astype(BF16)


def _moe_combine_kernel(cnt_ref, gates_ref, x1_ref, gf_ref, ys_hbm, y_ref, ybuf, sem, gstart_s, *, n_groups):
    j = pl.program_id(0)
    n_tiles = pl.num_programs(0)
    n_chunks = SORT_ROWS // SEG
    slot = lax.rem(j, 2)

    def copies(slot_, c, row):
        return pltpu.make_async_copy(ys_hbm.at[pl.ds(row, SEG), :], ybuf.at[slot_, pl.ds(c * SEG, SEG), :],
                                     sem.at[slot_])

    def fetch_tile(tile, slot_):
        padded, starts, _ = _tile_segments(cnt_ref, tile, n_groups)
        gstart = [gstart_s[g] for g in range(n_groups)]
        for c, row in enumerate(_chunk_rows(n_chunks, starts, gstart)):
            copies(slot_, c, row).start()
        for g in range(n_groups):
            gstart_s[g] = gstart[g] + padded[g]

    @pl.when(j == 0)
    def _():
        bases, _ = _group_bases(cnt_ref, n_tiles, n_groups)
        for g in range(n_groups):
            gstart_s[g] = bases[g]
        fetch_tile(0, 0)

    @pl.when(j + 1 < n_tiles)
    def _():
        fetch_tile(j + 1, 1 - slot)

    for c in range(n_chunks):
        copies(slot, c, 0).wait()
    _, starts, _ = _tile_segments(cnt_ref, j, n_groups)
    sort_t = _sort_matrix(gates_ref[...], starts)
    moe = jnp.dot(jnp.where(sort_t, 1.0, 0.0).astype(BF16), ybuf[slot], preferred_element_type=F32)
    y_ref[...] = _rms(x1_ref[...] + moe, gf_ref[...])


def _moe_sorted(x1, xn2, gates, counts, p):
    t, d = x1.shape
    n_groups, per_group = p["n_groups"], p["per_group"]
    n_tiles = t // TOK_TILE
    d_exp = p["w1_bf"].shape[2]
    cap = t + n_tiles * n_groups * SEG + n_groups * ROW_TILE + SORT_ROWS
    n_rt = -(-cap // ROW_TILE)
    rows = n_rt * ROW_TILE
    cnt = counts[:, :, :n_groups].reshape(-1)
    params = pltpu.CompilerParams(dimension_semantics=("arbitrary",), vmem_limit_bytes=VMEM_LIMIT_BYTES)
    any_spec = pl.BlockSpec(memory_space=pl.ANY)
    tok = lambda w: pl.BlockSpec((TOK_TILE, w), lambda j, c: (j, 0))

    xs, gs, tile_group = pl.pallas_call(
        functools.partial(_moe_dispatch_kernel, n_groups=n_groups, per_group=per_group),
        grid_spec=pltpu.PrefetchScalarGridSpec(
            num_scalar_prefetch=1, grid=(n_tiles,),
            in_specs=[tok(d), tok(ROUTER_LANES), any_spec, any_spec],
            out_specs=(any_spec, any_spec, pl.BlockSpec(memory_space=pltpu.SMEM)),
            scratch_shapes=[pltpu.VMEM((2, SORT_ROWS, d), BF16), pltpu.VMEM((2, SORT_ROWS, ROUTER_LANES), F32),
                            pltpu.SemaphoreType.DMA((2,)), pltpu.SMEM((n_groups,), jnp.int32)]),
        out_shape=(jax.ShapeDtypeStruct((rows, d), BF16), jax.ShapeDtypeStruct((rows, ROUTER_LANES), F32),
                   jax.ShapeDtypeStruct((n_rt + 1,), jnp.int32)),
        input_output_aliases={3: 0, 4: 1},
        compiler_params=params, name="moe_dispatch",
    )(cnt, xn2, gates, jnp.zeros((rows, d), BF16), jnp.zeros((rows, ROUTER_LANES), F32))

    used_tile = lambda i, tg: (jnp.minimum(i, tg[n_rt] - 1), 0)
    group_w = lambda i, tg: (jnp.minimum(tg[i], n_groups - 1), 0, 0)
    ys = pl.pallas_call(
        functools.partial(_moe_group_kernel, n_groups=n_groups, per_group=per_group),
        grid_spec=pltpu.PrefetchScalarGridSpec(
            num_scalar_prefetch=1, grid=(n_rt,),
            in_specs=[pl.BlockSpec((ROW_TILE, d), used_tile), pl.BlockSpec((ROW_TILE, ROUTER_LANES), used_tile),
                      pl.BlockSpec((per_group, d, d_exp), group_w), pl.BlockSpec((per_group, d, d_exp), group_w),
                      pl.BlockSpec((per_group, d_exp, d), group_w)],
            out_specs=pl.BlockSpec((ROW_TILE, d), used_tile),
            scratch_shapes=[pltpu.VMEM((per_group, d_exp, d), BF16)]),
        out_shape=jax.ShapeDtypeStruct((rows, d), BF16),
        input_output_aliases={1: 0},
        compiler_params=params, name="moe_experts",
    )(tile_group, xs, gs, p["w1_bf"], p["w3_bf"], p["w2"])

    return pl.pallas_call(
        functools.partial(_moe_combine_kernel, n_groups=n_groups),
        grid_spec=pltpu.PrefetchScalarGridSpec(
            num_scalar_prefetch=1, grid=(n_tiles,),
            in_specs=[tok(ROUTER_LANES), tok(d), pl.BlockSpec(p["gf"].shape, lambda j, c: (0, 0)), any_spec],
            out_specs=tok(d),
            scratch_shapes=[pltpu.VMEM((2, SORT_ROWS, d), BF16), pltpu.SemaphoreType.DMA((2,)),
                            pltpu.SMEM((n_groups,), jnp.int32)]),
        out_shape=jax.ShapeDtypeStruct((t, d), F32),
        compiler_params=params, name="moe_combine",
    )(cnt, gates, x1, p["gf"], ys)


def _sample_in_kernel(x_ref, c0_ref, c1_ref, c2_ref, h0_ref, lbw_ref, g1_ref, win_ref, cw_ref, cb_ref,
                      wx_ref, bx_ref, wa_ref, ba_ref, lam_ref,
                      q_ref, f_ref, v_ref, g_ref, yb_ref, hnew_ref, xr_ref):
    wa_w = v_ref.shape[1]
    wb_w = yb_ref.shape[1]
    xn = _rms(x_ref[...], g1_ref[...])
    proj = jnp.dot(xn, win_ref[...], precision=HIGHEST, preferred_element_type=F32)
    lb = _forget_lower_bound(lbw_ref[...])
    f = lb + (1.0 - lb) * _sigmoid(proj[:, wa_w:2 * wa_w])
    q_ref[...] = proj[:, 0:wa_w].T
    f_ref[...] = f.T
    v_ref[...] = proj[:, 2 * wa_w:3 * wa_w]
    g_ref[...] = proj[:, 3 * wa_w:4 * wa_w]
    xr = proj[:, 4 * wa_w:4 * wa_w + wb_w]
    xr_ref[...] = xr
    xc = (cb_ref[...] + cw_ref[0:1, :] * c0_ref[...] + cw_ref[1:2, :] * c1_ref[...]
          + cw_ref[2:3, :] * c2_ref[...] + cw_ref[3:4, :] * xr)
    gate_x = _sigmoid(jnp.dot(xc, wx_ref[...], precision=HIGHEST, preferred_element_type=F32) + bx_ref[...])
    gate_a = _sigmoid(jnp.dot(xc, wa_ref[...], precision=HIGHEST, preferred_element_type=F32) + ba_ref[...])
    log_a = (-LRU_C) * gate_a * _softplus(-lam_ref[...])
    a = jnp.exp(log_a)
    mult = jnp.sqrt(-_expm1(2.0 * log_a))
    h = a * h0_ref[...] + gate_x * xc * mult
    hnew_ref[...] = h
    yb_ref[...] = h * _gelu_tanh(proj[:, 4 * wa_w + wb_w:4 * wa_w + 2 * wb_w])


def _sample_state_kernel(s_ref, qt_ref, ft_ref, v_ref, snew_ref, o_ref):
    tb, n_heads = s_ref.shape[0], s_ref.shape[1]
    n_tok = qt_ref.shape[1]
    shift = lax.rem(n_tok - pl.program_id(0) * tb, n_tok)
    sq = (HEAD_DIM, HEAD_DIM)
    for h in range(n_heads):
        hs = slice(h * HEAD_DIM, (h + 1) * HEAD_DIM)
        qh = pltpu.roll(qt_ref[hs, :], shift, 1)
        fh = pltpu.roll(ft_ref[hs, :], shift, 1)
        rows = []
        for t in range(tb):
            f_all = jnp.broadcast_to(fh[:, t:t + 1], sq)
            s_new = f_all * s_ref[t, h] + (1.0 - f_all) * v_ref[t:t + 1, hs]
            snew_ref[t, h] = s_new
            rows.append(jnp.sum(jnp.broadcast_to(qh[:, t:t + 1], sq) * s_new, axis=0, keepdims=True))
        o_ref[:, hs] = jnp.concatenate(rows, axis=0)


def _sample_out_kernel(x_ref, o_ref, g_ref, yb_ref, hgg_ref, wout_ref, g2_ref, wr_ref, br_ref,
                       x1_ref, xn2_ref, gates_ref, *, n_groups, per_group):
    wa_w = o_ref.shape[1]
    ya = []
    for h in range(wa_w // HEAD_DIM):
        hs = slice(h * HEAD_DIM, (h + 1) * HEAD_DIM)
        oh = o_ref[:, hs]
        oh = oh * lax.rsqrt(jnp.mean(oh * oh, axis=-1, keepdims=True) + EPS) * hgg_ref[:, hs]
        ya.append(oh * _silu(g_ref[:, hs]))
    y = jnp.concatenate(ya + [yb_ref[...]], axis=-1)
    x1 = x_ref[...] + jnp.dot(y, wout_ref[...], precision=HIGHEST, preferred_element_type=F32)
    x1_ref[...] = x1
    xn2 = _rms(x1, g2_ref[...])
    xn2_ref[...] = xn2.astype(BF16)
    logits = jnp.dot(xn2, wr_ref[...], precision=HIGHEST, preferred_element_type=F32) + br_ref[...]
    gates_ref[...] = _route(logits, n_groups, per_group)


def _whole(kernel, out_shape, *args, name):
    return pl.pallas_call(
        kernel, out_shape=out_shape,
        compiler_params=pltpu.CompilerParams(vmem_limit_bytes=VMEM_LIMIT_BYTES), name=name)(*args)


def _mixer_sample(x, s0, h0, c0, p, tb):
    n, d = x.shape
    wa_w = p["hgg"].shape[1]
    wb_w = p["cb"].shape[1]
    n_heads = wa_w // HEAD_DIM
    sd = lambda w: jax.ShapeDtypeStruct((n, w), F32)
    key_major = jax.ShapeDtypeStruct((wa_w, n), F32)
    q, f, v, g, yb, h_new, xr = _whole(
        _sample_in_kernel, (key_major,) * 2 + (sd(wa_w),) * 2 + (sd(wb_w),) * 3,
        x, c0[:, 0, :], c0[:, 1, :], c0[:, 2, :], h0, p["lbw"], p["g1"], p["win"], p["cw"], p["cb"],
        p["wx"], p["bx"], p["wa"], p["ba"], p["lam"], name="sample_in")
    tok = lambda w: pl.BlockSpec((tb, w), lambda i: (i, 0))
    st = pl.BlockSpec((tb, n_heads, HEAD_DIM, HEAD_DIM), lambda i: (i, 0, 0, 0))
    s_new, o = pl.pallas_call(
        _sample_state_kernel,
        grid=(n // tb,),
        in_specs=[st, _const_spec((wa_w, n)), _const_spec((wa_w, n)), tok(wa_w)],
        out_specs=(st, tok(wa_w)),
        out_shape=(jax.ShapeDtypeStruct(s0.shape, F32), sd(wa_w)),
        compiler_params=pltpu.CompilerParams(dimension_semantics=("arbitrary",),
                                             vmem_limit_bytes=VMEM_LIMIT_BYTES),
        name="sample_state",
    )(s0, q, f, v)
    x1, xn2, gates = _whole(
        functools.partial(_sample_out_kernel, n_groups=p["n_groups"], per_group=p["per_group"]),
        (sd(d), jax.ShapeDtypeStruct((n, d), BF16), sd(ROUTER_LANES)),
        x, o, g, yb, p["hgg"], p["wout"], p["g2"], p["wr"], p["br"], name="sample_out")
    c_new = jnp.stack([c0[:, 1, :], c0[:, 2, :], xr], axis=1)
    return x1, xn2, gates, s_new, h_new, c_new


def _block_diag(w):
    n, c, _ = w.shape
    eye = jnp.eye(n, dtype=w.dtype)
    return (w[:, :, None, :] * eye[:, None, :, None]).reshape(n * c, n * c)


def _prepare(lower_bounds, ln1_g, w_in, hgrn_norm_g, conv_w, conv_b, lru_wx, lru_bx, lru_wa, lru_ba,
             lru_lambda, w_out, ln2_g, router_wg, router_bg, router_we, router_be, exp_w1, exp_w3,
             exp_w2, final_g):
    d = w_in.shape[1]
    n_groups = router_wg.shape[-1]
    per_group = router_we.shape[-1]
    row = lambda a: a.reshape(1, -1).astype(F32)
    we = jnp.transpose(router_we[0], (1, 0, 2)).reshape(d, n_groups * per_group)
    pad = ROUTER_LANES - n_groups - n_groups * per_group
    wr = jnp.concatenate([router_wg[0], we, jnp.zeros((d, pad), F32)], axis=1)
    br = jnp.concatenate([router_bg[0], router_be[0].reshape(-1), jnp.zeros((pad,), F32)]).reshape(1, -1)
    rows_t = -(-(8 + n_groups * per_group) // 16) * 16
    wr_t = jnp.concatenate([router_wg[0].T, jnp.zeros((8 - n_groups, d), F32), we.T,
                            jnp.zeros((rows_t - 8 - n_groups * per_group, d), F32)], axis=0)
    br_t = jnp.concatenate([router_bg[0], jnp.zeros((8 - n_groups,), F32), router_be[0].reshape(-1),
                            jnp.zeros((rows_t - 8 - n_groups * per_group,), F32)]).reshape(-1, 1)
    wx = _block_diag(lru_wx[0])
    wa = _block_diag(lru_wa[0])
    return dict(
        wr_t_bf=wr_t.astype(BF16), br_t=br_t,
        n_groups=n_groups, per_group=per_group,
        lbw=lower_bounds.astype(F32), g1=row(ln1_g[0]), win=w_in[0], win_bf=w_in[0].astype(BF16),
        hgg=row(hgrn_norm_g[0]), cw=conv_w[0], cb=row(conv_b[0]),
        wx=wx, wx_bf=wx.astype(BF16), bx=row(lru_bx[0]), wa=wa, wa_bf=wa.astype(BF16), ba=row(lru_ba[0]),
        lam=row(lru_lambda[0]), wout=w_out[0], wout_bf=w_out[0].astype(BF16), g2=row(ln2_g[0]),
        wr=wr, br=br, w1_bf=exp_w1[0].astype(BF16), w3_bf=exp_w3[0].astype(BF16),
        w2=exp_w2[0], gf=row(final_g))


def kernel(x_prompt, x_sample, state_hgrn, state_rglru, state_conv, lower_bounds, ln1_g, w_in, hgrn_norm_g, conv_w, conv_b, lru_wx, lru_bx, lru_wa, lru_ba, lru_lambda, w_out, ln2_g, router_wg, router_bg, router_we, router_be, exp_w1, exp_w3, exp_w2, final_g):
    assert w_in.shape[0] == 1, "single-layer trunk"
    p = _prepare(lower_bounds, ln1_g, w_in, hgrn_norm_g, conv_w, conv_b, lru_wx, lru_bx, lru_wa, lru_ba,
                 lru_lambda, w_out, ln2_g, router_wg, router_bg, router_we, router_be, exp_w1, exp_w3,
                 exp_w2, final_g)
    bsz, seq, d = x_prompt.shape
    lb_t = min(512, seq)
    x1, xn2, gates, counts, s_p, h_p, c_p = _mixer_prompt(x_prompt, p, lb_t)
    t = bsz * seq
    y_p = _moe_sorted(x1.reshape(t, d), xn2.reshape(t, d), gates.reshape(t, ROUTER_LANES), counts, p)

    n = x_sample.shape[0]
    x1s, xn2s, gates_s, s_s, h_s, c_s = _mixer_sample(x_sample[:, 0, :], state_hgrn[0], state_rglru[0],
                                                      state_conv[0], p, 8)
    y_s = _moe_dense(x1s, xn2s, gates_s, p, n)
    return (y_p.reshape(bsz, seq, d), y_s.reshape(n, 1, d),
            s_p[None], h_p.reshape(1, bsz, -1), c_p[None],
            s_s[None], h_s[None], c_s[None])
```

```python
import functools

import jax
import jax.numpy as jnp
from jax import lax
from jax.experimental import pallas as pl
from jax.experimental.pallas import tpu as pltpu

F32 = jnp.float32
BF16 = jnp.bfloat16
HIGHEST = lax.Precision.HIGHEST

EPS = 1e-6
LRU_C = 8.0
LOG2E = 1.4426950408889634
HEAD_DIM = 128
CHUNK = 64
SUB = 16
UNROLL = 8
ROUTER_LANES = 128
PROMPT_EXPERT_LANE = 8
MIXER_BLOCK = 512
SAMPLE_STEP_TOKENS = 8
TOK_TILE = 512
SEG = 16
MAX_GROUPS = 8
SORT_ROWS = TOK_TILE + MAX_GROUPS * SEG
ROW_TILE = 512
VMEM_LIMIT_BYTES = 56 * 1024 * 1024

NT_DIMS = (((1,), (1,)), ((), ()))
TN_DIMS = (((0,), (0,)), ((), ()))


def _rms(x, g):
    return x * lax.rsqrt(jnp.mean(x * x, axis=-1, keepdims=True) + EPS) * g


def _sigmoid(x):
    return 1.0 / (1.0 + jnp.exp(-x))


def _silu(x):
    return x * _sigmoid(x)


def _gelu_tanh(x):
    c = 0.7978845608028654
    return x * (0.5 * (1.0 + jnp.tanh(c * (x + 0.044715 * (x * x * x)))))


def _softplus(z):
    return jnp.maximum(z, 0.0) + jnp.log1p(jnp.exp(-jnp.abs(z)))


def _expm1(x):
    u = jnp.exp(x)
    um1 = u - 1.0
    small = um1 * x / jnp.log(u)
    return jnp.where(um1 == 0.0, x, jnp.where(jnp.abs(x) < 0.5, small, um1))


def _forget_lower_bound(lbw):
    m = jnp.max(lbw, axis=0, keepdims=True)
    e = jnp.exp(lbw - m)
    return e[0:1, :] / jnp.sum(e, axis=0, keepdims=True)


def _route(logits, n_groups, per_group):
    n = logits.shape[-1]
    col = lax.broadcasted_iota(jnp.int32, logits.shape, 1)
    neg = jnp.float32(-jnp.inf)
    big = jnp.int32(n)
    is_g = col < n_groups
    lg = jnp.where(is_g, logits, neg)
    mg = jnp.max(lg, axis=-1, keepdims=True)
    g_idx = jnp.min(jnp.where(lg == mg, col, big), axis=-1, keepdims=True)
    p_top = 1.0 / jnp.sum(jnp.where(is_g, jnp.exp(logits - mg), 0.0), axis=-1, keepdims=True)
    lo = n_groups + per_group * g_idx
    le = jnp.where((col >= lo) & (col < lo + per_group), logits, neg)
    m1 = jnp.max(le, axis=-1, keepdims=True)
    i1 = jnp.min(jnp.where(le == m1, col, big), axis=-1, keepdims=True)
    le2 = jnp.where(col == i1, neg, le)
    m2 = jnp.max(le2, axis=-1, keepdims=True)
    i2 = jnp.min(jnp.where(le2 == m2, col, big), axis=-1, keepdims=True)
    e2 = jnp.exp(m2 - m1)
    den = 1.0 + e2
    w1 = p_top / den
    w2 = p_top * (e2 / den)
    gates = jnp.where(col == i1, w1, 0.0) + jnp.where(col == i2, w2, 0.0)
    return gates + jnp.where(col == 0, g_idx.astype(F32), 0.0)


def _route_rows(logits_t, n_groups, per_group):
    assert n_groups <= 8 and per_group == 8
    n_tok = logits_t.shape[1]
    row = lax.broadcasted_iota(jnp.int32, (8, n_tok), 0)
    neg = jnp.float32(-jnp.inf)
    big = jnp.int32(8)
    lg = jnp.where(row < n_groups, logits_t[0:8], neg)
    mg = jnp.max(lg, axis=0, keepdims=True)
    g_idx = jnp.min(jnp.where(lg == mg, row, big), axis=0, keepdims=True)
    p_top = 1.0 / jnp.sum(jnp.where(row < n_groups, jnp.exp(lg - mg), 0.0), axis=0, keepdims=True)
    sel = logits_t[8:16]
    for g in range(1, n_groups):
        sel = jnp.where(g_idx == g, logits_t[8 + 8 * g:16 + 8 * g], sel)
    m1 = jnp.max(sel, axis=0, keepdims=True)
    i1 = jnp.min(jnp.where(sel == m1, row, big), axis=0, keepdims=True)
    sel2 = jnp.where(row == i1, neg, sel)
    m2 = jnp.max(sel2, axis=0, keepdims=True)
    i2 = jnp.min(jnp.where(sel2 == m2, row, big), axis=0, keepdims=True)
    e2 = jnp.exp(m2 - m1)
    den = 1.0 + e2
    w1 = p_top / den
    w2 = p_top * (e2 / den)
    own = jnp.where(row == i1, w1, 0.0) + jnp.where(row == i2, w2, 0.0)
    blocks = [jnp.where(row == 0, g_idx.astype(F32), 0.0)]
    blocks += [jnp.where(g_idx == g, own, 0.0) for g in range(n_groups)]
    blocks += [jnp.zeros((8, n_tok), F32)] * (ROUTER_LANES // 8 - len(blocks))
    return jnp.concatenate(blocks, axis=0), g_idx


def _mixer_prompt_kernel(x_ref, lbw_ref, g1_ref, win_ref, hgg_ref, cw_ref, cb_ref, wx_ref, bx_ref,
                         wa_ref, ba_ref, lam_ref, wout_ref, g2_ref, wr_ref, br_ref,
                         x1_ref, xn2_ref, gates_ref, cnt_ref, sout_ref, hout_ref, cout_ref,
                         proj_s, k_s, b_s, o_s, st_s, xr_s, a_s, u_s, hcar_s, yb_out_s,
                         *, n_groups, per_group):
    lb_t = x_ref.shape[1]
    wa_w = o_s.shape[1]
    wb_w = a_s.shape[1]
    n_heads = wa_w // HEAD_DIM
    j = pl.program_id(1)
    nj = pl.num_programs(1)

    @pl.when(j == 0)
    def _():
        st_s[...] = jnp.zeros_like(st_s)
        hcar_s[...] = jnp.zeros_like(hcar_s)
        xr_s[0:8, :] = jnp.zeros((8, wb_w), F32)

    x = x_ref[0]
    xn = _rms(x, g1_ref[...]).astype(BF16)
    for c, w in ((wa_w, wa_w), (4 * wa_w, wb_w), (4 * wa_w + wb_w, wb_w), (0, wa_w), (2 * wa_w, wa_w),
                 (3 * wa_w, wa_w)):
        proj_s[:, c:c + w] = jnp.dot(xn, win_ref[:, c:c + w], preferred_element_type=F32)

    xb0 = 4 * wa_w
    xr_s[pl.ds(8, lb_t), :] = proj_s[:, xb0:xb0 + wb_w]
    xc = (cb_ref[...] + cw_ref[3:4, :] * xr_s[pl.ds(8, lb_t), :] + cw_ref[2:3, :] * xr_s[pl.ds(7, lb_t), :]
          + cw_ref[1:2, :] * xr_s[pl.ds(6, lb_t), :] + cw_ref[0:1, :] * xr_s[pl.ds(5, lb_t), :])
    tail = xr_s[pl.ds(lb_t + 5, 3), :]
    xr_s[5:8, :] = tail
    cout_ref[0] = tail
    xcb = xc.astype(BF16)
    gate_x = _sigmoid(jnp.dot(xcb, wx_ref[...], preferred_element_type=F32) + bx_ref[...])
    gate_a = _sigmoid(jnp.dot(xcb, wa_ref[...], preferred_element_type=F32) + ba_ref[...])
    log_a = (-LRU_C) * gate_a * _softplus(-lam_ref[...])
    a = jnp.exp(log_a)
    mult = jnp.sqrt((1.0 - a) * (1.0 + a))
    first =(lax.broadcasted_iota(jnp.int32, (lb_t, 1), 0) == 0) & (j == 0)
    a = jnp.where(first, 0.0, a)
    mult = jnp.where(first, 1.0, mult)
    a_s[...] = a
    u_s[...] = gate_x * xc * mult

    lb = _forget_lower_bound(lbw_ref[...])
    f = lb + (1.0 - lb) * _sigmoid(proj_s[:, wa_w:2 * wa_w])
    for h in range(n_heads):
        k_s[h] = 1.0 - f[:, h * HEAD_DIM:(h + 1) * HEAD_DIM]
    logf = jnp.log(f)
    r_i = lax.broadcasted_iota(jnp.int32, (CHUNK, CHUNK), 0)
    c_i = lax.broadcasted_iota(jnp.int32, (CHUNK, CHUNK), 1)
    tri = jnp.where(r_i >= c_i, 1.0, 0.0).astype(BF16)
    lf_hi = logf.astype(BF16)
    rest = logf - lf_hi.astype(F32)
    lf_mid = rest.astype(BF16)
    lf_lo = (rest - lf_mid.astype(F32)).astype(BF16)
    for c in range(0, lb_t, CHUNK):
        cum = [jnp.dot(tri, part[c:c + CHUNK, :], preferred_element_type=F32) for part in (lf_lo, lf_mid, lf_hi)]
        b_all = LOG2E * ((cum[0] + cum[1]) + cum[2])
        for h in range(n_heads):
            b_s[h, c:c + CHUNK, :] = b_all[:, h * HEAD_DIM:(h + 1) * HEAD_DIM]

    row_sub = lax.broadcasted_iota(jnp.int32, (SUB, HEAD_DIM), 0)
    lane_sub = lax.broadcasted_iota(jnp.int32, (SUB, HEAD_DIM), 1)
    assert n_heads % 2 == 0
    r_kk = lax.broadcasted_iota(jnp.int32, (2 * HEAD_DIM, 2 * HEAD_DIM), 0)
    c_kk = lax.broadcasted_iota(jnp.int32, (2 * HEAD_DIM, 2 * HEAD_DIM), 1)
    ones_kk = jnp.where((r_kk < HEAD_DIM) == (c_kk < HEAD_DIM), 1.0, 0.0).astype(BF16)

    n_sub = CHUNK // SUB
    half = SUB // 2
    lower_left = (row_sub >= half) & (lane_sub < half)
    own_lane = jnp.where(row_sub >= half, half, 0)

    def chunk_start(r0):
        first = []
        for h in range(n_heads):
            hs = slice(h * HEAD_DIM, (h + 1) * HEAD_DIM)
            q = proj_s[pl.ds(r0, CHUNK), hs]
            b = b_s[h, pl.ds(r0, CHUNK), :]
            v = proj_s[pl.ds(r0, CHUNK), 2 * wa_w + h * HEAD_DIM:2 * wa_w + (h + 1) * HEAD_DIM]
            k = k_s[h, pl.ds(r0, CHUNK), :]

            def key_rows(ref, lo, j):
                return jnp.stack([jnp.broadcast_to(ref[h, pl.ds(r0 + lo + hf * half + j, 1), :], (half, HEAD_DIM))
                                  for hf in range(2)])
            vb = v.astype(BF16)
            st = st_s[h]
            b_last = b[CHUNK - 1:CHUNK, :]
            o = lax.dot_general((q * jnp.exp2(b)).astype(BF16), st.astype(BF16), NT_DIMS,
                                preferred_element_type=F32)
            k_end = k * jnp.exp2(b_last - b)
            st_s[h] = st * jnp.exp2(b_last) + lax.dot_general(vb, k_end.astype(BF16), TN_DIMS,
                                                               preferred_element_type=F32)
            terms, off, mid = [], [], []
            for i in range(n_sub):
                lo = i * SUB
                qi, bi, ki = q[lo:lo + SUB], b[lo:lo + SUB], k[lo:lo + SUB]
                q3, b3 = (a.reshape(2, half, HEAD_DIM) for a in (qi, bi))
                terms += [(q3 * (key_rows(k_s, lo, j) * jnp.exp2(b3 - key_rows(b_s, lo, j))))
                          .reshape(SUB, HEAD_DIM).astype(BF16) for j in range(half)]
                rm = bi[half - 1:half]
                mid.append(lax.dot_general((qi * jnp.exp2(bi - rm)).astype(BF16),
                                           (ki * jnp.exp2(rm - bi)).astype(BF16), NT_DIMS,
                                           preferred_element_type=F32))
                if i > 0:
                    r = b[lo - 1:lo]
                    qt = (qi * jnp.exp2(bi - r)).astype(BF16)
                    kt = (k[:lo] * jnp.exp2(r - b[:lo])).astype(BF16)
                    off.append(lax.dot_general(qt, kt, NT_DIMS, preferred_element_type=F32))
            first.append([o, vb, off, jnp.concatenate(terms, axis=0), mid])
        for h in range(0, n_heads, 2):
            both = jnp.concatenate([first[h][3], first[h + 1][3]], axis=1)
            sums = jnp.dot(both, ones_kk, preferred_element_type=F32)
            first[h][3] = sums[:, :HEAD_DIM]
            first[h + 1][3] = sums[:, HEAD_DIM:]
        return first

    def chunk_finish(r0, first):
        for h in range(n_heads):
            o, vb, off, sums, mid = first[h]
            outs = []
            for i in range(n_sub):
                lo = i * SUB
                sc = jnp.zeros((SUB, HEAD_DIM), F32)
                for j in range(half):
                    row0 = (i * half + j) * SUB
                    sc = jnp.where(lane_sub == own_lane + j, sums[row0:row0 + SUB], sc)
                sc = jnp.where(row_sub >= lane_sub, sc, 0.0)[:, :SUB]
                sc = jnp.where(lower_left[:, :SUB], mid[i], sc)
                od = o[lo:lo + SUB] + jnp.dot(sc.astype(BF16), vb[lo:lo + SUB], preferred_element_type=F32)
                if i > 0:
                    od = od + jnp.dot(off[i - 1].astype(BF16), vb[:lo], preferred_element_type=F32)
                outs.append(od)
            o_s[pl.ds(r0, CHUNK), h * HEAD_DIM:(h + 1) * HEAD_DIM] = jnp.concatenate(outs, axis=0)

    def chunks_body(ci, carry):
        rows = [pl.multiple_of((ci * UNROLL + u) * CHUNK, CHUNK) for u in range(UNROLL)]
        started = [chunk_start(r0) for r0 in rows]
        for r0, first in zip(rows, started):
            chunk_finish(r0, first)
        return carry

    row8 = lax.broadcasted_iota(jnp.int32, (8, 1), 0)

    def scan_body(gi, carry):
        r0 = pl.multiple_of(gi * 8, 8)
        aa = a_s[pl.ds(r0, 8), :]
        uu = u_s[pl.ds(r0, 8), :]
        for s in (1, 2, 4):
            m = row8 >= s
            uu = jnp.where(m, aa * pltpu.roll(uu, s, 0) + uu, uu)
            aa = jnp.where(m, aa * pltpu.roll(aa, s, 0), aa)
        hh = aa * carry + uu
        u_s[pl.ds(r0, 8), :] = hh
        return hh[7:8, :]

    h_last = lax.fori_loop(0, lb_t // 8, scan_body, hcar_s[...])
    hcar_s[...] = h_last
    hout_ref[0] = h_last
    yb = (u_s[...] * _gelu_tanh(proj_s[:, xb0 + wb_w:xb0 + 2 * wb_w])).astype(BF16)
    yb_out_s[...] = jnp.dot(yb, wout_ref[wa_w:, :], preferred_element_type=F32)

    assert lb_t % (CHUNK * UNROLL) == 0
    lax.fori_loop(0, lb_t // (CHUNK * UNROLL), chunks_body, 0)

    ya = []
    for h in range(n_heads):
        hs = slice(h * HEAD_DIM, (h + 1) * HEAD_DIM)
        oh = o_s[:, hs]
        oh = oh * lax.rsqrt(jnp.mean(oh * oh, axis=-1, keepdims=True) + EPS) * hgg_ref[:, hs]
        ya.append(oh * _silu(proj_s[:, 3 * wa_w + h * HEAD_DIM:3 * wa_w + (h + 1) * HEAD_DIM]))

    ya = jnp.concatenate(ya, axis=-1).astype(BF16)
    x1 = x + (jnp.dot(ya, wout_ref[:wa_w, :], preferred_element_type=F32) + yb_out_s[...])
    x1_ref[0] = x1
    xn2 = _rms(x1, g2_ref[...]).astype(BF16)
    xn2_ref[0] = xn2
    logits_t = lax.dot_general(wr_ref[...], xn2, NT_DIMS, preferred_element_type=F32) + br_ref[...]
    gates_t, g_idx = _route_rows(logits_t, n_groups, per_group)
    gates_ref[0] = gates_t.T
    lane_c = lax.broadcasted_iota(jnp.int32, (1, ROUTER_LANES), 1)
    for t in range(lb_t // TOK_TILE):
        gi = g_idx[:, t * TOK_TILE:(t + 1) * TOK_TILE]
        cnt = jnp.zeros((1, ROUTER_LANES), jnp.int32)
        for g in range(n_groups):
            cnt = cnt + jnp.where(lane_c == g, jnp.sum(jnp.where(gi == g, 1, 0), axis=1, keepdims=True), 0)
        cnt_ref[0, t:t + 1, :] = cnt

    @pl.when(j == nj - 1)
    def _():
        for h in range(n_heads):
            sout_ref[0, h] = st_s[h].T


def _const_spec(shape):
    nd = len(shape)
    return pl.BlockSpec(shape, lambda *_: (0,) * nd)


def _mixer_prompt(x, p, lb_t):
    bsz, seq, d = x.shape
    wa_w = p["hgg"].shape[1]
    wb_w = p["cb"].shape[1]
    n_heads = wa_w // HEAD_DIM
    n_cols = p["win_bf"].shape[1]
    weights = [p["lbw"], p["g1"], p["win_bf"], p["hgg"], p["cw"], p["cb"], p["wx_bf"], p["bx"],
               p["wa_bf"], p["ba"], p["lam"], p["wout_bf"], p["g2"], p["wr_t_bf"], p["br_t"]]
    nj = seq // lb_t
    tile = lambda w: pl.BlockSpec((1, lb_t, w), lambda b, j: (b, j, 0))
    out_shape = (
        jax.ShapeDtypeStruct((bsz, seq, d), F32),
        jax.ShapeDtypeStruct((bsz, seq, d), BF16),
        jax.ShapeDtypeStruct((bsz, seq, ROUTER_LANES), F32),
        jax.ShapeDtypeStruct((bsz * (seq // lb_t), lb_t // TOK_TILE, ROUTER_LANES), jnp.int32),
        jax.ShapeDtypeStruct((bsz, n_heads, HEAD_DIM, HEAD_DIM), F32),
        jax.ShapeDtypeStruct((bsz, 1, wb_w), F32),
        jax.ShapeDtypeStruct((bsz, 3, wb_w), F32),
    )
    out_specs = (
        tile(d), tile(d), tile(ROUTER_LANES),
        pl.BlockSpec((1, lb_t // TOK_TILE, ROUTER_LANES), lambda b, j: (b * nj + j, 0, 0)),
        pl.BlockSpec((1, n_heads, HEAD_DIM, HEAD_DIM), lambda b, j: (b, 0, 0, 0)),
        pl.BlockSpec((1, 1, wb_w), lambda b, j: (b, 0, 0)),
        pl.BlockSpec((1, 3, wb_w), lambda b, j: (b, 0, 0)),
    )
    scratch = [
        pltpu.VMEM((lb_t, n_cols), F32),
        pltpu.VMEM((n_heads, lb_t, HEAD_DIM), F32),
        pltpu.VMEM((n_heads, lb_t, HEAD_DIM), F32),
        pltpu.VMEM((lb_t, wa_w), F32),
        pltpu.VMEM((n_heads, HEAD_DIM, HEAD_DIM), F32),
        pltpu.VMEM((lb_t + 8, wb_w), F32),
        pltpu.VMEM((lb_t, wb_w), F32),
        pltpu.VMEM((lb_t, wb_w), F32),
        pltpu.VMEM((1, wb_w), F32),
        pltpu.VMEM((lb_t, d), F32),
    ]
    kern = functools.partial(_mixer_prompt_kernel, n_groups=p["n_groups"], per_group=p["per_group"])
    return pl.pallas_call(
        kern,
        grid=(bsz, nj),
        in_specs=[tile(d)] + [_const_spec(w.shape) for w in weights],
        out_specs=out_specs,
        out_shape=out_shape,
        scratch_shapes=scratch,
        compiler_params=pltpu.CompilerParams(dimension_semantics=("arbitrary", "arbitrary"),
                                             vmem_limit_bytes=VMEM_LIMIT_BYTES),
        name="mixer_prompt",
    )(x, *weights)


def _moe_kernel(x1_ref, xn2_ref, gates_ref, w1_ref, w3_ref, w2_ref, gf_ref, y_ref, acc_s, *, n_groups):
    e = pl.program_id(1)

    @pl.when(e == 0)
    def _():
        acc_s[...] = jnp.zeros_like(acc_s)

    xb = xn2_ref[...]
    per_step = w1_ref.shape[0]
    gates = pltpu.roll(gates_ref[...], ROUTER_LANES - n_groups - per_step * e, 1)
    acc = acc_s[...]
    for s in range(per_step):
        h = (_silu(jnp.dot(xb, w1_ref[s], preferred_element_type=F32))
             * jnp.dot(xb, w3_ref[s], preferred_element_type=F32) * gates[:, s:s + 1])
        acc = acc + jnp.dot(h.astype(BF16), w2_ref[s].astype(BF16), preferred_element_type=F32)
    acc_s[...] = acc

    @pl.when(e == pl.num_programs(1) - 1)
    def _():
        y_ref[...] = _rms(x1_ref[...] + acc_s[...], gf_ref[...])


def _moe_dense(x1, xn2, gates, p, tm):
    t, d = x1.shape
    n_exp, _, d_exp = p["w1_bf"].shape
    per_step = p["per_group"]
    row = lambda w: pl.BlockSpec((tm, w), lambda i, e: (i, 0))
    return pl.pallas_call(
        functools.partial(_moe_kernel, n_groups=p["n_groups"]),
        grid=(t // tm, n_exp // per_step),
        in_specs=[row(d), row(d), row(ROUTER_LANES),
                  pl.BlockSpec((per_step, d, d_exp), lambda i, e: (e, 0, 0)),
                  pl.BlockSpec((per_step, d, d_exp), lambda i, e: (e, 0, 0)),
                  pl.BlockSpec((per_step, d_exp, d), lambda i, e: (e, 0, 0)),
                  _const_spec(p["gf"].shape)],
        out_specs=row(d),
        out_shape=jax.ShapeDtypeStruct((t, d), F32),
        scratch_shapes=[pltpu.VMEM((tm, d), F32)],
        compiler_params=pltpu.CompilerParams(dimension_semantics=("arbitrary", "arbitrary"),
                                             vmem_limit_bytes=VMEM_LIMIT_BYTES),
        name="moe_dense",
    )(x1, xn2, gates, p["w1_bf"], p["w3_bf"], p["w2"], p["gf"])


def _seg_pad(n):
    return jnp.bitwise_and(n + (SEG - 1), -SEG)


def _row_tile_pad(n):
    return jnp.bitwise_and(n + (ROW_TILE - 1), -ROW_TILE)


def _tile_segments(cnt_ref, tile, n_groups):
    padded = [_seg_pad(cnt_ref[tile * n_groups + g]) for g in range(n_groups)]
    starts, acc = [], jnp.int32(0)
    for g in range(n_groups):
        starts.append(acc)
        acc = acc + padded[g]
    return padded, starts, acc


def _group_bases(cnt_ref, n_tiles, n_groups):
    def body(t, tot):
        return tuple(tot[g] + _seg_pad(cnt_ref[t * n_groups + g]) for g in range(n_groups))
    totals = lax.fori_loop(0, n_tiles, body, (jnp.int32(0),) * n_groups)
    bases, ends, acc = [], [], jnp.int32(0)
    for g in range(n_groups):
        bases.append(acc)
        acc = acc + _row_tile_pad(totals[g])
        ends.append(acc)
    return bases, ends


def _chunk_rows(n_chunks, starts, gstart):
    rows, row = [], None
    for c in range(n_chunks):
        row = gstart[0] if c == 0 else row + SEG
        for g in range(1, len(starts)):
            row = jnp.where(starts[g] == c * SEG, gstart[g], row)
        rows.append(pl.multiple_of(row, SEG))
    return rows


def _sort_matrix(gates, starts):
    n_groups = len(starts)
    col = lax.broadcasted_iota(jnp.int32, (TOK_TILE, ROUTER_LANES), 1)
    gi = gates[:, 0:1].astype(jnp.int32)
    onehot = col == gi
    r_i = lax.broadcasted_iota(jnp.int32, (TOK_TILE, TOK_TILE), 0)
    c_i = lax.broadcasted_iota(jnp.int32, (TOK_TILE, TOK_TILE), 1)
    earlier = jnp.where(r_i > c_i, 1.0, 0.0).astype(BF16)
    before = jnp.dot(earlier, jnp.where(onehot, 1.0, 0.0).astype(BF16), preferred_element_type=F32)
    rank = jnp.sum(jnp.where(onehot, before, 0.0), axis=-1, keepdims=True).astype(jnp.int32)
    base = jnp.zeros((TOK_TILE, 1), jnp.int32)
    for g in range(n_groups):
        base = base + jnp.where(gi == g, starts[g], 0)
    lane = lax.broadcasted_iota(jnp.int32, (TOK_TILE, SORT_ROWS), 1)
    return lane == base + rank


def _sort_matrix_rows(gates, starts):
    n_groups = len(starts)
    col = lax.broadcasted_iota(jnp.int32, (TOK_TILE, ROUTER_LANES), 1)
    onehot_t = jnp.where(col == gates[:, 0:1].astype(jnp.int32), 1.0, 0.0).T
    r_i = lax.broadcasted_iota(jnp.int32, (TOK_TILE, TOK_TILE), 0)
    c_i = lax.broadcasted_iota(jnp.int32, (TOK_TILE, TOK_TILE), 1)
    later = jnp.where(r_i < c_i, 1.0, 0.0).astype(BF16)
    before = jnp.dot(onehot_t.astype(BF16), later, preferred_element_type=F32)
    g_row = lax.broadcasted_iota(jnp.int32, (ROUTER_LANES, 1), 0)
    base = jnp.zeros((ROUTER_LANES, 1), jnp.int32)
    for g in range(n_groups):
        base = base + jnp.where(g_row == g, starts[g], 0)
    dest = jnp.sum(jnp.where(onehot_t > 0.0, before + base.astype(F32), 0.0), axis=0, keepdims=True)
    row = lax.broadcasted_iota(jnp.int32, (SORT_ROWS, TOK_TILE), 0)
    return row == dest.astype(jnp.int32)


def _dispatch_copies(xbuf, gbuf, xs_hbm, gs_hbm, sem, slot, c, row):
    return (pltpu.make_async_copy(xbuf.at[slot, pl.ds(c * SEG, SEG), :], xs_hbm.at[pl.ds(row, SEG), :], sem.at[slot]),
            pltpu.make_async_copy(gbuf.at[slot, pl.ds(c * SEG, SEG), :], gs_hbm.at[pl.ds(row, SEG), :], sem.at[slot]))


def _moe_dispatch_kernel(cnt_ref, xn2_ref, gates_ref, xs_in, gs_in, xs_hbm, gs_hbm, tg_ref,
                         xbuf, gbuf, sem, gstart_s, *, n_groups, per_group):
    del xs_in, gs_in
    j = pl.program_id(0)
    n_tiles = pl.num_programs(0)
    n_chunks = (TOK_TILE + n_groups * SEG) // SEG
    slot = lax.rem(j, 2)

    @pl.when(j == 0)
    def _():
        bases, ends = _group_bases(cnt_ref, n_tiles, n_groups)
        for g in range(n_groups):
            gstart_s[g] = bases[g]
        n_rt = tg_ref.shape[0] - 1
        for i in range(n_rt):
            tg = jnp.int32(0)
            for g in range(n_groups):
                tg = tg + jnp.where(ends[g] <= i * ROW_TILE, 1, 0)
            tg_ref[i] = tg
        tg_ref[n_rt] = ends[n_groups - 1] // ROW_TILE

    def wait_tile(tile, slot_):
        _, _, used = _tile_segments(cnt_ref, tile, n_groups)
        for c in range(n_chunks):
            @pl.when(c * SEG < used)
            def _():
                for cp in _dispatch_copies(xbuf, gbuf, xs_hbm, gs_hbm, sem, slot_, c, 0):
                    cp.wait()

    @pl.when(j >= 2)
    def _():
        wait_tile(j - 2, slot)

    padded, starts, used = _tile_segments(cnt_ref, j, n_groups)
    gates = gates_ref[...]
    d = xn2_ref.shape[1]
    g_hi = gates.astype(BF16)
    g_lo = (gates - g_hi.astype(F32)).astype(BF16)
    sort_m = jnp.where(_sort_matrix_rows(gates, starts), 1.0, 0.0).astype(BF16)
    moved = jnp.dot(sort_m, jnp.concatenate([xn2_ref[...], g_hi, g_lo], axis=1), preferred_element_type=F32)
    xbuf[slot] = moved[:, :d].astype(BF16)
    gbuf[slot] = moved[:, d:d + ROUTER_LANES] + moved[:, d + ROUTER_LANES:]
    gstart = [gstart_s[g] for g in range(n_groups)]
    rows = _chunk_rows(n_chunks, starts, gstart)
    for c in range(n_chunks):
        @pl.when(c * SEG < used)
        def _():
            for cp in _dispatch_copies(xbuf, gbuf, xs_hbm, gs_hbm, sem, slot, c, rows[c]):
                cp.start()
    for g in range(n_groups):
        gstart_s[g] = gstart[g] + padded[g]

    @pl.when(j == n_tiles - 1)
    def _():
        @pl.when(j >= 1)
        def _():
            wait_tile(j - 1, 1 - slot)
        wait_tile(j, slot)


def _moe_group_kernel(tg_ref, xs_ref, gs_ref, w1_ref, w3_ref, w2_ref, ys_ref, w2_s, *, n_groups, per_group):
    i = pl.program_id(0)

    @pl.when((i == 0) | (tg_ref[i] != tg_ref[jnp.maximum(i - 1, 0)]))
    def _():
        w2_s[...] = w2_ref[...].astype(BF16)

    @pl.when(tg_ref[i] < n_groups)
    def _():
        xb = xs_ref[...]
        gates = pltpu.roll(gs_ref[...], ROUTER_LANES - PROMPT_EXPERT_LANE - per_group * tg_ref[i], 1)
        acc = jnp.zeros(ys_ref.shape, F32)
        for s in range(per_group):
            h = (_silu(jnp.dot(xb, w1_ref[s], preferred_element_type=F32))
                 * jnp.dot(xb, w3_ref[s], preferred_element_type=F32) * gates[:, s:s + 1])
            acc = acc + jnp.dot(h.astype(BF16), w2_s[s], preferred_element_type=F32)
        ys_ref[...] = acc.astype(BF16)


def _moe_combine_kernel(cnt_ref, gates_ref, x1_ref, gf_ref, ys_hbm, y_ref, ybuf, sem, gstart_s, *, n_groups):
    j = pl.program_id(0)
    n_tiles = pl.num_programs(0)
    n_chunks = SORT_ROWS // SEG
    slot = lax.rem(j, 2)

    def copies(slot_, c, row):
        return pltpu.make_async_copy(ys_hbm.at[pl.ds(row, SEG), :], ybuf.at[slot_, pl.ds(c * SEG, SEG), :],
                                     sem.at[slot_])

    def fetch_tile(tile, slot_):
        padded, starts, _ = _tile_segments(cnt_ref, tile, n_groups)
        gstart = [gstart_s[g] for g in range(n_groups)]
        for c, row in enumerate(_chunk_rows(n_chunks, starts, gstart)):
            copies(slot_, c, row).start()
        for g in range(n_groups):
            gstart_s[g] = gstart[g] + padded[g]

    @pl.when(j == 0)
    def _():
        bases, _ = _group_bases(cnt_ref, n_tiles, n_groups)
        for g in range(n_groups):
            gstart_s[g] = bases[g]
        fetch_tile(0, 0)

    @pl.when(j + 1 < n_tiles)
    def _():
        fetch_tile(j + 1, 1 - slot)

    for c in range(n_chunks):
        copies(slot, c, 0).wait()
    _, starts, _ = _tile_segments(cnt_ref, j, n_groups)
    sort_t = _sort_matrix(gates_ref[...], starts)
    moe = jnp.dot(jnp.where(sort_t, 1.0, 0.0).astype(BF16), ybuf[slot], preferred_element_type=F32)
    y_ref[...] = _rms(x1_ref[...] + moe, gf_ref[...])


def _moe_sorted(x1, xn2, gates, counts, p):
    t, d = x1.shape
    n_groups, per_group = p["n_groups"], p["per_group"]
    assert n_groups <= MAX_GROUPS and t % TOK_TILE == 0
    n_tiles = t // TOK_TILE
    d_exp = p["w1_bf"].shape[2]
    cap = t + n_tiles * n_groups * SEG + n_groups * ROW_TILE + SORT_ROWS
    n_rt = -(-cap // ROW_TILE)
    rows = n_rt * ROW_TILE
    cnt = counts[:, :, :n_groups].reshape(-1)
    params = pltpu.CompilerParams(dimension_semantics=("arbitrary",), vmem_limit_bytes=VMEM_LIMIT_BYTES)
    any_spec = pl.BlockSpec(memory_space=pl.ANY)
    tok = lambda w: pl.BlockSpec((TOK_TILE, w), lambda j, c: (j, 0))

    xs, gs, tile_group = pl.pallas_call(
        functools.partial(_moe_dispatch_kernel, n_groups=n_groups, per_group=per_group),
        grid_spec=pltpu.PrefetchScalarGridSpec(
            num_scalar_prefetch=1, grid=(n_tiles,),
            in_specs=[tok(d), tok(ROUTER_LANES), any_spec, any_spec],
            out_specs=(any_spec, any_spec, pl.BlockSpec(memory_space=pltpu.SMEM)),
            scratch_shapes=[pltpu.VMEM((2, SORT_ROWS, d), BF16), pltpu.VMEM((2, SORT_ROWS, ROUTER_LANES), F32),
                            pltpu.SemaphoreType.DMA((2,)), pltpu.SMEM((n_groups,), jnp.int32)]),
        out_shape=(jax.ShapeDtypeStruct((rows, d), BF16), jax.ShapeDtypeStruct((rows, ROUTER_LANES), F32),
                   jax.ShapeDtypeStruct((n_rt + 1,), jnp.int32)),
        input_output_aliases={3: 0, 4: 1},
        compiler_params=params, name="moe_dispatch",
    )(cnt, xn2, gates, jnp.zeros((rows, d), BF16), jnp.zeros((rows, ROUTER_LANES), F32))

    used_tile = lambda i, tg: (jnp.minimum(i, tg[n_rt] - 1), 0)
    group_w = lambda i, tg: (jnp.minimum(tg[i], n_groups - 1), 0, 0)
    ys = pl.pallas_call(
        functools.partial(_moe_group_kernel, n_groups=n_groups, per_group=per_group),
        grid_spec=pltpu.PrefetchScalarGridSpec(
            num_scalar_prefetch=1, grid=(n_rt,),
            in_specs=[pl.BlockSpec((ROW_TILE, d), used_tile), pl.BlockSpec((ROW_TILE, ROUTER_LANES), used_tile),
                      pl.BlockSpec((per_group, d, d_exp), group_w), pl.BlockSpec((per_group, d, d_exp), group_w),
                      pl.BlockSpec((per_group, d_exp, d), group_w)],
            out_specs=pl.BlockSpec((ROW_TILE, d), used_tile),
            scratch_shapes=[pltpu.VMEM((per_group, d_exp, d), BF16)]),
        out_shape=jax.ShapeDtypeStruct((rows, d), BF16),
        input_output_aliases={1: 0},
        compiler_params=params, name="moe_experts",
    )(tile_group, xs, gs, p["w1_bf"], p["w3_bf"], p["w2"])

    return pl.pallas_call(
        functools.partial(_moe_combine_kernel, n_groups=n_groups),
        grid_spec=pltpu.PrefetchScalarGridSpec(
            num_scalar_prefetch=1, grid=(n_tiles,),
            in_specs=[tok(ROUTER_LANES), tok(d), pl.BlockSpec(p["gf"].shape, lambda j, c: (0, 0)), any_spec],
            out_specs=tok(d),
            scratch_shapes=[pltpu.VMEM((2, SORT_ROWS, d), BF16), pltpu.SemaphoreType.DMA((2,)),
                            pltpu.SMEM((n_groups,), jnp.int32)]),
        out_shape=jax.ShapeDtypeStruct((t, d), F32),
        compiler_params=params, name="moe_combine",
    )(cnt, gates, x1, p["gf"], ys)


def _sample_in_kernel(x_ref, c0_ref, c1_ref, c2_ref, h0_ref, lbw_ref, g1_ref, win_ref, cw_ref, cb_ref,
                      wx_ref, bx_ref, wa_ref, ba_ref, lam_ref,
                      q_ref, f_ref, v_ref, g_ref, yb_ref, hnew_ref, xr_ref):
    wa_w = v_ref.shape[1]
    wb_w = yb_ref.shape[1]
    xn = _rms(x_ref[...], g1_ref[...])
    proj = jnp.dot(xn, win_ref[...], precision=HIGHEST, preferred_element_type=F32)
    lb = _forget_lower_bound(lbw_ref[...])
    f = lb + (1.0 - lb) * _sigmoid(proj[:, wa_w:2 * wa_w])
    q_ref[...] = proj[:, 0:wa_w].T
    f_ref[...] = f.T
    v_ref[...] = proj[:, 2 * wa_w:3 * wa_w]
    g_ref[...] = proj[:, 3 * wa_w:4 * wa_w]
    xr = proj[:, 4 * wa_w:4 * wa_w + wb_w]
    xr_ref[...] = xr
    xc = (cb_ref[...] + cw_ref[0:1, :] * c0_ref[...] + cw_ref[1:2, :] * c1_ref[...]
          + cw_ref[2:3, :] * c2_ref[...] + cw_ref[3:4, :] * xr)
    gate_x = _sigmoid(jnp.dot(xc, wx_ref[...], precision=HIGHEST, preferred_element_type=F32) + bx_ref[...])
    gate_a = _sigmoid(jnp.dot(xc, wa_ref[...], precision=HIGHEST, preferred_element_type=F32) + ba_ref[...])
    log_a = (-LRU_C) * gate_a * _softplus(-lam_ref[...])
    a = jnp.exp(log_a)
    mult = jnp.sqrt(-_expm1(2.0 * log_a))
    h = a * h0_ref[...] + gate_x * xc * mult
    hnew_ref[...] = h
    yb_ref[...] = h * _gelu_tanh(proj[:, 4 * wa_w + wb_w:4 * wa_w + 2 * wb_w])


def _sample_state_kernel(s_ref, qt_ref, ft_ref, v_ref, snew_ref, o_ref):
    tb, n_heads = s_ref.shape[0], s_ref.shape[1]
    n_tok = qt_ref.shape[1]
    shift = lax.rem(n_tok - pl.program_id(0) * tb, n_tok)
    sq = (HEAD_DIM, HEAD_DIM)
    for h in range(n_heads):
        hs = slice(h * HEAD_DIM, (h + 1) * HEAD_DIM)
        qh = pltpu.roll(qt_ref[hs, :], shift, 1)
        fh = pltpu.roll(ft_ref[hs, :], shift, 1)
        rows = []
        for t in range(tb):
            f_all = jnp.broadcast_to(fh[:, t:t + 1], sq)
            s_new = f_all * s_ref[t, h] + (1.0 - f_all) * v_ref[t:t + 1, hs]
            snew_ref[t, h] = s_new
            rows.append(jnp.sum(jnp.broadcast_to(qh[:, t:t + 1], sq) * s_new, axis=0, keepdims=True))
        o_ref[:, hs] = jnp.concatenate(rows, axis=0)


def _sample_out_kernel(x_ref, o_ref, g_ref, yb_ref, hgg_ref, wout_ref, g2_ref, wr_ref, br_ref,
                       x1_ref, xn2_ref, gates_ref, *, n_groups, per_group):
    wa_w = o_ref.shape[1]
    ya = []
    for h in range(wa_w // HEAD_DIM):
        hs = slice(h * HEAD_DIM, (h + 1) * HEAD_DIM)
        oh = o_ref[:, hs]
        oh = oh * lax.rsqrt(jnp.mean(oh * oh, axis=-1, keepdims=True) + EPS) * hgg_ref[:, hs]
        ya.append(oh * _silu(g_ref[:, hs]))
    y = jnp.concatenate(ya + [yb_ref[...]], axis=-1)
    x1 = x_ref[...] + jnp.dot(y, wout_ref[...], precision=HIGHEST, preferred_element_type=F32)
    x1_ref[...] = x1
    xn2 = _rms(x1, g2_ref[...])
    xn2_ref[...] = xn2.astype(BF16)
    logits = jnp.dot(xn2, wr_ref[...], precision=HIGHEST, preferred_element_type=F32) + br_ref[...]
    gates_ref[...] = _route(logits, n_groups, per_group)


def _whole(kernel, out_shape, *args, name):
    return pl.pallas_call(
        kernel, out_shape=out_shape,
        compiler_params=pltpu.CompilerParams(vmem_limit_bytes=VMEM_LIMIT_BYTES), name=name)(*args)


def _mixer_sample(x, s0, h0, c0, p, tb):
    n, d = x.shape
    wa_w = p["hgg"].shape[1]
    wb_w = p["cb"].shape[1]
    n_heads = wa_w // HEAD_DIM
    sd = lambda w: jax.ShapeDtypeStruct((n, w), F32)
    key_major = jax.ShapeDtypeStruct((wa_w, n), F32)
    q, f, v, g, yb, h_new, xr = _whole(
        _sample_in_kernel, (key_major,) * 2 + (sd(wa_w),) * 2 + (sd(wb_w),) * 3,
        x, c0[:, 0, :], c0[:, 1, :], c0[:, 2, :], h0, p["lbw"], p["g1"], p["win"], p["cw"], p["cb"],
        p["wx"], p["bx"], p["wa"], p["ba"], p["lam"], name="sample_in")
    tok = lambda w: pl.BlockSpec((tb, w), lambda i: (i, 0))
    st = pl.BlockSpec((tb, n_heads, HEAD_DIM, HEAD_DIM), lambda i: (i, 0, 0, 0))
    s_new, o = pl.pallas_call(
        _sample_state_kernel,
        grid=(n // tb,),
        in_specs=[st, _const_spec((wa_w, n)), _const_spec((wa_w, n)), tok(wa_w)],
        out_specs=(st, tok(wa_w)),
        out_shape=(jax.ShapeDtypeStruct(s0.shape, F32), sd(wa_w)),
        compiler_params=pltpu.CompilerParams(dimension_semantics=("arbitrary",),
                                             vmem_limit_bytes=VMEM_LIMIT_BYTES),
        name="sample_state",
    )(s0, q, f, v)
    x1, xn2, gates = _whole(
        functools.partial(_sample_out_kernel, n_groups=p["n_groups"], per_group=p["per_group"]),
        (sd(d), jax.ShapeDtypeStruct((n, d), BF16), sd(ROUTER_LANES)),
        x, o, g, yb, p["hgg"], p["wout"], p["g2"], p["wr"], p["br"], name="sample_out")
    c_new = jnp.stack([c0[:, 1, :], c0[:, 2, :], xr], axis=1)
    return x1, xn2, gates, s_new, h_new, c_new


def _block_diag(w):
    n, c, _ = w.shape
    eye = jnp.eye(n, dtype=w.dtype)
    return (w[:, :, None, :] * eye[:, None, :, None]).reshape(n * c, n * c)


def _prepare(lower_bounds, ln1_g, w_in, hgrn_norm_g, conv_w, conv_b, lru_wx, lru_bx, lru_wa, lru_ba,
             lru_lambda, w_out, ln2_g, router_wg, router_bg, router_we, router_be, exp_w1, exp_w3,
             exp_w2, final_g):
    d = w_in.shape[1]
    n_groups = router_wg.shape[-1]
    per_group = router_we.shape[-1]
    row = lambda a: a.reshape(1, -1).astype(F32)
    we = jnp.transpose(router_we[0], (1, 0, 2)).reshape(d, n_groups * per_group)
    pad = ROUTER_LANES - n_groups - n_groups * per_group
    wr = jnp.concatenate([router_wg[0], we, jnp.zeros((d, pad), F32)], axis=1)
    br = jnp.concatenate([router_bg[0], router_be[0].reshape(-1), jnp.zeros((pad,), F32)]).reshape(1, -1)
    rows_t = -(-(8 + n_groups * per_group) // 16) * 16
    wr_t = jnp.concatenate([router_wg[0].T, jnp.zeros((8 - n_groups, d), F32), we.T,
                            jnp.zeros((rows_t - 8 - n_groups * per_group, d), F32)], axis=0)
    br_t = jnp.concatenate([router_bg[0], jnp.zeros((8 - n_groups,), F32), router_be[0].reshape(-1),
                            jnp.zeros((rows_t - 8 - n_groups * per_group,), F32)]).reshape(-1, 1)
    wx = _block_diag(lru_wx[0])
    wa = _block_diag(lru_wa[0])
    return dict(
        wr_t_bf=wr_t.astype(BF16), br_t=br_t,
        n_groups=n_groups, per_group=per_group,
        lbw=lower_bounds.astype(F32), g1=row(ln1_g[0]), win=w_in[0], win_bf=w_in[0].astype(BF16),
        hgg=row(hgrn_norm_g[0]), cw=conv_w[0], cb=row(conv_b[0]),
        wx=wx, wx_bf=wx.astype(BF16), bx=row(lru_bx[0]), wa=wa, wa_bf=wa.astype(BF16), ba=row(lru_ba[0]),
        lam=row(lru_lambda[0]), wout=w_out[0], wout_bf=w_out[0].astype(BF16), g2=row(ln2_g[0]),
        wr=wr, br=br, w1_bf=exp_w1[0].astype(BF16), w3_bf=exp_w3[0].astype(BF16),
        w2=exp_w2[0], gf=row(final_g))


def kernel(x_prompt, x_sample, state_hgrn, state_rglru, state_conv, lower_bounds, ln1_g, w_in, hgrn_norm_g, conv_w, conv_b, lru_wx, lru_bx, lru_wa, lru_ba, lru_lambda, w_out, ln2_g, router_wg, router_bg, router_we, router_be, exp_w1, exp_w3, exp_w2, final_g):
    assert w_in.shape[0] == 1, "single-layer trunk"
    p = _prepare(lower_bounds, ln1_g, w_in, hgrn_norm_g, conv_w, conv_b, lru_wx, lru_bx, lru_wa, lru_ba,
                 lru_lambda, w_out, ln2_g, router_wg, router_bg, router_we, router_be, exp_w1, exp_w3,
                 exp_w2, final_g)
    bsz, seq, d = x_prompt.shape
    x1, xn2, gates, counts, s_p, h_p, c_p = _mixer_prompt(x_prompt, p, min(MIXER_BLOCK, seq))
    t = bsz * seq
    y_p = _moe_sorted(x1.reshape(t, d), xn2.reshape(t, d), gates.reshape(t, ROUTER_LANES), counts, p)

    n = x_sample.shape[0]
    x1s, xn2s, gates_s, s_s, h_s, c_s = _mixer_sample(x_sample[:, 0, :], state_hgrn[0], state_rglru[0],
                                                      state_conv[0], p, SAMPLE_STEP_TOKENS)
    y_s = _moe_dense(x1s, xn2s, gates_s, p, n)
    return (y_p.reshape(bsz, seq, d), y_s.reshape(n, 1, d),
            s_p[None], h_p.reshape(1, bsz, -1), c_p[None],
            s_s[None], h_s[None], c_s[None])
```

```python
import functools

import jax
import jax.numpy as jnp
from jax import lax
from jax.experimental import pallas as pl
from jax.experimental.pallas import tpu as pltpu

F32 = jnp.float32
BF16 = jnp.bfloat16
HIGHEST = lax.Precision.HIGHEST

EPS = 1e-6
LRU_C = 8.0
LOG2E = 1.4426950408889634
HEAD_DIM = 128
CHUNK = 64
SUB = 16
UNROLL = 8
ROUTER_LANES = 128
PROMPT_EXPERT_LANE = 8
MIXER_BLOCK = 512
SAMPLE_STEP_TOKENS = 8
TOK_TILE = 512
SEG = 16
MAX_GROUPS = 8
SORT_ROWS = TOK_TILE + MAX_GROUPS * SEG
ROW_TILE = 512
VMEM_LIMIT_BYTES = 56 * 1024 * 1024

NT_DIMS = (((1,), (1,)), ((), ()))
TN_DIMS = (((0,), (0,)), ((), ()))


def _rms(x, g):
    return x * lax.rsqrt(jnp.mean(x * x, axis=-1, keepdims=True) + EPS) * g


def _sigmoid(x):
    return 1.0 / (1.0 + jnp.exp(-x))


def _silu(x):
    return x * _sigmoid(x)


def _gelu_tanh(x):
    c = 0.7978845608028654
    return x * (0.5 * (1.0 + jnp.tanh(c * (x + 0.044715 * (x * x * x)))))


def _softplus(z):
    return jnp.maximum(z, 0.0) + jnp.log1p(jnp.exp(-jnp.abs(z)))


def _expm1(x):
    u = jnp.exp(x)
    um1 = u - 1.0
    small = um1 * x / jnp.log(u)
    return jnp.where(um1 == 0.0, x, jnp.where(jnp.abs(x) < 0.5, small, um1))


def _forget_lower_bound(lbw):
    m = jnp.max(lbw, axis=0, keepdims=True)
    e = jnp.exp(lbw - m)
    return e[0:1, :] / jnp.sum(e, axis=0, keepdims=True)


def _route(logits, n_groups, per_group):
    n = logits.shape[-1]
    col = lax.broadcasted_iota(jnp.int32, logits.shape, 1)
    neg = jnp.float32(-jnp.inf)
    big = jnp.int32(n)
    is_g = col < n_groups
    lg = jnp.where(is_g, logits, neg)
    mg = jnp.max(lg, axis=-1, keepdims=True)
    g_idx = jnp.min(jnp.where(lg == mg, col, big), axis=-1, keepdims=True)
    p_top = 1.0 / jnp.sum(jnp.where(is_g, jnp.exp(logits - mg), 0.0), axis=-1, keepdims=True)
    lo = n_groups + per_group * g_idx
    le = jnp.where((col >= lo) & (col < lo + per_group), logits, neg)
    m1 = jnp.max(le, axis=-1, keepdims=True)
    i1 = jnp.min(jnp.where(le == m1, col, big), axis=-1, keepdims=True)
    le2 = jnp.where(col == i1, neg, le)
    m2 = jnp.max(le2, axis=-1, keepdims=True)
    i2 = jnp.min(jnp.where(le2 == m2, col, big), axis=-1, keepdims=True)
    e2 = jnp.exp(m2 - m1)
    den = 1.0 + e2
    w1 = p_top / den
    w2 = p_top * (e2 / den)
    gates = jnp.where(col == i1, w1, 0.0) + jnp.where(col == i2, w2, 0.0)
    return gates + jnp.where(col == 0, g_idx.astype(F32), 0.0)


def _route_rows(logits_t, n_groups, per_group):
    assert n_groups <= 8 and per_group == 8
    n_tok = logits_t.shape[1]
    row = lax.broadcasted_iota(jnp.int32, (8, n_tok), 0)
    neg = jnp.float32(-jnp.inf)
    big = jnp.int32(8)
    lg = jnp.where(row < n_groups, logits_t[0:8], neg)
    mg = jnp.max(lg, axis=0, keepdims=True)
    g_idx = jnp.min(jnp.where(lg == mg, row, big), axis=0, keepdims=True)
    p_top = 1.0 / jnp.sum(jnp.where(row < n_groups, jnp.exp(lg - mg), 0.0), axis=0, keepdims=True)
    sel = logits_t[8:16]
    for g in range(1, n_groups):
        sel = jnp.where(g_idx == g, logits_t[8 + 8 * g:16 + 8 * g], sel)
    m1 = jnp.max(sel, axis=0, keepdims=True)
    i1 = jnp.min(jnp.where(sel == m1, row, big), axis=0, keepdims=True)
    sel2 = jnp.where(row == i1, neg, sel)
    m2 = jnp.max(sel2, axis=0, keepdims=True)
    i2 = jnp.min(jnp.where(sel2 == m2, row, big), axis=0, keepdims=True)
    e2 = jnp.exp(m2 - m1)
    den = 1.0 + e2
    w1 = p_top / den
    w2 = p_top * (e2 / den)
    own = jnp.where(row == i1, w1, 0.0) + jnp.where(row == i2, w2, 0.0)
    blocks = [jnp.where(row == 0, g_idx.astype(F32), 0.0)]
    blocks += [jnp.where(g_idx == g, own, 0.0) for g in range(n_groups)]
    blocks += [jnp.zeros((8, n_tok), F32)] * (ROUTER_LANES // 8 - len(blocks))
    return jnp.concatenate(blocks, axis=0), g_idx


def _mixer_prompt_kernel(x_ref, lbw_ref, g1_ref, win_ref, hgg_ref, cw_ref, cb_ref, wx_ref, bx_ref,
                         wa_ref, ba_ref, lam_ref, wout_ref, g2_ref, wr_ref, br_ref,
                         x1_ref, xn2_ref, gates_ref, cnt_ref, sout_ref, hout_ref, cout_ref,
                         proj_s, k_s, b_s, o_s, st_s, xr_s, a_s, u_s, hcar_s, yb_out_s,
                         *, n_groups, per_group):
    lb_t = x_ref.shape[1]
    wa_w = o_s.shape[1]
    wb_w = a_s.shape[1]
    n_heads = wa_w // HEAD_DIM
    j = pl.program_id(1)
    nj = pl.num_programs(1)

    @pl.when(j == 0)
    def _():
        st_s[...] = jnp.zeros_like(st_s)
        hcar_s[...] = jnp.zeros_like(hcar_s)
        xr_s[0:8, :] = jnp.zeros((8, wb_w), F32)

    x = x_ref[0]
    xn = _rms(x, g1_ref[...]).astype(BF16)
    xb0 = 4 * wa_w

    def project(pieces):
        for c, w in pieces:
            proj_s[:, c:c + w] = jnp.dot(xn, win_ref[:, c:c + w], preferred_element_type=F32)

    project(((wa_w, wa_w), (xb0, wb_w), (xb0 + wb_w, wb_w), (0, wa_w), (2 * wa_w, wa_w), (3 * wa_w, wa_w)))

    xr_s[pl.ds(8, lb_t), :] = proj_s[:, xb0:xb0 + wb_w]
    xc = (cb_ref[...] + cw_ref[3:4, :] * xr_s[pl.ds(8, lb_t), :] + cw_ref[2:3, :] * xr_s[pl.ds(7, lb_t), :]
          + cw_ref[1:2, :] * xr_s[pl.ds(6, lb_t), :] + cw_ref[0:1, :] * xr_s[pl.ds(5, lb_t), :])
    tail = xr_s[pl.ds(lb_t + 5, 3), :]
    xr_s[5:8, :] = tail
    cout_ref[0] = tail
    xcb = xc.astype(BF16)
    gate_x = _sigmoid(jnp.dot(xcb, wx_ref[...], preferred_element_type=F32) + bx_ref[...])
    gate_a = _sigmoid(jnp.dot(xcb, wa_ref[...], preferred_element_type=F32) + ba_ref[...])
    log_a =(-LRU_C) * gate_a * _softplus(-lam_ref[...])
    a = jnp.exp(log_a)
    mult = jnp.sqrt((1.0 - a) * (1.0 + a))
    first =(lax.broadcasted_iota(jnp.int32, (lb_t, 1), 0) == 0) & (j == 0)
    a = jnp.where(first, 0.0, a)
    mult = jnp.where(first, 1.0, mult)
    a_s[...] = a
    u_s[...] = gate_x * xc * mult

    lb = _forget_lower_bound(lbw_ref[...])
    f = lb + (1.0 - lb) * _sigmoid(proj_s[:, wa_w:2 * wa_w])
    for h in range(n_heads):
        k_s[h] = 1.0 - f[:, h * HEAD_DIM:(h + 1) * HEAD_DIM]
    logf = jnp.log(f)
    r_i = lax.broadcasted_iota(jnp.int32, (CHUNK, CHUNK), 0)
    c_i = lax.broadcasted_iota(jnp.int32, (CHUNK, CHUNK), 1)
    tri = jnp.where(r_i >= c_i, 1.0, 0.0).astype(BF16)
    lf_hi = logf.astype(BF16)
    rest = logf - lf_hi.astype(F32)
    lf_mid = rest.astype(BF16)
    lf_lo = (rest - lf_mid.astype(F32)).astype(BF16)
    for c in range(0, lb_t, CHUNK):
        cum = [jnp.dot(tri, part[c:c + CHUNK, :], preferred_element_type=F32) for part in (lf_lo, lf_mid, lf_hi)]
        b_all = LOG2E * ((cum[0] + cum[1]) + cum[2])
        for h in range(n_heads):
            b_s[h, c:c + CHUNK, :] = b_all[:, h * HEAD_DIM:(h + 1) * HEAD_DIM]

    row_sub = lax.broadcasted_iota(jnp.int32, (SUB, HEAD_DIM), 0)
    lane_sub = lax.broadcasted_iota(jnp.int32, (SUB, HEAD_DIM), 1)
    assert n_heads % 2 == 0
    r_kk = lax.broadcasted_iota(jnp.int32, (2 * HEAD_DIM, 2 * HEAD_DIM), 0)
    c_kk = lax.broadcasted_iota(jnp.int32, (2 * HEAD_DIM, 2 * HEAD_DIM), 1)
    ones_kk = jnp.where((r_kk < HEAD_DIM) == (c_kk < HEAD_DIM), 1.0, 0.0).astype(BF16)

    n_sub = CHUNK // SUB
    half = SUB // 2
    lower_left = (row_sub >= half) & (lane_sub < half)
    own_lane = jnp.where(row_sub >= half, half, 0)

    def chunk_start(r0):
        first = []
        for h in range(n_heads):
            hs = slice(h * HEAD_DIM, (h + 1) * HEAD_DIM)
            q = proj_s[pl.ds(r0, CHUNK), hs]
            b = b_s[h, pl.ds(r0, CHUNK), :]
            v = proj_s[pl.ds(r0, CHUNK), 2 * wa_w + h * HEAD_DIM:2 * wa_w + (h + 1) * HEAD_DIM]
            k = k_s[h, pl.ds(r0, CHUNK), :]

            def key_rows(ref, lo, j):
                return jnp.stack([jnp.broadcast_to(ref[h, pl.ds(r0 + lo + hf * half + j, 1), :], (half, HEAD_DIM))
                                  for hf in range(2)])
            vb = v.astype(BF16)
            st = st_s[h]
            b_last = b[CHUNK - 1:CHUNK, :]
            o = lax.dot_general((q * jnp.exp2(b)).astype(BF16), st.astype(BF16), NT_DIMS,
                                preferred_element_type=F32)
            k_end = k * jnp.exp2(b_last - b)
            st_s[h] = st * jnp.exp2(b_last) + lax.dot_general(vb, k_end.astype(BF16), TN_DIMS,
                                                               preferred_element_type=F32)
            terms, off, mid = [], [], []
            for i in range(n_sub):
                lo = i * SUB
                qi, bi, ki = q[lo:lo + SUB], b[lo:lo + SUB], k[lo:lo + SUB]
                q3, b3 = (a.reshape(2, half, HEAD_DIM) for a in (qi, bi))
                terms += [(q3 * (key_rows(k_s, lo, j) * jnp.exp2(b3 - key_rows(b_s, lo, j))))
                          .reshape(SUB, HEAD_DIM).astype(BF16) for j in range(half)]
                rm = bi[half - 1:half]
                mid.append(lax.dot_general((qi * jnp.exp2(bi - rm)).astype(BF16),
                                           (ki * jnp.exp2(rm - bi)).astype(BF16), NT_DIMS,
                                           preferred_element_type=F32))
                if i > 0:
                    r = b[lo - 1:lo]
                    qt = (qi * jnp.exp2(bi - r)).astype(BF16)
                    kt = (k[:lo] * jnp.exp2(r - b[:lo])).astype(BF16)
                    off.append(lax.dot_general(qt, kt, NT_DIMS, preferred_element_type=F32))
            first.append([o, vb, off, jnp.concatenate(terms, axis=0), mid])
        for h in range(0, n_heads, 2):
            both = jnp.concatenate([first[h][3], first[h + 1][3]], axis=1)
            sums = jnp.dot(both, ones_kk, preferred_element_type=F32)
            first[h][3] = sums[:, :HEAD_DIM]
            first[h + 1][3] = sums[:, HEAD_DIM:]
        return first

    def chunk_finish(r0, first):
        for h in range(n_heads):
            o, vb, off, sums, mid = first[h]
            outs = []
            for i in range(n_sub):
                lo = i * SUB
                sc = jnp.zeros((SUB, HEAD_DIM), F32)
                for j in range(half):
                    row0 = (i * half + j) * SUB
                    sc = jnp.where(lane_sub == own_lane + j, sums[row0:row0 + SUB], sc)
                sc = jnp.where(row_sub >= lane_sub, sc, 0.0)[:, :SUB]
                sc = jnp.where(lower_left[:, :SUB], mid[i], sc)
                od = o[lo:lo + SUB] + jnp.dot(sc.astype(BF16), vb[lo:lo + SUB], preferred_element_type=F32)
                if i > 0:
                    od = od + jnp.dot(off[i - 1].astype(BF16), vb[:lo], preferred_element_type=F32)
                outs.append(od)
            o_s[pl.ds(r0, CHUNK), h * HEAD_DIM:(h + 1) * HEAD_DIM] = jnp.concatenate(outs, axis=0)

    def chunks_body(ci, carry):
        rows = [pl.multiple_of((ci * UNROLL + u) * CHUNK, CHUNK) for u in range(UNROLL)]
        started = [chunk_start(r0) for r0 in rows]
        for r0, first in zip(rows, started):
            chunk_finish(r0, first)
        return carry

    row8 = lax.broadcasted_iota(jnp.int32, (8, 1), 0)

    def scan_body(gi, carry):
        r0 = pl.multiple_of(gi * 8, 8)
        aa = a_s[pl.ds(r0, 8), :]
        uu = u_s[pl.ds(r0, 8), :]
        for s in (1, 2, 4):
            m = row8 >= s
            uu = jnp.where(m, aa * pltpu.roll(uu, s, 0) + uu, uu)
            aa = jnp.where(m, aa * pltpu.roll(aa, s, 0), aa)
        hh = aa * carry + uu
        u_s[pl.ds(r0, 8), :] = hh
        return hh[7:8, :]

    h_last = lax.fori_loop(0, lb_t // 8, scan_body, hcar_s[...])
    hcar_s[...] = h_last
    hout_ref[0] = h_last
    yb = (u_s[...] * _gelu_tanh(proj_s[:, xb0 + wb_w:xb0 + 2 * wb_w])).astype(BF16)
    yb_out_s[...] = jnp.dot(yb, wout_ref[wa_w:, :], preferred_element_type=F32)

    assert lb_t % (CHUNK * UNROLL) == 0
    lax.fori_loop(0, lb_t // (CHUNK * UNROLL), chunks_body, 0)

    ya = []
    for h in range(n_heads):
        hs = slice(h * HEAD_DIM, (h + 1) * HEAD_DIM)
        oh = o_s[:, hs]
        oh = oh * lax.rsqrt(jnp.mean(oh * oh, axis=-1, keepdims=True) + EPS) * hgg_ref[:, hs]
        ya.append(oh * _silu(proj_s[:, 3 * wa_w + h * HEAD_DIM:3 * wa_w + (h + 1) * HEAD_DIM]))

    ya = jnp.concatenate(ya, axis=-1).astype(BF16)
    x1 = x + (jnp.dot(ya, wout_ref[:wa_w, :], preferred_element_type=F32) + yb_out_s[...])
    x1_ref[0] = x1
    xn2 = _rms(x1, g2_ref[...]).astype(BF16)
    xn2_ref[0] = xn2
    logits_t = lax.dot_general(wr_ref[...], xn2, NT_DIMS, preferred_element_type=F32) + br_ref[...]
    gates_t, g_idx = _route_rows(logits_t, n_groups, per_group)
    gates_ref[0] = gates_t.T
    lane_c = lax.broadcasted_iota(jnp.int32, (1, ROUTER_LANES), 1)
    for t in range(lb_t // TOK_TILE):
        gi = g_idx[:, t * TOK_TILE:(t + 1) * TOK_TILE]
        cnt = jnp.zeros((1, ROUTER_LANES), jnp.int32)
        for g in range(n_groups):
            cnt = cnt + jnp.where(lane_c == g, jnp.sum(jnp.where(gi == g, 1, 0), axis=1, keepdims=True), 0)
        cnt_ref[0, t:t + 1, :] = cnt

    @pl.when(j == nj - 1)
    def _():
        for h in range(n_heads):
            sout_ref[0, h] = st_s[h].T


def _const_spec(shape):
    nd = len(shape)
    return pl.BlockSpec(shape, lambda *_: (0,) * nd)


def _mixer_prompt(x, p, lb_t):
    bsz, seq, d = x.shape
    wa_w = p["hgg"].shape[1]
    wb_w = p["cb"].shape[1]
    n_heads = wa_w // HEAD_DIM
    n_cols = p["win_bf"].shape[1]
    weights = [p["lbw"], p["g1"], p["win_bf"], p["hgg"], p["cw"], p["cb"], p["wx_bf"], p["bx"],
               p["wa_bf"], p["ba"], p["lam"], p["wout_bf"], p["g2"], p["wr_t_bf"], p["br_t"]]
    nj = seq // lb_t
    tile = lambda w: pl.BlockSpec((1, lb_t, w), lambda b, j: (b, j, 0))
    out_shape = (
        jax.ShapeDtypeStruct((bsz, seq, d), F32),
        jax.ShapeDtypeStruct((bsz, seq, d), BF16),
        jax.ShapeDtypeStruct((bsz, seq, ROUTER_LANES), F32),
        jax.ShapeDtypeStruct((bsz * (seq // lb_t), lb_t // TOK_TILE, ROUTER_LANES), jnp.int32),
        jax.ShapeDtypeStruct((bsz, n_heads, HEAD_DIM, HEAD_DIM), F32),
        jax.ShapeDtypeStruct((bsz, 1, wb_w), F32),
        jax.ShapeDtypeStruct((bsz, 3, wb_w), F32),
    )
    out_specs = (
        tile(d), tile(d), tile(ROUTER_LANES),
        pl.BlockSpec((1, lb_t // TOK_TILE, ROUTER_LANES), lambda b, j: (b * nj + j, 0, 0)),
        pl.BlockSpec((1, n_heads, HEAD_DIM, HEAD_DIM), lambda b, j: (b, 0, 0, 0)),
        pl.BlockSpec((1, 1, wb_w), lambda b, j: (b, 0, 0)),
        pl.BlockSpec((1, 3, wb_w), lambda b, j: (b, 0, 0)),
    )
    scratch = [
        pltpu.VMEM((lb_t, n_cols), F32),
        pltpu.VMEM((n_heads, lb_t, HEAD_DIM), F32),
        pltpu.VMEM((n_heads, lb_t, HEAD_DIM), F32),
        pltpu.VMEM((lb_t, wa_w), F32),
        pltpu.VMEM((n_heads, HEAD_DIM, HEAD_DIM), F32),
        pltpu.VMEM((lb_t + 8, wb_w), F32),
        pltpu.VMEM((lb_t, wb_w), F32),
        pltpu.VMEM((lb_t, wb_w), F32),
        pltpu.VMEM((1, wb_w), F32),
        pltpu.VMEM((lb_t, d), F32),
    ]
    kern = functools.partial(_mixer_prompt_kernel, n_groups=p["n_groups"], per_group=p["per_group"])
    return pl.pallas_call(
        kern,
        grid=(bsz, nj),
        in_specs=[tile(d)] + [_const_spec(w.shape) for w in weights],
        out_specs=out_specs,
        out_shape=out_shape,
        scratch_shapes=scratch,
        compiler_params=pltpu.CompilerParams(dimension_semantics=("arbitrary", "arbitrary"),
                                             vmem_limit_bytes=VMEM_LIMIT_BYTES),
        name="mixer_prompt",
    )(x, *weights)


def _moe_kernel(x1_ref, xn2_ref, gates_ref, w1_ref, w3_ref, w2_ref, gf_ref, y_ref, acc_s, *, n_groups):
    e = pl.program_id(1)

    @pl.when(e == 0)
    def _():
        acc_s[...] = jnp.zeros_like(acc_s)

    xb = xn2_ref[...]
    per_step = w1_ref.shape[0]
    gates = pltpu.roll(gates_ref[...], ROUTER_LANES - n_groups - per_step * e, 1)
    acc = acc_s[...]
    for s in range(per_step):
        h = (_silu(jnp.dot(xb, w1_ref[s], preferred_element_type=F32))
             * jnp.dot(xb, w3_ref[s], preferred_element_type=F32) * gates[:, s:s + 1])
        acc = acc + jnp.dot(h.astype(BF16), w2_ref[s].astype(BF16), preferred_element_type=F32)
    acc_s[...] = acc

    @pl.when(e == pl.num_programs(1) - 1)
    def _():
        y_ref[...] = _rms(x1_ref[...] + acc_s[...], gf_ref[...])


def _moe_dense(x1, xn2, gates, p, tm):
    t, d = x1.shape
    n_exp, _, d_exp = p["w1_bf"].shape
    per_step = p["per_group"]
    row = lambda w: pl.BlockSpec((tm, w), lambda i, e: (i, 0))
    return pl.pallas_call(
        functools.partial(_moe_kernel, n_groups=p["n_groups"]),
        grid=(t // tm, n_exp // per_step),
        in_specs=[row(d), row(d), row(ROUTER_LANES),
                  pl.BlockSpec((per_step, d, d_exp), lambda i, e: (e, 0, 0)),
                  pl.BlockSpec((per_step, d, d_exp), lambda i, e: (e, 0, 0)),
                  pl.BlockSpec((per_step, d_exp, d), lambda i, e: (e, 0, 0)),
                  _const_spec(p["gf"].shape)],
        out_specs=row(d),
        out_shape=jax.ShapeDtypeStruct((t, d), F32),
        scratch_shapes=[pltpu.VMEM((tm, d), F32)],
        compiler_params=pltpu.CompilerParams(dimension_semantics=("arbitrary", "arbitrary"),
                                             vmem_limit_bytes=VMEM_LIMIT_BYTES),
        name="moe_dense",
    )(x1, xn2, gates, p["w1_bf"], p["w3_bf"], p["w2"], p["gf"])


def _seg_pad(n):
    return jnp.bitwise_and(n + (SEG - 1), -SEG)


def _row_tile_pad(n):
    return jnp.bitwise_and(n + (ROW_TILE - 1), -ROW_TILE)


def _tile_segments(cnt_ref, tile, n_groups):
    padded = [_seg_pad(cnt_ref[tile * n_groups + g]) for g in range(n_groups)]
    starts, acc = [], jnp.int32(0)
    for g in range(n_groups):
        starts.append(acc)
        acc = acc + padded[g]
    return padded, starts, acc


def _group_bases(cnt_ref, n_tiles, n_groups):
    def body(t, tot):
        return tuple(tot[g] + _seg_pad(cnt_ref[t * n_groups + g]) for g in range(n_groups))
    totals = lax.fori_loop(0, n_tiles, body, (jnp.int32(0),) * n_groups)
    bases, ends, acc = [], [], jnp.int32(0)
    for g in range(n_groups):
        bases.append(acc)
        acc = acc + _row_tile_pad(totals[g])
        ends.append(acc)
    return bases, ends, totals


def _chunk_rows(n_chunks, starts, gstart):
    rows, row = [], None
    for c in range(n_chunks):
        row = gstart[0] if c == 0 else row + SEG
        for g in range(1, len(starts)):
            row = jnp.where(starts[g] == c * SEG, gstart[g], row)
        rows.append(pl.multiple_of(row, SEG))
    return rows


def _sort_matrix(gates, starts):
    n_groups = len(starts)
    col = lax.broadcasted_iota(jnp.int32, (TOK_TILE, ROUTER_LANES), 1)
    gi = gates[:, 0:1].astype(jnp.int32)
    onehot = col == gi
    r_i = lax.broadcasted_iota(jnp.int32, (TOK_TILE, TOK_TILE), 0)
    c_i = lax.broadcasted_iota(jnp.int32, (TOK_TILE, TOK_TILE), 1)
    earlier = jnp.where(r_i > c_i, 1.0, 0.0).astype(BF16)
    before = jnp.dot(earlier, jnp.where(onehot, 1.0, 0.0).astype(BF16), preferred_element_type=F32)
    rank = jnp.sum(jnp.where(onehot, before, 0.0), axis=-1, keepdims=True).astype(jnp.int32)
    base = jnp.zeros((TOK_TILE, 1), jnp.int32)
    for g in range(n_groups):
        base = base + jnp.where(gi == g, starts[g], 0)
    lane = lax.broadcasted_iota(jnp.int32, (TOK_TILE, SORT_ROWS), 1)
    return lane == base + rank


def _sort_matrix_rows(gates, starts):
    n_groups = len(starts)
    col = lax.broadcasted_iota(jnp.int32, (TOK_TILE, ROUTER_LANES), 1)
    onehot_t = jnp.where(col == gates[:, 0:1].astype(jnp.int32), 1.0, 0.0).T
    r_i = lax.broadcasted_iota(jnp.int32, (TOK_TILE, TOK_TILE), 0)
    c_i = lax.broadcasted_iota(jnp.int32, (TOK_TILE, TOK_TILE), 1)
    later = jnp.where(r_i < c_i, 1.0, 0.0).astype(BF16)
    before = jnp.dot(onehot_t.astype(BF16), later, preferred_element_type=F32)
    g_row = lax.broadcasted_iota(jnp.int32, (ROUTER_LANES, 1), 0)
    base = jnp.zeros((ROUTER_LANES, 1), jnp.int32)
    for g in range(n_groups):
        base = base + jnp.where(g_row == g, starts[g], 0)
    dest = jnp.sum(jnp.where(onehot_t > 0.0, before + base.astype(F32), 0.0), axis=0, keepdims=True)
    row = lax.broadcasted_iota(jnp.int32, (SORT_ROWS, TOK_TILE), 0)
    return row == dest.astype(jnp.int32)


def _dispatch_copies(xbuf, gbuf, xs_hbm, gs_hbm, sem, slot, c, row):
    return (pltpu.make_async_copy(xbuf.at[slot, pl.ds(c * SEG, SEG), :], xs_hbm.at[pl.ds(row, SEG), :], sem.at[slot]),
            pltpu.make_async_copy(gbuf.at[slot, pl.ds(c * SEG, SEG), :], gs_hbm.at[pl.ds(row, SEG), :], sem.at[slot]))


def _moe_dispatch_kernel(cnt_ref, xn2_ref, gates_ref, xs_hbm, gs_hbm, tg_ref,
                         xbuf, gbuf, sem, gstart_s, zx, zg, zsem, *, n_groups, per_group):
    j = pl.program_id(0)
    n_tiles = pl.num_programs(0)
    n_chunks = (TOK_TILE + n_groups * SEG) // SEG
    slot = lax.rem(j, 2)

    @pl.when(j == 0)
    def _():
        bases, ends, totals = _group_bases(cnt_ref, n_tiles, n_groups)
        for g in range(n_groups):
            gstart_s[g] = bases[g]
        n_rt = tg_ref.shape[0] - 1
        for i in range(n_rt):
            tg = jnp.int32(0)
            for g in range(n_groups):
                tg = tg + jnp.where(ends[g] <= i * ROW_TILE, 1, 0)
            tg_ref[i] = tg
        n_used = ends[n_groups - 1] // ROW_TILE
        tg_ref[n_rt] = n_used

        zx[...] = jnp.zeros_like(zx)
        zg[...] = jnp.zeros_like(zg)

        def zero_copies(row, n):
            return (pltpu.make_async_copy(zx.at[pl.ds(0, n), :], xs_hbm.at[pl.ds(row, n), :], zsem.at[0]),
                    pltpu.make_async_copy(zg.at[pl.ds(0, n), :], gs_hbm.at[pl.ds(row, n), :], zsem.at[0]))

        def for_each_gap(act):
            for g in range(n_groups):
                tail = bases[g] + totals[g]

                def seg_body(k, c, tail=tail):
                    for cp in zero_copies(pl.multiple_of(tail + k * SEG, SEG), SEG):
                        act(cp)
                    return c
                lax.fori_loop(0, (ends[g] - tail) // SEG, seg_body, 0)

            def tile_body(k, c):
                for cp in zero_copies(pl.multiple_of((n_used + k) * ROW_TILE, ROW_TILE), ROW_TILE):
                    act(cp)
                return c
            lax.fori_loop(0, n_rt - n_used, tile_body, 0)

        for_each_gap(lambda cp: cp.start())
        for_each_gap(lambda cp: cp.wait())

    def wait_tile(tile, slot_):
        _, _, used = _tile_segments(cnt_ref, tile, n_groups)
        for c in range(n_chunks):
            @pl.when(c * SEG < used)
            def _():
                for cp in _dispatch_copies(xbuf, gbuf, xs_hbm, gs_hbm, sem, slot_, c, 0):
                    cp.wait()

    @pl.when(j >= 2)
    def _():
        wait_tile(j - 2, slot)

    padded, starts, used = _tile_segments(cnt_ref, j, n_groups)
    gates = gates_ref[...]
    d = xn2_ref.shape[1]
    g_hi = gates.astype(BF16)
    g_lo = (gates - g_hi.astype(F32)).astype(BF16)
    sort_m = jnp.where(_sort_matrix_rows(gates, starts), 1.0, 0.0).astype(BF16)
    moved = jnp.dot(sort_m, jnp.concatenate([xn2_ref[...], g_hi, g_lo], axis=1), preferred_element_type=F32)
    xbuf[slot] = moved[:, :d].astype(BF16)
    gbuf[slot] = moved[:, d:d + ROUTER_LANES] + moved[:, d + ROUTER_LANES:]
    gstart = [gstart_s[g] for g in range(n_groups)]
    rows = _chunk_rows(n_chunks, starts, gstart)
    for c in range(n_chunks):
        @pl.when(c * SEG < used)
        def _():
            for cp in _dispatch_copies(xbuf, gbuf, xs_hbm, gs_hbm, sem, slot, c, rows[c]):
                cp.start()
    for g in range(n_groups):
        gstart_s[g] = gstart[g] + padded[g]

    @pl.when(j == n_tiles - 1)
    def _():
        @pl.when(j >= 1)
        def _():
            wait_tile(j - 1, 1 - slot)
        wait_tile(j, slot)


def _moe_group_kernel(tg_ref, xs_ref, gs_ref, w1_ref, w3_ref, w2_ref, ys_ref, w2_s, *, n_groups, per_group):
    i = pl.program_id(0)

    @pl.when((i == 0) | (tg_ref[i] != tg_ref[jnp.maximum(i - 1, 0)]))
    def _():
        w2_s[...] = w2_ref[...].astype(BF16)

    @pl.when(tg_ref[i] < n_groups)
    def _():
        xb = xs_ref[...]
        gates = pltpu.roll(gs_ref[...], ROUTER_LANES - PROMPT_EXPERT_LANE - per_group * tg_ref[i], 1)
        acc = jnp.zeros(ys_ref.shape, F32)
        for s in range(per_group):
            h = (_silu(jnp.dot(xb, w1_ref[s], preferred_element_type=F32))
                 * jnp.dot(xb, w3_ref[s], preferred_element_type=F32) * gates[:, s:s + 1])
            acc = acc + jnp.dot(h.astype(BF16), w2_s[s], preferred_element_type=F32)
        ys_ref[...] = acc.astype(BF16)


def _moe_combine_kernel(cnt_ref, gates_ref, x1_ref, gf_ref, ys_hbm, y_ref, ybuf, sem, gstart_s, *, n_groups):
    j = pl.program_id(0)
    n_tiles = pl.num_programs(0)
    n_chunks = SORT_ROWS // SEG
    slot = lax.rem(j, 2)

    def copies(slot_, c, row):
        return pltpu.make_async_copy(ys_hbm.at[pl.ds(row, SEG), :], ybuf.at[slot_, pl.ds(c * SEG, SEG), :],
                                     sem.at[slot_])

    def fetch_tile(tile, slot_):
        padded, starts, _ = _tile_segments(cnt_ref, tile, n_groups)
        gstart = [gstart_s[g] for g in range(n_groups)]
        for c, row in enumerate(_chunk_rows(n_chunks, starts, gstart)):
            copies(slot_, c, row).start()
        for g in range(n_groups):
            gstart_s[g] = gstart[g] + padded[g]

    @pl.when(j == 0)
    def _():
        bases, _, _ = _group_bases(cnt_ref, n_tiles, n_groups)
        for g in range(n_groups):
            gstart_s[g] = bases[g]
        fetch_tile(0, 0)

    @pl.when(j + 1 < n_tiles)
    def _():
        fetch_tile(j + 1, 1 - slot)

    for c in range(n_chunks):
        copies(slot, c, 0).wait()
    _, starts, _ = _tile_segments(cnt_ref, j, n_groups)
    sort_t = _sort_matrix(gates_ref[...], starts)
    moe = jnp.dot(jnp.where(sort_t, 1.0, 0.0).astype(BF16), ybuf[slot], preferred_element_type=F32)
    y_ref[...] = _rms(x1_ref[...] + moe, gf_ref[...])


def _moe_sorted(x1, xn2, gates, counts, p):
    t, d = x1.shape
    n_groups, per_group = p["n_groups"], p["per_group"]
    assert n_groups <= MAX_GROUPS and t % TOK_TILE == 0
    n_tiles = t // TOK_TILE
    d_exp = p["w1_bf"].shape[2]
    cap = t + n_tiles * n_groups * SEG + n_groups * ROW_TILE + SORT_ROWS
    n_rt = -(-cap // ROW_TILE)
    rows = n_rt * ROW_TILE
    cnt = counts[:, :, :n_groups].reshape(-1)
    params = pltpu.CompilerParams(dimension_semantics=("arbitrary",), vmem_limit_bytes=VMEM_LIMIT_BYTES)
    any_spec = pl.BlockSpec(memory_space=pl.ANY)
    tok = lambda w: pl.BlockSpec((TOK_TILE, w), lambda j, c: (j, 0))

    xs, gs, tile_group = pl.pallas_call(
        functools.partial(_moe_dispatch_kernel, n_groups=n_groups, per_group=per_group),
        grid_spec=pltpu.PrefetchScalarGridSpec(
            num_scalar_prefetch=1, grid=(n_tiles,),
            in_specs=[tok(d), tok(ROUTER_LANES)],
            out_specs=(any_spec, any_spec, pl.BlockSpec(memory_space=pltpu.SMEM)),
            scratch_shapes=[pltpu.VMEM((2, SORT_ROWS, d), BF16), pltpu.VMEM((2, SORT_ROWS, ROUTER_LANES), F32),
                            pltpu.SemaphoreType.DMA((2,)), pltpu.SMEM((n_groups,), jnp.int32),
                            pltpu.VMEM((ROW_TILE, d), BF16), pltpu.VMEM((ROW_TILE, ROUTER_LANES), F32),
                            pltpu.SemaphoreType.DMA((1,))]),
        out_shape=(jax.ShapeDtypeStruct((rows, d), BF16), jax.ShapeDtypeStruct((rows, ROUTER_LANES), F32),
                   jax.ShapeDtypeStruct((n_rt + 1,), jnp.int32)),
        compiler_params=params, name="moe_dispatch",
    )(cnt, xn2, gates)

    used_tile = lambda i, tg: (jnp.minimum(i, tg[n_rt] - 1), 0)
    group_w = lambda i, tg: (jnp.minimum(tg[i], n_groups - 1), 0, 0)
    ys = pl.pallas_call(
        functools.partial(_moe_group_kernel, n_groups=n_groups, per_group=per_group),
        grid_spec=pltpu.PrefetchScalarGridSpec(
            num_scalar_prefetch=1, grid=(n_rt,),
            in_specs=[pl.BlockSpec((ROW_TILE, d), used_tile), pl.BlockSpec((ROW_TILE, ROUTER_LANES), used_tile),
                      pl.BlockSpec((per_group, d, d_exp), group_w), pl.BlockSpec((per_group, d, d_exp), group_w),
                      pl.BlockSpec((per_group, d_exp, d), group_w)],
            out_specs=pl.BlockSpec((ROW_TILE, d), used_tile),
            scratch_shapes=[pltpu.VMEM((per_group, d_exp, d), BF16)]),
        out_shape=jax.ShapeDtypeStruct((rows, d), BF16),
        input_output_aliases={1: 0},
        compiler_params=params, name="moe_experts",
    )(tile_group, xs, gs, p["w1_bf"], p["w3_bf"], p["w2"])

    return pl.pallas_call(
        functools.partial(_moe_combine_kernel, n_groups=n_groups),
        grid_spec=pltpu.PrefetchScalarGridSpec(
            num_scalar_prefetch=1, grid=(n_tiles,),
            in_specs=[tok(ROUTER_LANES), tok(d), pl.BlockSpec(p["gf"].shape, lambda j, c: (0, 0)), any_spec],
            out_specs=tok(d),
            scratch_shapes=[pltpu.VMEM((2, SORT_ROWS, d), BF16), pltpu.SemaphoreType.DMA((2,)),
                            pltpu.SMEM((n_groups,), jnp.int32)]),
        out_shape=jax.ShapeDtypeStruct((t, d), F32),
        compiler_params=params, name="moe_combine",
    )(cnt, gates, x1, p["gf"], ys)


def _sample_in_kernel(x_ref, c0_ref, c1_ref, c2_ref, h0_ref, lbw_ref, g1_ref, win_ref, cw_ref, cb_ref,
                      wx_ref, bx_ref, wa_ref, ba_ref, lam_ref,
                      q_ref, f_ref, v_ref, g_ref, yb_ref, hnew_ref, xr_ref):
    wa_w = v_ref.shape[1]
    wb_w = yb_ref.shape[1]
    xn = _rms(x_ref[...], g1_ref[...])
    proj = jnp.dot(xn, win_ref[...], precision=HIGHEST, preferred_element_type=F32)
    lb = _forget_lower_bound(lbw_ref[...])
    f = lb + (1.0 - lb) * _sigmoid(proj[:, wa_w:2 * wa_w])
    q_ref[...] = proj[:, 0:wa_w].T
    f_ref[...] = f.T
    v_ref[...] = proj[:, 2 * wa_w:3 * wa_w]
    g_ref[...] = proj[:, 3 * wa_w:4 * wa_w]
    xr = proj[:, 4 * wa_w:4 * wa_w + wb_w]
    xr_ref[...] = xr
    xc = (cb_ref[...] + cw_ref[0:1, :] * c0_ref[...] + cw_ref[1:2, :] * c1_ref[...]
          + cw_ref[2:3, :] * c2_ref[...] + cw_ref[3:4, :] * xr)
    gate_x = _sigmoid(jnp.dot(xc, wx_ref[...], precision=HIGHEST, preferred_element_type=F32) + bx_ref[...])
    gate_a = _sigmoid(jnp.dot(xc, wa_ref[...], precision=HIGHEST, preferred_element_type=F32) + ba_ref[...])
    log_a = (-LRU_C) * gate_a * _softplus(-lam_ref[...])
    a = jnp.exp(log_a)
    mult = jnp.sqrt(-_expm1(2.0 * log_a))
    h = a * h0_ref[...] + gate_x * xc * mult
    hnew_ref[...] = h
    yb_ref[...] = h * _gelu_tanh(proj[:, 4 * wa_w + wb_w:4 * wa_w + 2 * wb_w])


def _sample_state_kernel(s_ref, qt_ref, ft_ref, v_ref, snew_ref, o_ref):
    tb, n_heads = s_ref.shape[0], s_ref.shape[1]
    n_tok = qt_ref.shape[1]
    shift = lax.rem(n_tok - pl.program_id(0) * tb, n_tok)
    sq = (HEAD_DIM, HEAD_DIM)
    for h in range(n_heads):
        hs = slice(h * HEAD_DIM, (h + 1) * HEAD_DIM)
        qh = pltpu.roll(qt_ref[hs, :], shift, 1)
        fh = pltpu.roll(ft_ref[hs, :], shift, 1)
        rows = []
        for t in range(tb):
            f_all = jnp.broadcast_to(fh[:, t:t + 1], sq)
            s_new = f_all * s_ref[t, h] + (1.0 - f_all) * v_ref[t:t + 1, hs]
            snew_ref[t, h] = s_new
            rows.append(jnp.sum(jnp.broadcast_to(qh[:, t:t + 1], sq) * s_new, axis=0, keepdims=True))
        o_ref[:, hs] = jnp.concatenate(rows, axis=0)


def _sample_out_kernel(x_ref, o_ref, g_ref, yb_ref, hgg_ref, wout_ref, g2_ref, wr_ref, br_ref,
                       x1_ref, xn2_ref, gates_ref, *, n_groups, per_group):
    wa_w = o_ref.shape[1]
    ya = []
    for h in range(wa_w // HEAD_DIM):
        hs = slice(h * HEAD_DIM, (h + 1) * HEAD_DIM)
        oh = o_ref[:, hs]
        oh = oh * lax.rsqrt(jnp.mean(oh * oh, axis=-1, keepdims=True) + EPS) * hgg_ref[:, hs]
        ya.append(oh * _silu(g_ref[:, hs]))
    y = jnp.concatenate(ya + [yb_ref[...]], axis=-1)
    x1 = x_ref[...] + jnp.dot(y, wout_ref[...], precision=HIGHEST, preferred_element_type=F32)
    x1_ref[...] = x1
    xn2 = _rms(x1, g2_ref[...])
    xn2_ref[...] = xn2.astype(BF16)
    logits = jnp.dot(xn2, wr_ref[...], precision=HIGHEST, preferred_element_type=F32) + br_ref[...]
    gates_ref[...] = _route(logits, n_groups, per_group)


def _whole(kernel, out_shape, *args, name):
    return pl.pallas_call(
        kernel, out_shape=out_shape,
        compiler_params=pltpu.CompilerParams(vmem_limit_bytes=VMEM_LIMIT_BYTES), name=name)(*args)


def _mixer_sample(x, s0, h0, c0, p, tb):
    n, d = x.shape
    wa_w = p["hgg"].shape[1]
    wb_w = p["cb"].shape[1]
    n_heads = wa_w // HEAD_DIM
    sd = lambda w: jax.ShapeDtypeStruct((n, w), F32)
    key_major = jax.ShapeDtypeStruct((wa_w, n), F32)
    q, f, v, g, yb, h_new, xr = _whole(
        _sample_in_kernel, (key_major,) * 2 + (sd(wa_w),) * 2 + (sd(wb_w),) * 3,
        x, c0[:, 0, :], c0[:, 1, :], c0[:, 2, :], h0, p["lbw"], p["g1"], p["win"], p["cw"], p["cb"],
        p["wx"], p["bx"], p["wa"], p["ba"], p["lam"], name="sample_in")
    tok = lambda w: pl.BlockSpec((tb, w), lambda i: (i, 0))
    st = pl.BlockSpec((tb, n_heads, HEAD_DIM, HEAD_DIM), lambda i: (i, 0, 0, 0))
    s_new, o = pl.pallas_call(
        _sample_state_kernel,
        grid=(n // tb,),
        in_specs=[st, _const_spec((wa_w, n)), _const_spec((wa_w, n)), tok(wa_w)],
        out_specs=(st, tok(wa_w)),
        out_shape=(jax.ShapeDtypeStruct(s0.shape, F32), sd(wa_w)),
        compiler_params=pltpu.CompilerParams(dimension_semantics=("arbitrary",),
                                             vmem_limit_bytes=VMEM_LIMIT_BYTES),
        name="sample_state",
    )(s0, q, f, v)
    x1, xn2, gates = _whole(
        functools.partial(_sample_out_kernel, n_groups=p["n_groups"], per_group=p["per_group"]),
        (sd(d), jax.ShapeDtypeStruct((n, d), BF16), sd(ROUTER_LANES)),
        x, o, g, yb, p["hgg"], p["wout"], p["g2"], p["wr"], p["br"], name="sample_out")
    c_new = jnp.stack([c0[:, 1, :], c0[:, 2, :], xr], axis=1)
    return x1, xn2, gates, s_new, h_new, c_new


def _block_diag(w):
    n, c, _ = w.shape
    eye = jnp.eye(n, dtype=w.dtype)
    return (w[:, :, None, :] * eye[:, None, :, None]).reshape(n * c, n * c)


def _prepare(lower_bounds, ln1_g, w_in, hgrn_norm_g, conv_w, conv_b, lru_wx, lru_bx, lru_wa, lru_ba,
             lru_lambda, w_out, ln2_g, router_wg, router_bg, router_we, router_be, exp_w1, exp_w3,
             exp_w2, final_g):
    d = w_in.shape[1]
    n_groups = router_wg.shape[-1]
    per_group = router_we.shape[-1]
    row = lambda a: a.reshape(1, -1).astype(F32)
    we = jnp.transpose(router_we[0], (1, 0, 2)).reshape(d, n_groups * per_group)
    pad = ROUTER_LANES - n_groups - n_groups * per_group
    wr = jnp.concatenate([router_wg[0], we, jnp.zeros((d, pad), F32)], axis=1)
    br = jnp.concatenate([router_bg[0], router_be[0].reshape(-1), jnp.zeros((pad,), F32)]).reshape(1, -1)
    rows_t = -(-(8 + n_groups * per_group) // 16) * 16
    wr_t = jnp.concatenate([router_wg[0].T, jnp.zeros((8 - n_groups, d), F32), we.T,
                            jnp.zeros((rows_t - 8 - n_groups * per_group, d), F32)], axis=0)
    br_t = jnp.concatenate([router_bg[0], jnp.zeros((8 - n_groups,), F32), router_be[0].reshape(-1),
                            jnp.zeros((rows_t - 8 - n_groups * per_group,), F32)]).reshape(-1, 1)
    wx = _block_diag(lru_wx[0])
    wa = _block_diag(lru_wa[0])
    return dict(
        wr_t_bf=wr_t.astype(BF16), br_t=br_t,
        n_groups=n_groups, per_group=per_group,
        lbw=lower_bounds.astype(F32), g1=row(ln1_g[0]), win=w_in[0], win_bf=w_in[0].astype(BF16),
        hgg=row(hgrn_norm_g[0]), cw=conv_w[0], cb=row(conv_b[0]),
        wx=wx, wx_bf=wx.astype(BF16), bx=row(lru_bx[0]), wa=wa, wa_bf=wa.astype(BF16), ba=row(lru_ba[0]),
        lam=row(lru_lambda[0]), wout=w_out[0], wout_bf=w_out[0].astype(BF16), g2=row(ln2_g[0]),
        wr=wr, br=br, w1_bf=exp_w1[0].astype(BF16), w3_bf=exp_w3[0].astype(BF16),
        w2=exp_w2[0], gf=row(final_g))


def kernel(x_prompt, x_sample, state_hgrn, state_rglru, state_conv, lower_bounds, ln1_g, w_in, hgrn_norm_g, conv_w, conv_b, lru_wx, lru_bx, lru_wa, lru_ba, lru_lambda, w_out, ln2_g, router_wg, router_bg, router_we, router_be, exp_w1, exp_w3, exp_w2, final_g):
    assert w_in.shape[0] == 1, "single-layer trunk"
    p = _prepare(lower_bounds, ln1_g, w_in, hgrn_norm_g, conv_w, conv_b, lru_wx, lru_bx, lru_wa, lru_ba,
                 lru_lambda, w_out, ln2_g, router_wg, router_bg, router_we, router_be, exp_w1, exp_w3,
                 exp_w2, final_g)
    bsz, seq, d = x_prompt.shape
    x1, xn2, gates, counts, s_p, h_p, c_p = _mixer_prompt(x_prompt, p, min(MIXER_BLOCK, seq))
    t = bsz * seq
    y_p = _moe_sorted(x1.reshape(t, d), xn2.reshape(t, d), gates.reshape(t, ROUTER_LANES), counts, p)

    n = x_sample.shape[0]
    x1s, xn2s, gates_s, s_s, h_s, c_s = _mixer_sample(x_sample[:, 0, :], state_hgrn[0], state_rglru[0],
                                                      state_conv[0], p, SAMPLE_STEP_TOKENS)
    y_s = _moe_dense(x1s, xn2s, gates_s, p, n)
    return (y_p.reshape(bsz, seq, d), y_s.reshape(n, 1, d),
            s_p[None], h_p.reshape(1, bsz, -1), c_p[None],
            s_s[None], h_s[None], c_s[None])
```

```python
import functools

import jax
import jax.numpy as jnp
from jax import lax
from jax.experimental import pallas as pl
from jax.experimental.pallas import tpu as pltpu

F32 = jnp.float32
BF16 = jnp.bfloat16
HIGHEST = lax.Precision.HIGHEST

EPS = 1e-6
LRU_C = 8.0
LOG2E = 1.4426950408889634
HEAD_DIM = 128
CHUNK = 64
SUB = 16
UNROLL = 8
ROUTER_LANES = 128
PROMPT_EXPERT_LANE = 8
MIXER_BLOCK = 512
SAMPLE_STEP_TOKENS = 8
TOK_TILE = 512
SEG = 16
MAX_GROUPS = 8
SORT_ROWS = TOK_TILE + MAX_GROUPS * SEG
ROW_TILE = 512
VMEM_LIMIT_BYTES = 56 * 1024 * 1024

NT_DIMS = (((1,), (1,)), ((), ()))
TN_DIMS = (((0,), (0,)), ((), ()))


def _rms(x, g):
    return x * lax.rsqrt(jnp.mean(x * x, axis=-1, keepdims=True) + EPS) * g


def _sigmoid(x):
    return 1.0 / (1.0 + jnp.exp(-x))


def _silu(x):
    return x * _sigmoid(x)


def _gelu_tanh(x):
    c = 0.7978845608028654
    return x * (0.5 * (1.0 + jnp.tanh(c * (x + 0.044715 * (x * x * x)))))


def _softplus(z):
    return jnp.maximum(z, 0.0) + jnp.log1p(jnp.exp(-jnp.abs(z)))


def _expm1(x):
    u = jnp.exp(x)
    um1 = u - 1.0
    small = um1 * x / jnp.log(u)
    return jnp.where(um1 == 0.0, x, jnp.where(jnp.abs(x) < 0.5, small, um1))


def _forget_lower_bound(lbw):
    m = jnp.max(lbw, axis=0, keepdims=True)
    e = jnp.exp(lbw - m)
    return e[0:1, :] / jnp.sum(e, axis=0, keepdims=True)


def _route(logits, n_groups, per_group):
    n = logits.shape[-1]
    col = lax.broadcasted_iota(jnp.int32, logits.shape, 1)
    neg = jnp.float32(-jnp.inf)
    big = jnp.int32(n)
    is_g = col < n_groups
    lg = jnp.where(is_g, logits, neg)
    mg = jnp.max(lg, axis=-1, keepdims=True)
    g_idx = jnp.min(jnp.where(lg == mg, col, big), axis=-1, keepdims=True)
    p_top = 1.0 / jnp.sum(jnp.where(is_g, jnp.exp(logits - mg), 0.0), axis=-1, keepdims=True)
    lo = n_groups + per_group * g_idx
    le = jnp.where((col >= lo) & (col < lo + per_group), logits, neg)
    m1 = jnp.max(le, axis=-1, keepdims=True)
    i1 = jnp.min(jnp.where(le == m1, col, big), axis=-1, keepdims=True)
    le2 = jnp.where(col == i1, neg, le)
    m2 = jnp.max(le2, axis=-1, keepdims=True)
    i2 = jnp.min(jnp.where(le2 == m2, col, big), axis=-1, keepdims=True)
    e2 = jnp.exp(m2 - m1)
    den = 1.0 + e2
    w1 = p_top / den
    w2 = p_top * (e2 / den)
    gates = jnp.where(col == i1, w1, 0.0) + jnp.where(col == i2, w2, 0.0)
    return gates + jnp.where(col == 0, g_idx.astype(F32), 0.0)


def _route_rows(logits_t, n_groups, per_group):
    assert n_groups <= 8 and per_group == 8
    n_tok = logits_t.shape[1]
    row = lax.broadcasted_iota(jnp.int32, (8, n_tok), 0)
    neg = jnp.float32(-jnp.inf)
    big = jnp.int32(8)
    lg = jnp.where(row < n_groups, logits_t[0:8], neg)
    mg = jnp.max(lg, axis=0, keepdims=True)
    g_idx = jnp.min(jnp.where(lg == mg, row, big), axis=0, keepdims=True)
    p_top = 1.0 / jnp.sum(jnp.where(row < n_groups, jnp.exp(lg - mg), 0.0), axis=0, keepdims=True)
    sel = logits_t[8:16]
    for g in range(1, n_groups):
        sel = jnp.where(g_idx == g, logits_t[8 + 8 * g:16 + 8 * g], sel)
    m1 = jnp.max(sel, axis=0, keepdims=True)
    i1 = jnp.min(jnp.where(sel == m1, row, big), axis=0, keepdims=True)
    sel2 = jnp.where(row == i1, neg, sel)
    m2 = jnp.max(sel2, axis=0, keepdims=True)
    i2 = jnp.min(jnp.where(sel2 == m2, row, big), axis=0, keepdims=True)
    e2 = jnp.exp(m2 - m1)
    den = 1.0 + e2
    w1 = p_top / den
    w2 = p_top * (e2 / den)
    own = jnp.where(row == i1, w1, 0.0) + jnp.where(row == i2, w2, 0.0)
    blocks = [jnp.where(row == 0, g_idx.astype(F32), 0.0)]
    blocks += [jnp.where(g_idx == g, own, 0.0) for g in range(n_groups)]
    blocks += [jnp.zeros((8, n_tok), F32)] * (ROUTER_LANES // 8 - len(blocks))
    return jnp.concatenate(blocks, axis=0), g_idx


def _mixer_prompt_kernel(x_ref, lbw_ref, g1_ref, win_ref, hgg_ref, cw_ref, cb_ref, wx_ref, bx_ref,
                         wa_ref, ba_ref, lam_ref, wout_ref, g2_ref, wr_ref, br_ref,
                         x1_ref, xn2_ref, gates_ref, cnt_ref, sout_ref, hout_ref, cout_ref,
                         proj_s, k_s, b_s, o_s, st_s, xr_s, a_s, u_s, hcar_s, yb_out_s,
                         *, n_groups, per_group):
    lb_t = x_ref.shape[1]
    wa_w = o_s.shape[1]
    wb_w = a_s.shape[1]
    n_heads = wa_w // HEAD_DIM
    j = pl.program_id(1)
    nj = pl.num_programs(1)

    @pl.when(j == 0)
    def _():
        st_s[...] = jnp.zeros_like(st_s)
        hcar_s[...] = jnp.zeros_like(hcar_s)
        xr_s[0:8, :] = jnp.zeros((8, wb_w), F32)

    x = x_ref[0]
    xn = _rms(x, g1_ref[...]).astype(BF16)
    xb0 = 4 * wa_w

    def project(pieces):
        for c, w in pieces:
            proj_s[:, c:c + w] = jnp.dot(xn, win_ref[:, c:c + w], preferred_element_type=F32)

    project(((wa_w, wa_w), (xb0, wb_w), (xb0 + wb_w, wb_w), (0, wa_w), (2 * wa_w, wa_w), (3 * wa_w, wa_w)))

    xr_s[pl.ds(8, lb_t), :] = proj_s[:, xb0:xb0 + wb_w]
    xc = (cb_ref[...] + cw_ref[3:4, :] * xr_s[pl.ds(8, lb_t), :] + cw_ref[2:3, :] * xr_s[pl.ds(7, lb_t), :]
          + cw_ref[1:2, :] * xr_s[pl.ds(6, lb_t), :] + cw_ref[0:1, :] * xr_s[pl.ds(5, lb_t), :])
    tail = xr_s[pl.ds(lb_t + 5, 3), :]
    xr_s[5:8, :] = tail
    cout_ref[0] = tail
    xcb = xc.astype(BF16)
    gate_x = _sigmoid(jnp.dot(xcb, wx_ref[...], preferred_element_type=F32) + bx_ref[...])
    gate_a = _sigmoid(jnp.dot(xcb, wa_ref[...], preferred_element_type=F32) + ba_ref[...])
    log_a =(-LRU_C) * gate_a * _softplus(-lam_ref[...])
    a = jnp.exp(log_a)
    mult = jnp.sqrt((1.0 - a) * (1.0 + a))
    first =(lax.broadcasted_iota(jnp.int32, (lb_t, 1), 0) == 0) & (j == 0)
    a = jnp.where(first, 0.0, a)
    mult = jnp.where(first, 1.0, mult)
    a_s[...] = a
    u_s[...] = gate_x * xc * mult

    lb = _forget_lower_bound(lbw_ref[...])
    f = lb + (1.0 - lb) * _sigmoid(proj_s[:, wa_w:2 * wa_w])
    for h in range(n_heads):
        k_s[h] = 1.0 - f[:, h * HEAD_DIM:(h + 1) * HEAD_DIM]
    logf = jnp.log(f)
    r_i = lax.broadcasted_iota(jnp.int32, (CHUNK, CHUNK), 0)
    c_i = lax.broadcasted_iota(jnp.int32, (CHUNK, CHUNK), 1)
    tri = jnp.where(r_i >= c_i, 1.0, 0.0).astype(BF16)
    lf_hi = logf.astype(BF16)
    rest = logf - lf_hi.astype(F32)
    lf_mid = rest.astype(BF16)
    lf_lo = (rest - lf_mid.astype(F32)).astype(BF16)
    for c in range(0, lb_t, CHUNK):
        cum = [jnp.dot(tri, part[c:c + CHUNK, :], preferred_element_type=F32) for part in (lf_lo, lf_mid, lf_hi)]
        b_all = LOG2E * ((cum[0] + cum[1]) + cum[2])
        for h in range(n_heads):
            b_s[h, c:c + CHUNK, :] = b_all[:, h * HEAD_DIM:(h + 1) * HEAD_DIM]

    row_sub = lax.broadcasted_iota(jnp.int32, (SUB, HEAD_DIM), 0)
    lane_sub = lax.broadcasted_iota(jnp.int32, (SUB, HEAD_DIM), 1)
    assert n_heads % 2 == 0
    r_kk = lax.broadcasted_iota(jnp.int32, (2 * HEAD_DIM, 2 * HEAD_DIM), 0)
    c_kk = lax.broadcasted_iota(jnp.int32, (2 * HEAD_DIM, 2 * HEAD_DIM), 1)
    ones_kk = jnp.where((r_kk < HEAD_DIM) == (c_kk < HEAD_DIM), 1.0, 0.0).astype(BF16)

    n_sub = CHUNK // SUB
    half = SUB // 2
    lower_left = (row_sub >= half) & (lane_sub < half)
    own_lane = jnp.where(row_sub >= half, half, 0)

    def chunk_start(r0):
        first = []
        for h in range(n_heads):
            hs = slice(h * HEAD_DIM, (h + 1) * HEAD_DIM)
            q = proj_s[pl.ds(r0, CHUNK), hs]
            b = b_s[h, pl.ds(r0, CHUNK), :]
            v = proj_s[pl.ds(r0, CHUNK), 2 * wa_w + h * HEAD_DIM:2 * wa_w + (h + 1) * HEAD_DIM]
            k = k_s[h, pl.ds(r0, CHUNK), :]

            def key_rows(ref, lo, j):
                return jnp.stack([jnp.broadcast_to(ref[h, pl.ds(r0 + lo + hf * half + j, 1), :], (half, HEAD_DIM))
                                  for hf in range(2)])
            vb = v.astype(BF16)
            st = st_s[h]
            b_last = b[CHUNK - 1:CHUNK, :]
            o = lax.dot_general((q * jnp.exp2(b)).astype(BF16), st.astype(BF16), NT_DIMS,
                                preferred_element_type=F32)
            k_end = k * jnp.exp2(b_last - b)
            st_s[h] = st * jnp.exp2(b_last) + lax.dot_general(vb, k_end.astype(BF16), TN_DIMS,
                                                               preferred_element_type=F32)
            terms, off, mid = [], [], []
            for i in range(n_sub):
                lo = i * SUB
                qi, bi, ki = q[lo:lo + SUB], b[lo:lo + SUB], k[lo:lo + SUB]
                q3, b3 = (a.reshape(2, half, HEAD_DIM) for a in (qi, bi))
                terms += [(q3 * (key_rows(k_s, lo, j) * jnp.exp2(b3 - key_rows(b_s, lo, j))))
                          .reshape(SUB, HEAD_DIM).astype(BF16) for j in range(half)]
                rm = bi[half - 1:half]
                mid.append(lax.dot_general((qi * jnp.exp2(bi - rm)).astype(BF16),
                                           (ki * jnp.exp2(rm - bi)).astype(BF16), NT_DIMS,
                                           preferred_element_type=F32))
                if i > 0:
                    r = b[lo - 1:lo]
                    qt = (qi * jnp.exp2(bi - r)).astype(BF16)
                    kt = (k[:lo] * jnp.exp2(r - b[:lo])).astype(BF16)
                    off.append(lax.dot_general(qt, kt, NT_DIMS, preferred_element_type=F32))
            first.append([o, vb, off, jnp.concatenate(terms, axis=0), mid])
        for h in range(0, n_heads, 2):
            both = jnp.concatenate([first[h][3], first[h + 1][3]], axis=1)
            sums = jnp.dot(both, ones_kk, preferred_element_type=F32)
            first[h][3] = sums[:, :HEAD_DIM]
            first[h + 1][3] = sums[:, HEAD_DIM:]
        return first

    def chunk_finish(r0, first):
        for h in range(n_heads):
            o, vb, off, sums, mid = first[h]
            outs = []
            for i in range(n_sub):
                lo = i * SUB
                sc = jnp.zeros((SUB, HEAD_DIM), F32)
                for j in range(half):
                    row0 = (i * half + j) * SUB
                    sc = jnp.where(lane_sub == own_lane + j, sums[row0:row0 + SUB], sc)
                sc = jnp.where(row_sub >= lane_sub, sc, 0.0)[:, :SUB]
                sc = jnp.where(lower_left[:, :SUB], mid[i], sc)
                od = o[lo:lo + SUB] + jnp.dot(sc.astype(BF16), vb[lo:lo + SUB], preferred_element_type=F32)
                if i > 0:
                    od = od + jnp.dot(off[i - 1].astype(BF16), vb[:lo], preferred_element_type=F32)
                outs.append(od)
            o_s[pl.ds(r0, CHUNK), h * HEAD_DIM:(h + 1) * HEAD_DIM] = jnp.concatenate(outs, axis=0)

    def chunks_body(ci, carry):
        rows = [pl.multiple_of((ci * UNROLL + u) * CHUNK, CHUNK) for u in range(UNROLL)]
        started = [chunk_start(r0) for r0 in rows]
        for r0, first in zip(rows, started):
            chunk_finish(r0, first)
        return carry

    row8 = lax.broadcasted_iota(jnp.int32, (8, 1), 0)

    def scan_body(gi, carry):
        r0 = pl.multiple_of(gi * 8, 8)
        aa = a_s[pl.ds(r0, 8), :]
        uu = u_s[pl.ds(r0, 8), :]
        for s in (1, 2, 4):
            m = row8 >= s
            uu = jnp.where(m, aa * pltpu.roll(uu, s, 0) + uu, uu)
            aa = jnp.where(m, aa * pltpu.roll(aa, s, 0), aa)
        hh = aa * carry + uu
        u_s[pl.ds(r0, 8), :] = hh
        return hh[7:8, :]

    h_last = lax.fori_loop(0, lb_t // 8, scan_body, hcar_s[...])
    hcar_s[...] = h_last
    hout_ref[0] = h_last
    yb = (u_s[...] * _gelu_tanh(proj_s[:, xb0 + wb_w:xb0 + 2 * wb_w])).astype(BF16)
    yb_out_s[...] = jnp.dot(yb, wout_ref[wa_w:, :], preferred_element_type=F32)

    assert lb_t % (CHUNK * UNROLL) == 0
    lax.fori_loop(0, lb_t // (CHUNK * UNROLL), chunks_body, 0)

    ya = []
    for h in range(n_heads):
        hs = slice(h * HEAD_DIM, (h + 1) * HEAD_DIM)
        oh = o_s[:, hs]
        oh = oh * lax.rsqrt(jnp.mean(oh * oh, axis=-1, keepdims=True) + EPS) * hgg_ref[:, hs]
        ya.append(oh * _silu(proj_s[:, 3 * wa_w + h * HEAD_DIM:3 * wa_w + (h + 1) * HEAD_DIM]))

    ya = jnp.concatenate(ya, axis=-1).astype(BF16)
    x1 = x + (jnp.dot(ya, wout_ref[:wa_w, :], preferred_element_type=F32) + yb_out_s[...])
    x1_ref[0] = x1
    xn2 = _rms(x1, g2_ref[...]).astype(BF16)
    xn2_ref[0] = xn2
    logits_t = lax.dot_general(wr_ref[...], xn2, NT_DIMS, preferred_element_type=F32) + br_ref[...]
    gates_t, g_idx = _route_rows(logits_t, n_groups, per_group)
    gates_ref[0] = gates_t.T
    lane_c = lax.broadcasted_iota(jnp.int32, (1, ROUTER_LANES), 1)
    for t in range(lb_t // TOK_TILE):
        gi = g_idx[:, t * TOK_TILE:(t + 1) * TOK_TILE]
        cnt = jnp.zeros((1, ROUTER_LANES), jnp.int32)
        for g in range(n_groups):
            cnt = cnt + jnp.where(lane_c == g, jnp.sum(jnp.where(gi == g, 1, 0), axis=1, keepdims=True), 0)
        cnt_ref[0, t:t + 1, :] = cnt

    @pl.when(j == nj - 1)
    def _():
        for h in range(n_heads):
            sout_ref[0, h] = st_s[h].T


def _const_spec(shape):
    nd = len(shape)
    return pl.BlockSpec(shape, lambda *_: (0,) * nd)


def _mixer_prompt(x, p, lb_t):
    bsz, seq, d = x.shape
    wa_w = p["hgg"].shape[1]
    wb_w = p["cb"].shape[1]
    n_heads = wa_w // HEAD_DIM
    n_cols = p["win_bf"].shape[1]
    weights = [p["lbw"], p["g1"], p["win_bf"], p["hgg"], p["cw"], p["cb"], p["wx_bf"], p["bx"],
               p["wa_bf"], p["ba"], p["lam"], p["wout_bf"], p["g2"], p["wr_t_bf"], p["br_t"]]
    nj = seq // lb_t
    tile = lambda w: pl.BlockSpec((1, lb_t, w), lambda b, j: (b, j, 0))
    out_shape = (
        jax.ShapeDtypeStruct((bsz, seq, d), F32),
        jax.ShapeDtypeStruct((bsz, seq, d), BF16),
        jax.ShapeDtypeStruct((bsz, seq, ROUTER_LANES), F32),
        jax.ShapeDtypeStruct((bsz * (seq // lb_t), lb_t // TOK_TILE, ROUTER_LANES), jnp.int32),
        jax.ShapeDtypeStruct((bsz, n_heads, HEAD_DIM, HEAD_DIM), F32),
        jax.ShapeDtypeStruct((bsz, 1, wb_w), F32),
        jax.ShapeDtypeStruct((bsz, 3, wb_w), F32),
    )
    out_specs = (
        tile(d), tile(d), tile(ROUTER_LANES),
        pl.BlockSpec((1, lb_t // TOK_TILE, ROUTER_LANES), lambda b, j: (b * nj + j, 0, 0)),
        pl.BlockSpec((1, n_heads, HEAD_DIM, HEAD_DIM), lambda b, j: (b, 0, 0, 0)),
        pl.BlockSpec((1, 1, wb_w), lambda b, j: (b, 0, 0)),
        pl.BlockSpec((1, 3, wb_w), lambda b, j: (b, 0, 0)),
    )
    scratch = [
        pltpu.VMEM((lb_t, n_cols), F32),
        pltpu.VMEM((n_heads, lb_t, HEAD_DIM), F32),
        pltpu.VMEM((n_heads, lb_t, HEAD_DIM), F32),
        pltpu.VMEM((lb_t, wa_w), F32),
        pltpu.VMEM((n_heads, HEAD_DIM, HEAD_DIM), F32),
        pltpu.VMEM((lb_t + 8, wb_w), F32),
        pltpu.VMEM((lb_t, wb_w), F32),
        pltpu.VMEM((lb_t, wb_w), F32),
        pltpu.VMEM((1, wb_w), F32),
        pltpu.VMEM((lb_t, d), F32),
    ]
    kern = functools.partial(_mixer_prompt_kernel, n_groups=p["n_groups"], per_group=p["per_group"])
    return pl.pallas_call(
        kern,
        grid=(bsz, nj),
        in_specs=[tile(d)] + [_const_spec(w.shape) for w in weights],
        out_specs=out_specs,
        out_shape=out_shape,
        scratch_shapes=scratch,
        compiler_params=pltpu.CompilerParams(dimension_semantics=("arbitrary", "arbitrary"),
                                             vmem_limit_bytes=VMEM_LIMIT_BYTES),
        name="mixer_prompt",
    )(x, *weights)


def _moe_kernel(x1_ref, xn2_ref, gates_ref, w1_ref, w3_ref, w2_ref, gf_ref, y_ref, acc_s, *, n_groups):
    e = pl.program_id(1)

    @pl.when(e == 0)
    def _():
        acc_s[...] = jnp.zeros_like(acc_s)

    xb = xn2_ref[...]
    per_step = w1_ref.shape[0]
    gates = pltpu.roll(gates_ref[...], ROUTER_LANES - n_groups - per_step * e, 1)
    acc = acc_s[...]
    for s in range(per_step):
        h = (_silu(jnp.dot(xb, w1_ref[s], preferred_element_type=F32))
             * jnp.dot(xb, w3_ref[s], preferred_element_type=F32) * gates[:, s:s + 1])
        acc = acc + jnp.dot(h.astype(BF16), w2_ref[s].astype(BF16), preferred_element_type=F32)
    acc_s[...] = acc

    @pl.when(e == pl.num_programs(1) - 1)
    def _():
        y_ref[...] = _rms(x1_ref[...] + acc_s[...], gf_ref[...])


def _moe_dense(x1, xn2, gates, p, tm):
    t, d = x1.shape
    n_exp, _, d_exp = p["w1_bf"].shape
    per_step = p["per_group"]
    row = lambda w: pl.BlockSpec((tm, w), lambda i, e: (i, 0))
    return pl.pallas_call(
        functools.partial(_moe_kernel, n_groups=p["n_groups"]),
        grid=(t // tm, n_exp // per_step),
        in_specs=[row(d), row(d), row(ROUTER_LANES),
                  pl.BlockSpec((per_step, d, d_exp), lambda i, e: (e, 0, 0)),
                  pl.BlockSpec((per_step, d, d_exp), lambda i, e: (e, 0, 0)),
                  pl.BlockSpec((per_step, d_exp, d), lambda i, e: (e, 0, 0)),
                  _const_spec(p["gf"].shape)],
        out_specs=row(d),
        out_shape=jax.ShapeDtypeStruct((t, d), F32),
        scratch_shapes=[pltpu.VMEM((tm, d), F32)],
        compiler_params=pltpu.CompilerParams(dimension_semantics=("arbitrary", "arbitrary"),
                                             vmem_limit_bytes=VMEM_LIMIT_BYTES),
        name="moe_dense",
    )(x1, xn2, gates, p["w1_bf"], p["w3_bf"], p["w2"], p["gf"])


def _seg_pad(n):
    return jnp.bitwise_and(n + (SEG - 1), -SEG)


def _row_tile_pad(n):
    return jnp.bitwise_and(n + (ROW_TILE - 1), -ROW_TILE)


def _tile_segments(cnt_ref, tile, n_groups):
    padded = [_seg_pad(cnt_ref[tile * n_groups + g]) for g in range(n_groups)]
    starts, acc = [], jnp.int32(0)
    for g in range(n_groups):
        starts.append(acc)
        acc = acc + padded[g]
    return padded, starts, acc


def _group_bases(cnt_ref, n_tiles, n_groups):
    def body(t, tot):
        return tuple(tot[g] + _seg_pad(cnt_ref[t * n_groups + g]) for g in range(n_groups))
    totals = lax.fori_loop(0, n_tiles, body, (jnp.int32(0),) * n_groups)
    bases, ends, acc = [], [], jnp.int32(0)
    for g in range(n_groups):
        bases.append(acc)
        acc = acc + _row_tile_pad(totals[g])
        ends.append(acc)
    return bases, ends, totals


def _chunk_rows(n_chunks, starts, gstart):
    rows, row = [], None
    for c in range(n_chunks):
        row = gstart[0] if c == 0 else row + SEG
        for g in range(1, len(starts)):
            row = jnp.where(starts[g] == c * SEG, gstart[g], row)
        rows.append(pl.multiple_of(row, SEG))
    return rows


def _sort_matrix(gates, starts):
    n_groups = len(starts)
    col = lax.broadcasted_iota(jnp.int32, (TOK_TILE, ROUTER_LANES), 1)
    gi = gates[:, 0:1].astype(jnp.int32)
    onehot = col == gi
    r_i = lax.broadcasted_iota(jnp.int32, (TOK_TILE, TOK_TILE), 0)
    c_i = lax.broadcasted_iota(jnp.int32, (TOK_TILE, TOK_TILE), 1)
    earlier = jnp.where(r_i > c_i, 1.0, 0.0).astype(BF16)
    before = jnp.dot(earlier, jnp.where(onehot, 1.0, 0.0).astype(BF16), preferred_element_type=F32)
    rank = jnp.sum(jnp.where(onehot, before, 0.0), axis=-1, keepdims=True).astype(jnp.int32)
    base = jnp.zeros((TOK_TILE, 1), jnp.int32)
    for g in range(n_groups):
        base = base + jnp.where(gi == g, starts[g], 0)
    lane = lax.broadcasted_iota(jnp.int32, (TOK_TILE, SORT_ROWS), 1)
    return lane == base + rank


def _sort_matrix_rows(gates, starts):
    n_groups = len(starts)
    col = lax.broadcasted_iota(jnp.int32, (TOK_TILE, ROUTER_LANES), 1)
    onehot_t = jnp.where(col == gates[:, 0:1].astype(jnp.int32), 1.0, 0.0).T
    r_i = lax.broadcasted_iota(jnp.int32, (TOK_TILE, TOK_TILE), 0)
    c_i = lax.broadcasted_iota(jnp.int32, (TOK_TILE, TOK_TILE), 1)
    later = jnp.where(r_i < c_i, 1.0, 0.0).astype(BF16)
    before = jnp.dot(onehot_t.astype(BF16), later, preferred_element_type=F32)
    g_row = lax.broadcasted_iota(jnp.int32, (ROUTER_LANES, 1), 0)
    base = jnp.zeros((ROUTER_LANES, 1), jnp.int32)
    for g in range(n_groups):
        base = base + jnp.where(g_row == g, starts[g], 0)
    dest = jnp.sum(jnp.where(onehot_t > 0.0, before + base.astype(F32), 0.0), axis=0, keepdims=True)
    row = lax.broadcasted_iota(jnp.int32, (SORT_ROWS, TOK_TILE), 0)
    return row == dest.astype(jnp.int32)


def _dispatch_copies(xbuf, gbuf, xs_hbm, gs_hbm, sem, slot, c, row):
    return (pltpu.make_async_copy(xbuf.at[slot, pl.ds(c * SEG, SEG), :], xs_hbm.at[pl.ds(row, SEG), :], sem.at[slot]),
            pltpu.make_async_copy(gbuf.at[slot, pl.ds(c * SEG, SEG), :], gs_hbm.at[pl.ds(row, SEG), :], sem.at[slot]))


def _moe_dispatch_kernel(cnt_ref, xn2_ref, gates_ref, xs_hbm, gs_hbm, tg_ref,
                         xbuf, gbuf, sem, gstart_s, zx, zg, zsem, *, n_groups, per_group):
    j = pl.program_id(0)
    n_tiles = pl.num_programs(0)
    n_chunks = (TOK_TILE + n_groups * SEG) // SEG
    slot = lax.rem(j, 2)

    @pl.when(j == 0)
    def _():
        bases, ends, totals = _group_bases(cnt_ref, n_tiles, n_groups)
        for g in range(n_groups):
            gstart_s[g] = bases[g]
        n_rt = tg_ref.shape[0] - 1
        for i in range(n_rt):
            tg = jnp.int32(0)
            for g in range(n_groups):
                tg = tg + jnp.where(ends[g] <= i * ROW_TILE, 1, 0)
            tg_ref[i] = tg
        n_used = ends[n_groups - 1] // ROW_TILE
        tg_ref[n_rt] = n_used

        zx[...] = jnp.zeros_like(zx)
        zg[...] = jnp.zeros_like(zg)

        def zero_copies(row, n):
            return (pltpu.make_async_copy(zx.at[pl.ds(0, n), :], xs_hbm.at[pl.ds(row, n), :], zsem.at[0]),
                    pltpu.make_async_copy(zg.at[pl.ds(0, n), :], gs_hbm.at[pl.ds(row, n), :], zsem.at[0]))

        def for_each_gap(act):
            for g in range(n_groups):
                tail = bases[g] + totals[g]

                def seg_body(k, c, tail=tail):
                    for cp in zero_copies(pl.multiple_of(tail + k * SEG, SEG), SEG):
                        act(cp)
                    return c
                lax.fori_loop(0, (ends[g] - tail) // SEG, seg_body, 0)

            def tile_body(k, c):
                for cp in zero_copies(pl.multiple_of((n_used + k) * ROW_TILE, ROW_TILE), ROW_TILE):
                    act(cp)
                return c
            lax.fori_loop(0, n_rt - n_used, tile_body, 0)

        for_each_gap(lambda cp: cp.start())
        for_each_gap(lambda cp: cp.wait())

    def wait_tile(tile, slot_):
        _, _, used = _tile_segments(cnt_ref, tile, n_groups)
        for c in range(n_chunks):
            @pl.when(c * SEG < used)
            def _():
                for cp in _dispatch_copies(xbuf, gbuf, xs_hbm, gs_hbm, sem, slot_, c, 0):
                    cp.wait()

    @pl.when(j >= 2)
    def _():
        wait_tile(j - 2, slot)

    padded, starts, used = _tile_segments(cnt_ref, j, n_groups)
    gates = gates_ref[...]
    d = xn2_ref.shape[1]
    g_hi = gates.astype(BF16)
    g_lo = (gates - g_hi.astype(F32)).astype(BF16)
    sort_m = jnp.where(_sort_matrix_rows(gates, starts), 1.0, 0.0).astype(BF16)
    moved = jnp.dot(sort_m, jnp.concatenate([xn2_ref[...], g_hi, g_lo], axis=1), preferred_element_type=F32)
    xbuf[slot] = moved[:, :d].astype(BF16)
    gbuf[slot] = moved[:, d:d + ROUTER_LANES] + moved[:, d + ROUTER_LANES:]
    gstart = [gstart_s[g] for g in range(n_groups)]
    rows = _chunk_rows(n_chunks, starts, gstart)
    for c in range(n_chunks):
        @pl.when(c * SEG < used)
        def _():
            for cp in _dispatch_copies(xbuf, gbuf, xs_hbm, gs_hbm, sem, slot, c, rows[c]):
                cp.start()
    for g in range(n_groups):
        gstart_s[g] = gstart[g] + padded[g]

    @pl.when(j == n_tiles - 1)
    def _():
        @pl.when(j >= 1)
        def _():
            wait_tile(j - 1, 1 - slot)
        wait_tile(j, slot)


def _moe_group_kernel(tg_ref, xs_ref, gs_ref, w1_ref, w3_ref, w2_ref, ys_ref, w2_s, *, n_groups, per_group):
    i = pl.program_id(0)

    @pl.when((i == 0) | (tg_ref[i] != tg_ref[jnp.maximum(i - 1, 0)]))
    def _():
        w2_s[...] = w2_ref[...].astype(BF16)

    @pl.when(tg_ref[i] < n_groups)
    def _():
        xb = xs_ref[...]
        gates = pltpu.roll(gs_ref[...], ROUTER_LANES - PROMPT_EXPERT_LANE - per_group * tg_ref[i], 1)
        acc = jnp.zeros(ys_ref.shape, F32)
        for s in range(per_group):
            h = (_silu(jnp.dot(xb, w1_ref[s], preferred_element_type=F32))
                 * jnp.dot(xb, w3_ref[s], preferred_element_type=F32) * gates[:, s:s + 1])
            acc = acc + jnp.dot(h.astype(BF16), w2_s[s], preferred_element_type=F32)
        ys_ref[...] = acc.astype(BF16)


X1_SLOTS = 3


def _moe_combine_kernel(cnt_ref, gates_ref, x1_hbm, gf_ref, ys_hbm, y_ref, ybuf, sem, gstart_s, x1buf, x1sem,
                        *, n_groups):
    j = pl.program_id(0)
    n_tiles = pl.num_programs(0)
    n_chunks = SORT_ROWS // SEG
    slot = lax.rem(j, 2)

    def x1_copy(tile):
        s = lax.rem(tile, X1_SLOTS)
        return pltpu.make_async_copy(x1_hbm.at[pl.ds(pl.multiple_of(tile * TOK_TILE, TOK_TILE), TOK_TILE), :],
                                     x1buf.at[s], x1sem.at[s])

    @pl.when(j == 0)
    def _():
        for ahead in range(X1_SLOTS - 1):
            @pl.when(ahead < n_tiles)
            def _():
                x1_copy(ahead).start()

    @pl.when(j + (X1_SLOTS - 1) < n_tiles)
    def _():
        x1_copy(j + (X1_SLOTS - 1)).start()

    def copies(slot_, c, row):
        return pltpu.make_async_copy(ys_hbm.at[pl.ds(row, SEG), :], ybuf.at[slot_, pl.ds(c * SEG, SEG), :],
                                     sem.at[slot_])

    def fetch_tile(tile, slot_):
        padded, starts, _ = _tile_segments(cnt_ref, tile, n_groups)
        gstart = [gstart_s[g] for g in range(n_groups)]
        for c, row in enumerate(_chunk_rows(n_chunks, starts, gstart)):
            copies(slot_, c, row).start()
        for g in range(n_groups):
            gstart_s[g] = gstart[g] + padded[g]

    @pl.when(j == 0)
    def _():
        bases, _, _ = _group_bases(cnt_ref, n_tiles, n_groups)
        for g in range(n_groups):
            gstart_s[g] = bases[g]
        fetch_tile(0, 0)

    @pl.when(j + 1 < n_tiles)
    def _():
        fetch_tile(j + 1, 1 - slot)

    for c in range(n_chunks):
        copies(slot, c, 0).wait()
    _, starts, _ = _tile_segments(cnt_ref, j, n_groups)
    sort_t = _sort_matrix(gates_ref[...], starts)
    moe = jnp.dot(jnp.where(sort_t, 1.0, 0.0).astype(BF16), ybuf[slot], preferred_element_type=F32)
    x1_copy(j).wait()
    y_ref[...] = _rms(x1buf[lax.rem(j, X1_SLOTS)] + moe, gf_ref[...])


def _moe_sorted(x1, xn2, gates, counts, p):
    t, d = x1.shape
    n_groups, per_group = p["n_groups"], p["per_group"]
    assert n_groups <= MAX_GROUPS and t % TOK_TILE == 0
    n_tiles = t // TOK_TILE
    d_exp = p["w1_bf"].shape[2]
    cap = t + n_tiles * n_groups * SEG + n_groups * ROW_TILE + SORT_ROWS
    n_rt = -(-cap // ROW_TILE)
    rows = n_rt * ROW_TILE
    cnt = counts[:, :, :n_groups].reshape(-1)
    params = pltpu.CompilerParams(dimension_semantics=("arbitrary",), vmem_limit_bytes=VMEM_LIMIT_BYTES)
    any_spec = pl.BlockSpec(memory_space=pl.ANY)
    tok = lambda w: pl.BlockSpec((TOK_TILE, w), lambda j, c: (j, 0))

    xs, gs, tile_group = pl.pallas_call(
        functools.partial(_moe_dispatch_kernel, n_groups=n_groups, per_group=per_group),
        grid_spec=pltpu.PrefetchScalarGridSpec(
            num_scalar_prefetch=1, grid=(n_tiles,),
            in_specs=[tok(d), tok(ROUTER_LANES)],
            out_specs=(any_spec, any_spec, pl.BlockSpec(memory_space=pltpu.SMEM)),
            scratch_shapes=[pltpu.VMEM((2, SORT_ROWS, d), BF16), pltpu.VMEM((2, SORT_ROWS, ROUTER_LANES), F32),
                            pltpu.SemaphoreType.DMA((2,)), pltpu.SMEM((n_groups,), jnp.int32),
                            pltpu.VMEM((ROW_TILE, d), BF16), pltpu.VMEM((ROW_TILE, ROUTER_LANES), F32),
                            pltpu.SemaphoreType.DMA((1,))]),
        out_shape=(jax.ShapeDtypeStruct((rows, d), BF16), jax.ShapeDtypeStruct((rows, ROUTER_LANES), F32),
                   jax.ShapeDtypeStruct((n_rt + 1,), jnp.int32)),
        compiler_params=params, name="moe_dispatch",
    )(cnt, xn2, gates)

    used_tile = lambda i, tg: (jnp.minimum(i, tg[n_rt] - 1), 0)
    group_w = lambda i, tg: (jnp.minimum(tg[i], n_groups - 1), 0, 0)
    ys = pl.pallas_call(
        functools.partial(_moe_group_kernel, n_groups=n_groups, per_group=per_group),
        grid_spec=pltpu.PrefetchScalarGridSpec(
            num_scalar_prefetch=1, grid=(n_rt,),
            in_specs=[pl.BlockSpec((ROW_TILE, d), used_tile), pl.BlockSpec((ROW_TILE, ROUTER_LANES), used_tile),
                      pl.BlockSpec((per_group, d, d_exp), group_w), pl.BlockSpec((per_group, d, d_exp), group_w),
                      pl.BlockSpec((per_group, d_exp, d), group_w)],
            out_specs=pl.BlockSpec((ROW_TILE, d), used_tile),
            scratch_shapes=[pltpu.VMEM((per_group, d_exp, d), BF16)]),
        out_shape=jax.ShapeDtypeStruct((rows, d), BF16),
        input_output_aliases={1: 0},
        compiler_params=params, name="moe_experts",
    )(tile_group, xs, gs, p["w1_bf"], p["w3_bf"], p["w2"])

    return pl.pallas_call(
        functools.partial(_moe_combine_kernel, n_groups=n_groups),
        grid_spec=pltpu.PrefetchScalarGridSpec(
            num_scalar_prefetch=1, grid=(n_tiles,),
            in_specs=[tok(ROUTER_LANES), any_spec, pl.BlockSpec(p["gf"].shape, lambda j, c: (0, 0)), any_spec],
            out_specs=tok(d),
            scratch_shapes=[pltpu.VMEM((2, SORT_ROWS, d), BF16), pltpu.SemaphoreType.DMA((2,)),
                            pltpu.SMEM((n_groups,), jnp.int32),
                            pltpu.VMEM((X1_SLOTS, TOK_TILE, d), F32), pltpu.SemaphoreType.DMA((X1_SLOTS,))]),
        out_shape=jax.ShapeDtypeStruct((t, d), F32),
        compiler_params=params, name="moe_combine",
    )(cnt, gates, x1, p["gf"], ys)


def _sample_in_kernel(x_ref, c0_ref, c1_ref, c2_ref, h0_ref, lbw_ref, g1_ref, win_ref, cw_ref, cb_ref,
                      wx_ref, bx_ref, wa_ref, ba_ref, lam_ref,
                      q_ref, f_ref, v_ref, g_ref, yb_ref, hnew_ref, xr_ref):
    wa_w = v_ref.shape[1]
    wb_w = yb_ref.shape[1]
    xn = _rms(x_ref[...], g1_ref[...])
    proj = jnp.dot(xn, win_ref[...], precision=HIGHEST, preferred_element_type=F32)
    lb = _forget_lower_bound(lbw_ref[...])
    f = lb + (1.0 - lb) * _sigmoid(proj[:, wa_w:2 * wa_w])
    q_ref[...] = proj[:, 0:wa_w].T
    f_ref[...] = f.T
    v_ref[...] = proj[:, 2 * wa_w:3 * wa_w]
    g_ref[...] = proj[:, 3 * wa_w:4 * wa_w]
    xr = proj[:, 4 * wa_w:4 * wa_w + wb_w]
    xr_ref[...] = xr
    xc = (cb_ref[...] + cw_ref[0:1, :] * c0_ref[...] + cw_ref[1:2, :] * c1_ref[...]
          + cw_ref[2:3, :] * c2_ref[...] + cw_ref[3:4, :] * xr)
    gate_x = _sigmoid(jnp.dot(xc, wx_ref[...], precision=HIGHEST, preferred_element_type=F32) + bx_ref[...])
    gate_a = _sigmoid(jnp.dot(xc, wa_ref[...], precision=HIGHEST, preferred_element_type=F32) + ba_ref[...])
    log_a = (-LRU_C) * gate_a * _softplus(-lam_ref[...])
    a = jnp.exp(log_a)
    mult = jnp.sqrt(-_expm1(2.0 * log_a))
    h = a * h0_ref[...] + gate_x * xc * mult
    hnew_ref[...] = h
    yb_ref[...] = h * _gelu_tanh(proj[:, 4 * wa_w + wb_w:4 * wa_w + 2 * wb_w])


def _sample_state_kernel(s_ref, qt_ref, ft_ref, v_ref, snew_ref, o_ref):
    tb, n_heads = s_ref.shape[0], s_ref.shape[1]
    n_tok = qt_ref.shape[1]
    shift = lax.rem(n_tok - pl.program_id(0) * tb, n_tok)
    sq = (HEAD_DIM, HEAD_DIM)
    for h in range(n_heads):
        hs = slice(h * HEAD_DIM, (h + 1) * HEAD_DIM)
        qh = pltpu.roll(qt_ref[hs, :], shift, 1)
        fh = pltpu.roll(ft_ref[hs, :], shift, 1)
        rows = []
        for t in range(tb):
            f_all = jnp.broadcast_to(fh[:, t:t + 1], sq)
            s_new = f_all * s_ref[t, h] + (1.0 - f_all) * v_ref[t:t + 1, hs]
            snew_ref[t, h] = s_new
            rows.append(jnp.sum(jnp.broadcast_to(qh[:, t:t + 1], sq) * s_new, axis=0, keepdims=True))
        o_ref[:, hs] = jnp.concatenate(rows, axis=0)


def _sample_out_kernel(x_ref, o_ref, g_ref, yb_ref, hgg_ref, wout_ref, g2_ref, wr_ref, br_ref,
                       x1_ref, xn2_ref, gates_ref, *, n_groups, per_group):
    wa_w = o_ref.shape[1]
    ya = []
    for h in range(wa_w // HEAD_DIM):
        hs = slice(h * HEAD_DIM, (h + 1) * HEAD_DIM)
        oh = o_ref[:, hs]
        oh = oh * lax.rsqrt(jnp.mean(oh * oh, axis=-1, keepdims=True) + EPS) * hgg_ref[:, hs]
        ya.append(oh * _silu(g_ref[:, hs]))
    y = jnp.concatenate(ya + [yb_ref[...]], axis=-1)
    x1 = x_ref[...] + jnp.dot(y, wout_ref[...], precision=HIGHEST, preferred_element_type=F32)
    x1_ref[...] = x1
    xn2 = _rms(x1, g2_ref[...])
    xn2_ref[...] = xn2.astype(BF16)
    logits = jnp.dot(xn2, wr_ref[...], precision=HIGHEST, preferred_element_type=F32) + br_ref[...]
    gates_ref[...] = _route(logits, n_groups, per_group)


def _whole(kernel, out_shape, *args, name):
    return pl.pallas_call(
        kernel, out_shape=out_shape,
        compiler_params=pltpu.CompilerParams(vmem_limit_bytes=VMEM_LIMIT_BYTES), name=name)(*args)


def _mixer_sample(x, s0, h0, c0, p, tb):
    n, d = x.shape
    wa_w = p["hgg"].shape[1]
    wb_w = p["cb"].shape[1]
    n_heads = wa_w // HEAD_DIM
    sd = lambda w: jax.ShapeDtypeStruct((n, w), F32)
    key_major = jax.ShapeDtypeStruct((wa_w, n), F32)
    q, f, v, g, yb, h_new, xr = _whole(
        _sample_in_kernel, (key_major,) * 2 + (sd(wa_w),) * 2 + (sd(wb_w),) * 3,
        x, c0[:, 0, :], c0[:, 1, :], c0[:, 2, :], h0, p["lbw"], p["g1"], p["win"], p["cw"], p["cb"],
        p["wx"], p["bx"], p["wa"], p["ba"], p["lam"], name="sample_in")
    tok = lambda w: pl.BlockSpec((tb, w), lambda i: (i, 0))
    st = pl.BlockSpec((tb, n_heads, HEAD_DIM, HEAD_DIM), lambda i: (i, 0, 0, 0))
    s_new, o = pl.pallas_call(
        _sample_state_kernel,
        grid=(n // tb,),
        in_specs=[st, _const_spec((wa_w, n)), _const_spec((wa_w, n)), tok(wa_w)],
        out_specs=(st, tok(wa_w)),
        out_shape=(jax.ShapeDtypeStruct(s0.shape, F32), sd(wa_w)),
        compiler_params=pltpu.CompilerParams(dimension_semantics=("arbitrary",),
                                             vmem_limit_bytes=VMEM_LIMIT_BYTES),
        name="sample_state",
    )(s0, q, f, v)
    x1, xn2, gates = _whole(
        functools.partial(_sample_out_kernel, n_groups=p["n_groups"], per_group=p["per_group"]),
        (sd(d), jax.ShapeDtypeStruct((n, d), BF16), sd(ROUTER_LANES)),
        x, o, g, yb, p["hgg"], p["wout"], p["g2"], p["wr"], p["br"], name="sample_out")
    c_new = jnp.stack([c0[:, 1, :], c0[:, 2, :], xr], axis=1)
    return x1, xn2, gates, s_new, h_new, c_new


def _block_diag(w):
    n, c, _ = w.shape
    eye = jnp.eye(n, dtype=w.dtype)
    return (w[:, :, None, :] * eye[:, None, :, None]).reshape(n * c, n * c)


def _prepare(lower_bounds, ln1_g, w_in, hgrn_norm_g, conv_w, conv_b, lru_wx, lru_bx, lru_wa, lru_ba,
             lru_lambda, w_out, ln2_g, router_wg, router_bg, router_we, router_be, exp_w1, exp_w3,
             exp_w2, final_g):
    d = w_in.shape[1]
    n_groups = router_wg.shape[-1]
    per_group = router_we.shape[-1]
    row = lambda a: a.reshape(1, -1).astype(F32)
    we = jnp.transpose(router_we[0], (1, 0, 2)).reshape(d, n_groups * per_group)
    pad = ROUTER_LANES - n_groups - n_groups * per_group
    wr = jnp.concatenate([router_wg[0], we, jnp.zeros((d, pad), F32)], axis=1)
    br = jnp.concatenate([router_bg[0], router_be[0].reshape(-1), jnp.zeros((pad,), F32)]).reshape(1, -1)
    rows_t = -(-(8 + n_groups * per_group) // 16) * 16
    wr_t = jnp.concatenate([router_wg[0].T, jnp.zeros((8 - n_groups, d), F32), we.T,
                            jnp.zeros((rows_t - 8 - n_groups * per_group, d), F32)], axis=0)
    br_t = jnp.concatenate([router_bg[0], jnp.zeros((8 - n_groups,), F32), router_be[0].reshape(-1),
                            jnp.zeros((rows_t - 8 - n_groups * per_group,), F32)]).reshape(-1, 1)
    wx = _block_diag(lru_wx[0])
    wa = _block_diag(lru_wa[0])
    return dict(
        wr_t_bf=wr_t.astype(BF16), br_t=br_t,
        n_groups=n_groups, per_group=per_group,
        lbw=lower_bounds.astype(F32), g1=row(ln1_g[0]), win=w_in[0], win_bf=w_in[0].astype(BF16),
        hgg=row(hgrn_norm_g[0]), cw=conv_w[0], cb=row(conv_b[0]),
        wx=wx, wx_bf=wx.astype(BF16), bx=row(lru_bx[0]), wa=wa, wa_bf=wa.astype(BF16), ba=row(lru_ba[0]),
        lam=row(lru_lambda[0]), wout=w_out[0], wout_bf=w_out[0].astype(BF16), g2=row(ln2_g[0]),
        wr=wr, br=br, w1_bf=exp_w1[0].astype(BF16), w3_bf=exp_w3[0].astype(BF16),
        w2=exp_w2[0], gf=row(final_g))


def kernel(x_prompt, x_sample, state_hgrn, state_rglru, state_conv, lower_bounds, ln1_g, w_in, hgrn_norm_g, conv_w, conv_b, lru_wx, lru_bx, lru_wa, lru_ba, lru_lambda, w_out, ln2_g, router_wg, router_bg, router_we, router_be, exp_w1, exp_w3, exp_w2, final_g):
    assert w_in.shape[0] == 1, "single-layer trunk"
    p = _prepare(lower_bounds, ln1_g, w_in, hgrn_norm_g, conv_w, conv_b, lru_wx, lru_bx, lru_wa, lru_ba,
                 lru_lambda, w_out, ln2_g, router_wg, router_bg, router_we, router_be, exp_w1, exp_w3,
                 exp_w2, final_g)
    bsz, seq, d = x_prompt.shape
    x1, xn2, gates, counts, s_p, h_p, c_p = _mixer_prompt(x_prompt, p, min(MIXER_BLOCK, seq))
    t = bsz * seq
    y_p = _moe_sorted(x1.reshape(t, d), xn2.reshape(t, d), gates.reshape(t, ROUTER_LANES), counts, p)

    n = x_sample.shape[0]
    x1s, xn2s, gates_s, s_s, h_s, c_s = _mixer_sample(x_sample[:, 0, :], state_hgrn[0], state_rglru[0],
                                                      state_conv[0], p, SAMPLE_STEP_TOKENS)
    y_s = _moe_dense(x1s, xn2s, gates_s, p, n)
    return (y_p.reshape(bsz, seq, d), y_s.reshape(n, 1, d),
            s_p[None], h_p.reshape(1, bsz, -1), c_p[None],
            s_s[None], h_s[None], c_s[None])
```

```python
import functools

import jax
import jax.numpy as jnp
from jax import lax
from jax.experimental import pallas as pl
from jax.experimental.pallas import tpu as pltpu

F32 = jnp.float32
BF16 = jnp.bfloat16
HIGHEST = lax.Precision.HIGHEST

EPS = 1e-6
LRU_C = 8.0
LOG2E = 1.4426950408889634
HEAD_DIM = 128
CHUNK = 64
SUB = 16
UNROLL = 8
ROUTER_LANES = 128
PROMPT_EXPERT_LANE = 8
MIXER_BLOCK = 512
SAMPLE_STEP_TOKENS = 8
TOK_TILE = 512
SEG = 16
MAX_GROUPS = 8
SORT_ROWS = TOK_TILE + MAX_GROUPS * SEG
ROW_TILE = 512
VMEM_LIMIT_BYTES = 56 * 1024 * 1024

NT_DIMS = (((1,), (1,)), ((), ()))
TN_DIMS = (((0,), (0,)), ((), ()))


def _rms(x, g):
    return x * lax.rsqrt(jnp.mean(x * x, axis=-1, keepdims=True) + EPS) * g


def _sigmoid(x):
    return 1.0 / (1.0 + jnp.exp(-x))


def _silu(x):
    return x * _sigmoid(x)


def _gelu_tanh(x):
    c = 0.7978845608028654
    return x * (0.5 * (1.0 + jnp.tanh(c * (x + 0.044715 * (x * x * x)))))


def _softplus(z):
    return jnp.maximum(z, 0.0) + jnp.log1p(jnp.exp(-jnp.abs(z)))


def _expm1(x):
    u = jnp.exp(x)
    um1 = u - 1.0
    small = um1 * x / jnp.log(u)
    return jnp.where(um1 == 0.0, x, jnp.where(jnp.abs(x) < 0.5, small, um1))


def _forget_lower_bound(lbw):
    m = jnp.max(lbw, axis=0, keepdims=True)
    e = jnp.exp(lbw - m)
    return e[0:1, :] / jnp.sum(e, axis=0, keepdims=True)


def _route(logits, n_groups, per_group):
    n = logits.shape[-1]
    col = lax.broadcasted_iota(jnp.int32, logits.shape, 1)
    neg = jnp.float32(-jnp.inf)
    big = jnp.int32(n)
    is_g = col < n_groups
    lg = jnp.where(is_g, logits, neg)
    mg = jnp.max(lg, axis=-1, keepdims=True)
    g_idx = jnp.min(jnp.where(lg == mg, col, big), axis=-1, keepdims=True)
    p_top = 1.0 / jnp.sum(jnp.where(is_g, jnp.exp(logits - mg), 0.0), axis=-1, keepdims=True)
    lo = n_groups + per_group * g_idx
    le = jnp.where((col >= lo) & (col < lo + per_group), logits, neg)
    m1 = jnp.max(le, axis=-1, keepdims=True)
    i1 = jnp.min(jnp.where(le == m1, col, big), axis=-1, keepdims=True)
    le2 = jnp.where(col == i1, neg, le)
    m2 = jnp.max(le2, axis=-1, keepdims=True)
    i2 = jnp.min(jnp.where(le2 == m2, col, big), axis=-1, keepdims=True)
    e2 = jnp.exp(m2 - m1)
    den = 1.0 + e2
    w1 = p_top / den
    w2 = p_top * (e2 / den)
    gates = jnp.where(col == i1, w1, 0.0) + jnp.where(col == i2, w2, 0.0)
    return gates + jnp.where(col == 0, g_idx.astype(F32), 0.0)


def _route_rows(logits_t, n_groups, per_group):
    assert n_groups <= 8 and per_group == 8
    n_tok = logits_t.shape[1]
    row = lax.broadcasted_iota(jnp.int32, (8, n_tok), 0)
    neg = jnp.float32(-jnp.inf)
    big = jnp.int32(8)
    lg = jnp.where(row < n_groups, logits_t[0:8], neg)
    mg = jnp.max(lg, axis=0, keepdims=True)
    g_idx = jnp.min(jnp.where(lg == mg, row, big), axis=0, keepdims=True)
    p_top = 1.0 / jnp.sum(jnp.where(row < n_groups, jnp.exp(lg - mg), 0.0), axis=0, keepdims=True)
    sel = logits_t[8:16]
    for g in range(1, n_groups):
        sel = jnp.where(g_idx == g, logits_t[8 + 8 * g:16 + 8 * g], sel)
    m1 = jnp.max(sel, axis=0, keepdims=True)
    i1 = jnp.min(jnp.where(sel == m1, row, big), axis=0, keepdims=True)
    sel2 = jnp.where(row == i1, neg, sel)
    m2 = jnp.max(sel2, axis=0, keepdims=True)
    i2 = jnp.min(jnp.where(sel2 == m2, row, big), axis=0, keepdims=True)
    e2 = jnp.exp(m2 - m1)
    den = 1.0 + e2
    w1 = p_top / den
    w2 = p_top * (e2 / den)
    own = jnp.where(row == i1, w1, 0.0) + jnp.where(row == i2, w2, 0.0)
    blocks = [jnp.where(row == 0, g_idx.astype(F32), 0.0)]
    blocks += [jnp.where(g_idx == g, own, 0.0) for g in range(n_groups)]
    blocks += [jnp.zeros((8, n_tok), F32)] * (ROUTER_LANES // 8 - len(blocks))
    return jnp.concatenate(blocks, axis=0), g_idx


def _mixer_prompt_kernel(x_ref, lbw_ref, g1_ref, win_ref, hgg_ref, cw_ref, cb_ref, wx_ref, bx_ref,
                         wa_ref, ba_ref, lam_ref, wout_ref, g2_ref, wr_ref, br_ref,
                         x1_ref, xn2_ref, gates_ref, cnt_ref, sout_ref, hout_ref, cout_ref,
                         proj_s, k_s, b_s, o_s, st_s, xr_s, a_s, u_s, hcar_s, yb_out_s, win_s, wout_s,
                         *, n_groups, per_group):
    lb_t = x_ref.shape[1]
    wa_w = o_s.shape[1]
    wb_w = a_s.shape[1]
    n_heads = wa_w // HEAD_DIM
    j = pl.program_id(1)
    nj = pl.num_programs(1)

    @pl.when((pl.program_id(0) == 0) & (j == 0))
    def _():
        for c in range(0, win_ref.shape[1], wa_w):
            win_s[:, c:c + wa_w] = win_ref[:, c:c + wa_w].astype(BF16)
        wout_s[...] = wout_ref[...].astype(BF16)

    @pl.when(j == 0)
    def _():
        st_s[...] = jnp.zeros_like(st_s)
        hcar_s[...] = jnp.zeros_like(hcar_s)
        xr_s[0:8, :] = jnp.zeros((8, wb_w), F32)

    x = x_ref[0]
    xn = _rms(x, g1_ref[...]).astype(BF16)
    xb0 = 4 * wa_w

    def project(pieces):
        for c, w in pieces:
            proj_s[:, c:c + w] = jnp.dot(xn, win_s[:, c:c + w], preferred_element_type=F32)

    project(((wa_w, wa_w), (xb0, wb_w), (xb0 + wb_w, wb_w), (0, wa_w), (2 * wa_w, wa_w), (3 * wa_w, wa_w)))

    xr_s[pl.ds(8, lb_t), :] = proj_s[:, xb0:xb0 + wb_w]
    xc = (cb_ref[...] + cw_ref[3:4, :] * xr_s[pl.ds(8, lb_t), :] + cw_ref[2:3, :] * xr_s[pl.ds(7, lb_t), :]
          + cw_ref[1:2, :] * xr_s[pl.ds(6, lb_t), :] + cw_ref[0:1, :] * xr_s[pl.ds(5, lb_t), :])
    tail = xr_s[pl.ds(lb_t + 5, 3), :]
    xr_s[5:8, :] = tail
    cout_ref[0] = tail
    xcb = xc.astype(BF16)
    gate_x = _sigmoid(jnp.dot(xcb, wx_ref[...], preferred_element_type=F32) + bx_ref[...])
    gate_a = _sigmoid(jnp.dot(xcb, wa_ref[...], preferred_element_type=F32) + ba_ref[...])
    log_a =(-LRU_C) * gate_a * _softplus(-lam_ref[...])
    a = jnp.exp(log_a)
    mult = jnp.sqrt((1.0 - a) * (1.0 + a))
    first =(lax.broadcasted_iota(jnp.int32, (lb_t, 1), 0) == 0) & (j == 0)
    a = jnp.where(first, 0.0, a)
    mult = jnp.where(first, 1.0, mult)
    a_s[...] = a
    u_s[...] = gate_x * xc * mult

    lb = _forget_lower_bound(lbw_ref[...])
    f = lb + (1.0 - lb) * _sigmoid(proj_s[:, wa_w:2 * wa_w])
    for h in range(n_heads):
        k_s[h] = 1.0 - f[:, h * HEAD_DIM:(h + 1) * HEAD_DIM]
    logf = jnp.log(f)
    r_i = lax.broadcasted_iota(jnp.int32, (CHUNK, CHUNK), 0)
    c_i = lax.broadcasted_iota(jnp.int32, (CHUNK, CHUNK), 1)
    tri = jnp.where(r_i >= c_i, 1.0, 0.0).astype(BF16)
    lf_hi = logf.astype(BF16)
    rest = logf - lf_hi.astype(F32)
    lf_mid = rest.astype(BF16)
    lf_lo = (rest - lf_mid.astype(F32)).astype(BF16)
    for c in range(0, lb_t, CHUNK):
        cum = [jnp.dot(tri, part[c:c + CHUNK, :], preferred_element_type=F32) for part in (lf_lo, lf_mid, lf_hi)]
        b_all = LOG2E * ((cum[0] + cum[1]) + cum[2])
        for h in range(n_heads):
            b_s[h, c:c + CHUNK, :] = b_all[:, h * HEAD_DIM:(h + 1) * HEAD_DIM]

    row_sub = lax.broadcasted_iota(jnp.int32, (SUB, HEAD_DIM), 0)
    lane_sub = lax.broadcasted_iota(jnp.int32, (SUB, HEAD_DIM), 1)
    assert n_heads % 2 == 0
    r_kk = lax.broadcasted_iota(jnp.int32, (2 * HEAD_DIM, 2 * HEAD_DIM), 0)
    c_kk = lax.broadcasted_iota(jnp.int32, (2 * HEAD_DIM, 2 * HEAD_DIM), 1)
    ones_kk = jnp.where((r_kk < HEAD_DIM) == (c_kk < HEAD_DIM), 1.0, 0.0).astype(BF16)

    n_sub = CHUNK // SUB
    half = SUB // 2
    lower_left = (row_sub >= half) & (lane_sub < half)
    own_lane = jnp.where(row_sub >= half, half, 0)

    def chunk_start(r0):
        first = []
        for h in range(n_heads):
            hs = slice(h * HEAD_DIM, (h + 1) * HEAD_DIM)
            q = proj_s[pl.ds(r0, CHUNK), hs]
            b = b_s[h, pl.ds(r0, CHUNK), :]
            v = proj_s[pl.ds(r0, CHUNK), 2 * wa_w + h * HEAD_DIM:2 * wa_w + (h + 1) * HEAD_DIM]
            k = k_s[h, pl.ds(r0, CHUNK), :]

            def key_rows(ref, lo, j):
                return jnp.stack([jnp.broadcast_to(ref[h, pl.ds(r0 + lo + hf * half + j, 1), :], (half, HEAD_DIM))
                                  for hf in range(2)])
            vb = v.astype(BF16)
            st = st_s[h]
            b_last = b[CHUNK - 1:CHUNK, :]
            o = lax.dot_general((q * jnp.exp2(b)).astype(BF16), st.astype(BF16), NT_DIMS,
                                preferred_element_type=F32)
            k_end = k * jnp.exp2(b_last - b)
            st_s[h] = st * jnp.exp2(b_last) + lax.dot_general(vb, k_end.astype(BF16), TN_DIMS,
                                                               preferred_element_type=F32)
            terms, off, mid = [], [], []
            for i in range(n_sub):
                lo = i * SUB
                qi, bi, ki = q[lo:lo + SUB], b[lo:lo + SUB], k[lo:lo + SUB]
                q3, b3 = (a.reshape(2, half, HEAD_DIM) for a in (qi, bi))
                terms += [(q3 * (key_rows(k_s, lo, j) * jnp.exp2(b3 - key_rows(b_s, lo, j))))
                          .reshape(SUB, HEAD_DIM).astype(BF16) for j in range(half)]
                rm = bi[half - 1:half]
                mid.append(lax.dot_general((qi * jnp.exp2(bi - rm)).astype(BF16),
                                           (ki * jnp.exp2(rm - bi)).astype(BF16), NT_DIMS,
                                           preferred_element_type=F32))
                if i > 0:
                    r = b[lo - 1:lo]
                    qt = (qi * jnp.exp2(bi - r)).astype(BF16)
                    kt = (k[:lo] * jnp.exp2(r - b[:lo])).astype(BF16)
                    off.append(lax.dot_general(qt, kt, NT_DIMS, preferred_element_type=F32))
            first.append([o, vb, off, jnp.concatenate(terms, axis=0), mid])
        for h in range(0, n_heads, 2):
            both = jnp.concatenate([first[h][3], first[h + 1][3]], axis=1)
            sums = jnp.dot(both, ones_kk, preferred_element_type=F32)
            first[h][3] = sums[:, :HEAD_DIM]
            first[h + 1][3] = sums[:, HEAD_DIM:]
        return first

    def chunk_finish(r0, first):
        for h in range(n_heads):
            o, vb, off, sums, mid = first[h]
            outs = []
            for i in range(n_sub):
                lo = i * SUB
                sc = jnp.zeros((SUB, HEAD_DIM), F32)
                for j in range(half):
                    row0 = (i * half + j) * SUB
                    sc = jnp.where(lane_sub == own_lane + j, sums[row0:row0 + SUB], sc)
                sc = jnp.where(row_sub >= lane_sub, sc, 0.0)[:, :SUB]
                sc = jnp.where(lower_left[:, :SUB], mid[i], sc)
                od = o[lo:lo + SUB] + jnp.dot(sc.astype(BF16), vb[lo:lo + SUB], preferred_element_type=F32)
                if i > 0:
                    od = od + jnp.dot(off[i - 1].astype(BF16), vb[:lo], preferred_element_type=F32)
                outs.append(od)
            o_s[pl.ds(r0, CHUNK), h * HEAD_DIM:(h + 1) * HEAD_DIM] = jnp.concatenate(outs, axis=0)

    def chunks_body(ci, carry):
        rows = [pl.multiple_of((ci * UNROLL + u) * CHUNK, CHUNK) for u in range(UNROLL)]
        started = [chunk_start(r0) for r0 in rows]
        for r0, first in zip(rows, started):
            chunk_finish(r0, first)
        return carry

    row8 = lax.broadcasted_iota(jnp.int32, (8, 1), 0)

    def scan_body(gi, carry):
        r0 = pl.multiple_of(gi * 8, 8)
        aa = a_s[pl.ds(r0, 8), :]
        uu = u_s[pl.ds(r0, 8), :]
        for s in (1, 2, 4):
            m = row8 >= s
            uu = jnp.where(m, aa * pltpu.roll(uu, s, 0) + uu, uu)
            aa = jnp.where(m, aa * pltpu.roll(aa, s, 0), aa)
        hh = aa * carry + uu
        u_s[pl.ds(r0, 8), :] = hh
        return hh[7:8, :]

    h_last = lax.fori_loop(0, lb_t // 8, scan_body, hcar_s[...])
    hcar_s[...] = h_last
    hout_ref[0] = h_last
    yb = (u_s[...] * _gelu_tanh(proj_s[:, xb0 + wb_w:xb0 + 2 * wb_w])).astype(BF16)
    yb_out_s[...] = jnp.dot(yb, wout_s[wa_w:, :], preferred_element_type=F32)

    assert lb_t % (CHUNK * UNROLL) == 0
    lax.fori_loop(0, lb_t // (CHUNK * UNROLL), chunks_body, 0)

    ya = []
    for h in range(n_heads):
        hs = slice(h * HEAD_DIM, (h + 1) * HEAD_DIM)
        oh = o_s[:, hs]
        oh = oh * lax.rsqrt(jnp.mean(oh * oh, axis=-1, keepdims=True) + EPS) * hgg_ref[:, hs]
        ya.append(oh * _silu(proj_s[:, 3 * wa_w + h * HEAD_DIM:3 * wa_w + (h + 1) * HEAD_DIM]))

    ya = jnp.concatenate(ya, axis=-1).astype(BF16)
    x1 = x + (jnp.dot(ya, wout_s[:wa_w, :], preferred_element_type=F32) + yb_out_s[...])
    x1_ref[0] = x1
    xn2 = _rms(x1, g2_ref[...]).astype(BF16)
    xn2_ref[0] = xn2
    logits_t = lax.dot_general(wr_ref[...], xn2, NT_DIMS, preferred_element_type=F32) + br_ref[...]
    gates_t, g_idx = _route_rows(logits_t, n_groups, per_group)
    gates_ref[0] = gates_t.T
    lane_c = lax.broadcasted_iota(jnp.int32, (1, ROUTER_LANES), 1)
    for t in range(lb_t // TOK_TILE):
        gi = g_idx[:, t * TOK_TILE:(t + 1) * TOK_TILE]
        cnt = jnp.zeros((1, ROUTER_LANES), jnp.int32)
        for g in range(n_groups):
            cnt = cnt + jnp.where(lane_c == g, jnp.sum(jnp.where(gi == g, 1, 0), axis=1, keepdims=True), 0)
        cnt_ref[0, t:t + 1, :] = cnt

    @pl.when(j == nj - 1)
    def _():
        for h in range(n_heads):
            sout_ref[0, h] = st_s[h].T


def _const_spec(shape):
    nd = len(shape)
    return pl.BlockSpec(shape, lambda *_: (0,) * nd)


def _mixer_prompt(x, p, lb_t):
    bsz, seq, d = x.shape
    wa_w = p["hgg"].shape[1]
    wb_w = p["cb"].shape[1]
    n_heads = wa_w // HEAD_DIM
    n_cols = p["win"].shape[1]
    weights = [p["lbw"], p["g1"], p["win"], p["hgg"], p["cw"], p["cb"], p["wx_bf"], p["bx"],
               p["wa_bf"], p["ba"], p["lam"], p["wout"], p["g2"], p["wr_t_bf"], p["br_t"]]
    cast_in_kernel = (p["win"], p["wout"])
    nj = seq // lb_t
    tile = lambda w: pl.BlockSpec((1, lb_t, w), lambda b, j: (b, j, 0))
    out_shape = (
        jax.ShapeDtypeStruct((bsz, seq, d), F32),
        jax.ShapeDtypeStruct((bsz, seq, d), BF16),
        jax.ShapeDtypeStruct((bsz, seq, ROUTER_LANES), F32),
        jax.ShapeDtypeStruct((bsz * (seq // lb_t), lb_t // TOK_TILE, ROUTER_LANES), jnp.int32),
        jax.ShapeDtypeStruct((bsz, n_heads, HEAD_DIM, HEAD_DIM), F32),
        jax.ShapeDtypeStruct((bsz, 1, wb_w), F32),
        jax.ShapeDtypeStruct((bsz, 3, wb_w), F32),
    )
    out_specs = (
        tile(d), tile(d), tile(ROUTER_LANES),
        pl.BlockSpec((1, lb_t // TOK_TILE, ROUTER_LANES), lambda b, j: (b * nj + j, 0, 0)),
        pl.BlockSpec((1, n_heads, HEAD_DIM, HEAD_DIM), lambda b, j: (b, 0, 0, 0)),
        pl.BlockSpec((1, 1, wb_w), lambda b, j: (b, 0, 0)),
        pl.BlockSpec((1, 3, wb_w), lambda b, j: (b, 0, 0)),
    )
    scratch = [
        pltpu.VMEM((lb_t, n_cols), F32),
        pltpu.VMEM((n_heads, lb_t, HEAD_DIM), F32),
        pltpu.VMEM((n_heads, lb_t, HEAD_DIM), F32),
        pltpu.VMEM((lb_t, wa_w), F32),
        pltpu.VMEM((n_heads, HEAD_DIM, HEAD_DIM), F32),
        pltpu.VMEM((lb_t + 8, wb_w), F32),
        pltpu.VMEM((lb_t, wb_w), F32),
        pltpu.VMEM((lb_t, wb_w), F32),
        pltpu.VMEM((1, wb_w), F32),
        pltpu.VMEM((lb_t, d), F32),
        pltpu.VMEM((d, n_cols), BF16),
        pltpu.VMEM(p["wout"].shape, BF16),
    ]
    weight_specs = [pl.BlockSpec(w.shape, lambda *_, nd=w.ndim: (0,) * nd, pipeline_mode=pl.Buffered(1))
                    if any(w is c for c in cast_in_kernel) else _const_spec(w.shape) for w in weights]
    kern = functools.partial(_mixer_prompt_kernel, n_groups=p["n_groups"], per_group=p["per_group"])
    return pl.pallas_call(
        kern,
        grid=(bsz, nj),
        in_specs=[tile(d)] + weight_specs,
        out_specs=out_specs,
        out_shape=out_shape,
        scratch_shapes=scratch,
        compiler_params=pltpu.CompilerParams(dimension_semantics=("arbitrary", "arbitrary"),
                                             vmem_limit_bytes=VMEM_LIMIT_BYTES),
        name="mixer_prompt",
    )(x, *weights)


def _moe_kernel(x1_ref, xn2_ref, gates_ref, w1_ref, w3_ref, w2_ref, gf_ref, y_ref, acc_s, *, n_groups):
    e = pl.program_id(1)

    @pl.when(e == 0)
    def _():
        acc_s[...] = jnp.zeros_like(acc_s)

    xb = xn2_ref[...]
    per_step = w1_ref.shape[0]
    gates = pltpu.roll(gates_ref[...], ROUTER_LANES - n_groups - per_step * e, 1)
    acc = acc_s[...]
    for s in range(per_step):
        h = (_silu(jnp.dot(xb, w1_ref[s], preferred_element_type=F32))
             * jnp.dot(xb, w3_ref[s], preferred_element_type=F32) * gates[:, s:s + 1])
        acc = acc + jnp.dot(h.astype(BF16), w2_ref[s].astype(BF16), preferred_element_type=F32)
    acc_s[...] = acc

    @pl.when(e == pl.num_programs(1) - 1)
    def _():
        y_ref[...] = _rms(x1_ref[...] + acc_s[...], gf_ref[...])


def _moe_dense(x1, xn2, gates, p, tm):
    t, d = x1.shape
    n_exp, _, d_exp = p["w1_bf"].shape
    per_step = p["per_group"]
    row = lambda w: pl.BlockSpec((tm, w), lambda i, e: (i, 0))
    return pl.pallas_call(
        functools.partial(_moe_kernel, n_groups=p["n_groups"]),
        grid=(t // tm, n_exp // per_step),
        in_specs=[row(d), row(d), row(ROUTER_LANES),
                  pl.BlockSpec((per_step, d, d_exp), lambda i, e: (e, 0, 0)),
                  pl.BlockSpec((per_step, d, d_exp), lambda i, e: (e, 0, 0)),
                  pl.BlockSpec((per_step, d_exp, d), lambda i, e: (e, 0, 0)),
                  _const_spec(p["gf"].shape)],
        out_specs=row(d),
        out_shape=jax.ShapeDtypeStruct((t, d), F32),
        scratch_shapes=[pltpu.VMEM((tm, d), F32)],
        compiler_params=pltpu.CompilerParams(dimension_semantics=("arbitrary", "arbitrary"),
                                             vmem_limit_bytes=VMEM_LIMIT_BYTES),
        name="moe_dense",
    )(x1, xn2, gates, p["w1_bf"], p["w3_bf"], p["w2"], p["gf"])


def _seg_pad(n):
    return jnp.bitwise_and(n + (SEG - 1), -SEG)


def _row_tile_pad(n):
    return jnp.bitwise_and(n + (ROW_TILE - 1), -ROW_TILE)


def _tile_segments(cnt_ref, tile, n_groups):
    padded = [_seg_pad(cnt_ref[tile * n_groups + g]) for g in range(n_groups)]
    starts, acc = [], jnp.int32(0)
    for g in range(n_groups):
        starts.append(acc)
        acc = acc + padded[g]
    return padded, starts, acc


def _group_bases(cnt_ref, n_tiles, n_groups):
    def body(t, tot):
        return tuple(tot[g] + _seg_pad(cnt_ref[t * n_groups + g]) for g in range(n_groups))
    totals = lax.fori_loop(0, n_tiles, body, (jnp.int32(0),) * n_groups)
    bases, ends, acc = [], [], jnp.int32(0)
    for g in range(n_groups):
        bases.append(acc)
        acc = acc + _row_tile_pad(totals[g])
        ends.append(acc)
    return bases, ends, totals


def _chunk_rows(n_chunks, starts, gstart):
    rows, row = [], None
    for c in range(n_chunks):
        row = gstart[0] if c == 0 else row + SEG
        for g in range(1, len(starts)):
            row = jnp.where(starts[g] == c * SEG, gstart[g], row)
        rows.append(pl.multiple_of(row, SEG))
    return rows


def _sort_matrix(gates, starts):
    n_groups = len(starts)
    col = lax.broadcasted_iota(jnp.int32, (TOK_TILE, ROUTER_LANES), 1)
    gi = gates[:, 0:1].astype(jnp.int32)
    onehot = col == gi
    r_i = lax.broadcasted_iota(jnp.int32, (TOK_TILE, TOK_TILE), 0)
    c_i = lax.broadcasted_iota(jnp.int32, (TOK_TILE, TOK_TILE), 1)
    earlier = jnp.where(r_i > c_i, 1.0, 0.0).astype(BF16)
    before = jnp.dot(earlier, jnp.where(onehot, 1.0, 0.0).astype(BF16), preferred_element_type=F32)
    rank = jnp.sum(jnp.where(onehot, before, 0.0), axis=-1, keepdims=True).astype(jnp.int32)
    base = jnp.zeros((TOK_TILE, 1), jnp.int32)
    for g in range(n_groups):
        base = base + jnp.where(gi == g, starts[g], 0)
    lane = lax.broadcasted_iota(jnp.int32, (TOK_TILE, SORT_ROWS), 1)
    return lane == base + rank


def _sort_matrix_rows(gates, starts):
    n_groups = len(starts)
    col = lax.broadcasted_iota(jnp.int32, (TOK_TILE, ROUTER_LANES), 1)
    onehot_t = jnp.where(col == gates[:, 0:1].astype(jnp.int32), 1.0, 0.0).T
    r_i = lax.broadcasted_iota(jnp.int32, (TOK_TILE, TOK_TILE), 0)
    c_i = lax.broadcasted_iota(jnp.int32, (TOK_TILE, TOK_TILE), 1)
    later = jnp.where(r_i < c_i, 1.0, 0.0).astype(BF16)
    before = jnp.dot(onehot_t.astype(BF16), later, preferred_element_type=F32)
    g_row = lax.broadcasted_iota(jnp.int32, (ROUTER_LANES, 1), 0)
    base = jnp.zeros((ROUTER_LANES, 1), jnp.int32)
    for g in range(n_groups):
        base = base + jnp.where(g_row == g, starts[g], 0)
    dest = jnp.sum(jnp.where(onehot_t > 0.0, before + base.astype(F32), 0.0), axis=0, keepdims=True)
    row = lax.broadcasted_iota(jnp.int32, (SORT_ROWS, TOK_TILE), 0)
    return row == dest.astype(jnp.int32)


def _dispatch_copies(xbuf, gbuf, xs_hbm, gs_hbm, sem, slot, c, row):
    return (pltpu.make_async_copy(xbuf.at[slot, pl.ds(c * SEG, SEG), :], xs_hbm.at[pl.ds(row, SEG), :], sem.at[slot]),
            pltpu.make_async_copy(gbuf.at[slot, pl.ds(c * SEG, SEG), :], gs_hbm.at[pl.ds(row, SEG), :], sem.at[slot]))


def _moe_dispatch_kernel(cnt_ref, xn2_ref, gates_ref, xs_hbm, gs_hbm, tg_ref,
                         xbuf, gbuf, sem, gstart_s, zx, zg, zsem, *, n_groups, per_group):
    j = pl.program_id(0)
    n_tiles = pl.num_programs(0)
    n_chunks = (TOK_TILE + n_groups * SEG) // SEG
    slot = lax.rem(j, 2)

    @pl.when(j == 0)
    def _():
        bases, ends, totals = _group_bases(cnt_ref, n_tiles, n_groups)
        for g in range(n_groups):
            gstart_s[g] = bases[g]
        n_rt = tg_ref.shape[0] - 1
        for i in range(n_rt):
            tg = jnp.int32(0)
            for g in range(n_groups):
                tg = tg + jnp.where(ends[g] <= i * ROW_TILE, 1, 0)
            tg_ref[i] = tg
        n_used = ends[n_groups - 1] // ROW_TILE
        tg_ref[n_rt] = n_used

        zx[...] = jnp.zeros_like(zx)
        zg[...] = jnp.zeros_like(zg)

        def zero_copies(row, n):
            return (pltpu.make_async_copy(zx.at[pl.ds(0, n), :], xs_hbm.at[pl.ds(row, n), :], zsem.at[0]),
                    pltpu.make_async_copy(zg.at[pl.ds(0, n), :], gs_hbm.at[pl.ds(row, n), :], zsem.at[0]))

        def for_each_gap(act):
            for g in range(n_groups):
                tail = bases[g] + totals[g]

                def seg_body(k, c, tail=tail):
                    for cp in zero_copies(pl.multiple_of(tail + k * SEG, SEG), SEG):
                        act(cp)
                    return c
                lax.fori_loop(0, (ends[g] - tail) // SEG, seg_body, 0)

            def tile_body(k, c):
                for cp in zero_copies(pl.multiple_of((n_used + k) * ROW_TILE, ROW_TILE), ROW_TILE):
                    act(cp)
                return c
            lax.fori_loop(0, n_rt - n_used, tile_body, 0)

        for_each_gap(lambda cp: cp.start())
        for_each_gap(lambda cp: cp.wait())

    def wait_tile(tile, slot_):
        _, _, used = _tile_segments(cnt_ref, tile, n_groups)
        for c in range(n_chunks):
            @pl.when(c * SEG < used)
            def _():
                for cp in _dispatch_copies(xbuf, gbuf, xs_hbm, gs_hbm, sem, slot_, c, 0):
                    cp.wait()

    @pl.when(j >= 2)
    def _():
        wait_tile(j - 2, slot)

    padded, starts, used = _tile_segments(cnt_ref, j, n_groups)
    gates = gates_ref[...]
    d = xn2_ref.shape[1]
    g_hi = gates.astype(BF16)
    g_lo = (gates - g_hi.astype(F32)).astype(BF16)
    sort_m = jnp.where(_sort_matrix_rows(gates, starts), 1.0, 0.0).astype(BF16)
    moved = jnp.dot(sort_m, jnp.concatenate([xn2_ref[...], g_hi, g_lo], axis=1), preferred_element_type=F32)
    xbuf[slot] = moved[:, :d].astype(BF16)
    gbuf[slot] = moved[:, d:d + ROUTER_LANES] + moved[:, d + ROUTER_LANES:]
    gstart = [gstart_s[g] for g in range(n_groups)]
    rows = _chunk_rows(n_chunks, starts, gstart)
    for c in range(n_chunks):
        @pl.when(c * SEG < used)
        def _():
            for cp in _dispatch_copies(xbuf, gbuf, xs_hbm, gs_hbm, sem, slot, c, rows[c]):
                cp.start()
    for g in range(n_groups):
        gstart_s[g] = gstart[g] + padded[g]

    @pl.when(j == n_tiles - 1)
    def _():
        @pl.when(j >= 1)
        def _():
            wait_tile(j - 1, 1 - slot)
        wait_tile(j, slot)


def _moe_group_kernel(tg_ref, xs_ref, gs_ref, w1_ref, w3_ref, w2_ref, ys_ref, w2_s, *, n_groups, per_group):
    i = pl.program_id(0)

    @pl.when((i == 0) | (tg_ref[i] != tg_ref[jnp.maximum(i - 1, 0)]))
    def _():
        w2_s[...] = w2_ref[...].astype(BF16)

    @pl.when(tg_ref[i] < n_groups)
    def _():
        xb = xs_ref[...]
        gates = pltpu.roll(gs_ref[...], ROUTER_LANES - PROMPT_EXPERT_LANE - per_group * tg_ref[i], 1)
        acc = jnp.zeros(ys_ref.shape, F32)
        for s in range(per_group):
            h = (_silu(jnp.dot(xb, w1_ref[s], preferred_element_type=F32))
                 * jnp.dot(xb, w3_ref[s], preferred_element_type=F32) * gates[:, s:s + 1])
            acc = acc + jnp.dot(h.astype(BF16), w2_s[s], preferred_element_type=F32)
        ys_ref[...] = acc.astype(BF16)


def _moe_combine_kernel(cnt_ref, gates_ref, x1_ref, gf_ref, ys_hbm, y_ref, ybuf, sem, gstart_s, *, n_groups):
    j = pl.program_id(0)
    n_tiles = pl.num_programs(0)
    n_chunks = SORT_ROWS // SEG
    slot = lax.rem(j, 2)

    def copies(slot_, c, row):
        return pltpu.make_async_copy(ys_hbm.at[pl.ds(row, SEG), :], ybuf.at[slot_, pl.ds(c * SEG, SEG), :],
                                     sem.at[slot_])

    def fetch_tile(tile, slot_):
        padded, starts, _ = _tile_segments(cnt_ref, tile, n_groups)
        gstart = [gstart_s[g] for g in range(n_groups)]
        for c, row in enumerate(_chunk_rows(n_chunks, starts, gstart)):
            copies(slot_, c, row).start()
        for g in range(n_groups):
            gstart_s[g] = gstart[g] + padded[g]

    @pl.when(j == 0)
    def _():
        bases, _, _ = _group_bases(cnt_ref, n_tiles, n_groups)
        for g in range(n_groups):
            gstart_s[g] = bases[g]
        fetch_tile(0, 0)

    @pl.when(j + 1 < n_tiles)
    def _():
        fetch_tile(j + 1, 1 - slot)

    for c in range(n_chunks):
        copies(slot, c, 0).wait()
    _, starts, _ = _tile_segments(cnt_ref, j, n_groups)
    sort_t = _sort_matrix(gates_ref[...], starts)
    moe = jnp.dot(jnp.where(sort_t, 1.0, 0.0).astype(BF16), ybuf[slot], preferred_element_type=F32)
    y_ref[...] = _rms(x1_ref[...] + moe, gf_ref[...])


def _moe_sorted(x1, xn2, gates, counts, p):
    t, d = x1.shape
    n_groups, per_group = p["n_groups"], p["per_group"]
    assert n_groups <= MAX_GROUPS and t % TOK_TILE == 0
    n_tiles = t // TOK_TILE
    d_exp = p["w1_bf"].shape[2]
    cap = t + n_tiles * n_groups * SEG + n_groups * ROW_TILE + SORT_ROWS
    n_rt = -(-cap // ROW_TILE)
    rows = n_rt * ROW_TILE
    cnt = counts[:, :, :n_groups].reshape(-1)
    params = pltpu.CompilerParams(dimension_semantics=("arbitrary",), vmem_limit_bytes=VMEM_LIMIT_BYTES)
    any_spec = pl.BlockSpec(memory_space=pl.ANY)
    tok = lambda w: pl.BlockSpec((TOK_TILE, w), lambda j, c: (j, 0))

    xs, gs, tile_group = pl.pallas_call(
        functools.partial(_moe_dispatch_kernel, n_groups=n_groups, per_group=per_group),
        grid_spec=pltpu.PrefetchScalarGridSpec(
            num_scalar_prefetch=1, grid=(n_tiles,),
            in_specs=[tok(d), tok(ROUTER_LANES)],
            out_specs=(any_spec, any_spec, pl.BlockSpec(memory_space=pltpu.SMEM)),
            scratch_shapes=[pltpu.VMEM((2, SORT_ROWS, d), BF16), pltpu.VMEM((2, SORT_ROWS, ROUTER_LANES), F32),
                            pltpu.SemaphoreType.DMA((2,)), pltpu.SMEM((n_groups,), jnp.int32),
                            pltpu.VMEM((ROW_TILE, d), BF16), pltpu.VMEM((ROW_TILE, ROUTER_LANES), F32),
                            pltpu.SemaphoreType.DMA((1,))]),
        out_shape=(jax.ShapeDtypeStruct((rows, d), BF16), jax.ShapeDtypeStruct((rows, ROUTER_LANES), F32),
                   jax.ShapeDtypeStruct((n_rt + 1,), jnp.int32)),
        compiler_params=params, name="moe_dispatch",
    )(cnt, xn2, gates)

    used_tile = lambda i, tg: (jnp.minimum(i, tg[n_rt] - 1), 0)
    group_w = lambda i, tg: (jnp.minimum(tg[i], n_groups - 1), 0, 0)
    ys = pl.pallas_call(
        functools.partial(_moe_group_kernel, n_groups=n_groups, per_group=per_group),
        grid_spec=pltpu.PrefetchScalarGridSpec(
            num_scalar_prefetch=1, grid=(n_rt,),
            in_specs=[pl.BlockSpec((ROW_TILE, d), used_tile), pl.BlockSpec((ROW_TILE, ROUTER_LANES), used_tile),
                      pl.BlockSpec((per_group, d, d_exp), group_w), pl.BlockSpec((per_group, d, d_exp), group_w),
                      pl.BlockSpec((per_group, d_exp, d), group_w)],
            out_specs=pl.BlockSpec((ROW_TILE, d), used_tile),
            scratch_shapes=[pltpu.VMEM((per_group, d_exp, d), BF16)]),
        out_shape=jax.ShapeDtypeStruct((rows, d), BF16),
        input_output_aliases={1: 0},
        compiler_params=params, name="moe_experts",
    )(tile_group, xs, gs, p["w1_bf"], p["w3_bf"], p["w2"])

    return pl.pallas_call(
        functools.partial(_moe_combine_kernel, n_groups=n_groups),
        grid_spec=pltpu.PrefetchScalarGridSpec(
            num_scalar_prefetch=1, grid=(n_tiles,),
            in_specs=[tok(ROUTER_LANES), tok(d), pl.BlockSpec(p["gf"].shape, lambda j, c: (0, 0)), any_spec],
            out_specs=tok(d),
            scratch_shapes=[pltpu.VMEM((2, SORT_ROWS, d), BF16), pltpu.SemaphoreType.DMA((2,)),
                            pltpu.SMEM((n_groups,), jnp.int32)]),
        out_shape=jax.ShapeDtypeStruct((t, d), F32),
        compiler_params=params, name="moe_combine",
    )(cnt, gates, x1, p["gf"], ys)


def _sample_in_kernel(x_ref, c0_ref, c1_ref, c2_ref, h0_ref, lbw_ref, g1_ref, win_ref, cw_ref, cb_ref,
                      wx_ref, bx_ref, wa_ref, ba_ref, lam_ref,
                      q_ref, f_ref, v_ref, g_ref, yb_ref, hnew_ref, xr_ref):
    wa_w = v_ref.shape[1]
    wb_w = yb_ref.shape[1]
    xn = _rms(x_ref[...], g1_ref[...])
    proj = jnp.dot(xn, win_ref[...], precision=HIGHEST, preferred_element_type=F32)
    lb = _forget_lower_bound(lbw_ref[...])
    f = lb + (1.0 - lb) * _sigmoid(proj[:, wa_w:2 * wa_w])
    q_ref[...] = proj[:, 0:wa_w].T
    f_ref[...] = f.T
    v_ref[...] = proj[:, 2 * wa_w:3 * wa_w]
    g_ref[...] = proj[:, 3 * wa_w:4 * wa_w]
    xr = proj[:, 4 * wa_w:4 * wa_w + wb_w]
    xr_ref[...] = xr
    xc = (cb_ref[...] + cw_ref[0:1, :] * c0_ref[...] + cw_ref[1:2, :] * c1_ref[...]
          + cw_ref[2:3, :] * c2_ref[...] + cw_ref[3:4, :] * xr)
    gate_x = _sigmoid(jnp.dot(xc, wx_ref[...], precision=HIGHEST, preferred_element_type=F32) + bx_ref[...])
    gate_a = _sigmoid(jnp.dot(xc, wa_ref[...], precision=HIGHEST, preferred_element_type=F32) + ba_ref[...])
    log_a = (-LRU_C) * gate_a * _softplus(-lam_ref[...])
    a = jnp.exp(log_a)
    mult = jnp.sqrt(-_expm1(2.0 * log_a))
    h = a * h0_ref[...] + gate_x * xc * mult
    hnew_ref[...] = h
    yb_ref[...] = h * _gelu_tanh(proj[:, 4 * wa_w + wb_w:4 * wa_w + 2 * wb_w])


def _sample_state_kernel(s_ref, qt_ref, ft_ref, v_ref, snew_ref, o_ref):
    tb, n_heads = s_ref.shape[0], s_ref.shape[1]
    n_tok = qt_ref.shape[1]
    shift = lax.rem(n_tok - pl.program_id(0) * tb, n_tok)
    sq = (HEAD_DIM, HEAD_DIM)
    for h in range(n_heads):
        hs = slice(h * HEAD_DIM, (h + 1) * HEAD_DIM)
        qh = pltpu.roll(qt_ref[hs, :], shift, 1)
        fh = pltpu.roll(ft_ref[hs, :], shift, 1)
        rows = []
        for t in range(tb):
            f_all = jnp.broadcast_to(fh[:, t:t + 1], sq)
            s_new = f_all * s_ref[t, h] + (1.0 - f_all) * v_ref[t:t + 1, hs]
            snew_ref[t, h] = s_new
            rows.append(jnp.sum(jnp.broadcast_to(qh[:, t:t + 1], sq) * s_new, axis=0, keepdims=True))
        o_ref[:, hs] = jnp.concatenate(rows, axis=0)


def _sample_out_kernel(x_ref, o_ref, g_ref, yb_ref, hgg_ref, wout_ref, g2_ref, wr_ref, br_ref,
                       x1_ref, xn2_ref, gates_ref, *, n_groups, per_group):
    wa_w = o_ref.shape[1]
    ya = []
    for h in range(wa_w // HEAD_DIM):
        hs = slice(h * HEAD_DIM, (h + 1) * HEAD_DIM)
        oh = o_ref[:, hs]
        oh = oh * lax.rsqrt(jnp.mean(oh * oh, axis=-1, keepdims=True) + EPS) * hgg_ref[:, hs]
        ya.append(oh * _silu(g_ref[:, hs]))
    y = jnp.concatenate(ya + [yb_ref[...]], axis=-1)
    x1 = x_ref[...] + jnp.dot(y, wout_ref[...], precision=HIGHEST, preferred_element_type=F32)
    x1_ref[...] = x1
    xn2 = _rms(x1, g2_ref[...])
    xn2_ref[...] = xn2.astype(BF16)
    logits = jnp.dot(xn2, wr_ref[...], precision=HIGHEST, preferred_element_type=F32) + br_ref[...]
    gates_ref[...] = _route(logits, n_groups, per_group)


def _whole(kernel, out_shape, *args, name):
    return pl.pallas_call(
        kernel, out_shape=out_shape,
        compiler_params=pltpu.CompilerParams(vmem_limit_bytes=VMEM_LIMIT_BYTES), name=name)(*args)


def _mixer_sample(x, s0, h0, c0, p, tb):
    n, d = x.shape
    wa_w = p["hgg"].shape[1]
    wb_w = p["cb"].shape[1]
    n_heads = wa_w // HEAD_DIM
    sd = lambda w: jax.ShapeDtypeStruct((n, w), F32)
    key_major = jax.ShapeDtypeStruct((wa_w, n), F32)
    q, f, v, g, yb, h_new, xr = _whole(
        _sample_in_kernel, (key_major,) * 2 + (sd(wa_w),) * 2 + (sd(wb_w),) * 3,
        x, c0[:, 0, :], c0[:, 1, :], c0[:, 2, :], h0, p["lbw"], p["g1"], p["win"], p["cw"], p["cb"],
        p["wx"], p["bx"], p["wa"], p["ba"], p["lam"], name="sample_in")
    tok = lambda w: pl.BlockSpec((tb, w), lambda i: (i, 0))
    st = pl.BlockSpec((tb, n_heads, HEAD_DIM, HEAD_DIM), lambda i: (i, 0, 0, 0))
    s_new, o = pl.pallas_call(
        _sample_state_kernel,
        grid=(n // tb,),
        in_specs=[st, _const_spec((wa_w, n)), _const_spec((wa_w, n)), tok(wa_w)],
        out_specs=(st, tok(wa_w)),
        out_shape=(jax.ShapeDtypeStruct(s0.shape, F32), sd(wa_w)),
        compiler_params=pltpu.CompilerParams(dimension_semantics=("arbitrary",),
                                             vmem_limit_bytes=VMEM_LIMIT_BYTES),
        name="sample_state",
    )(s0, q, f, v)
    x1, xn2, gates = _whole(
        functools.partial(_sample_out_kernel, n_groups=p["n_groups"], per_group=p["per_group"]),
        (sd(d), jax.ShapeDtypeStruct((n, d), BF16), sd(ROUTER_LANES)),
        x, o, g, yb, p["hgg"], p["wout"], p["g2"], p["wr"], p["br"], name="sample_out")
    c_new = jnp.stack([c0[:, 1, :], c0[:, 2, :], xr], axis=1)
    return x1, xn2, gates, s_new, h_new, c_new


def _block_diag(w):
    n, c, _ = w.shape
    eye = jnp.eye(n, dtype=w.dtype)
    return (w[:, :, None, :] * eye[:, None, :, None]).reshape(n * c, n * c)


def _prepare(lower_bounds, ln1_g, w_in, hgrn_norm_g, conv_w, conv_b, lru_wx, lru_bx, lru_wa, lru_ba,
             lru_lambda, w_out, ln2_g, router_wg, router_bg, router_we, router_be, exp_w1, exp_w3,
             exp_w2, final_g):
    d = w_in.shape[1]
    n_groups = router_wg.shape[-1]
    per_group = router_we.shape[-1]
    row = lambda a: a.reshape(1, -1).astype(F32)
    we = jnp.transpose(router_we[0], (1, 0, 2)).reshape(d, n_groups * per_group)
    pad = ROUTER_LANES - n_groups - n_groups * per_group
    wr = jnp.concatenate([router_wg[0], we, jnp.zeros((d, pad), F32)], axis=1)
    br = jnp.concatenate([router_bg[0], router_be[0].reshape(-1), jnp.zeros((pad,), F32)]).reshape(1, -1)
    rows_t = -(-(8 + n_groups * per_group) // 16) * 16
    wr_t = jnp.concatenate([router_wg[0].T, jnp.zeros((8 - n_groups, d), F32), we.T,
                            jnp.zeros((rows_t - 8 - n_groups * per_group, d), F32)], axis=0)
    br_t = jnp.concatenate([router_bg[0], jnp.zeros((8 - n_groups,), F32), router_be[0].reshape(-1),
                            jnp.zeros((rows_t - 8 - n_groups * per_group,), F32)]).reshape(-1, 1)
    wx = _block_diag(lru_wx[0])
    wa = _block_diag(lru_wa[0])
    return dict(
        wr_t_bf=wr_t.astype(BF16), br_t=br_t,
        n_groups=n_groups, per_group=per_group,
        lbw=lower_bounds.astype(F32), g1=row(ln1_g[0]), win=w_in[0],
        hgg=row(hgrn_norm_g[0]), cw=conv_w[0], cb=row(conv_b[0]),
        wx=wx, wx_bf=wx.astype(BF16), bx=row(lru_bx[0]), wa=wa, wa_bf=wa.astype(BF16), ba=row(lru_ba[0]),
        lam=row(lru_lambda[0]), wout=w_out[0], g2=row(ln2_g[0]),
        wr=wr, br=br, w1_bf=exp_w1[0].astype(BF16), w3_bf=exp_w3[0].astype(BF16),
        w2=exp_w2[0], gf=row(final_g))


def kernel(x_prompt, x_sample, state_hgrn, state_rglru, state_conv, lower_bounds, ln1_g, w_in, hgrn_norm_g, conv_w, conv_b, lru_wx, lru_bx, lru_wa, lru_ba, lru_lambda, w_out, ln2_g, router_wg, router_bg, router_we, router_be, exp_w1, exp_w3, exp_w2, final_g):
    assert w_in.shape[0] == 1, "single-layer trunk"
    p = _prepare(lower_bounds, ln1_g, w_in, hgrn_norm_g, conv_w, conv_b, lru_wx, lru_bx, lru_wa, lru_ba,
                 lru_lambda, w_out, ln2_g, router_wg, router_bg, router_we, router_be, exp_w1, exp_w3,
                 exp_w2, final_g)
    bsz, seq, d = x_prompt.shape
    x1, xn2, gates, counts, s_p, h_p, c_p = _mixer_prompt(x_prompt, p, min(MIXER_BLOCK, seq))
    t = bsz * seq
    y_p = _moe_sorted(x1.reshape(t, d), xn2.reshape(t, d), gates.reshape(t, ROUTER_LANES), counts, p)

    n = x_sample.shape[0]
    x1s, xn2s, gates_s, s_s, h_s, c_s = _mixer_sample(x_sample[:, 0, :], state_hgrn[0], state_rglru[0],
                                                      state_conv[0], p, SAMPLE_STEP_TOKENS)
    y_s = _moe_dense(x1s, xn2s, gates_s, p, n)
    return (y_p.reshape(bsz, seq, d), y_s.reshape(n, 1, d),
            s_p[None], h_p.reshape(1, bsz, -1), c_p[None],
            s_s[None], h_s[None], c_s[None])
```

```python
import functools

import jax
import jax.numpy as jnp
from jax import lax
from jax.experimental import pallas as pl
from jax.experimental.pallas import tpu as pltpu

F32 = jnp.float32
BF16 = jnp.bfloat16
HIGHEST = lax.Precision.HIGHEST

EPS = 1e-6
LRU_C = 8.0
LOG2E = 1.4426950408889634
HEAD_DIM = 128
CHUNK = 64
SUB = 16
UNROLL = 8
ROUTER_LANES = 128
PROMPT_EXPERT_LANE = 8
MIXER_BLOCK = 512
SAMPLE_STEP_TOKENS = 8
TOK_TILE = 512
SEG = 16
MAX_GROUPS = 8
SORT_ROWS = TOK_TILE + MAX_GROUPS * SEG
ROW_TILE = 512
VMEM_LIMIT_BYTES = 56 * 1024 * 1024

NT_DIMS = (((1,), (1,)), ((), ()))
TN_DIMS = (((0,), (0,)), ((), ()))


def _rms(x, g):
    return x * lax.rsqrt(jnp.mean(x * x, axis=-1, keepdims=True) + EPS) * g


def _sigmoid(x):
    return 1.0 / (1.0 + jnp.exp(-x))


def _silu(x):
    return x * _sigmoid(x)


def _gelu_tanh(x):
    c = 0.7978845608028654
    return x * (0.5 * (1.0 + jnp.tanh(c * (x + 0.044715 * (x * x * x)))))


def _softplus(z):
    return jnp.maximum(z, 0.0) + jnp.log1p(jnp.exp(-jnp.abs(z)))


def _expm1(x):
    u = jnp.exp(x)
    um1 = u - 1.0
    small = um1 * x / jnp.log(u)
    return jnp.where(um1 == 0.0, x, jnp.where(jnp.abs(x) < 0.5, small, um1))


def _forget_lower_bound(lbw):
    m = jnp.max(lbw, axis=0, keepdims=True)
    e = jnp.exp(lbw - m)
    return e[0:1, :] / jnp.sum(e, axis=0, keepdims=True)


def _route(logits, n_groups, per_group):
    n = logits.shape[-1]
    col = lax.broadcasted_iota(jnp.int32, logits.shape, 1)
    neg = jnp.float32(-jnp.inf)
    big = jnp.int32(n)
    is_g = col < n_groups
    lg = jnp.where(is_g, logits, neg)
    mg = jnp.max(lg, axis=-1, keepdims=True)
    g_idx = jnp.min(jnp.where(lg == mg, col, big), axis=-1, keepdims=True)
    p_top = 1.0 / jnp.sum(jnp.where(is_g, jnp.exp(logits - mg), 0.0), axis=-1, keepdims=True)
    lo = n_groups + per_group * g_idx
    le = jnp.where((col >= lo) & (col < lo + per_group), logits, neg)
    m1 = jnp.max(le, axis=-1, keepdims=True)
    i1 = jnp.min(jnp.where(le == m1, col, big), axis=-1, keepdims=True)
    le2 = jnp.where(col == i1, neg, le)
    m2 = jnp.max(le2, axis=-1, keepdims=True)
    i2 = jnp.min(jnp.where(le2 == m2, col, big), axis=-1, keepdims=True)
    e2 = jnp.exp(m2 - m1)
    den = 1.0 + e2
    w1 = p_top / den
    w2 = p_top * (e2 / den)
    gates = jnp.where(col == i1, w1, 0.0) + jnp.where(col == i2, w2, 0.0)
    return gates + jnp.where(col == 0, g_idx.astype(F32), 0.0)


def _route_rows(logits_t, n_groups, per_group):
    assert n_groups <= 8 and per_group == 8
    n_tok = logits_t.shape[1]
    row = lax.broadcasted_iota(jnp.int32, (8, n_tok), 0)
    neg = jnp.float32(-jnp.inf)
    big = jnp.int32(8)
    lg = jnp.where(row < n_groups, logits_t[0:8], neg)
    mg = jnp.max(lg, axis=0, keepdims=True)
    g_idx = jnp.min(jnp.where(lg == mg, row, big), axis=0, keepdims=True)
    p_top = 1.0 / jnp.sum(jnp.where(row < n_groups, jnp.exp(lg - mg), 0.0), axis=0, keepdims=True)
    sel = logits_t[8:16]
    for g in range(1, n_groups):
        sel = jnp.where(g_idx == g, logits_t[8 + 8 * g:16 + 8 * g], sel)
    m1 = jnp.max(sel, axis=0, keepdims=True)
    i1 = jnp.min(jnp.where(sel == m1, row, big), axis=0, keepdims=True)
    sel2 = jnp.where(row == i1, neg, sel)
    m2 = jnp.max(sel2, axis=0, keepdims=True)
    i2 = jnp.min(jnp.where(sel2 == m2, row, big), axis=0, keepdims=True)
    e2 = jnp.exp(m2 - m1)
    den = 1.0 + e2
    w1 = p_top / den
    w2 = p_top * (e2 / den)
    own = jnp.where(row == i1, w1, 0.0) + jnp.where(row == i2, w2, 0.0)
    blocks = [jnp.where(row == 0, g_idx.astype(F32), 0.0)]
    blocks += [jnp.where(g_idx == g, own, 0.0) for g in range(n_groups)]
    blocks += [jnp.zeros((8, n_tok), F32)] * (ROUTER_LANES // 8 - len(blocks))
    return jnp.concatenate(blocks, axis=0), g_idx


def _mixer_prompt_kernel(x_ref, lbw_ref, g1_ref, win_ref, hgg_ref, cw_ref, cb_ref, wx_ref, bx_ref,
                         wa_ref, ba_ref, lam_ref, wout_ref, g2_ref, wr_ref, br_ref,
                         x1_ref, xn2_ref, gates_ref, cnt_ref, sout_ref, hout_ref, cout_ref,
                         proj_s, k_s, b_s, o_s, st_s, xr_s, a_s, u_s, hcar_s, yb_out_s, win_s, wout_s,
                         *, n_groups, per_group):
    lb_t = x_ref.shape[1]
    wa_w = o_s.shape[1]
    wb_w = a_s.shape[1]
    n_heads = wa_w // HEAD_DIM
    j = pl.program_id(1)
    nj = pl.num_programs(1)

    @pl.when((pl.program_id(0) == 0) & (j == 0))
    def _():
        for c in range(0, win_ref.shape[1], wa_w):
            win_s[:, c:c + wa_w] = win_ref[:, c:c + wa_w].astype(BF16)
        wout_s[...] = wout_ref[...].astype(BF16)

    @pl.when(j == 0)
    def _():
        st_s[...] = jnp.zeros_like(st_s)
        hcar_s[...] = jnp.zeros_like(hcar_s)
        xr_s[0:8, :] = jnp.zeros((8, wb_w), F32)

    x = x_ref[0]
    xn = _rms(x, g1_ref[...]).astype(BF16)
    xb0 = 4 * wa_w

    def project(pieces):
        for c, w in pieces:
            proj_s[:, c:c + w] = jnp.dot(xn, win_s[:, c:c + w], preferred_element_type=F32)

    project(((wa_w, wa_w), (xb0, wb_w), (xb0 + wb_w, wb_w), (0, wa_w), (2 * wa_w, wa_w), (3 * wa_w, wa_w)))

    xr_s[pl.ds(8, lb_t), :] = proj_s[:, xb0:xb0 + wb_w]
    xc = (cb_ref[...] + cw_ref[3:4, :] * xr_s[pl.ds(8, lb_t), :] + cw_ref[2:3, :] * xr_s[pl.ds(7, lb_t), :]
          + cw_ref[1:2, :] * xr_s[pl.ds(6, lb_t), :] + cw_ref[0:1, :] * xr_s[pl.ds(5, lb_t), :])
    tail = xr_s[pl.ds(lb_t + 5, 3), :]
    xr_s[5:8, :] = tail
    cout_ref[0] = tail
    xcb = xc.astype(BF16)
    gate_x = _sigmoid(jnp.dot(xcb, wx_ref[...], preferred_element_type=F32) + bx_ref[...])
    gate_a = _sigmoid(jnp.dot(xcb, wa_ref[...], preferred_element_type=F32) + ba_ref[...])
    log_a =(-LRU_C) * gate_a * _softplus(-lam_ref[...])
    a = jnp.exp(log_a)
    mult = jnp.sqrt((1.0 - a) * (1.0 + a))
    first =(lax.broadcasted_iota(jnp.int32, (lb_t, 1), 0) == 0) & (j == 0)
    a = jnp.where(first, 0.0, a)
    mult = jnp.where(first, 1.0, mult)
    a_s[...] = a
    u_s[...] = gate_x * xc * mult

    lb = _forget_lower_bound(lbw_ref[...])
    f = lb + (1.0 - lb) * _sigmoid(proj_s[:, wa_w:2 * wa_w])
    for h in range(n_heads):
        k_s[h] = 1.0 - f[:, h * HEAD_DIM:(h + 1) * HEAD_DIM]
    logf = jnp.log(f)
    r_i = lax.broadcasted_iota(jnp.int32, (CHUNK, CHUNK), 0)
    c_i = lax.broadcasted_iota(jnp.int32, (CHUNK, CHUNK), 1)
    tri = jnp.where(r_i >= c_i, 1.0, 0.0).astype(BF16)
    lf_hi = logf.astype(BF16)
    rest = logf - lf_hi.astype(F32)
    lf_mid = rest.astype(BF16)
    lf_lo = (rest - lf_mid.astype(F32)).astype(BF16)
    for c in range(0, lb_t, CHUNK):
        cum = [jnp.dot(tri, part[c:c + CHUNK, :], preferred_element_type=F32) for part in (lf_lo, lf_mid, lf_hi)]
        b_all = LOG2E * ((cum[0] + cum[1]) + cum[2])
        for h in range(n_heads):
            b_s[h, c:c + CHUNK, :] = b_all[:, h * HEAD_DIM:(h + 1) * HEAD_DIM]

    row_sub = lax.broadcasted_iota(jnp.int32, (SUB, HEAD_DIM), 0)
    lane_sub = lax.broadcasted_iota(jnp.int32, (SUB, HEAD_DIM), 1)
    assert n_heads % 2 == 0
    r_kk = lax.broadcasted_iota(jnp.int32, (2 * HEAD_DIM, 2 * HEAD_DIM), 0)
    c_kk = lax.broadcasted_iota(jnp.int32, (2 * HEAD_DIM, 2 * HEAD_DIM), 1)
    ones_kk = jnp.where((r_kk < HEAD_DIM) == (c_kk < HEAD_DIM), 1.0, 0.0).astype(BF16)

    n_sub = CHUNK // SUB
    half = SUB // 2
    lower_left = (row_sub >= half) & (lane_sub < half)
    own_lane = jnp.where(row_sub >= half, half, 0)

    def chunk_start(r0):
        first = []
        for h in range(n_heads):
            hs = slice(h * HEAD_DIM, (h + 1) * HEAD_DIM)
            q = proj_s[pl.ds(r0, CHUNK), hs]
            b = b_s[h, pl.ds(r0, CHUNK), :]
            v = proj_s[pl.ds(r0, CHUNK), 2 * wa_w + h * HEAD_DIM:2 * wa_w + (h + 1) * HEAD_DIM]
            k = k_s[h, pl.ds(r0, CHUNK), :]

            def key_rows(ref, lo, j):
                return jnp.stack([jnp.broadcast_to(ref[h, pl.ds(r0 + lo + hf * half + j, 1), :], (half, HEAD_DIM))
                                  for hf in range(2)])
            vb = v.astype(BF16)
            st = st_s[h]
            b_last = b[CHUNK - 1:CHUNK, :]
            o = lax.dot_general((q * jnp.exp2(b)).astype(BF16), st.astype(BF16), NT_DIMS,
                                preferred_element_type=F32)
            k_end = k * jnp.exp2(b_last - b)
            st_s[h] = st * jnp.exp2(b_last) + lax.dot_general(vb, k_end.astype(BF16), TN_DIMS,
                                                               preferred_element_type=F32)
            terms, off, mid = [], [], []
            for i in range(n_sub):
                lo = i * SUB
                qi, bi, ki = q[lo:lo + SUB], b[lo:lo + SUB], k[lo:lo + SUB]
                q3, b3 = (a.reshape(2, half, HEAD_DIM) for a in (qi, bi))
                terms += [(q3 * (key_rows(k_s, lo, j) * jnp.exp2(b3 - key_rows(b_s, lo, j))))
                          .reshape(SUB, HEAD_DIM).astype(BF16) for j in range(half)]
                rm = bi[half - 1:half]
                mid.append(lax.dot_general((qi * jnp.exp2(bi - rm)).astype(BF16),
                                           (ki * jnp.exp2(rm - bi)).astype(BF16), NT_DIMS,
                                           preferred_element_type=F32))
                if i > 0:
                    r = b[lo - 1:lo]
                    qt = (qi * jnp.exp2(bi - r)).astype(BF16)
                    kt = (k[:lo] * jnp.exp2(r - b[:lo])).astype(BF16)
                    off.append(lax.dot_general(qt, kt, NT_DIMS, preferred_element_type=F32))
            first.append([o, vb, off, jnp.concatenate(terms, axis=0), mid])
        for h in range(0, n_heads, 2):
            both = jnp.concatenate([first[h][3], first[h + 1][3]], axis=1)
            sums = jnp.dot(both, ones_kk, preferred_element_type=F32)
            first[h][3] = sums[:, :HEAD_DIM]
            first[h + 1][3] = sums[:, HEAD_DIM:]
        return first

    def chunk_finish(r0, first):
        for h in range(n_heads):
            o, vb, off, sums, mid = first[h]
            outs = []
            for i in range(n_sub):
                lo = i * SUB
                sc = jnp.zeros((SUB, HEAD_DIM), F32)
                for j in range(half):
                    row0 = (i * half + j) * SUB
                    sc = jnp.where(lane_sub == own_lane + j, sums[row0:row0 + SUB], sc)
                sc = jnp.where(row_sub >= lane_sub, sc, 0.0)[:, :SUB]
                sc = jnp.where(lower_left[:, :SUB], mid[i], sc)
                od = o[lo:lo + SUB] + jnp.dot(sc.astype(BF16), vb[lo:lo + SUB], preferred_element_type=F32)
                if i > 0:
                    od = od + jnp.dot(off[i - 1].astype(BF16), vb[:lo], preferred_element_type=F32)
                outs.append(od)
            o_s[pl.ds(r0, CHUNK), h * HEAD_DIM:(h + 1) * HEAD_DIM] = jnp.concatenate(outs, axis=0)

    def chunks_body(ci, carry):
        rows = [pl.multiple_of((ci * UNROLL + u) * CHUNK, CHUNK) for u in range(UNROLL)]
        started = [chunk_start(r0) for r0 in rows]
        for r0, first in zip(rows, started):
            chunk_finish(r0, first)
        return carry

    row8 = lax.broadcasted_iota(jnp.int32, (8, 1), 0)

    def scan_body(gi, carry):
        r0 = pl.multiple_of(gi * 8, 8)
        aa = a_s[pl.ds(r0, 8), :]
        uu = u_s[pl.ds(r0, 8), :]
        for s in (1, 2, 4):
            m = row8 >= s
            uu = jnp.where(m, aa * pltpu.roll(uu, s, 0) + uu, uu)
            aa = jnp.where(m, aa * pltpu.roll(aa, s, 0), aa)
        hh = aa * carry + uu
        u_s[pl.ds(r0, 8), :] = hh
        return hh[7:8, :]

    h_last = lax.fori_loop(0, lb_t // 8, scan_body, hcar_s[...])
    hcar_s[...] = h_last
    hout_ref[0] = h_last
    yb = (u_s[...] * _gelu_tanh(proj_s[:, xb0 + wb_w:xb0 + 2 * wb_w])).astype(BF16)
    yb_out_s[...] = jnp.dot(yb, wout_s[wa_w:, :], preferred_element_type=F32)

    assert lb_t % (CHUNK * UNROLL) == 0
    lax.fori_loop(0, lb_t // (CHUNK * UNROLL), chunks_body, 0)

    ya = []
    for h in range(n_heads):
        hs = slice(h * HEAD_DIM, (h + 1) * HEAD_DIM)
        oh = o_s[:, hs]
        oh = oh * lax.rsqrt(jnp.mean(oh * oh, axis=-1, keepdims=True) + EPS) * hgg_ref[:, hs]
        ya.append(oh * _silu(proj_s[:, 3 * wa_w + h * HEAD_DIM:3 * wa_w + (h + 1) * HEAD_DIM]))

    ya = jnp.concatenate(ya, axis=-1).astype(BF16)
    x1 = x + (jnp.dot(ya, wout_s[:wa_w, :], preferred_element_type=F32) + yb_out_s[...])
    x1_ref[0] = x1
    xn2 = _rms(x1, g2_ref[...]).astype(BF16)
    xn2_ref[0] = xn2
    logits_t = lax.dot_general(wr_ref[...], xn2, NT_DIMS, preferred_element_type=F32) + br_ref[...]
    gates_t, g_idx = _route_rows(logits_t, n_groups, per_group)
    gates_ref[0] = gates_t.T
    lane_c = lax.broadcasted_iota(jnp.int32, (1, ROUTER_LANES), 1)
    for t in range(lb_t // TOK_TILE):
        gi = g_idx[:, t * TOK_TILE:(t + 1) * TOK_TILE]
        cnt = jnp.zeros((1, ROUTER_LANES), jnp.int32)
        for g in range(n_groups):
            cnt = cnt + jnp.where(lane_c == g, jnp.sum(jnp.where(gi == g, 1, 0), axis=1, keepdims=True), 0)
        cnt_ref[0, t:t + 1, :] = cnt

    @pl.when(j == nj - 1)
    def _():
        for h in range(n_heads):
            sout_ref[0, h] = st_s[h].T


def _const_spec(shape):
    nd = len(shape)
    return pl.BlockSpec(shape, lambda *_: (0,) * nd)


def _mixer_prompt(x, p, lb_t):
    bsz, seq, d = x.shape
    wa_w = p["hgg"].shape[1]
    wb_w = p["cb"].shape[1]
    n_heads = wa_w // HEAD_DIM
    n_cols = p["win"].shape[1]
    weights = [p["lbw"], p["g1"], p["win"], p["hgg"], p["cw"], p["cb"], p["wx_bf"], p["bx"],
               p["wa_bf"], p["ba"], p["lam"], p["wout"], p["g2"], p["wr_t_bf"], p["br_t"]]
    cast_in_kernel = (p["win"], p["wout"])
    nj = seq // lb_t
    tile = lambda w: pl.BlockSpec((1, lb_t, w), lambda b, j: (b, j, 0))
    out_shape = (
        jax.ShapeDtypeStruct((bsz, seq, d), F32),
        jax.ShapeDtypeStruct((bsz, seq, d), BF16),
        jax.ShapeDtypeStruct((bsz, seq, ROUTER_LANES), F32),
        jax.ShapeDtypeStruct((bsz * (seq // lb_t), lb_t // TOK_TILE, ROUTER_LANES), jnp.int32),
        jax.ShapeDtypeStruct((bsz, n_heads, HEAD_DIM, HEAD_DIM), F32),
        jax.ShapeDtypeStruct((bsz, 1, wb_w), F32),
        jax.ShapeDtypeStruct((bsz, 3, wb_w), F32),
    )
    out_specs = (
        tile(d), tile(d), tile(ROUTER_LANES),
        pl.BlockSpec((1, lb_t // TOK_TILE, ROUTER_LANES), lambda b, j: (b * nj + j, 0, 0)),
        pl.BlockSpec((1, n_heads, HEAD_DIM, HEAD_DIM), lambda b, j: (b, 0, 0, 0)),
        pl.BlockSpec((1, 1, wb_w), lambda b, j: (b, 0, 0)),
        pl.BlockSpec((1, 3, wb_w), lambda b, j: (b, 0, 0)),
    )
    scratch = [
        pltpu.VMEM((lb_t, n_cols), F32),
        pltpu.VMEM((n_heads, lb_t, HEAD_DIM), F32),
        pltpu.VMEM((n_heads, lb_t, HEAD_DIM), F32),
        pltpu.VMEM((lb_t, wa_w), F32),
        pltpu.VMEM((n_heads, HEAD_DIM, HEAD_DIM), F32),
        pltpu.VMEM((lb_t + 8, wb_w), F32),
        pltpu.VMEM((lb_t, wb_w), F32),
        pltpu.VMEM((lb_t, wb_w), F32),
        pltpu.VMEM((1, wb_w), F32),
        pltpu.VMEM((lb_t, d), F32),
        pltpu.VMEM((d, n_cols), BF16),
        pltpu.VMEM(p["wout"].shape, BF16),
    ]
    weight_specs = [pl.BlockSpec(w.shape, lambda *_, nd=w.ndim: (0,) * nd, pipeline_mode=pl.Buffered(1))
                    if any(w is c for c in cast_in_kernel) else _const_spec(w.shape) for w in weights]
    kern = functools.partial(_mixer_prompt_kernel, n_groups=p["n_groups"], per_group=p["per_group"])
    return pl.pallas_call(
        kern,
        grid=(bsz, nj),
        in_specs=[tile(d)] + weight_specs,
        out_specs=out_specs,
        out_shape=out_shape,
        scratch_shapes=scratch,
        compiler_params=pltpu.CompilerParams(dimension_semantics=("arbitrary", "arbitrary"),
                                             vmem_limit_bytes=VMEM_LIMIT_BYTES),
        name="mixer_prompt",
    )(x, *weights)


def _moe_kernel(x1_ref, xn2_ref, gates_ref, w1_ref, w3_ref, w2_ref, gf_ref, y_ref, acc_s, *, n_groups):
    e = pl.program_id(1)

    @pl.when(e == 0)
    def _():
        acc_s[...] = jnp.zeros_like(acc_s)

    xb = xn2_ref[...]
    per_step = w1_ref.shape[0]
    gates = pltpu.roll(gates_ref[...], ROUTER_LANES - n_groups - per_step * e, 1)
    acc = acc_s[...]
    for s in range(per_step):
        h = (_silu(jnp.dot(xb, w1_ref[s], preferred_element_type=F32))
             * jnp.dot(xb, w3_ref[s], preferred_element_type=F32) * gates[:, s:s + 1])
        acc = acc + jnp.dot(h.astype(BF16), w2_ref[s].astype(BF16), preferred_element_type=F32)
    acc_s[...] = acc

    @pl.when(e == pl.num_programs(1) - 1)
    def _():
        y_ref[...] = _rms(x1_ref[...] + acc_s[...], gf_ref[...])


def _moe_dense(x1, xn2, gates, p, tm):
    t, d = x1.shape
    n_exp, _, d_exp = p["w1_bf"].shape
    per_step = p["per_group"]
    row = lambda w: pl.BlockSpec((tm, w), lambda i, e: (i, 0))
    return pl.pallas_call(
        functools.partial(_moe_kernel, n_groups=p["n_groups"]),
        grid=(t // tm, n_exp // per_step),
        in_specs=[row(d), row(d), row(ROUTER_LANES),
                  pl.BlockSpec((per_step, d, d_exp), lambda i, e: (e, 0, 0)),
                  pl.BlockSpec((per_step, d, d_exp), lambda i, e: (e, 0, 0)),
                  pl.BlockSpec((per_step, d_exp, d), lambda i, e: (e, 0, 0)),
                  _const_spec(p["gf"].shape)],
        out_specs=row(d),
        out_shape=jax.ShapeDtypeStruct((t, d), F32),
        scratch_shapes=[pltpu.VMEM((tm, d), F32)],
        compiler_params=pltpu.CompilerParams(dimension_semantics=("arbitrary", "arbitrary"),
                                             vmem_limit_bytes=VMEM_LIMIT_BYTES),
        name="moe_dense",
    )(x1, xn2, gates, p["w1_bf"], p["w3_bf"], p["w2"], p["gf"])


def _seg_pad(n):
    return jnp.bitwise_and(n + (SEG - 1), -SEG)


def _row_tile_pad(n):
    return jnp.bitwise_and(n + (ROW_TILE - 1), -ROW_TILE)


def _tile_segments(cnt_ref, tile, n_groups):
    padded = [_seg_pad(cnt_ref[tile * n_groups + g]) for g in range(n_groups)]
    starts, acc = [], jnp.int32(0)
    for g in range(n_groups):
        starts.append(acc)
        acc = acc + padded[g]
    return padded, starts, acc


def _group_bases(cnt_ref, n_tiles, n_groups):
    def body(t, tot):
        return tuple(tot[g] + _seg_pad(cnt_ref[t * n_groups + g]) for g in range(n_groups))
    totals = lax.fori_loop(0, n_tiles, body, (jnp.int32(0),) * n_groups)
    bases, ends, acc = [], [], jnp.int32(0)
    for g in range(n_groups):
        bases.append(acc)
        acc = acc + _row_tile_pad(totals[g])
        ends.append(acc)
    return bases, ends, totals


def _chunk_rows(n_chunks, starts, gstart):
    rows, row = [], None
    for c in range(n_chunks):
        row = gstart[0] if c == 0 else row + SEG
        for g in range(1, len(starts)):
            row = jnp.where(starts[g] == c * SEG, gstart[g], row)
        rows.append(pl.multiple_of(row, SEG))
    return rows


def _sort_matrix(gates, starts):
    n_groups = len(starts)
    col = lax.broadcasted_iota(jnp.int32, (TOK_TILE, ROUTER_LANES), 1)
    gi = gates[:, 0:1].astype(jnp.int32)
    onehot = col == gi
    r_i = lax.broadcasted_iota(jnp.int32, (TOK_TILE, TOK_TILE), 0)
    c_i = lax.broadcasted_iota(jnp.int32, (TOK_TILE, TOK_TILE), 1)
    earlier = jnp.where(r_i > c_i, 1.0, 0.0).astype(BF16)
    before = jnp.dot(earlier, jnp.where(onehot, 1.0, 0.0).astype(BF16), preferred_element_type=F32)
    rank = jnp.sum(jnp.where(onehot, before, 0.0), axis=-1, keepdims=True).astype(jnp.int32)
    base = jnp.zeros((TOK_TILE, 1), jnp.int32)
    for g in range(n_groups):
        base = base + jnp.where(gi == g, starts[g], 0)
    lane = lax.broadcasted_iota(jnp.int32, (TOK_TILE, SORT_ROWS), 1)
    return lane == base + rank


def _sort_matrix_rows(gates, starts):
    n_groups = len(starts)
    col = lax.broadcasted_iota(jnp.int32, (TOK_TILE, ROUTER_LANES), 1)
    onehot_t = jnp.where(col == gates[:, 0:1].astype(jnp.int32), 1.0, 0.0).T
    r_i = lax.broadcasted_iota(jnp.int32, (TOK_TILE, TOK_TILE), 0)
    c_i = lax.broadcasted_iota(jnp.int32, (TOK_TILE, TOK_TILE), 1)
    later = jnp.where(r_i < c_i, 1.0, 0.0).astype(BF16)
    before = jnp.dot(onehot_t.astype(BF16), later, preferred_element_type=F32)
    g_row = lax.broadcasted_iota(jnp.int32, (ROUTER_LANES, 1), 0)
    base = jnp.zeros((ROUTER_LANES, 1), jnp.int32)
    for g in range(n_groups):
        base = base + jnp.where(g_row == g, starts[g], 0)
    dest = jnp.sum(jnp.where(onehot_t > 0.0, before + base.astype(F32), 0.0), axis=0, keepdims=True)
    row = lax.broadcasted_iota(jnp.int32, (SORT_ROWS, TOK_TILE), 0)
    return row == dest.astype(jnp.int32)


def _dispatch_copies(xbuf, gbuf, xs_hbm, gs_hbm, sem, slot, c, row):
    return (pltpu.make_async_copy(xbuf.at[slot, pl.ds(c * SEG, SEG), :], xs_hbm.at[pl.ds(row, SEG), :], sem.at[slot]),
            pltpu.make_async_copy(gbuf.at[slot, pl.ds(c * SEG, SEG), :], gs_hbm.at[pl.ds(row, SEG), :], sem.at[slot]))


def _moe_dispatch_kernel(cnt_ref, xn2_ref, gates_ref, xs_hbm, gs_hbm, tg_ref,
                         xbuf, gbuf, sem, gstart_s, zx, zg, zsem, *, n_groups, per_group):
    j = pl.program_id(0)
    n_tiles = pl.num_programs(0)
    n_chunks = (TOK_TILE + n_groups * SEG) // SEG
    slot = lax.rem(j, 2)

    @pl.when(j == 0)
    def _():
        bases, ends, totals = _group_bases(cnt_ref, n_tiles, n_groups)
        for g in range(n_groups):
            gstart_s[g] = bases[g]
        n_rt = tg_ref.shape[0] - 1
        for i in range(n_rt):
            tg = jnp.int32(0)
            for g in range(n_groups):
                tg = tg + jnp.where(ends[g] <= i * ROW_TILE, 1, 0)
            tg_ref[i] = tg
        n_used = ends[n_groups - 1] // ROW_TILE
        tg_ref[n_rt] = n_used

        zx[...] = jnp.zeros_like(zx)
        zg[...] = jnp.zeros_like(zg)

        def zero_copies(row, n):
            return (pltpu.make_async_copy(zx.at[pl.ds(0, n), :], xs_hbm.at[pl.ds(row, n), :], zsem.at[0]),
                    pltpu.make_async_copy(zg.at[pl.ds(0, n), :], gs_hbm.at[pl.ds(row, n), :], zsem.at[0]))

        def for_each_gap(act):
            for g in range(n_groups):
                tail = bases[g] + totals[g]

                def seg_body(k, c, tail=tail):
                    for cp in zero_copies(pl.multiple_of(tail + k * SEG, SEG), SEG):
                        act(cp)
                    return c
                lax.fori_loop(0, (ends[g] - tail) // SEG, seg_body, 0)

            def tile_body(k, c):
                for cp in zero_copies(pl.multiple_of((n_used + k) * ROW_TILE, ROW_TILE), ROW_TILE):
                    act(cp)
                return c
            lax.fori_loop(0, n_rt - n_used, tile_body, 0)

        for_each_gap(lambda cp: cp.start())
        for_each_gap(lambda cp: cp.wait())

    def wait_tile(tile, slot_):
        _, _, used = _tile_segments(cnt_ref, tile, n_groups)
        for c in range(n_chunks):
            @pl.when(c * SEG < used)
            def _():
                for cp in _dispatch_copies(xbuf, gbuf, xs_hbm, gs_hbm, sem, slot_, c, 0):
                    cp.wait()

    @pl.when(j >= 2)
    def _():
        wait_tile(j - 2, slot)

    padded, starts, used = _tile_segments(cnt_ref, j, n_groups)
    gates = gates_ref[...]
    d = xn2_ref.shape[1]
    g_hi = gates.astype(BF16)
    g_lo = (gates - g_hi.astype(F32)).astype(BF16)
    sort_m = jnp.where(_sort_matrix_rows(gates, starts), 1.0, 0.0).astype(BF16)
    moved = jnp.dot(sort_m, jnp.concatenate([xn2_ref[...], g_hi, g_lo], axis=1), preferred_element_type=F32)
    xbuf[slot] = moved[:, :d].astype(BF16)
    gbuf[slot] = moved[:, d:d + ROUTER_LANES] + moved[:, d + ROUTER_LANES:]
    gstart = [gstart_s[g] for g in range(n_groups)]
    rows = _chunk_rows(n_chunks, starts, gstart)
    for c in range(n_chunks):
        @pl.when(c * SEG < used)
        def _():
            for cp in _dispatch_copies(xbuf, gbuf, xs_hbm, gs_hbm, sem, slot, c, rows[c]):
                cp.start()
    for g in range(n_groups):
        gstart_s[g] = gstart[g] + padded[g]

    @pl.when(j == n_tiles - 1)
    def _():
        @pl.when(j >= 1)
        def _():
            wait_tile(j - 1, 1 - slot)
        wait_tile(j, slot)


def _moe_group_kernel(tg_ref, xs_ref, gs_ref, w1_ref, w3_ref, w2_ref, ys_ref, w2_s, *, n_groups, per_group):
    i = pl.program_id(0)

    @pl.when((i == 0) | (tg_ref[i] != tg_ref[jnp.maximum(i - 1, 0)]))
    def _():
        w2_s[...] = w2_ref[...].astype(BF16)

    @pl.when(tg_ref[i] < n_groups)
    def _():
        xb = xs_ref[...]
        gates = pltpu.roll(gs_ref[...], ROUTER_LANES - PROMPT_EXPERT_LANE - per_group * tg_ref[i], 1)
        acc = jnp.zeros(ys_ref.shape, F32)
        for s in range(per_group):
            h = (_silu(jnp.dot(xb, w1_ref[s], preferred_element_type=F32))
                 * jnp.dot(xb, w3_ref[s], preferred_element_type=F32) * gates[:, s:s + 1])
            acc = acc + jnp.dot(h.astype(BF16), w2_s[s], preferred_element_type=F32)
        ys_ref[...] = acc.astype(BF16)


def _moe_combine_kernel(cnt_ref, gates_ref, x1_ref, gf_ref, ys_hbm, y_ref, ybuf, sem, gstart_s, *, n_groups):
    j = pl.program_id(0)
    n_tiles = pl.num_programs(0)
    n_chunks = SORT_ROWS // SEG
    slot = lax.rem(j, 2)

    def copies(slot_, c, row):
        return pltpu.make_async_copy(ys_hbm.at[pl.ds(row, SEG), :], ybuf.at[slot_, pl.ds(c * SEG, SEG), :],
                                     sem.at[slot_])

    def fetch_tile(tile, slot_):
        padded, starts, _ = _tile_segments(cnt_ref, tile, n_groups)
        gstart = [gstart_s[g] for g in range(n_groups)]
        for c, row in enumerate(_chunk_rows(n_chunks, starts, gstart)):
            copies(slot_, c, row).start()
        for g in range(n_groups):
            gstart_s[g] = gstart[g] + padded[g]

    @pl.when(j == 0)
    def _():
        bases, _, _ = _group_bases(cnt_ref, n_tiles, n_groups)
        for g in range(n_groups):
            gstart_s[g] = bases[g]
        fetch_tile(0, 0)

    @pl.when(j + 1 < n_tiles)
    def _():
        fetch_tile(j + 1, 1 - slot)

    for c in range(n_chunks):
        copies(slot, c, 0).wait()
    _, starts, _ = _tile_segments(cnt_ref, j, n_groups)
    sort_t = _sort_matrix(gates_ref[...], starts)
    moe = jnp.dot(jnp.where(sort_t, 1.0, 0.0).astype(BF16), ybuf[slot], preferred_element_type=F32)
    y_ref[...] = _rms(x1_ref[...] + moe, gf_ref[...])


def _moe_sorted(x1, xn2, gates, counts, p):
    t, d = x1.shape
    n_groups, per_group = p["n_groups"], p["per_group"]
    assert n_groups <= MAX_GROUPS and t % TOK_TILE == 0
    n_tiles = t // TOK_TILE
    d_exp = p["w1_bf"].shape[2]
    cap = t + n_tiles * n_groups * SEG + n_groups * ROW_TILE + SORT_ROWS
    n_rt = -(-cap // ROW_TILE)
    rows = n_rt * ROW_TILE
    cnt = counts[:, :, :n_groups].reshape(-1)
    params = pltpu.CompilerParams(dimension_semantics=("arbitrary",), vmem_limit_bytes=VMEM_LIMIT_BYTES)
    any_spec = pl.BlockSpec(memory_space=pl.ANY)
    tok = lambda w: pl.BlockSpec((TOK_TILE, w), lambda j, c: (j, 0))

    xs, gs, tile_group = pl.pallas_call(
        functools.partial(_moe_dispatch_kernel, n_groups=n_groups, per_group=per_group),
        grid_spec=pltpu.PrefetchScalarGridSpec(
            num_scalar_prefetch=1, grid=(n_tiles,),
            in_specs=[tok(d), tok(ROUTER_LANES)],
            out_specs=(any_spec, any_spec, pl.BlockSpec(memory_space=pltpu.SMEM)),
            scratch_shapes=[pltpu.VMEM((2, SORT_ROWS, d), BF16), pltpu.VMEM((2, SORT_ROWS, ROUTER_LANES), F32),
                            pltpu.SemaphoreType.DMA((2,)), pltpu.SMEM((n_groups,), jnp.int32),
                            pltpu.VMEM((ROW_TILE, d), BF16), pltpu.VMEM((ROW_TILE, ROUTER_LANES), F32),
                            pltpu.SemaphoreType.DMA((1,))]),
        out_shape=(jax.ShapeDtypeStruct((rows, d), BF16), jax.ShapeDtypeStruct((rows, ROUTER_LANES), F32),
                   jax.ShapeDtypeStruct((n_rt + 1,), jnp.int32)),
        compiler_params=params, name="moe_dispatch",
    )(cnt, xn2, gates)

    used_tile = lambda i, tg: (jnp.minimum(i, tg[n_rt] - 1), 0)
    group_w = lambda i, tg: (jnp.minimum(tg[i], n_groups - 1), 0, 0)
    ys = pl.pallas_call(
        functools.partial(_moe_group_kernel, n_groups=n_groups, per_group=per_group),
        grid_spec=pltpu.PrefetchScalarGridSpec(
            num_scalar_prefetch=1, grid=(n_rt,),
            in_specs=[pl.BlockSpec((ROW_TILE, d), used_tile), pl.BlockSpec((ROW_TILE, ROUTER_LANES), used_tile),
                      pl.BlockSpec((per_group, d, d_exp), group_w), pl.BlockSpec((per_group, d, d_exp), group_w),
                      pl.BlockSpec((per_group, d_exp, d), group_w)],
            out_specs=pl.BlockSpec((ROW_TILE, d), used_tile),
            scratch_shapes=[pltpu.VMEM((per_group, d_exp, d), BF16)]),
        out_shape=jax.ShapeDtypeStruct((rows, d), BF16),
        input_output_aliases={1: 0},
        compiler_params=params, name="moe_experts",
    )(tile_group, xs, gs, p["w1_bf"], p["w3_bf"], p["w2"])

    return pl.pallas_call(
        functools.partial(_moe_combine_kernel, n_groups=n_groups),
        grid_spec=pltpu.PrefetchScalarGridSpec(
            num_scalar_prefetch=1, grid=(n_tiles,),
            in_specs=[tok(ROUTER_LANES), tok(d), pl.BlockSpec(p["gf"].shape, lambda j, c: (0, 0)), any_spec],
            out_specs=tok(d),
            scratch_shapes=[pltpu.VMEM((2, SORT_ROWS, d), BF16), pltpu.SemaphoreType.DMA((2,)),
                            pltpu.SMEM((n_groups,), jnp.int32)]),
        out_shape=jax.ShapeDtypeStruct((t, d), F32),
        compiler_params=params, name="moe_combine",
    )(cnt, gates, x1, p["gf"], ys)


def _sample_in_kernel(x_ref, c0_ref, c1_ref, c2_ref, h0_ref, lbw_ref, g1_ref, win_ref, cw_ref, cb_ref,
                      wx_ref, bx_ref, wa_ref, ba_ref, lam_ref,
                      q_ref, f_ref, v_ref, g_ref, yb_ref, hnew_ref, xr_ref):
    wa_w = v_ref.shape[1]
    wb_w = yb_ref.shape[1]
    xn = _rms(x_ref[...], g1_ref[...])
    proj = jnp.dot(xn, win_ref[...], precision=HIGHEST, preferred_element_type=F32)
    lb = _forget_lower_bound(lbw_ref[...])
    f = lb + (1.0 - lb) * _sigmoid(proj[:, wa_w:2 * wa_w])
    q_ref[...] = proj[:, 0:wa_w].T
    f_ref[...] = f.T
    v_ref[...] = proj[:, 2 * wa_w:3 * wa_w]
    g_ref[...] = proj[:, 3 * wa_w:4 * wa_w]
    xr = proj[:, 4 * wa_w:4 * wa_w + wb_w]
    xr_ref[...] = xr
    xc = (cb_ref[...] + cw_ref[0:1, :] * c0_ref[...] + cw_ref[1:2, :] * c1_ref[...]
          + cw_ref[2:3, :] * c2_ref[...] + cw_ref[3:4, :] * xr)
    gate_x = _sigmoid(jnp.dot(xc, wx_ref[...], precision=HIGHEST, preferred_element_type=F32) + bx_ref[...])
    gate_a = _sigmoid(jnp.dot(xc, wa_ref[...], precision=HIGHEST, preferred_element_type=F32) + ba_ref[...])
    log_a = (-LRU_C) * gate_a * _softplus(-lam_ref[...])
    a = jnp.exp(log_a)
    mult = jnp.sqrt(-_expm1(2.0 * log_a))
    h = a * h0_ref[...] + gate_x * xc * mult
    hnew_ref[...] = h
    yb_ref[...] = h * _gelu_tanh(proj[:, 4 * wa_w + wb_w:4 * wa_w + 2 * wb_w])


def _sample_state_kernel(s_ref, qt_ref, ft_ref, v_ref, snew_ref, o_ref):
    tb, n_heads = s_ref.shape[0], s_ref.shape[1]
    n_tok = qt_ref.shape[1]
    shift = lax.rem(n_tok - pl.program_id(0) * tb, n_tok)
    assert 3 * tb <= HEAD_DIM
    lane = lax.broadcasted_iota(jnp.int32, (HEAD_DIM, n_tok), 1)
    r_e = lax.broadcasted_iota(jnp.int32, (n_tok, tb * HEAD_DIM), 0)
    c_e = lax.broadcasted_iota(jnp.int32, (n_tok, tb * HEAD_DIM), 1)
    spread = jnp.where((r_e < 3 * tb) & (lax.rem(r_e, tb) == c_e // HEAD_DIM), 1.0, 0.0).astype(BF16)

    def across_lanes(cols):
        hi = cols.astype(BF16).astype(F32)
        rest = cols - hi
        mid = rest.astype(BF16).astype(F32)
        low = rest - mid
        pieces = jnp.where(lane < tb, hi, jnp.where(lane < 2 * tb, pltpu.roll(mid, tb, 1),
                                                    jnp.where(lane < 3 * tb, pltpu.roll(low, 2 * tb, 1), 0.0)))
        return jnp.dot(pieces.astype(BF16), spread, preferred_element_type=F32)

    for h in range(n_heads):
        hs = slice(h * HEAD_DIM, (h + 1) * HEAD_DIM)
        q_all = across_lanes(pltpu.roll(qt_ref[hs, :], shift, 1))
        f_all = across_lanes(pltpu.roll(ft_ref[hs, :], shift, 1))
        rows = []
        for t in range(tb):
            ts = slice(t * HEAD_DIM, (t + 1) * HEAD_DIM)
            s_new = f_all[:, ts] * s_ref[t, h] + (1.0 - f_all[:, ts]) * v_ref[t:t + 1, hs]
            snew_ref[t, h] = s_new
            rows.append(jnp.sum(q_all[:, ts] * s_new, axis=0, keepdims=True))
        o_ref[:, hs] = jnp.concatenate(rows, axis=0)


def _sample_out_kernel(x_ref, o_ref, g_ref, yb_ref, hgg_ref, wout_ref, g2_ref, wr_ref, br_ref,
                       x1_ref, xn2_ref, gates_ref, *, n_groups, per_group):
    wa_w = o_ref.shape[1]
    ya = []
    for h in range(wa_w // HEAD_DIM):
        hs = slice(h * HEAD_DIM, (h + 1) * HEAD_DIM)
        oh = o_ref[:, hs]
        oh = oh * lax.rsqrt(jnp.mean(oh * oh, axis=-1, keepdims=True) + EPS) * hgg_ref[:, hs]
        ya.append(oh * _silu(g_ref[:, hs]))
    y = jnp.concatenate(ya + [yb_ref[...]], axis=-1)
    x1 = x_ref[...] + jnp.dot(y, wout_ref[...], precision=HIGHEST, preferred_element_type=F32)
    x1_ref[...] = x1
    xn2 = _rms(x1, g2_ref[...])
    xn2_ref[...] = xn2.astype(BF16)
    logits = jnp.dot(xn2, wr_ref[...], precision=HIGHEST, preferred_element_type=F32) + br_ref[...]
    gates_ref[...] = _route(logits, n_groups, per_group)


def _whole(kernel, out_shape, *args, name):
    return pl.pallas_call(
        kernel, out_shape=out_shape,
        compiler_params=pltpu.CompilerParams(vmem_limit_bytes=VMEM_LIMIT_BYTES), name=name)(*args)


def _mixer_sample(x, s0, h0, c0, p, tb):
    n, d = x.shape
    wa_w = p["hgg"].shape[1]
    wb_w = p["cb"].shape[1]
    n_heads = wa_w // HEAD_DIM
    sd = lambda w: jax.ShapeDtypeStruct((n, w), F32)
    key_major = jax.ShapeDtypeStruct((wa_w, n), F32)
    q, f, v, g, yb, h_new, xr = _whole(
        _sample_in_kernel, (key_major,) * 2 + (sd(wa_w),) * 2 + (sd(wb_w),) * 3,
        x, c0[:, 0, :], c0[:, 1, :], c0[:, 2, :], h0, p["lbw"], p["g1"], p["win"], p["cw"], p["cb"],
        p["wx"], p["bx"], p["wa"], p["ba"], p["lam"], name="sample_in")
    tok = lambda w: pl.BlockSpec((tb, w), lambda i: (i, 0))
    st = pl.BlockSpec((tb, n_heads, HEAD_DIM, HEAD_DIM), lambda i: (i, 0, 0, 0))
    s_new, o = pl.pallas_call(
        _sample_state_kernel,
        grid=(n // tb,),
        in_specs=[st, _const_spec((wa_w, n)), _const_spec((wa_w, n)), tok(wa_w)],
        out_specs=(st, tok(wa_w)),
        out_shape=(jax.ShapeDtypeStruct(s0.shape, F32), sd(wa_w)),
        compiler_params=pltpu.CompilerParams(dimension_semantics=("arbitrary",),
                                             vmem_limit_bytes=VMEM_LIMIT_BYTES),
        name="sample_state",
    )(s0, q, f, v)
    x1, xn2, gates = _whole(
        functools.partial(_sample_out_kernel, n_groups=p["n_groups"], per_group=p["per_group"]),
        (sd(d), jax.ShapeDtypeStruct((n, d), BF16), sd(ROUTER_LANES)),
        x, o, g, yb, p["hgg"], p["wout"], p["g2"], p["wr"], p["br"], name="sample_out")
    c_new = jnp.stack([c0[:, 1, :], c0[:, 2, :], xr], axis=1)
    return x1, xn2, gates, s_new, h_new, c_new


def _block_diag(w):
    n, c, _ = w.shape
    eye = jnp.eye(n, dtype=w.dtype)
    return (w[:, :, None, :] * eye[:, None, :, None]).reshape(n * c, n * c)


def _prepare(lower_bounds, ln1_g, w_in, hgrn_norm_g, conv_w, conv_b, lru_wx, lru_bx, lru_wa, lru_ba,
             lru_lambda, w_out, ln2_g, router_wg, router_bg, router_we, router_be, exp_w1, exp_w3,
             exp_w2, final_g):
    d = w_in.shape[1]
    n_groups = router_wg.shape[-1]
    per_group = router_we.shape[-1]
    row = lambda a: a.reshape(1, -1).astype(F32)
    we = jnp.transpose(router_we[0], (1, 0, 2)).reshape(d, n_groups * per_group)
    pad = ROUTER_LANES - n_groups - n_groups * per_group
    wr = jnp.concatenate([router_wg[0], we, jnp.zeros((d, pad), F32)], axis=1)
    br = jnp.concatenate([router_bg[0], router_be[0].reshape(-1), jnp.zeros((pad,), F32)]).reshape(1, -1)
    rows_t = -(-(8 + n_groups * per_group) // 16) * 16
    wr_t = jnp.concatenate([router_wg[0].T, jnp.zeros((8 - n_groups, d), F32), we.T,
                            jnp.zeros((rows_t - 8 - n_groups * per_group, d), F32)], axis=0)
    br_t = jnp.concatenate([router_bg[0], jnp.zeros((8 - n_groups,), F32), router_be[0].reshape(-1),
                            jnp.zeros((rows_t - 8 - n_groups * per_group,), F32)]).reshape(-1, 1)
    wx = _block_diag(lru_wx[0])
    wa = _block_diag(lru_wa[0])
    return dict(
        wr_t_bf=wr_t.astype(BF16), br_t=br_t,
        n_groups=n_groups, per_group=per_group,
        lbw=lower_bounds.astype(F32), g1=row(ln1_g[0]), win=w_in[0],
        hgg=row(hgrn_norm_g[0]), cw=conv_w[0], cb=row(conv_b[0]),
        wx=wx, wx_bf=wx.astype(BF16), bx=row(lru_bx[0]), wa=wa, wa_bf=wa.astype(BF16), ba=row(lru_ba[0]),
        lam=row(lru_lambda[0]), wout=w_out[0], g2=row(ln2_g[0]),
        wr=wr, br=br, w1_bf=exp_w1[0].astype(BF16), w3_bf=exp_w3[0].astype(BF16),
        w2=exp_w2[0], gf=row(final_g))


def kernel(x_prompt, x_sample, state_hgrn, state_rglru, state_conv, lower_bounds, ln1_g, w_in, hgrn_norm_g, conv_w, conv_b, lru_wx, lru_bx, lru_wa, lru_ba, lru_lambda, w_out, ln2_g, router_wg, router_bg, router_we, router_be, exp_w1, exp_w3, exp_w2, final_g):
    assert w_in.shape[0] == 1, "single-layer trunk"
    p = _prepare(lower_bounds, ln1_g, w_in, hgrn_norm_g, conv_w, conv_b, lru_wx, lru_bx, lru_wa, lru_ba,
                 lru_lambda, w_out, ln2_g, router_wg, router_bg, router_we, router_be, exp_w1, exp_w3,
                 exp_w2, final_g)
    bsz, seq, d = x_prompt.shape
    x1, xn2, gates, counts, s_p, h_p, c_p = _mixer_prompt(x_prompt, p, min(MIXER_BLOCK, seq))
    t = bsz * seq
    y_p = _moe_sorted(x1.reshape(t, d), xn2.reshape(t, d), gates.reshape(t, ROUTER_LANES), counts, p)

    n = x_sample.shape[0]
    x1s, xn2s, gates_s, s_s, h_s, c_s = _mixer_sample(x_sample[:, 0, :], state_hgrn[0], state_rglru[0],
                                                      state_conv[0], p, SAMPLE_STEP_TOKENS)
    y_s = _moe_dense(x1s, xn2s, gates_s, p, n)
    return (y_p.reshape(bsz, seq, d), y_s.reshape(n, 1, d),
            s_p[None], h_p.reshape(1, bsz, -1), c_p[None],
            s_s[None], h_s[None], c_s[None])
```

```python
import functools

import jax
import jax.numpy as jnp
from jax import lax
from jax.experimental import pallas as pl
from jax.experimental.pallas import tpu as pltpu

F32 = jnp.float32
BF16 = jnp.bfloat16
HIGHEST = lax.Precision.HIGHEST

EPS = 1e-6
LRU_C = 8.0
LOG2E = 1.4426950408889634
HEAD_DIM = 128
CHUNK = 64
SUB = 16
UNROLL = 8
ROUTER_LANES = 128
PROMPT_EXPERT_LANE = 8
MIXER_BLOCK = 512
SAMPLE_STEP_TOKENS = 8
TOK_TILE = 512
SEG = 16
MAX_GROUPS = 8
SORT_ROWS = TOK_TILE + MAX_GROUPS * SEG
ROW_TILE = 512
VMEM_LIMIT_BYTES = 56 * 1024 * 1024

NT_DIMS = (((1,), (1,)), ((), ()))
TN_DIMS = (((0,), (0,)), ((), ()))


def _rms(x, g):
    return x * lax.rsqrt(jnp.mean(x * x, axis=-1, keepdims=True) + EPS) * g


def _sigmoid(x):
    return 1.0 / (1.0 + jnp.exp(-x))


def _silu(x):
    return x * _sigmoid(x)


def _gelu_tanh(x):
    c = 0.7978845608028654
    return x * (0.5 * (1.0 + jnp.tanh(c * (x + 0.044715 * (x * x * x)))))


def _softplus(z):
    return jnp.maximum(z, 0.0) + jnp.log1p(jnp.exp(-jnp.abs(z)))


def _expm1(x):
    u = jnp.exp(x)
    um1 = u - 1.0
    small = um1 * x / jnp.log(u)
    return jnp.where(um1 == 0.0, x, jnp.where(jnp.abs(x) < 0.5, small, um1))


def _forget_lower_bound(lbw):
    m = jnp.max(lbw, axis=0, keepdims=True)
    e = jnp.exp(lbw - m)
    return e[0:1, :] / jnp.sum(e, axis=0, keepdims=True)


def _route_rows(logits_t, n_groups, per_group):
    assert n_groups <= 8 and per_group == 8
    n_tok = logits_t.shape[1]
    row = lax.broadcasted_iota(jnp.int32, (8, n_tok), 0)
    neg = jnp.float32(-jnp.inf)
    big = jnp.int32(8)
    lg = jnp.where(row < n_groups, logits_t[0:8], neg)
    mg = jnp.max(lg, axis=0, keepdims=True)
    g_idx = jnp.min(jnp.where(lg == mg, row, big), axis=0, keepdims=True)
    p_top = 1.0 / jnp.sum(jnp.where(row < n_groups, jnp.exp(lg - mg), 0.0), axis=0, keepdims=True)
    sel = logits_t[8:16]
    for g in range(1, n_groups):
        sel = jnp.where(g_idx == g, logits_t[8 + 8 * g:16 + 8 * g], sel)
    m1 = jnp.max(sel, axis=0, keepdims=True)
    i1 = jnp.min(jnp.where(sel == m1, row, big), axis=0, keepdims=True)
    sel2 = jnp.where(row == i1, neg, sel)
    m2 = jnp.max(sel2, axis=0, keepdims=True)
    i2 = jnp.min(jnp.where(sel2 == m2, row, big), axis=0, keepdims=True)
    e2 = jnp.exp(m2 - m1)
    den = 1.0 + e2
    w1 = p_top / den
    w2 = p_top * (e2 / den)
    own = jnp.where(row == i1, w1, 0.0) + jnp.where(row == i2, w2, 0.0)
    blocks = [jnp.where(row == 0, g_idx.astype(F32), 0.0)]
    blocks += [jnp.where(g_idx == g, own, 0.0) for g in range(n_groups)]
    blocks += [jnp.zeros((8, n_tok), F32)] * (ROUTER_LANES // 8 - len(blocks))
    return jnp.concatenate(blocks, axis=0), g_idx


def _mixer_prompt_kernel(x_ref, lbw_ref, g1_ref, win_ref, hgg_ref, cw_ref, cb_ref, wx_ref, bx_ref,
                         wa_ref, ba_ref, lam_ref, wout_ref, g2_ref, wr_ref, br_ref,
                         x1_ref, xn2_ref, gates_ref, cnt_ref, sout_ref, hout_ref, cout_ref,
                         proj_s, k_s, b_s, o_s, st_s, xr_s, a_s, u_s, hcar_s, yb_out_s, win_s, wout_s,
                         *, n_groups, per_group):
    lb_t = x_ref.shape[1]
    wa_w = o_s.shape[1]
    wb_w = a_s.shape[1]
    n_heads = wa_w // HEAD_DIM
    j = pl.program_id(1)
    nj = pl.num_programs(1)

    @pl.when((pl.program_id(0) == 0) & (j == 0))
    def _():
        for c in range(0, win_ref.shape[1], wa_w):
            win_s[:, c:c + wa_w] = win_ref[:, c:c + wa_w].astype(BF16)
        wout_s[...] = wout_ref[...].astype(BF16)

    @pl.when(j == 0)
    def _():
        st_s[...] = jnp.zeros_like(st_s)
        hcar_s[...] = jnp.zeros_like(hcar_s)
        xr_s[0:8, :] = jnp.zeros((8, wb_w), F32)

    x = x_ref[0]
    xn = _rms(x, g1_ref[...]).astype(BF16)
    xb0 = 4 * wa_w

    def project(pieces):
        for c, w in pieces:
            proj_s[:, c:c + w] = jnp.dot(xn, win_s[:, c:c + w], preferred_element_type=F32)

    project(((wa_w, wa_w), (xb0, wb_w), (xb0 + wb_w, wb_w), (0, wa_w), (2 * wa_w, wa_w), (3 * wa_w, wa_w)))

    xr_s[pl.ds(8, lb_t), :] = proj_s[:, xb0:xb0 + wb_w]
    xc = (cb_ref[...] + cw_ref[3:4, :] * xr_s[pl.ds(8, lb_t), :] + cw_ref[2:3, :] * xr_s[pl.ds(7, lb_t), :]
          + cw_ref[1:2, :] * xr_s[pl.ds(6, lb_t), :] + cw_ref[0:1, :] * xr_s[pl.ds(5, lb_t), :])
    tail = xr_s[pl.ds(lb_t + 5, 3), :]
    xr_s[5:8, :] = tail
    cout_ref[0] = tail
    xcb = xc.astype(BF16)
    gate_x = _sigmoid(jnp.dot(xcb, wx_ref[...], preferred_element_type=F32) + bx_ref[...])
    gate_a = _sigmoid(jnp.dot(xcb, wa_ref[...], preferred_element_type=F32) + ba_ref[...])
    log_a =(-LRU_C) * gate_a * _softplus(-lam_ref[...])
    a = jnp.exp(log_a)
    mult = jnp.sqrt((1.0 - a) * (1.0 + a))
    first = (lax.broadcasted_iota(jnp.int32, (lb_t, 1), 0) == 0) & (j == 0)
    a = jnp.where(first, 0.0, a)
    mult = jnp.where(first, 1.0, mult)
    a_s[...] = a
    u_s[...] = gate_x * xc * mult

    lb = _forget_lower_bound(lbw_ref[...])
    f = lb + (1.0 - lb) * _sigmoid(proj_s[:, wa_w:2 * wa_w])
    for h in range(n_heads):
        k_s[h] = 1.0 - f[:, h * HEAD_DIM:(h + 1) * HEAD_DIM]
    logf = jnp.log(f)
    r_i = lax.broadcasted_iota(jnp.int32, (CHUNK, CHUNK), 0)
    c_i = lax.broadcasted_iota(jnp.int32, (CHUNK, CHUNK), 1)
    tri = jnp.where(r_i >= c_i, 1.0, 0.0).astype(BF16)
    lf_hi = logf.astype(BF16)
    rest = logf - lf_hi.astype(F32)
    lf_mid = rest.astype(BF16)
    lf_lo = (rest - lf_mid.astype(F32)).astype(BF16)
    for c in range(0, lb_t, CHUNK):
        cum = [jnp.dot(tri, part[c:c + CHUNK, :], preferred_element_type=F32) for part in (lf_lo, lf_mid, lf_hi)]
        b_all = LOG2E * ((cum[0] + cum[1]) + cum[2])
        for h in range(n_heads):
            b_s[h, c:c + CHUNK, :] = b_all[:, h * HEAD_DIM:(h + 1) * HEAD_DIM]

    row_sub = lax.broadcasted_iota(jnp.int32, (SUB, HEAD_DIM), 0)
    lane_sub = lax.broadcasted_iota(jnp.int32, (SUB, HEAD_DIM), 1)
    assert n_heads % 2 == 0
    r_kk = lax.broadcasted_iota(jnp.int32, (2 * HEAD_DIM, 2 * HEAD_DIM), 0)
    c_kk = lax.broadcasted_iota(jnp.int32, (2 * HEAD_DIM, 2 * HEAD_DIM), 1)
    ones_kk = jnp.where((r_kk < HEAD_DIM) == (c_kk < HEAD_DIM), 1.0, 0.0).astype(BF16)

    n_sub = CHUNK // SUB
    half = SUB // 2
    lower_left = (row_sub >= half) & (lane_sub < half)
    own_lane = jnp.where(row_sub >= half, half, 0)

    def chunk_start(r0):
        first = []
        for h in range(n_heads):
            hs = slice(h * HEAD_DIM, (h + 1) * HEAD_DIM)
            q = proj_s[pl.ds(r0, CHUNK), hs]
            b = b_s[h, pl.ds(r0, CHUNK), :]
            v = proj_s[pl.ds(r0, CHUNK), 2 * wa_w + h * HEAD_DIM:2 * wa_w + (h + 1) * HEAD_DIM]
            k = k_s[h, pl.ds(r0, CHUNK), :]

            def key_rows(ref, lo, j):
                return jnp.stack([jnp.broadcast_to(ref[h, pl.ds(r0 + lo + hf * half + j, 1), :], (half, HEAD_DIM))
                                  for hf in range(2)])
            vb = v.astype(BF16)
            st = st_s[h]
            b_last = b[CHUNK - 1:CHUNK, :]
            o = lax.dot_general((q * jnp.exp2(b)).astype(BF16), st.astype(BF16), NT_DIMS,
                                preferred_element_type=F32)
            k_end = k * jnp.exp2(b_last - b)
            st_s[h] = st * jnp.exp2(b_last) + lax.dot_general(vb, k_end.astype(BF16), TN_DIMS,
                                                               preferred_element_type=F32)
            terms, off, mid = [], [], []
            for i in range(n_sub):
                lo = i * SUB
                qi, bi, ki = q[lo:lo + SUB], b[lo:lo + SUB], k[lo:lo + SUB]
                q3, b3 = (a.reshape(2, half, HEAD_DIM) for a in (qi, bi))
                terms += [(q3 * (key_rows(k_s, lo, j) * jnp.exp2(b3 - key_rows(b_s, lo, j))))
                          .reshape(SUB, HEAD_DIM).astype(BF16) for j in range(half)]
                rm = bi[half - 1:half]
                mid.append(lax.dot_general((qi * jnp.exp2(bi - rm)).astype(BF16),
                                           (ki * jnp.exp2(rm - bi)).astype(BF16), NT_DIMS,
                                           preferred_element_type=F32))
                if i > 0:
                    r = b[lo - 1:lo]
                    qt = (qi * jnp.exp2(bi - r)).astype(BF16)
                    kt = (k[:lo] * jnp.exp2(r - b[:lo])).astype(BF16)
                    off.append(lax.dot_general(qt, kt, NT_DIMS, preferred_element_type=F32))
            first.append([o, vb, off, jnp.concatenate(terms, axis=0), mid])
        for h in range(0, n_heads, 2):
            both = jnp.concatenate([first[h][3], first[h + 1][3]], axis=1)
            sums = jnp.dot(both, ones_kk, preferred_element_type=F32)
            first[h][3] = sums[:, :HEAD_DIM]
            first[h + 1][3] = sums[:, HEAD_DIM:]
        return first

    def chunk_finish(r0, first):
        for h in range(n_heads):
            o, vb, off, sums, mid = first[h]
            outs = []
            for i in range(n_sub):
                lo = i * SUB
                sc = jnp.zeros((SUB, HEAD_DIM), F32)
                for j in range(half):
                    row0 = (i * half + j) * SUB
                    sc = jnp.where(lane_sub == own_lane + j, sums[row0:row0 + SUB], sc)
                sc = jnp.where(row_sub >= lane_sub, sc, 0.0)[:, :SUB]
                sc = jnp.where(lower_left[:, :SUB], mid[i], sc)
                od = o[lo:lo + SUB] + jnp.dot(sc.astype(BF16), vb[lo:lo + SUB], preferred_element_type=F32)
                if i > 0:
                    od = od + jnp.dot(off[i - 1].astype(BF16), vb[:lo], preferred_element_type=F32)
                outs.append(od)
            o_s[pl.ds(r0, CHUNK), h * HEAD_DIM:(h + 1) * HEAD_DIM] = jnp.concatenate(outs, axis=0)

    def chunks_body(ci, carry):
        rows = [pl.multiple_of((ci * UNROLL + u) * CHUNK, CHUNK) for u in range(UNROLL)]
        started = [chunk_start(r0) for r0 in rows]
        for r0, first in zip(rows, started):
            chunk_finish(r0, first)
        return carry

    row8 = lax.broadcasted_iota(jnp.int32, (8, 1), 0)

    def scan_body(gi, carry):
        r0 = pl.multiple_of(gi * 8, 8)
        aa = a_s[pl.ds(r0, 8), :]
        uu = u_s[pl.ds(r0, 8), :]
        for s in (1, 2, 4):
            m = row8 >= s
            uu = jnp.where(m, aa * pltpu.roll(uu, s, 0) + uu, uu)
            aa = jnp.where(m, aa * pltpu.roll(aa, s, 0), aa)
        hh = aa * carry + uu
        u_s[pl.ds(r0, 8), :] = hh
        return hh[7:8, :]

    h_last = lax.fori_loop(0, lb_t // 8, scan_body, hcar_s[...])
    hcar_s[...] = h_last
    hout_ref[0] = h_last
    yb = (u_s[...] * _gelu_tanh(proj_s[:, xb0 + wb_w:xb0 + 2 * wb_w])).astype(BF16)
    yb_out_s[...] = jnp.dot(yb, wout_s[wa_w:, :], preferred_element_type=F32)

    assert lb_t % (CHUNK * UNROLL) == 0
    lax.fori_loop(0, lb_t // (CHUNK * UNROLL), chunks_body, 0)

    ya = []
    for h in range(n_heads):
        hs = slice(h * HEAD_DIM, (h + 1) * HEAD_DIM)
        oh = o_s[:, hs]
        oh = oh * lax.rsqrt(jnp.mean(oh * oh, axis=-1, keepdims=True) + EPS) * hgg_ref[:, hs]
        ya.append(oh * _silu(proj_s[:, 3 * wa_w + h * HEAD_DIM:3 * wa_w + (h + 1) * HEAD_DIM]))

    ya = jnp.concatenate(ya, axis=-1).astype(BF16)
    x1 = x + (jnp.dot(ya, wout_s[:wa_w, :], preferred_element_type=F32) + yb_out_s[...])
    x1_ref[0] = x1
    xn2 = _rms(x1, g2_ref[...]).astype(BF16)
    xn2_ref[0] = xn2
    logits_t = lax.dot_general(wr_ref[...], xn2, NT_DIMS, preferred_element_type=F32) + br_ref[...]
    gates_t, g_idx = _route_rows(logits_t, n_groups, per_group)
    gates_ref[0] = gates_t.T
    lane_c = lax.broadcasted_iota(jnp.int32, (1, ROUTER_LANES), 1)
    for t in range(lb_t // TOK_TILE):
        gi = g_idx[:, t * TOK_TILE:(t + 1) * TOK_TILE]
        cnt = jnp.zeros((1, ROUTER_LANES), jnp.int32)
        for g in range(n_groups):
            cnt = cnt + jnp.where(lane_c == g, jnp.sum(jnp.where(gi == g, 1, 0), axis=1, keepdims=True), 0)
        cnt_ref[0, t:t + 1, :] = cnt

    @pl.when(j == nj - 1)
    def _():
        for h in range(n_heads):
            sout_ref[0, h] = st_s[h].T


def _const_spec(shape):
    nd = len(shape)
    return pl.BlockSpec(shape, lambda *_: (0,) * nd)


def _mixer_prompt(x, p, lb_t):
    bsz, seq, d = x.shape
    wa_w = p["hgg"].shape[1]
    wb_w = p["cb"].shape[1]
    n_heads = wa_w // HEAD_DIM
    n_cols = p["win"].shape[1]
    weights = [p["lbw"], p["g1"], p["win"], p["hgg"], p["cw"], p["cb"], p["wx_bf"], p["bx"],
               p["wa_bf"], p["ba"], p["lam"], p["wout"], p["g2"], p["wr_t_bf"], p["br_t"]]
    cast_in_kernel = (p["win"], p["wout"])
    nj = seq // lb_t
    tile = lambda w: pl.BlockSpec((1, lb_t, w), lambda b, j: (b, j, 0))
    out_shape = (
        jax.ShapeDtypeStruct((bsz, seq, d), F32),
        jax.ShapeDtypeStruct((bsz, seq, d), BF16),
        jax.ShapeDtypeStruct((bsz, seq, ROUTER_LANES), F32),
        jax.ShapeDtypeStruct((bsz * (seq // lb_t), lb_t // TOK_TILE, ROUTER_LANES), jnp.int32),
        jax.ShapeDtypeStruct((bsz, n_heads, HEAD_DIM, HEAD_DIM), F32),
        jax.ShapeDtypeStruct((bsz, 1, wb_w), F32),
        jax.ShapeDtypeStruct((bsz, 3, wb_w), F32),
    )
    out_specs = (
        tile(d), tile(d), tile(ROUTER_LANES),
        pl.BlockSpec((1, lb_t // TOK_TILE, ROUTER_LANES), lambda b, j: (b * nj + j, 0, 0)),
        pl.BlockSpec((1, n_heads, HEAD_DIM, HEAD_DIM), lambda b, j: (b, 0, 0, 0)),
        pl.BlockSpec((1, 1, wb_w), lambda b, j: (b, 0, 0)),
        pl.BlockSpec((1, 3, wb_w), lambda b, j: (b, 0, 0)),
    )
    scratch = [
        pltpu.VMEM((lb_t, n_cols), F32),
        pltpu.VMEM((n_heads, lb_t, HEAD_DIM), F32),
        pltpu.VMEM((n_heads, lb_t, HEAD_DIM), F32),
        pltpu.VMEM((lb_t, wa_w), F32),
        pltpu.VMEM((n_heads, HEAD_DIM, HEAD_DIM), F32),
        pltpu.VMEM((lb_t + 8, wb_w), F32),
        pltpu.VMEM((lb_t, wb_w), F32),
        pltpu.VMEM((lb_t, wb_w), F32),
        pltpu.VMEM((1, wb_w), F32),
        pltpu.VMEM((lb_t, d), F32),
        pltpu.VMEM((d, n_cols), BF16),
        pltpu.VMEM(p["wout"].shape, BF16),
    ]
    weight_specs = [pl.BlockSpec(w.shape, lambda *_, nd=w.ndim: (0,) * nd, pipeline_mode=pl.Buffered(1))
                    if any(w is c for c in cast_in_kernel) else _const_spec(w.shape) for w in weights]
    kern = functools.partial(_mixer_prompt_kernel, n_groups=p["n_groups"], per_group=p["per_group"])
    return pl.pallas_call(
        kern,
        grid=(bsz, nj),
        in_specs=[tile(d)] + weight_specs,
        out_specs=out_specs,
        out_shape=out_shape,
        scratch_shapes=scratch,
        compiler_params=pltpu.CompilerParams(dimension_semantics=("arbitrary", "arbitrary"),
                                             vmem_limit_bytes=VMEM_LIMIT_BYTES),
        name="mixer_prompt",
    )(x, *weights)


def _seg_pad(n):
    return jnp.bitwise_and(n + (SEG - 1), -SEG)


def _row_tile_pad(n):
    return jnp.bitwise_and(n + (ROW_TILE - 1), -ROW_TILE)


def _tile_segments(cnt_ref, tile, n_groups):
    padded = [_seg_pad(cnt_ref[tile * n_groups + g]) for g in range(n_groups)]
    starts, acc = [], jnp.int32(0)
    for g in range(n_groups):
        starts.append(acc)
        acc = acc + padded[g]
    return padded, starts, acc


def _group_bases(cnt_ref, n_tiles, n_groups):
    def body(t, tot):
        return tuple(tot[g] + _seg_pad(cnt_ref[t * n_groups + g]) for g in range(n_groups))
    totals = lax.fori_loop(0, n_tiles, body, (jnp.int32(0),) * n_groups)
    bases, ends, acc = [], [], jnp.int32(0)
    for g in range(n_groups):
        bases.append(acc)
        acc = acc + _row_tile_pad(totals[g])
        ends.append(acc)
    return bases, ends, totals


def _chunk_rows(n_chunks, starts, gstart):
    rows, row = [], None
    for c in range(n_chunks):
        row = gstart[0] if c == 0 else row + SEG
        for g in range(1, len(starts)):
            row = jnp.where(starts[g] == c * SEG, gstart[g], row)
        rows.append(pl.multiple_of(row, SEG))
    return rows


def _sort_matrix(gates, starts):
    n_groups = len(starts)
    col = lax.broadcasted_iota(jnp.int32, (TOK_TILE, ROUTER_LANES), 1)
    gi = gates[:, 0:1].astype(jnp.int32)
    onehot = col == gi
    r_i = lax.broadcasted_iota(jnp.int32, (TOK_TILE, TOK_TILE), 0)
    c_i = lax.broadcasted_iota(jnp.int32, (TOK_TILE, TOK_TILE), 1)
    earlier = jnp.where(r_i > c_i, 1.0, 0.0).astype(BF16)
    before = jnp.dot(earlier, jnp.where(onehot, 1.0, 0.0).astype(BF16), preferred_element_type=F32)
    rank = jnp.sum(jnp.where(onehot, before, 0.0), axis=-1, keepdims=True).astype(jnp.int32)
    base = jnp.zeros((TOK_TILE, 1), jnp.int32)
    for g in range(n_groups):
        base = base + jnp.where(gi == g, starts[g], 0)
    lane = lax.broadcasted_iota(jnp.int32, (TOK_TILE, SORT_ROWS), 1)
    return lane == base + rank


def _sort_matrix_rows(gates, starts):
    n_groups = len(starts)
    col = lax.broadcasted_iota(jnp.int32, (TOK_TILE, ROUTER_LANES), 1)
    onehot_t = jnp.where(col == gates[:, 0:1].astype(jnp.int32), 1.0, 0.0).T
    r_i = lax.broadcasted_iota(jnp.int32, (TOK_TILE, TOK_TILE), 0)
    c_i = lax.broadcasted_iota(jnp.int32, (TOK_TILE, TOK_TILE), 1)
    later = jnp.where(r_i < c_i, 1.0, 0.0).astype(BF16)
    before = jnp.dot(onehot_t.astype(BF16), later, preferred_element_type=F32)
    g_row = lax.broadcasted_iota(jnp.int32, (ROUTER_LANES, 1), 0)
    base = jnp.zeros((ROUTER_LANES, 1), jnp.int32)
    for g in range(n_groups):
        base = base + jnp.where(g_row == g, starts[g], 0)
    dest = jnp.sum(jnp.where(onehot_t > 0.0, before + base.astype(F32), 0.0), axis=0, keepdims=True)
    row = lax.broadcasted_iota(jnp.int32, (SORT_ROWS, TOK_TILE), 0)
    return row == dest.astype(jnp.int32)


def _dispatch_copies(xbuf, gbuf, xs_hbm, gs_hbm, sem, slot, c, row):
    return (pltpu.make_async_copy(xbuf.at[slot, pl.ds(c * SEG, SEG), :], xs_hbm.at[pl.ds(row, SEG), :], sem.at[slot]),
            pltpu.make_async_copy(gbuf.at[slot, pl.ds(c * SEG, SEG), :], gs_hbm.at[pl.ds(row, SEG), :], sem.at[slot]))


def _moe_dispatch_kernel(cnt_ref, xn2_ref, gates_ref, xn2b_ref, gatesb_ref, xs_hbm, gs_hbm, tg_ref,
                         xbuf, gbuf, sem, gstart_s, zx, zg, zsem, *, n_groups, per_group, n_first):
    j = pl.program_id(0)
    n_tiles = pl.num_programs(0)
    n_chunks = (TOK_TILE + n_groups * SEG) // SEG
    slot = lax.rem(j, 2)

    @pl.when(j == 0)
    def _():
        bases, ends, totals = _group_bases(cnt_ref, n_tiles, n_groups)
        for g in range(n_groups):
            gstart_s[g] = bases[g]
        n_rt = tg_ref.shape[0] - 1
        for i in range(n_rt):
            tg = jnp.int32(0)
            for g in range(n_groups):
                tg = tg + jnp.where(ends[g] <= i * ROW_TILE, 1, 0)
            tg_ref[i] = tg
        n_used = ends[n_groups - 1] // ROW_TILE
        tg_ref[n_rt] = n_used

        zx[...] = jnp.zeros_like(zx)
        zg[...] = jnp.zeros_like(zg)

        def zero_copies(row, n):
            return (pltpu.make_async_copy(zx.at[pl.ds(0, n), :], xs_hbm.at[pl.ds(row, n), :], zsem.at[0]),
                    pltpu.make_async_copy(zg.at[pl.ds(0, n), :], gs_hbm.at[pl.ds(row, n), :], zsem.at[0]))

        def for_each_gap(act):
            for g in range(n_groups):
                tail = bases[g] + totals[g]

                def seg_body(k, c, tail=tail):
                    for cp in zero_copies(pl.multiple_of(tail + k * SEG, SEG), SEG):
                        act(cp)
                    return c
                lax.fori_loop(0, (ends[g] - tail) // SEG, seg_body, 0)

            def tile_body(k, c):
                for cp in zero_copies(pl.multiple_of((n_used + k) * ROW_TILE, ROW_TILE), ROW_TILE):
                    act(cp)
                return c
            lax.fori_loop(0, n_rt - n_used, tile_body, 0)

        for_each_gap(lambda cp: cp.start())
        for_each_gap(lambda cp: cp.wait())

    def wait_tile(tile, slot_):
        _, _, used = _tile_segments(cnt_ref, tile, n_groups)
        for c in range(n_chunks):
            @pl.when(c * SEG < used)
            def _():
                for cp in _dispatch_copies(xbuf, gbuf, xs_hbm, gs_hbm, sem, slot_, c, 0):
                    cp.wait()

    @pl.when(j >= 2)
    def _():
        wait_tile(j - 2, slot)

    padded, starts, used = _tile_segments(cnt_ref, j, n_groups)
    second = j >= n_first
    gates = jnp.where(second, gatesb_ref[...], gates_ref[...])
    xn2 = jnp.where(second, xn2b_ref[...], xn2_ref[...])
    d = xn2_ref.shape[1]
    g_hi = gates.astype(BF16)
    g_lo = (gates - g_hi.astype(F32)).astype(BF16)
    sort_m = jnp.where(_sort_matrix_rows(gates, starts), 1.0, 0.0).astype(BF16)
    moved = jnp.dot(sort_m, jnp.concatenate([xn2, g_hi, g_lo], axis=1), preferred_element_type=F32)
    xbuf[slot] = moved[:, :d].astype(BF16)
    gbuf[slot] = moved[:, d:d + ROUTER_LANES] + moved[:, d + ROUTER_LANES:]
    gstart = [gstart_s[g] for g in range(n_groups)]
    rows = _chunk_rows(n_chunks, starts, gstart)
    for c in range(n_chunks):
        @pl.when(c * SEG < used)
        def _():
            for cp in _dispatch_copies(xbuf, gbuf, xs_hbm, gs_hbm, sem, slot, c, rows[c]):
                cp.start()
    for g in range(n_groups):
        gstart_s[g] = gstart[g] + padded[g]

    @pl.when(j == n_tiles - 1)
    def _():
        @pl.when(j >= 1)
        def _():
            wait_tile(j - 1, 1 - slot)
        wait_tile(j, slot)


def _moe_group_kernel(tg_ref, xs_ref, gs_ref, w1_ref, w3_ref, w2_ref, ys_ref, w2_s, *, n_groups, per_group):
    i = pl.program_id(0)

    @pl.when((i == 0) | (tg_ref[i] != tg_ref[jnp.maximum(i - 1, 0)]))
    def _():
        w2_s[...] = w2_ref[...].astype(BF16)

    @pl.when(tg_ref[i] < n_groups)
    def _():
        xb = xs_ref[...]
        gates = pltpu.roll(gs_ref[...], ROUTER_LANES - PROMPT_EXPERT_LANE - per_group * tg_ref[i], 1)
        acc = jnp.zeros(ys_ref.shape, F32)
        for s in range(per_group):
            h = (_silu(jnp.dot(xb, w1_ref[s], preferred_element_type=F32))
                 * jnp.dot(xb, w3_ref[s], preferred_element_type=F32) * gates[:, s:s + 1])
            acc = acc + jnp.dot(h.astype(BF16), w2_s[s], preferred_element_type=F32)
        ys_ref[...] = acc.astype(BF16)


def _moe_combine_kernel(cnt_ref, gates_ref, x1_ref, gatesb_ref, x1b_ref, gf_ref, ys_hbm, y_ref, yb_ref,
                        ybuf, sem, gstart_s, *, n_groups, n_first):
    j = pl.program_id(0)
    n_tiles = pl.num_programs(0)
    n_chunks = SORT_ROWS // SEG
    slot = lax.rem(j, 2)

    def copies(slot_, c, row):
        return pltpu.make_async_copy(ys_hbm.at[pl.ds(row, SEG), :], ybuf.at[slot_, pl.ds(c * SEG, SEG), :],
                                     sem.at[slot_])

    def fetch_tile(tile, slot_):
        padded, starts, _ = _tile_segments(cnt_ref, tile, n_groups)
        gstart = [gstart_s[g] for g in range(n_groups)]
        for c, row in enumerate(_chunk_rows(n_chunks, starts, gstart)):
            copies(slot_, c, row).start()
        for g in range(n_groups):
            gstart_s[g] = gstart[g] + padded[g]

    @pl.when(j == 0)
    def _():
        bases, _, _ = _group_bases(cnt_ref, n_tiles, n_groups)
        for g in range(n_groups):
            gstart_s[g] = bases[g]
        fetch_tile(0, 0)

    @pl.when(j + 1 < n_tiles)
    def _():
        fetch_tile(j + 1, 1 - slot)

    for c in range(n_chunks):
        copies(slot, c, 0).wait()
    _, starts, _ = _tile_segments(cnt_ref, j, n_groups)
    second = j >= n_first
    sort_t = _sort_matrix(jnp.where(second, gatesb_ref[...], gates_ref[...]), starts)
    moe = jnp.dot(jnp.where(sort_t, 1.0, 0.0).astype(BF16), ybuf[slot], preferred_element_type=F32)
    y = _rms(jnp.where(second, x1b_ref[...], x1_ref[...]) + moe, gf_ref[...])

    @pl.when(jnp.logical_not(second))
    def _():
        y_ref[...] = y

    @pl.when(second)
    def _():
        yb_ref[...] = y


def _moe_sorted(first, second, p):
    x1, xn2, gates, counts = first
    x1b, xn2b, gatesb, countsb = second
    t, d = x1.shape
    tb = x1b.shape[0]
    n_groups, per_group = p["n_groups"], p["per_group"]
    assert n_groups <= MAX_GROUPS and t % TOK_TILE == 0 and tb % TOK_TILE == 0
    n_first = t // TOK_TILE
    n_tiles = n_first + tb // TOK_TILE
    d_exp = p["w1_bf"].shape[2]
    cap = t + tb + n_tiles * n_groups * SEG + n_groups * ROW_TILE + SORT_ROWS
    n_rt = -(-cap // ROW_TILE)
    rows = n_rt * ROW_TILE
    cnt = jnp.concatenate([counts[..., :n_groups].reshape(-1), countsb[..., :n_groups].reshape(-1)])
    params = pltpu.CompilerParams(dimension_semantics=("arbitrary",), vmem_limit_bytes=VMEM_LIMIT_BYTES)
    any_spec = pl.BlockSpec(memory_space=pl.ANY)
    tok = lambda w: pl.BlockSpec((TOK_TILE, w), lambda j, c: (jnp.minimum(j, n_first - 1), 0))
    tokb = lambda w: pl.BlockSpec((TOK_TILE, w), lambda j, c: (jnp.maximum(j - n_first, 0), 0))

    xs, gs, tile_group = pl.pallas_call(
        functools.partial(_moe_dispatch_kernel, n_groups=n_groups, per_group=per_group, n_first=n_first),
        grid_spec=pltpu.PrefetchScalarGridSpec(
            num_scalar_prefetch=1, grid=(n_tiles,),
            in_specs=[tok(d), tok(ROUTER_LANES), tokb(d), tokb(ROUTER_LANES)],
            out_specs=(any_spec, any_spec, pl.BlockSpec(memory_space=pltpu.SMEM)),
            scratch_shapes=[pltpu.VMEM((2, SORT_ROWS, d), BF16), pltpu.VMEM((2, SORT_ROWS, ROUTER_LANES), F32),
                            pltpu.SemaphoreType.DMA((2,)), pltpu.SMEM((n_groups,), jnp.int32),
                            pltpu.VMEM((ROW_TILE, d), BF16), pltpu.VMEM((ROW_TILE, ROUTER_LANES), F32),
                            pltpu.SemaphoreType.DMA((1,))]),
        out_shape=(jax.ShapeDtypeStruct((rows, d), BF16), jax.ShapeDtypeStruct((rows, ROUTER_LANES), F32),
                   jax.ShapeDtypeStruct((n_rt + 1,), jnp.int32)),
        compiler_params=params, name="moe_dispatch",
    )(cnt, xn2, gates, xn2b, gatesb)

    used_tile = lambda i, tg: (jnp.minimum(i, tg[n_rt] - 1), 0)
    group_w = lambda i, tg: (jnp.minimum(tg[i], n_groups - 1), 0, 0)
    ys = pl.pallas_call(
        functools.partial(_moe_group_kernel, n_groups=n_groups, per_group=per_group),
        grid_spec=pltpu.PrefetchScalarGridSpec(
            num_scalar_prefetch=1, grid=(n_rt,),
            in_specs=[pl.BlockSpec((ROW_TILE, d), used_tile), pl.BlockSpec((ROW_TILE, ROUTER_LANES), used_tile),
                      pl.BlockSpec((per_group, d, d_exp), group_w), pl.BlockSpec((per_group, d, d_exp), group_w),
                      pl.BlockSpec((per_group, d_exp, d), group_w)],
            out_specs=pl.BlockSpec((ROW_TILE, d), used_tile),
            scratch_shapes=[pltpu.VMEM((per_group, d_exp, d), BF16)]),
        out_shape=jax.ShapeDtypeStruct((rows, d), BF16),
        input_output_aliases={1: 0},
        compiler_params=params, name="moe_experts",
    )(tile_group, xs, gs, p["w1_bf"], p["w3_bf"], p["w2"])

    return pl.pallas_call(
        functools.partial(_moe_combine_kernel, n_groups=n_groups, n_first=n_first),
        grid_spec=pltpu.PrefetchScalarGridSpec(
            num_scalar_prefetch=1, grid=(n_tiles,),
            in_specs=[tok(ROUTER_LANES), tok(d), tokb(ROUTER_LANES), tokb(d),
                      pl.BlockSpec(p["gf"].shape, lambda j, c: (0, 0)), any_spec],
            out_specs=(tok(d), tokb(d)),
            scratch_shapes=[pltpu.VMEM((2, SORT_ROWS, d), BF16), pltpu.SemaphoreType.DMA((2,)),
                            pltpu.SMEM((n_groups,), jnp.int32)]),
        out_shape=(jax.ShapeDtypeStruct((t, d), F32), jax.ShapeDtypeStruct((tb, d), F32)),
        compiler_params=params, name="moe_combine",
    )(cnt, gates, x1, gatesb, x1b, p["gf"], ys)


def _sample_in_kernel(x_ref, c0_ref, c1_ref, c2_ref, h0_ref, lbw_ref, g1_ref, win_ref, cw_ref, cb_ref,
                      wx_ref, bx_ref, wa_ref, ba_ref, lam_ref,
                      q_ref, f_ref, v_ref, g_ref, yb_ref, hnew_ref, xr_ref):
    wa_w = v_ref.shape[1]
    wb_w = yb_ref.shape[1]
    xn = _rms(x_ref[...], g1_ref[...])
    proj = jnp.dot(xn, win_ref[...], precision=HIGHEST, preferred_element_type=F32)
    lb = _forget_lower_bound(lbw_ref[...])
    f = lb + (1.0 - lb) * _sigmoid(proj[:, wa_w:2 * wa_w])
    q_ref[...] = proj[:, 0:wa_w].T
    f_ref[...] = f.T
    v_ref[...] = proj[:, 2 * wa_w:3 * wa_w]
    g_ref[...] = proj[:, 3 * wa_w:4 * wa_w]
    xr = proj[:, 4 * wa_w:4 * wa_w + wb_w]
    xr_ref[...] = xr
    xc = (cb_ref[...] + cw_ref[0:1, :] * c0_ref[...] + cw_ref[1:2, :] * c1_ref[...]
          + cw_ref[2:3, :] * c2_ref[...] + cw_ref[3:4, :] * xr)
    gate_x = _sigmoid(jnp.dot(xc, wx_ref[...], precision=HIGHEST, preferred_element_type=F32) + bx_ref[...])
    gate_a = _sigmoid(jnp.dot(xc, wa_ref[...], precision=HIGHEST, preferred_element_type=F32) + ba_ref[...])
    log_a = (-LRU_C) * gate_a * _softplus(-lam_ref[...])
    a = jnp.exp(log_a)
    mult = jnp.sqrt(-_expm1(2.0 * log_a))
    h = a * h0_ref[...] + gate_x * xc * mult
    hnew_ref[...] = h
    yb_ref[...] = h * _gelu_tanh(proj[:, 4 * wa_w + wb_w:4 * wa_w + 2 * wb_w])


def _sample_state_kernel(s_ref, qt_ref, ft_ref, v_ref, snew_ref, o_ref):
    tb, n_heads = s_ref.shape[0], s_ref.shape[1]
    n_tok = qt_ref.shape[1]
    shift = lax.rem(n_tok - pl.program_id(0) * tb, n_tok)
    assert 3 * tb <= HEAD_DIM
    lane = lax.broadcasted_iota(jnp.int32, (HEAD_DIM, n_tok), 1)
    r_e = lax.broadcasted_iota(jnp.int32, (n_tok, tb * HEAD_DIM), 0)
    c_e = lax.broadcasted_iota(jnp.int32, (n_tok, tb * HEAD_DIM), 1)
    spread = jnp.where((r_e < 3 * tb) & (lax.rem(r_e, tb) == c_e // HEAD_DIM), 1.0, 0.0).astype(BF16)

    def across_lanes(cols):
        hi = cols.astype(BF16).astype(F32)
        rest = cols - hi
        mid = rest.astype(BF16).astype(F32)
        low = rest - mid
        pieces = jnp.where(lane < tb, hi, jnp.where(lane < 2 * tb, pltpu.roll(mid, tb, 1),
                                                    jnp.where(lane < 3 * tb, pltpu.roll(low, 2 * tb, 1), 0.0)))
        return jnp.dot(pieces.astype(BF16), spread, preferred_element_type=F32)

    for h in range(n_heads):
        hs = slice(h * HEAD_DIM, (h + 1) * HEAD_DIM)
        q_all = across_lanes(pltpu.roll(qt_ref[hs, :], shift, 1))
        f_all = across_lanes(pltpu.roll(ft_ref[hs, :], shift, 1))
        rows = []
        for t in range(tb):
            ts = slice(t * HEAD_DIM, (t + 1) * HEAD_DIM)
            s_new = f_all[:, ts] * s_ref[t, h] + (1.0 - f_all[:, ts]) * v_ref[t:t + 1, hs]
            snew_ref[t, h] = s_new
            rows.append(jnp.sum(q_all[:, ts] * s_new, axis=0, keepdims=True))
        o_ref[:, hs] = jnp.concatenate(rows, axis=0)


def _sample_out_kernel(x_ref, o_ref, g_ref, yb_ref, hgg_ref, wout_ref, g2_ref, wr_ref, br_ref,
                       x1_ref, xn2_ref, gates_ref, cnt_ref, *, n_groups, per_group):
    wa_w = o_ref.shape[1]
    ya = []
    for h in range(wa_w // HEAD_DIM):
        hs = slice(h * HEAD_DIM, (h + 1) * HEAD_DIM)
        oh = o_ref[:, hs]
        oh = oh * lax.rsqrt(jnp.mean(oh * oh, axis=-1, keepdims=True) + EPS) * hgg_ref[:, hs]
        ya.append(oh * _silu(g_ref[:, hs]))
    y = jnp.concatenate(ya + [yb_ref[...]], axis=-1)
    x1 = x_ref[...] + jnp.dot(y, wout_ref[...], precision=HIGHEST, preferred_element_type=F32)
    xn2 = _rms(x1, g2_ref[...])
    logits_t = lax.dot_general(wr_ref[...], xn2, NT_DIMS, precision=HIGHEST,
                               preferred_element_type=F32) + br_ref[...]
    gates_t, g_idx = _route_rows(logits_t, n_groups, per_group)
    n = x1.shape[0]
    pad = x1_ref.shape[0] - n
    x1_ref[0:n, :] = x1
    xn2_ref[0:n, :] = xn2.astype(BF16)
    gates_ref[0:n, :] = gates_t.T
    if pad:
        x1_ref[n:, :] = jnp.zeros((pad, x1.shape[1]), F32)
        xn2_ref[n:, :] = jnp.zeros((pad, x1.shape[1]), BF16)
        gates_ref[n:, :] = jnp.zeros((pad, ROUTER_LANES), F32)
    lane_c = lax.broadcasted_iota(jnp.int32, (1, ROUTER_LANES), 1)
    tok = lax.broadcasted_iota(jnp.int32, g_idx.shape, 1)
    n_tiles = cnt_ref.shape[0]
    for t in range(n_tiles):
        in_tile = (tok >= t * TOK_TILE) & (tok < (t + 1) * TOK_TILE)
        cnt = jnp.where(lane_c == 0, pad if t == n_tiles - 1 else 0, 0)
        for g in range(n_groups):
            here = jnp.sum(jnp.where(in_tile & (g_idx == g), 1, 0), axis=1, keepdims=True)
            cnt = cnt + jnp.where(lane_c == g, here, 0)
        cnt_ref[t:t + 1, :] = cnt


def _whole(kernel, out_shape, *args, name):
    return pl.pallas_call(
        kernel, out_shape=out_shape,
        compiler_params=pltpu.CompilerParams(vmem_limit_bytes=VMEM_LIMIT_BYTES), name=name)(*args)


def _mixer_sample(x, s0, h0, c0, p, tb):
    n, d = x.shape
    wa_w = p["hgg"].shape[1]
    wb_w = p["cb"].shape[1]
    n_heads = wa_w // HEAD_DIM
    sd = lambda w: jax.ShapeDtypeStruct((n, w), F32)
    key_major = jax.ShapeDtypeStruct((wa_w, n), F32)
    q, f, v, g, yb, h_new, xr = _whole(
        _sample_in_kernel, (key_major,) * 2 + (sd(wa_w),) * 2 + (sd(wb_w),) * 3,
        x, c0[:, 0, :], c0[:, 1, :], c0[:, 2, :], h0, p["lbw"], p["g1"], p["win"], p["cw"], p["cb"],
        p["wx"], p["bx"], p["wa"], p["ba"], p["lam"], name="sample_in")
    tok = lambda w: pl.BlockSpec((tb, w), lambda i: (i, 0))
    st = pl.BlockSpec((tb, n_heads, HEAD_DIM, HEAD_DIM), lambda i: (i, 0, 0, 0))
    s_new, o = pl.pallas_call(
        _sample_state_kernel,
        grid=(n // tb,),
        in_specs=[st, _const_spec((wa_w, n)), _const_spec((wa_w, n)), tok(wa_w)],
        out_specs=(st, tok(wa_w)),
        out_shape=(jax.ShapeDtypeStruct(s0.shape, F32), sd(wa_w)),
        compiler_params=pltpu.CompilerParams(dimension_semantics=("arbitrary",),
                                             vmem_limit_bytes=VMEM_LIMIT_BYTES),
        name="sample_state",
    )(s0, q, f, v)
    n_pad = -(-n // TOK_TILE) * TOK_TILE
    x1, xn2, gates, counts = _whole(
        functools.partial(_sample_out_kernel, n_groups=p["n_groups"], per_group=p["per_group"]),
        (jax.ShapeDtypeStruct((n_pad, d), F32), jax.ShapeDtypeStruct((n_pad, d), BF16),
         jax.ShapeDtypeStruct((n_pad, ROUTER_LANES), F32),
         jax.ShapeDtypeStruct((n_pad // TOK_TILE, ROUTER_LANES), jnp.int32)),
        x, o, g, yb, p["hgg"], p["wout"], p["g2"], p["wr_t"], p["br_t"], name="sample_out")
    c_new = jnp.stack([c0[:, 1, :], c0[:, 2, :], xr], axis=1)
    return x1, xn2, gates, counts, s_new, h_new, c_new


def _block_diag(w):
    n, c, _ = w.shape
    eye = jnp.eye(n, dtype=w.dtype)
    return (w[:, :, None, :] * eye[:, None, :, None]).reshape(n * c, n * c)


def _prepare(lower_bounds, ln1_g, w_in, hgrn_norm_g, conv_w, conv_b, lru_wx, lru_bx, lru_wa, lru_ba,
             lru_lambda, w_out, ln2_g, router_wg, router_bg, router_we, router_be, exp_w1, exp_w3,
             exp_w2, final_g):
    d = w_in.shape[1]
    n_groups = router_wg.shape[-1]
    per_group = router_we.shape[-1]
    row = lambda a: a.reshape(1, -1).astype(F32)
    we = jnp.transpose(router_we[0], (1, 0, 2)).reshape(d, n_groups * per_group)
    rows_t = -(-(8 + n_groups * per_group) // 16) * 16
    wr_t = jnp.concatenate([router_wg[0].T, jnp.zeros((8 - n_groups, d), F32), we.T,
                            jnp.zeros((rows_t - 8 - n_groups * per_group, d), F32)], axis=0)
    br_t = jnp.concatenate([router_bg[0], jnp.zeros((8 - n_groups,), F32), router_be[0].reshape(-1),
                            jnp.zeros((rows_t - 8 - n_groups * per_group,), F32)]).reshape(-1, 1)
    wx = _block_diag(lru_wx[0])
    wa = _block_diag(lru_wa[0])
    return dict(
        wr_t=wr_t, wr_t_bf=wr_t.astype(BF16), br_t=br_t,
        n_groups=n_groups, per_group=per_group,
        lbw=lower_bounds.astype(F32), g1=row(ln1_g[0]), win=w_in[0],
        hgg=row(hgrn_norm_g[0]), cw=conv_w[0], cb=row(conv_b[0]),
        wx=wx, wx_bf=wx.astype(BF16), bx=row(lru_bx[0]), wa=wa, wa_bf=wa.astype(BF16), ba=row(lru_ba[0]),
        lam=row(lru_lambda[0]), wout=w_out[0], g2=row(ln2_g[0]),
        w1_bf=exp_w1[0].astype(BF16), w3_bf=exp_w3[0].astype(BF16),
        w2=exp_w2[0], gf=row(final_g))


def kernel(x_prompt, x_sample, state_hgrn, state_rglru, state_conv, lower_bounds, ln1_g, w_in, hgrn_norm_g, conv_w, conv_b, lru_wx, lru_bx, lru_wa, lru_ba, lru_lambda, w_out, ln2_g, router_wg, router_bg, router_we, router_be, exp_w1, exp_w3, exp_w2, final_g):
    assert w_in.shape[0] == 1, "single-layer trunk"
    p = _prepare(lower_bounds, ln1_g, w_in, hgrn_norm_g, conv_w, conv_b, lru_wx, lru_bx, lru_wa, lru_ba,
                 lru_lambda, w_out, ln2_g, router_wg, router_bg, router_we, router_be, exp_w1, exp_w3,
                 exp_w2, final_g)
    bsz, seq, d = x_prompt.shape
    x1, xn2, gates, counts, s_p, h_p, c_p = _mixer_prompt(x_prompt, p, min(MIXER_BLOCK, seq))
    t = bsz * seq
    n = x_sample.shape[0]
    x1s, xn2s, gates_s, counts_s, s_s, h_s, c_s = _mixer_sample(
        x_sample[:, 0, :], state_hgrn[0], state_rglru[0], state_conv[0], p, SAMPLE_STEP_TOKENS)
    y_p, y_s = _moe_sorted((x1.reshape(t, d), xn2.reshape(t, d), gates.reshape(t, ROUTER_LANES), counts),
                           (x1s, xn2s, gates_s, counts_s), p)
    return (y_p.reshape(bsz, seq, d), y_s[:n].reshape(n, 1, d),
            s_p[None], h_p.reshape(1, bsz, -1), c_p[None],
            s_s[None], h_s[None], c_s[None])
```

```python
import functools

import jax
import jax.numpy as jnp
from jax import lax
from jax.experimental import pallas as pl
from jax.experimental.pallas import tpu as pltpu

F32 = jnp.float32
BF16 = jnp.bfloat16
HIGHEST = lax.Precision.HIGHEST

EPS = 1e-6
LRU_C = 8.0
LOG2E = 1.4426950408889634
HEAD_DIM = 128
CHUNK = 64
SUB = 16
UNROLL = 8
ROUTER_LANES = 128
PROMPT_EXPERT_LANE = 8
MIXER_BLOCK = 512
SAMPLE_STEP_TOKENS = 8
TOK_TILE = 512
SEG = 16
MAX_GROUPS = 8
SORT_ROWS = TOK_TILE + MAX_GROUPS * SEG
ROW_TILE = 512
VMEM_LIMIT_BYTES = 56 * 1024 * 1024
MIXER_VMEM_LIMIT_BYTES = 60 * 1024 * 1024

NT_DIMS = (((1,), (1,)), ((), ()))
TN_DIMS = (((0,), (0,)), ((), ()))


def _rms(x, g):
    return x * lax.rsqrt(jnp.mean(x * x, axis=-1, keepdims=True) + EPS) * g


def _sigmoid(x):
    return 1.0 / (1.0 + jnp.exp(-x))


def _silu(x):
    return x * _sigmoid(x)


def _gelu_tanh(x):
    c = 0.7978845608028654
    return x * (0.5 * (1.0 + jnp.tanh(c * (x + 0.044715 * (x * x * x)))))


def _softplus(z):
    return jnp.maximum(z, 0.0) + jnp.log1p(jnp.exp(-jnp.abs(z)))


def _expm1(x):
    u = jnp.exp(x)
    um1 = u - 1.0
    small = um1 * x / jnp.log(u)
    return jnp.where(um1 == 0.0, x, jnp.where(jnp.abs(x) < 0.5, small, um1))


def _forget_lower_bound(lbw):
    m = jnp.max(lbw, axis=0, keepdims=True)
    e = jnp.exp(lbw - m)
    return e[0:1, :] / jnp.sum(e, axis=0, keepdims=True)


def _route_rows(logits_t, n_groups, per_group):
    assert n_groups <= 8 and per_group == 8
    n_tok = logits_t.shape[1]
    row = lax.broadcasted_iota(jnp.int32, (8, n_tok), 0)
    neg = jnp.float32(-jnp.inf)
    big = jnp.int32(8)
    lg = jnp.where(row < n_groups, logits_t[0:8], neg)
    mg = jnp.max(lg, axis=0, keepdims=True)
    g_idx = jnp.min(jnp.where(lg == mg, row, big), axis=0, keepdims=True)
    p_top = 1.0 / jnp.sum(jnp.where(row < n_groups, jnp.exp(lg - mg), 0.0), axis=0, keepdims=True)
    sel = logits_t[8:16]
    for g in range(1, n_groups):
        sel = jnp.where(g_idx == g, logits_t[8 + 8 * g:16 + 8 * g], sel)
    m1 = jnp.max(sel, axis=0, keepdims=True)
    i1 = jnp.min(jnp.where(sel == m1, row, big), axis=0, keepdims=True)
    sel2 = jnp.where(row == i1, neg, sel)
    m2 = jnp.max(sel2, axis=0, keepdims=True)
    i2 = jnp.min(jnp.where(sel2 == m2, row, big), axis=0, keepdims=True)
    e2 = jnp.exp(m2 - m1)
    den = 1.0 + e2
    w1 = p_top / den
    w2 = p_top * (e2 / den)
    own = jnp.where(row == i1, w1, 0.0) + jnp.where(row == i2, w2, 0.0)
    blocks = [jnp.where(row == 0, g_idx.astype(F32), 0.0)]
    blocks += [jnp.where(g_idx == g, own, 0.0) for g in range(n_groups)]
    blocks += [jnp.zeros((8, n_tok), F32)] * (ROUTER_LANES // 8 - len(blocks))
    return jnp.concatenate(blocks, axis=0), g_idx


def _mixer_prompt_kernel(x_ref, lbw_ref, g1_ref, win_ref, hgg_ref, cw_ref, cb_ref, wx_ref, bx_ref,
                         wa_ref, ba_ref, lam_ref, wout_ref, g2_ref, wr_ref, br_ref, w1_ref, w3_ref,
                         x1_ref, xn2_ref, gates_ref, cnt_ref, sout_ref, hout_ref, cout_ref, w1o_ref, w3o_ref,
                         proj_s, k_s, b_s, o_s, st_s, xr_s, a_s, u_s, hcar_s, yb_out_s, win_s, wout_s,
                         *, n_groups, per_group):
    lb_t = x_ref.shape[1]
    wa_w = o_s.shape[1]
    wb_w = a_s.shape[1]
    n_heads = wa_w // HEAD_DIM
    j = pl.program_id(1)
    nj = pl.num_programs(1)

    @pl.when((pl.program_id(0) == 0) & (j == 0))
    def _():
        for c in range(0, win_ref.shape[1], wa_w):
            win_s[:, c:c + wa_w] = win_ref[:, c:c + wa_w].astype(BF16)
        wout_s[...] = wout_ref[...].astype(BF16)

    @pl.when(j == 0)
    def _():
        st_s[...] = jnp.zeros_like(st_s)
        hcar_s[...] = jnp.zeros_like(hcar_s)
        xr_s[0:8, :] = jnp.zeros((8, wb_w), F32)

    x = x_ref[0]
    xn = _rms(x, g1_ref[...]).astype(BF16)
    xb0 = 4 * wa_w

    def project(pieces):
        for c, w in pieces:
            proj_s[:, c:c + w] = jnp.dot(xn, win_s[:, c:c + w], preferred_element_type=F32)

    project(((wa_w, wa_w), (xb0, wb_w), (xb0 + wb_w, wb_w), (0, wa_w), (2 * wa_w, wa_w), (3 * wa_w, wa_w)))

    w1o_ref[...] = w1_ref[...].astype(BF16)
    w3o_ref[...] = w3_ref[...].astype(BF16)

    xr_s[pl.ds(8, lb_t), :] = proj_s[:, xb0:xb0 + wb_w]
    xc = (cb_ref[...] + cw_ref[3:4, :] * xr_s[pl.ds(8, lb_t), :] + cw_ref[2:3, :] * xr_s[pl.ds(7, lb_t), :]
          + cw_ref[1:2, :] * xr_s[pl.ds(6, lb_t), :] + cw_ref[0:1, :] * xr_s[pl.ds(5, lb_t), :])
    tail = xr_s[pl.ds(lb_t + 5, 3), :]
    xr_s[5:8, :] = tail
    cout_ref[0] = tail
    xcb = xc.astype(BF16)
    gate_x = _sigmoid(jnp.dot(xcb, wx_ref[...], preferred_element_type=F32) + bx_ref[...])
    gate_a = _sigmoid(jnp.dot(xcb, wa_ref[...], preferred_element_type=F32) + ba_ref[...])
    log_a =(-LRU_C) * gate_a * _softplus(-lam_ref[...])
    a = jnp.exp(log_a)
    mult = jnp.sqrt((1.0 - a) * (1.0 + a))
    first = (lax.broadcasted_iota(jnp.int32, (lb_t, 1), 0) == 0) & (j == 0)
    a = jnp.where(first, 0.0, a)
    mult = jnp.where(first, 1.0, mult)
    a_s[...] = a
    u_s[...] = gate_x * xc * mult

    lb = _forget_lower_bound(lbw_ref[...])
    f = lb + (1.0 - lb) * _sigmoid(proj_s[:, wa_w:2 * wa_w])
    for h in range(n_heads):
        k_s[h] = 1.0 - f[:, h * HEAD_DIM:(h + 1) * HEAD_DIM]
    logf = jnp.log(f)
    r_i = lax.broadcasted_iota(jnp.int32, (CHUNK, CHUNK), 0)
    c_i = lax.broadcasted_iota(jnp.int32, (CHUNK, CHUNK), 1)
    tri = jnp.where(r_i >= c_i, 1.0, 0.0).astype(BF16)
    lf_hi = logf.astype(BF16)
    rest = logf - lf_hi.astype(F32)
    lf_mid = rest.astype(BF16)
    lf_lo = (rest - lf_mid.astype(F32)).astype(BF16)
    for c in range(0, lb_t, CHUNK):
        cum = [jnp.dot(tri, part[c:c + CHUNK, :], preferred_element_type=F32) for part in (lf_lo, lf_mid, lf_hi)]
        b_all = LOG2E * ((cum[0] + cum[1]) + cum[2])
        for h in range(n_heads):
            b_s[h, c:c + CHUNK, :] = b_all[:, h * HEAD_DIM:(h + 1) * HEAD_DIM]

    row_sub = lax.broadcasted_iota(jnp.int32, (SUB, HEAD_DIM), 0)
    lane_sub = lax.broadcasted_iota(jnp.int32, (SUB, HEAD_DIM), 1)
    assert n_heads % 2 == 0
    r_kk = lax.broadcasted_iota(jnp.int32, (2 * HEAD_DIM, 2 * HEAD_DIM), 0)
    c_kk = lax.broadcasted_iota(jnp.int32, (2 * HEAD_DIM, 2 * HEAD_DIM), 1)
    ones_kk = jnp.where((r_kk < HEAD_DIM) == (c_kk < HEAD_DIM), 1.0, 0.0).astype(BF16)

    n_sub = CHUNK // SUB
    half = SUB // 2
    lower_left = (row_sub >= half) & (lane_sub < half)
    own_lane = jnp.where(row_sub >= half, half, 0)

    def chunk_start(r0):
        first = []
        for h in range(n_heads):
            hs = slice(h * HEAD_DIM, (h + 1) * HEAD_DIM)
            q = proj_s[pl.ds(r0, CHUNK), hs]
            b = b_s[h, pl.ds(r0, CHUNK), :]
            v = proj_s[pl.ds(r0, CHUNK), 2 * wa_w + h * HEAD_DIM:2 * wa_w + (h + 1) * HEAD_DIM]
            k = k_s[h, pl.ds(r0, CHUNK), :]

            def key_rows(ref, lo, j):
                return jnp.stack([jnp.broadcast_to(ref[h, pl.ds(r0 + lo + hf * half + j, 1), :], (half, HEAD_DIM))
                                  for hf in range(2)])
            vb = v.astype(BF16)
            st = st_s[h]
            b_last = b[CHUNK - 1:CHUNK, :]
            o = lax.dot_general((q * jnp.exp2(b)).astype(BF16), st.astype(BF16), NT_DIMS,
                                preferred_element_type=F32)
            k_end = k * jnp.exp2(b_last - b)
            st_s[h] = st * jnp.exp2(b_last) + lax.dot_general(vb, k_end.astype(BF16), TN_DIMS,
                                                               preferred_element_type=F32)
            terms, off, mid = [], [], []
            for i in range(n_sub):
                lo = i * SUB
                qi, bi, ki = q[lo:lo + SUB], b[lo:lo + SUB], k[lo:lo + SUB]
                q3, b3 = (a.reshape(2, half, HEAD_DIM) for a in (qi, bi))
                terms += [(q3 * (key_rows(k_s, lo, j) * jnp.exp2(b3 - key_rows(b_s, lo, j))))
                          .reshape(SUB, HEAD_DIM).astype(BF16) for j in range(half)]
                rm = bi[half - 1:half]
                mid.append(lax.dot_general((qi * jnp.exp2(bi - rm)).astype(BF16),
                                           (ki * jnp.exp2(rm - bi)).astype(BF16), NT_DIMS,
                                           preferred_element_type=F32))
                if i > 0:
                    r = b[lo - 1:lo]
                    qt = (qi * jnp.exp2(bi - r)).astype(BF16)
                    kt = (k[:lo] * jnp.exp2(r - b[:lo])).astype(BF16)
                    off.append(lax.dot_general(qt, kt, NT_DIMS, preferred_element_type=F32))
            first.append([o, vb, off, jnp.concatenate(terms, axis=0), mid])
        for h in range(0, n_heads, 2):
            both = jnp.concatenate([first[h][3], first[h + 1][3]], axis=1)
            sums = jnp.dot(both, ones_kk, preferred_element_type=F32)
            first[h][3] = sums[:, :HEAD_DIM]
            first[h + 1][3] = sums[:, HEAD_DIM:]
        return first

    def chunk_finish(r0, first):
        for h in range(n_heads):
            o, vb, off, sums, mid = first[h]
            outs = []
            for i in range(n_sub):
                lo = i * SUB
                sc = jnp.zeros((SUB, HEAD_DIM), F32)
                for j in range(half):
                    row0 = (i * half + j) * SUB
                    sc = jnp.where(lane_sub == own_lane + j, sums[row0:row0 + SUB], sc)
                sc = jnp.where(row_sub >= lane_sub, sc, 0.0)[:, :SUB]
                sc = jnp.where(lower_left[:, :SUB], mid[i], sc)
                od = o[lo:lo + SUB] + jnp.dot(sc.astype(BF16), vb[lo:lo + SUB], preferred_element_type=F32)
                if i > 0:
                    od = od + jnp.dot(off[i - 1].astype(BF16), vb[:lo], preferred_element_type=F32)
                outs.append(od)
            o_s[pl.ds(r0, CHUNK), h * HEAD_DIM:(h + 1) * HEAD_DIM] = jnp.concatenate(outs, axis=0)

    def chunks_body(ci, carry):
        rows = [pl.multiple_of((ci * UNROLL + u) * CHUNK, CHUNK) for u in range(UNROLL)]
        started = [chunk_start(r0) for r0 in rows]
        for r0, first in zip(rows, started):
            chunk_finish(r0, first)
        return carry

    row8 = lax.broadcasted_iota(jnp.int32, (8, 1), 0)

    def scan_body(gi, carry):
        r0 = pl.multiple_of(gi * 8, 8)
        aa = a_s[pl.ds(r0, 8), :]
        uu = u_s[pl.ds(r0, 8), :]
        for s in (1, 2, 4):
            m = row8 >= s
            uu = jnp.where(m, aa * pltpu.roll(uu, s, 0) + uu, uu)
            aa = jnp.where(m, aa * pltpu.roll(aa, s, 0), aa)
        hh = aa * carry + uu
        u_s[pl.ds(r0, 8), :] = hh
        return hh[7:8, :]

    h_last = lax.fori_loop(0, lb_t // 8, scan_body, hcar_s[...])
    hcar_s[...] = h_last
    hout_ref[0] = h_last
    yb = (u_s[...] * _gelu_tanh(proj_s[:, xb0 + wb_w:xb0 + 2 * wb_w])).astype(BF16)
    yb_out_s[...] = jnp.dot(yb, wout_s[wa_w:, :], preferred_element_type=F32)

    assert lb_t % (CHUNK * UNROLL) == 0
    lax.fori_loop(0, lb_t // (CHUNK * UNROLL), chunks_body, 0)

    ya = []
    for h in range(n_heads):
        hs = slice(h * HEAD_DIM, (h + 1) * HEAD_DIM)
        oh = o_s[:, hs]
        oh = oh * lax.rsqrt(jnp.mean(oh * oh, axis=-1, keepdims=True) + EPS) * hgg_ref[:, hs]
        ya.append(oh * _silu(proj_s[:, 3 * wa_w + h * HEAD_DIM:3 * wa_w + (h + 1) * HEAD_DIM]))

    ya = jnp.concatenate(ya, axis=-1).astype(BF16)
    x1 = x + (jnp.dot(ya, wout_s[:wa_w, :], preferred_element_type=F32) + yb_out_s[...])
    x1_ref[0] = x1
    xn2 = _rms(x1, g2_ref[...]).astype(BF16)
    xn2_ref[0] = xn2
    logits_t = lax.dot_general(wr_ref[...], xn2, NT_DIMS, preferred_element_type=F32) + br_ref[...]
    gates_t, g_idx = _route_rows(logits_t, n_groups, per_group)
    gates_ref[0] = gates_t.T
    lane_c = lax.broadcasted_iota(jnp.int32, (1, ROUTER_LANES), 1)
    for t in range(lb_t // TOK_TILE):
        gi = g_idx[:, t * TOK_TILE:(t + 1) * TOK_TILE]
        cnt = jnp.zeros((1, ROUTER_LANES), jnp.int32)
        for g in range(n_groups):
            cnt = cnt + jnp.where(lane_c == g, jnp.sum(jnp.where(gi == g, 1, 0), axis=1, keepdims=True), 0)
        cnt_ref[0, t:t + 1, :] = cnt

    @pl.when(j == nj - 1)
    def _():
        for h in range(n_heads):
            sout_ref[0, h] = st_s[h].T


def _const_spec(shape):
    nd = len(shape)
    return pl.BlockSpec(shape, lambda *_: (0,) * nd)


def _mixer_prompt(x, p, lb_t):
    bsz, seq, d = x.shape
    wa_w = p["hgg"].shape[1]
    wb_w = p["cb"].shape[1]
    n_heads = wa_w // HEAD_DIM
    n_cols = p["win"].shape[1]
    weights = [p["lbw"], p["g1"], p["win"], p["hgg"], p["cw"], p["cb"], p["wx_bf"], p["bx"],
               p["wa_bf"], p["ba"], p["lam"], p["wout"], p["g2"], p["wr_t_bf"], p["br_t"]]
    nj = seq // lb_t
    tile = lambda w: pl.BlockSpec((1, lb_t, w), lambda b, j: (b, j, 0))
    n_exp = p["w1"].shape[0]
    per_step = next(k for k in range(-(-n_exp // (bsz * nj)), n_exp + 1) if n_exp % k == 0)
    exp_spec = pl.BlockSpec((per_step,) + p["w1"].shape[1:],
                            lambda b, j: (jnp.minimum(b * nj + j, n_exp // per_step - 1), 0, 0))
    out_shape = (
        jax.ShapeDtypeStruct((bsz, seq, d), F32),
        jax.ShapeDtypeStruct((bsz, seq, d), BF16),
        jax.ShapeDtypeStruct((bsz, seq, ROUTER_LANES), F32),
        jax.ShapeDtypeStruct((bsz * (seq // lb_t), lb_t // TOK_TILE, ROUTER_LANES), jnp.int32),
        jax.ShapeDtypeStruct((bsz, n_heads, HEAD_DIM, HEAD_DIM), F32),
        jax.ShapeDtypeStruct((bsz, 1, wb_w), F32),
        jax.ShapeDtypeStruct((bsz, 3, wb_w), F32),
        jax.ShapeDtypeStruct(p["w1"].shape, BF16),
        jax.ShapeDtypeStruct(p["w3"].shape, BF16),
    )
    out_specs = (
        tile(d), tile(d), tile(ROUTER_LANES),
        pl.BlockSpec((1, lb_t // TOK_TILE, ROUTER_LANES), lambda b, j: (b * nj + j, 0, 0)),
        pl.BlockSpec((1, n_heads, HEAD_DIM, HEAD_DIM), lambda b, j: (b, 0, 0, 0)),
        pl.BlockSpec((1, 1, wb_w), lambda b, j: (b, 0, 0)),
        pl.BlockSpec((1, 3, wb_w), lambda b, j: (b, 0, 0)),
        exp_spec, exp_spec,
    )
    scratch = [
        pltpu.VMEM((lb_t, n_cols), F32),
        pltpu.VMEM((n_heads, lb_t, HEAD_DIM), F32),
        pltpu.VMEM((n_heads, lb_t, HEAD_DIM), F32),
        pltpu.VMEM((lb_t, wa_w), F32),
        pltpu.VMEM((n_heads, HEAD_DIM, HEAD_DIM), F32),
        pltpu.VMEM((lb_t + 8, wb_w), F32),
        pltpu.VMEM((lb_t, wb_w), F32),
        pltpu.VMEM((lb_t, wb_w), F32),
        pltpu.VMEM((1, wb_w), F32),
        pltpu.VMEM((lb_t, d), F32),
        pltpu.VMEM((d, n_cols), BF16),
        pltpu.VMEM(p["wout"].shape, BF16),
    ]
    weight_specs = [pl.BlockSpec(w.shape, lambda *_, nd=w.ndim: (0,) * nd, pipeline_mode=pl.Buffered(1))
                    for w in weights]
    kern = functools.partial(_mixer_prompt_kernel, n_groups=p["n_groups"], per_group=p["per_group"])
    return pl.pallas_call(
        kern,
        grid=(bsz, nj),
        in_specs=[tile(d)] + weight_specs + [exp_spec, exp_spec],
        out_specs=out_specs,
        out_shape=out_shape,
        scratch_shapes=scratch,
        compiler_params=pltpu.CompilerParams(dimension_semantics=("arbitrary", "arbitrary"),
                                             vmem_limit_bytes=MIXER_VMEM_LIMIT_BYTES),
        name="mixer_prompt",
    )(x, *weights, p["w1"], p["w3"])


def _seg_pad(n):
    return jnp.bitwise_and(n + (SEG - 1), -SEG)


def _row_tile_pad(n):
    return jnp.bitwise_and(n + (ROW_TILE - 1), -ROW_TILE)


def _tile_segments(cnt_ref, tile, n_groups):
    padded = [_seg_pad(cnt_ref[tile * n_groups + g]) for g in range(n_groups)]
    starts, acc = [], jnp.int32(0)
    for g in range(n_groups):
        starts.append(acc)
        acc = acc + padded[g]
    return padded, starts, acc


def _group_bases(cnt_ref, n_tiles, n_groups):
    def body(t, tot):
        return tuple(tot[g] + _seg_pad(cnt_ref[t * n_groups + g]) for g in range(n_groups))
    totals = lax.fori_loop(0, n_tiles, body, (jnp.int32(0),) * n_groups)
    bases, ends, acc = [], [], jnp.int32(0)
    for g in range(n_groups):
        bases.append(acc)
        acc = acc + _row_tile_pad(totals[g])
        ends.append(acc)
    return bases, ends, totals


def _chunk_rows(n_chunks, starts, gstart):
    rows, row = [], None
    for c in range(n_chunks):
        row = gstart[0] if c == 0 else row + SEG
        for g in range(1, len(starts)):
            row = jnp.where(starts[g] == c * SEG, gstart[g], row)
        rows.append(pl.multiple_of(row, SEG))
    return rows


def _sort_matrix(gates, starts):
    n_groups = len(starts)
    col = lax.broadcasted_iota(jnp.int32, (TOK_TILE, ROUTER_LANES), 1)
    gi = gates[:, 0:1].astype(jnp.int32)
    onehot = col == gi
    r_i = lax.broadcasted_iota(jnp.int32, (TOK_TILE, TOK_TILE), 0)
    c_i = lax.broadcasted_iota(jnp.int32, (TOK_TILE, TOK_TILE), 1)
    earlier = jnp.where(r_i > c_i, 1.0, 0.0).astype(BF16)
    before = jnp.dot(earlier, jnp.where(onehot, 1.0, 0.0).astype(BF16), preferred_element_type=F32)
    rank = jnp.sum(jnp.where(onehot, before, 0.0), axis=-1, keepdims=True).astype(jnp.int32)
    base = jnp.zeros((TOK_TILE, 1), jnp.int32)
    for g in range(n_groups):
        base = base + jnp.where(gi == g, starts[g], 0)
    lane = lax.broadcasted_iota(jnp.int32, (TOK_TILE, SORT_ROWS), 1)
    return lane == jnp.where(gi >= 0, base + rank, -1)


def _sort_matrix_rows(gates, starts):
    n_groups = len(starts)
    n_rows = TOK_TILE + n_groups * SEG
    col = lax.broadcasted_iota(jnp.int32, (TOK_TILE, ROUTER_LANES), 1)
    onehot_t = jnp.where(col == gates[:, 0:1].astype(jnp.int32), 1.0, 0.0).T
    r_i = lax.broadcasted_iota(jnp.int32, (TOK_TILE, TOK_TILE), 0)
    c_i = lax.broadcasted_iota(jnp.int32, (TOK_TILE, TOK_TILE), 1)
    later = jnp.where(r_i < c_i, 1.0, 0.0).astype(BF16)
    before = jnp.dot(onehot_t.astype(BF16), later, preferred_element_type=F32)
    g_row = lax.broadcasted_iota(jnp.int32, (ROUTER_LANES, 1), 0)
    base = jnp.zeros((ROUTER_LANES, 1), jnp.int32)
    for g in range(n_groups):
        base = base + jnp.where(g_row == g, starts[g], 0)
    dest = jnp.sum(jnp.where(onehot_t > 0.0, before + base.astype(F32), 0.0), axis=0, keepdims=True)
    placed = jnp.sum(onehot_t, axis=0, keepdims=True) > 0.0
    row = lax.broadcasted_iota(jnp.int32, (n_rows, TOK_TILE), 0)
    return row == jnp.where(placed, dest, -1.0).astype(jnp.int32)


def _dispatch_copies(xbuf, gbuf, xs_hbm, gs_hbm, sem, slot, c, row):
    return (pltpu.make_async_copy(xbuf.at[slot, pl.ds(c * SEG, SEG), :], xs_hbm.at[pl.ds(row, SEG), :], sem.at[slot]),
            pltpu.make_async_copy(gbuf.at[slot, pl.ds(c * SEG, SEG), :], gs_hbm.at[pl.ds(row, SEG), :], sem.at[slot]))


def _moe_dispatch_kernel(cnt_ref, xn2_ref, gates_ref, xn2b_ref, gatesb_ref, xs_hbm, gs_hbm, tg_ref,
                         xbuf, gbuf, sem, gstart_s, zx, zg, zsem, *, n_groups, per_group, n_first):
    j = pl.program_id(0)
    n_tiles = pl.num_programs(0)
    n_chunks = (TOK_TILE + n_groups * SEG) // SEG
    slot = lax.rem(j, 2)

    @pl.when(j == 0)
    def _():
        bases, ends, totals = _group_bases(cnt_ref, n_tiles, n_groups)
        for g in range(n_groups):
            gstart_s[g] = bases[g]
        n_rt = tg_ref.shape[0] - 1
        for i in range(n_rt):
            tg = jnp.int32(0)
            for g in range(n_groups):
                tg = tg + jnp.where(ends[g] <= i * ROW_TILE, 1, 0)
            tg_ref[i] = tg
        n_used = ends[n_groups - 1] // ROW_TILE
        tg_ref[n_rt] = n_used

        zx[...] = jnp.zeros_like(zx)
        zg[...] = jnp.zeros_like(zg)

        def zero_copies(row, n):
            return (pltpu.make_async_copy(zx.at[pl.ds(0, n), :], xs_hbm.at[pl.ds(row, n), :], zsem.at[0]),
                    pltpu.make_async_copy(zg.at[pl.ds(0, n), :], gs_hbm.at[pl.ds(row, n), :], zsem.at[0]))

        def for_each_gap(act):
            for g in range(n_groups):
                tail = bases[g] + totals[g]

                def seg_body(k, c, tail=tail):
                    for cp in zero_copies(pl.multiple_of(tail + k * SEG, SEG), SEG):
                        act(cp)
                    return c
                lax.fori_loop(0, (ends[g] - tail) // SEG, seg_body, 0)

            def tile_body(k, c):
                for cp in zero_copies(pl.multiple_of((n_used + k) * ROW_TILE, ROW_TILE), ROW_TILE):
                    act(cp)
                return c
            lax.fori_loop(0, n_rt - n_used, tile_body, 0)

        for_each_gap(lambda cp: cp.start())
        for_each_gap(lambda cp: cp.wait())

    def wait_tile(tile, slot_):
        _, _, used = _tile_segments(cnt_ref, tile, n_groups)
        for c in range(n_chunks):
            @pl.when(c * SEG < used)
            def _():
                for cp in _dispatch_copies(xbuf, gbuf, xs_hbm, gs_hbm, sem, slot_, c, 0):
                    cp.wait()

    @pl.when(j >= 2)
    def _():
        wait_tile(j - 2, slot)

    padded, starts, used = _tile_segments(cnt_ref, j, n_groups)
    second = j >= n_first
    gates = jnp.where(second, gatesb_ref[...], gates_ref[...])
    xn2 = jnp.where(second, xn2b_ref[...], xn2_ref[...])
    d = xn2_ref.shape[1]
    g_hi = gates.astype(BF16)
    g_lo = (gates - g_hi.astype(F32)).astype(BF16)
    sort_m = jnp.where(_sort_matrix_rows(gates, starts), 1.0, 0.0).astype(BF16)
    moved = jnp.dot(sort_m, jnp.concatenate([xn2, g_hi, g_lo], axis=1), preferred_element_type=F32)
    xbuf[slot] = moved[:, :d].astype(BF16)
    gbuf[slot] = moved[:, d:d + ROUTER_LANES] + moved[:, d + ROUTER_LANES:]
    gstart = [gstart_s[g] for g in range(n_groups)]
    rows = _chunk_rows(n_chunks, starts, gstart)
    for c in range(n_chunks):
        @pl.when(c * SEG < used)
        def _():
            for cp in _dispatch_copies(xbuf, gbuf, xs_hbm, gs_hbm, sem, slot, c, rows[c]):
                cp.start()
    for g in range(n_groups):
        gstart_s[g] = gstart[g] + padded[g]

    @pl.when(j == n_tiles - 1)
    def _():
        @pl.when(j >= 1)
        def _():
            wait_tile(j - 1, 1 - slot)
        wait_tile(j, slot)


def _moe_group_kernel(tg_ref, xs_ref, gs_ref, w1_ref, w3_ref, w2_ref, ys_ref, w2_s, *, n_groups, per_group):
    i = pl.program_id(0)

    @pl.when((i == 0) | (tg_ref[i] != tg_ref[jnp.maximum(i - 1, 0)]))
    def _():
        w2_s[...] = w2_ref[...].astype(BF16)

    @pl.when(tg_ref[i] < n_groups)
    def _():
        xb = xs_ref[...]
        gates = pltpu.roll(gs_ref[...], ROUTER_LANES - PROMPT_EXPERT_LANE - per_group * tg_ref[i], 1)
        acc = jnp.zeros(ys_ref.shape, F32)
        for s in range(per_group):
            h = (_silu(jnp.dot(xb, w1_ref[s], preferred_element_type=F32))
                 * jnp.dot(xb, w3_ref[s], preferred_element_type=F32) * gates[:, s:s + 1])
            acc = acc + jnp.dot(h.astype(BF16), w2_s[s], preferred_element_type=F32)
        ys_ref[...] = acc.astype(BF16)


def _moe_combine_kernel(cnt_ref, gates_ref, x1_ref, gatesb_ref, x1b_ref, gf_ref, ys_hbm, y_ref, yb_ref,
                        ybuf, sem, gstart_s, *, n_groups, n_first):
    j = pl.program_id(0)
    n_tiles = pl.num_programs(0)
    n_chunks = SORT_ROWS // SEG
    slot = lax.rem(j, 2)

    def copies(slot_, c, row):
        return pltpu.make_async_copy(ys_hbm.at[pl.ds(row, SEG), :], ybuf.at[slot_, pl.ds(c * SEG, SEG), :],
                                     sem.at[slot_])

    def fetch_tile(tile, slot_):
        padded, starts, _ = _tile_segments(cnt_ref, tile, n_groups)
        gstart = [gstart_s[g] for g in range(n_groups)]
        for c, row in enumerate(_chunk_rows(n_chunks, starts, gstart)):
            copies(slot_, c, row).start()
        for g in range(n_groups):
            gstart_s[g] = gstart[g] + padded[g]

    @pl.when(j == 0)
    def _():
        bases, _, _ = _group_bases(cnt_ref, n_tiles, n_groups)
        for g in range(n_groups):
            gstart_s[g] = bases[g]
        fetch_tile(0, 0)

    @pl.when(j + 1 < n_tiles)
    def _():
        fetch_tile(j + 1, 1 - slot)

    for c in range(n_chunks):
        copies(slot, c, 0).wait()
    _, starts, _ = _tile_segments(cnt_ref, j, n_groups)
    second = j >= n_first
    sort_t = _sort_matrix(jnp.where(second, gatesb_ref[...], gates_ref[...]), starts)
    moe = jnp.dot(jnp.where(sort_t, 1.0, 0.0).astype(BF16), ybuf[slot], preferred_element_type=F32)

    @pl.when(jnp.logical_not(second))
    def _():
        y_ref[...] = _rms(x1_ref[...] + moe, gf_ref[...])

    @pl.when(second)
    def _():
        yb_ref[...] = _rms(x1b_ref[...] + moe, gf_ref[...])


def _moe_sorted(first, second, w1_bf, w3_bf, p):
    x1, xn2, gates, counts = first
    x1b, xn2b, gatesb, countsb = second
    t, d = x1.shape
    tb = x1b.shape[0]
    n_groups, per_group = p["n_groups"], p["per_group"]
    assert n_groups <= MAX_GROUPS and t % TOK_TILE == 0 and tb % TOK_TILE == 0
    n_first = t // TOK_TILE
    n_tiles = n_first + tb // TOK_TILE
    d_exp = w1_bf.shape[2]
    cap = t + tb + n_tiles * n_groups * SEG + n_groups * ROW_TILE + SORT_ROWS
    n_rt = -(-cap // ROW_TILE)
    rows = n_rt * ROW_TILE
    cnt = jnp.concatenate([counts[..., :n_groups].reshape(-1), countsb[..., :n_groups].reshape(-1)])
    params = pltpu.CompilerParams(dimension_semantics=("arbitrary",), vmem_limit_bytes=VMEM_LIMIT_BYTES)
    any_spec = pl.BlockSpec(memory_space=pl.ANY)
    tok = lambda w: pl.BlockSpec((TOK_TILE, w), lambda j, c: (jnp.minimum(j, n_first - 1), 0))
    tokb = lambda w: pl.BlockSpec((TOK_TILE, w), lambda j, c: (jnp.maximum(j - n_first, 0), 0))

    xs, gs, tile_group = pl.pallas_call(
        functools.partial(_moe_dispatch_kernel, n_groups=n_groups, per_group=per_group, n_first=n_first),
        grid_spec=pltpu.PrefetchScalarGridSpec(
            num_scalar_prefetch=1, grid=(n_tiles,),
            in_specs=[tok(d), tok(ROUTER_LANES), tokb(d), tokb(ROUTER_LANES)],
            out_specs=(any_spec, any_spec, pl.BlockSpec(memory_space=pltpu.SMEM)),
            scratch_shapes=[pltpu.VMEM((2, TOK_TILE + n_groups * SEG, d), BF16),
                            pltpu.VMEM((2, TOK_TILE + n_groups * SEG, ROUTER_LANES), F32),
                            pltpu.SemaphoreType.DMA((2,)), pltpu.SMEM((n_groups,), jnp.int32),
                            pltpu.VMEM((ROW_TILE, d), BF16), pltpu.VMEM((ROW_TILE, ROUTER_LANES), F32),
                            pltpu.SemaphoreType.DMA((1,))]),
        out_shape=(jax.ShapeDtypeStruct((rows, d), BF16), jax.ShapeDtypeStruct((rows, ROUTER_LANES), F32),
                   jax.ShapeDtypeStruct((n_rt + 1,), jnp.int32)),
        compiler_params=params, name="moe_dispatch",
    )(cnt, xn2, gates, xn2b, gatesb)

    used_tile = lambda i, tg: (jnp.minimum(i, tg[n_rt] - 1), 0)
    group_w = lambda i, tg: (jnp.minimum(tg[i], n_groups - 1), 0, 0)
    ys = pl.pallas_call(
        functools.partial(_moe_group_kernel, n_groups=n_groups, per_group=per_group),
        grid_spec=pltpu.PrefetchScalarGridSpec(
            num_scalar_prefetch=1, grid=(n_rt,),
            in_specs=[pl.BlockSpec((ROW_TILE, d), used_tile), pl.BlockSpec((ROW_TILE, ROUTER_LANES), used_tile),
                      pl.BlockSpec((per_group, d, d_exp), group_w), pl.BlockSpec((per_group, d, d_exp), group_w),
                      pl.BlockSpec((per_group, d_exp, d), group_w)],
            out_specs=pl.BlockSpec((ROW_TILE, d), used_tile),
            scratch_shapes=[pltpu.VMEM((per_group, d_exp, d), BF16)]),
        out_shape=jax.ShapeDtypeStruct((rows, d), BF16),
        input_output_aliases={1: 0},
        compiler_params=params, name="moe_experts",
    )(tile_group, xs, gs, w1_bf, w3_bf, p["w2"])

    return pl.pallas_call(
        functools.partial(_moe_combine_kernel, n_groups=n_groups, n_first=n_first),
        grid_spec=pltpu.PrefetchScalarGridSpec(
            num_scalar_prefetch=1, grid=(n_tiles,),
            in_specs=[tok(ROUTER_LANES), tok(d), tokb(ROUTER_LANES), tokb(d),
                      pl.BlockSpec(p["gf"].shape, lambda j, c: (0, 0)), any_spec],
            out_specs=(tok(d), tokb(d)),
            scratch_shapes=[pltpu.VMEM((2, SORT_ROWS, d), BF16), pltpu.SemaphoreType.DMA((2,)),
                            pltpu.SMEM((n_groups,), jnp.int32)]),
        out_shape=(jax.ShapeDtypeStruct((t, d), F32), jax.ShapeDtypeStruct((tb, d), F32)),
        compiler_params=params, name="moe_combine",
    )(cnt, gates, x1, gatesb, x1b, p["gf"], ys)


def _sample_in_kernel(x_ref, c0_ref, c1_ref, c2_ref, h0_ref, lbw_ref, g1_ref, win_ref, cw_ref, cb_ref,
                      wx_ref, bx_ref, wa_ref, ba_ref, lam_ref,
                      q_ref, f_ref, v_ref, g_ref, yb_ref, hnew_ref, xr_ref):
    wa_w = v_ref.shape[1]
    wb_w = yb_ref.shape[1]
    xn = _rms(x_ref[...], g1_ref[...])
    proj = jnp.dot(xn, win_ref[...], precision=HIGHEST, preferred_element_type=F32)
    lb = _forget_lower_bound(lbw_ref[...])
    f = lb + (1.0 - lb) * _sigmoid(proj[:, wa_w:2 * wa_w])
    q_ref[...] = proj[:, 0:wa_w].T
    f_ref[...] = f.T
    v_ref[...] = proj[:, 2 * wa_w:3 * wa_w]
    g_ref[...] = proj[:, 3 * wa_w:4 * wa_w]
    xr = proj[:, 4 * wa_w:4 * wa_w + wb_w]
    xr_ref[...] = xr
    xc = (cb_ref[...] + cw_ref[0:1, :] * c0_ref[...] + cw_ref[1:2, :] * c1_ref[...]
          + cw_ref[2:3, :] * c2_ref[...] + cw_ref[3:4, :] * xr)
    gate_x = _sigmoid(jnp.dot(xc, wx_ref[...], precision=HIGHEST, preferred_element_type=F32) + bx_ref[...])
    gate_a = _sigmoid(jnp.dot(xc, wa_ref[...], precision=HIGHEST, preferred_element_type=F32) + ba_ref[...])
    log_a = (-LRU_C) * gate_a * _softplus(-lam_ref[...])
    a = jnp.exp(log_a)
    mult = jnp.sqrt(-_expm1(2.0 * log_a))
    h = a * h0_ref[...] + gate_x * xc * mult
    hnew_ref[...] = h
    yb_ref[...] = h * _gelu_tanh(proj[:, 4 * wa_w + wb_w:4 * wa_w + 2 * wb_w])


def _sample_state_kernel(s_ref, qt_ref, ft_ref, v_ref, snew_ref, o_ref):
    tb, n_heads = s_ref.shape[0], s_ref.shape[1]
    n_tok = qt_ref.shape[1]
    shift = lax.rem(n_tok - pl.program_id(0) * tb, n_tok)
    assert 3 * tb <= HEAD_DIM
    lane = lax.broadcasted_iota(jnp.int32, (HEAD_DIM, n_tok), 1)
    r_e = lax.broadcasted_iota(jnp.int32, (n_tok, tb * HEAD_DIM), 0)
    c_e = lax.broadcasted_iota(jnp.int32, (n_tok, tb * HEAD_DIM), 1)
    spread = jnp.where((r_e < 3 * tb) & (lax.rem(r_e, tb) == c_e // HEAD_DIM), 1.0, 0.0).astype(BF16)

    def pieces_of(cols):
        hi = cols.astype(BF16).astype(F32)
        rest = cols - hi
        mid = rest.astype(BF16).astype(F32)
        low = rest - mid
        pieces = jnp.where(lane < tb, hi, jnp.where(lane < 2 * tb, pltpu.roll(mid, tb, 1),
                                                    jnp.where(lane < 3 * tb, pltpu.roll(low, 2 * tb, 1), 0.0)))
        return pieces.astype(BF16)

    for h in range(n_heads):
        hs = slice(h * HEAD_DIM, (h + 1) * HEAD_DIM)
        q_p = pieces_of(pltpu.roll(qt_ref[hs, :], shift, 1))
        f_p = pieces_of(pltpu.roll(ft_ref[hs, :], shift, 1))
        rows = []
        for t in range(tb):
            one = spread[:, t * HEAD_DIM:(t + 1) * HEAD_DIM]
            f_t = jnp.dot(f_p, one, preferred_element_type=F32)
            q_t = jnp.dot(q_p, one, preferred_element_type=F32)
            s_new = f_t * s_ref[t, h] + (1.0 - f_t) * v_ref[t:t + 1, hs]
            snew_ref[t, h] = s_new
            rows.append(jnp.sum(q_t * s_new, axis=0, keepdims=True))
        o_ref[:, hs] = jnp.concatenate(rows, axis=0)


def _sample_out_kernel(x_ref, o_ref, g_ref, yb_ref, hgg_ref, wout_ref, g2_ref, wr_ref, br_ref,
                       x1_ref, xn2_ref, gates_ref, cnt_ref, *, n_groups, per_group):
    wa_w = o_ref.shape[1]
    ya = []
    for h in range(wa_w // HEAD_DIM):
        hs = slice(h * HEAD_DIM, (h + 1) * HEAD_DIM)
        oh = o_ref[:, hs]
        oh = oh * lax.rsqrt(jnp.mean(oh * oh, axis=-1, keepdims=True) + EPS) * hgg_ref[:, hs]
        ya.append(oh * _silu(g_ref[:, hs]))
    y = jnp.concatenate(ya + [yb_ref[...]], axis=-1)
    x1 = x_ref[...] + jnp.dot(y, wout_ref[...], precision=HIGHEST, preferred_element_type=F32)
    xn2 = _rms(x1, g2_ref[...])
    logits_t = lax.dot_general(wr_ref[...], xn2, NT_DIMS, precision=HIGHEST,
                               preferred_element_type=F32) + br_ref[...]
    gates_t, g_idx = _route_rows(logits_t, n_groups, per_group)
    n = x1.shape[0]
    pad = x1_ref.shape[0] - n
    x1_ref[0:n, :] = x1
    xn2_ref[0:n, :] = xn2.astype(BF16)
    gates_ref[0:n, :] = gates_t.T
    if pad:
        x1_ref[n:, :] = jnp.zeros((pad, x1.shape[1]), F32)
        xn2_ref[n:, :] = jnp.zeros((pad, x1.shape[1]), BF16)
        gates_ref[n:, :] = jnp.where(lax.broadcasted_iota(jnp.int32, (pad, ROUTER_LANES), 1) == 0, -1.0, 0.0)
    lane_c = lax.broadcasted_iota(jnp.int32, (1, ROUTER_LANES), 1)
    tok = lax.broadcasted_iota(jnp.int32, g_idx.shape, 1)
    n_tiles = cnt_ref.shape[0]
    for t in range(n_tiles):
        in_tile = (tok >= t * TOK_TILE) & (tok < (t + 1) * TOK_TILE)
        cnt = jnp.zeros((1, ROUTER_LANES), jnp.int32)
        for g in range(n_groups):
            here = jnp.sum(jnp.where(in_tile & (g_idx == g), 1, 0), axis=1, keepdims=True)
            cnt = cnt + jnp.where(lane_c == g, here, 0)
        cnt_ref[t:t + 1, :] = cnt


def _whole(kernel, out_shape, *args, name):
    return pl.pallas_call(
        kernel, out_shape=out_shape,
        compiler_params=pltpu.CompilerParams(vmem_limit_bytes=VMEM_LIMIT_BYTES), name=name)(*args)


def _mixer_sample(x, s0, h0, c0, p, tb):
    n, d = x.shape
    wa_w = p["hgg"].shape[1]
    wb_w = p["cb"].shape[1]
    n_heads = wa_w // HEAD_DIM
    sd = lambda w: jax.ShapeDtypeStruct((n, w), F32)
    key_major = jax.ShapeDtypeStruct((wa_w, n), F32)
    q, f, v, g, yb, h_new, xr = _whole(
        _sample_in_kernel, (key_major,) * 2 + (sd(wa_w),) * 2 + (sd(wb_w),) * 3,
        x, c0[:, 0, :], c0[:, 1, :], c0[:, 2, :], h0, p["lbw"], p["g1"], p["win"], p["cw"], p["cb"],
        p["wx"], p["bx"], p["wa"], p["ba"], p["lam"], name="sample_in")
    tok = lambda w: pl.BlockSpec((tb, w), lambda i: (i, 0))
    st = pl.BlockSpec((tb, n_heads, HEAD_DIM, HEAD_DIM), lambda i: (i, 0, 0, 0))
    s_new, o = pl.pallas_call(
        _sample_state_kernel,
        grid=(n // tb,),
        in_specs=[st, _const_spec((wa_w, n)), _const_spec((wa_w, n)), tok(wa_w)],
        out_specs=(st, tok(wa_w)),
        out_shape=(jax.ShapeDtypeStruct(s0.shape, F32), sd(wa_w)),
        compiler_params=pltpu.CompilerParams(dimension_semantics=("arbitrary",),
                                             vmem_limit_bytes=VMEM_LIMIT_BYTES),
        name="sample_state",
    )(s0, q, f, v)
    n_pad = -(-n // TOK_TILE) * TOK_TILE
    x1, xn2, gates, counts = _whole(
        functools.partial(_sample_out_kernel, n_groups=p["n_groups"], per_group=p["per_group"]),
        (jax.ShapeDtypeStruct((n_pad, d), F32), jax.ShapeDtypeStruct((n_pad, d), BF16),
         jax.ShapeDtypeStruct((n_pad, ROUTER_LANES), F32),
         jax.ShapeDtypeStruct((n_pad // TOK_TILE, ROUTER_LANES), jnp.int32)),
        x, o, g, yb, p["hgg"], p["wout"], p["g2"], p["wr_t"], p["br_t"], name="sample_out")
    c_new = jnp.stack([c0[:, 1, :], c0[:, 2, :], xr], axis=1)
    return x1, xn2, gates, counts, s_new, h_new, c_new


def _block_diag(w):
    n, c, _ = w.shape
    eye = jnp.eye(n, dtype=w.dtype)
    return (w[:, :, None, :] * eye[:, None, :, None]).reshape(n * c, n * c)


def _prepare(lower_bounds, ln1_g, w_in, hgrn_norm_g, conv_w, conv_b, lru_wx, lru_bx, lru_wa, lru_ba,
             lru_lambda, w_out, ln2_g, router_wg, router_bg, router_we, router_be, exp_w1, exp_w3,
             exp_w2, final_g):
    d = w_in.shape[1]
    n_groups = router_wg.shape[-1]
    per_group = router_we.shape[-1]
    row = lambda a: a.reshape(1, -1).astype(F32)
    we = jnp.transpose(router_we[0], (1, 0, 2)).reshape(d, n_groups * per_group)
    rows_t = -(-(8 + n_groups * per_group) // 16) * 16
    wr_t = jnp.concatenate([router_wg[0].T, jnp.zeros((8 - n_groups, d), F32), we.T,
                            jnp.zeros((rows_t - 8 - n_groups * per_group, d), F32)], axis=0)
    br_t = jnp.concatenate([router_bg[0], jnp.zeros((8 - n_groups,), F32), router_be[0].reshape(-1),
                            jnp.zeros((rows_t - 8 - n_groups * per_group,), F32)]).reshape(-1, 1)
    wx = _block_diag(lru_wx[0])
    wa = _block_diag(lru_wa[0])
    return dict(
        wr_t=wr_t, wr_t_bf=wr_t.astype(BF16), br_t=br_t,
        n_groups=n_groups, per_group=per_group,
        lbw=lower_bounds.astype(F32), g1=row(ln1_g[0]), win=w_in[0],
        hgg=row(hgrn_norm_g[0]), cw=conv_w[0], cb=row(conv_b[0]),
        wx=wx, wx_bf=wx.astype(BF16), bx=row(lru_bx[0]), wa=wa, wa_bf=wa.astype(BF16), ba=row(lru_ba[0]),
        lam=row(lru_lambda[0]), wout=w_out[0], g2=row(ln2_g[0]),
        w1=exp_w1[0], w3=exp_w3[0],
        w2=exp_w2[0], gf=row(final_g))


def kernel(x_prompt, x_sample, state_hgrn, state_rglru, state_conv, lower_bounds, ln1_g, w_in, hgrn_norm_g, conv_w, conv_b, lru_wx, lru_bx, lru_wa, lru_ba, lru_lambda, w_out, ln2_g, router_wg, router_bg, router_we, router_be, exp_w1, exp_w3, exp_w2, final_g):
    assert w_in.shape[0] == 1, "single-layer trunk"
    p = _prepare(lower_bounds, ln1_g, w_in, hgrn_norm_g, conv_w, conv_b, lru_wx, lru_bx, lru_wa, lru_ba,
                 lru_lambda, w_out, ln2_g, router_wg, router_bg, router_we, router_be, exp_w1, exp_w3,
                 exp_w2, final_g)
    bsz, seq, d = x_prompt.shape
    x1, xn2, gates, counts, s_p, h_p, c_p, w1_bf, w3_bf = _mixer_prompt(x_prompt, p, min(MIXER_BLOCK, seq))
    t = bsz * seq
    n = x_sample.shape[0]
    x1s, xn2s, gates_s, counts_s, s_s, h_s, c_s = _mixer_sample(
        x_sample[:, 0, :], state_hgrn[0], state_rglru[0], state_conv[0], p, SAMPLE_STEP_TOKENS)
    y_p, y_s = _moe_sorted((x1.reshape(t, d), xn2.reshape(t, d), gates.reshape(t, ROUTER_LANES), counts),
                           (x1s, xn2s, gates_s, counts_s), w1_bf, w3_bf, p)
    return (y_p.reshape(bsz, seq, d), y_s[:n].reshape(n, 1, d),
            s_p[None], h_p.reshape(1, bsz, -1), c_p[None],
            s_s[None], h_s[None], c_s[None])
```

```python
import functools

import jax
import jax.numpy as jnp
from jax import lax
from jax.experimental import pallas as pl
from jax.experimental.pallas import tpu as pltpu

F32 = jnp.float32
BF16 = jnp.bfloat16
HIGHEST = lax.Precision.HIGHEST

EPS = 1e-6
LRU_C = 8.0
LOG2E = 1.4426950408889634
HEAD_DIM = 128
CHUNK = 64
SUB = 16
UNROLL = 8
ROUTER_LANES = 128
PROMPT_EXPERT_LANE = 8
MIXER_BLOCK = 512
SAMPLE_STEP_TOKENS = 8
TOK_TILE = 512
SEG = 16
MAX_GROUPS = 8
SORT_ROWS = TOK_TILE + MAX_GROUPS * SEG
ROW_TILE = 512
VMEM_LIMIT_BYTES = 56 * 1024 * 1024
MIXER_VMEM_LIMIT_BYTES = 60 * 1024 * 1024

NT_DIMS = (((1,), (1,)), ((), ()))
TN_DIMS = (((0,), (0,)), ((), ()))


def _rms(x, g):
    return x * lax.rsqrt(jnp.mean(x * x, axis=-1, keepdims=True) + EPS) * g


def _sigmoid(x):
    return 1.0 / (1.0 + jnp.exp(-x))


def _silu(x):
    return x * _sigmoid(x)


def _gelu_tanh(x):
    c = 0.7978845608028654
    return x * (0.5 * (1.0 + jnp.tanh(c * (x + 0.044715 * (x * x * x)))))


def _softplus(z):
    return jnp.maximum(z, 0.0) + jnp.log1p(jnp.exp(-jnp.abs(z)))


def _expm1(x):
    u = jnp.exp(x)
    um1 = u - 1.0
    small = um1 * x / jnp.log(u)
    return jnp.where(um1 == 0.0, x, jnp.where(jnp.abs(x) < 0.5, small, um1))


def _forget_lower_bound(lbw):
    m = jnp.max(lbw, axis=0, keepdims=True)
    e = jnp.exp(lbw - m)
    return e[0:1, :] / jnp.sum(e, axis=0, keepdims=True)


def _route_rows(logits_t, n_groups, per_group):
    assert n_groups <= 8 and per_group == 8
    n_tok = logits_t.shape[1]
    row = lax.broadcasted_iota(jnp.int32, (8, n_tok), 0)
    neg = jnp.float32(-jnp.inf)
    big = jnp.int32(8)
    lg = jnp.where(row < n_groups, logits_t[0:8], neg)
    mg = jnp.max(lg, axis=0, keepdims=True)
    g_idx = jnp.min(jnp.where(lg == mg, row, big), axis=0, keepdims=True)
    p_top = 1.0 / jnp.sum(jnp.where(row < n_groups, jnp.exp(lg - mg), 0.0), axis=0, keepdims=True)
    sel = logits_t[8:16]
    for g in range(1, n_groups):
        sel = jnp.where(g_idx == g, logits_t[8 + 8 * g:16 + 8 * g], sel)
    m1 = jnp.max(sel, axis=0, keepdims=True)
    i1 = jnp.min(jnp.where(sel == m1, row, big), axis=0, keepdims=True)
    sel2 = jnp.where(row == i1, neg, sel)
    m2 = jnp.max(sel2, axis=0, keepdims=True)
    i2 = jnp.min(jnp.where(sel2 == m2, row, big), axis=0, keepdims=True)
    e2 = jnp.exp(m2 - m1)
    den = 1.0 + e2
    w1 = p_top / den
    w2 = p_top * (e2 / den)
    own = jnp.where(row == i1, w1, 0.0) + jnp.where(row == i2, w2, 0.0)
    blocks = [jnp.where(row == 0, g_idx.astype(F32), 0.0)]
    blocks += [jnp.where(g_idx == g, own, 0.0) for g in range(n_groups)]
    blocks += [jnp.zeros((8, n_tok), F32)] * (ROUTER_LANES // 8 - len(blocks))
    return jnp.concatenate(blocks, axis=0), g_idx


def _mixer_prompt_kernel(x_ref, lbw_ref, g1_ref, win_ref, hgg_ref, cw_ref, cb_ref, wx_ref, bx_ref,
                         wa_ref, ba_ref, lam_ref, wout_ref, g2_ref, wr_ref, br_ref, w1_ref, w3_ref,
                         x1_ref, xn2_ref, gates_ref, cnt_ref, sout_ref, hout_ref, cout_ref, w1o_ref, w3o_ref,
                         proj_s, k_s, b_s, o_s, st_s, xr_s, a_s, u_s, hcar_s, yb_out_s, win_s, wout_s,
                         *, n_groups, per_group):
    lb_t = x_ref.shape[1]
    wa_w = o_s.shape[1]
    wb_w = a_s.shape[1]
    n_heads = wa_w // HEAD_DIM
    j = pl.program_id(1)
    nj = pl.num_programs(1)

    @pl.when((pl.program_id(0) == 0) & (j == 0))
    def _():
        for c in range(0, win_ref.shape[1], wa_w):
            win_s[:, c:c + wa_w] = win_ref[:, c:c + wa_w].astype(BF16)
        wout_s[...] = wout_ref[...].astype(BF16)

    @pl.when(j == 0)
    def _():
        st_s[...] = jnp.zeros_like(st_s)
        hcar_s[...] = jnp.zeros_like(hcar_s)
        xr_s[0:8, :] = jnp.zeros((8, wb_w), F32)

    x = x_ref[0]
    xn = _rms(x, g1_ref[...]).astype(BF16)
    xb0 = 4 * wa_w

    def project(pieces):
        for c, w in pieces:
            proj_s[:, c:c + w] = jnp.dot(xn, win_s[:, c:c + w], preferred_element_type=F32)

    project(((wa_w, wa_w), (xb0, wb_w), (xb0 + wb_w, wb_w), (0, wa_w), (2 * wa_w, wa_w), (3 * wa_w, wa_w)))

    w1o_ref[...] = w1_ref[...].astype(BF16)
    w3o_ref[...] = w3_ref[...].astype(BF16)

    xr_s[pl.ds(8, lb_t), :] = proj_s[:, xb0:xb0 + wb_w]
    xc = (cb_ref[...] + cw_ref[3:4, :] * xr_s[pl.ds(8, lb_t), :] + cw_ref[2:3, :] * xr_s[pl.ds(7, lb_t), :]
          + cw_ref[1:2, :] * xr_s[pl.ds(6, lb_t), :] + cw_ref[0:1, :] * xr_s[pl.ds(5, lb_t), :])
    tail = xr_s[pl.ds(lb_t + 5, 3), :]
    xr_s[5:8, :] = tail
    cout_ref[0] = tail
    xcb = xc.astype(BF16)
    gate_x = _sigmoid(jnp.dot(xcb, wx_ref[...], preferred_element_type=F32) + bx_ref[...])
    gate_a = _sigmoid(jnp.dot(xcb, wa_ref[...], preferred_element_type=F32) + ba_ref[...])
    log_a =(-LRU_C) * gate_a * _softplus(-lam_ref[...])
    a = jnp.exp(log_a)
    mult = jnp.sqrt((1.0 - a) * (1.0 + a))
    first = (lax.broadcasted_iota(jnp.int32, (lb_t, 1), 0) == 0) & (j == 0)
    a = jnp.where(first, 0.0, a)
    mult = jnp.where(first, 1.0, mult)
    a_s[...] = a
    u_s[...] = gate_x * xc * mult

    lb = _forget_lower_bound(lbw_ref[...])
    f = lb + (1.0 - lb) * _sigmoid(proj_s[:, wa_w:2 * wa_w])
    for h in range(n_heads):
        k_s[h] = 1.0 - f[:, h * HEAD_DIM:(h + 1) * HEAD_DIM]
    logf = jnp.log(f)
    r_i = lax.broadcasted_iota(jnp.int32, (CHUNK, CHUNK), 0)
    c_i = lax.broadcasted_iota(jnp.int32, (CHUNK, CHUNK), 1)
    tri = jnp.where(r_i >= c_i, 1.0, 0.0).astype(BF16)
    lf_hi = logf.astype(BF16)
    rest = logf - lf_hi.astype(F32)
    lf_mid = rest.astype(BF16)
    lf_lo = (rest - lf_mid.astype(F32)).astype(BF16)
    for c in range(0, lb_t, CHUNK):
        cum = [jnp.dot(tri, part[c:c + CHUNK, :], preferred_element_type=F32) for part in (lf_lo, lf_mid, lf_hi)]
        b_all = LOG2E * ((cum[0] + cum[1]) + cum[2])
        for h in range(n_heads):
            b_s[h, c:c + CHUNK, :] = b_all[:, h * HEAD_DIM:(h + 1) * HEAD_DIM]

    row_sub = lax.broadcasted_iota(jnp.int32, (SUB, HEAD_DIM), 0)
    lane_sub = lax.broadcasted_iota(jnp.int32, (SUB, HEAD_DIM), 1)
    assert n_heads % 2 == 0
    r_kk = lax.broadcasted_iota(jnp.int32, (2 * HEAD_DIM, 2 * HEAD_DIM), 0)
    c_kk = lax.broadcasted_iota(jnp.int32, (2 * HEAD_DIM, 2 * HEAD_DIM), 1)
    ones_kk = jnp.where((r_kk < HEAD_DIM) == (c_kk < HEAD_DIM), 1.0, 0.0).astype(BF16)

    n_sub = CHUNK // SUB
    half = SUB // 2
    lower_left = (row_sub >= half) & (lane_sub < half)
    own_lane = jnp.where(row_sub >= half, half, 0)

    def chunk_start(r0):
        first = []
        for h in range(n_heads):
            hs = slice(h * HEAD_DIM, (h + 1) * HEAD_DIM)
            q = proj_s[pl.ds(r0, CHUNK), hs]
            b = b_s[h, pl.ds(r0, CHUNK), :]
            v = proj_s[pl.ds(r0, CHUNK), 2 * wa_w + h * HEAD_DIM:2 * wa_w + (h + 1) * HEAD_DIM]
            k = k_s[h, pl.ds(r0, CHUNK), :]

            def key_rows(ref, lo, j):
                return jnp.stack([jnp.broadcast_to(ref[h, pl.ds(r0 + lo + hf * half + j, 1), :], (half, HEAD_DIM))
                                  for hf in range(2)])
            vb = v.astype(BF16)
            st = st_s[h]
            b_last = b[CHUNK - 1:CHUNK, :]
            o = lax.dot_general((q * jnp.exp2(b)).astype(BF16), st.astype(BF16), NT_DIMS,
                                preferred_element_type=F32)
            k_end = k * jnp.exp2(b_last - b)
            st_s[h] = st * jnp.exp2(b_last) + lax.dot_general(vb, k_end.astype(BF16), TN_DIMS,
                                                               preferred_element_type=F32)
            terms, off, mid = [], [], []
            for i in range(n_sub):
                lo = i * SUB
                qi, bi, ki = q[lo:lo + SUB], b[lo:lo + SUB], k[lo:lo + SUB]
                q3, b3 = (a.reshape(2, half, HEAD_DIM) for a in (qi, bi))
                terms += [(q3 * (key_rows(k_s, lo, j) * jnp.exp2(b3 - key_rows(b_s, lo, j))))
                          .reshape(SUB, HEAD_DIM).astype(BF16) for j in range(half)]
                rm = bi[half - 1:half]
                mid.append(lax.dot_general((qi * jnp.exp2(bi - rm)).astype(BF16),
                                           (ki * jnp.exp2(rm - bi)).astype(BF16), NT_DIMS,
                                           preferred_element_type=F32))
                if i > 0:
                    r = b[lo - 1:lo]
                    qt = (qi * jnp.exp2(bi - r)).astype(BF16)
                    kt = (k[:lo] * jnp.exp2(r - b[:lo])).astype(BF16)
                    off.append(lax.dot_general(qt, kt, NT_DIMS, preferred_element_type=F32))
            first.append([o, vb, off, jnp.concatenate(terms, axis=0), mid])
        for h in range(0, n_heads, 2):
            both = jnp.concatenate([first[h][3], first[h + 1][3]], axis=1)
            sums = jnp.dot(both, ones_kk, preferred_element_type=F32)
            first[h][3] = sums[:, :HEAD_DIM]
            first[h + 1][3] = sums[:, HEAD_DIM:]
        return first

    def chunk_finish(r0, first):
        for h in range(n_heads):
            o, vb, off, sums, mid = first[h]
            outs = []
            for i in range(n_sub):
                lo = i * SUB
                sc = jnp.zeros((SUB, HEAD_DIM), F32)
                for j in range(half):
                    row0 = (i * half + j) * SUB
                    sc = jnp.where(lane_sub == own_lane + j, sums[row0:row0 + SUB], sc)
                sc = jnp.where(row_sub >= lane_sub, sc, 0.0)[:, :SUB]
                sc = jnp.where(lower_left[:, :SUB], mid[i], sc)
                od = o[lo:lo + SUB] + jnp.dot(sc.astype(BF16), vb[lo:lo + SUB], preferred_element_type=F32)
                if i > 0:
                    od = od + jnp.dot(off[i - 1].astype(BF16), vb[:lo], preferred_element_type=F32)
                outs.append(od)
            o_s[pl.ds(r0, CHUNK), h * HEAD_DIM:(h + 1) * HEAD_DIM] = jnp.concatenate(outs, axis=0)

    def chunks_body(ci, carry):
        rows = [pl.multiple_of((ci * UNROLL + u) * CHUNK, CHUNK) for u in range(UNROLL)]
        started = [chunk_start(r0) for r0 in rows]
        for r0, first in zip(rows, started):
            chunk_finish(r0, first)
        return carry

    row8 = lax.broadcasted_iota(jnp.int32, (8, 1), 0)

    def scan_body(gi, carry):
        r0 = pl.multiple_of(gi * 8, 8)
        aa = a_s[pl.ds(r0, 8), :]
        uu = u_s[pl.ds(r0, 8), :]
        for s in (1, 2, 4):
            m = row8 >= s
            uu = jnp.where(m, aa * pltpu.roll(uu, s, 0) + uu, uu)
            aa = jnp.where(m, aa * pltpu.roll(aa, s, 0), aa)
        hh = aa * carry + uu
        u_s[pl.ds(r0, 8), :] = hh
        return hh[7:8, :]

    h_last = lax.fori_loop(0, lb_t // 8, scan_body, hcar_s[...])
    hcar_s[...] = h_last
    hout_ref[0] = h_last
    yb = (u_s[...] * _gelu_tanh(proj_s[:, xb0 + wb_w:xb0 + 2 * wb_w])).astype(BF16)
    yb_out_s[...] = jnp.dot(yb, wout_s[wa_w:, :], preferred_element_type=F32)

    assert lb_t % (CHUNK * UNROLL) == 0
    lax.fori_loop(0, lb_t // (CHUNK * UNROLL), chunks_body, 0)

    ya = []
    for h in range(n_heads):
        hs = slice(h * HEAD_DIM, (h + 1) * HEAD_DIM)
        oh = o_s[:, hs]
        oh = oh * lax.rsqrt(jnp.mean(oh * oh, axis=-1, keepdims=True) + EPS) * hgg_ref[:, hs]
        ya.append(oh * _silu(proj_s[:, 3 * wa_w + h * HEAD_DIM:3 * wa_w + (h + 1) * HEAD_DIM]))

    ya = jnp.concatenate(ya, axis=-1).astype(BF16)
    x1 = x + (jnp.dot(ya, wout_s[:wa_w, :], preferred_element_type=F32) + yb_out_s[...])
    x1_ref[0] = x1
    xn2 = _rms(x1, g2_ref[...]).astype(BF16)
    xn2_ref[0] = xn2
    logits_t = lax.dot_general(wr_ref[...], xn2, NT_DIMS, preferred_element_type=F32) + br_ref[...]
    gates_t, g_idx = _route_rows(logits_t, n_groups, per_group)
    gates_ref[0] = gates_t.T
    lane_c = lax.broadcasted_iota(jnp.int32, (1, ROUTER_LANES), 1)
    for t in range(lb_t // TOK_TILE):
        gi = g_idx[:, t * TOK_TILE:(t + 1) * TOK_TILE]
        cnt = jnp.zeros((1, ROUTER_LANES), jnp.int32)
        for g in range(n_groups):
            cnt = cnt + jnp.where(lane_c == g, jnp.sum(jnp.where(gi == g, 1, 0), axis=1, keepdims=True), 0)
        cnt_ref[0, t:t + 1, :] = cnt

    @pl.when(j == nj - 1)
    def _():
        for h in range(n_heads):
            sout_ref[0, h] = st_s[h].T


def _const_spec(shape):
    nd = len(shape)
    return pl.BlockSpec(shape, lambda *_: (0,) * nd)


def _mixer_prompt(x, p, lb_t):
    bsz, seq, d = x.shape
    wa_w = p["hgg"].shape[1]
    wb_w = p["cb"].shape[1]
    n_heads = wa_w // HEAD_DIM
    n_cols = p["win"].shape[1]
    weights = [p["lbw"], p["g1"], p["win"], p["hgg"], p["cw"], p["cb"], p["wx_bf"], p["bx"],
               p["wa_bf"], p["ba"], p["lam"], p["wout"], p["g2"], p["wr_t_bf"], p["br_t"]]
    nj = seq // lb_t
    tile = lambda w: pl.BlockSpec((1, lb_t, w), lambda b, j: (b, j, 0))
    n_exp = p["w1"].shape[0]
    per_step = next(k for k in range(-(-n_exp // (bsz * nj)), n_exp + 1) if n_exp % k == 0)
    exp_spec = pl.BlockSpec((per_step,) + p["w1"].shape[1:],
                            lambda b, j: (jnp.minimum(b * nj + j, n_exp // per_step - 1), 0, 0))
    out_shape = (
        jax.ShapeDtypeStruct((bsz, seq, d), F32),
        jax.ShapeDtypeStruct((bsz, seq, d), BF16),
        jax.ShapeDtypeStruct((bsz, seq, ROUTER_LANES), F32),
        jax.ShapeDtypeStruct((bsz * (seq // lb_t), lb_t // TOK_TILE, ROUTER_LANES), jnp.int32),
        jax.ShapeDtypeStruct((bsz, n_heads, HEAD_DIM, HEAD_DIM), F32),
        jax.ShapeDtypeStruct((bsz, 1, wb_w), F32),
        jax.ShapeDtypeStruct((bsz, 3, wb_w), F32),
        jax.ShapeDtypeStruct(p["w1"].shape, BF16),
        jax.ShapeDtypeStruct(p["w3"].shape, BF16),
    )
    out_specs = (
        tile(d), tile(d), tile(ROUTER_LANES),
        pl.BlockSpec((1, lb_t // TOK_TILE, ROUTER_LANES), lambda b, j: (b * nj + j, 0, 0)),
        pl.BlockSpec((1, n_heads, HEAD_DIM, HEAD_DIM), lambda b, j: (b, 0, 0, 0)),
        pl.BlockSpec((1, 1, wb_w), lambda b, j: (b, 0, 0)),
        pl.BlockSpec((1, 3, wb_w), lambda b, j: (b, 0, 0)),
        exp_spec, exp_spec,
    )
    scratch = [
        pltpu.VMEM((lb_t, n_cols), F32),
        pltpu.VMEM((n_heads, lb_t, HEAD_DIM), F32),
        pltpu.VMEM((n_heads, lb_t, HEAD_DIM), F32),
        pltpu.VMEM((lb_t, wa_w), F32),
        pltpu.VMEM((n_heads, HEAD_DIM, HEAD_DIM), F32),
        pltpu.VMEM((lb_t + 8, wb_w), F32),
        pltpu.VMEM((lb_t, wb_w), F32),
        pltpu.VMEM((lb_t, wb_w), F32),
        pltpu.VMEM((1, wb_w), F32),
        pltpu.VMEM((lb_t, d), F32),
        pltpu.VMEM((d, n_cols), BF16),
        pltpu.VMEM(p["wout"].shape, BF16),
    ]
    weight_specs = [pl.BlockSpec(w.shape, lambda *_, nd=w.ndim: (0,) * nd, pipeline_mode=pl.Buffered(1))
                    for w in weights]
    kern = functools.partial(_mixer_prompt_kernel, n_groups=p["n_groups"], per_group=p["per_group"])
    return pl.pallas_call(
        kern,
        grid=(bsz, nj),
        in_specs=[tile(d)] + weight_specs + [exp_spec, exp_spec],
        out_specs=out_specs,
        out_shape=out_shape,
        scratch_shapes=scratch,
        compiler_params=pltpu.CompilerParams(dimension_semantics=("arbitrary", "arbitrary"),
                                             vmem_limit_bytes=MIXER_VMEM_LIMIT_BYTES),
        name="mixer_prompt",
    )(x, *weights, p["w1"], p["w3"])


def _seg_pad(n):
    return jnp.bitwise_and(n + (SEG - 1), -SEG)


def _row_tile_pad(n):
    return jnp.bitwise_and(n + (ROW_TILE - 1), -ROW_TILE)


def _tile_segments(cnt_ref, tile, n_groups):
    padded = [_seg_pad(cnt_ref[tile * n_groups + g]) for g in range(n_groups)]
    starts, acc = [], jnp.int32(0)
    for g in range(n_groups):
        starts.append(acc)
        acc = acc + padded[g]
    return padded, starts, acc


def _group_bases(cnt_ref, n_tiles, n_groups):
    def body(t, tot):
        return tuple(tot[g] + _seg_pad(cnt_ref[t * n_groups + g]) for g in range(n_groups))
    totals = lax.fori_loop(0, n_tiles, body, (jnp.int32(0),) * n_groups)
    bases, ends, acc = [], [], jnp.int32(0)
    for g in range(n_groups):
        bases.append(acc)
        acc = acc + _row_tile_pad(totals[g])
        ends.append(acc)
    return bases, ends, totals


def _chunk_rows(n_chunks, starts, gstart):
    rows, row = [], None
    for c in range(n_chunks):
        row = gstart[0] if c == 0 else row + SEG
        for g in range(1, len(starts)):
            row = jnp.where(starts[g] == c * SEG, gstart[g], row)
        rows.append(pl.multiple_of(row, SEG))
    return rows


def _sort_matrix_rows(gates, starts):
    n_groups = len(starts)
    n_rows = TOK_TILE + n_groups * SEG
    col = lax.broadcasted_iota(jnp.int32, (TOK_TILE, ROUTER_LANES), 1)
    onehot_t = jnp.where(col == gates[:, 0:1].astype(jnp.int32), 1.0, 0.0).T
    r_i = lax.broadcasted_iota(jnp.int32, (TOK_TILE, TOK_TILE), 0)
    c_i = lax.broadcasted_iota(jnp.int32, (TOK_TILE, TOK_TILE), 1)
    later = jnp.where(r_i < c_i, 1.0, 0.0).astype(BF16)
    before = jnp.dot(onehot_t.astype(BF16), later, preferred_element_type=F32)
    g_row = lax.broadcasted_iota(jnp.int32, (ROUTER_LANES, 1), 0)
    base = jnp.zeros((ROUTER_LANES, 1), jnp.int32)
    for g in range(n_groups):
        base = base + jnp.where(g_row == g, starts[g], 0)
    dest = jnp.sum(jnp.where(onehot_t > 0.0, before + base.astype(F32), 0.0), axis=0, keepdims=True)
    placed = jnp.sum(onehot_t, axis=0, keepdims=True) > 0.0
    row = lax.broadcasted_iota(jnp.int32, (n_rows, TOK_TILE), 0)
    dest = jnp.where(placed, dest, -1.0)
    return row == dest.astype(jnp.int32), dest


def _dispatch_copies(xbuf, gbuf, xs_hbm, gs_hbm, sem, slot, c, row):
    return (pltpu.make_async_copy(xbuf.at[slot, pl.ds(c * SEG, SEG), :], xs_hbm.at[pl.ds(row, SEG), :], sem.at[slot]),
            pltpu.make_async_copy(gbuf.at[slot, pl.ds(c * SEG, SEG), :], gs_hbm.at[pl.ds(row, SEG), :], sem.at[slot]))


def _moe_dispatch_kernel(cnt_ref, xn2_ref, gates_ref, xn2b_ref, gatesb_ref, xs_hbm, gs_hbm, tg_ref, dest_ref,
                         xbuf, gbuf, sem, gstart_s, zx, zg, zsem, *, n_groups, per_group, n_first):
    j = pl.program_id(0)
    n_tiles = pl.num_programs(0)
    n_chunks = (TOK_TILE + n_groups * SEG) // SEG
    slot = lax.rem(j, 2)

    @pl.when(j == 0)
    def _():
        bases, ends, totals = _group_bases(cnt_ref, n_tiles, n_groups)
        for g in range(n_groups):
            gstart_s[g] = bases[g]
        n_rt = tg_ref.shape[0] - 1
        for i in range(n_rt):
            tg = jnp.int32(0)
            for g in range(n_groups):
                tg = tg + jnp.where(ends[g] <= i * ROW_TILE, 1, 0)
            tg_ref[i] = tg
        n_used = ends[n_groups - 1] // ROW_TILE
        tg_ref[n_rt] = n_used

        zx[...] = jnp.zeros_like(zx)
        zg[...] = jnp.zeros_like(zg)

        def zero_copies(row, n):
            return (pltpu.make_async_copy(zx.at[pl.ds(0, n), :], xs_hbm.at[pl.ds(row, n), :], zsem.at[0]),
                    pltpu.make_async_copy(zg.at[pl.ds(0, n), :], gs_hbm.at[pl.ds(row, n), :], zsem.at[0]))

        def for_each_gap(act):
            for g in range(n_groups):
                tail = bases[g] + totals[g]

                def seg_body(k, c, tail=tail):
                    for cp in zero_copies(pl.multiple_of(tail + k * SEG, SEG), SEG):
                        act(cp)
                    return c
                lax.fori_loop(0, (ends[g] - tail) // SEG, seg_body, 0)

            def tile_body(k, c):
                for cp in zero_copies(pl.multiple_of((n_used + k) * ROW_TILE, ROW_TILE), ROW_TILE):
                    act(cp)
                return c
            lax.fori_loop(0, n_rt - n_used, tile_body, 0)

        for_each_gap(lambda cp: cp.start())
        for_each_gap(lambda cp: cp.wait())

    def wait_tile(tile, slot_):
        _, _, used = _tile_segments(cnt_ref, tile, n_groups)
        for c in range(n_chunks):
            @pl.when(c * SEG < used)
            def _():
                for cp in _dispatch_copies(xbuf, gbuf, xs_hbm, gs_hbm, sem, slot_, c, 0):
                    cp.wait()

    @pl.when(j >= 2)
    def _():
        wait_tile(j - 2, slot)

    padded, starts, used = _tile_segments(cnt_ref, j, n_groups)
    second = j >= n_first
    gates = jnp.where(second, gatesb_ref[...], gates_ref[...])
    xn2 = jnp.where(second, xn2b_ref[...], xn2_ref[...])
    d = xn2_ref.shape[1]
    g_hi = gates.astype(BF16)
    g_lo = (gates - g_hi.astype(F32)).astype(BF16)
    onehot, dest = _sort_matrix_rows(gates, starts)
    dest_ref[0] = jnp.broadcast_to(dest, dest_ref.shape[1:])
    sort_m = jnp.where(onehot, 1.0, 0.0).astype(BF16)
    moved = jnp.dot(sort_m, jnp.concatenate([xn2, g_hi, g_lo], axis=1), preferred_element_type=F32)
    xbuf[slot] = moved[:, :d].astype(BF16)
    gbuf[slot] = moved[:, d:d + ROUTER_LANES] + moved[:, d + ROUTER_LANES:]
    gstart = [gstart_s[g] for g in range(n_groups)]
    rows = _chunk_rows(n_chunks, starts, gstart)
    for c in range(n_chunks):
        @pl.when(c * SEG < used)
        def _():
            for cp in _dispatch_copies(xbuf, gbuf, xs_hbm, gs_hbm, sem, slot, c, rows[c]):
                cp.start()
    for g in range(n_groups):
        gstart_s[g] = gstart[g] + padded[g]

    @pl.when(j == n_tiles - 1)
    def _():
        @pl.when(j >= 1)
        def _():
            wait_tile(j - 1, 1 - slot)
        wait_tile(j, slot)


def _moe_group_kernel(tg_ref, xs_ref, gs_ref, w1_ref, w3_ref, w2_ref, ys_ref, w2_s, *, n_groups, per_group):
    i = pl.program_id(0)

    @pl.when((i == 0) | (tg_ref[i] != tg_ref[jnp.maximum(i - 1, 0)]))
    def _():
        w2_s[...] = w2_ref[...].astype(BF16)

    @pl.when(tg_ref[i] < n_groups)
    def _():
        xb = xs_ref[...]
        gates = pltpu.roll(gs_ref[...], ROUTER_LANES - PROMPT_EXPERT_LANE - per_group * tg_ref[i], 1)
        acc = jnp.zeros(ys_ref.shape, F32)
        for s in range(per_group):
            h = (_silu(jnp.dot(xb, w1_ref[s], preferred_element_type=F32))
                 * jnp.dot(xb, w3_ref[s], preferred_element_type=F32) * gates[:, s:s + 1])
            acc = acc + jnp.dot(h.astype(BF16), w2_s[s], preferred_element_type=F32)
        ys_ref[...] = acc.astype(BF16)


def _moe_combine_kernel(cnt_ref, dest_ref, x1_ref, x1b_ref, gf_ref, ys_hbm, y_ref, yb_ref,
                        ybuf, sem, gstart_s, *, n_groups, n_first):
    j = pl.program_id(0)
    n_tiles = pl.num_programs(0)
    n_chunks = SORT_ROWS // SEG
    slot = lax.rem(j, 2)

    def copies(slot_, c, row):
        return pltpu.make_async_copy(ys_hbm.at[pl.ds(row, SEG), :], ybuf.at[slot_, pl.ds(c * SEG, SEG), :],
                                     sem.at[slot_])

    def fetch_tile(tile, slot_):
        padded, starts, _ = _tile_segments(cnt_ref, tile, n_groups)
        gstart = [gstart_s[g] for g in range(n_groups)]
        for c, row in enumerate(_chunk_rows(n_chunks, starts, gstart)):
            copies(slot_, c, row).start()
        for g in range(n_groups):
            gstart_s[g] = gstart[g] + padded[g]

    @pl.when(j == 0)
    def _():
        bases, _, _ = _group_bases(cnt_ref, n_tiles, n_groups)
        for g in range(n_groups):
            gstart_s[g] = bases[g]
        fetch_tile(0, 0)

    @pl.when(j + 1 < n_tiles)
    def _():
        fetch_tile(j + 1, 1 - slot)

    for c in range(n_chunks):
        copies(slot, c, 0).wait()
    second = j >= n_first
    target = dest_ref[0].T[:, 0:1].astype(jnp.int32)
    sort_t = lax.broadcasted_iota(jnp.int32, (TOK_TILE, SORT_ROWS), 1) == target
    moe = jnp.dot(jnp.where(sort_t, 1.0, 0.0).astype(BF16), ybuf[slot], preferred_element_type=F32)

    @pl.when(jnp.logical_not(second))
    def _():
        y_ref[...] = _rms(x1_ref[...] + moe, gf_ref[...])

    @pl.when(second)
    def _():
        yb_ref[...] = _rms(x1b_ref[...] + moe, gf_ref[...])


def _moe_sorted(first, second, w1_bf, w3_bf, p):
    x1, xn2, gates, counts = first
    x1b, xn2b, gatesb, countsb = second
    t, d = x1.shape
    tb = x1b.shape[0]
    n_groups, per_group = p["n_groups"], p["per_group"]
    assert n_groups <= MAX_GROUPS and t % TOK_TILE == 0 and tb % TOK_TILE == 0
    n_first = t // TOK_TILE
    n_tiles = n_first + tb // TOK_TILE
    d_exp = w1_bf.shape[2]
    cap = t + tb + n_tiles * n_groups * SEG + n_groups * ROW_TILE + SORT_ROWS
    n_rt = -(-cap // ROW_TILE)
    rows = n_rt * ROW_TILE
    cnt = jnp.concatenate([counts[..., :n_groups].reshape(-1), countsb[..., :n_groups].reshape(-1)])
    params = pltpu.CompilerParams(dimension_semantics=("arbitrary",), vmem_limit_bytes=VMEM_LIMIT_BYTES)
    any_spec = pl.BlockSpec(memory_space=pl.ANY)
    tok = lambda w: pl.BlockSpec((TOK_TILE, w), lambda j, c: (jnp.minimum(j, n_first - 1), 0))
    tokb = lambda w: pl.BlockSpec((TOK_TILE, w), lambda j, c: (jnp.maximum(j - n_first, 0), 0))

    dest_spec = pl.BlockSpec((1, 8, TOK_TILE), lambda j, c: (j, 0, 0))

    xs, gs, tile_group, dest = pl.pallas_call(
        functools.partial(_moe_dispatch_kernel, n_groups=n_groups, per_group=per_group, n_first=n_first),
        grid_spec=pltpu.PrefetchScalarGridSpec(
            num_scalar_prefetch=1, grid=(n_tiles,),
            in_specs=[tok(d), tok(ROUTER_LANES), tokb(d), tokb(ROUTER_LANES)],
            out_specs=(any_spec, any_spec, pl.BlockSpec(memory_space=pltpu.SMEM), dest_spec),
            scratch_shapes=[pltpu.VMEM((2, TOK_TILE + n_groups * SEG, d), BF16),
                            pltpu.VMEM((2, TOK_TILE + n_groups * SEG, ROUTER_LANES), F32),
                            pltpu.SemaphoreType.DMA((2,)), pltpu.SMEM((n_groups,), jnp.int32),
                            pltpu.VMEM((ROW_TILE, d), BF16), pltpu.VMEM((ROW_TILE, ROUTER_LANES), F32),
                            pltpu.SemaphoreType.DMA((1,))]),
        out_shape=(jax.ShapeDtypeStruct((rows, d), BF16), jax.ShapeDtypeStruct((rows, ROUTER_LANES), F32),
                   jax.ShapeDtypeStruct((n_rt + 1,), jnp.int32),
                   jax.ShapeDtypeStruct((n_tiles, 8, TOK_TILE), F32)),
        compiler_params=params, name="moe_dispatch",
    )(cnt, xn2, gates, xn2b, gatesb)

    used_tile = lambda i, tg: (jnp.minimum(i, tg[n_rt] - 1), 0)
    group_w = lambda i, tg: (jnp.minimum(tg[i], n_groups - 1), 0, 0)
    ys = pl.pallas_call(
        functools.partial(_moe_group_kernel, n_groups=n_groups, per_group=per_group),
        grid_spec=pltpu.PrefetchScalarGridSpec(
            num_scalar_prefetch=1, grid=(n_rt,),
            in_specs=[pl.BlockSpec((ROW_TILE, d), used_tile), pl.BlockSpec((ROW_TILE, ROUTER_LANES), used_tile),
                      pl.BlockSpec((per_group, d, d_exp), group_w), pl.BlockSpec((per_group, d, d_exp), group_w),
                      pl.BlockSpec((per_group, d_exp, d), group_w)],
            out_specs=pl.BlockSpec((ROW_TILE, d), used_tile),
            scratch_shapes=[pltpu.VMEM((per_group, d_exp, d), BF16)]),
        out_shape=jax.ShapeDtypeStruct((rows, d), BF16),
        input_output_aliases={1: 0},
        compiler_params=params, name="moe_experts",
    )(tile_group, xs, gs, w1_bf, w3_bf, p["w2"])

    return pl.pallas_call(
        functools.partial(_moe_combine_kernel, n_groups=n_groups, n_first=n_first),
        grid_spec=pltpu.PrefetchScalarGridSpec(
            num_scalar_prefetch=1, grid=(n_tiles,),
            in_specs=[dest_spec, tok(d), tokb(d), pl.BlockSpec(p["gf"].shape, lambda j, c: (0, 0)), any_spec],
            out_specs=(tok(d), tokb(d)),
            scratch_shapes=[pltpu.VMEM((2, SORT_ROWS, d), BF16), pltpu.SemaphoreType.DMA((2,)),
                            pltpu.SMEM((n_groups,), jnp.int32)]),
        out_shape=(jax.ShapeDtypeStruct((t, d), F32), jax.ShapeDtypeStruct((tb, d), F32)),
        compiler_params=params, name="moe_combine",
    )(cnt, dest, x1, x1b, p["gf"], ys)


def _sample_in_kernel(x_ref, c0_ref, c1_ref, c2_ref, h0_ref, lbw_ref, g1_ref, win_ref, cw_ref, cb_ref,
                      wx_ref, bx_ref, wa_ref, ba_ref, lam_ref,
                      q_ref, f_ref, v_ref, g_ref, yb_ref, hnew_ref, xr_ref):
    wa_w = v_ref.shape[1]
    wb_w = yb_ref.shape[1]
    xn = _rms(x_ref[...], g1_ref[...])
    proj = jnp.dot(xn, win_ref[...], precision=HIGHEST, preferred_element_type=F32)
    lb = _forget_lower_bound(lbw_ref[...])
    f = lb + (1.0 - lb) * _sigmoid(proj[:, wa_w:2 * wa_w])
    q_ref[...] = proj[:, 0:wa_w].T
    f_ref[...] = f.T
    v_ref[...] = proj[:, 2 * wa_w:3 * wa_w]
    g_ref[...] = proj[:, 3 * wa_w:4 * wa_w]
    xr = proj[:, 4 * wa_w:4 * wa_w + wb_w]
    xr_ref[...] = xr
    xc = (cb_ref[...] + cw_ref[0:1, :] * c0_ref[...] + cw_ref[1:2, :] * c1_ref[...]
          + cw_ref[2:3, :] * c2_ref[...] + cw_ref[3:4, :] * xr)
    gate_x = _sigmoid(jnp.dot(xc, wx_ref[...], precision=HIGHEST, preferred_element_type=F32) + bx_ref[...])
    gate_a = _sigmoid(jnp.dot(xc, wa_ref[...], precision=HIGHEST, preferred_element_type=F32) + ba_ref[...])
    log_a = (-LRU_C) * gate_a * _softplus(-lam_ref[...])
    a = jnp.exp(log_a)
    mult = jnp.sqrt(-_expm1(2.0 * log_a))
    h = a * h0_ref[...] + gate_x * xc * mult
    hnew_ref[...] = h
    yb_ref[...] = h * _gelu_tanh(proj[:, 4 * wa_w + wb_w:4 * wa_w + 2 * wb_w])


def _sample_state_kernel(s_ref, qt_ref, ft_ref, v_ref, snew_ref, o_ref):
    tb, n_heads = s_ref.shape[0], s_ref.shape[1]
    n_tok = qt_ref.shape[1]
    shift = lax.rem(n_tok - pl.program_id(0) * tb, n_tok)
    assert 3 * tb <= HEAD_DIM
    lane = lax.broadcasted_iota(jnp.int32, (HEAD_DIM, n_tok), 1)
    r_e = lax.broadcasted_iota(jnp.int32, (n_tok, tb * HEAD_DIM), 0)
    c_e = lax.broadcasted_iota(jnp.int32, (n_tok, tb * HEAD_DIM), 1)
    spread = jnp.where((r_e < 3 * tb) & (lax.rem(r_e, tb) == c_e // HEAD_DIM), 1.0, 0.0).astype(BF16)

    def pieces_of(cols):
        hi = cols.astype(BF16).astype(F32)
        rest = cols - hi
        mid = rest.astype(BF16).astype(F32)
        low = rest - mid
        pieces = jnp.where(lane < tb, hi, jnp.where(lane < 2 * tb, pltpu.roll(mid, tb, 1),
                                                    jnp.where(lane < 3 * tb, pltpu.roll(low, 2 * tb, 1), 0.0)))
        return pieces.astype(BF16)

    for h in range(n_heads):
        hs = slice(h * HEAD_DIM, (h + 1) * HEAD_DIM)
        q_p = pieces_of(pltpu.roll(qt_ref[hs, :], shift, 1))
        f_p = pieces_of(pltpu.roll(ft_ref[hs, :], shift, 1))
        rows = []
        for t in range(tb):
            one = spread[:, t * HEAD_DIM:(t + 1) * HEAD_DIM]
            f_t = jnp.dot(f_p, one, preferred_element_type=F32)
            q_t = jnp.dot(q_p, one, preferred_element_type=F32)
            s_new = f_t * s_ref[t, h] + (1.0 - f_t) * v_ref[t:t + 1, hs]
            snew_ref[t, h] = s_new
            rows.append(jnp.sum(q_t * s_new, axis=0, keepdims=True))
        o_ref[:, hs] = jnp.concatenate(rows, axis=0)


def _sample_out_kernel(x_ref, o_ref, g_ref, yb_ref, hgg_ref, wout_ref, g2_ref, wr_ref, br_ref,
                       x1_ref, xn2_ref, gates_ref, cnt_ref, *, n_groups, per_group):
    wa_w = o_ref.shape[1]
    ya = []
    for h in range(wa_w // HEAD_DIM):
        hs = slice(h * HEAD_DIM, (h + 1) * HEAD_DIM)
        oh = o_ref[:, hs]
        oh = oh * lax.rsqrt(jnp.mean(oh * oh, axis=-1, keepdims=True) + EPS) * hgg_ref[:, hs]
        ya.append(oh * _silu(g_ref[:, hs]))
    y = jnp.concatenate(ya + [yb_ref[...]], axis=-1)
    x1 = x_ref[...] + jnp.dot(y, wout_ref[...], precision=HIGHEST, preferred_element_type=F32)
    xn2 = _rms(x1, g2_ref[...])
    logits_t = lax.dot_general(wr_ref[...], xn2, NT_DIMS, precision=HIGHEST,
                               preferred_element_type=F32) + br_ref[...]
    gates_t, g_idx = _route_rows(logits_t, n_groups, per_group)
    n = x1.shape[0]
    pad = x1_ref.shape[0] - n
    x1_ref[0:n, :] = x1
    xn2_ref[0:n, :] = xn2.astype(BF16)
    gates_ref[0:n, :] = gates_t.T
    if pad:
        x1_ref[n:, :] = jnp.zeros((pad, x1.shape[1]), F32)
        xn2_ref[n:, :] = jnp.zeros((pad, x1.shape[1]), BF16)
        gates_ref[n:, :] = jnp.where(lax.broadcasted_iota(jnp.int32, (pad, ROUTER_LANES), 1) == 0, -1.0, 0.0)
    lane_c = lax.broadcasted_iota(jnp.int32, (1, ROUTER_LANES), 1)
    tok = lax.broadcasted_iota(jnp.int32, g_idx.shape, 1)
    n_tiles = cnt_ref.shape[0]
    for t in range(n_tiles):
        in_tile = (tok >= t * TOK_TILE) & (tok < (t + 1) * TOK_TILE)
        cnt = jnp.zeros((1, ROUTER_LANES), jnp.int32)
        for g in range(n_groups):
            here = jnp.sum(jnp.where(in_tile & (g_idx == g), 1, 0), axis=1, keepdims=True)
            cnt = cnt + jnp.where(lane_c == g, here, 0)
        cnt_ref[t:t + 1, :] = cnt


def _whole(kernel, out_shape, *args, name):
    return pl.pallas_call(
        kernel, out_shape=out_shape,
        compiler_params=pltpu.CompilerParams(vmem_limit_bytes=VMEM_LIMIT_BYTES), name=name)(*args)


def _mixer_sample(x, s0, h0, c0, p, tb):
    n, d = x.shape
    wa_w = p["hgg"].shape[1]
    wb_w = p["cb"].shape[1]
    n_heads = wa_w // HEAD_DIM
    sd = lambda w: jax.ShapeDtypeStruct((n, w), F32)
    key_major = jax.ShapeDtypeStruct((wa_w, n), F32)
    q, f, v, g, yb, h_new, xr = _whole(
        _sample_in_kernel, (key_major,) * 2 + (sd(wa_w),) * 2 + (sd(wb_w),) * 3,
        x, c0[:, 0, :], c0[:, 1, :], c0[:, 2, :], h0, p["lbw"], p["g1"], p["win"], p["cw"], p["cb"],
        p["wx"], p["bx"], p["wa"], p["ba"], p["lam"], name="sample_in")
    tok = lambda w: pl.BlockSpec((tb, w), lambda i: (i, 0))
    st = pl.BlockSpec((tb, n_heads, HEAD_DIM, HEAD_DIM), lambda i: (i, 0, 0, 0))
    s_new, o = pl.pallas_call(
        _sample_state_kernel,
        grid=(n // tb,),
        in_specs=[st, _const_spec((wa_w, n)), _const_spec((wa_w, n)), tok(wa_w)],
        out_specs=(st, tok(wa_w)),
        out_shape=(jax.ShapeDtypeStruct(s0.shape, F32), sd(wa_w)),
        compiler_params=pltpu.CompilerParams(dimension_semantics=("arbitrary",),
                                             vmem_limit_bytes=VMEM_LIMIT_BYTES),
        name="sample_state",
    )(s0, q, f, v)
    n_pad = -(-n // TOK_TILE) * TOK_TILE
    x1, xn2, gates, counts = _whole(
        functools.partial(_sample_out_kernel, n_groups=p["n_groups"], per_group=p["per_group"]),
        (jax.ShapeDtypeStruct((n_pad, d), F32), jax.ShapeDtypeStruct((n_pad, d), BF16),
         jax.ShapeDtypeStruct((n_pad, ROUTER_LANES), F32),
         jax.ShapeDtypeStruct((n_pad // TOK_TILE, ROUTER_LANES), jnp.int32)),
        x, o, g, yb, p["hgg"], p["wout"], p["g2"], p["wr_t"], p["br_t"], name="sample_out")
    c_new = jnp.stack([c0[:, 1, :], c0[:, 2, :], xr], axis=1)
    return x1, xn2, gates, counts, s_new, h_new, c_new


def _block_diag(w):
    n, c, _ = w.shape
    eye = jnp.eye(n, dtype=w.dtype)
    return (w[:, :, None, :] * eye[:, None, :, None]).reshape(n * c, n * c)


def _prepare(lower_bounds, ln1_g, w_in, hgrn_norm_g, conv_w, conv_b, lru_wx, lru_bx, lru_wa, lru_ba,
             lru_lambda, w_out, ln2_g, router_wg, router_bg, router_we, router_be, exp_w1, exp_w3,
             exp_w2, final_g):
    d = w_in.shape[1]
    n_groups = router_wg.shape[-1]
    per_group = router_we.shape[-1]
    row = lambda a: a.reshape(1, -1).astype(F32)
    we = jnp.transpose(router_we[0], (1, 0, 2)).reshape(d, n_groups * per_group)
    rows_t = -(-(8 + n_groups * per_group) // 16) * 16
    wr_t = jnp.concatenate([router_wg[0].T, jnp.zeros((8 - n_groups, d), F32), we.T,
                            jnp.zeros((rows_t - 8 - n_groups * per_group, d), F32)], axis=0)
    br_t = jnp.concatenate([router_bg[0], jnp.zeros((8 - n_groups,), F32), router_be[0].reshape(-1),
                            jnp.zeros((rows_t - 8 - n_groups * per_group,), F32)]).reshape(-1, 1)
    wx = _block_diag(lru_wx[0])
    wa = _block_diag(lru_wa[0])
    return dict(
        wr_t=wr_t, wr_t_bf=wr_t.astype(BF16), br_t=br_t,
        n_groups=n_groups, per_group=per_group,
        lbw=lower_bounds.astype(F32), g1=row(ln1_g[0]), win=w_in[0],
        hgg=row(hgrn_norm_g[0]), cw=conv_w[0], cb=row(conv_b[0]),
        wx=wx, wx_bf=wx.astype(BF16), bx=row(lru_bx[0]), wa=wa, wa_bf=wa.astype(BF16), ba=row(lru_ba[0]),
        lam=row(lru_lambda[0]), wout=w_out[0], g2=row(ln2_g[0]),
        w1=exp_w1[0], w3=exp_w3[0],
        w2=exp_w2[0], gf=row(final_g))


def kernel(x_prompt, x_sample, state_hgrn, state_rglru, state_conv, lower_bounds, ln1_g, w_in, hgrn_norm_g, conv_w, conv_b, lru_wx, lru_bx, lru_wa, lru_ba, lru_lambda, w_out, ln2_g, router_wg, router_bg, router_we, router_be, exp_w1, exp_w3, exp_w2, final_g):
    assert w_in.shape[0] == 1, "single-layer trunk"
    p = _prepare(lower_bounds, ln1_g, w_in, hgrn_norm_g, conv_w, conv_b, lru_wx, lru_bx, lru_wa, lru_ba,
                 lru_lambda, w_out, ln2_g, router_wg, router_bg, router_we, router_be, exp_w1, exp_w3,
                 exp_w2, final_g)
    bsz, seq, d = x_prompt.shape
    x1, xn2, gates, counts, s_p, h_p, c_p, w1_bf, w3_bf = _mixer_prompt(x_prompt, p, min(MIXER_BLOCK, seq))
    t = bsz * seq
    n = x_sample.shape[0]
    x1s, xn2s, gates_s, counts_s, s_s, h_s, c_s = _mixer_sample(
        x_sample[:, 0, :], state_hgrn[0], state_rglru[0], state_conv[0], p, SAMPLE_STEP_TOKENS)
    y_p, y_s = _moe_sorted((x1.reshape(t, d), xn2.reshape(t, d), gates.reshape(t, ROUTER_LANES), counts),
                           (x1s, xn2s, gates_s, counts_s), w1_bf, w3_bf, p)
    return (y_p.reshape(bsz, seq, d), y_s[:n].reshape(n, 1, d),
            s_p[None], h_p.reshape(1, bsz, -1), c_p[None],
            s_s[None], h_s[None], c_s[None])
```

```python
import functools

import jax
import jax.numpy as jnp
from jax import lax
from jax.experimental import pallas as pl
from jax.experimental.pallas import tpu as pltpu

F32 = jnp.float32
BF16 = jnp.bfloat16
HIGHEST = lax.Precision.HIGHEST

EPS = 1e-6
LRU_C = 8.0
LOG2E = 1.4426950408889634
HEAD_DIM = 128
CHUNK = 64
SUB = 16
UNROLL = 8
ROUTER_LANES = 128
PROMPT_EXPERT_LANE = 8
MIXER_BLOCK = 512
SAMPLE_STEP_TOKENS = 8
TOK_TILE = 512
SEG = 16
MAX_GROUPS = 8
SORT_ROWS = TOK_TILE + MAX_GROUPS * SEG
ROW_TILE = 512
VMEM_LIMIT_BYTES = 56 * 1024 * 1024
MIXER_VMEM_LIMIT_BYTES = 60 * 1024 * 1024

NT_DIMS = (((1,), (1,)), ((), ()))
TN_DIMS = (((0,), (0,)), ((), ()))


def _rms(x, g):
    return x * lax.rsqrt(jnp.mean(x * x, axis=-1, keepdims=True) + EPS) * g


def _sigmoid(x):
    return 1.0 / (1.0 + jnp.exp(-x))


def _silu(x):
    return x * _sigmoid(x)


def _gelu_tanh(x):
    c = 0.7978845608028654
    return x * (0.5 * (1.0 + jnp.tanh(c * (x + 0.044715 * (x * x * x)))))


def _softplus(z):
    return jnp.maximum(z, 0.0) + jnp.log1p(jnp.exp(-jnp.abs(z)))


def _expm1(x):
    u = jnp.exp(x)
    um1 = u - 1.0
    small = um1 * x / jnp.log(u)
    return jnp.where(um1 == 0.0, x, jnp.where(jnp.abs(x) < 0.5, small, um1))


def _forget_lower_bound(lbw):
    m = jnp.max(lbw, axis=0, keepdims=True)
    e = jnp.exp(lbw - m)
    return e[0:1, :] / jnp.sum(e, axis=0, keepdims=True)


def _route_rows(logits_t, n_groups, per_group):
    assert n_groups <= 8 and per_group == 8
    n_tok = logits_t.shape[1]
    row = lax.broadcasted_iota(jnp.int32, (8, n_tok), 0)
    neg = jnp.float32(-jnp.inf)
    big = jnp.int32(8)
    lg = jnp.where(row < n_groups, logits_t[0:8], neg)
    mg = jnp.max(lg, axis=0, keepdims=True)
    g_idx = jnp.min(jnp.where(lg == mg, row, big), axis=0, keepdims=True)
    p_top = 1.0 / jnp.sum(jnp.where(row < n_groups, jnp.exp(lg - mg), 0.0), axis=0, keepdims=True)
    sel = logits_t[8:16]
    for g in range(1, n_groups):
        sel = jnp.where(g_idx == g, logits_t[8 + 8 * g:16 + 8 * g], sel)
    m1 = jnp.max(sel, axis=0, keepdims=True)
    i1 = jnp.min(jnp.where(sel == m1, row, big), axis=0, keepdims=True)
    sel2 = jnp.where(row == i1, neg, sel)
    m2 = jnp.max(sel2, axis=0, keepdims=True)
    i2 = jnp.min(jnp.where(sel2 == m2, row, big), axis=0, keepdims=True)
    e2 = jnp.exp(m2 - m1)
    den = 1.0 + e2
    w1 = p_top / den
    w2 = p_top * (e2 / den)
    own = jnp.where(row == i1, w1, 0.0) + jnp.where(row == i2, w2, 0.0)
    blocks = [jnp.where(row == 0, g_idx.astype(F32), 0.0)]
    blocks += [jnp.where(g_idx == g, own, 0.0) for g in range(n_groups)]
    blocks += [jnp.zeros((8, n_tok), F32)] * (ROUTER_LANES // 8 - len(blocks))
    return jnp.concatenate(blocks, axis=0), g_idx


def _mixer_prompt_kernel(x_ref, lbw_ref, g1_ref, win_ref, hgg_ref, cw_ref, cb_ref, wx_ref, bx_ref,
                         wa_ref, ba_ref, lam_ref, wout_ref, g2_ref, wr_ref, br_ref, w1_ref, w3_ref,
                         x1_ref, xn2_ref, gates_ref, cnt_ref, sout_ref, hout_ref, cout_ref, w1o_ref, w3o_ref,
                         proj_s, k_s, b_s, o_s, st_s, xr_s, a_s, u_s, hcar_s, yb_out_s, win_s, wout_s,
                         *, n_groups, per_group):
    lb_t = x_ref.shape[1]
    wa_w = o_s.shape[1]
    wb_w = a_s.shape[1]
    n_heads = wa_w // HEAD_DIM
    j = pl.program_id(1)
    nj = pl.num_programs(1)

    @pl.when((pl.program_id(0) == 0) & (j == 0))
    def _():
        for c in range(0, win_ref.shape[1], wa_w):
            win_s[:, c:c + wa_w] = win_ref[:, c:c + wa_w].astype(BF16)
        wout_s[...] = wout_ref[...].astype(BF16)

    @pl.when(j == 0)
    def _():
        st_s[...] = jnp.zeros_like(st_s)
        hcar_s[...] = jnp.zeros_like(hcar_s)
        xr_s[0:8, :] = jnp.zeros((8, wb_w), F32)

    x = x_ref[0]
    xn = _rms(x, g1_ref[...]).astype(BF16)
    xb0 = 4 * wa_w

    def project(pieces):
        for c, w in pieces:
            proj_s[:, c:c + w] = jnp.dot(xn, win_s[:, c:c + w], preferred_element_type=F32)

    project(((wa_w, wa_w), (xb0, wb_w), (xb0 + wb_w, wb_w), (0, wa_w), (2 * wa_w, wa_w), (3 * wa_w, wa_w)))

    w1o_ref[...] = w1_ref[...].astype(BF16)
    w3o_ref[...] = w3_ref[...].astype(BF16)

    xr_s[pl.ds(8, lb_t), :] = proj_s[:, xb0:xb0 + wb_w]
    xc = (cb_ref[...] + cw_ref[3:4, :] * xr_s[pl.ds(8, lb_t), :] + cw_ref[2:3, :] * xr_s[pl.ds(7, lb_t), :]
          + cw_ref[1:2, :] * xr_s[pl.ds(6, lb_t), :] + cw_ref[0:1, :] * xr_s[pl.ds(5, lb_t), :])
    tail = xr_s[pl.ds(lb_t + 5, 3), :]
    xr_s[5:8, :] = tail
    cout_ref[0] = tail
    xcb = xc.astype(BF16)
    gate_x = _sigmoid(jnp.dot(xcb, wx_ref[...], preferred_element_type=F32) + bx_ref[...])
    gate_a = _sigmoid(jnp.dot(xcb, wa_ref[...], preferred_element_type=F32) + ba_ref[...])
    log_a =(-LRU_C) * gate_a * _softplus(-lam_ref[...])
    a = jnp.exp(log_a)
    mult = jnp.sqrt((1.0 - a) * (1.0 + a))
    first = (lax.broadcasted_iota(jnp.int32, (lb_t, 1), 0) == 0) & (j == 0)
    a = jnp.where(first, 0.0, a)
    mult = jnp.where(first, 1.0, mult)
    a_s[...] = a
    u_s[...] = gate_x * xc * mult

    lb = _forget_lower_bound(lbw_ref[...])
    f = lb + (1.0 - lb) * _sigmoid(proj_s[:, wa_w:2 * wa_w])
    for h in range(n_heads):
        k_s[h] = 1.0 - f[:, h * HEAD_DIM:(h + 1) * HEAD_DIM]
    logf = jnp.log(f)
    r_i = lax.broadcasted_iota(jnp.int32, (CHUNK, CHUNK), 0)
    c_i = lax.broadcasted_iota(jnp.int32, (CHUNK, CHUNK), 1)
    tri = jnp.where(r_i >= c_i, 1.0, 0.0).astype(BF16)
    lf_hi = logf.astype(BF16)
    rest = logf - lf_hi.astype(F32)
    lf_mid = rest.astype(BF16)
    lf_lo = (rest - lf_mid.astype(F32)).astype(BF16)
    for c in range(0, lb_t, CHUNK):
        cum = [jnp.dot(tri, part[c:c + CHUNK, :], preferred_element_type=F32) for part in (lf_lo, lf_mid, lf_hi)]
        b_all = LOG2E * ((cum[0] + cum[1]) + cum[2])
        for h in range(n_heads):
            b_s[h, c:c + CHUNK, :] = b_all[:, h * HEAD_DIM:(h + 1) * HEAD_DIM]

    row_sub = lax.broadcasted_iota(jnp.int32, (SUB, HEAD_DIM), 0)
    lane_sub = lax.broadcasted_iota(jnp.int32, (SUB, HEAD_DIM), 1)
    assert n_heads % 2 == 0
    r_kk = lax.broadcasted_iota(jnp.int32, (2 * HEAD_DIM, 2 * HEAD_DIM), 0)
    c_kk = lax.broadcasted_iota(jnp.int32, (2 * HEAD_DIM, 2 * HEAD_DIM), 1)
    ones_kk = jnp.where((r_kk < HEAD_DIM) == (c_kk < HEAD_DIM), 1.0, 0.0).astype(BF16)

    n_sub = CHUNK // SUB
    half = SUB // 2
    lower_left = (row_sub >= half) & (lane_sub < half)
    own_lane = jnp.where(row_sub >= half, half, 0)

    def chunk_start(r0):
        first = []
        for h in range(n_heads):
            hs = slice(h * HEAD_DIM, (h + 1) * HEAD_DIM)
            q = proj_s[pl.ds(r0, CHUNK), hs]
            b = b_s[h, pl.ds(r0, CHUNK), :]
            v = proj_s[pl.ds(r0, CHUNK), 2 * wa_w + h * HEAD_DIM:2 * wa_w + (h + 1) * HEAD_DIM]
            k = k_s[h, pl.ds(r0, CHUNK), :]

            def key_rows(ref, lo, j):
                return jnp.stack([jnp.broadcast_to(ref[h, pl.ds(r0 + lo + hf * half + j, 1), :], (half, HEAD_DIM))
                                  for hf in range(2)])
            vb = v.astype(BF16)
            st = st_s[h]
            b_last = b[CHUNK - 1:CHUNK, :]
            o = lax.dot_general((q * jnp.exp2(b)).astype(BF16), st.astype(BF16), NT_DIMS,
                                preferred_element_type=F32)
            k_end = k * jnp.exp2(b_last - b)
            st_s[h] = st * jnp.exp2(b_last) + lax.dot_general(vb, k_end.astype(BF16), TN_DIMS,
                                                               preferred_element_type=F32)
            terms, off, mid = [], [], []
            for i in range(n_sub):
                lo = i * SUB
                qi, bi, ki = q[lo:lo + SUB], b[lo:lo + SUB], k[lo:lo + SUB]
                q3, b3 = (a.reshape(2, half, HEAD_DIM) for a in (qi, bi))
                terms += [(q3 * (key_rows(k_s, lo, j) * jnp.exp2(b3 - key_rows(b_s, lo, j))))
                          .reshape(SUB, HEAD_DIM).astype(BF16) for j in range(half)]
                rm = bi[half - 1:half]
                mid.append(lax.dot_general((qi * jnp.exp2(bi - rm)).astype(BF16),
                                           (ki * jnp.exp2(rm - bi)).astype(BF16), NT_DIMS,
                                           preferred_element_type=F32))
                if i > 0:
                    r = b[lo - 1:lo]
                    qt = (qi * jnp.exp2(bi - r)).astype(BF16)
                    kt = (k[:lo] * jnp.exp2(r - b[:lo])).astype(BF16)
                    off.append(lax.dot_general(qt, kt, NT_DIMS, preferred_element_type=F32))
            first.append([o, vb, off, jnp.concatenate(terms, axis=0), mid])
        for h in range(0, n_heads, 2):
            both = jnp.concatenate([first[h][3], first[h + 1][3]], axis=1)
            sums = jnp.dot(both, ones_kk, preferred_element_type=F32)
            first[h][3] = sums[:, :HEAD_DIM]
            first[h + 1][3] = sums[:, HEAD_DIM:]
        return first

    def chunk_finish(r0, first):
        for h in range(n_heads):
            o, vb, off, sums, mid = first[h]
            outs = []
            for i in range(n_sub):
                lo = i * SUB
                sc = jnp.zeros((SUB, HEAD_DIM), F32)
                for j in range(half):
                    row0 = (i * half + j) * SUB
                    sc = jnp.where(lane_sub == own_lane + j, sums[row0:row0 + SUB], sc)
                sc = jnp.where(row_sub >= lane_sub, sc, 0.0)[:, :SUB]
                sc = jnp.where(lower_left[:, :SUB], mid[i], sc)
                od = o[lo:lo + SUB] + jnp.dot(sc.astype(BF16), vb[lo:lo + SUB], preferred_element_type=F32)
                if i > 0:
                    od = od + jnp.dot(off[i - 1].astype(BF16), vb[:lo], preferred_element_type=F32)
                outs.append(od)
            o_s[pl.ds(r0, CHUNK), h * HEAD_DIM:(h + 1) * HEAD_DIM] = jnp.concatenate(outs, axis=0)

    def chunks_body(ci, carry):
        rows = [pl.multiple_of((ci * UNROLL + u) * CHUNK, CHUNK) for u in range(UNROLL)]
        started = [chunk_start(r0) for r0 in rows]
        for r0, first in zip(rows, started):
            chunk_finish(r0, first)
        return carry

    row8 = lax.broadcasted_iota(jnp.int32, (8, 1), 0)

    def scan_body(gi, carry):
        r0 = pl.multiple_of(gi * 8, 8)
        aa = a_s[pl.ds(r0, 8), :]
        uu = u_s[pl.ds(r0, 8), :]
        for s in (1, 2, 4):
            m = row8 >= s
            uu = jnp.where(m, aa * pltpu.roll(uu, s, 0) + uu, uu)
            aa = jnp.where(m, aa * pltpu.roll(aa, s, 0), aa)
        hh = aa * carry + uu
        u_s[pl.ds(r0, 8), :] = hh
        return hh[7:8, :]

    h_last = lax.fori_loop(0, lb_t // 8, scan_body, hcar_s[...])
    hcar_s[...] = h_last
    hout_ref[0] = h_last
    yb = (u_s[...] * _gelu_tanh(proj_s[:, xb0 + wb_w:xb0 + 2 * wb_w])).astype(BF16)
    yb_out_s[...] = jnp.dot(yb, wout_s[wa_w:, :], preferred_element_type=F32)

    assert lb_t % (CHUNK * UNROLL) == 0
    lax.fori_loop(0, lb_t // (CHUNK * UNROLL), chunks_body, 0)

    ya = []
    for h in range(n_heads):
        hs = slice(h * HEAD_DIM, (h + 1) * HEAD_DIM)
        oh = o_s[:, hs]
        oh = oh * lax.rsqrt(jnp.mean(oh * oh, axis=-1, keepdims=True) + EPS) * hgg_ref[:, hs]
        ya.append(oh * _silu(proj_s[:, 3 * wa_w + h * HEAD_DIM:3 * wa_w + (h + 1) * HEAD_DIM]))

    ya = jnp.concatenate(ya, axis=-1).astype(BF16)
    x1 = x + (jnp.dot(ya, wout_s[:wa_w, :], preferred_element_type=F32) + yb_out_s[...])
    x1_ref[0] = x1
    xn2 = _rms(x1, g2_ref[...]).astype(BF16)
    xn2_ref[0] = xn2
    logits_t = lax.dot_general(wr_ref[...], xn2, NT_DIMS, preferred_element_type=F32) + br_ref[...]
    gates_t, g_idx = _route_rows(logits_t, n_groups, per_group)
    gates_ref[0] = gates_t.T
    lane_c = lax.broadcasted_iota(jnp.int32, (1, ROUTER_LANES), 1)
    for t in range(lb_t // TOK_TILE):
        gi = g_idx[:, t * TOK_TILE:(t + 1) * TOK_TILE]
        cnt = jnp.zeros((1, ROUTER_LANES), jnp.int32)
        for g in range(n_groups):
            cnt = cnt + jnp.where(lane_c == g, jnp.sum(jnp.where(gi == g, 1, 0), axis=1, keepdims=True), 0)
        cnt_ref[0, t:t + 1, :] = cnt

    @pl.when(j == nj - 1)
    def _():
        for h in range(n_heads):
            sout_ref[0, h] = st_s[h].T


def _const_spec(shape):
    nd = len(shape)
    return pl.BlockSpec(shape, lambda *_: (0,) * nd)


def _mixer_prompt(x, p, lb_t):
    bsz, seq, d = x.shape
    wa_w = p["hgg"].shape[1]
    wb_w = p["cb"].shape[1]
    n_heads = wa_w // HEAD_DIM
    n_cols = p["win"].shape[1]
    weights = [p["lbw"], p["g1"], p["win"], p["hgg"], p["cw"], p["cb"], p["wx_bf"], p["bx"],
               p["wa_bf"], p["ba"], p["lam"], p["wout"], p["g2"], p["wr_t_bf"], p["br_t"]]
    nj = seq // lb_t
    tile = lambda w: pl.BlockSpec((1, lb_t, w), lambda b, j: (b, j, 0))
    n_exp = p["w1"].shape[0]
    per_step = next(k for k in range(-(-n_exp // (bsz * nj)), n_exp + 1) if n_exp % k == 0)
    exp_spec = pl.BlockSpec((per_step,) + p["w1"].shape[1:],
                            lambda b, j: (jnp.minimum(b * nj + j, n_exp // per_step - 1), 0, 0))
    out_shape = (
        jax.ShapeDtypeStruct((bsz, seq, d), F32),
        jax.ShapeDtypeStruct((bsz, seq, d), BF16),
        jax.ShapeDtypeStruct((bsz, seq, ROUTER_LANES), F32),
        jax.ShapeDtypeStruct((bsz * (seq // lb_t), lb_t // TOK_TILE, ROUTER_LANES), jnp.int32),
        jax.ShapeDtypeStruct((bsz, n_heads, HEAD_DIM, HEAD_DIM), F32),
        jax.ShapeDtypeStruct((bsz, 1, wb_w), F32),
        jax.ShapeDtypeStruct((bsz, 3, wb_w), F32),
        jax.ShapeDtypeStruct(p["w1"].shape, BF16),
        jax.ShapeDtypeStruct(p["w3"].shape, BF16),
    )
    out_specs = (
        tile(d), tile(d), tile(ROUTER_LANES),
        pl.BlockSpec((1, lb_t // TOK_TILE, ROUTER_LANES), lambda b, j: (b * nj + j, 0, 0)),
        pl.BlockSpec((1, n_heads, HEAD_DIM, HEAD_DIM), lambda b, j: (b, 0, 0, 0)),
        pl.BlockSpec((1, 1, wb_w), lambda b, j: (b, 0, 0)),
        pl.BlockSpec((1, 3, wb_w), lambda b, j: (b, 0, 0)),
        exp_spec, exp_spec,
    )
    scratch = [
        pltpu.VMEM((lb_t, n_cols), F32),
        pltpu.VMEM((n_heads, lb_t, HEAD_DIM), F32),
        pltpu.VMEM((n_heads, lb_t, HEAD_DIM), F32),
        pltpu.VMEM((lb_t, wa_w), F32),
        pltpu.VMEM((n_heads, HEAD_DIM, HEAD_DIM), F32),
        pltpu.VMEM((lb_t + 8, wb_w), F32),
        pltpu.VMEM((lb_t, wb_w), F32),
        pltpu.VMEM((lb_t, wb_w), F32),
        pltpu.VMEM((1, wb_w), F32),
        pltpu.VMEM((lb_t, d), F32),
        pltpu.VMEM((d, n_cols), BF16),
        pltpu.VMEM(p["wout"].shape, BF16),
    ]
    weight_specs = [pl.BlockSpec(w.shape, lambda *_, nd=w.ndim: (0,) * nd, pipeline_mode=pl.Buffered(1))
                    for w in weights]
    kern = functools.partial(_mixer_prompt_kernel, n_groups=p["n_groups"], per_group=p["per_group"])
    return pl.pallas_call(
        kern,
        grid=(bsz, nj),
        in_specs=[tile(d)] + weight_specs + [exp_spec, exp_spec],
        out_specs=out_specs,
        out_shape=out_shape,
        scratch_shapes=scratch,
        compiler_params=pltpu.CompilerParams(dimension_semantics=("arbitrary", "arbitrary"),
                                             vmem_limit_bytes=MIXER_VMEM_LIMIT_BYTES),
        name="mixer_prompt",
    )(x, *weights, p["w1"], p["w3"])


def _seg_pad(n):
    return jnp.bitwise_and(n + (SEG - 1), -SEG)


def _row_tile_pad(n):
    return jnp.bitwise_and(n + (ROW_TILE - 1), -ROW_TILE)


def _tile_segments(cnt_ref, tile, n_groups):
    padded = [_seg_pad(cnt_ref[tile * n_groups + g]) for g in range(n_groups)]
    starts, acc = [], jnp.int32(0)
    for g in range(n_groups):
        starts.append(acc)
        acc = acc + padded[g]
    return padded, starts, acc


def _group_bases(cnt_ref, n_tiles, n_groups):
    def body(t, tot):
        return tuple(tot[g] + _seg_pad(cnt_ref[t * n_groups + g]) for g in range(n_groups))
    totals = lax.fori_loop(0, n_tiles, body, (jnp.int32(0),) * n_groups)
    bases, ends, acc = [], [], jnp.int32(0)
    for g in range(n_groups):
        bases.append(acc)
        acc = acc + _row_tile_pad(totals[g])
        ends.append(acc)
    return bases, ends, totals


def _chunk_rows(n_chunks, starts, gstart):
    rows, row = [], None
    for c in range(n_chunks):
        row = gstart[0] if c == 0 else row + SEG
        for g in range(1, len(starts)):
            row = jnp.where(starts[g] == c * SEG, gstart[g], row)
        rows.append(pl.multiple_of(row, SEG))
    return rows


def _sort_matrix_rows(gates, starts):
    n_groups = len(starts)
    n_rows = TOK_TILE + n_groups * SEG
    col = lax.broadcasted_iota(jnp.int32, (TOK_TILE, ROUTER_LANES), 1)
    onehot_t = jnp.where(col == gates[:, 0:1].astype(jnp.int32), 1.0, 0.0).T
    r_i = lax.broadcasted_iota(jnp.int32, (TOK_TILE, TOK_TILE), 0)
    c_i = lax.broadcasted_iota(jnp.int32, (TOK_TILE, TOK_TILE), 1)
    later = jnp.where(r_i < c_i, 1.0, 0.0).astype(BF16)
    before = jnp.dot(onehot_t.astype(BF16), later, preferred_element_type=F32)
    g_row = lax.broadcasted_iota(jnp.int32, (ROUTER_LANES, 1), 0)
    base = jnp.zeros((ROUTER_LANES, 1), jnp.int32)
    for g in range(n_groups):
        base = base + jnp.where(g_row == g, starts[g], 0)
    dest = jnp.sum(jnp.where(onehot_t > 0.0, before + base.astype(F32), 0.0), axis=0, keepdims=True)
    placed = jnp.sum(onehot_t, axis=0, keepdims=True) > 0.0
    row = lax.broadcasted_iota(jnp.int32, (n_rows, TOK_TILE), 0)
    dest = jnp.where(placed, dest, -1.0)
    return row == dest.astype(jnp.int32), dest


def _dispatch_copies(xbuf, gbuf, xs_hbm, gs_hbm, sem, slot, c, row):
    return (pltpu.make_async_copy(xbuf.at[slot, pl.ds(c * SEG, SEG), :], xs_hbm.at[pl.ds(row, SEG), :], sem.at[slot]),
            pltpu.make_async_copy(gbuf.at[slot, pl.ds(c * SEG, SEG), :], gs_hbm.at[pl.ds(row, SEG), :], sem.at[slot]))


def _moe_dispatch_kernel(cnt_ref, xn2_ref, gates_ref, xn2b_ref, gatesb_ref, xs_hbm, gs_hbm, tg_ref, dest_ref,
                         xbuf, gbuf, sem, gstart_s, zx, zg, zsem, *, n_groups, per_group, n_first):
    j = pl.program_id(0)
    n_tiles = pl.num_programs(0)
    n_chunks = (TOK_TILE + n_groups * SEG) // SEG
    slot = lax.rem(j, 2)

    @pl.when(j == 0)
    def _():
        bases, ends, totals = _group_bases(cnt_ref, n_tiles, n_groups)
        for g in range(n_groups):
            gstart_s[g] = bases[g]
        n_rt = tg_ref.shape[0] - 1
        for i in range(n_rt):
            tg = jnp.int32(0)
            for g in range(n_groups):
                tg = tg + jnp.where(ends[g] <= i * ROW_TILE, 1, 0)
            tg_ref[i] = tg
        n_used = ends[n_groups - 1] // ROW_TILE
        tg_ref[n_rt] = n_used

        zx[...] = jnp.zeros_like(zx)
        zg[...] = jnp.zeros_like(zg)

        def zero_copies(row, n):
            return (pltpu.make_async_copy(zx.at[pl.ds(0, n), :], xs_hbm.at[pl.ds(row, n), :], zsem.at[0]),
                    pltpu.make_async_copy(zg.at[pl.ds(0, n), :], gs_hbm.at[pl.ds(row, n), :], zsem.at[0]))

        def for_each_gap(act):
            for g in range(n_groups):
                tail = bases[g] + totals[g]

                def seg_body(k, c, tail=tail):
                    for cp in zero_copies(pl.multiple_of(tail + k * SEG, SEG), SEG):
                        act(cp)
                    return c
                lax.fori_loop(0, (ends[g] - tail) // SEG, seg_body, 0)

            def tile_body(k, c):
                for cp in zero_copies(pl.multiple_of((n_used + k) * ROW_TILE, ROW_TILE), ROW_TILE):
                    act(cp)
                return c
            lax.fori_loop(0, n_rt - n_used, tile_body, 0)

        for_each_gap(lambda cp: cp.start())
        for_each_gap(lambda cp: cp.wait())

    def wait_tile(tile, slot_):
        _, _, used = _tile_segments(cnt_ref, tile, n_groups)
        for c in range(n_chunks):
            @pl.when(c * SEG < used)
            def _():
                for cp in _dispatch_copies(xbuf, gbuf, xs_hbm, gs_hbm, sem, slot_, c, 0):
                    cp.wait()

    @pl.when(j >= 2)
    def _():
        wait_tile(j - 2, slot)

    padded, starts, used = _tile_segments(cnt_ref, j, n_groups)
    second = j >= n_first
    gates = jnp.where(second, gatesb_ref[...], gates_ref[...])
    xn2 = jnp.where(second, xn2b_ref[...], xn2_ref[...])
    d = xn2_ref.shape[1]
    g_hi = gates.astype(BF16)
    g_lo = (gates - g_hi.astype(F32)).astype(BF16)
    onehot, dest = _sort_matrix_rows(gates, starts)
    dest_ref[0] = jnp.broadcast_to(dest, dest_ref.shape[1:])
    sort_m = jnp.where(onehot, 1.0, 0.0).astype(BF16)
    moved = jnp.dot(sort_m, jnp.concatenate([xn2, g_hi, g_lo], axis=1), preferred_element_type=F32)
    xbuf[slot] = moved[:, :d].astype(BF16)
    gbuf[slot] = moved[:, d:d + ROUTER_LANES] + moved[:, d + ROUTER_LANES:]
    gstart = [gstart_s[g] for g in range(n_groups)]
    rows = _chunk_rows(n_chunks, starts, gstart)
    for c in range(n_chunks):
        @pl.when(c * SEG < used)
        def _():
            for cp in _dispatch_copies(xbuf, gbuf, xs_hbm, gs_hbm, sem, slot, c, rows[c]):
                cp.start()
    for g in range(n_groups):
        gstart_s[g] = gstart[g] + padded[g]

    @pl.when(j == n_tiles - 1)
    def _():
        @pl.when(j >= 1)
        def _():
            wait_tile(j - 1, 1 - slot)
        wait_tile(j, slot)


def _moe_group_kernel(tg_ref, xs_ref, gs_ref, w1_ref, w3_ref, w2_ref, ys_ref, w2_s, *, n_groups, per_group):
    i = pl.program_id(0)

    @pl.when((i == 0) | (tg_ref[i] != tg_ref[jnp.maximum(i - 1, 0)]))
    def _():
        w2_s[...] = w2_ref[...].astype(BF16)

    @pl.when(tg_ref[i] < n_groups)
    def _():
        xb = xs_ref[...]
        gates = pltpu.roll(gs_ref[...], ROUTER_LANES - PROMPT_EXPERT_LANE - per_group * tg_ref[i], 1)
        acc = jnp.zeros(ys_ref.shape, F32)
        for s in range(per_group):
            h = (_silu(jnp.dot(xb, w1_ref[s], preferred_element_type=F32))
                 * jnp.dot(xb, w3_ref[s], preferred_element_type=F32) * gates[:, s:s + 1])
            acc = acc + jnp.dot(h.astype(BF16), w2_s[s], preferred_element_type=F32)
        ys_ref[...] = acc.astype(BF16)


def _moe_combine_kernel(cnt_ref, dest_ref, x1_ref, x1b_ref, gf_ref, ys_hbm, y_ref, yb_ref,
                        ybuf, sem, gstart_s, *, n_groups, n_first):
    j = pl.program_id(0)
    n_tiles = pl.num_programs(0)
    n_chunks = (TOK_TILE + n_groups * SEG) // SEG
    slot = lax.rem(j, 2)

    def copies(slot_, c, row):
        return pltpu.make_async_copy(ys_hbm.at[pl.ds(row, SEG), :], ybuf.at[slot_, pl.ds(c * SEG, SEG), :],
                                     sem.at[slot_])

    def fetch_tile(tile, slot_):
        padded, starts, _ = _tile_segments(cnt_ref, tile, n_groups)
        gstart = [gstart_s[g] for g in range(n_groups)]
        for c, row in enumerate(_chunk_rows(n_chunks, starts, gstart)):
            copies(slot_, c, row).start()
        for g in range(n_groups):
            gstart_s[g] = gstart[g] + padded[g]

    @pl.when(j == 0)
    def _():
        bases, _, _ = _group_bases(cnt_ref, n_tiles, n_groups)
        for g in range(n_groups):
            gstart_s[g] = bases[g]
        if n_chunks * SEG < SORT_ROWS:
            ybuf[:, n_chunks * SEG:, :] = jnp.zeros((2, SORT_ROWS - n_chunks * SEG, ybuf.shape[2]), BF16)
        fetch_tile(0, 0)

    @pl.when(j + 1 < n_tiles)
    def _():
        fetch_tile(j + 1, 1 - slot)

    for c in range(n_chunks):
        copies(slot, c, 0).wait()
    second = j >= n_first
    target = dest_ref[0].T[:, 0:1].astype(jnp.int32)
    sort_t = lax.broadcasted_iota(jnp.int32, (TOK_TILE, SORT_ROWS), 1) == target
    moe = jnp.dot(jnp.where(sort_t, 1.0, 0.0).astype(BF16), ybuf[slot], preferred_element_type=F32)

    @pl.when(jnp.logical_not(second))
    def _():
        y_ref[...] = _rms(x1_ref[...] + moe, gf_ref[...])

    @pl.when(second)
    def _():
        yb_ref[...] = _rms(x1b_ref[...] + moe, gf_ref[...])


def _moe_sorted(first, second, w1_bf, w3_bf, p):
    x1, xn2, gates, counts = first
    x1b, xn2b, gatesb, countsb = second
    t, d = x1.shape
    tb = x1b.shape[0]
    n_groups, per_group = p["n_groups"], p["per_group"]
    assert n_groups <= MAX_GROUPS and t % TOK_TILE == 0 and tb % TOK_TILE == 0
    n_first = t // TOK_TILE
    n_tiles = n_first + tb // TOK_TILE
    d_exp = w1_bf.shape[2]
    cap = t + tb + n_tiles * n_groups * SEG + n_groups * ROW_TILE + SORT_ROWS
    n_rt = -(-cap // ROW_TILE)
    rows = n_rt * ROW_TILE
    cnt = jnp.concatenate([counts[..., :n_groups].reshape(-1), countsb[..., :n_groups].reshape(-1)])
    params = pltpu.CompilerParams(dimension_semantics=("arbitrary",), vmem_limit_bytes=VMEM_LIMIT_BYTES)
    any_spec = pl.BlockSpec(memory_space=pl.ANY)
    tok = lambda w: pl.BlockSpec((TOK_TILE, w), lambda j, c: (jnp.minimum(j, n_first - 1), 0))
    tokb = lambda w: pl.BlockSpec((TOK_TILE, w), lambda j, c: (jnp.maximum(j - n_first, 0), 0))

    dest_spec = pl.BlockSpec((1, 8, TOK_TILE), lambda j, c: (j, 0, 0))

    xs, gs, tile_group, dest = pl.pallas_call(
        functools.partial(_moe_dispatch_kernel, n_groups=n_groups, per_group=per_group, n_first=n_first),
        grid_spec=pltpu.PrefetchScalarGridSpec(
            num_scalar_prefetch=1, grid=(n_tiles,),
            in_specs=[tok(d), tok(ROUTER_LANES), tokb(d), tokb(ROUTER_LANES)],
            out_specs=(any_spec, any_spec, pl.BlockSpec(memory_space=pltpu.SMEM), dest_spec),
            scratch_shapes=[pltpu.VMEM((2, TOK_TILE + n_groups * SEG, d), BF16),
                            pltpu.VMEM((2, TOK_TILE + n_groups * SEG, ROUTER_LANES), F32),
                            pltpu.SemaphoreType.DMA((2,)), pltpu.SMEM((n_groups,), jnp.int32),
                            pltpu.VMEM((ROW_TILE, d), BF16), pltpu.VMEM((ROW_TILE, ROUTER_LANES), F32),
                            pltpu.SemaphoreType.DMA((1,))]),
        out_shape=(jax.ShapeDtypeStruct((rows, d), BF16), jax.ShapeDtypeStruct((rows, ROUTER_LANES), F32),
                   jax.ShapeDtypeStruct((n_rt + 1,), jnp.int32),
                   jax.ShapeDtypeStruct((n_tiles, 8, TOK_TILE), F32)),
        compiler_params=params, name="moe_dispatch",
    )(cnt, xn2, gates, xn2b, gatesb)

    used_tile = lambda i, tg: (jnp.minimum(i, tg[n_rt] - 1), 0)
    group_w = lambda i, tg: (jnp.minimum(tg[i], n_groups - 1), 0, 0)
    ys = pl.pallas_call(
        functools.partial(_moe_group_kernel, n_groups=n_groups, per_group=per_group),
        grid_spec=pltpu.PrefetchScalarGridSpec(
            num_scalar_prefetch=1, grid=(n_rt,),
            in_specs=[pl.BlockSpec((ROW_TILE, d), used_tile), pl.BlockSpec((ROW_TILE, ROUTER_LANES), used_tile),
                      pl.BlockSpec((per_group, d, d_exp), group_w), pl.BlockSpec((per_group, d, d_exp), group_w),
                      pl.BlockSpec((per_group, d_exp, d), group_w)],
            out_specs=pl.BlockSpec((ROW_TILE, d), used_tile),
            scratch_shapes=[pltpu.VMEM((per_group, d_exp, d), BF16)]),
        out_shape=jax.ShapeDtypeStruct((rows, d), BF16),
        input_output_aliases={1: 0},
        compiler_params=params, name="moe_experts",
    )(tile_group, xs, gs, w1_bf, w3_bf, p["w2"])

    return pl.pallas_call(
        functools.partial(_moe_combine_kernel, n_groups=n_groups, n_first=n_first),
        grid_spec=pltpu.PrefetchScalarGridSpec(
            num_scalar_prefetch=1, grid=(n_tiles,),
            in_specs=[dest_spec, tok(d), tokb(d), pl.BlockSpec(p["gf"].shape, lambda j, c: (0, 0)), any_spec],
            out_specs=(tok(d), tokb(d)),
            scratch_shapes=[pltpu.VMEM((2, SORT_ROWS, d), BF16), pltpu.SemaphoreType.DMA((2,)),
                            pltpu.SMEM((n_groups,), jnp.int32)]),
        out_shape=(jax.ShapeDtypeStruct((t, d), F32), jax.ShapeDtypeStruct((tb, d), F32)),
        compiler_params=params, name="moe_combine",
    )(cnt, dest, x1, x1b, p["gf"], ys)


def _sample_in_kernel(x_ref, c0_ref, c1_ref, c2_ref, h0_ref, lbw_ref, g1_ref, win_ref, cw_ref, cb_ref,
                      wx_ref, bx_ref, wa_ref, ba_ref, lam_ref,
                      q_ref, f_ref, v_ref, g_ref, yb_ref, hnew_ref, xr_ref):
    wa_w = v_ref.shape[1]
    wb_w = yb_ref.shape[1]
    xn = _rms(x_ref[...], g1_ref[...])
    proj = jnp.dot(xn, win_ref[...], precision=HIGHEST, preferred_element_type=F32)
    lb = _forget_lower_bound(lbw_ref[...])
    f = lb + (1.0 - lb) * _sigmoid(proj[:, wa_w:2 * wa_w])
    q_ref[...] = proj[:, 0:wa_w].T
    f_ref[...] = f.T
    v_ref[...] = proj[:, 2 * wa_w:3 * wa_w]
    g_ref[...] = proj[:, 3 * wa_w:4 * wa_w]
    xr = proj[:, 4 * wa_w:4 * wa_w + wb_w]
    xr_ref[...] = xr
    xc = (cb_ref[...] + cw_ref[0:1, :] * c0_ref[...] + cw_ref[1:2, :] * c1_ref[...]
          + cw_ref[2:3, :] * c2_ref[...] + cw_ref[3:4, :] * xr)
    gate_x = _sigmoid(jnp.dot(xc, wx_ref[...], precision=HIGHEST, preferred_element_type=F32) + bx_ref[...])
    gate_a = _sigmoid(jnp.dot(xc, wa_ref[...], precision=HIGHEST, preferred_element_type=F32) + ba_ref[...])
    log_a = (-LRU_C) * gate_a * _softplus(-lam_ref[...])
    a = jnp.exp(log_a)
    mult = jnp.sqrt(-_expm1(2.0 * log_a))
    h = a * h0_ref[...] + gate_x * xc * mult
    hnew_ref[...] = h
    yb_ref[...] = h * _gelu_tanh(proj[:, 4 * wa_w + wb_w:4 * wa_w + 2 * wb_w])


def _sample_state_kernel(s_ref, qt_ref, ft_ref, v_ref, snew_ref, o_ref):
    tb, n_heads = s_ref.shape[0], s_ref.shape[1]
    n_tok = qt_ref.shape[1]
    shift = lax.rem(n_tok - pl.program_id(0) * tb, n_tok)
    assert 3 * tb <= HEAD_DIM
    lane = lax.broadcasted_iota(jnp.int32, (HEAD_DIM, n_tok), 1)
    r_e = lax.broadcasted_iota(jnp.int32, (n_tok, tb * HEAD_DIM), 0)
    c_e = lax.broadcasted_iota(jnp.int32, (n_tok, tb * HEAD_DIM), 1)
    spread = jnp.where((r_e < 3 * tb) & (lax.rem(r_e, tb) == c_e // HEAD_DIM), 1.0, 0.0).astype(BF16)

    def pieces_of(cols):
        hi = cols.astype(BF16).astype(F32)
        rest = cols - hi
        mid = rest.astype(BF16).astype(F32)
        low = rest - mid
        pieces = jnp.where(lane < tb, hi, jnp.where(lane < 2 * tb, pltpu.roll(mid, tb, 1),
                                                    jnp.where(lane < 3 * tb, pltpu.roll(low, 2 * tb, 1), 0.0)))
        return pieces.astype(BF16)

    for h in range(n_heads):
        hs = slice(h * HEAD_DIM, (h + 1) * HEAD_DIM)
        q_p = pieces_of(pltpu.roll(qt_ref[hs, :], shift, 1))
        f_p = pieces_of(pltpu.roll(ft_ref[hs, :], shift, 1))
        rows = []
        for t in range(tb):
            one = spread[:, t * HEAD_DIM:(t + 1) * HEAD_DIM]
            f_t = jnp.dot(f_p, one, preferred_element_type=F32)
            q_t = jnp.dot(q_p, one, preferred_element_type=F32)
            s_new = f_t * s_ref[t, h] + (1.0 - f_t) * v_ref[t:t + 1, hs]
            snew_ref[t, h] = s_new
            rows.append(jnp.sum(q_t * s_new, axis=0, keepdims=True))
        o_ref[:, hs] = jnp.concatenate(rows, axis=0)


def _sample_out_kernel(x_ref, o_ref, g_ref, yb_ref, hgg_ref, wout_ref, g2_ref, wr_ref, br_ref,
                       x1_ref, xn2_ref, gates_ref, cnt_ref, *, n_groups, per_group):
    wa_w = o_ref.shape[1]
    ya = []
    for h in range(wa_w // HEAD_DIM):
        hs = slice(h * HEAD_DIM, (h + 1) * HEAD_DIM)
        oh = o_ref[:, hs]
        oh = oh * lax.rsqrt(jnp.mean(oh * oh, axis=-1, keepdims=True) + EPS) * hgg_ref[:, hs]
        ya.append(oh * _silu(g_ref[:, hs]))
    y = jnp.concatenate(ya + [yb_ref[...]], axis=-1)
    x1 = x_ref[...] + jnp.dot(y, wout_ref[...], precision=HIGHEST, preferred_element_type=F32)
    xn2 = _rms(x1, g2_ref[...])
    logits_t = lax.dot_general(wr_ref[...], xn2, NT_DIMS, precision=HIGHEST,
                               preferred_element_type=F32) + br_ref[...]
    gates_t, g_idx = _route_rows(logits_t, n_groups, per_group)
    n = x1.shape[0]
    pad = x1_ref.shape[0] - n
    x1_ref[0:n, :] = x1
    xn2_ref[0:n, :] = xn2.astype(BF16)
    gates_ref[0:n, :] = gates_t.T
    if pad:
        x1_ref[n:, :] = jnp.zeros((pad, x1.shape[1]), F32)
        xn2_ref[n:, :] = jnp.zeros((pad, x1.shape[1]), BF16)
        gates_ref[n:, :] = jnp.where(lax.broadcasted_iota(jnp.int32, (pad, ROUTER_LANES), 1) == 0, -1.0, 0.0)
    lane_c = lax.broadcasted_iota(jnp.int32, (1, ROUTER_LANES), 1)
    tok = lax.broadcasted_iota(jnp.int32, g_idx.shape, 1)
    n_tiles = cnt_ref.shape[0]
    for t in range(n_tiles):
        in_tile = (tok >= t * TOK_TILE) & (tok < (t + 1) * TOK_TILE)
        cnt = jnp.zeros((1, ROUTER_LANES), jnp.int32)
        for g in range(n_groups):
            here = jnp.sum(jnp.where(in_tile & (g_idx == g), 1, 0), axis=1, keepdims=True)
            cnt = cnt + jnp.where(lane_c == g, here, 0)
        cnt_ref[t:t + 1, :] = cnt


def _whole(kernel, out_shape, *args, name):
    return pl.pallas_call(
        kernel, out_shape=out_shape,
        compiler_params=pltpu.CompilerParams(vmem_limit_bytes=VMEM_LIMIT_BYTES), name=name)(*args)


def _mixer_sample(x, s0, h0, c0, p, tb):
    n, d = x.shape
    wa_w = p["hgg"].shape[1]
    wb_w = p["cb"].shape[1]
    n_heads = wa_w // HEAD_DIM
    sd = lambda w: jax.ShapeDtypeStruct((n, w), F32)
    key_major = jax.ShapeDtypeStruct((wa_w, n), F32)
    q, f, v, g, yb, h_new, xr = _whole(
        _sample_in_kernel, (key_major,) * 2 + (sd(wa_w),) * 2 + (sd(wb_w),) * 3,
        x, c0[:, 0, :], c0[:, 1, :], c0[:, 2, :], h0, p["lbw"], p["g1"], p["win"], p["cw"], p["cb"],
        p["wx"], p["bx"], p["wa"], p["ba"], p["lam"], name="sample_in")
    tok = lambda w: pl.BlockSpec((tb, w), lambda i: (i, 0))
    st = pl.BlockSpec((tb, n_heads, HEAD_DIM, HEAD_DIM), lambda i: (i, 0, 0, 0))
    s_new, o = pl.pallas_call(
        _sample_state_kernel,
        grid=(n // tb,),
        in_specs=[st, _const_spec((wa_w, n)), _const_spec((wa_w, n)), tok(wa_w)],
        out_specs=(st, tok(wa_w)),
        out_shape=(jax.ShapeDtypeStruct(s0.shape, F32), sd(wa_w)),
        compiler_params=pltpu.CompilerParams(dimension_semantics=("arbitrary",),
                                             vmem_limit_bytes=VMEM_LIMIT_BYTES),
        name="sample_state",
    )(s0, q, f, v)
    n_pad = -(-n // TOK_TILE) * TOK_TILE
    x1, xn2, gates, counts = _whole(
        functools.partial(_sample_out_kernel, n_groups=p["n_groups"], per_group=p["per_group"]),
        (jax.ShapeDtypeStruct((n_pad, d), F32), jax.ShapeDtypeStruct((n_pad, d), BF16),
         jax.ShapeDtypeStruct((n_pad, ROUTER_LANES), F32),
         jax.ShapeDtypeStruct((n_pad // TOK_TILE, ROUTER_LANES), jnp.int32)),
        x, o, g, yb, p["hgg"], p["wout"], p["g2"], p["wr_t"], p["br_t"], name="sample_out")
    c_new = jnp.stack([c0[:, 1, :], c0[:, 2, :], xr], axis=1)
    return x1, xn2, gates, counts, s_new, h_new, c_new


def _block_diag(w):
    n, c, _ = w.shape
    eye = jnp.eye(n, dtype=w.dtype)
    return (w[:, :, None, :] * eye[:, None, :, None]).reshape(n * c, n * c)


def _prepare(lower_bounds, ln1_g, w_in, hgrn_norm_g, conv_w, conv_b, lru_wx, lru_bx, lru_wa, lru_ba,
             lru_lambda, w_out, ln2_g, router_wg, router_bg, router_we, router_be, exp_w1, exp_w3,
             exp_w2, final_g):
    d = w_in.shape[1]
    n_groups = router_wg.shape[-1]
    per_group = router_we.shape[-1]
    row = lambda a: a.reshape(1, -1).astype(F32)
    we = jnp.transpose(router_we[0], (1, 0, 2)).reshape(d, n_groups * per_group)
    rows_t = -(-(8 + n_groups * per_group) // 16) * 16
    wr_t = jnp.concatenate([router_wg[0].T, jnp.zeros((8 - n_groups, d), F32), we.T,
                            jnp.zeros((rows_t - 8 - n_groups * per_group, d), F32)], axis=0)
    br_t = jnp.concatenate([router_bg[0], jnp.zeros((8 - n_groups,), F32), router_be[0].reshape(-1),
                            jnp.zeros((rows_t - 8 - n_groups * per_group,), F32)]).reshape(-1, 1)
    wx = _block_diag(lru_wx[0])
    wa = _block_diag(lru_wa[0])
    return dict(
        wr_t=wr_t, wr_t_bf=wr_t.astype(BF16), br_t=br_t,
        n_groups=n_groups, per_group=per_group,
        lbw=lower_bounds.astype(F32), g1=row(ln1_g[0]), win=w_in[0],
        hgg=row(hgrn_norm_g[0]), cw=conv_w[0], cb=row(conv_b[0]),
        wx=wx, wx_bf=wx.astype(BF16), bx=row(lru_bx[0]), wa=wa, wa_bf=wa.astype(BF16), ba=row(lru_ba[0]),
        lam=row(lru_lambda[0]), wout=w_out[0], g2=row(ln2_g[0]),
        w1=exp_w1[0], w3=exp_w3[0],
        w2=exp_w2[0], gf=row(final_g))


def kernel(x_prompt, x_sample, state_hgrn, state_rglru, state_conv, lower_bounds, ln1_g, w_in, hgrn_norm_g, conv_w, conv_b, lru_wx, lru_bx, lru_wa, lru_ba, lru_lambda, w_out, ln2_g, router_wg, router_bg, router_we, router_be, exp_w1, exp_w3, exp_w2, final_g):
    assert w_in.shape[0] == 1, "single-layer trunk"
    p = _prepare(lower_bounds, ln1_g, w_in, hgrn_norm_g, conv_w, conv_b, lru_wx, lru_bx, lru_wa, lru_ba,
                 lru_lambda, w_out, ln2_g, router_wg, router_bg, router_we, router_be, exp_w1, exp_w3,
                 exp_w2, final_g)
    bsz, seq, d = x_prompt.shape
    x1, xn2, gates, counts, s_p, h_p, c_p, w1_bf, w3_bf = _mixer_prompt(x_prompt, p, min(MIXER_BLOCK, seq))
    t = bsz * seq
    n = x_sample.shape[0]
    x1s, xn2s, gates_s, counts_s, s_s, h_s, c_s = _mixer_sample(
        x_sample[:, 0, :], state_hgrn[0], state_rglru[0], state_conv[0], p, SAMPLE_STEP_TOKENS)
    y_p, y_s = _moe_sorted((x1.reshape(t, d), xn2.reshape(t, d), gates.reshape(t, ROUTER_LANES), counts),
                           (x1s, xn2s, gates_s, counts_s), w1_bf, w3_bf, p)
    return (y_p.reshape(bsz, seq, d), y_s[:n].reshape(n, 1, d),
            s_p[None], h_p.reshape(1, bsz, -1), c_p[None],
            s_s[None], h_s[None], c_s[None])
```

```python
import functools

import jax
import jax.numpy as jnp
from jax import lax
from jax.experimental import pallas as pl
from jax.experimental.pallas import tpu as pltpu

F32 = jnp.float32
BF16 = jnp.bfloat16
HIGHEST = lax.Precision.HIGHEST

EPS = 1e-6
LRU_C = 8.0
LOG2E = 1.4426950408889634
HEAD_DIM = 128
CHUNK = 64
SUB = 16
UNROLL = 8
ROUTER_LANES = 128
PROMPT_EXPERT_LANE = 8
MIXER_BLOCK = 512
SAMPLE_STEP_TOKENS = 16
TOK_TILE = 512
SEG = 16
MAX_GROUPS = 8
SORT_ROWS = TOK_TILE + MAX_GROUPS * SEG
ROW_TILE = 512
VMEM_LIMIT_BYTES = 56 * 1024 * 1024
MIXER_VMEM_LIMIT_BYTES = 60 * 1024 * 1024

NT_DIMS = (((1,), (1,)), ((), ()))
TN_DIMS = (((0,), (0,)), ((), ()))


def _rms(x, g):
    return x * lax.rsqrt(jnp.mean(x * x, axis=-1, keepdims=True) + EPS) * g


def _sigmoid(x):
    return 1.0 / (1.0 + jnp.exp(-x))


def _silu(x):
    return x * _sigmoid(x)


def _gelu_tanh(x):
    c = 0.7978845608028654
    return x * (0.5 * (1.0 + jnp.tanh(c * (x + 0.044715 * (x * x * x)))))


def _softplus(z):
    return jnp.maximum(z, 0.0) + jnp.log1p(jnp.exp(-jnp.abs(z)))


def _expm1(x):
    u = jnp.exp(x)
    um1 = u - 1.0
    small = um1 * x / jnp.log(u)
    return jnp.where(um1 == 0.0, x, jnp.where(jnp.abs(x) < 0.5, small, um1))


def _forget_lower_bound(lbw):
    m = jnp.max(lbw, axis=0, keepdims=True)
    e = jnp.exp(lbw - m)
    return e[0:1, :] / jnp.sum(e, axis=0, keepdims=True)


def _route_rows(logits_t, n_groups, per_group):
    assert n_groups <= 8 and per_group == 8
    n_tok = logits_t.shape[1]
    row = lax.broadcasted_iota(jnp.int32, (8, n_tok), 0)
    neg = jnp.float32(-jnp.inf)
    big = jnp.int32(8)
    lg = jnp.where(row < n_groups, logits_t[0:8], neg)
    mg = jnp.max(lg, axis=0, keepdims=True)
    g_idx = jnp.min(jnp.where(lg == mg, row, big), axis=0, keepdims=True)
    p_top = 1.0 / jnp.sum(jnp.where(row < n_groups, jnp.exp(lg - mg), 0.0), axis=0, keepdims=True)
    sel = logits_t[8:16]
    for g in range(1, n_groups):
        sel = jnp.where(g_idx == g, logits_t[8 + 8 * g:16 + 8 * g], sel)
    m1 = jnp.max(sel, axis=0, keepdims=True)
    i1 = jnp.min(jnp.where(sel == m1, row, big), axis=0, keepdims=True)
    sel2 = jnp.where(row == i1, neg, sel)
    m2 = jnp.max(sel2, axis=0, keepdims=True)
    i2 = jnp.min(jnp.where(sel2 == m2, row, big), axis=0, keepdims=True)
    e2 = jnp.exp(m2 - m1)
    den = 1.0 + e2
    w1 = p_top / den
    w2 = p_top * (e2 / den)
    own = jnp.where(row == i1, w1, 0.0) + jnp.where(row == i2, w2, 0.0)
    blocks = [jnp.where(row == 0, g_idx.astype(F32), 0.0)]
    blocks += [jnp.where(g_idx == g, own, 0.0) for g in range(n_groups)]
    blocks += [jnp.zeros((8, n_tok), F32)] * (ROUTER_LANES // 8 - len(blocks))
    return jnp.concatenate(blocks, axis=0), g_idx


def _mixer_prompt_kernel(x_ref, lbw_ref, g1_ref, win_ref, hgg_ref, cw_ref, cb_ref, wx_ref, bx_ref,
                         wa_ref, ba_ref, lam_ref, wout_ref, g2_ref, wr_ref, br_ref, w1_ref, w3_ref,
                         x1_ref, xn2_ref, gates_ref, cnt_ref, sout_ref, hout_ref, cout_ref, w1o_ref, w3o_ref,
                         proj_s, k_s, b_s, o_s, st_s, xr_s, a_s, u_s, hcar_s, yb_out_s, win_s, wout_s,
                         *, n_groups, per_group):
    lb_t = x_ref.shape[1]
    wa_w = o_s.shape[1]
    wb_w = a_s.shape[1]
    n_heads = wa_w // HEAD_DIM
    j = pl.program_id(1)
    nj = pl.num_programs(1)

    @pl.when((pl.program_id(0) == 0) & (j == 0))
    def _():
        for c in range(0, win_ref.shape[1], wa_w):
            win_s[:, c:c + wa_w] = win_ref[:, c:c + wa_w].astype(BF16)
        wout_s[...] = wout_ref[...].astype(BF16)

    @pl.when(j == 0)
    def _():
        st_s[...] = jnp.zeros_like(st_s)
        hcar_s[...] = jnp.zeros_like(hcar_s)
        xr_s[0:8, :] = jnp.zeros((8, wb_w), F32)

    x = x_ref[0]
    xn = _rms(x, g1_ref[...]).astype(BF16)
    xb0 = 4 * wa_w

    def project(pieces):
        for c, w in pieces:
            proj_s[:, c:c + w] = jnp.dot(xn, win_s[:, c:c + w], preferred_element_type=F32)

    project(((wa_w, wa_w), (xb0, wb_w), (xb0 + wb_w, wb_w), (0, wa_w), (2 * wa_w, wa_w), (3 * wa_w, wa_w)))

    w1o_ref[...] = w1_ref[...].astype(BF16)
    w3o_ref[...] = w3_ref[...].astype(BF16)

    xr_s[pl.ds(8, lb_t), :] = proj_s[:, xb0:xb0 + wb_w]
    xc = (cb_ref[...] + cw_ref[3:4, :] * xr_s[pl.ds(8, lb_t), :] + cw_ref[2:3, :] * xr_s[pl.ds(7, lb_t), :]
          + cw_ref[1:2, :] * xr_s[pl.ds(6, lb_t), :] + cw_ref[0:1, :] * xr_s[pl.ds(5, lb_t), :])
    tail = xr_s[pl.ds(lb_t + 5, 3), :]
    xr_s[5:8, :] = tail
    cout_ref[0] = tail
    xcb = xc.astype(BF16)
    gate_x = _sigmoid(jnp.dot(xcb, wx_ref[...], preferred_element_type=F32) + bx_ref[...])
    gate_a = _sigmoid(jnp.dot(xcb, wa_ref[...], preferred_element_type=F32) + ba_ref[...])
    log_a =(-LRU_C) * gate_a * _softplus(-lam_ref[...])
    a = jnp.exp(log_a)
    mult = jnp.sqrt((1.0 - a) * (1.0 + a))
    first = (lax.broadcasted_iota(jnp.int32, (lb_t, 1), 0) == 0) & (j == 0)
    a = jnp.where(first, 0.0, a)
    mult = jnp.where(first, 1.0, mult)
    a_s[...] = a
    u_s[...] = gate_x * xc * mult

    lb = _forget_lower_bound(lbw_ref[...])
    f = lb + (1.0 - lb) * _sigmoid(proj_s[:, wa_w:2 * wa_w])
    for h in range(n_heads):
        k_s[h] = 1.0 - f[:, h * HEAD_DIM:(h + 1) * HEAD_DIM]
    logf = jnp.log(f)
    r_i = lax.broadcasted_iota(jnp.int32, (CHUNK, CHUNK), 0)
    c_i = lax.broadcasted_iota(jnp.int32, (CHUNK, CHUNK), 1)
    tri = jnp.where(r_i >= c_i, 1.0, 0.0).astype(BF16)
    lf_hi = logf.astype(BF16)
    rest = logf - lf_hi.astype(F32)
    lf_mid = rest.astype(BF16)
    lf_lo = (rest - lf_mid.astype(F32)).astype(BF16)
    for c in range(0, lb_t, CHUNK):
        cum = [jnp.dot(tri, part[c:c + CHUNK, :], preferred_element_type=F32) for part in (lf_lo, lf_mid, lf_hi)]
        b_all = LOG2E * ((cum[0] + cum[1]) + cum[2])
        for h in range(n_heads):
            b_s[h, c:c + CHUNK, :] = b_all[:, h * HEAD_DIM:(h + 1) * HEAD_DIM]

    row_sub = lax.broadcasted_iota(jnp.int32, (SUB, HEAD_DIM), 0)
    lane_sub = lax.broadcasted_iota(jnp.int32, (SUB, HEAD_DIM), 1)
    assert n_heads % 2 == 0
    r_kk = lax.broadcasted_iota(jnp.int32, (2 * HEAD_DIM, 2 * HEAD_DIM), 0)
    c_kk = lax.broadcasted_iota(jnp.int32, (2 * HEAD_DIM, 2 * HEAD_DIM), 1)
    ones_kk = jnp.where((r_kk < HEAD_DIM) == (c_kk < HEAD_DIM), 1.0, 0.0).astype(BF16)

    n_sub = CHUNK // SUB
    half = SUB // 2
    lower_left = (row_sub >= half) & (lane_sub < half)
    own_lane = jnp.where(row_sub >= half, half, 0)

    def chunk_start(r0):
        first = []
        for h in range(n_heads):
            hs = slice(h * HEAD_DIM, (h + 1) * HEAD_DIM)
            q = proj_s[pl.ds(r0, CHUNK), hs]
            b = b_s[h, pl.ds(r0, CHUNK), :]
            v = proj_s[pl.ds(r0, CHUNK), 2 * wa_w + h * HEAD_DIM:2 * wa_w + (h + 1) * HEAD_DIM]
            k = k_s[h, pl.ds(r0, CHUNK), :]

            def key_rows(ref, lo, j):
                return jnp.stack([jnp.broadcast_to(ref[h, pl.ds(r0 + lo + hf * half + j, 1), :], (half, HEAD_DIM))
                                  for hf in range(2)])
            vb = v.astype(BF16)
            st = st_s[h]
            b_last = b[CHUNK - 1:CHUNK, :]
            o = lax.dot_general((q * jnp.exp2(b)).astype(BF16), st.astype(BF16), NT_DIMS,
                                preferred_element_type=F32)
            k_end = k * jnp.exp2(b_last - b)
            st_s[h] = st * jnp.exp2(b_last) + lax.dot_general(vb, k_end.astype(BF16), TN_DIMS,
                                                               preferred_element_type=F32)
            terms, off, mid = [], [], []
            for i in range(n_sub):
                lo = i * SUB
                qi, bi, ki = q[lo:lo + SUB], b[lo:lo + SUB], k[lo:lo + SUB]
                q3, b3 = (a.reshape(2, half, HEAD_DIM) for a in (qi, bi))
                terms += [(q3 * (key_rows(k_s, lo, j) * jnp.exp2(b3 - key_rows(b_s, lo, j))))
                          .reshape(SUB, HEAD_DIM).astype(BF16) for j in range(half)]
                rm = bi[half - 1:half]
                mid.append(lax.dot_general((qi * jnp.exp2(bi - rm)).astype(BF16),
                                           (ki * jnp.exp2(rm - bi)).astype(BF16), NT_DIMS,
                                           preferred_element_type=F32))
                if i > 0:
                    r = b[lo - 1:lo]
                    qt = (qi * jnp.exp2(bi - r)).astype(BF16)
                    kt = (k[:lo] * jnp.exp2(r - b[:lo])).astype(BF16)
                    off.append(lax.dot_general(qt, kt, NT_DIMS, preferred_element_type=F32))
            first.append([o, vb, off, jnp.concatenate(terms, axis=0), mid])
        for h in range(0, n_heads, 2):
            both = jnp.concatenate([first[h][3], first[h + 1][3]], axis=1)
            sums = jnp.dot(both, ones_kk, preferred_element_type=F32)
            first[h][3] = sums[:, :HEAD_DIM]
            first[h + 1][3] = sums[:, HEAD_DIM:]
        return first

    def chunk_finish(r0, first):
        for h in range(n_heads):
            o, vb, off, sums, mid = first[h]
            outs = []
            for i in range(n_sub):
                lo = i * SUB
                sc = jnp.zeros((SUB, HEAD_DIM), F32)
                for j in range(half):
                    row0 = (i * half + j) * SUB
                    sc = jnp.where(lane_sub == own_lane + j, sums[row0:row0 + SUB], sc)
                sc = jnp.where(row_sub >= lane_sub, sc, 0.0)[:, :SUB]
                sc = jnp.where(lower_left[:, :SUB], mid[i], sc)
                od = o[lo:lo + SUB] + jnp.dot(sc.astype(BF16), vb[lo:lo + SUB], preferred_element_type=F32)
                if i > 0:
                    od = od + jnp.dot(off[i - 1].astype(BF16), vb[:lo], preferred_element_type=F32)
                outs.append(od)
            o_s[pl.ds(r0, CHUNK), h * HEAD_DIM:(h + 1) * HEAD_DIM] = jnp.concatenate(outs, axis=0)

    def chunks_body(ci, carry):
        rows = [pl.multiple_of((ci * UNROLL + u) * CHUNK, CHUNK) for u in range(UNROLL)]
        started = [chunk_start(r0) for r0 in rows]
        for r0, first in zip(rows, started):
            chunk_finish(r0, first)
        return carry

    row8 = lax.broadcasted_iota(jnp.int32, (8, 1), 0)

    def scan_body(gi, carry):
        r0 = pl.multiple_of(gi * 8, 8)
        aa = a_s[pl.ds(r0, 8), :]
        uu = u_s[pl.ds(r0, 8), :]
        for s in (1, 2, 4):
            m = row8 >= s
            uu = jnp.where(m, aa * pltpu.roll(uu, s, 0) + uu, uu)
            aa = jnp.where(m, aa * pltpu.roll(aa, s, 0), aa)
        hh = aa * carry + uu
        u_s[pl.ds(r0, 8), :] = hh
        return hh[7:8, :]

    h_last = lax.fori_loop(0, lb_t // 8, scan_body, hcar_s[...])
    hcar_s[...] = h_last
    hout_ref[0] = h_last
    yb = (u_s[...] * _gelu_tanh(proj_s[:, xb0 + wb_w:xb0 + 2 * wb_w])).astype(BF16)
    yb_out_s[...] = jnp.dot(yb, wout_s[wa_w:, :], preferred_element_type=F32)

    assert lb_t % (CHUNK * UNROLL) == 0
    lax.fori_loop(0, lb_t // (CHUNK * UNROLL), chunks_body, 0)

    ya = []
    for h in range(n_heads):
        hs = slice(h * HEAD_DIM, (h + 1) * HEAD_DIM)
        oh = o_s[:, hs]
        oh = oh * lax.rsqrt(jnp.mean(oh * oh, axis=-1, keepdims=True) + EPS) * hgg_ref[:, hs]
        ya.append(oh * _silu(proj_s[:, 3 * wa_w + h * HEAD_DIM:3 * wa_w + (h + 1) * HEAD_DIM]))

    ya = jnp.concatenate(ya, axis=-1).astype(BF16)
    x1 = x + (jnp.dot(ya, wout_s[:wa_w, :], preferred_element_type=F32) + yb_out_s[...])
    x1_ref[0] = x1
    xn2 = _rms(x1, g2_ref[...]).astype(BF16)
    xn2_ref[0] = xn2
    logits_t = lax.dot_general(wr_ref[...], xn2, NT_DIMS, preferred_element_type=F32) + br_ref[...]
    gates_t, g_idx = _route_rows(logits_t, n_groups, per_group)
    gates_ref[0] = gates_t.T
    lane_c = lax.broadcasted_iota(jnp.int32, (1, ROUTER_LANES), 1)
    for t in range(lb_t // TOK_TILE):
        gi = g_idx[:, t * TOK_TILE:(t + 1) * TOK_TILE]
        cnt = jnp.zeros((1, ROUTER_LANES), jnp.int32)
        for g in range(n_groups):
            cnt = cnt + jnp.where(lane_c == g, jnp.sum(jnp.where(gi == g, 1, 0), axis=1, keepdims=True), 0)
        cnt_ref[0, t:t + 1, :] = cnt

    @pl.when(j == nj - 1)
    def _():
        for h in range(n_heads):
            sout_ref[0, h] = st_s[h].T


def _const_spec(shape):
    nd = len(shape)
    return pl.BlockSpec(shape, lambda *_: (0,) * nd)


def _mixer_prompt(x, p, lb_t):
    bsz, seq, d = x.shape
    wa_w = p["hgg"].shape[1]
    wb_w = p["cb"].shape[1]
    n_heads = wa_w // HEAD_DIM
    n_cols = p["win"].shape[1]
    weights = [p["lbw"], p["g1"], p["win"], p["hgg"], p["cw"], p["cb"], p["wx_bf"], p["bx"],
               p["wa_bf"], p["ba"], p["lam"], p["wout"], p["g2"], p["wr_t_bf"], p["br_t"]]
    nj = seq // lb_t
    tile = lambda w: pl.BlockSpec((1, lb_t, w), lambda b, j: (b, j, 0))
    n_exp = p["w1"].shape[0]
    per_step = next(k for k in range(-(-n_exp // (bsz * nj)), n_exp + 1) if n_exp % k == 0)
    exp_spec = pl.BlockSpec((per_step,) + p["w1"].shape[1:],
                            lambda b, j: (jnp.minimum(b * nj + j, n_exp // per_step - 1), 0, 0))
    out_shape = (
        jax.ShapeDtypeStruct((bsz, seq, d), F32),
        jax.ShapeDtypeStruct((bsz, seq, d), BF16),
        jax.ShapeDtypeStruct((bsz, seq, ROUTER_LANES), F32),
        jax.ShapeDtypeStruct((bsz * (seq // lb_t), lb_t // TOK_TILE, ROUTER_LANES), jnp.int32),
        jax.ShapeDtypeStruct((bsz, n_heads, HEAD_DIM, HEAD_DIM), F32),
        jax.ShapeDtypeStruct((bsz, 1, wb_w), F32),
        jax.ShapeDtypeStruct((bsz, 3, wb_w), F32),
        jax.ShapeDtypeStruct(p["w1"].shape, BF16),
        jax.ShapeDtypeStruct(p["w3"].shape, BF16),
    )
    out_specs = (
        tile(d), tile(d), tile(ROUTER_LANES),
        pl.BlockSpec((1, lb_t // TOK_TILE, ROUTER_LANES), lambda b, j: (b * nj + j, 0, 0)),
        pl.BlockSpec((1, n_heads, HEAD_DIM, HEAD_DIM), lambda b, j: (b, 0, 0, 0)),
        pl.BlockSpec((1, 1, wb_w), lambda b, j: (b, 0, 0)),
        pl.BlockSpec((1, 3, wb_w), lambda b, j: (b, 0, 0)),
        exp_spec, exp_spec,
    )
    scratch = [
        pltpu.VMEM((lb_t, n_cols), F32),
        pltpu.VMEM((n_heads, lb_t, HEAD_DIM), F32),
        pltpu.VMEM((n_heads, lb_t, HEAD_DIM), F32),
        pltpu.VMEM((lb_t, wa_w), F32),
        pltpu.VMEM((n_heads, HEAD_DIM, HEAD_DIM), F32),
        pltpu.VMEM((lb_t + 8, wb_w), F32),
        pltpu.VMEM((lb_t, wb_w), F32),
        pltpu.VMEM((lb_t, wb_w), F32),
        pltpu.VMEM((1, wb_w), F32),
        pltpu.VMEM((lb_t, d), F32),
        pltpu.VMEM((d, n_cols), BF16),
        pltpu.VMEM(p["wout"].shape, BF16),
    ]
    weight_specs = [pl.BlockSpec(w.shape, lambda *_, nd=w.ndim: (0,) * nd, pipeline_mode=pl.Buffered(1))
                    for w in weights]
    kern = functools.partial(_mixer_prompt_kernel, n_groups=p["n_groups"], per_group=p["per_group"])
    return pl.pallas_call(
        kern,
        grid=(bsz, nj),
        in_specs=[tile(d)] + weight_specs + [exp_spec, exp_spec],
        out_specs=out_specs,
        out_shape=out_shape,
        scratch_shapes=scratch,
        compiler_params=pltpu.CompilerParams(dimension_semantics=("arbitrary", "arbitrary"),
                                             vmem_limit_bytes=MIXER_VMEM_LIMIT_BYTES),
        name="mixer_prompt",
    )(x, *weights, p["w1"], p["w3"])


def _seg_pad(n):
    return jnp.bitwise_and(n + (SEG - 1), -SEG)


def _row_tile_pad(n):
    return jnp.bitwise_and(n + (ROW_TILE - 1), -ROW_TILE)


def _tile_segments(cnt_ref, tile, n_groups):
    padded = [_seg_pad(cnt_ref[tile * n_groups + g]) for g in range(n_groups)]
    starts, acc = [], jnp.int32(0)
    for g in range(n_groups):
        starts.append(acc)
        acc = acc + padded[g]
    return padded, starts, acc


def _group_bases(cnt_ref, n_tiles, n_groups):
    def body(t, tot):
        return tuple(tot[g] + _seg_pad(cnt_ref[t * n_groups + g]) for g in range(n_groups))
    totals = lax.fori_loop(0, n_tiles, body, (jnp.int32(0),) * n_groups)
    bases, ends, acc = [], [], jnp.int32(0)
    for g in range(n_groups):
        bases.append(acc)
        acc = acc + _row_tile_pad(totals[g])
        ends.append(acc)
    return bases, ends, totals


def _chunk_rows(n_chunks, starts, gstart):
    rows, row = [], None
    for c in range(n_chunks):
        row = gstart[0] if c == 0 else row + SEG
        for g in range(1, len(starts)):
            row = jnp.where(starts[g] == c * SEG, gstart[g], row)
        rows.append(pl.multiple_of(row, SEG))
    return rows


def _sort_matrix_rows(gates, starts):
    n_groups = len(starts)
    n_rows = TOK_TILE + n_groups * SEG
    col = lax.broadcasted_iota(jnp.int32, (TOK_TILE, ROUTER_LANES), 1)
    onehot_t = jnp.where(col == gates[:, 0:1].astype(jnp.int32), 1.0, 0.0).T
    r_i = lax.broadcasted_iota(jnp.int32, (TOK_TILE, TOK_TILE), 0)
    c_i = lax.broadcasted_iota(jnp.int32, (TOK_TILE, TOK_TILE), 1)
    later = jnp.where(r_i < c_i, 1.0, 0.0).astype(BF16)
    before = jnp.dot(onehot_t.astype(BF16), later, preferred_element_type=F32)
    g_row = lax.broadcasted_iota(jnp.int32, (ROUTER_LANES, 1), 0)
    base = jnp.zeros((ROUTER_LANES, 1), jnp.int32)
    for g in range(n_groups):
        base = base + jnp.where(g_row == g, starts[g], 0)
    dest = jnp.sum(jnp.where(onehot_t > 0.0, before + base.astype(F32), 0.0), axis=0, keepdims=True)
    placed = jnp.sum(onehot_t, axis=0, keepdims=True) > 0.0
    row = lax.broadcasted_iota(jnp.int32, (n_rows, TOK_TILE), 0)
    dest = jnp.where(placed, dest, -1.0)
    return row == dest.astype(jnp.int32), dest


def _dispatch_copies(xbuf, gbuf, xs_hbm, gs_hbm, sem, slot, c, row):
    return (pltpu.make_async_copy(xbuf.at[slot, pl.ds(c * SEG, SEG), :], xs_hbm.at[pl.ds(row, SEG), :], sem.at[slot]),
            pltpu.make_async_copy(gbuf.at[slot, pl.ds(c * SEG, SEG), :], gs_hbm.at[pl.ds(row, SEG), :], sem.at[slot]))


def _moe_dispatch_kernel(cnt_ref, xn2_ref, gates_ref, xn2b_ref, gatesb_ref, xs_hbm, gs_hbm, tg_ref, dest_ref,
                         xbuf, gbuf, sem, gstart_s, zx, zg, zsem, *, n_groups, per_group, n_first):
    j = pl.program_id(0)
    n_tiles = pl.num_programs(0)
    n_chunks = (TOK_TILE + n_groups * SEG) // SEG
    slot = lax.rem(j, 2)

    @pl.when(j == 0)
    def _():
        bases, ends, totals = _group_bases(cnt_ref, n_tiles, n_groups)
        for g in range(n_groups):
            gstart_s[g] = bases[g]
        n_rt = tg_ref.shape[0] - 1
        for i in range(n_rt):
            tg = jnp.int32(0)
            for g in range(n_groups):
                tg = tg + jnp.where(ends[g] <= i * ROW_TILE, 1, 0)
            tg_ref[i] = tg
        n_used = ends[n_groups - 1] // ROW_TILE
        tg_ref[n_rt] = n_used

        zx[...] = jnp.zeros_like(zx)
        zg[...] = jnp.zeros_like(zg)

        def zero_copies(row, n):
            return (pltpu.make_async_copy(zx.at[pl.ds(0, n), :], xs_hbm.at[pl.ds(row, n), :], zsem.at[0]),
                    pltpu.make_async_copy(zg.at[pl.ds(0, n), :], gs_hbm.at[pl.ds(row, n), :], zsem.at[0]))

        def for_each_gap(act):
            for g in range(n_groups):
                tail = bases[g] + totals[g]

                def seg_body(k, c, tail=tail):
                    for cp in zero_copies(pl.multiple_of(tail + k * SEG, SEG), SEG):
                        act(cp)
                    return c
                lax.fori_loop(0, (ends[g] - tail) // SEG, seg_body, 0)

            def tile_body(k, c):
                for cp in zero_copies(pl.multiple_of((n_used + k) * ROW_TILE, ROW_TILE), ROW_TILE):
                    act(cp)
                return c
            lax.fori_loop(0, n_rt - n_used, tile_body, 0)

        for_each_gap(lambda cp: cp.start())
        for_each_gap(lambda cp: cp.wait())

    def wait_tile(tile, slot_):
        _, _, used = _tile_segments(cnt_ref, tile, n_groups)
        for c in range(n_chunks):
            @pl.when(c * SEG < used)
            def _():
                for cp in _dispatch_copies(xbuf, gbuf, xs_hbm, gs_hbm, sem, slot_, c, 0):
                    cp.wait()

    @pl.when(j >= 2)
    def _():
        wait_tile(j - 2, slot)

    padded, starts, used = _tile_segments(cnt_ref, j, n_groups)
    second = j >= n_first
    gates = jnp.where(second, gatesb_ref[...], gates_ref[...])
    xn2 = jnp.where(second, xn2b_ref[...], xn2_ref[...])
    d = xn2_ref.shape[1]
    g_hi = gates.astype(BF16)
    g_lo = (gates - g_hi.astype(F32)).astype(BF16)
    onehot, dest = _sort_matrix_rows(gates, starts)
    dest_ref[0] = jnp.broadcast_to(dest, dest_ref.shape[1:])
    sort_m = jnp.where(onehot, 1.0, 0.0).astype(BF16)
    moved = jnp.dot(sort_m, jnp.concatenate([xn2, g_hi, g_lo], axis=1), preferred_element_type=F32)
    xbuf[slot] = moved[:, :d].astype(BF16)
    gbuf[slot] = moved[:, d:d + ROUTER_LANES] + moved[:, d + ROUTER_LANES:]
    gstart = [gstart_s[g] for g in range(n_groups)]
    rows = _chunk_rows(n_chunks, starts, gstart)
    for c in range(n_chunks):
        @pl.when(c * SEG < used)
        def _():
            for cp in _dispatch_copies(xbuf, gbuf, xs_hbm, gs_hbm, sem, slot, c, rows[c]):
                cp.start()
    for g in range(n_groups):
        gstart_s[g] = gstart[g] + padded[g]

    @pl.when(j == n_tiles - 1)
    def _():
        @pl.when(j >= 1)
        def _():
            wait_tile(j - 1, 1 - slot)
        wait_tile(j, slot)


def _moe_group_kernel(tg_ref, xs_ref, gs_ref, w1_ref, w3_ref, w2_ref, ys_ref, w2_s, *, n_groups, per_group):
    i = pl.program_id(0)

    @pl.when((i == 0) | (tg_ref[i] != tg_ref[jnp.maximum(i - 1, 0)]))
    def _():
        w2_s[...] = w2_ref[...].astype(BF16)

    @pl.when(tg_ref[i] < n_groups)
    def _():
        xb = xs_ref[...]
        gates = pltpu.roll(gs_ref[...], ROUTER_LANES - PROMPT_EXPERT_LANE - per_group * tg_ref[i], 1)
        acc = jnp.zeros(ys_ref.shape, F32)
        for s in range(per_group):
            h = (_silu(jnp.dot(xb, w1_ref[s], preferred_element_type=F32))
                 * jnp.dot(xb, w3_ref[s], preferred_element_type=F32) * gates[:, s:s + 1])
            acc = acc + jnp.dot(h.astype(BF16), w2_s[s], preferred_element_type=F32)
        ys_ref[...] = acc.astype(BF16)


def _moe_combine_kernel(cnt_ref, dest_ref, x1_ref, x1b_ref, gf_ref, ys_hbm, y_ref, yb_ref,
                        ybuf, sem, gstart_s, *, n_groups, n_first):
    j = pl.program_id(0)
    n_tiles = pl.num_programs(0)
    n_chunks = (TOK_TILE + n_groups * SEG) // SEG
    slot = lax.rem(j, 2)

    def copies(slot_, c, row):
        return pltpu.make_async_copy(ys_hbm.at[pl.ds(row, SEG), :], ybuf.at[slot_, pl.ds(c * SEG, SEG), :],
                                     sem.at[slot_])

    def fetch_tile(tile, slot_):
        padded, starts, _ = _tile_segments(cnt_ref, tile, n_groups)
        gstart = [gstart_s[g] for g in range(n_groups)]
        for c, row in enumerate(_chunk_rows(n_chunks, starts, gstart)):
            copies(slot_, c, row).start()
        for g in range(n_groups):
            gstart_s[g] = gstart[g] + padded[g]

    @pl.when(j == 0)
    def _():
        bases, _, _ = _group_bases(cnt_ref, n_tiles, n_groups)
        for g in range(n_groups):
            gstart_s[g] = bases[g]
        if n_chunks * SEG < SORT_ROWS:
            ybuf[:, n_chunks * SEG:, :] = jnp.zeros((2, SORT_ROWS - n_chunks * SEG, ybuf.shape[2]), BF16)
        fetch_tile(0, 0)

    @pl.when(j + 1 < n_tiles)
    def _():
        fetch_tile(j + 1, 1 - slot)

    for c in range(n_chunks):
        copies(slot, c, 0).wait()
    second = j >= n_first
    target = dest_ref[0].T[:, 0:1].astype(jnp.int32)
    sort_t = lax.broadcasted_iota(jnp.int32, (TOK_TILE, SORT_ROWS), 1) == target
    moe = jnp.dot(jnp.where(sort_t, 1.0, 0.0).astype(BF16), ybuf[slot], preferred_element_type=F32)

    @pl.when(jnp.logical_not(second))
    def _():
        y_ref[...] = _rms(x1_ref[...] + moe, gf_ref[...])

    @pl.when(second)
    def _():
        yb_ref[...] = _rms(x1b_ref[...] + moe, gf_ref[...])


def _moe_sorted(first, second, w1_bf, w3_bf, p):
    x1, xn2, gates, counts = first
    x1b, xn2b, gatesb, countsb = second
    t, d = x1.shape
    tb = x1b.shape[0]
    n_groups, per_group = p["n_groups"], p["per_group"]
    assert n_groups <= MAX_GROUPS and t % TOK_TILE == 0 and tb % TOK_TILE == 0
    n_first = t // TOK_TILE
    n_tiles = n_first + tb // TOK_TILE
    d_exp = w1_bf.shape[2]
    cap = t + tb + n_tiles * n_groups * SEG + n_groups * ROW_TILE + SORT_ROWS
    n_rt = -(-cap // ROW_TILE)
    rows = n_rt * ROW_TILE
    cnt = jnp.concatenate([counts[..., :n_groups].reshape(-1), countsb[..., :n_groups].reshape(-1)])
    params = pltpu.CompilerParams(dimension_semantics=("arbitrary",), vmem_limit_bytes=VMEM_LIMIT_BYTES)
    any_spec = pl.BlockSpec(memory_space=pl.ANY)
    tok = lambda w: pl.BlockSpec((TOK_TILE, w), lambda j, c: (jnp.minimum(j, n_first - 1), 0))
    tokb = lambda w: pl.BlockSpec((TOK_TILE, w), lambda j, c: (jnp.maximum(j - n_first, 0), 0))

    dest_spec = pl.BlockSpec((1, 8, TOK_TILE), lambda j, c: (j, 0, 0))

    xs, gs, tile_group, dest = pl.pallas_call(
        functools.partial(_moe_dispatch_kernel, n_groups=n_groups, per_group=per_group, n_first=n_first),
        grid_spec=pltpu.PrefetchScalarGridSpec(
            num_scalar_prefetch=1, grid=(n_tiles,),
            in_specs=[tok(d), tok(ROUTER_LANES), tokb(d), tokb(ROUTER_LANES)],
            out_specs=(any_spec, any_spec, pl.BlockSpec(memory_space=pltpu.SMEM), dest_spec),
            scratch_shapes=[pltpu.VMEM((2, TOK_TILE + n_groups * SEG, d), BF16),
                            pltpu.VMEM((2, TOK_TILE + n_groups * SEG, ROUTER_LANES), F32),
                            pltpu.SemaphoreType.DMA((2,)), pltpu.SMEM((n_groups,), jnp.int32),
                            pltpu.VMEM((ROW_TILE, d), BF16), pltpu.VMEM((ROW_TILE, ROUTER_LANES), F32),
                            pltpu.SemaphoreType.DMA((1,))]),
        out_shape=(jax.ShapeDtypeStruct((rows, d), BF16), jax.ShapeDtypeStruct((rows, ROUTER_LANES), F32),
                   jax.ShapeDtypeStruct((n_rt + 1,), jnp.int32),
                   jax.ShapeDtypeStruct((n_tiles, 8, TOK_TILE), F32)),
        compiler_params=params, name="moe_dispatch",
    )(cnt, xn2, gates, xn2b, gatesb)

    used_tile = lambda i, tg: (jnp.minimum(i, tg[n_rt] - 1), 0)
    group_w = lambda i, tg: (jnp.minimum(tg[i], n_groups - 1), 0, 0)
    ys = pl.pallas_call(
        functools.partial(_moe_group_kernel, n_groups=n_groups, per_group=per_group),
        grid_spec=pltpu.PrefetchScalarGridSpec(
            num_scalar_prefetch=1, grid=(n_rt,),
            in_specs=[pl.BlockSpec((ROW_TILE, d), used_tile), pl.BlockSpec((ROW_TILE, ROUTER_LANES), used_tile),
                      pl.BlockSpec((per_group, d, d_exp), group_w), pl.BlockSpec((per_group, d, d_exp), group_w),
                      pl.BlockSpec((per_group, d_exp, d), group_w)],
            out_specs=pl.BlockSpec((ROW_TILE, d), used_tile),
            scratch_shapes=[pltpu.VMEM((per_group, d_exp, d), BF16)]),
        out_shape=jax.ShapeDtypeStruct((rows, d), BF16),
        input_output_aliases={1: 0},
        compiler_params=params, name="moe_experts",
    )(tile_group, xs, gs, w1_bf, w3_bf, p["w2"])

    return pl.pallas_call(
        functools.partial(_moe_combine_kernel, n_groups=n_groups, n_first=n_first),
        grid_spec=pltpu.PrefetchScalarGridSpec(
            num_scalar_prefetch=1, grid=(n_tiles,),
            in_specs=[dest_spec, tok(d), tokb(d), pl.BlockSpec(p["gf"].shape, lambda j, c: (0, 0)), any_spec],
            out_specs=(tok(d), tokb(d)),
            scratch_shapes=[pltpu.VMEM((2, SORT_ROWS, d), BF16), pltpu.SemaphoreType.DMA((2,)),
                            pltpu.SMEM((n_groups,), jnp.int32)]),
        out_shape=(jax.ShapeDtypeStruct((t, d), F32), jax.ShapeDtypeStruct((tb, d), F32)),
        compiler_params=params, name="moe_combine",
    )(cnt, dest, x1, x1b, p["gf"], ys)


def _sample_in_kernel(x_ref, c0_ref, c1_ref, c2_ref, h0_ref, lbw_ref, g1_ref, win_ref, cw_ref, cb_ref,
                      wx_ref, bx_ref, wa_ref, ba_ref, lam_ref,
                      q_ref, f_ref, v_ref, g_ref, yb_ref, hnew_ref, xr_ref):
    wa_w = v_ref.shape[1]
    wb_w = yb_ref.shape[1]
    xn = _rms(x_ref[...], g1_ref[...])
    proj = jnp.dot(xn, win_ref[...], precision=HIGHEST, preferred_element_type=F32)
    lb = _forget_lower_bound(lbw_ref[...])
    f = lb + (1.0 - lb) * _sigmoid(proj[:, wa_w:2 * wa_w])
    q_ref[...] = proj[:, 0:wa_w].T
    f_ref[...] = f.T
    v_ref[...] = proj[:, 2 * wa_w:3 * wa_w]
    g_ref[...] = proj[:, 3 * wa_w:4 * wa_w]
    xr = proj[:, 4 * wa_w:4 * wa_w + wb_w]
    xr_ref[...] = xr
    xc = (cb_ref[...] + cw_ref[0:1, :] * c0_ref[...] + cw_ref[1:2, :] * c1_ref[...]
          + cw_ref[2:3, :] * c2_ref[...] + cw_ref[3:4, :] * xr)
    gate_x = _sigmoid(jnp.dot(xc, wx_ref[...], precision=HIGHEST, preferred_element_type=F32) + bx_ref[...])
    gate_a = _sigmoid(jnp.dot(xc, wa_ref[...], precision=HIGHEST, preferred_element_type=F32) + ba_ref[...])
    log_a = (-LRU_C) * gate_a * _softplus(-lam_ref[...])
    a = jnp.exp(log_a)
    mult = jnp.sqrt(-_expm1(2.0 * log_a))
    h = a * h0_ref[...] + gate_x * xc * mult
    hnew_ref[...] = h
    yb_ref[...] = h * _gelu_tanh(proj[:, 4 * wa_w + wb_w:4 * wa_w + 2 * wb_w])


def _sample_state_kernel(s_ref, qt_ref, ft_ref, v_ref, snew_ref, o_ref):
    tb, n_heads = s_ref.shape[0], s_ref.shape[1]
    n_tok = qt_ref.shape[1]
    shift = lax.rem(n_tok - pl.program_id(0) * tb, n_tok)
    assert 3 * tb <= HEAD_DIM
    lane = lax.broadcasted_iota(jnp.int32, (HEAD_DIM, n_tok), 1)
    r_e = lax.broadcasted_iota(jnp.int32, (n_tok, tb * HEAD_DIM), 0)
    c_e = lax.broadcasted_iota(jnp.int32, (n_tok, tb * HEAD_DIM), 1)
    spread = jnp.where((r_e < 3 * tb) & (lax.rem(r_e, tb) == c_e // HEAD_DIM), 1.0, 0.0).astype(BF16)

    def pieces_of(cols):
        hi = cols.astype(BF16).astype(F32)
        rest = cols - hi
        mid = rest.astype(BF16).astype(F32)
        low = rest - mid
        pieces = jnp.where(lane < tb, hi, jnp.where(lane < 2 * tb, pltpu.roll(mid, tb, 1),
                                                    jnp.where(lane < 3 * tb, pltpu.roll(low, 2 * tb, 1), 0.0)))
        return pieces.astype(BF16)

    for h in range(n_heads):
        hs = slice(h * HEAD_DIM, (h + 1) * HEAD_DIM)
        q_p = pieces_of(pltpu.roll(qt_ref[hs, :], shift, 1))
        f_p = pieces_of(pltpu.roll(ft_ref[hs, :], shift, 1))
        rows = []
        for t in range(tb):
            one = spread[:, t * HEAD_DIM:(t + 1) * HEAD_DIM]
            f_t = jnp.dot(f_p, one, preferred_element_type=F32)
            q_t = jnp.dot(q_p, one, preferred_element_type=F32)
            s_new = f_t * s_ref[t, h] + (1.0 - f_t) * v_ref[t:t + 1, hs]
            snew_ref[t, h] = s_new
            rows.append(jnp.sum(q_t * s_new, axis=0, keepdims=True))
        o_ref[:, hs] = jnp.concatenate(rows, axis=0)


def _sample_out_kernel(x_ref, o_ref, g_ref, yb_ref, hgg_ref, wout_ref, g2_ref, wr_ref, br_ref,
                       x1_ref, xn2_ref, gates_ref, cnt_ref, *, n_groups, per_group):
    wa_w = o_ref.shape[1]
    ya = []
    for h in range(wa_w // HEAD_DIM):
        hs = slice(h * HEAD_DIM, (h + 1) * HEAD_DIM)
        oh = o_ref[:, hs]
        oh = oh * lax.rsqrt(jnp.mean(oh * oh, axis=-1, keepdims=True) + EPS) * hgg_ref[:, hs]
        ya.append(oh * _silu(g_ref[:, hs]))
    y = jnp.concatenate(ya + [yb_ref[...]], axis=-1)
    x1 = x_ref[...] + jnp.dot(y, wout_ref[...], precision=HIGHEST, preferred_element_type=F32)
    xn2 = _rms(x1, g2_ref[...])
    logits_t = lax.dot_general(wr_ref[...], xn2, NT_DIMS, precision=HIGHEST,
                               preferred_element_type=F32) + br_ref[...]
    gates_t, g_idx = _route_rows(logits_t, n_groups, per_group)
    n = x1.shape[0]
    pad = x1_ref.shape[0] - n
    x1_ref[0:n, :] = x1
    xn2_ref[0:n, :] = xn2.astype(BF16)
    gates_ref[0:n, :] = gates_t.T
    if pad:
        x1_ref[n:, :] = jnp.zeros((pad, x1.shape[1]), F32)
        xn2_ref[n:, :] = jnp.zeros((pad, x1.shape[1]), BF16)
        gates_ref[n:, :] = jnp.where(lax.broadcasted_iota(jnp.int32, (pad, ROUTER_LANES), 1) == 0, -1.0, 0.0)
    lane_c = lax.broadcasted_iota(jnp.int32, (1, ROUTER_LANES), 1)
    tok = lax.broadcasted_iota(jnp.int32, g_idx.shape, 1)
    n_tiles = cnt_ref.shape[0]
    for t in range(n_tiles):
        in_tile = (tok >= t * TOK_TILE) & (tok < (t + 1) * TOK_TILE)
        cnt = jnp.zeros((1, ROUTER_LANES), jnp.int32)
        for g in range(n_groups):
            here = jnp.sum(jnp.where(in_tile & (g_idx == g), 1, 0), axis=1, keepdims=True)
            cnt = cnt + jnp.where(lane_c == g, here, 0)
        cnt_ref[t:t + 1, :] = cnt


def _whole(kernel, out_shape, *args, name):
    return pl.pallas_call(
        kernel, out_shape=out_shape,
        compiler_params=pltpu.CompilerParams(vmem_limit_bytes=VMEM_LIMIT_BYTES), name=name)(*args)


def _mixer_sample(x, s0, h0, c0, p, tb):
    n, d = x.shape
    wa_w = p["hgg"].shape[1]
    wb_w = p["cb"].shape[1]
    n_heads = wa_w // HEAD_DIM
    sd = lambda w: jax.ShapeDtypeStruct((n, w), F32)
    key_major = jax.ShapeDtypeStruct((wa_w, n), F32)
    q, f, v, g, yb, h_new, xr = _whole(
        _sample_in_kernel, (key_major,) * 2 + (sd(wa_w),) * 2 + (sd(wb_w),) * 3,
        x, c0[:, 0, :], c0[:, 1, :], c0[:, 2, :], h0, p["lbw"], p["g1"], p["win"], p["cw"], p["cb"],
        p["wx"], p["bx"], p["wa"], p["ba"], p["lam"], name="sample_in")
    tok = lambda w: pl.BlockSpec((tb, w), lambda i: (i, 0))
    st = pl.BlockSpec((tb, n_heads, HEAD_DIM, HEAD_DIM), lambda i: (i, 0, 0, 0))
    s_new, o = pl.pallas_call(
        _sample_state_kernel,
        grid=(n // tb,),
        in_specs=[st, _const_spec((wa_w, n)), _const_spec((wa_w, n)), tok(wa_w)],
        out_specs=(st, tok(wa_w)),
        out_shape=(jax.ShapeDtypeStruct(s0.shape, F32), sd(wa_w)),
        compiler_params=pltpu.CompilerParams(dimension_semantics=("arbitrary",),
                                             vmem_limit_bytes=VMEM_LIMIT_BYTES),
        name="sample_state",
    )(s0, q, f, v)
    n_pad = -(-n // TOK_TILE) * TOK_TILE
    x1, xn2, gates, counts = _whole(
        functools.partial(_sample_out_kernel, n_groups=p["n_groups"], per_group=p["per_group"]),
        (jax.ShapeDtypeStruct((n_pad, d), F32), jax.ShapeDtypeStruct((n_pad, d), BF16),
         jax.ShapeDtypeStruct((n_pad, ROUTER_LANES), F32),
         jax.ShapeDtypeStruct((n_pad // TOK_TILE, ROUTER_LANES), jnp.int32)),
        x, o, g, yb, p["hgg"], p["wout"], p["g2"], p["wr_t"], p["br_t"], name="sample_out")
    c_new = jnp.stack([c0[:, 1, :], c0[:, 2, :], xr], axis=1)
    return x1, xn2, gates, counts, s_new, h_new, c_new


def _block_diag(w):
    n, c, _ = w.shape
    eye = jnp.eye(n, dtype=w.dtype)
    return (w[:, :, None, :] * eye[:, None, :, None]).reshape(n * c, n * c)


def _prepare(lower_bounds, ln1_g, w_in, hgrn_norm_g, conv_w, conv_b, lru_wx, lru_bx, lru_wa, lru_ba,
             lru_lambda, w_out, ln2_g, router_wg, router_bg, router_we, router_be, exp_w1, exp_w3,
             exp_w2, final_g):
    d = w_in.shape[1]
    n_groups = router_wg.shape[-1]
    per_group = router_we.shape[-1]
    row = lambda a: a.reshape(1, -1).astype(F32)
    we = jnp.transpose(router_we[0], (1, 0, 2)).reshape(d, n_groups * per_group)
    rows_t = -(-(8 + n_groups * per_group) // 16) * 16
    wr_t = jnp.concatenate([router_wg[0].T, jnp.zeros((8 - n_groups, d), F32), we.T,
                            jnp.zeros((rows_t - 8 - n_groups * per_group, d), F32)], axis=0)
    br_t = jnp.concatenate([router_bg[0], jnp.zeros((8 - n_groups,), F32), router_be[0].reshape(-1),
                            jnp.zeros((rows_t - 8 - n_groups * per_group,), F32)]).reshape(-1, 1)
    wx = _block_diag(lru_wx[0])
    wa = _block_diag(lru_wa[0])
    return dict(
        wr_t=wr_t, wr_t_bf=wr_t.astype(BF16), br_t=br_t,
        n_groups=n_groups, per_group=per_group,
        lbw=lower_bounds.astype(F32), g1=row(ln1_g[0]), win=w_in[0],
        hgg=row(hgrn_norm_g[0]), cw=conv_w[0], cb=row(conv_b[0]),
        wx=wx, wx_bf=wx.astype(BF16), bx=row(lru_bx[0]), wa=wa, wa_bf=wa.astype(BF16), ba=row(lru_ba[0]),
        lam=row(lru_lambda[0]), wout=w_out[0], g2=row(ln2_g[0]),
        w1=exp_w1[0], w3=exp_w3[0],
        w2=exp_w2[0], gf=row(final_g))


def kernel(x_prompt, x_sample, state_hgrn, state_rglru, state_conv, lower_bounds, ln1_g, w_in, hgrn_norm_g, conv_w, conv_b, lru_wx, lru_bx, lru_wa, lru_ba, lru_lambda, w_out, ln2_g, router_wg, router_bg, router_we, router_be, exp_w1, exp_w3, exp_w2, final_g):
    assert w_in.shape[0] == 1, "single-layer trunk"
    p = _prepare(lower_bounds, ln1_g, w_in, hgrn_norm_g, conv_w, conv_b, lru_wx, lru_bx, lru_wa, lru_ba,
                 lru_lambda, w_out, ln2_g, router_wg, router_bg, router_we, router_be, exp_w1, exp_w3,
                 exp_w2, final_g)
    bsz, seq, d = x_prompt.shape
    x1, xn2, gates, counts, s_p, h_p, c_p, w1_bf, w3_bf = _mixer_prompt(x_prompt, p, min(MIXER_BLOCK, seq))
    t = bsz * seq
    n = x_sample.shape[0]
    x1s, xn2s, gates_s, counts_s, s_s, h_s, c_s = _mixer_sample(
        x_sample[:, 0, :], state_hgrn[0], state_rglru[0], state_conv[0], p, SAMPLE_STEP_TOKENS)
    y_p, y_s = _moe_sorted((x1.reshape(t, d), xn2.reshape(t, d), gates.reshape(t, ROUTER_LANES), counts),
                           (x1s, xn2s, gates_s, counts_s), w1_bf, w3_bf, p)
    return (y_p.reshape(bsz, seq, d), y_s[:n].reshape(n, 1, d),
            s_p[None], h_p.reshape(1, bsz, -1), c_p[None],
            s_s[None], h_s[None], c_s[None])
```

```python
import functools

import jax
import jax.numpy as jnp
from jax import lax
from jax.experimental import pallas as pl
from jax.experimental.pallas import tpu as pltpu

F32 = jnp.float32
BF16 = jnp.bfloat16
HIGHEST = lax.Precision.HIGHEST

EPS = 1e-6
LRU_C = 8.0
LOG2E = 1.4426950408889634
HEAD_DIM = 128
CHUNK = 64
SUB = 16
UNROLL = 8
ROUTER_LANES = 128
PROMPT_EXPERT_LANE = 8
MIXER_BLOCK = 512
SAMPLE_STEP_TOKENS = 16
TOK_TILE = 512
SEG = 16
MAX_GROUPS = 8
SORT_ROWS = TOK_TILE + MAX_GROUPS * SEG
ROW_TILE = 512
VMEM_LIMIT_BYTES = 56 * 1024 * 1024
MIXER_VMEM_LIMIT_BYTES = 60 * 1024 * 1024

NT_DIMS = (((1,), (1,)), ((), ()))
TN_DIMS = (((0,), (0,)), ((), ()))


def _rms(x, g):
    return x * lax.rsqrt(jnp.mean(x * x, axis=-1, keepdims=True) + EPS) * g


def _sigmoid(x):
    return 1.0 / (1.0 + jnp.exp(-x))


def _silu(x):
    return x * _sigmoid(x)


def _gelu_tanh(x):
    c = 0.7978845608028654
    return x * (0.5 * (1.0 + jnp.tanh(c * (x + 0.044715 * (x * x * x)))))


def _softplus(z):
    return jnp.maximum(z, 0.0) + jnp.log1p(jnp.exp(-jnp.abs(z)))


def _expm1(x):
    u = jnp.exp(x)
    um1 = u - 1.0
    small = um1 * x / jnp.log(u)
    return jnp.where(um1 == 0.0, x, jnp.where(jnp.abs(x) < 0.5, small, um1))


def _forget_lower_bound(lbw):
    m = jnp.max(lbw, axis=0, keepdims=True)
    e = jnp.exp(lbw - m)
    return e[0:1, :] / jnp.sum(e, axis=0, keepdims=True)


def _route_rows(logits_t, n_groups, per_group):
    assert n_groups <= 8 and per_group == 8
    n_tok = logits_t.shape[1]
    row = lax.broadcasted_iota(jnp.int32, (8, n_tok), 0)
    neg = jnp.float32(-jnp.inf)
    big = jnp.int32(8)
    lg = jnp.where(row < n_groups, logits_t[0:8], neg)
    mg = jnp.max(lg, axis=0, keepdims=True)
    g_idx = jnp.min(jnp.where(lg == mg, row, big), axis=0, keepdims=True)
    p_top = 1.0 / jnp.sum(jnp.where(row < n_groups, jnp.exp(lg - mg), 0.0), axis=0, keepdims=True)
    sel = logits_t[8:16]
    for g in range(1, n_groups):
        sel = jnp.where(g_idx == g, logits_t[8 + 8 * g:16 + 8 * g], sel)
    m1 = jnp.max(sel, axis=0, keepdims=True)
    i1 = jnp.min(jnp.where(sel == m1, row, big), axis=0, keepdims=True)
    sel2 = jnp.where(row == i1, neg, sel)
    m2 = jnp.max(sel2, axis=0, keepdims=True)
    i2 = jnp.min(jnp.where(sel2 == m2, row, big), axis=0, keepdims=True)
    e2 = jnp.exp(m2 - m1)
    den = 1.0 + e2
    w1 = p_top / den
    w2 = p_top * (e2 / den)
    own = jnp.where(row == i1, w1, 0.0) + jnp.where(row == i2, w2, 0.0)
    blocks = [jnp.where(row == 0, g_idx.astype(F32), 0.0)]
    blocks += [jnp.where(g_idx == g, own, 0.0) for g in range(n_groups)]
    blocks += [jnp.zeros((8, n_tok), F32)] * (ROUTER_LANES // 8 - len(blocks))
    return jnp.concatenate(blocks, axis=0), g_idx


def _mixer_prompt_kernel(x_ref, lbw_ref, g1_ref, win_ref, hgg_ref, cw_ref, cb_ref, wx_ref, bx_ref,
                         wa_ref, ba_ref, lam_ref, wout_ref, g2_ref, wr_ref, br_ref, w1_ref, w3_ref,
                         x1_ref, xn2_ref, gates_ref, cnt_ref, sout_ref, hout_ref, cout_ref, w1o_ref, w3o_ref,
                         proj_s, k_s, b_s, o_s, st_s, xr_s, a_s, u_s, hcar_s, yb_out_s, win_s, wout_s,
                         *, n_groups, per_group):
    lb_t = x_ref.shape[1]
    wa_w = o_s.shape[1]
    wb_w = a_s.shape[1]
    n_heads = wa_w // HEAD_DIM
    j = pl.program_id(1)
    nj = pl.num_programs(1)

    @pl.when((pl.program_id(0) == 0) & (j == 0))
    def _():
        for c in range(0, win_ref.shape[1], wa_w):
            win_s[:, c:c + wa_w] = win_ref[:, c:c + wa_w].astype(BF16)
        wout_s[...] = wout_ref[...].astype(BF16)

    @pl.when(j == 0)
    def _():
        st_s[...] = jnp.zeros_like(st_s)
        hcar_s[...] = jnp.zeros_like(hcar_s)
        xr_s[0:8, :] = jnp.zeros((8, wb_w), F32)

    x = x_ref[0]
    xn = _rms(x, g1_ref[...]).astype(BF16)
    xb0 = 4 * wa_w

    def project(pieces):
        for c, w in pieces:
            proj_s[:, c:c + w] = jnp.dot(xn, win_s[:, c:c + w], preferred_element_type=F32)

    project(((wa_w, wa_w), (xb0, wb_w), (xb0 + wb_w, wb_w), (0, wa_w), (2 * wa_w, wa_w), (3 * wa_w, wa_w)))

    w1o_ref[...] = w1_ref[...].astype(BF16)
    w3o_ref[...] = w3_ref[...].astype(BF16)

    xr_s[pl.ds(8, lb_t), :] = proj_s[:, xb0:xb0 + wb_w]
    xc = (cb_ref[...] + cw_ref[3:4, :] * xr_s[pl.ds(8, lb_t), :] + cw_ref[2:3, :] * xr_s[pl.ds(7, lb_t), :]
          + cw_ref[1:2, :] * xr_s[pl.ds(6, lb_t), :] + cw_ref[0:1, :] * xr_s[pl.ds(5, lb_t), :])
    tail = xr_s[pl.ds(lb_t + 5, 3), :]
    xr_s[5:8, :] = tail
    cout_ref[0] = tail
    xcb = xc.astype(BF16)
    gate_x = _sigmoid(jnp.dot(xcb, wx_ref[...], preferred_element_type=F32) + bx_ref[...])
    gate_a = _sigmoid(jnp.dot(xcb, wa_ref[...], preferred_element_type=F32) + ba_ref[...])
    log_a =(-LRU_C) * gate_a * _softplus(-lam_ref[...])
    a = jnp.exp(log_a)
    mult = jnp.sqrt((1.0 - a) * (1.0 + a))
    first = (lax.broadcasted_iota(jnp.int32, (lb_t, 1), 0) == 0) & (j == 0)
    a = jnp.where(first, 0.0, a)
    mult = jnp.where(first, 1.0, mult)
    a_s[...] = a
    u_s[...] = gate_x * xc * mult

    lb = _forget_lower_bound(lbw_ref[...])
    f = lb + (1.0 - lb) * _sigmoid(proj_s[:, wa_w:2 * wa_w])
    for h in range(n_heads):
        k_s[h] = 1.0 - f[:, h * HEAD_DIM:(h + 1) * HEAD_DIM]
    logf = jnp.log(f)
    r_i = lax.broadcasted_iota(jnp.int32, (CHUNK, CHUNK), 0)
    c_i = lax.broadcasted_iota(jnp.int32, (CHUNK, CHUNK), 1)
    tri = jnp.where(r_i >= c_i, 1.0, 0.0).astype(BF16)
    lf_hi = logf.astype(BF16)
    rest = logf - lf_hi.astype(F32)
    lf_mid = rest.astype(BF16)
    lf_lo = (rest - lf_mid.astype(F32)).astype(BF16)
    for c in range(0, lb_t, CHUNK):
        cum = [jnp.dot(tri, part[c:c + CHUNK, :], preferred_element_type=F32) for part in (lf_lo, lf_mid, lf_hi)]
        b_all = LOG2E * ((cum[0] + cum[1]) + cum[2])
        for h in range(n_heads):
            b_s[h, c:c + CHUNK, :] = b_all[:, h * HEAD_DIM:(h + 1) * HEAD_DIM]

    row_sub = lax.broadcasted_iota(jnp.int32, (SUB, HEAD_DIM), 0)
    lane_sub = lax.broadcasted_iota(jnp.int32, (SUB, HEAD_DIM), 1)
    assert n_heads % 2 == 0
    r_kk = lax.broadcasted_iota(jnp.int32, (2 * HEAD_DIM, 2 * HEAD_DIM), 0)
    c_kk = lax.broadcasted_iota(jnp.int32, (2 * HEAD_DIM, 2 * HEAD_DIM), 1)
    ones_kk = jnp.where((r_kk < HEAD_DIM) == (c_kk < HEAD_DIM), 1.0, 0.0).astype(BF16)

    n_sub = CHUNK // SUB
    half = SUB // 2
    lower_left = (row_sub >= half) & (lane_sub < half)
    own_lane = jnp.where(row_sub >= half, half, 0)

    def chunk_start(r0):
        first = []
        for h in range(n_heads):
            hs = slice(h * HEAD_DIM, (h + 1) * HEAD_DIM)
            q = proj_s[pl.ds(r0, CHUNK), hs]
            b = b_s[h, pl.ds(r0, CHUNK), :]
            v = proj_s[pl.ds(r0, CHUNK), 2 * wa_w + h * HEAD_DIM:2 * wa_w + (h + 1) * HEAD_DIM]
            k = k_s[h, pl.ds(r0, CHUNK), :]

            def key_rows(ref, lo, j):
                return jnp.stack([jnp.broadcast_to(ref[h, pl.ds(r0 + lo + hf * half + j, 1), :], (half, HEAD_DIM))
                                  for hf in range(2)])
            vb = v.astype(BF16)
            st = st_s[h]
            b_last = b[CHUNK - 1:CHUNK, :]
            o = lax.dot_general((q * jnp.exp2(b)).astype(BF16), st.astype(BF16), NT_DIMS,
                                preferred_element_type=F32)
            k_end = k * jnp.exp2(b_last - b)
            st_s[h] = st * jnp.exp2(b_last) + lax.dot_general(vb, k_end.astype(BF16), TN_DIMS,
                                                               preferred_element_type=F32)
            terms, off, mid = [], [], []
            for i in range(n_sub):
                lo = i * SUB
                qi, bi, ki = q[lo:lo + SUB], b[lo:lo + SUB], k[lo:lo + SUB]
                q3, b3 = (a.reshape(2, half, HEAD_DIM) for a in (qi, bi))
                terms += [(q3 * (key_rows(k_s, lo, j) * jnp.exp2(b3 - key_rows(b_s, lo, j))))
                          .reshape(SUB, HEAD_DIM).astype(BF16) for j in range(half)]
                rm = bi[half - 1:half]
                mid.append(lax.dot_general((qi * jnp.exp2(bi - rm)).astype(BF16),
                                           (ki * jnp.exp2(rm - bi)).astype(BF16), NT_DIMS,
                                           preferred_element_type=F32))
                if i > 0:
                    r = b[lo - 1:lo]
                    qt = (qi * jnp.exp2(bi - r)).astype(BF16)
                    kt = (k[:lo] * jnp.exp2(r - b[:lo])).astype(BF16)
                    off.append(lax.dot_general(qt, kt, NT_DIMS, preferred_element_type=F32))
            first.append([o, vb, off, jnp.concatenate(terms, axis=0), mid])
        for h in range(0, n_heads, 2):
            both = jnp.concatenate([first[h][3], first[h + 1][3]], axis=1)
            sums = jnp.dot(both, ones_kk, preferred_element_type=F32)
            first[h][3] = sums[:, :HEAD_DIM]
            first[h + 1][3] = sums[:, HEAD_DIM:]
        return first

    def chunk_finish(r0, first):
        for h in range(n_heads):
            o, vb, off, sums, mid = first[h]
            outs = []
            for i in range(n_sub):
                lo = i * SUB
                sc = jnp.zeros((SUB, HEAD_DIM), F32)
                for j in range(half):
                    row0 = (i * half + j) * SUB
                    sc = jnp.where(lane_sub == own_lane + j, sums[row0:row0 + SUB], sc)
                sc = jnp.where(row_sub >= lane_sub, sc, 0.0)[:, :SUB]
                sc = jnp.where(lower_left[:, :SUB], mid[i], sc)
                od = o[lo:lo + SUB] + jnp.dot(sc.astype(BF16), vb[lo:lo + SUB], preferred_element_type=F32)
                if i > 0:
                    od = od + jnp.dot(off[i - 1].astype(BF16), vb[:lo], preferred_element_type=F32)
                outs.append(od)
            o_s[pl.ds(r0, CHUNK), h * HEAD_DIM:(h + 1) * HEAD_DIM] = jnp.concatenate(outs, axis=0)

    def chunks_body(ci, carry):
        rows = [pl.multiple_of((ci * UNROLL + u) * CHUNK, CHUNK) for u in range(UNROLL)]
        started = [chunk_start(r0) for r0 in rows]
        for r0, first in zip(rows, started):
            chunk_finish(r0, first)
        return carry

    row8 = lax.broadcasted_iota(jnp.int32, (8, 1), 0)

    def scan_body(gi, carry):
        r0 = pl.multiple_of(gi * 8, 8)
        aa = a_s[pl.ds(r0, 8), :]
        uu = u_s[pl.ds(r0, 8), :]
        for s in (1, 2, 4):
            m = row8 >= s
            uu = jnp.where(m, aa * pltpu.roll(uu, s, 0) + uu, uu)
            aa = jnp.where(m, aa * pltpu.roll(aa, s, 0), aa)
        hh = aa * carry + uu
        u_s[pl.ds(r0, 8), :] = hh
        return hh[7:8, :]

    h_last = lax.fori_loop(0, lb_t // 8, scan_body, hcar_s[...])
    hcar_s[...] = h_last
    hout_ref[0] = h_last
    yb = (u_s[...] * _gelu_tanh(proj_s[:, xb0 + wb_w:xb0 + 2 * wb_w])).astype(BF16)
    yb_out_s[...] = jnp.dot(yb, wout_s[wa_w:, :], preferred_element_type=F32)

    assert lb_t % (CHUNK * UNROLL) == 0
    lax.fori_loop(0, lb_t // (CHUNK * UNROLL), chunks_body, 0)

    ya = []
    for h in range(n_heads):
        hs = slice(h * HEAD_DIM, (h + 1) * HEAD_DIM)
        oh = o_s[:, hs]
        oh = oh * lax.rsqrt(jnp.mean(oh * oh, axis=-1, keepdims=True) + EPS) * hgg_ref[:, hs]
        ya.append(oh * _silu(proj_s[:, 3 * wa_w + h * HEAD_DIM:3 * wa_w + (h + 1) * HEAD_DIM]))

    ya = jnp.concatenate(ya, axis=-1).astype(BF16)
    x1 = x + (jnp.dot(ya, wout_s[:wa_w, :], preferred_element_type=F32) + yb_out_s[...])
    x1_ref[0] = x1
    xn2 = _rms(x1, g2_ref[...]).astype(BF16)
    xn2_ref[0] = xn2
    logits_t = lax.dot_general(wr_ref[...], xn2, NT_DIMS, preferred_element_type=F32) + br_ref[...]
    gates_t, g_idx = _route_rows(logits_t, n_groups, per_group)
    gates_ref[0] = gates_t.T
    lane_c = lax.broadcasted_iota(jnp.int32, (1, ROUTER_LANES), 1)
    for t in range(lb_t // TOK_TILE):
        gi = g_idx[:, t * TOK_TILE:(t + 1) * TOK_TILE]
        cnt = jnp.zeros((1, ROUTER_LANES), jnp.int32)
        for g in range(n_groups):
            cnt = cnt + jnp.where(lane_c == g, jnp.sum(jnp.where(gi == g, 1, 0), axis=1, keepdims=True), 0)
        cnt_ref[0, t:t + 1, :] = cnt

    @pl.when(j == nj - 1)
    def _():
        for h in range(n_heads):
            sout_ref[0, h] = st_s[h].T


def _const_spec(shape):
    nd = len(shape)
    return pl.BlockSpec(shape, lambda *_: (0,) * nd)


def _mixer_prompt(x, p, lb_t):
    bsz, seq, d = x.shape
    wa_w = p["hgg"].shape[1]
    wb_w = p["cb"].shape[1]
    n_heads = wa_w // HEAD_DIM
    n_cols = p["win"].shape[1]
    weights = [p["lbw"], p["g1"], p["win"], p["hgg"], p["cw"], p["cb"], p["wx_bf"], p["bx"],
               p["wa_bf"], p["ba"], p["lam"], p["wout"], p["g2"], p["wr_t_bf"], p["br_t"]]
    nj = seq // lb_t
    tile = lambda w: pl.BlockSpec((1, lb_t, w), lambda b, j: (b, j, 0))
    n_exp = p["w1"].shape[0]
    per_step = next(k for k in range(-(-n_exp // (bsz * nj)), n_exp + 1) if n_exp % k == 0)
    exp_spec = pl.BlockSpec((per_step,) + p["w1"].shape[1:],
                            lambda b, j: (jnp.minimum(b * nj + j, n_exp // per_step - 1), 0, 0))
    out_shape = (
        jax.ShapeDtypeStruct((bsz, seq, d), F32),
        jax.ShapeDtypeStruct((bsz, seq, d), BF16),
        jax.ShapeDtypeStruct((bsz, seq, ROUTER_LANES), F32),
        jax.ShapeDtypeStruct((bsz * (seq // lb_t), lb_t // TOK_TILE, ROUTER_LANES), jnp.int32),
        jax.ShapeDtypeStruct((bsz, n_heads, HEAD_DIM, HEAD_DIM), F32),
        jax.ShapeDtypeStruct((bsz, 1, wb_w), F32),
        jax.ShapeDtypeStruct((bsz, 3, wb_w), F32),
        jax.ShapeDtypeStruct(p["w1"].shape, BF16),
        jax.ShapeDtypeStruct(p["w3"].shape, BF16),
    )
    out_specs = (
        tile(d), tile(d), tile(ROUTER_LANES),
        pl.BlockSpec((1, lb_t // TOK_TILE, ROUTER_LANES), lambda b, j: (b * nj + j, 0, 0)),
        pl.BlockSpec((1, n_heads, HEAD_DIM, HEAD_DIM), lambda b, j: (b, 0, 0, 0)),
        pl.BlockSpec((1, 1, wb_w), lambda b, j: (b, 0, 0)),
        pl.BlockSpec((1, 3, wb_w), lambda b, j: (b, 0, 0)),
        exp_spec, exp_spec,
    )
    scratch = [
        pltpu.VMEM((lb_t, n_cols), F32),
        pltpu.VMEM((n_heads, lb_t, HEAD_DIM), F32),
        pltpu.VMEM((n_heads, lb_t, HEAD_DIM), F32),
        pltpu.VMEM((lb_t, wa_w), F32),
        pltpu.VMEM((n_heads, HEAD_DIM, HEAD_DIM), F32),
        pltpu.VMEM((lb_t + 8, wb_w), F32),
        pltpu.VMEM((lb_t, wb_w), F32),
        pltpu.VMEM((lb_t, wb_w), F32),
        pltpu.VMEM((1, wb_w), F32),
        pltpu.VMEM((lb_t, d), F32),
        pltpu.VMEM((d, n_cols), BF16),
        pltpu.VMEM(p["wout"].shape, BF16),
    ]
    weight_specs = [pl.BlockSpec(w.shape, lambda *_, nd=w.ndim: (0,) * nd, pipeline_mode=pl.Buffered(1))
                    for w in weights]
    kern = functools.partial(_mixer_prompt_kernel, n_groups=p["n_groups"], per_group=p["per_group"])
    return pl.pallas_call(
        kern,
        grid=(bsz, nj),
        in_specs=[tile(d)] + weight_specs + [exp_spec, exp_spec],
        out_specs=out_specs,
        out_shape=out_shape,
        scratch_shapes=scratch,
        compiler_params=pltpu.CompilerParams(dimension_semantics=("arbitrary", "arbitrary"),
                                             vmem_limit_bytes=MIXER_VMEM_LIMIT_BYTES),
        name="mixer_prompt",
    )(x, *weights, p["w1"], p["w3"])


def _seg_pad(n):
    return jnp.bitwise_and(n + (SEG - 1), -SEG)


def _row_tile_pad(n):
    return jnp.bitwise_and(n + (ROW_TILE - 1), -ROW_TILE)


def _tile_segments(cnt_ref, tile, n_groups):
    padded = [_seg_pad(cnt_ref[tile * n_groups + g]) for g in range(n_groups)]
    starts, acc = [], jnp.int32(0)
    for g in range(n_groups):
        starts.append(acc)
        acc = acc + padded[g]
    return padded, starts, acc


def _group_bases(cnt_ref, n_tiles, n_groups):
    def body(t, tot):
        return tuple(tot[g] + _seg_pad(cnt_ref[t * n_groups + g]) for g in range(n_groups))
    totals = lax.fori_loop(0, n_tiles, body, (jnp.int32(0),) * n_groups)
    bases, ends, acc = [], [], jnp.int32(0)
    for g in range(n_groups):
        bases.append(acc)
        acc = acc + _row_tile_pad(totals[g])
        ends.append(acc)
    return bases, ends, totals


def _chunk_rows(n_chunks, starts, gstart):
    rows, row = [], None
    for c in range(n_chunks):
        row = gstart[0] if c == 0 else row + SEG
        for g in range(1, len(starts)):
            row = jnp.where(starts[g] == c * SEG, gstart[g], row)
        rows.append(pl.multiple_of(row, SEG))
    return rows


def _sort_matrix_rows(gates, starts):
    n_groups = len(starts)
    n_rows = TOK_TILE + n_groups * SEG
    col = lax.broadcasted_iota(jnp.int32, (TOK_TILE, ROUTER_LANES), 1)
    onehot_t = jnp.where(col == gates[:, 0:1].astype(jnp.int32), 1.0, 0.0).T
    r_i = lax.broadcasted_iota(jnp.int32, (TOK_TILE, TOK_TILE), 0)
    c_i = lax.broadcasted_iota(jnp.int32, (TOK_TILE, TOK_TILE), 1)
    later = jnp.where(r_i < c_i, 1.0, 0.0).astype(BF16)
    before = jnp.dot(onehot_t.astype(BF16), later, preferred_element_type=F32)
    g_row = lax.broadcasted_iota(jnp.int32, (ROUTER_LANES, 1), 0)
    base = jnp.zeros((ROUTER_LANES, 1), jnp.int32)
    for g in range(n_groups):
        base = base + jnp.where(g_row == g, starts[g], 0)
    dest = jnp.sum(jnp.where(onehot_t > 0.0, before + base.astype(F32), 0.0), axis=0, keepdims=True)
    placed = jnp.sum(onehot_t, axis=0, keepdims=True) > 0.0
    row = lax.broadcasted_iota(jnp.int32, (n_rows, TOK_TILE), 0)
    dest = jnp.where(placed, dest, -1.0)
    return row == dest.astype(jnp.int32), dest


def _dispatch_copies(xbuf, gbuf, xs_hbm, gs_hbm, sem, slot, c, row):
    return (pltpu.make_async_copy(xbuf.at[slot, pl.ds(c * SEG, SEG), :], xs_hbm.at[pl.ds(row, SEG), :], sem.at[slot]),
            pltpu.make_async_copy(gbuf.at[slot, pl.ds(c * SEG, SEG), :], gs_hbm.at[pl.ds(row, SEG), :], sem.at[slot]))


def _moe_dispatch_kernel(cnt_ref, xn2_ref, gates_ref, xn2b_ref, gatesb_ref, xs_hbm, gs_hbm, tg_ref, dest_ref,
                         xbuf, gbuf, sem, gstart_s, zx, zg, zsem, *, n_groups, per_group, n_first):
    j = pl.program_id(0)
    n_tiles = pl.num_programs(0)
    n_chunks = (TOK_TILE + n_groups * SEG) // SEG
    slot = lax.rem(j, 2)

    @pl.when(j == 0)
    def _():
        bases, ends, totals = _group_bases(cnt_ref, n_tiles, n_groups)
        for g in range(n_groups):
            gstart_s[g] = bases[g]
        n_rt = tg_ref.shape[0] - 1
        for i in range(n_rt):
            tg = jnp.int32(0)
            for g in range(n_groups):
                tg = tg + jnp.where(ends[g] <= i * ROW_TILE, 1, 0)
            tg_ref[i] = tg
        n_used = ends[n_groups - 1] // ROW_TILE
        tg_ref[n_rt] = n_used

        zx[...] = jnp.zeros_like(zx)
        zg[...] = jnp.zeros_like(zg)

        def zero_copies(row, n):
            return (pltpu.make_async_copy(zx.at[pl.ds(0, n), :], xs_hbm.at[pl.ds(row, n), :], zsem.at[0]),
                    pltpu.make_async_copy(zg.at[pl.ds(0, n), :], gs_hbm.at[pl.ds(row, n), :], zsem.at[0]))

        def for_each_gap(act):
            for g in range(n_groups):
                tail = bases[g] + totals[g]

                def seg_body(k, c, tail=tail):
                    for cp in zero_copies(pl.multiple_of(tail + k * SEG, SEG), SEG):
                        act(cp)
                    return c
                lax.fori_loop(0, (ends[g] - tail) // SEG, seg_body, 0)

            def tile_body(k, c):
                for cp in zero_copies(pl.multiple_of((n_used + k) * ROW_TILE, ROW_TILE), ROW_TILE):
                    act(cp)
                return c
            lax.fori_loop(0, n_rt - n_used, tile_body, 0)

        for_each_gap(lambda cp: cp.start())
        for_each_gap(lambda cp: cp.wait())

    def wait_tile(tile, slot_):
        _, _, used = _tile_segments(cnt_ref, tile, n_groups)
        for c in range(n_chunks):
            @pl.when(c * SEG < used)
            def _():
                for cp in _dispatch_copies(xbuf, gbuf, xs_hbm, gs_hbm, sem, slot_, c, 0):
                    cp.wait()

    @pl.when(j >= 2)
    def _():
        wait_tile(j - 2, slot)

    padded, starts, used = _tile_segments(cnt_ref, j, n_groups)
    second = j >= n_first
    gates = jnp.where(second, gatesb_ref[...], gates_ref[...])
    xn2 = jnp.where(second, xn2b_ref[...], xn2_ref[...])
    d = xn2_ref.shape[1]
    g_hi = gates.astype(BF16)
    g_lo = (gates - g_hi.astype(F32)).astype(BF16)
    onehot, dest = _sort_matrix_rows(gates, starts)
    dest_ref[0] = jnp.broadcast_to(dest, dest_ref.shape[1:])
    sort_m = jnp.where(onehot, 1.0, 0.0).astype(BF16)
    moved = jnp.dot(sort_m, jnp.concatenate([xn2, g_hi, g_lo], axis=1), preferred_element_type=F32)
    xbuf[slot] = moved[:, :d].astype(BF16)
    gbuf[slot] = moved[:, d:d + ROUTER_LANES] + moved[:, d + ROUTER_LANES:]
    gstart = [gstart_s[g] for g in range(n_groups)]
    rows = _chunk_rows(n_chunks, starts, gstart)
    for c in range(n_chunks):
        @pl.when(c * SEG < used)
        def _():
            for cp in _dispatch_copies(xbuf, gbuf, xs_hbm, gs_hbm, sem, slot, c, rows[c]):
                cp.start(priority=c % 2)
    for g in range(n_groups):
        gstart_s[g] = gstart[g] + padded[g]

    @pl.when(j == n_tiles - 1)
    def _():
        @pl.when(j >= 1)
        def _():
            wait_tile(j - 1, 1 - slot)
        wait_tile(j, slot)


def _moe_group_kernel(tg_ref, xs_ref, gs_ref, w1_ref, w3_ref, w2_ref, ys_ref, w2_s, *, n_groups, per_group):
    i = pl.program_id(0)

    @pl.when((i == 0) | (tg_ref[i] != tg_ref[jnp.maximum(i - 1, 0)]))
    def _():
        w2_s[...] = w2_ref[...].astype(BF16)

    @pl.when(tg_ref[i] < n_groups)
    def _():
        xb = xs_ref[...]
        gates = pltpu.roll(gs_ref[...], ROUTER_LANES - PROMPT_EXPERT_LANE - per_group * tg_ref[i], 1)
        acc = jnp.zeros(ys_ref.shape, F32)
        for s in range(per_group):
            h = (_silu(jnp.dot(xb, w1_ref[s], preferred_element_type=F32))
                 * jnp.dot(xb, w3_ref[s], preferred_element_type=F32) * gates[:, s:s + 1])
            acc = acc + jnp.dot(h.astype(BF16), w2_s[s], preferred_element_type=F32)
        ys_ref[...] = acc.astype(BF16)


def _moe_combine_kernel(cnt_ref, dest_ref, x1_ref, x1b_ref, gf_ref, ys_hbm, y_ref, yb_ref,
                        ybuf, sem, gstart_s, *, n_groups, n_first):
    j = pl.program_id(0)
    n_tiles = pl.num_programs(0)
    n_chunks = (TOK_TILE + n_groups * SEG) // SEG
    slot = lax.rem(j, 2)

    def copies(slot_, c, row):
        return pltpu.make_async_copy(ys_hbm.at[pl.ds(row, SEG), :], ybuf.at[slot_, pl.ds(c * SEG, SEG), :],
                                     sem.at[slot_])

    def fetch_tile(tile, slot_):
        padded, starts, _ = _tile_segments(cnt_ref, tile, n_groups)
        gstart = [gstart_s[g] for g in range(n_groups)]
        for c, row in enumerate(_chunk_rows(n_chunks, starts, gstart)):
            copies(slot_, c, row).start(priority=c % 2)
        for g in range(n_groups):
            gstart_s[g] = gstart[g] + padded[g]

    @pl.when(j == 0)
    def _():
        bases, _, _ = _group_bases(cnt_ref, n_tiles, n_groups)
        for g in range(n_groups):
            gstart_s[g] = bases[g]
        if n_chunks * SEG < SORT_ROWS:
            ybuf[:, n_chunks * SEG:, :] = jnp.zeros((2, SORT_ROWS - n_chunks * SEG, ybuf.shape[2]), BF16)
        fetch_tile(0, 0)

    @pl.when(j + 1 < n_tiles)
    def _():
        fetch_tile(j + 1, 1 - slot)

    for c in range(n_chunks):
        copies(slot, c, 0).wait()
    second = j >= n_first
    target = dest_ref[0].T[:, 0:1].astype(jnp.int32)
    sort_t = lax.broadcasted_iota(jnp.int32, (TOK_TILE, SORT_ROWS), 1) == target
    moe = jnp.dot(jnp.where(sort_t, 1.0, 0.0).astype(BF16), ybuf[slot], preferred_element_type=F32)

    @pl.when(jnp.logical_not(second))
    def _():
        y_ref[...] = _rms(x1_ref[...] + moe, gf_ref[...])

    @pl.when(second)
    def _():
        yb_ref[...] = _rms(x1b_ref[...] + moe, gf_ref[...])


def _moe_sorted(first, second, w1_bf, w3_bf, p):
    x1, xn2, gates, counts = first
    x1b, xn2b, gatesb, countsb = second
    t, d = x1.shape
    tb = x1b.shape[0]
    n_groups, per_group = p["n_groups"], p["per_group"]
    assert n_groups <= MAX_GROUPS and t % TOK_TILE == 0 and tb % TOK_TILE == 0
    n_first = t // TOK_TILE
    n_tiles = n_first + tb // TOK_TILE
    d_exp = w1_bf.shape[2]
    cap = t + tb + n_tiles * n_groups * SEG + n_groups * ROW_TILE + SORT_ROWS
    n_rt = -(-cap // ROW_TILE)
    rows = n_rt * ROW_TILE
    cnt = jnp.concatenate([counts[..., :n_groups].reshape(-1), countsb[..., :n_groups].reshape(-1)])
    params = pltpu.CompilerParams(dimension_semantics=("arbitrary",), vmem_limit_bytes=VMEM_LIMIT_BYTES)
    any_spec = pl.BlockSpec(memory_space=pl.ANY)
    tok = lambda w: pl.BlockSpec((TOK_TILE, w), lambda j, c: (jnp.minimum(j, n_first - 1), 0))
    tokb = lambda w: pl.BlockSpec((TOK_TILE, w), lambda j, c: (jnp.maximum(j - n_first, 0), 0))

    dest_spec = pl.BlockSpec((1, 8, TOK_TILE), lambda j, c: (j, 0, 0))

    xs, gs, tile_group, dest = pl.pallas_call(
        functools.partial(_moe_dispatch_kernel, n_groups=n_groups, per_group=per_group, n_first=n_first),
        grid_spec=pltpu.PrefetchScalarGridSpec(
            num_scalar_prefetch=1, grid=(n_tiles,),
            in_specs=[tok(d), tok(ROUTER_LANES), tokb(d), tokb(ROUTER_LANES)],
            out_specs=(any_spec, any_spec, pl.BlockSpec(memory_space=pltpu.SMEM), dest_spec),
            scratch_shapes=[pltpu.VMEM((2, TOK_TILE + n_groups * SEG, d), BF16),
                            pltpu.VMEM((2, TOK_TILE + n_groups * SEG, ROUTER_LANES), F32),
                            pltpu.SemaphoreType.DMA((2,)), pltpu.SMEM((n_groups,), jnp.int32),
                            pltpu.VMEM((ROW_TILE, d), BF16), pltpu.VMEM((ROW_TILE, ROUTER_LANES), F32),
                            pltpu.SemaphoreType.DMA((1,))]),
        out_shape=(jax.ShapeDtypeStruct((rows, d), BF16), jax.ShapeDtypeStruct((rows, ROUTER_LANES), F32),
                   jax.ShapeDtypeStruct((n_rt + 1,), jnp.int32),
                   jax.ShapeDtypeStruct((n_tiles, 8, TOK_TILE), F32)),
        compiler_params=params, name="moe_dispatch",
    )(cnt, xn2, gates, xn2b, gatesb)

    used_tile = lambda i, tg: (jnp.minimum(i, tg[n_rt] - 1), 0)
    group_w = lambda i, tg: (jnp.minimum(tg[i], n_groups - 1), 0, 0)
    ys = pl.pallas_call(
        functools.partial(_moe_group_kernel, n_groups=n_groups, per_group=per_group),
        grid_spec=pltpu.PrefetchScalarGridSpec(
            num_scalar_prefetch=1, grid=(n_rt,),
            in_specs=[pl.BlockSpec((ROW_TILE, d), used_tile), pl.BlockSpec((ROW_TILE, ROUTER_LANES), used_tile),
                      pl.BlockSpec((per_group, d, d_exp), group_w), pl.BlockSpec((per_group, d, d_exp), group_w),
                      pl.BlockSpec((per_group, d_exp, d), group_w)],
            out_specs=pl.BlockSpec((ROW_TILE, d), used_tile),
            scratch_shapes=[pltpu.VMEM((per_group, d_exp, d), BF16)]),
        out_shape=jax.ShapeDtypeStruct((rows, d), BF16),
        input_output_aliases={1: 0},
        compiler_params=params, name="moe_experts",
    )(tile_group, xs, gs, w1_bf, w3_bf, p["w2"])

    return pl.pallas_call(
        functools.partial(_moe_combine_kernel, n_groups=n_groups, n_first=n_first),
        grid_spec=pltpu.PrefetchScalarGridSpec(
            num_scalar_prefetch=1, grid=(n_tiles,),
            in_specs=[dest_spec, tok(d), tokb(d), pl.BlockSpec(p["gf"].shape, lambda j, c: (0, 0)), any_spec],
            out_specs=(tok(d), tokb(d)),
            scratch_shapes=[pltpu.VMEM((2, SORT_ROWS, d), BF16), pltpu.SemaphoreType.DMA((2,)),
                            pltpu.SMEM((n_groups,), jnp.int32)]),
        out_shape=(jax.ShapeDtypeStruct((t, d), F32), jax.ShapeDtypeStruct((tb, d), F32)),
        compiler_params=params, name="moe_combine",
    )(cnt, dest, x1, x1b, p["gf"], ys)


def _sample_in_kernel(x_ref, c0_ref, c1_ref, c2_ref, h0_ref, lbw_ref, g1_ref, win_ref, cw_ref, cb_ref,
                      wx_ref, bx_ref, wa_ref, ba_ref, lam_ref,
                      q_ref, f_ref, v_ref, g_ref, yb_ref, hnew_ref, xr_ref):
    wa_w = v_ref.shape[1]
    wb_w = yb_ref.shape[1]
    xn = _rms(x_ref[...], g1_ref[...])
    proj = jnp.dot(xn, win_ref[...], precision=HIGHEST, preferred_element_type=F32)
    lb = _forget_lower_bound(lbw_ref[...])
    f = lb + (1.0 - lb) * _sigmoid(proj[:, wa_w:2 * wa_w])
    q_ref[...] = proj[:, 0:wa_w].T
    f_ref[...] = f.T
    v_ref[...] = proj[:, 2 * wa_w:3 * wa_w]
    g_ref[...] = proj[:, 3 * wa_w:4 * wa_w]
    xr = proj[:, 4 * wa_w:4 * wa_w + wb_w]
    xr_ref[...] = xr
    xc = (cb_ref[...] + cw_ref[0:1, :] * c0_ref[...] + cw_ref[1:2, :] * c1_ref[...]
          + cw_ref[2:3, :] * c2_ref[...] + cw_ref[3:4, :] * xr)
    gate_x = _sigmoid(jnp.dot(xc, wx_ref[...], precision=HIGHEST, preferred_element_type=F32) + bx_ref[...])
    gate_a = _sigmoid(jnp.dot(xc, wa_ref[...], precision=HIGHEST, preferred_element_type=F32) + ba_ref[...])
    log_a = (-LRU_C) * gate_a * _softplus(-lam_ref[...])
    a = jnp.exp(log_a)
    mult = jnp.sqrt(-_expm1(2.0 * log_a))
    h = a * h0_ref[...] + gate_x * xc * mult
    hnew_ref[...] = h
    yb_ref[...] = h * _gelu_tanh(proj[:, 4 * wa_w + wb_w:4 * wa_w + 2 * wb_w])


def _sample_state_kernel(s_ref, qt_ref, ft_ref, v_ref, snew_ref, o_ref):
    tb, n_heads = s_ref.shape[0], s_ref.shape[1]
    n_tok = qt_ref.shape[1]
    shift = lax.rem(n_tok - pl.program_id(0) * tb, n_tok)
    assert 3 * tb <= HEAD_DIM
    lane = lax.broadcasted_iota(jnp.int32, (HEAD_DIM, n_tok), 1)
    r_e = lax.broadcasted_iota(jnp.int32, (n_tok, tb * HEAD_DIM), 0)
    c_e = lax.broadcasted_iota(jnp.int32, (n_tok, tb * HEAD_DIM), 1)
    spread = jnp.where((r_e < 3 * tb) & (lax.rem(r_e, tb) == c_e // HEAD_DIM), 1.0, 0.0).astype(BF16)

    def pieces_of(cols):
        hi = cols.astype(BF16).astype(F32)
        rest = cols - hi
        mid = rest.astype(BF16).astype(F32)
        low = rest - mid
        pieces = jnp.where(lane < tb, hi, jnp.where(lane < 2 * tb, pltpu.roll(mid, tb, 1),
                                                    jnp.where(lane < 3 * tb, pltpu.roll(low, 2 * tb, 1), 0.0)))
        return pieces.astype(BF16)

    for h in range(n_heads):
        hs = slice(h * HEAD_DIM, (h + 1) * HEAD_DIM)
        q_p = pieces_of(pltpu.roll(qt_ref[hs, :], shift, 1))
        f_p = pieces_of(pltpu.roll(ft_ref[hs, :], shift, 1))
        rows = []
        for t in range(tb):
            one = spread[:, t * HEAD_DIM:(t + 1) * HEAD_DIM]
            f_t = jnp.dot(f_p, one, preferred_element_type=F32)
            q_t = jnp.dot(q_p, one, preferred_element_type=F32)
            s_new = f_t * s_ref[t, h] + (1.0 - f_t) * v_ref[t:t + 1, hs]
            snew_ref[t, h] = s_new
            rows.append(jnp.sum(q_t * s_new, axis=0, keepdims=True))
        o_ref[:, hs] = jnp.concatenate(rows, axis=0)


def _sample_out_kernel(x_ref, o_ref, g_ref, yb_ref, hgg_ref, wout_ref, g2_ref, wr_ref, br_ref,
                       x1_ref, xn2_ref, gates_ref, cnt_ref, *, n_groups, per_group):
    wa_w = o_ref.shape[1]
    ya = []
    for h in range(wa_w // HEAD_DIM):
        hs = slice(h * HEAD_DIM, (h + 1) * HEAD_DIM)
        oh = o_ref[:, hs]
        oh = oh * lax.rsqrt(jnp.mean(oh * oh, axis=-1, keepdims=True) + EPS) * hgg_ref[:, hs]
        ya.append(oh * _silu(g_ref[:, hs]))
    y = jnp.concatenate(ya + [yb_ref[...]], axis=-1)
    x1 = x_ref[...] + jnp.dot(y, wout_ref[...], precision=HIGHEST, preferred_element_type=F32)
    xn2 = _rms(x1, g2_ref[...])
    logits_t = lax.dot_general(wr_ref[...], xn2, NT_DIMS, precision=HIGHEST,
                               preferred_element_type=F32) + br_ref[...]
    gates_t, g_idx = _route_rows(logits_t, n_groups, per_group)
    n = x1.shape[0]
    pad = x1_ref.shape[0] - n
    x1_ref[0:n, :] = x1
    xn2_ref[0:n, :] = xn2.astype(BF16)
    gates_ref[0:n, :] = gates_t.T
    if pad:
        x1_ref[n:, :] = jnp.zeros((pad, x1.shape[1]), F32)
        xn2_ref[n:, :] = jnp.zeros((pad, x1.shape[1]), BF16)
        gates_ref[n:, :] = jnp.where(lax.broadcasted_iota(jnp.int32, (pad, ROUTER_LANES), 1) == 0, -1.0, 0.0)
    lane_c = lax.broadcasted_iota(jnp.int32, (1, ROUTER_LANES), 1)
    tok = lax.broadcasted_iota(jnp.int32, g_idx.shape, 1)
    n_tiles = cnt_ref.shape[0]
    for t in range(n_tiles):
        in_tile = (tok >= t * TOK_TILE) & (tok < (t + 1) * TOK_TILE)
        cnt = jnp.zeros((1, ROUTER_LANES), jnp.int32)
        for g in range(n_groups):
            here = jnp.sum(jnp.where(in_tile & (g_idx == g), 1, 0), axis=1, keepdims=True)
            cnt = cnt + jnp.where(lane_c == g, here, 0)
        cnt_ref[t:t + 1, :] = cnt


def _whole(kernel, out_shape, *args, name):
    return pl.pallas_call(
        kernel, out_shape=out_shape,
        compiler_params=pltpu.CompilerParams(vmem_limit_bytes=VMEM_LIMIT_BYTES), name=name)(*args)


def _mixer_sample(x, s0, h0, c0, p, tb):
    n, d = x.shape
    wa_w = p["hgg"].shape[1]
    wb_w = p["cb"].shape[1]
    n_heads = wa_w // HEAD_DIM
    sd = lambda w: jax.ShapeDtypeStruct((n, w), F32)
    key_major = jax.ShapeDtypeStruct((wa_w, n), F32)
    q, f, v, g, yb, h_new, xr = _whole(
        _sample_in_kernel, (key_major,) * 2 + (sd(wa_w),) * 2 + (sd(wb_w),) * 3,
        x, c0[:, 0, :], c0[:, 1, :], c0[:, 2, :], h0, p["lbw"], p["g1"], p["win"], p["cw"], p["cb"],
        p["wx"], p["bx"], p["wa"], p["ba"], p["lam"], name="sample_in")
    tok = lambda w: pl.BlockSpec((tb, w), lambda i: (i, 0))
    st = pl.BlockSpec((tb, n_heads, HEAD_DIM, HEAD_DIM), lambda i: (i, 0, 0, 0))
    s_new, o = pl.pallas_call(
        _sample_state_kernel,
        grid=(n // tb,),
        in_specs=[st, _const_spec((wa_w, n)), _const_spec((wa_w, n)), tok(wa_w)],
        out_specs=(st, tok(wa_w)),
        out_shape=(jax.ShapeDtypeStruct(s0.shape, F32), sd(wa_w)),
        compiler_params=pltpu.CompilerParams(dimension_semantics=("arbitrary",),
                                             vmem_limit_bytes=VMEM_LIMIT_BYTES),
        name="sample_state",
    )(s0, q, f, v)
    n_pad = -(-n // TOK_TILE) * TOK_TILE
    x1, xn2, gates, counts = _whole(
        functools.partial(_sample_out_kernel, n_groups=p["n_groups"], per_group=p["per_group"]),
        (jax.ShapeDtypeStruct((n_pad, d), F32), jax.ShapeDtypeStruct((n_pad, d), BF16),
         jax.ShapeDtypeStruct((n_pad, ROUTER_LANES), F32),
         jax.ShapeDtypeStruct((n_pad // TOK_TILE, ROUTER_LANES), jnp.int32)),
        x, o, g, yb, p["hgg"], p["wout"], p["g2"], p["wr_t"], p["br_t"], name="sample_out")
    c_new = jnp.stack([c0[:, 1, :], c0[:, 2, :], xr], axis=1)
    return x1, xn2, gates, counts, s_new, h_new, c_new


def _block_diag(w):
    n, c, _ = w.shape
    eye = jnp.eye(n, dtype=w.dtype)
    return (w[:, :, None, :] * eye[:, None, :, None]).reshape(n * c, n * c)


def _prepare(lower_bounds, ln1_g, w_in, hgrn_norm_g, conv_w, conv_b, lru_wx, lru_bx, lru_wa, lru_ba,
             lru_lambda, w_out, ln2_g, router_wg, router_bg, router_we, router_be, exp_w1, exp_w3,
             exp_w2, final_g):
    d = w_in.shape[1]
    n_groups = router_wg.shape[-1]
    per_group = router_we.shape[-1]
    row = lambda a: a.reshape(1, -1).astype(F32)
    we = jnp.transpose(router_we[0], (1, 0, 2)).reshape(d, n_groups * per_group)
    rows_t = -(-(8 + n_groups * per_group) // 16) * 16
    wr_t = jnp.concatenate([router_wg[0].T, jnp.zeros((8 - n_groups, d), F32), we.T,
                            jnp.zeros((rows_t - 8 - n_groups * per_group, d), F32)], axis=0)
    br_t = jnp.concatenate([router_bg[0], jnp.zeros((8 - n_groups,), F32), router_be[0].reshape(-1),
                            jnp.zeros((rows_t - 8 - n_groups * per_group,), F32)]).reshape(-1, 1)
    wx = _block_diag(lru_wx[0])
    wa = _block_diag(lru_wa[0])
    return dict(
        wr_t=wr_t, wr_t_bf=wr_t.astype(BF16), br_t=br_t,
        n_groups=n_groups, per_group=per_group,
        lbw=lower_bounds.astype(F32), g1=row(ln1_g[0]), win=w_in[0],
        hgg=row(hgrn_norm_g[0]), cw=conv_w[0], cb=row(conv_b[0]),
        wx=wx, wx_bf=wx.astype(BF16), bx=row(lru_bx[0]), wa=wa, wa_bf=wa.astype(BF16), ba=row(lru_ba[0]),
        lam=row(lru_lambda[0]), wout=w_out[0], g2=row(ln2_g[0]),
        w1=exp_w1[0], w3=exp_w3[0],
        w2=exp_w2[0], gf=row(final_g))


def kernel(x_prompt, x_sample, state_hgrn, state_rglru, state_conv, lower_bounds, ln1_g, w_in, hgrn_norm_g, conv_w, conv_b, lru_wx, lru_bx, lru_wa, lru_ba, lru_lambda, w_out, ln2_g, router_wg, router_bg, router_we, router_be, exp_w1, exp_w3, exp_w2, final_g):
    assert w_in.shape[0] == 1, "single-layer trunk"
    p = _prepare(lower_bounds, ln1_g, w_in, hgrn_norm_g, conv_w, conv_b, lru_wx, lru_bx, lru_wa, lru_ba,
                 lru_lambda, w_out, ln2_g, router_wg, router_bg, router_we, router_be, exp_w1, exp_w3,
                 exp_w2, final_g)
    bsz, seq, d = x_prompt.shape
    x1, xn2, gates, counts, s_p, h_p, c_p, w1_bf, w3_bf = _mixer_prompt(x_prompt, p, min(MIXER_BLOCK, seq))
    t = bsz * seq
    n = x_sample.shape[0]
    x1s, xn2s, gates_s, counts_s, s_s, h_s, c_s = _mixer_sample(
        x_sample[:, 0, :], state_hgrn[0], state_rglru[0], state_conv[0], p, SAMPLE_STEP_TOKENS)
    y_p, y_s = _moe_sorted((x1.reshape(t, d), xn2.reshape(t, d), gates.reshape(t, ROUTER_LANES), counts),
                           (x1s, xn2s, gates_s, counts_s), w1_bf, w3_bf, p)
    return (y_p.reshape(bsz, seq, d), y_s[:n].reshape(n, 1, d),
            s_p[None], h_p.reshape(1, bsz, -1), c_p[None],
            s_s[None], h_s[None], c_s[None])
```
